```python
import jax, jax.numpy as jnp
from jax import lax
import numpy as np

D_MODEL = 1024
BATCH = 8
SEQ = 8192
DEPTH = 2

N_MIXERS = 2
N_META = 16
FOX_HEADS = 16
FOX_HEAD_DIM = D_MODEL // FOX_HEADS
FOX_Q_BLOCK = 128
HGRN_EXPAND = 128
HGRN_HEADS = D_MODEL // HGRN_EXPAND
HGRN_DK = HGRN_EXPAND
HGRN_DV = D_MODEL // HGRN_HEADS
HGRN_CHUNK = 64
FFN_HIDDEN = -(-8 * D_MODEL // (3 * 256)) * 256
N_FOX = (DEPTH + N_MIXERS - 1) // N_MIXERS
N_HGRN = DEPTH // N_MIXERS
EPS = 1e-6

kernel_name = "fox_hgrn2_interleaved_meta_trunk"


def rms_norm(x, gain):
    xf = x.astype(jnp.float32)
    y = xf * lax.rsqrt(jnp.mean(xf * xf, axis=-1, keepdims=True) + EPS)
    return (y * gain.astype(jnp.float32)).astype(x.dtype)


def _fox_attend(q_blk, c_q, pos_q, k, v, c_k, pos_k):
    s = jnp.einsum('bhqd,bhkd->bhqk', q_blk, k).astype(jnp.float32) * (FOX_HEAD_DIM ** -0.5)
    s = s + (c_q[..., :, None] - c_k[..., None, :])
    s = jnp.where(pos_k[None, :] <= pos_q[:, None], s, -jnp.inf)
    p = jax.nn.softmax(s, axis=-1)
    return jnp.einsum('bhqk,bhkd->bhqd', p, v.astype(jnp.float32))


def fox_mixer(h, w_in, b_f, q_gain, k_gain, w_out):
    B, L, D = h.shape
    n_blk = (L - N_META) // FOX_Q_BLOCK
    proj = h @ w_in
    q, k, v, gate, f_logit = jnp.split(proj, [D, 2 * D, 3 * D, 4 * D], axis=-1)
    q = rms_norm(q.reshape(B, L, FOX_HEADS, FOX_HEAD_DIM), q_gain)
    k = rms_norm(k.reshape(B, L, FOX_HEADS, FOX_HEAD_DIM), k_gain)
    v = v.reshape(B, L, FOX_HEADS, FOX_HEAD_DIM)
    log_f = jax.nn.log_sigmoid(f_logit.astype(jnp.float32) + b_f.astype(jnp.float32))
    c = jnp.cumsum(log_f, axis=1).transpose(0, 2, 1)
    q, k, v = (t.transpose(0, 2, 1, 3) for t in (q, k, v))
    pos = jnp.arange(L)
    o_meta = _fox_attend(q[:, :, :N_META], c[:, :, :N_META], pos[:N_META],
                         k[:, :, :N_META], v[:, :, :N_META], c[:, :, :N_META], pos[:N_META])
    qb = q[:, :, N_META:].reshape(B, FOX_HEADS, n_blk, FOX_Q_BLOCK, FOX_HEAD_DIM).transpose(2, 0, 1, 3, 4)
    cb = c[:, :, N_META:].reshape(B, FOX_HEADS, n_blk, FOX_Q_BLOCK).transpose(2, 0, 1, 3)
    pb = pos[N_META:].reshape(n_blk, FOX_Q_BLOCK)
    o_real = lax.map(lambda a: _fox_attend(a[0], a[1], a[2], k, v, c, pos), (qb, cb, pb))
    o_real = o_real.transpose(1, 2, 0, 3, 4).reshape(B, FOX_HEADS, L - N_META, FOX_HEAD_DIM)
    o = jnp.concatenate([o_meta, o_real], axis=2).transpose(0, 2, 1, 3).reshape(B, L, D)
    o = o * jax.nn.sigmoid(gate.astype(jnp.float32))
    return (o.astype(h.dtype) @ w_out).astype(h.dtype)


def _hgrn_chunk(S, inp):
    q, k, v, g = inp
    C = q.shape[2]
    b = jnp.cumsum(g, axis=2)
    causal = jnp.tril(jnp.ones((C, C), dtype=bool))
    diff = b[:, :, :, None, :] - b[:, :, None, :, :]
    decay = jnp.exp(jnp.where(causal[..., None], diff, -jnp.inf))
    attn = jnp.einsum('bhtd,bhsd,bhtsd->bhts', q, k, decay)
    o = jnp.einsum('bhts,bhsv->bhtv', attn, v) + jnp.einsum('bhtd,bhdv->bhtv', q * jnp.exp(b), S)
    b_last = b[:, :, -1:, :]
    S_new = jnp.exp(b_last[:, :, 0, :])[..., None] * S + jnp.einsum(
        'bhsd,bhsv->bhdv', k * jnp.exp(b_last - b), v)
    return S_new, o


def hgrn2_mixer(h, w_in, lb, g_gain, w_out):
    B, L, D = h.shape
    n_chunk = (L - N_META) // HGRN_CHUNK
    proj = h @ w_in
    q, f_logit, i, g_out = jnp.split(proj, 4, axis=-1)
    z = f_logit.astype(jnp.float32)
    lbf = lb.astype(jnp.float32)
    log_f = jnp.logaddexp(jnp.log(lbf), jnp.log1p(-lbf) + jax.nn.log_sigmoid(z))
    k = (1.0 - lbf) * jax.nn.sigmoid(-z)
    q = jax.nn.silu(q.astype(jnp.float32))
    v = i.astype(jnp.float32)
    heads = lambda t, d: t.reshape(B, L, HGRN_HEADS, d).transpose(0, 2, 1, 3)
    q, k, log_f, v = heads(q, HGRN_DK), heads(k, HGRN_DK), heads(log_f, HGRN_DK), heads(v, HGRN_DV)
    S0 = jnp.zeros((B, HGRN_HEADS, HGRN_DK, HGRN_DV), jnp.float32)
    S_meta, o_meta = _hgrn_chunk(S0, (q[:, :, :N_META], k[:, :, :N_META], v[:, :, :N_META], log_f[:, :, :N_META]))
    chunks = lambda t: t[:, :, N_META:].reshape(B, HGRN_HEADS, n_chunk, HGRN_CHUNK, t.shape[-1]).transpose(2, 0, 1, 3, 4)
    _, o_real = lax.scan(_hgrn_chunk, S_meta, (chunks(q), chunks(k), chunks(v), chunks(log_f)))
    o_real = o_real.transpose(1, 2, 0, 3, 4).reshape(B, HGRN_HEADS, L - N_META, HGRN_DV)
    o = jnp.concatenate([o_meta, o_real], axis=2).transpose(0, 2, 1, 3)
    o = rms_norm(o, g_gain) * jax.nn.silu(g_out.astype(jnp.float32).reshape(B, L, HGRN_HEADS, HGRN_DV))
    return (o.reshape(B, L, D).astype(h.dtype) @ w_out).astype(h.dtype)


def swiglu(h, w_in, w_out):
    gate, up = jnp.split(h @ w_in, 2, axis=-1)
    return ((jax.nn.silu(gate) * up) @ w_out).astype(h.dtype)


def _fwd_setup_inputs(seed: int = 0) -> dict:
    key = jax.random.key(seed)
    ks = jax.random.split(key, 18)
    D = D_MODEL
    nrm = lambda k, shape, fan: jax.random.normal(k, shape, jnp.float32) * fan ** -0.5
    gain = lambda k, shape: 1.0 + 0.02 * jax.random.normal(k, shape, jnp.float32)
    return {
        "x": jax.random.normal(ks[0], (BATCH, SEQ, D), jnp.float32),
        "meta_tokens": jax.random.normal(ks[1], (N_META, D), jnp.float32),
        "attn_norm": gain(ks[2], (DEPTH, D)),
        "ffn_norm": gain(ks[3], (DEPTH, D)),
        "final_norm": gain(ks[4], (D,)),
        "fox_w_in": nrm(ks[5], (N_FOX, D, 4 * D + FOX_HEADS), D),
        "fox_b_f": jax.random.uniform(ks[6], (N_FOX, FOX_HEADS), jnp.float32, 1.0, 4.0),
        "fox_q_norm": gain(ks[7], (N_FOX, FOX_HEAD_DIM)),
        "fox_k_norm": gain(ks[8], (N_FOX, FOX_HEAD_DIM)),
        "fox_w_out": nrm(ks[9], (N_FOX, D, D), D),
        "hgrn_w_in": nrm(ks[10], (N_HGRN, D, 4 * D), D),
        "hgrn_lower_bounds": 0.1 * jax.random.normal(ks[11], (DEPTH, D), jnp.float32),
        "hgrn_g_norm": gain(ks[12], (N_HGRN, HGRN_DV)),
        "hgrn_w_out": nrm(ks[13], (N_HGRN, D, D), D),
        "ffn_w_in": nrm(ks[14], (DEPTH, D, 2 * FFN_HIDDEN), D),
        "ffn_w_out": nrm(ks[15], (DEPTH, FFN_HIDDEN, D), FFN_HIDDEN),
    }


def _fwd_reference(x, meta_tokens, attn_norm, ffn_norm, final_norm, fox_w_in, fox_b_f, fox_q_norm,
              fox_k_norm, fox_w_out, hgrn_w_in, hgrn_lower_bounds, hgrn_g_norm, hgrn_w_out,
              ffn_w_in, ffn_w_out):
    B = x.shape[0]
    meta = jnp.broadcast_to(meta_tokens[None].astype(x.dtype), (B, N_META, D_MODEL))
    h = jnp.concatenate([meta, x], axis=1)
    lb_soft = jax.nn.softmax(hgrn_lower_bounds.astype(jnp.float32), axis=0)
    lower_bounds = jnp.cumsum(lb_soft, axis=0) - lb_soft[0]
    for i in range(DEPTH):
        hn = rms_norm(h, attn_norm[i])
        j = i // N_MIXERS
        if i % N_MIXERS == 0:
            h = h + fox_mixer(hn, fox_w_in[j], fox_b_f[j], fox_q_norm[j], fox_k_norm[j], fox_w_out[j])
        else:
            h = h + hgrn2_mixer(hn, hgrn_w_in[j], lower_bounds[i], hgrn_g_norm[j], hgrn_w_out[j])
        h = h + swiglu(rms_norm(h, ffn_norm[i]), ffn_w_in[i], ffn_w_out[i])
    h = rms_norm(h, final_norm)
    return h[:, N_META:]


import jax as _jax
import jax.numpy as _jnp

TWIN_FORMAT = 'train_step'
FWD_PARAMS = ['x', 'meta_tokens', 'attn_norm', 'ffn_norm', 'final_norm', 'fox_w_in', 'fox_b_f', 'fox_q_norm', 'fox_k_norm', 'fox_w_out', 'hgrn_w_in', 'hgrn_lower_bounds', 'hgrn_g_norm', 'hgrn_w_out', 'ffn_w_in', 'ffn_w_out']
TWIN_WEIGHTS = ['meta_tokens', 'attn_norm', 'ffn_norm', 'final_norm', 'fox_w_in', 'fox_b_f', 'fox_q_norm', 'fox_k_norm', 'fox_w_out', 'hgrn_w_in', 'hgrn_lower_bounds', 'hgrn_g_norm', 'hgrn_w_out', 'ffn_w_in', 'ffn_w_out']
TWIN_DIFF_INPUT = 'x'
TWIN_INPUTS = ['x', 'meta_tokens', 'attn_norm', 'ffn_norm', 'final_norm', 'fox_w_in', 'fox_b_f', 'fox_q_norm', 'fox_k_norm', 'fox_w_out', 'hgrn_w_in', 'hgrn_lower_bounds', 'hgrn_g_norm', 'hgrn_w_out', 'ffn_w_in', 'ffn_w_out', 'loss_target', 'm_meta_tokens', 'm_attn_norm', 'm_ffn_norm', 'm_final_norm', 'm_fox_w_in', 'm_fox_b_f', 'm_fox_q_norm', 'm_fox_k_norm', 'm_fox_w_out', 'm_hgrn_w_in', 'm_hgrn_lower_bounds', 'm_hgrn_g_norm', 'm_hgrn_w_out', 'm_ffn_w_in', 'm_ffn_w_out', 'v_meta_tokens', 'v_attn_norm', 'v_ffn_norm', 'v_final_norm', 'v_fox_w_in', 'v_fox_b_f', 'v_fox_q_norm', 'v_fox_k_norm', 'v_fox_w_out', 'v_hgrn_w_in', 'v_hgrn_lower_bounds', 'v_hgrn_g_norm', 'v_hgrn_w_out', 'v_ffn_w_in', 'v_ffn_w_out']
TWIN_OUTPUTS = ['loss', 'grad_x', 'grad_meta_tokens', 'grad_attn_norm', 'grad_ffn_norm', 'grad_final_norm', 'grad_fox_w_in', 'grad_fox_b_f', 'grad_fox_q_norm', 'grad_fox_k_norm', 'grad_fox_w_out', 'grad_hgrn_w_in', 'grad_hgrn_lower_bounds', 'grad_hgrn_g_norm', 'grad_hgrn_w_out', 'grad_ffn_w_in', 'grad_ffn_w_out', 'delta_meta_tokens', 'delta_attn_norm', 'delta_ffn_norm', 'delta_final_norm', 'delta_fox_w_in', 'delta_fox_b_f', 'delta_fox_q_norm', 'delta_fox_k_norm', 'delta_fox_w_out', 'delta_hgrn_w_in', 'delta_hgrn_lower_bounds', 'delta_hgrn_g_norm', 'delta_hgrn_w_out', 'delta_ffn_w_in', 'delta_ffn_w_out', 'new_m_meta_tokens', 'new_m_attn_norm', 'new_m_ffn_norm', 'new_m_final_norm', 'new_m_fox_w_in', 'new_m_fox_b_f', 'new_m_fox_q_norm', 'new_m_fox_k_norm', 'new_m_fox_w_out', 'new_m_hgrn_w_in', 'new_m_hgrn_lower_bounds', 'new_m_hgrn_g_norm', 'new_m_hgrn_w_out', 'new_m_ffn_w_in', 'new_m_ffn_w_out', 'new_v_meta_tokens', 'new_v_attn_norm', 'new_v_ffn_norm', 'new_v_final_norm', 'new_v_fox_w_in', 'new_v_fox_b_f', 'new_v_fox_q_norm', 'new_v_fox_k_norm', 'new_v_fox_w_out', 'new_v_hgrn_w_in', 'new_v_hgrn_lower_bounds', 'new_v_hgrn_g_norm', 'new_v_hgrn_w_out', 'new_v_ffn_w_in', 'new_v_ffn_w_out']
TWIN_LEAF_KINDS = {'loss': 'loss', 'grad_x': 'grad_x', 'grad_meta_tokens': 'grad_w', 'grad_attn_norm': 'grad_w', 'grad_ffn_norm': 'grad_w', 'grad_final_norm': 'grad_w', 'grad_fox_w_in': 'grad_w', 'grad_fox_b_f': 'grad_w', 'grad_fox_q_norm': 'grad_w', 'grad_fox_k_norm': 'grad_w', 'grad_fox_w_out': 'grad_w', 'grad_hgrn_w_in': 'grad_w', 'grad_hgrn_lower_bounds': 'grad_w', 'grad_hgrn_g_norm': 'grad_w', 'grad_hgrn_w_out': 'grad_w', 'grad_ffn_w_in': 'grad_w', 'grad_ffn_w_out': 'grad_w', 'delta_meta_tokens': 'delta_w', 'delta_attn_norm': 'delta_w', 'delta_ffn_norm': 'delta_w', 'delta_final_norm': 'delta_w', 'delta_fox_w_in': 'delta_w', 'delta_fox_b_f': 'delta_w', 'delta_fox_q_norm': 'delta_w', 'delta_fox_k_norm': 'delta_w', 'delta_fox_w_out': 'delta_w', 'delta_hgrn_w_in': 'delta_w', 'delta_hgrn_lower_bounds': 'delta_w', 'delta_hgrn_g_norm': 'delta_w', 'delta_hgrn_w_out': 'delta_w', 'delta_ffn_w_in': 'delta_w', 'delta_ffn_w_out': 'delta_w', 'new_m_meta_tokens': 'new_m', 'new_m_attn_norm': 'new_m', 'new_m_ffn_norm': 'new_m', 'new_m_final_norm': 'new_m', 'new_m_fox_w_in': 'new_m', 'new_m_fox_b_f': 'new_m', 'new_m_fox_q_norm': 'new_m', 'new_m_fox_k_norm': 'new_m', 'new_m_fox_w_out': 'new_m', 'new_m_hgrn_w_in': 'new_m', 'new_m_hgrn_lower_bounds': 'new_m', 'new_m_hgrn_g_norm': 'new_m', 'new_m_hgrn_w_out': 'new_m', 'new_m_ffn_w_in': 'new_m', 'new_m_ffn_w_out': 'new_m', 'new_v_meta_tokens': 'new_v', 'new_v_attn_norm': 'new_v', 'new_v_ffn_norm': 'new_v', 'new_v_final_norm': 'new_v', 'new_v_fox_w_in': 'new_v', 'new_v_fox_b_f': 'new_v', 'new_v_fox_q_norm': 'new_v', 'new_v_fox_k_norm': 'new_v', 'new_v_fox_w_out': 'new_v', 'new_v_hgrn_w_in': 'new_v', 'new_v_hgrn_lower_bounds': 'new_v', 'new_v_hgrn_g_norm': 'new_v', 'new_v_hgrn_w_out': 'new_v', 'new_v_ffn_w_in': 'new_v', 'new_v_ffn_w_out': 'new_v'}


def _forward(args):
    return _fwd_reference(*[args[k] for k in FWD_PARAMS])


def _output_shape():
    def fwd():
        inp = _fwd_setup_inputs(0)
        return _fwd_reference(*[inp[k] for k in FWD_PARAMS])
    out = _jax.eval_shape(fwd)
    return out.shape, out.dtype

N_MICROBATCH = 1
ADAM_LR = 0.001
ADAM_B1 = 0.9
ADAM_B2 = 0.999
ADAM_EPS = 1e-08
ADAM_WD = 0.01
ADAM_STEP = 10
PER_EXAMPLE_BATCH_AXIS = {'x': 0, 'loss_target': 0}
SHARED_INPUTS = []
_WEIGHT_DTYPES = {'meta_tokens': _jnp.float32, 'attn_norm': _jnp.float32, 'ffn_norm': _jnp.float32, 'final_norm': _jnp.float32, 'fox_w_in': _jnp.float32, 'fox_b_f': _jnp.float32, 'fox_q_norm': _jnp.float32, 'fox_k_norm': _jnp.float32, 'fox_w_out': _jnp.float32, 'hgrn_w_in': _jnp.float32, 'hgrn_lower_bounds': _jnp.float32, 'hgrn_g_norm': _jnp.float32, 'hgrn_w_out': _jnp.float32, 'ffn_w_in': _jnp.float32, 'ffn_w_out': _jnp.float32}
MOMENT_SCALE = {'meta_tokens': 4.387938e-03, 'attn_norm': 1.462683e-01, 'ffn_norm': 1.886383e-01, 'final_norm': 6.394174e+01, 'fox_w_in': 4.863076e-02, 'fox_b_f': 2.873777e-01, 'fox_q_norm': 2.087557e-01, 'fox_k_norm': 2.069661e-01, 'fox_w_out': 6.472476e-02, 'hgrn_w_in': 8.949356e-02, 'hgrn_lower_bounds': 1.146854e-02, 'hgrn_g_norm': 3.273761e-01, 'hgrn_w_out': 1.230012e-01, 'ffn_w_in': 7.972726e-02, 'ffn_w_out': 1.300218e-01}


def _to_microbatches(a, axis):
    t = _jnp.moveaxis(a, axis, 0)
    t = t.reshape((N_MICROBATCH, t.shape[0] // N_MICROBATCH) + t.shape[1:])
    return _jnp.moveaxis(t, 1, axis + 1)


def setup_inputs(seed: int = 0) -> dict:
    inp = _fwd_setup_inputs(seed)
    key = _jax.random.fold_in(_jax.random.key(seed), 7919)
    shape, _ = _output_shape()
    out = dict(inp)
    out["loss_target"] = _jax.random.normal(_jax.random.fold_in(key, 0), shape, _jnp.float32)
    for i, name in enumerate(TWIN_WEIGHTS):
        w = inp[name].astype(_jnp.float32)
        if MOMENT_SCALE is None:
            s = _jnp.sqrt(_jnp.mean(_jnp.square(w)) + 1e-30)
        else:
            s = MOMENT_SCALE[name]
        km, kv = _jax.random.split(_jax.random.fold_in(key, i + 1))
        out[name] = w
        out["m_" + name] = s * _jax.random.normal(km, w.shape, _jnp.float32)
        out["v_" + name] = (s * s) * _jax.random.uniform(kv, w.shape, _jnp.float32, 0.5, 1.5)
    if N_MICROBATCH > 1:
        for name, axis in PER_EXAMPLE_BATCH_AXIS.items():
            out[name] = _to_microbatches(out[name], axis)
    return {'x': out['x'], 'meta_tokens': out['meta_tokens'], 'attn_norm': out['attn_norm'], 'ffn_norm': out['ffn_norm'], 'final_norm': out['final_norm'], 'fox_w_in': out['fox_w_in'], 'fox_b_f': out['fox_b_f'], 'fox_q_norm': out['fox_q_norm'], 'fox_k_norm': out['fox_k_norm'], 'fox_w_out': out['fox_w_out'], 'hgrn_w_in': out['hgrn_w_in'], 'hgrn_lower_bounds': out['hgrn_lower_bounds'], 'hgrn_g_norm': out['hgrn_g_norm'], 'hgrn_w_out': out['hgrn_w_out'], 'ffn_w_in': out['ffn_w_in'], 'ffn_w_out': out['ffn_w_out'], 'loss_target': out['loss_target'], 'm_meta_tokens': out['m_meta_tokens'], 'm_attn_norm': out['m_attn_norm'], 'm_ffn_norm': out['m_ffn_norm'], 'm_final_norm': out['m_final_norm'], 'm_fox_w_in': out['m_fox_w_in'], 'm_fox_b_f': out['m_fox_b_f'], 'm_fox_q_norm': out['m_fox_q_norm'], 'm_fox_k_norm': out['m_fox_k_norm'], 'm_fox_w_out': out['m_fox_w_out'], 'm_hgrn_w_in': out['m_hgrn_w_in'], 'm_hgrn_lower_bounds': out['m_hgrn_lower_bounds'], 'm_hgrn_g_norm': out['m_hgrn_g_norm'], 'm_hgrn_w_out': out['m_hgrn_w_out'], 'm_ffn_w_in': out['m_ffn_w_in'], 'm_ffn_w_out': out['m_ffn_w_out'], 'v_meta_tokens': out['v_meta_tokens'], 'v_attn_norm': out['v_attn_norm'], 'v_ffn_norm': out['v_ffn_norm'], 'v_final_norm': out['v_final_norm'], 'v_fox_w_in': out['v_fox_w_in'], 'v_fox_b_f': out['v_fox_b_f'], 'v_fox_q_norm': out['v_fox_q_norm'], 'v_fox_k_norm': out['v_fox_k_norm'], 'v_fox_w_out': out['v_fox_w_out'], 'v_hgrn_w_in': out['v_hgrn_w_in'], 'v_hgrn_lower_bounds': out['v_hgrn_lower_bounds'], 'v_hgrn_g_norm': out['v_hgrn_g_norm'], 'v_hgrn_w_out': out['v_hgrn_w_out'], 'v_ffn_w_in': out['v_ffn_w_in'], 'v_ffn_w_out': out['v_ffn_w_out']}


def _loss(weights, diff, rest, loss_target):
    with _jax.named_scope("forward"):
        args = {**rest, TWIN_DIFF_INPUT: diff, **{k: w.astype(_WEIGHT_DTYPES[k]) for k, w in weights.items()}}
        y = _forward(args)
    with _jax.named_scope("loss_head"):
        err = _jnp.square(y.astype(_jnp.float32) - loss_target)
        return 0.5 * _jnp.sum(_jnp.mean(err, axis=-1)) if err.ndim else 0.5 * err


def _adamw(w, g, m, v):
    m = ADAM_B1 * m + (1.0 - ADAM_B1) * g
    v = ADAM_B2 * v + (1.0 - ADAM_B2) * _jnp.square(g)
    m_hat = m / (1.0 - ADAM_B1 ** ADAM_STEP)
    v_hat = v / (1.0 - ADAM_B2 ** ADAM_STEP)
    delta = -ADAM_LR * (m_hat / (_jnp.sqrt(v_hat) + ADAM_EPS) + ADAM_WD * w)
    return delta, m, v


def reference(x, meta_tokens, attn_norm, ffn_norm, final_norm, fox_w_in, fox_b_f, fox_q_norm, fox_k_norm, fox_w_out, hgrn_w_in, hgrn_lower_bounds, hgrn_g_norm, hgrn_w_out, ffn_w_in, ffn_w_out, loss_target, m_meta_tokens, m_attn_norm, m_ffn_norm, m_final_norm, m_fox_w_in, m_fox_b_f, m_fox_q_norm, m_fox_k_norm, m_fox_w_out, m_hgrn_w_in, m_hgrn_lower_bounds, m_hgrn_g_norm, m_hgrn_w_out, m_ffn_w_in, m_ffn_w_out, v_meta_tokens, v_attn_norm, v_ffn_norm, v_final_norm, v_fox_w_in, v_fox_b_f, v_fox_q_norm, v_fox_k_norm, v_fox_w_out, v_hgrn_w_in, v_hgrn_lower_bounds, v_hgrn_g_norm, v_hgrn_w_out, v_ffn_w_in, v_ffn_w_out):
    given = dict(x=x, meta_tokens=meta_tokens, attn_norm=attn_norm, ffn_norm=ffn_norm, final_norm=final_norm, fox_w_in=fox_w_in, fox_b_f=fox_b_f, fox_q_norm=fox_q_norm, fox_k_norm=fox_k_norm, fox_w_out=fox_w_out, hgrn_w_in=hgrn_w_in, hgrn_lower_bounds=hgrn_lower_bounds, hgrn_g_norm=hgrn_g_norm, hgrn_w_out=hgrn_w_out, ffn_w_in=ffn_w_in, ffn_w_out=ffn_w_out, loss_target=loss_target, m_meta_tokens=m_meta_tokens, m_attn_norm=m_attn_norm, m_ffn_norm=m_ffn_norm, m_final_norm=m_final_norm, m_fox_w_in=m_fox_w_in, m_fox_b_f=m_fox_b_f, m_fox_q_norm=m_fox_q_norm, m_fox_k_norm=m_fox_k_norm, m_fox_w_out=m_fox_w_out, m_hgrn_w_in=m_hgrn_w_in, m_hgrn_lower_bounds=m_hgrn_lower_bounds, m_hgrn_g_norm=m_hgrn_g_norm, m_hgrn_w_out=m_hgrn_w_out, m_ffn_w_in=m_ffn_w_in, m_ffn_w_out=m_ffn_w_out, v_meta_tokens=v_meta_tokens, v_attn_norm=v_attn_norm, v_ffn_norm=v_ffn_norm, v_final_norm=v_final_norm, v_fox_w_in=v_fox_w_in, v_fox_b_f=v_fox_b_f, v_fox_q_norm=v_fox_q_norm, v_fox_k_norm=v_fox_k_norm, v_fox_w_out=v_fox_w_out, v_hgrn_w_in=v_hgrn_w_in, v_hgrn_lower_bounds=v_hgrn_lower_bounds, v_hgrn_g_norm=v_hgrn_g_norm, v_hgrn_w_out=v_hgrn_w_out, v_ffn_w_in=v_ffn_w_in, v_ffn_w_out=v_ffn_w_out)
    weights = {n: given[n] for n in TWIN_WEIGHTS}
    shared = {n: given[n] for n in SHARED_INPUTS}
    per_example = {n: given[n] for n in ['x']}
    grad_fn = _jax.value_and_grad(_loss, argnums=(0, 1))

    def one_microbatch(ex, loss_target):
        ex = dict(ex)
        diff = ex.pop(TWIN_DIFF_INPUT)
        return grad_fn(weights, diff, {**shared, **ex}, loss_target)

    if N_MICROBATCH == 1:
        loss, (grad_w, grad_x) = one_microbatch(per_example, given["loss_target"])
    else:
        def body(carry, xs):
            loss_sum, grad_sum = carry
            l_k, (gw_k, gx_k) = one_microbatch(xs[0], xs[1])
            with _jax.named_scope("update"):
                return (loss_sum + l_k, _jax.tree.map(_jnp.add, grad_sum, gw_k)), gx_k

        init = (_jnp.zeros((), _jnp.float32), _jax.tree.map(_jnp.zeros_like, weights))
        (loss, grad_w), grad_x = _jax.lax.scan(body, init, (per_example, given["loss_target"]))
    with _jax.named_scope("update"):
        delta_w, new_m, new_v = {}, {}, {}
        for n in TWIN_WEIGHTS:
            delta_w[n], new_m[n], new_v[n] = _adamw(weights[n], grad_w[n], given["m_" + n], given["v_" + n])
    return (loss, grad_x, *[grad_w[n] for n in TWIN_WEIGHTS], *[delta_w[n] for n in TWIN_WEIGHTS],
            *[new_m[n] for n in TWIN_WEIGHTS], *[new_v[n] for n in TWIN_WEIGHTS])
```

```python
import functools
import math

import jax
import jax.numpy as jnp
from jax import lax
from jax.experimental import pallas as pl
from jax.experimental.pallas import tpu as pltpu

F32 = jnp.float32
BF16 = jnp.bfloat16
EPS = 1e-6
N_META = 16
LANES = 128
HEAD_ROWS = 128
N_PAD = HEAD_ROWS - N_META
FOX_DH = 64
HGRN_CHUNK = 64
N_DEV = 8
NEG = -1e30
VMEM_LIMIT = 56 * 1024 * 1024
HI = lax.Precision.HIGHEST

ADAM_LR = 0.001
ADAM_B1 = 0.9
ADAM_B2 = 0.999
ADAM_EPS = 1e-08
ADAM_WD = 0.01
ADAM_STEP = 10

BIG = ("meta_tokens", "fox_w_in", "fox_w_out", "hgrn_w_in", "hgrn_w_out", "ffn_w_in", "ffn_w_out")
SMALL = ("attn_norm", "ffn_norm", "final_norm", "fox_b_f", "fox_q_norm", "fox_k_norm",
         "hgrn_lower_bounds", "hgrn_g_norm")
WEIGHTS = ("meta_tokens", "attn_norm", "ffn_norm", "final_norm", "fox_w_in", "fox_b_f", "fox_q_norm",
           "fox_k_norm", "fox_w_out", "hgrn_w_in", "hgrn_lower_bounds", "hgrn_g_norm", "hgrn_w_out",
           "ffn_w_in", "ffn_w_out")
COL_SHARDED = ("meta_tokens", "fox_w_in", "hgrn_w_in", "ffn_w_in")


def _params(sem=None):
    return pltpu.CompilerParams(dimension_semantics=sem, vmem_limit_bytes=VMEM_LIMIT)


def _tile(n, cap):
    best = None
    for t in range(LANES, min(n, cap) + 1, LANES):
        if n % t == 0:
            best = t
    assert best is not None, (n, cap)
    return best


def _row_chunk(n, cap):
    best = n
    for t in range(8, min(n, cap) + 1, 8):
        if n % t == 0:
            best = t
    return best


def _dg(a, b, ca, cb):
    return lax.dot_general(a.astype(BF16), b.astype(BF16), (((ca,), (cb,)), ((), ())),
                           preferred_element_type=F32)


@jax.custom_vjp
def _d_nn(a, b):
    return _dg(a, b, 1, 0)


@jax.custom_vjp
def _d_nt(a, b):
    return _dg(a, b, 1, 1)


@jax.custom_vjp
def _d_tn(a, b):
    return _dg(a, b, 0, 0)


_d_nn.defvjp(lambda a, b: (_d_nn(a, b), (a, b)), lambda r, g: (_d_nt(g, r[1]), _d_tn(r[0], g)))
_d_nt.defvjp(lambda a, b: (_d_nt(a, b), (a, b)), lambda r, g: (_d_nn(g, r[1]), _d_tn(g, r[0])))
_d_tn.defvjp(lambda a, b: (_d_tn(a, b), (a, b)), lambda r, g: (_d_nt(r[1], g), _d_nn(r[0], g)))


def _log_sigmoid(x):
    return jnp.minimum(x, 0.0) - jnp.log1p(jnp.exp(-jnp.abs(x)))


def _rms(x, g):
    return x * lax.rsqrt(jnp.mean(x * x, axis=-1, keepdims=True) + EPS) * g


def _mm(a, b, mode, out_dtype, name, tm=None, tn=None, tk=None):
    assert a.dtype == BF16 and b.dtype == BF16, (name, a.dtype, b.dtype)
    if mode == "nn":
        (M, K), N = a.shape, b.shape[1]
    elif mode == "nt":
        (M, K), N = a.shape, b.shape[0]
    else:
        (K, M), N = a.shape, b.shape[1]
    tm = tm or _tile(M, 1408 if mode == "tn" else 640)
    tn = tn or _tile(N, 1408)
    tk = tk or _tile(K, 640 if mode == "tn" else 1408)
    nk = K // tk
    if mode == "tn":
        a_spec = pl.BlockSpec((tk, tm), lambda i, j, k: (k, i))
        dims = (((0,), (0,)), ((), ()))
    else:
        a_spec = pl.BlockSpec((tm, tk), lambda i, j, k: (i, k))
        dims = (((1,), (1 if mode == "nt" else 0,)), ((), ()))
    if mode == "nt":
        b_spec = pl.BlockSpec((tn, tk), lambda i, j, k: (j, k))
    else:
        b_spec = pl.BlockSpec((tk, tn), lambda i, j, k: (k, j))

    def body(a_ref, b_ref, o_ref, acc_ref):
        k = pl.program_id(2)

        @pl.when(k == 0)
        def _():
            acc_ref[...] = jnp.zeros_like(acc_ref)

        acc_ref[...] += lax.dot_general(a_ref[...], b_ref[...], dims, preferred_element_type=F32)

        @pl.when(k == nk - 1)
        def _():
            o_ref[...] = acc_ref[...].astype(out_dtype)

    return pl.pallas_call(
        body, name=name, grid=(M // tm, N // tn, nk),
        in_specs=[a_spec, b_spec],
        out_specs=pl.BlockSpec((tm, tn), lambda i, j, k: (i, j)),
        out_shape=jax.ShapeDtypeStruct((M, N), out_dtype),
        scratch_shapes=[pltpu.VMEM((tm, tn), F32)],
        compiler_params=_params(("parallel", "parallel", "arbitrary")),
    )(a, b)


def _rms_fwd(x, g, T, name):
    Lp, D = x.shape

    def body(x_ref, g_ref, o_ref):
        o_ref[...] = _rms(x_ref[...], g_ref[...]).astype(BF16)

    return pl.pallas_call(
        body, name=name, grid=(Lp // T,),
        in_specs=[pl.BlockSpec((T, D), lambda i: (i, 0)), pl.BlockSpec((1, D), lambda i: (0, 0))],
        out_specs=pl.BlockSpec((T, D), lambda i: (i, 0)),
        out_shape=jax.ShapeDtypeStruct((Lp, D), BF16),
        compiler_params=_params(("parallel",)),
    )(x, g)


def _rms_bwd(x, g, dy, dres, T, name):
    Lp, D = x.shape

    def body(x_ref, g_ref, dy_ref, dr_ref, dx_ref, dxb_ref, dg_ref):
        @pl.when(pl.program_id(0) == 0)
        def _():
            dg_ref[...] = jnp.zeros_like(dg_ref)

        _, vjp = jax.vjp(_rms, x_ref[...], g_ref[...])
        dx, dg = vjp(dy_ref[...])
        dx = dx + dr_ref[...]
        dx_ref[...] = dx
        dxb_ref[...] = dx.astype(BF16)
        dg_ref[...] += dg

    row = pl.BlockSpec((T, D), lambda i: (i, 0))
    vec = pl.BlockSpec((1, D), lambda i: (0, 0))
    return pl.pallas_call(
        body, name=name, grid=(Lp // T,),
        in_specs=[row, vec, row, row],
        out_specs=[row, row, vec],
        out_shape=[jax.ShapeDtypeStruct((Lp, D), F32), jax.ShapeDtypeStruct((Lp, D), BF16),
                   jax.ShapeDtypeStruct((1, D), F32)],
        compiler_params=_params(("arbitrary",)),
    )(x, g, dy, dres)


def _swiglu(gate, up):
    return gate * jax.nn.sigmoid(gate) * up


def _swiglu_fwd(gu, name):
    Lp, F2 = gu.shape
    F = F2 // 2
    TR = 128

    def body(gu_ref, o_ref):
        o_ref[...] = _swiglu(gu_ref[:, :F], gu_ref[:, F:]).astype(BF16)

    return pl.pallas_call(
        body, name=name, grid=(Lp // TR,),
        in_specs=[pl.BlockSpec((TR, F2), lambda i: (i, 0))],
        out_specs=pl.BlockSpec((TR, F), lambda i: (i, 0)),
        out_shape=jax.ShapeDtypeStruct((Lp, F), BF16),
        compiler_params=_params(("parallel",)),
    )(gu)


def _swiglu_bwd(gu, dact, name):
    Lp, F2 = gu.shape
    F = F2 // 2
    TR = 128

    def body(gu_ref, da_ref, o_ref):
        _, vjp = jax.vjp(_swiglu, gu_ref[:, :F], gu_ref[:, F:])
        dg, du = vjp(da_ref[...])
        o_ref[:, :F] = dg.astype(BF16)
        o_ref[:, F:] = du.astype(BF16)

    return pl.pallas_call(
        body, name=name, grid=(Lp // TR,),
        in_specs=[pl.BlockSpec((TR, F2), lambda i: (i, 0)), pl.BlockSpec((TR, F), lambda i: (i, 0))],
        out_specs=pl.BlockSpec((TR, F2), lambda i: (i, 0)),
        out_shape=jax.ShapeDtypeStruct((Lp, F2), BF16),
        compiler_params=_params(("parallel",)),
    )(gu, dact)


def _final_loss(h, g, target, name):
    Lp, D = h.shape
    TR = HEAD_ROWS

    def loss_fn(hh, gg, tt):
        err = _rms(hh, gg) - tt
        return 0.5 * jnp.sum(jnp.mean(err * err, axis=-1))

    def body(h_ref, g_ref, t_ref, loss_ref, dh_ref, dg_ref):
        i = pl.program_id(0)

        @pl.when(i == 0)
        def _():
            loss_ref[...] = jnp.zeros_like(loss_ref)
            dg_ref[...] = jnp.zeros_like(dg_ref)
            dh_ref[...] = jnp.zeros_like(dh_ref)

        @pl.when(i > 0)
        def _():
            val, vjp = jax.vjp(lambda hh, gg: loss_fn(hh, gg, t_ref[...]), h_ref[...], g_ref[...])
            dh, dg = vjp(jnp.ones((), F32))
            dh_ref[...] = dh
            dg_ref[...] += dg
            loss_ref[...] += val

    return pl.pallas_call(
        body, name=name, grid=(Lp // TR,),
        in_specs=[pl.BlockSpec((TR, D), lambda i: (i, 0)), pl.BlockSpec((1, D), lambda i: (0, 0)),
                  pl.BlockSpec((TR, D), lambda i: (jnp.maximum(i - 1, 0), 0))],
        out_specs=[pl.BlockSpec((8, LANES), lambda i: (0, 0)), pl.BlockSpec((TR, D), lambda i: (i, 0)),
                   pl.BlockSpec((1, D), lambda i: (0, 0))],
        out_shape=[jax.ShapeDtypeStruct((8, LANES), F32), jax.ShapeDtypeStruct((Lp, D), F32),
                   jax.ShapeDtypeStruct((1, D), F32)],
        compiler_params=_params(("arbitrary",)),
    )(h, g, target)


def _lane_lo():
    return lax.broadcasted_iota(jnp.int32, (1, LANES), 1) < FOX_DH


def _headnorm(x, g, scale):
    lo = _lane_lo()
    x2 = x * x
    s0 = jnp.sum(jnp.where(lo, x2, 0.0), axis=-1, keepdims=True)
    s1 = jnp.sum(jnp.where(lo, 0.0, x2), axis=-1, keepdims=True)
    r = jnp.where(lo, lax.rsqrt(s0 / FOX_DH + EPS), lax.rsqrt(s1 / FOX_DH + EPS))
    return x * r * g * scale


def _fox_prep_fwd(proj, bf, qg, kg, T, D, name):
    Lp = proj.shape[0]
    nb = D // LANES
    scale = FOX_DH ** -0.5

    def body(q_ref, k_ref, v_ref, fl_ref, bf_ref, qg_ref, kg_ref, qn_ref, kn_ref, vb_ref, c_ref, carry_ref):
        @pl.when(pl.program_id(0) == 0)
        def _():
            carry_ref[...] = jnp.zeros_like(carry_ref)

        for b in range(nb):
            sl = slice(b * LANES, (b + 1) * LANES)
            qn_ref[:, sl] = _headnorm(q_ref[:, sl], qg_ref[...], scale).astype(BF16)
            kn_ref[:, sl] = _headnorm(k_ref[:, sl], kg_ref[...], 1.0).astype(BF16)
        vb_ref[...] = v_ref[...].astype(BF16)
        log_f = _log_sigmoid(fl_ref[...] + bf_ref[...])
        row = lax.broadcasted_iota(jnp.int32, (T, T), 0)
        col = lax.broadcasted_iota(jnp.int32, (T, T), 1)
        tri = (col <= row).astype(F32)
        c = jnp.dot(tri, log_f, precision=HI, preferred_element_type=F32) + carry_ref[...]
        c_ref[...] = c
        last = lax.broadcasted_iota(jnp.int32, (T, 1), 0) == T - 1
        carry_ref[...] = jnp.sum(jnp.where(last, c, 0.0), axis=0, keepdims=True)

    wide = lambda j: pl.BlockSpec((T, D), lambda i: (i, j))
    vec = pl.BlockSpec((1, LANES), lambda i: (0, 0))
    return pl.pallas_call(
        body, name=name, grid=(Lp // T,),
        in_specs=[wide(0), wide(1), wide(2), pl.BlockSpec((T, LANES), lambda i: (i, 4 * nb)), vec, vec, vec],
        out_specs=[wide(0), wide(0), wide(0), pl.BlockSpec((T, LANES), lambda i: (i, 0))],
        out_shape=[jax.ShapeDtypeStruct((Lp, D), BF16)] * 3 + [jax.ShapeDtypeStruct((Lp, LANES), F32)],
        scratch_shapes=[pltpu.VMEM((1, LANES), F32)],
        compiler_params=_params(("arbitrary",)),
    )(proj, proj, proj, proj, bf, qg, kg)


def _fox_prep_bwd(proj, bf, qg, kg, dqn, dkn, dc, T, D, name):
    Lp = proj.shape[0]
    nb = D // LANES
    nt = Lp // T
    scale = FOX_DH ** -0.5

    def body(q_ref, k_ref, fl_ref, bf_ref, qg_ref, kg_ref, dqn_ref, dkn_ref, dc_ref,
             dq_ref, dk_ref, dfl_ref, sm_ref, carry_ref):
        @pl.when(pl.program_id(0) == 0)
        def _():
            carry_ref[...] = jnp.zeros_like(carry_ref)
            sm_ref[...] = jnp.zeros_like(sm_ref)

        dqg = jnp.zeros((1, LANES), F32)
        dkg = jnp.zeros((1, LANES), F32)
        for b in range(nb):
            sl = slice(b * LANES, (b + 1) * LANES)
            _, vjp = jax.vjp(lambda x, g: _headnorm(x, g, scale), q_ref[:, sl], qg_ref[...])
            dx, dg = vjp(dqn_ref[:, sl])
            dq_ref[:, sl] = dx.astype(BF16)
            dqg = dqg + dg
            _, vjp = jax.vjp(lambda x, g: _headnorm(x, g, 1.0), k_ref[:, sl], kg_ref[...])
            dx, dg = vjp(dkn_ref[:, sl])
            dk_ref[:, sl] = dx.astype(BF16)
            dkg = dkg + dg
        dcv = dc_ref[...]
        row = lax.broadcasted_iota(jnp.int32, (T, T), 0)
        col = lax.broadcasted_iota(jnp.int32, (T, T), 1)
        triu = (col >= row).astype(F32)
        dlogf = jnp.dot(triu, dcv, precision=HI, preferred_element_type=F32) + carry_ref[...]
        carry_ref[...] += jnp.sum(dcv, axis=0, keepdims=True)
        _, vjp = jax.vjp(_log_sigmoid, fl_ref[...] + bf_ref[...])
        (dfl,) = vjp(dlogf)
        dfl_ref[...] = dfl.astype(BF16)
        sm_ref[0:1, :] += jnp.sum(dfl, axis=0, keepdims=True)
        sm_ref[1:2, :] += dqg
        sm_ref[2:3, :] += dkg

    wide = lambda j: pl.BlockSpec((T, D), lambda i: (nt - 1 - i, j))
    narrow = lambda j: pl.BlockSpec((T, LANES), lambda i: (nt - 1 - i, j))
    vec = pl.BlockSpec((1, LANES), lambda i: (0, 0))
    return pl.pallas_call(
        body, name=name, grid=(nt,),
        in_specs=[wide(0), wide(1), narrow(4 * nb), vec, vec, vec, wide(0), wide(0), narrow(0)],
        out_specs=[wide(0), wide(0), narrow(0), pl.BlockSpec((8, LANES), lambda i: (0, 0))],
        out_shape=[jax.ShapeDtypeStruct((Lp, D), BF16)] * 2 + [jax.ShapeDtypeStruct((Lp, LANES), BF16),
                                                                 jax.ShapeDtypeStruct((8, LANES), F32)],
        scratch_shapes=[pltpu.VMEM((1, LANES), F32)],
        compiler_params=_params(("arbitrary",)),
    )(proj, proj, proj, bf, qg, kg, dqn, dkn, dc)


def _ln2_ceil(m):
    return jnp.ceil(m * (1.0 / math.log(2.0))) * math.log(2.0)


def _fox_mask(i, k0, T):
    qpos = i * T + lax.broadcasted_iota(jnp.int32, (T, 1), 0)
    kpos = k0 + lax.broadcasted_iota(jnp.int32, (1, T), 1)
    return (kpos <= qpos) & ((kpos >= N_PAD) | (qpos < N_PAD))


def _pick_col(blk, idx):
    lane = lax.broadcasted_iota(jnp.int32, (1, LANES), 1)
    return jnp.sum(jnp.where(lane == idx, blk, 0.0), axis=1, keepdims=True)


def _split_halves(blk):
    lo = _lane_lo()
    return (jnp.max(jnp.where(lo, blk, -jnp.inf), axis=1, keepdims=True),
            jnp.max(jnp.where(lo, -jnp.inf, blk), axis=1, keepdims=True))


def _fox_attn_fwd(qn, kn, vb, c, cT, proj, T, D, name):
    Lp = qn.shape[0]
    P = D // LANES
    nt = Lp // T
    H = cT.shape[0]

    def body(q_ref, k_ref, v_ref, c_ref, cT_ref, g_ref, o_ref, og_ref, m_ref, li_ref):
        p = pl.program_id(0)
        i = pl.program_id(1)
        lo = _lane_lo()
        q = q_ref[...]
        zero = jnp.zeros_like(q)
        qh = (jnp.where(lo, q, zero), jnp.where(lo, zero, q))
        cblk = c_ref[...]
        cq = tuple(_pick_col(cblk, 2 * p + h) for h in (0, 1))

        def step(j, carry):
            k0 = pl.multiple_of(j * T, LANES)
            kj = k_ref[pl.ds(k0, T), :]
            vj = v_ref[pl.ds(k0, T), :]
            mask = _fox_mask(i, k0, T)
            out = []
            for h in (0, 1):
                m, l, acc = carry[h]
                ck = cT_ref[pl.ds(2 * p + h, 1), pl.ds(k0, T)]
                s = lax.dot_general(qh[h], kj, (((1,), (1,)), ((), ())), preferred_element_type=F32)
                s = jnp.where(mask, s + (cq[h] - ck), NEG)
                m_new = _ln2_ceil(jnp.maximum(m, jnp.max(s, axis=1, keepdims=True)))
                alpha = jnp.exp(m - m_new)
                pr = jnp.exp(s - m_new).astype(BF16)
                l = alpha * l + jnp.sum(pr.astype(F32), axis=1, keepdims=True)
                acc = alpha * acc + jnp.dot(pr, vj, preferred_element_type=F32)
                out.append((m_new, l, acc))
            return tuple(out)

        init = tuple((jnp.full((T, 1), NEG, F32), jnp.zeros((T, 1), F32), jnp.zeros((T, LANES), F32))
                     for _ in (0, 1))
        (m0, l0, a0), (m1, l1, a1) = lax.fori_loop(0, i + 1, step, init)
        o = jnp.where(lo, a0 / l0, a1 / l1)
        o_ref[...] = o
        m_ref[...] = jnp.where(lo, m0, m1)
        li_ref[...] = jnp.where(lo, 1.0 / l0, 1.0 / l1)
        og_ref[...] = (o * jax.nn.sigmoid(g_ref[...])).astype(BF16)

    tile = pl.BlockSpec((T, LANES), lambda p, i: (i, p))
    full = pl.BlockSpec((Lp, LANES), lambda p, i: (0, p))
    return pl.pallas_call(
        body, name=name, grid=(P, nt),
        in_specs=[tile, full, full, pl.BlockSpec((T, LANES), lambda p, i: (i, 0)),
                  pl.BlockSpec((H, Lp), lambda p, i: (0, 0)),
                  pl.BlockSpec((T, LANES), lambda p, i: (i, 3 * P + p))],
        out_specs=[tile, tile, tile, tile],
        out_shape=[jax.ShapeDtypeStruct((Lp, D), F32), jax.ShapeDtypeStruct((Lp, D), BF16),
                   jax.ShapeDtypeStruct((Lp, D), F32), jax.ShapeDtypeStruct((Lp, D), F32)],
        compiler_params=_params(("parallel", "arbitrary")),
    )(qn, kn, vb, c, cT, proj)


def _fox_gate_bwd(dog, o, proj, T, D, name):
    Lp = o.shape[0]
    P = D // LANES

    def body(dog_ref, o_ref, g_ref, do_ref, dg_ref, dl_ref):
        lo = _lane_lo()
        sig = jax.nn.sigmoid(g_ref[...])
        ov = o_ref[...]
        do = (dog_ref[...] * sig).astype(BF16)
        do_ref[...] = do
        dg_ref[...] = (dog_ref[...] * ov * sig * (1.0 - sig)).astype(BF16)
        t = do.astype(F32) * ov
        d0 = jnp.sum(jnp.where(lo, t, 0.0), axis=1, keepdims=True)
        d1 = jnp.sum(jnp.where(lo, 0.0, t), axis=1, keepdims=True)
        dl_ref[...] = jnp.where(lo, d0, d1)

    tile = pl.BlockSpec((T, LANES), lambda i, p: (i, p))
    return pl.pallas_call(
        body, name=name, grid=(Lp // T, P),
        in_specs=[tile, tile, pl.BlockSpec((T, LANES), lambda i, p: (i, 3 * P + p))],
        out_specs=[tile, tile, tile],
        out_shape=[jax.ShapeDtypeStruct((Lp, D), BF16), jax.ShapeDtypeStruct((Lp, D), BF16),
                   jax.ShapeDtypeStruct((Lp, D), F32)],
        compiler_params=_params(("parallel", "parallel")),
    )(dog, o, proj)


def _fox_attn_bwd(qn, kn, vb, c, cT, do, mshift, linv, delta, T, D, name):
    Lp = qn.shape[0]
    P = D // LANES
    nt = Lp // T
    H = cT.shape[0]

    def body(q_ref, do_ref, m_ref, li_ref, dl_ref, c_ref, k_ref, v_ref, cT_ref, dq_ref, dk_ref, dv_ref, dc_ref):
        p = pl.program_id(0)
        i = pl.program_id(1)

        @pl.when(i == 0)
        def _():
            dk_ref[...] = jnp.zeros_like(dk_ref)
            dv_ref[...] = jnp.zeros_like(dv_ref)
            dc_ref[...] = jnp.zeros_like(dc_ref)

        lo = _lane_lo()
        q = q_ref[...]
        do = do_ref[...]
        zero = jnp.zeros_like(q)
        qh = (jnp.where(lo, q, zero), jnp.where(lo, zero, q))
        doh = (jnp.where(lo, do, zero), jnp.where(lo, zero, do))
        msh = _split_halves(m_ref[...])
        linv = _split_halves(li_ref[...])
        dlt = _split_halves(dl_ref[...])
        cblk = c_ref[...]
        cq = tuple(_pick_col(cblk, 2 * p + h) for h in (0, 1))

        def step(j, carry):
            k0 = pl.multiple_of(j * T, LANES)
            kj = k_ref[pl.ds(k0, T), :]
            vj = v_ref[pl.ds(k0, T), :]
            mask = _fox_mask(i, k0, T)
            dqs, dks, dvs = [], [], []
            for h in (0, 1):
                ck = cT_ref[pl.ds(2 * p + h, 1), pl.ds(k0, T)]
                s = lax.dot_general(qh[h], kj, (((1,), (1,)), ((), ())), preferred_element_type=F32)
                s = jnp.where(mask, s + (cq[h] - ck), NEG)
                pr = jnp.exp(s - msh[h]).astype(BF16).astype(F32) * linv[h]
                dp = lax.dot_general(doh[h], vj, (((1,), (1,)), ((), ())), preferred_element_type=F32)
                ds = pr * (dp - dlt[h])
                dsb = ds.astype(BF16)
                dqs.append(carry[h] + jnp.dot(dsb, kj, preferred_element_type=F32))
                dks.append(lax.dot_general(dsb, q, (((0,), (0,)), ((), ())), preferred_element_type=F32))
                dvs.append(lax.dot_general(pr.astype(BF16), do, (((0,), (0,)), ((), ())),
                                           preferred_element_type=F32))
                dc_ref[0, h:h + 1, pl.ds(k0, T)] += -jnp.sum(ds, axis=0, keepdims=True)
            dk_ref[pl.ds(k0, T), :] += jnp.where(lo, dks[0], dks[1])
            dv_ref[pl.ds(k0, T), :] += jnp.where(lo, dvs[0], dvs[1])
            return tuple(dqs)

        init = (jnp.zeros((T, LANES), F32), jnp.zeros((T, LANES), F32))
        dq0, dq1 = lax.fori_loop(0, i + 1, step, init)
        dq_ref[...] = jnp.where(lo, dq0, dq1)

    tile = pl.BlockSpec((T, LANES), lambda p, i: (i, p))
    full = pl.BlockSpec((Lp, LANES), lambda p, i: (0, p))
    return pl.pallas_call(
        body, name=name, grid=(P, nt),
        in_specs=[tile, tile, tile, tile, tile, pl.BlockSpec((T, LANES), lambda p, i: (i, 0)), full, full,
                  pl.BlockSpec((H, Lp), lambda p, i: (0, 0))],
        out_specs=[tile, full, full, pl.BlockSpec((1, 8, Lp), lambda p, i: (p, 0, 0))],
        out_shape=[jax.ShapeDtypeStruct((Lp, D), F32)] * 3 + [jax.ShapeDtypeStruct((P, 8, Lp), F32)],
        compiler_params=_params(("parallel", "arbitrary")),
    )(qn, do, mshift, linv, delta, c, kn, vb, cT)


def _hgrn_chunk(St, qr, z, vi, go, p0, p1, gg):
    C = qr.shape[0]
    lb = jax.nn.sigmoid(p1 - p0)
    a = jnp.log(lb)
    cc = jnp.log1p(-lb) + _log_sigmoid(z)
    log_f = jnp.maximum(a, cc) + jnp.log1p(jnp.exp(-jnp.abs(a - cc)))
    k = (1.0 - lb) * jax.nn.sigmoid(-z)
    q = qr * jax.nn.sigmoid(qr)
    row = lax.broadcasted_iota(jnp.int32, (C, C), 0)
    col = lax.broadcasted_iota(jnp.int32, (C, C), 1)
    causal = col <= row
    b = jnp.dot(causal.astype(F32), log_f, precision=HI, preferred_element_type=F32)
    mid = lax.broadcasted_iota(jnp.int32, (C, 1), 0) == C // 2 - 1
    r = jnp.sum(jnp.where(mid, b, 0.0), axis=0, keepdims=True)
    b_last = jnp.sum(log_f, axis=0, keepdims=True)
    attn = jnp.where(causal, _d_nt(q * jnp.exp(b - r), k * jnp.exp(r - b)), 0.0)
    o = _d_nn(attn, vi) + _d_nt(q * jnp.exp(b), St)
    St_new = St * jnp.exp(b_last) + _d_tn(vi, k * jnp.exp(b_last - b))
    og = _rms(o, gg) * (go * jax.nn.sigmoid(go))
    return St_new, og


def _hgrn_specs(T, H, rev_nt=None):
    if rev_nt is None:
        return [pl.BlockSpec((T, LANES), functools.partial(lambda h, t, g: (t, g * H + h), g=g)) for g in range(4)]
    return [pl.BlockSpec((T, LANES), functools.partial(lambda h, t, g: (rev_nt - 1 - t, g * H + h), g=g))
            for g in range(4)]


def _hgrn_fwd(proj, lbp, gg, T, name):
    Lp = proj.shape[0]
    D = proj.shape[1] // 4
    H = D // LANES
    nt = Lp // T
    ncc = T // HGRN_CHUNK

    def body(q_ref, z_ref, i_ref, go_ref, p_ref, gg_ref, og_ref, ss_ref, st_ref):
        @pl.when(pl.program_id(1) == 0)
        def _():
            st_ref[...] = jnp.zeros_like(st_ref)

        p0 = p_ref[0:1, :]
        p1 = p_ref[1:2, :]
        gain = gg_ref[...]

        def chunk(cidx, carry):
            sl = pl.ds(pl.multiple_of(cidx * HGRN_CHUNK, HGRN_CHUNK), HGRN_CHUNK)
            St = st_ref[...]
            ss_ref[0, cidx] = St
            St_new, og = _hgrn_chunk(St, q_ref[sl, :], z_ref[sl, :], i_ref[sl, :], go_ref[sl, :], p0, p1, gain)
            st_ref[...] = St_new
            og_ref[sl, :] = og.astype(BF16)
            return carry

        lax.fori_loop(0, ncc, chunk, 0)

    return pl.pallas_call(
        body, name=name, grid=(H, nt),
        in_specs=_hgrn_specs(T, H) + [pl.BlockSpec((2, LANES), lambda h, t: (0, h)),
                                      pl.BlockSpec((1, LANES), lambda h, t: (0, 0))],
        out_specs=[pl.BlockSpec((T, LANES), lambda h, t: (t, h)),
                   pl.BlockSpec((1, ncc, LANES, LANES), lambda h, t: (h, t, 0, 0))],
        out_shape=[jax.ShapeDtypeStruct((Lp, D), BF16),
                   jax.ShapeDtypeStruct((H, Lp // HGRN_CHUNK, LANES, LANES), F32)],
        scratch_shapes=[pltpu.VMEM((LANES, LANES), F32)],
        compiler_params=_params(("parallel", "arbitrary")),
    )(proj, proj, proj, proj, lbp, gg)


def _hgrn_bwd(proj, lbp, gg, dog, ss, T, name):
    Lp = proj.shape[0]
    D = proj.shape[1] // 4
    H = D // LANES
    nt = Lp // T
    ncc = T // HGRN_CHUNK

    def body(q_ref, z_ref, i_ref, go_ref, p_ref, gg_ref, dog_ref, ss_ref,
             dq_ref, dz_ref, di_ref, dgo_ref, dp_ref, dgg_ref, dst_ref):
        h = pl.program_id(0)
        t = pl.program_id(1)

        @pl.when(t == 0)
        def _():
            dst_ref[...] = jnp.zeros_like(dst_ref)
            dp_ref[...] = jnp.zeros_like(dp_ref)

        @pl.when((t == 0) & (h == 0))
        def _():
            dgg_ref[...] = jnp.zeros_like(dgg_ref)

        p0 = p_ref[0:1, :]
        p1 = p_ref[1:2, :]
        gain = gg_ref[...]
        row0 = (nt - 1 - t) * T

        def chunk(cc, carry):
            cidx = ncc - 1 - cc
            r0 = pl.multiple_of(cidx * HGRN_CHUNK, HGRN_CHUNK)
            sl = pl.ds(r0, HGRN_CHUNK)
            _, vjp = jax.vjp(_hgrn_chunk, ss_ref[0, cidx], q_ref[sl, :], z_ref[sl, :], i_ref[sl, :],
                             go_ref[sl, :], p0, p1, gain)
            dSt, dq, dz, di, dgo, dp0, dp1, dgain = vjp((dst_ref[...], dog_ref[sl, :]))
            real = (row0 + r0 + lax.broadcasted_iota(jnp.int32, (HGRN_CHUNK, 1), 0)) >= N_PAD
            dq_ref[sl, :] = jnp.where(real, dq, 0.0).astype(BF16)
            dz_ref[sl, :] = jnp.where(real, dz, 0.0).astype(BF16)
            di_ref[sl, :] = jnp.where(real, di, 0.0).astype(BF16)
            dgo_ref[sl, :] = jnp.where(real, dgo, 0.0).astype(BF16)
            dst_ref[...] = dSt
            dp_ref[0:1, :] += dp0
            dp_ref[1:2, :] += dp1
            dgg_ref[0:1, :] += dgain
            return carry

        lax.fori_loop(0, ncc, chunk, 0)

    rev = pl.BlockSpec((T, LANES), lambda h, t: (nt - 1 - t, h))
    return pl.pallas_call(
        body, name=name, grid=(H, nt),
        in_specs=_hgrn_specs(T, H, nt) + [pl.BlockSpec((2, LANES), lambda h, t: (0, h)),
                                          pl.BlockSpec((1, LANES), lambda h, t: (0, 0)), rev,
                                          pl.BlockSpec((1, ncc, LANES, LANES), lambda h, t: (h, nt - 1 - t, 0, 0))],
        out_specs=[rev, rev, rev, rev, pl.BlockSpec((8, LANES), lambda h, t: (0, h)),
                   pl.BlockSpec((8, LANES), lambda h, t: (0, 0))],
        out_shape=[jax.ShapeDtypeStruct((Lp, D), BF16)] * 4 + [jax.ShapeDtypeStruct((8, D), F32),
                                                                 jax.ShapeDtypeStruct((8, LANES), F32)],
        scratch_shapes=[pltpu.VMEM((LANES, LANES), F32)],
        compiler_params=_params(("arbitrary", "arbitrary")),
    )(proj, proj, proj, proj, lbp, gg, dog, ss)


def _exchange(arrays, per_peer, name):
    n_arr = len(arrays)
    HBM = pl.BlockSpec(memory_space=pltpu.HBM)

    def body(*refs):
        ins = refs[:n_arr]
        outs = refs[n_arr:2 * n_arr]
        send_sems, recv_sems, local_sems = refs[2 * n_arr:]
        x, y, c = lax.axis_index("x"), lax.axis_index("y"), lax.axis_index("c")
        me = 4 * x + 2 * y + c
        local, remote = [], []
        for n in range(n_arr):
            src = ins[n].at[me] if per_peer[n] else ins[n]
            cp = pltpu.make_async_copy(src, outs[n].at[me], local_sems.at[n])
            cp.start()
            local.append(cp)
        for rel in range(1, N_DEV):
            fx, fy, fc = (rel >> 2) & 1, (rel >> 1) & 1, rel & 1
            px = 1 - x if fx else x
            py = 1 - y if fy else y
            pc = 1 - c if fc else c
            peer = 4 * px + 2 * py + pc
            for n in range(n_arr):
                src = ins[n].at[peer] if per_peer[n] else ins[n]
                cp = pltpu.make_async_remote_copy(
                    src_ref=src, dst_ref=outs[n].at[me],
                    send_sem=send_sems.at[n * (N_DEV - 1) + rel - 1],
                    recv_sem=recv_sems.at[n * (N_DEV - 1) + rel - 1],
                    device_id=(px, py, pc), device_id_type=pl.DeviceIdType.MESH)
                cp.start()
                remote.append(cp)
        for cp in remote:
            cp.wait()
        for cp in local:
            cp.wait()

    out_shape = [jax.ShapeDtypeStruct((N_DEV,) + a.shape[-2:], a.dtype) for a in arrays]
    return pl.pallas_call(
        body, name=name,
        in_specs=[HBM] * n_arr, out_specs=[HBM] * n_arr, out_shape=out_shape,
        scratch_shapes=[pltpu.SemaphoreType.DMA((n_arr * (N_DEV - 1),)),
                        pltpu.SemaphoreType.DMA((n_arr * (N_DEV - 1),)),
                        pltpu.SemaphoreType.DMA((n_arr,))],
    )(*arrays)


def _adamw(recv, w, m, v, name):
    R = w.shape[0]
    rc = _row_chunk(R, 2400)

    def body(r_ref, w_ref, m_ref, v_ref, g_ref, d_ref, mo_ref, vo_ref):
        g = r_ref[0]
        for s in range(1, N_DEV):
            g = g + r_ref[s]
        mn = ADAM_B1 * m_ref[...] + (1.0 - ADAM_B1) * g
        vn = ADAM_B2 * v_ref[...] + (1.0 - ADAM_B2) * (g * g)
        m_hat = mn / (1.0 - ADAM_B1 ** ADAM_STEP)
        v_hat = vn / (1.0 - ADAM_B2 ** ADAM_STEP)
        g_ref[...] = g
        d_ref[...] = -ADAM_LR * (m_hat / (jnp.sqrt(v_hat) + ADAM_EPS) + ADAM_WD * w_ref[...])
        mo_ref[...] = mn
        vo_ref[...] = vn

    row = pl.BlockSpec((rc, LANES), lambda i: (i, 0))
    return pl.pallas_call(
        body, name=name, grid=(R // rc,),
        in_specs=[pl.BlockSpec((N_DEV, rc, LANES), lambda i: (0, i, 0)), row, row, row],
        out_specs=[row] * 4,
        out_shape=[jax.ShapeDtypeStruct((R, LANES), F32)] * 4,
        compiler_params=_params(("parallel",)),
    )(recv, w, m, v)


def _pack_rows(arrs, dtype):
    return jnp.concatenate([a.astype(dtype).reshape(-1, LANES) for a in arrs], axis=0)


def _unpack_rows(packed, shapes):
    out, off = [], 0
    for shp in shapes:
        n = math.prod(shp) // LANES
        out.append(packed[..., off:off + n, :].reshape(packed.shape[:-2] + tuple(shp)))
        off += n
    return out


def _gathered_to_full(g, name):
    if name in COL_SHARDED:
        g = jnp.moveaxis(g, 0, -2)
        return g.reshape(g.shape[:-2] + (g.shape[-2] * g.shape[-1],))
    g = jnp.moveaxis(g, 0, -3)
    return g.reshape(g.shape[:-3] + (g.shape[-3] * g.shape[-2], g.shape[-1]))


def _full_to_slabs(full, name):
    if name in COL_SHARDED:
        f = full.reshape(full.shape[:-1] + (N_DEV, full.shape[-1] // N_DEV))
        return jnp.moveaxis(f, -2, 0)
    f = full.reshape(full.shape[:-2] + (N_DEV, full.shape[-2] // N_DEV, full.shape[-1]))
    return jnp.moveaxis(f, -3, 0)


def _pack_small(arrs):
    rows = []
    for a in arrs:
        flat = a.astype(F32).reshape(-1)
        pad = (-flat.shape[0]) % LANES
        rows.append(jnp.pad(flat, (0, pad)).reshape(-1, LANES))
    p = jnp.concatenate(rows, axis=0)
    return jnp.pad(p, ((0, (-p.shape[0]) % 8), (0, 0)))


def _unpack_small(packed, shapes):
    out, off = [], 0
    for shp in shapes:
        n = math.prod(shp)
        nr = -(-n // LANES)
        out.append(packed[off:off + nr].reshape(-1)[:n].reshape(shp))
        off += nr
    return out


def _local_step(x, target, meta, wb, small):
    S, D = x.shape
    Lp = S + HEAD_ROWS
    T = 640 if Lp % 640 == 0 else 128
    P = D // LANES
    row = lambda v: v.reshape(1, -1).astype(F32)

    w_fin = jnp.pad(wb["fox_w_in"][0], ((0, 0), (0, LANES - wb["fox_w_in"].shape[-1] % LANES)))
    w_fout, w_hin, w_hout = wb["fox_w_out"][0], wb["hgrn_w_in"][0], wb["hgrn_w_out"][0]
    w_uin, w_uout = wb["ffn_w_in"], wb["ffn_w_out"]
    n_heads = wb["fox_w_in"].shape[-1] - 4 * D
    bf = jnp.pad(row(small["fox_b_f"]), ((0, 0), (0, LANES - small["fox_b_f"].size)))
    qg = jnp.tile(row(small["fox_q_norm"]), (1, 2))
    kg = jnp.tile(row(small["fox_k_norm"]), (1, 2))

    h0 = jnp.concatenate([jnp.zeros((N_PAD, D), F32), meta, x], axis=0)

    hn0 = _rms_fwd(h0, row(small["attn_norm"][0]), T, "rms0_fwd")
    proj0 = _mm(hn0, w_fin, "nn", F32, "fox_in_fwd")
    qn, kn, vb, c = _fox_prep_fwd(proj0, bf, qg, kg, T, D, "fox_prep_fwd")
    cT = c.T[:2 * P]
    o, og0, mshift, linv = _fox_attn_fwd(qn, kn, vb, c, cT, proj0, T, D, "fox_attn_fwd")
    h1 = h0 + _mm(og0, w_fout, "nn", F32, "fox_out_fwd")
    hf0 = _rms_fwd(h1, row(small["ffn_norm"][0]), T, "rmsf0_fwd")
    gu0 = _mm(hf0, w_uin[0], "nn", F32, "ffn0_in_fwd")
    act0 = _swiglu_fwd(gu0, "swiglu0_fwd")
    h2 = h1 + _mm(act0, w_uout[0], "nn", F32, "ffn0_out_fwd")
    hn1 = _rms_fwd(h2, row(small["attn_norm"][1]), T, "rms1_fwd")
    proj1 = _mm(hn1, w_hin, "nn", F32, "hgrn_in_fwd")
    lbp = small["hgrn_lower_bounds"].astype(F32)
    ggn = row(small["hgrn_g_norm"])
    og1, ss = _hgrn_fwd(proj1, lbp, ggn, T, "hgrn_fwd")
    h3 = h2 + _mm(og1, w_hout, "nn", F32, "hgrn_out_fwd")
    hf1 = _rms_fwd(h3, row(small["ffn_norm"][1]), T, "rmsf1_fwd")
    gu1 = _mm(hf1, w_uin[1], "nn", F32, "ffn1_in_fwd")
    act1 = _swiglu_fwd(gu1, "swiglu1_fwd")
    h4 = h3 + _mm(act1, w_uout[1], "nn", F32, "ffn1_out_fwd")
    loss_blk, dh4, d_final = _final_loss(h4, row(small["final_norm"]), target, "final_loss")
    dh4b = dh4.astype(BF16)

    grads = {}

    def ffn_bwd(i, dh, dhb, h_in, hf, gu, act, tag):
        grads_out = _mm(act, dhb, "tn", F32, f"ffn{i}_out_dw")
        dact = _mm(dhb, w_uout[i], "nt", F32, f"ffn{i}_out_dx")
        dgu = _swiglu_bwd(gu, dact, f"swiglu{i}_bwd")
        grads_in = _mm(hf, dgu, "tn", F32, f"ffn{i}_in_dw")
        dhf = _mm(dgu, w_uin[i], "nt", F32, f"ffn{i}_in_dx")
        dh_new, dh_newb, dgain = _rms_bwd(h_in, row(small["ffn_norm"][i]), dhf, dh, T, f"rmsf{i}_bwd")
        return dh_new, dh_newb, grads_in, grads_out, dgain

    dh3, dh3b, g_uin1, g_uout1, d_fn1 = ffn_bwd(1, dh4, dh4b, h3, hf1, gu1, act1, "1")
    grads["hgrn_w_out"] = _mm(og1, dh3b, "tn", F32, "hgrn_out_dw")[None]
    dog1 = _mm(dh3b, w_hout, "nt", F32, "hgrn_out_dx")
    dq1, dz1, di1, dgo1, d_lb, d_gg = _hgrn_bwd(proj1, lbp, ggn, dog1, ss, T, "hgrn_bwd")
    dproj1 = jnp.concatenate([dq1, dz1, di1, dgo1], axis=1)
    grads["hgrn_w_in"] = _mm(hn1, dproj1, "tn", F32, "hgrn_in_dw")[None]
    dhn1 = _mm(dproj1, w_hin, "nt", F32, "hgrn_in_dx")
    dh2, dh2b, d_an1 = _rms_bwd(h2, row(small["attn_norm"][1]), dhn1, dh3, T, "rms1_bwd")
    dh1, dh1b, g_uin0, g_uout0, d_fn0 = ffn_bwd(0, dh2, dh2b, h1, hf0, gu0, act0, "0")
    grads["ffn_w_in"] = jnp.stack([g_uin0, g_uin1])
    grads["ffn_w_out"] = jnp.stack([g_uout0, g_uout1])
    grads["fox_w_out"] = _mm(og0, dh1b, "tn", F32, "fox_out_dw")[None]
    dog0 = _mm(dh1b, w_fout, "nt", F32, "fox_out_dx")
    do, dgate, delta = _fox_gate_bwd(dog0, o, proj0, T, D, "fox_gate_bwd")
    dqn, dkn, dv, dcr = _fox_attn_bwd(qn, kn, vb, c, cT, do, mshift, linv, delta, T, D, "fox_attn_bwd")
    dc = jnp.pad(dcr[:, :2, :].reshape(2 * P, Lp).T, ((0, 0), (0, LANES - 2 * P)))
    dq0, dk0, dfl, sm = _fox_prep_bwd(proj0, bf, qg, kg, dqn, dkn, dc, T, D, "fox_prep_bwd")
    dproj0 = jnp.concatenate([dq0, dk0, dv.astype(BF16), dgate, dfl], axis=1)
    grads["fox_w_in"] = _mm(hn0, dproj0, "tn", F32, "fox_in_dw")[:, :4 * D + n_heads][None]
    dhn0 = _mm(dproj0, w_fin, "nt", F32, "fox_in_dx")
    dh0, _, d_an0 = _rms_bwd(h0, row(small["attn_norm"][0]), dhn0, dh1, T, "rms0_bwd")

    grads["meta_tokens"] = dh0[N_PAD:HEAD_ROWS]
    grads["attn_norm"] = jnp.concatenate([d_an0, d_an1], axis=0)
    grads["ffn_norm"] = jnp.concatenate([d_fn0, d_fn1], axis=0)
    grads["final_norm"] = d_final[0]
    grads["fox_b_f"] = sm[0:1, :n_heads]
    grads["fox_q_norm"] = sm[1:2, :FOX_DH] + sm[1:2, FOX_DH:]
    grads["fox_k_norm"] = sm[2:3, :FOX_DH] + sm[2:3, FOX_DH:]
    grads["hgrn_lower_bounds"] = d_lb[0:2]
    grads["hgrn_g_norm"] = d_gg[0:1]
    return loss_blk[0, 0], dh0[HEAD_ROWS:], grads


def kernel(x, meta_tokens, attn_norm, ffn_norm, final_norm, fox_w_in, fox_b_f, fox_q_norm, fox_k_norm, fox_w_out, hgrn_w_in, hgrn_lower_bounds, hgrn_g_norm, hgrn_w_out, ffn_w_in, ffn_w_out, loss_target, m_meta_tokens, m_attn_norm, m_ffn_norm, m_final_norm, m_fox_w_in, m_fox_b_f, m_fox_q_norm, m_fox_k_norm, m_fox_w_out, m_hgrn_w_in, m_hgrn_lower_bounds, m_hgrn_g_norm, m_hgrn_w_out, m_ffn_w_in, m_ffn_w_out, v_meta_tokens, v_attn_norm, v_ffn_norm, v_final_norm, v_fox_w_in, v_fox_b_f, v_fox_q_norm, v_fox_k_norm, v_fox_w_out, v_hgrn_w_in, v_hgrn_lower_bounds, v_hgrn_g_norm, v_hgrn_w_out, v_ffn_w_in, v_ffn_w_out):
    w = dict(meta_tokens=meta_tokens, attn_norm=attn_norm, ffn_norm=ffn_norm, final_norm=final_norm,
             fox_w_in=fox_w_in, fox_b_f=fox_b_f, fox_q_norm=fox_q_norm, fox_k_norm=fox_k_norm,
             fox_w_out=fox_w_out, hgrn_w_in=hgrn_w_in, hgrn_lower_bounds=hgrn_lower_bounds,
             hgrn_g_norm=hgrn_g_norm, hgrn_w_out=hgrn_w_out, ffn_w_in=ffn_w_in, ffn_w_out=ffn_w_out)
    m = dict(meta_tokens=m_meta_tokens, attn_norm=m_attn_norm, ffn_norm=m_ffn_norm, final_norm=m_final_norm,
             fox_w_in=m_fox_w_in, fox_b_f=m_fox_b_f, fox_q_norm=m_fox_q_norm, fox_k_norm=m_fox_k_norm,
             fox_w_out=m_fox_w_out, hgrn_w_in=m_hgrn_w_in, hgrn_lower_bounds=m_hgrn_lower_bounds,
             hgrn_g_norm=m_hgrn_g_norm, hgrn_w_out=m_hgrn_w_out, ffn_w_in=m_ffn_w_in, ffn_w_out=m_ffn_w_out)
    v = dict(meta_tokens=v_meta_tokens, attn_norm=v_attn_norm, ffn_norm=v_ffn_norm, final_norm=v_final_norm,
             fox_w_in=v_fox_w_in, fox_b_f=v_fox_b_f, fox_q_norm=v_fox_q_norm, fox_k_norm=v_fox_k_norm,
             fox_w_out=v_fox_w_out, hgrn_w_in=v_hgrn_w_in, hgrn_lower_bounds=v_hgrn_lower_bounds,
             hgrn_g_norm=v_hgrn_g_norm, hgrn_w_out=v_hgrn_w_out, ffn_w_in=v_ffn_w_in, ffn_w_out=v_ffn_w_out)
    axes = ("x", "y", "c")
    mats = BIG[1:]
    mat_shapes = [w[n].shape for n in mats]
    big_shapes = [w[n].shape for n in BIG]
    small_shapes = [w[n].shape for n in SMALL]

    wpack = _pack_rows([w[n] for n in mats], BF16)
    mpack = w["meta_tokens"].astype(F32)
    g_w, g_meta = _exchange([wpack, mpack], [False, False], "gather_weights")
    wb = {n: _gathered_to_full(g, n) for n, g in zip(mats, _unpack_rows(g_w, mat_shapes))}
    meta_full = _gathered_to_full(g_meta, "meta_tokens")
    small = {n: w[n] for n in SMALL}

    loss_local, grad_x, grads = _local_step(x[0], loss_target[0], meta_full, wb, small)
    loss = lax.psum(loss_local, axes)

    gpack = jnp.concatenate([_full_to_slabs(grads[n], n).reshape(N_DEV, -1, LANES) for n in BIG], axis=1)
    spack = _pack_small([grads[n] for n in SMALL])
    r_big, r_small = _exchange([gpack, spack], [True, False], "scatter_grads")

    g_b, d_b, m_b, v_b = _adamw(r_big, _pack_rows([w[n] for n in BIG], F32), _pack_rows([m[n] for n in BIG], F32),
                                _pack_rows([v[n] for n in BIG], F32), "adamw_big")
    g_s, d_s, m_s, v_s = _adamw(r_small, _pack_small([w[n] for n in SMALL]), _pack_small([m[n] for n in SMALL]),
                                _pack_small([v[n] for n in SMALL]), "adamw_small")
    outs = []
    for big, sml in ((g_b, g_s), (d_b, d_s), (m_b, m_s), (v_b, v_s)):
        d = dict(zip(BIG, _unpack_rows(big, big_shapes)))
        d.update(zip(SMALL, _unpack_small(sml, small_shapes)))
        outs.extend(d[n] for n in WEIGHTS)
    return (loss, grad_x[None], *outs)
```

```python
import functools
import math

import jax
import jax.numpy as jnp
from jax import lax
from jax.experimental import pallas as pl
from jax.experimental.pallas import tpu as pltpu

F32 = jnp.float32
BF16 = jnp.bfloat16
EPS = 1e-6
N_META = 16
LANES = 128
HEAD_ROWS = 128
N_PAD = HEAD_ROWS - N_META
FOX_DH = 64
HGRN_CHUNK = 64
N_DEV = 8
NEG = -1e30
VMEM_LIMIT = 56 * 1024 * 1024
HI = lax.Precision.HIGHEST

ADAM_LR = 0.001
ADAM_B1 = 0.9
ADAM_B2 = 0.999
ADAM_EPS = 1e-08
ADAM_WD = 0.01
ADAM_STEP = 10

BIG = ("meta_tokens", "fox_w_in", "fox_w_out", "hgrn_w_in", "hgrn_w_out", "ffn_w_in", "ffn_w_out")
SMALL = ("attn_norm", "ffn_norm", "final_norm", "fox_b_f", "fox_q_norm", "fox_k_norm",
         "hgrn_lower_bounds", "hgrn_g_norm")
WEIGHTS = ("meta_tokens", "attn_norm", "ffn_norm", "final_norm", "fox_w_in", "fox_b_f", "fox_q_norm",
           "fox_k_norm", "fox_w_out", "hgrn_w_in", "hgrn_lower_bounds", "hgrn_g_norm", "hgrn_w_out",
           "ffn_w_in", "ffn_w_out")
COL_SHARDED = ("meta_tokens", "fox_w_in", "hgrn_w_in", "ffn_w_in")


def _params(sem=None):
    return pltpu.CompilerParams(dimension_semantics=sem, vmem_limit_bytes=VMEM_LIMIT)


def _tile(n, cap):
    best = None
    for t in range(LANES, min(n, cap) + 1, LANES):
        if n % t == 0:
            best = t
    assert best is not None, (n, cap)
    return best


def _row_chunk(n, cap, mult=8):
    best = n
    for t in range(mult, min(n, cap) + 1, mult):
        if n % t == 0:
            best = t
    return best


def _dg(a, b, ca, cb):
    return lax.dot_general(a.astype(BF16), b.astype(BF16), (((ca,), (cb,)), ((), ())),
                           preferred_element_type=F32)


@jax.custom_vjp
def _d_nn(a, b):
    return _dg(a, b, 1, 0)


@jax.custom_vjp
def _d_nt(a, b):
    return _dg(a, b, 1, 1)


@jax.custom_vjp
def _d_tn(a, b):
    return _dg(a, b, 0, 0)


_d_nn.defvjp(lambda a, b: (_d_nn(a, b), (a, b)), lambda r, g: (_d_nt(g, r[1]), _d_tn(r[0], g)))
_d_nt.defvjp(lambda a, b: (_d_nt(a, b), (a, b)), lambda r, g: (_d_nn(g, r[1]), _d_tn(g, r[0])))
_d_tn.defvjp(lambda a, b: (_d_tn(a, b), (a, b)), lambda r, g: (_d_nt(r[1], g), _d_nn(r[0], g)))


def _log_sigmoid(x):
    return jnp.minimum(x, 0.0) - jnp.log1p(jnp.exp(-jnp.abs(x)))


def _rms(x, g):
    return x * lax.rsqrt(jnp.mean(x * x, axis=-1, keepdims=True) + EPS) * g


def _mm(a, b, mode, out_dtype, name, res=None, tm=None, tn=None, tk=None):
    assert a.dtype == BF16 and b.dtype == BF16, (name, a.dtype, b.dtype)
    if mode == "nn":
        (M, K), N = a.shape, b.shape[1]
    elif mode == "nt":
        (M, K), N = a.shape, b.shape[0]
    else:
        (K, M), N = a.shape, b.shape[1]
    tm = tm or _tile(M, 1408 if mode == "tn" else 640)
    tn = tn or _tile(N, 1408)
    tk = tk or _tile(K, 640 if mode == "tn" else 1408)
    nk = K // tk
    if mode == "tn":
        a_spec = pl.BlockSpec((tk, tm), lambda i, j, k: (k, i))
        dims = (((0,), (0,)), ((), ()))
    else:
        a_spec = pl.BlockSpec((tm, tk), lambda i, j, k: (i, k))
        dims = (((1,), (1 if mode == "nt" else 0,)), ((), ()))
    if mode == "nt":
        b_spec = pl.BlockSpec((tn, tk), lambda i, j, k: (j, k))
    else:
        b_spec = pl.BlockSpec((tk, tn), lambda i, j, k: (k, j))

    o_spec = pl.BlockSpec((tm, tn), lambda i, j, k: (i, j))

    def body(a_ref, b_ref, *rest):
        r_ref = rest[0] if res is not None else None
        o_ref, acc_ref = rest[-2:]
        k = pl.program_id(2)

        @pl.when(k == 0)
        def _():
            acc_ref[...] = jnp.zeros_like(acc_ref)

        acc_ref[...] += lax.dot_general(a_ref[...], b_ref[...], dims, preferred_element_type=F32)

        @pl.when(k == nk - 1)
        def _():
            out = acc_ref[...] if r_ref is None else acc_ref[...] + r_ref[...]
            o_ref[...] = out.astype(out_dtype)

    return pl.pallas_call(
        body, name=name, grid=(M // tm, N // tn, nk),
        in_specs=[a_spec, b_spec] + ([o_spec] if res is not None else []),
        out_specs=o_spec,
        out_shape=jax.ShapeDtypeStruct((M, N), out_dtype),
        scratch_shapes=[pltpu.VMEM((tm, tn), F32)],
        compiler_params=_params(("parallel", "parallel", "arbitrary")),
    )(a, b, *([res] if res is not None else []))


def _rms_fwd(x, g, T, name):
    Lp, D = x.shape

    def body(x_ref, g_ref, o_ref):
        o_ref[...] = _rms(x_ref[...], g_ref[...]).astype(BF16)

    return pl.pallas_call(
        body, name=name, grid=(Lp // T,),
        in_specs=[pl.BlockSpec((T, D), lambda i: (i, 0)), pl.BlockSpec((1, D), lambda i: (0, 0))],
        out_specs=pl.BlockSpec((T, D), lambda i: (i, 0)),
        out_shape=jax.ShapeDtypeStruct((Lp, D), BF16),
        compiler_params=_params(("parallel",)),
    )(x, g)


def _rms_bwd(x, g, dy, dres, T, name):
    Lp, D = x.shape

    def body(x_ref, g_ref, dy_ref, dr_ref, dx_ref, dxb_ref, dg_ref):
        @pl.when(pl.program_id(0) == 0)
        def _():
            dg_ref[...] = jnp.zeros_like(dg_ref)

        _, vjp = jax.vjp(_rms, x_ref[...], g_ref[...])
        dx, dg = vjp(dy_ref[...])
        dx = dx + dr_ref[...]
        dx_ref[...] = dx
        dxb_ref[...] = dx.astype(BF16)
        dg_ref[...] += dg

    row = pl.BlockSpec((T, D), lambda i: (i, 0))
    vec = pl.BlockSpec((1, D), lambda i: (0, 0))
    return pl.pallas_call(
        body, name=name, grid=(Lp // T,),
        in_specs=[row, vec, row, row],
        out_specs=[row, row, vec],
        out_shape=[jax.ShapeDtypeStruct((Lp, D), F32), jax.ShapeDtypeStruct((Lp, D), BF16),
                   jax.ShapeDtypeStruct((1, D), F32)],
        compiler_params=_params(("arbitrary",)),
    )(x, g, dy, dres)


def _swiglu(gate, up):
    return gate * jax.nn.sigmoid(gate) * up


def _swiglu_fwd(gu, name):
    Lp, F2 = gu.shape
    F = F2 // 2
    TR = 128

    def body(gu_ref, o_ref):
        o_ref[...] = _swiglu(gu_ref[:, :F], gu_ref[:, F:]).astype(BF16)

    return pl.pallas_call(
        body, name=name, grid=(Lp // TR,),
        in_specs=[pl.BlockSpec((TR, F2), lambda i: (i, 0))],
        out_specs=pl.BlockSpec((TR, F), lambda i: (i, 0)),
        out_shape=jax.ShapeDtypeStruct((Lp, F), BF16),
        compiler_params=_params(("parallel",)),
    )(gu)


def _swiglu_bwd(gu, dact, name):
    Lp, F2 = gu.shape
    F = F2 // 2
    TR = 128

    def body(gu_ref, da_ref, o_ref):
        _, vjp = jax.vjp(_swiglu, gu_ref[:, :F], gu_ref[:, F:])
        dg, du = vjp(da_ref[...])
        o_ref[:, :F] = dg.astype(BF16)
        o_ref[:, F:] = du.astype(BF16)

    return pl.pallas_call(
        body, name=name, grid=(Lp // TR,),
        in_specs=[pl.BlockSpec((TR, F2), lambda i: (i, 0)), pl.BlockSpec((TR, F), lambda i: (i, 0))],
        out_specs=pl.BlockSpec((TR, F2), lambda i: (i, 0)),
        out_shape=jax.ShapeDtypeStruct((Lp, F2), BF16),
        compiler_params=_params(("parallel",)),
    )(gu, dact)


def _final_loss(h, g, target, name):
    Lp, D = h.shape
    TR = HEAD_ROWS

    def loss_fn(hh, gg, tt):
        err = _rms(hh, gg) - tt
        return 0.5 * jnp.sum(jnp.mean(err * err, axis=-1))

    def body(h_ref, g_ref, t_ref, loss_ref, dh_ref, dhb_ref, dg_ref):
        i = pl.program_id(0)

        @pl.when(i == 0)
        def _():
            loss_ref[...] = jnp.zeros_like(loss_ref)
            dg_ref[...] = jnp.zeros_like(dg_ref)
            dh_ref[...] = jnp.zeros_like(dh_ref)
            dhb_ref[...] = jnp.zeros_like(dhb_ref)

        @pl.when(i > 0)
        def _():
            val, vjp = jax.vjp(lambda hh, gg: loss_fn(hh, gg, t_ref[...]), h_ref[...], g_ref[...])
            dh, dg = vjp(jnp.ones((), F32))
            dh_ref[...] = dh
            dhb_ref[...] = dh.astype(BF16)
            dg_ref[...] += dg
            loss_ref[...] += val

    row = pl.BlockSpec((TR, D), lambda i: (i, 0))
    return pl.pallas_call(
        body, name=name, grid=(Lp // TR,),
        in_specs=[row, pl.BlockSpec((1, D), lambda i: (0, 0)),
                  pl.BlockSpec((TR, D), lambda i: (jnp.maximum(i - 1, 0), 0))],
        out_specs=[pl.BlockSpec((8, LANES), lambda i: (0, 0)), row, row, pl.BlockSpec((1, D), lambda i: (0, 0))],
        out_shape=[jax.ShapeDtypeStruct((8, LANES), F32), jax.ShapeDtypeStruct((Lp, D), F32),
                   jax.ShapeDtypeStruct((Lp, D), BF16), jax.ShapeDtypeStruct((1, D), F32)],
        compiler_params=_params(("arbitrary",)),
    )(h, g, target)


def _lane_lo():
    return lax.broadcasted_iota(jnp.int32, (1, LANES), 1) < FOX_DH


def _headnorm(x, g, scale):
    lo = _lane_lo()
    x2 = x * x
    s0 = jnp.sum(jnp.where(lo, x2, 0.0), axis=-1, keepdims=True)
    s1 = jnp.sum(jnp.where(lo, 0.0, x2), axis=-1, keepdims=True)
    r = jnp.where(lo, lax.rsqrt(s0 / FOX_DH + EPS), lax.rsqrt(s1 / FOX_DH + EPS))
    return x * r * g * scale


def _fox_prep_fwd(proj, bf, qg, kg, T, D, name):
    Lp = proj.shape[0]
    nb = D // LANES
    scale = FOX_DH ** -0.5

    def body(q_ref, k_ref, v_ref, fl_ref, bf_ref, qg_ref, kg_ref, qn_ref, kn_ref, vb_ref, c_ref, carry_ref):
        @pl.when(pl.program_id(0) == 0)
        def _():
            carry_ref[...] = jnp.zeros_like(carry_ref)

        for b in range(nb):
            sl = slice(b * LANES, (b + 1) * LANES)
            qn_ref[:, sl] = _headnorm(q_ref[:, sl], qg_ref[...], scale).astype(BF16)
            kn_ref[:, sl] = _headnorm(k_ref[:, sl], kg_ref[...], 1.0).astype(BF16)
        vb_ref[...] = v_ref[...].astype(BF16)
        log_f = _log_sigmoid(fl_ref[...] + bf_ref[...])
        row = lax.broadcasted_iota(jnp.int32, (T, T), 0)
        col = lax.broadcasted_iota(jnp.int32, (T, T), 1)
        tri = (col <= row).astype(F32)
        c = jnp.dot(tri, log_f, precision=HI, preferred_element_type=F32) + carry_ref[...]
        c_ref[...] = c
        last = lax.broadcasted_iota(jnp.int32, (T, 1), 0) == T - 1
        carry_ref[...] = jnp.sum(jnp.where(last, c, 0.0), axis=0, keepdims=True)

    wide = lambda j: pl.BlockSpec((T, D), lambda i: (i, j))
    vec = pl.BlockSpec((1, LANES), lambda i: (0, 0))
    return pl.pallas_call(
        body, name=name, grid=(Lp // T,),
        in_specs=[wide(0), wide(1), wide(2), pl.BlockSpec((T, LANES), lambda i: (i, 4 * nb)), vec, vec, vec],
        out_specs=[wide(0), wide(0), wide(0), pl.BlockSpec((T, LANES), lambda i: (i, 0))],
        out_shape=[jax.ShapeDtypeStruct((Lp, D), BF16)] * 3 + [jax.ShapeDtypeStruct((Lp, LANES), F32)],
        scratch_shapes=[pltpu.VMEM((1, LANES), F32)],
        compiler_params=_params(("arbitrary",)),
    )(proj, proj, proj, proj, bf, qg, kg)


def _fox_prep_bwd(proj, bf, qg, kg, dqn, dkn, dc, T, D, name):
    Lp = proj.shape[0]
    nb = D // LANES
    nt = Lp // T
    scale = FOX_DH ** -0.5

    def body(q_ref, k_ref, fl_ref, bf_ref, qg_ref, kg_ref, dqn_ref, dkn_ref, dc_ref,
             dq_ref, dk_ref, dfl_ref, sm_ref, carry_ref):
        @pl.when(pl.program_id(0) == 0)
        def _():
            carry_ref[...] = jnp.zeros_like(carry_ref)
            sm_ref[...] = jnp.zeros_like(sm_ref)

        dqg = jnp.zeros((1, LANES), F32)
        dkg = jnp.zeros((1, LANES), F32)
        for b in range(nb):
            sl = slice(b * LANES, (b + 1) * LANES)
            _, vjp = jax.vjp(lambda x, g: _headnorm(x, g, scale), q_ref[:, sl], qg_ref[...])
            dx, dg = vjp(dqn_ref[:, sl])
            dq_ref[:, sl] = dx.astype(BF16)
            dqg = dqg + dg
            _, vjp = jax.vjp(lambda x, g: _headnorm(x, g, 1.0), k_ref[:, sl], kg_ref[...])
            dx, dg = vjp(dkn_ref[:, sl])
            dk_ref[:, sl] = dx.astype(BF16)
            dkg = dkg + dg
        dcv = dc_ref[...]
        row = lax.broadcasted_iota(jnp.int32, (T, T), 0)
        col = lax.broadcasted_iota(jnp.int32, (T, T), 1)
        triu = (col >= row).astype(F32)
        dlogf = jnp.dot(triu, dcv, precision=HI, preferred_element_type=F32) + carry_ref[...]
        carry_ref[...] += jnp.sum(dcv, axis=0, keepdims=True)
        _, vjp = jax.vjp(_log_sigmoid, fl_ref[...] + bf_ref[...])
        (dfl,) = vjp(dlogf)
        dfl_ref[...] = dfl.astype(BF16)
        sm_ref[0:1, :] += jnp.sum(dfl, axis=0, keepdims=True)
        sm_ref[1:2, :] += dqg
        sm_ref[2:3, :] += dkg

    wide = lambda j: pl.BlockSpec((T, D), lambda i: (nt - 1 - i, j))
    narrow = lambda j: pl.BlockSpec((T, LANES), lambda i: (nt - 1 - i, j))
    vec = pl.BlockSpec((1, LANES), lambda i: (0, 0))
    return pl.pallas_call(
        body, name=name, grid=(nt,),
        in_specs=[wide(0), wide(1), narrow(4 * nb), vec, vec, vec, wide(0), wide(0), narrow(0)],
        out_specs=[wide(0), wide(0), narrow(0), pl.BlockSpec((8, LANES), lambda i: (0, 0))],
        out_shape=[jax.ShapeDtypeStruct((Lp, D), BF16)] * 2 + [jax.ShapeDtypeStruct((Lp, LANES), BF16),
                                                                 jax.ShapeDtypeStruct((8, LANES), F32)],
        scratch_shapes=[pltpu.VMEM((1, LANES), F32)],
        compiler_params=_params(("arbitrary",)),
    )(proj, proj, proj, bf, qg, kg, dqn, dkn, dc)


def _ln2_ceil(m):
    return jnp.ceil(m * (1.0 / math.log(2.0))) * math.log(2.0)


def _fox_mask(i, k0, T):
    qpos = i * T + lax.broadcasted_iota(jnp.int32, (T, 1), 0)
    kpos = k0 + lax.broadcasted_iota(jnp.int32, (1, T), 1)
    return (kpos <= qpos) & ((kpos >= N_PAD) | (qpos < N_PAD))


def _pick_col(blk, idx):
    lane = lax.broadcasted_iota(jnp.int32, (1, LANES), 1)
    return jnp.sum(jnp.where(lane == idx, blk, 0.0), axis=1, keepdims=True)


def _split_halves(blk):
    lo = _lane_lo()
    return (jnp.max(jnp.where(lo, blk, -jnp.inf), axis=1, keepdims=True),
            jnp.max(jnp.where(lo, -jnp.inf, blk), axis=1, keepdims=True))


def _fox_attn_fwd(qn, kn, vb, c, cT, proj, T, D, name):
    Lp = qn.shape[0]
    P = D // LANES
    nt = Lp // T
    H = cT.shape[0]

    def body(q_ref, k_ref, v_ref, c_ref, cT_ref, g_ref, o_ref, og_ref, m_ref, li_ref):
        p = pl.program_id(0)
        i = pl.program_id(1)
        lo = _lane_lo()
        q = q_ref[...]
        zero = jnp.zeros_like(q)
        qh = (jnp.where(lo, q, zero), jnp.where(lo, zero, q))
        cblk = c_ref[...]
        cq = tuple(_pick_col(cblk, 2 * p + h) for h in (0, 1))
        one = jnp.ones_like(q)

        def step(j, carry, masked):
            k0 = pl.multiple_of(j * T, LANES)
            kj = k_ref[pl.ds(k0, T), :]
            vj = v_ref[pl.ds(k0, T), :]
            vh = (jnp.where(lo, vj, one), jnp.where(lo, one, vj))
            mask = _fox_mask(i, k0, T) if masked else None
            out = []
            for h in (0, 1):
                m, acc = carry[h]
                ck = cT_ref[pl.ds(2 * p + h, 1), pl.ds(k0, T)]
                t = lax.dot_general(qh[h], kj, (((1,), (1,)), ((), ())), preferred_element_type=F32) - ck
                if masked:
                    t = jnp.where(mask, t, NEG)
                m_new = _ln2_ceil(jnp.maximum(m, cq[h] + jnp.max(t, axis=1, keepdims=True)))
                pr = jnp.exp(t + (cq[h] - m_new)).astype(BF16)
                acc = jnp.exp(m - m_new) * acc + jnp.dot(pr, vh[h], preferred_element_type=F32)
                out.append((m_new, acc))
            return tuple(out)

        init = tuple((jnp.full((T, 1), NEG, F32), jnp.zeros((T, LANES), F32)) for _ in (0, 1))
        carry = step(0, init, True)
        carry = lax.fori_loop(1, i, lambda j, cr: step(j, cr, False), carry)
        (m0, a0), (m1, a1) = lax.cond(i > 0, lambda cr: step(i, cr, True), lambda cr: cr, carry)
        l0 = pltpu.roll(a0, FOX_DH, 1)
        l1 = pltpu.roll(a1, FOX_DH, 1)
        o = jnp.where(lo, a0 / l0, a1 / l1)
        o_ref[...] = o
        m_ref[...] = jnp.where(lo, m0, m1)
        li_ref[...] = jnp.where(lo, 1.0 / l0, 1.0 / l1)
        og_ref[...] = (o * jax.nn.sigmoid(g_ref[...])).astype(BF16)

    tile = pl.BlockSpec((T, LANES), lambda p, i: (i, p))
    full = pl.BlockSpec((Lp, LANES), lambda p, i: (0, p))
    return pl.pallas_call(
        body, name=name, grid=(P, nt),
        in_specs=[tile, full, full, pl.BlockSpec((T, LANES), lambda p, i: (i, 0)),
                  pl.BlockSpec((H, Lp), lambda p, i: (0, 0)),
                  pl.BlockSpec((T, LANES), lambda p, i: (i, 3 * P + p))],
        out_specs=[tile, tile, tile, tile],
        out_shape=[jax.ShapeDtypeStruct((Lp, D), F32), jax.ShapeDtypeStruct((Lp, D), BF16),
                   jax.ShapeDtypeStruct((Lp, D), F32), jax.ShapeDtypeStruct((Lp, D), F32)],
        compiler_params=_params(("parallel", "arbitrary")),
    )(qn, kn, vb, c, cT, proj)


def _fox_gate_bwd(dog, o, proj, linv, T, D, name):
    Lp = o.shape[0]
    P = D // LANES

    def body(dog_ref, o_ref, g_ref, li_ref, do_ref, dg_ref, dl_ref):
        lo = _lane_lo()
        sig = jax.nn.sigmoid(g_ref[...])
        ov = o_ref[...]
        do = (dog_ref[...] * sig * li_ref[...]).astype(BF16)
        do_ref[...] = do
        dg_ref[...] = (dog_ref[...] * ov * sig * (1.0 - sig)).astype(BF16)
        t = do.astype(F32) * ov
        d0 = jnp.sum(jnp.where(lo, t, 0.0), axis=1, keepdims=True)
        d1 = jnp.sum(jnp.where(lo, 0.0, t), axis=1, keepdims=True)
        dl_ref[...] = jnp.where(lo, d0, d1)

    tile = pl.BlockSpec((T, LANES), lambda i, p: (i, p))
    return pl.pallas_call(
        body, name=name, grid=(Lp // T, P),
        in_specs=[tile, tile, pl.BlockSpec((T, LANES), lambda i, p: (i, 3 * P + p)), tile],
        out_specs=[tile, tile, tile],
        out_shape=[jax.ShapeDtypeStruct((Lp, D), BF16), jax.ShapeDtypeStruct((Lp, D), BF16),
                   jax.ShapeDtypeStruct((Lp, D), F32)],
        compiler_params=_params(("parallel", "parallel")),
    )(dog, o, proj, linv)


def _fox_attn_bwd(qn, kn, vb, c, cT, do, mshift, delta, T, D, name):
    Lp = qn.shape[0]
    P = D // LANES
    nt = Lp // T
    H = cT.shape[0]

    def body(q_ref, do_ref, m_ref, dl_ref, c_ref, k_ref, v_ref, cT_ref, dq_ref, dk_ref, dv_ref, dc_ref):
        p = pl.program_id(0)
        i = pl.program_id(1)

        @pl.when(i == 0)
        def _():
            dk_ref[...] = jnp.zeros_like(dk_ref)
            dv_ref[...] = jnp.zeros_like(dv_ref)
            dc_ref[...] = jnp.zeros_like(dc_ref)

        lo = _lane_lo()
        q = q_ref[...]
        do = do_ref[...]
        zero = jnp.zeros_like(q)
        qh = (jnp.where(lo, q, zero), jnp.where(lo, zero, q))
        doh = (jnp.where(lo, do, zero), jnp.where(lo, zero, do))
        msh = _split_halves(m_ref[...])
        dlt = _split_halves(dl_ref[...])
        cblk = c_ref[...]
        shift = tuple(_pick_col(cblk, 2 * p + h) - msh[h] for h in (0, 1))

        def step(j, carry, masked):
            k0 = pl.multiple_of(j * T, LANES)
            kj = k_ref[pl.ds(k0, T), :]
            vj = v_ref[pl.ds(k0, T), :]
            mask = _fox_mask(i, k0, T) if masked else None
            dqs, dks, dvs = [], [], []
            for h in (0, 1):
                ck = cT_ref[pl.ds(2 * p + h, 1), pl.ds(k0, T)]
                t = lax.dot_general(qh[h], kj, (((1,), (1,)), ((), ())), preferred_element_type=F32) - ck
                if masked:
                    t = jnp.where(mask, t, NEG)
                pb = jnp.exp(t + shift[h]).astype(BF16)
                dp = lax.dot_general(doh[h], vj, (((1,), (1,)), ((), ())), preferred_element_type=F32)
                ds = pb.astype(F32) * (dp - dlt[h])
                dsb = ds.astype(BF16)
                dqs.append(carry[h] + jnp.dot(dsb, kj, preferred_element_type=F32))
                dks.append(lax.dot_general(dsb, q, (((0,), (0,)), ((), ())), preferred_element_type=F32))
                dvs.append(lax.dot_general(pb, do, (((0,), (0,)), ((), ())), preferred_element_type=F32))
                dc_ref[0, h:h + 1, pl.ds(k0, T)] += -jnp.sum(ds, axis=0, keepdims=True)
            dk_ref[pl.ds(k0, T), :] += jnp.where(lo, dks[0], dks[1])
            dv_ref[pl.ds(k0, T), :] += jnp.where(lo, dvs[0], dvs[1])
            return tuple(dqs)

        init = (jnp.zeros((T, LANES), F32), jnp.zeros((T, LANES), F32))
        carry = step(0, init, True)
        carry = lax.fori_loop(1, i, lambda j, cr: step(j, cr, False), carry)
        dq0, dq1 = lax.cond(i > 0, lambda cr: step(i, cr, True), lambda cr: cr, carry)
        dq_ref[...] = jnp.where(lo, dq0, dq1)

    tile = pl.BlockSpec((T, LANES), lambda p, i: (i, p))
    full = pl.BlockSpec((Lp, LANES), lambda p, i: (0, p))
    return pl.pallas_call(
        body, name=name, grid=(P, nt),
        in_specs=[tile, tile, tile, tile, pl.BlockSpec((T, LANES), lambda p, i: (i, 0)), full, full,
                  pl.BlockSpec((H, Lp), lambda p, i: (0, 0))],
        out_specs=[tile, full, full, pl.BlockSpec((1, 8, Lp), lambda p, i: (p, 0, 0))],
        out_shape=[jax.ShapeDtypeStruct((Lp, D), F32)] * 3 + [jax.ShapeDtypeStruct((P, 8, Lp), F32)],
        compiler_params=_params(("parallel", "arbitrary")),
    )(qn, do, mshift, delta, c, kn, vb, cT)


def _hgrn_chunk(St, qr, z, vi, go, p0, p1, gg):
    C = qr.shape[0]
    lb = jax.nn.sigmoid(p1 - p0)
    a = jnp.log(lb)
    cc = jnp.log1p(-lb) + _log_sigmoid(z)
    log_f = jnp.maximum(a, cc) + jnp.log1p(jnp.exp(-jnp.abs(a - cc)))
    k = (1.0 - lb) * jax.nn.sigmoid(-z)
    q = qr * jax.nn.sigmoid(qr)
    row = lax.broadcasted_iota(jnp.int32, (C, C), 0)
    col = lax.broadcasted_iota(jnp.int32, (C, C), 1)
    causal = col <= row
    b = jnp.dot(causal.astype(F32), log_f, precision=HI, preferred_element_type=F32)
    mid = lax.broadcasted_iota(jnp.int32, (C, 1), 0) == C // 2 - 1
    r = jnp.sum(jnp.where(mid, b, 0.0), axis=0, keepdims=True)
    b_last = jnp.sum(log_f, axis=0, keepdims=True)
    attn = jnp.where(causal, _d_nt(q * jnp.exp(b - r), k * jnp.exp(r - b)), 0.0)
    o = _d_nn(attn, vi) + _d_nt(q * jnp.exp(b), St)
    St_new = St * jnp.exp(b_last) + _d_tn(vi, k * jnp.exp(b_last - b))
    og = _rms(o, gg) * (go * jax.nn.sigmoid(go))
    return St_new, og


def _hgrn_heads_per_step(H):
    return 4 if H % 4 == 0 else 1


def _hgrn_specs(T, W, nhb, rev_nt=None):
    if rev_nt is None:
        return [pl.BlockSpec((T, W), functools.partial(lambda hb, t, g: (t, g * nhb + hb), g=g)) for g in range(4)]
    return [pl.BlockSpec((T, W), functools.partial(lambda hb, t, g: (rev_nt - 1 - t, g * nhb + hb), g=g))
            for g in range(4)]


def _hgrn_fwd(proj, lbp, gg, T, name):
    Lp = proj.shape[0]
    D = proj.shape[1] // 4
    H = D // LANES
    hps = _hgrn_heads_per_step(H)
    W = hps * LANES
    nhb = H // hps
    nt = Lp // T
    ncc = T // HGRN_CHUNK

    def body(q_ref, z_ref, i_ref, go_ref, p_ref, gg_ref, og_ref, ss_ref, st_ref):
        @pl.when(pl.program_id(1) == 0)
        def _():
            st_ref[...] = jnp.zeros_like(st_ref)

        gain = gg_ref[...]

        def chunk(cidx, carry):
            sl = pl.ds(pl.multiple_of(cidx * HGRN_CHUNK, HGRN_CHUNK), HGRN_CHUNK)
            for hh in range(hps):
                ln = slice(hh * LANES, (hh + 1) * LANES)
                St = st_ref[hh]
                ss_ref[hh, cidx] = St
                St_new, og = _hgrn_chunk(St, q_ref[sl, ln], z_ref[sl, ln], i_ref[sl, ln], go_ref[sl, ln],
                                         p_ref[0:1, ln], p_ref[1:2, ln], gain)
                st_ref[hh] = St_new
                og_ref[sl, ln] = og.astype(BF16)
            return carry

        lax.fori_loop(0, ncc, chunk, 0)

    return pl.pallas_call(
        body, name=name, grid=(nhb, nt),
        in_specs=_hgrn_specs(T, W, nhb) + [pl.BlockSpec((2, W), lambda hb, t: (0, hb)),
                                           pl.BlockSpec((1, LANES), lambda hb, t: (0, 0))],
        out_specs=[pl.BlockSpec((T, W), lambda hb, t: (t, hb)),
                   pl.BlockSpec((hps, ncc, LANES, LANES), lambda hb, t: (hb, t, 0, 0))],
        out_shape=[jax.ShapeDtypeStruct((Lp, D), BF16),
                   jax.ShapeDtypeStruct((H, Lp // HGRN_CHUNK, LANES, LANES), F32)],
        scratch_shapes=[pltpu.VMEM((hps, LANES, LANES), F32)],
        compiler_params=_params(("parallel", "arbitrary")),
    )(proj, proj, proj, proj, lbp, gg)


def _hgrn_bwd(proj, lbp, gg, dog, ss, T, name):
    Lp = proj.shape[0]
    D = proj.shape[1] // 4
    H = D // LANES
    hps = _hgrn_heads_per_step(H)
    W = hps * LANES
    nhb = H // hps
    nt = Lp // T
    ncc = T // HGRN_CHUNK

    def body(q_ref, z_ref, i_ref, go_ref, p_ref, gg_ref, dog_ref, ss_ref,
             dq_ref, dz_ref, di_ref, dgo_ref, dp_ref, dgg_ref, dst_ref):
        hb = pl.program_id(0)
        t = pl.program_id(1)

        @pl.when(t == 0)
        def _():
            dst_ref[...] = jnp.zeros_like(dst_ref)
            dp_ref[...] = jnp.zeros_like(dp_ref)

        @pl.when((t == 0) & (hb == 0))
        def _():
            dgg_ref[...] = jnp.zeros_like(dgg_ref)

        gain = gg_ref[...]
        row0 = (nt - 1 - t) * T

        def chunk(cc, carry):
            cidx = ncc - 1 - cc
            r0 = pl.multiple_of(cidx * HGRN_CHUNK, HGRN_CHUNK)
            sl = pl.ds(r0, HGRN_CHUNK)
            real = (row0 + r0 + lax.broadcasted_iota(jnp.int32, (HGRN_CHUNK, 1), 0)) >= N_PAD
            for hh in range(hps):
                ln = slice(hh * LANES, (hh + 1) * LANES)
                _, vjp = jax.vjp(_hgrn_chunk, ss_ref[hh, cidx], q_ref[sl, ln], z_ref[sl, ln], i_ref[sl, ln],
                                 go_ref[sl, ln], p_ref[0:1, ln], p_ref[1:2, ln], gain)
                dSt, dq, dz, di, dgo, dp0, dp1, dgain = vjp((dst_ref[hh], dog_ref[sl, ln]))
                dq_ref[sl, ln] = jnp.where(real, dq, 0.0).astype(BF16)
                dz_ref[sl, ln] = jnp.where(real, dz, 0.0).astype(BF16)
                di_ref[sl, ln] = jnp.where(real, di, 0.0).astype(BF16)
                dgo_ref[sl, ln] = jnp.where(real, dgo, 0.0).astype(BF16)
                dst_ref[hh] = dSt
                dp_ref[0:1, ln] += dp0
                dp_ref[1:2, ln] += dp1
                dgg_ref[0:1, :] += dgain
            return carry

        lax.fori_loop(0, ncc, chunk, 0)

    rev = pl.BlockSpec((T, W), lambda hb, t: (nt - 1 - t, hb))
    return pl.pallas_call(
        body, name=name, grid=(nhb, nt),
        in_specs=_hgrn_specs(T, W, nhb, nt) + [pl.BlockSpec((2, W), lambda hb, t: (0, hb)),
                                               pl.BlockSpec((1, LANES), lambda hb, t: (0, 0)), rev,
                                               pl.BlockSpec((hps, ncc, LANES, LANES),
                                                            lambda hb, t: (hb, nt - 1 - t, 0, 0))],
        out_specs=[rev, rev, rev, rev, pl.BlockSpec((8, W), lambda hb, t: (0, hb)),
                   pl.BlockSpec((8, LANES), lambda hb, t: (0, 0))],
        out_shape=[jax.ShapeDtypeStruct((Lp, D), BF16)] * 4 + [jax.ShapeDtypeStruct((8, D), F32),
                                                                 jax.ShapeDtypeStruct((8, LANES), F32)],
        scratch_shapes=[pltpu.VMEM((hps, LANES, LANES), F32)],
        compiler_params=_params(("arbitrary", "arbitrary")),
    )(proj, proj, proj, proj, lbp, gg, dog, ss)


def _exchange(arrays, per_peer, name):
    n_arr = len(arrays)
    HBM = pl.BlockSpec(memory_space=pltpu.HBM)

    def body(*refs):
        ins = refs[:n_arr]
        outs = refs[n_arr:2 * n_arr]
        send_sems, recv_sems, local_sems = refs[2 * n_arr:]
        x, y, c = lax.axis_index("x"), lax.axis_index("y"), lax.axis_index("c")
        me = 4 * x + 2 * y + c
        local, remote = [], []
        for n in range(n_arr):
            src = ins[n].at[me] if per_peer[n] else ins[n]
            cp = pltpu.make_async_copy(src, outs[n].at[me], local_sems.at[n])
            cp.start()
            local.append(cp)
        for rel in range(1, N_DEV):
            fx, fy, fc = (rel >> 2) & 1, (rel >> 1) & 1, rel & 1
            px = 1 - x if fx else x
            py = 1 - y if fy else y
            pc = 1 - c if fc else c
            peer = 4 * px + 2 * py + pc
            for n in range(n_arr):
                src = ins[n].at[peer] if per_peer[n] else ins[n]
                cp = pltpu.make_async_remote_copy(
                    src_ref=src, dst_ref=outs[n].at[me],
                    send_sem=send_sems.at[n * (N_DEV - 1) + rel - 1],
                    recv_sem=recv_sems.at[n * (N_DEV - 1) + rel - 1],
                    device_id=(px, py, pc), device_id_type=pl.DeviceIdType.MESH)
                cp.start()
                remote.append(cp)
        for cp in remote:
            cp.wait()
        for cp in local:
            cp.wait()

    out_shape = [jax.ShapeDtypeStruct((N_DEV,) + a.shape[-2:], a.dtype) for a in arrays]
    return pl.pallas_call(
        body, name=name,
        in_specs=[HBM] * n_arr, out_specs=[HBM] * n_arr, out_shape=out_shape,
        scratch_shapes=[pltpu.SemaphoreType.DMA((n_arr * (N_DEV - 1),)),
                        pltpu.SemaphoreType.DMA((n_arr * (N_DEV - 1),)),
                        pltpu.SemaphoreType.DMA((n_arr,))],
    )(*arrays)


def _adamw(recv, w, m, v, name):
    R = w.shape[0]
    rc = _row_chunk(R, 2400, 16 if recv.dtype == BF16 else 8)

    def body(r_ref, w_ref, m_ref, v_ref, g_ref, d_ref, mo_ref, vo_ref):
        g = r_ref[0].astype(F32)
        for s in range(1, N_DEV):
            g = g + r_ref[s].astype(F32)
        mn = ADAM_B1 * m_ref[...] + (1.0 - ADAM_B1) * g
        vn = ADAM_B2 * v_ref[...] + (1.0 - ADAM_B2) * (g * g)
        m_hat = mn / (1.0 - ADAM_B1 ** ADAM_STEP)
        v_hat = vn / (1.0 - ADAM_B2 ** ADAM_STEP)
        g_ref[...] = g
        d_ref[...] = -ADAM_LR * (m_hat / (jnp.sqrt(v_hat) + ADAM_EPS) + ADAM_WD * w_ref[...])
        mo_ref[...] = mn
        vo_ref[...] = vn

    row = pl.BlockSpec((rc, LANES), lambda i: (i, 0))
    return pl.pallas_call(
        body, name=name, grid=(R // rc,),
        in_specs=[pl.BlockSpec((N_DEV, rc, LANES), lambda i: (0, i, 0)), row, row, row],
        out_specs=[row] * 4,
        out_shape=[jax.ShapeDtypeStruct((R, LANES), F32)] * 4,
        compiler_params=_params(("parallel",)),
    )(recv, w, m, v)


PACK_ALIGN = 512


def _pad_rows(p):
    pad = (-p.shape[-2]) % PACK_ALIGN
    return jnp.pad(p, [(0, 0)] * (p.ndim - 2) + [(0, pad), (0, 0)])


def _pack_rows(arrs, dtype):
    return _pad_rows(jnp.concatenate([a.astype(dtype).reshape(-1, LANES) for a in arrs], axis=0))


def _unpack_rows(packed, shapes):
    out, off = [], 0
    for shp in shapes:
        n = math.prod(shp) // LANES
        out.append(packed[..., off:off + n, :].reshape(packed.shape[:-2] + tuple(shp)))
        off += n
    return out


def _gathered_to_full(g, name):
    if name in COL_SHARDED:
        g = jnp.moveaxis(g, 0, -2)
        return g.reshape(g.shape[:-2] + (g.shape[-2] * g.shape[-1],))
    g = jnp.moveaxis(g, 0, -3)
    return g.reshape(g.shape[:-3] + (g.shape[-3] * g.shape[-2], g.shape[-1]))


def _full_to_slabs(full, name):
    if name in COL_SHARDED:
        f = full.reshape(full.shape[:-1] + (N_DEV, full.shape[-1] // N_DEV))
        return jnp.moveaxis(f, -2, 0)
    f = full.reshape(full.shape[:-2] + (N_DEV, full.shape[-2] // N_DEV, full.shape[-1]))
    return jnp.moveaxis(f, -3, 0)


def _pack_small(arrs):
    rows = []
    for a in arrs:
        flat = a.astype(F32).reshape(-1)
        pad = (-flat.shape[0]) % LANES
        rows.append(jnp.pad(flat, (0, pad)).reshape(-1, LANES))
    p = jnp.concatenate(rows, axis=0)
    return jnp.pad(p, ((0, (-p.shape[0]) % 8), (0, 0)))


def _unpack_small(packed, shapes):
    out, off = [], 0
    for shp in shapes:
        n = math.prod(shp)
        nr = -(-n // LANES)
        out.append(packed[off:off + nr].reshape(-1)[:n].reshape(shp))
        off += nr
    return out


def _local_step(x, target, meta, wb, small):
    S, D = x.shape
    Lp = S + HEAD_ROWS
    T = 640 if Lp % 640 == 0 else 128
    P = D // LANES
    row = lambda v: v.reshape(1, -1).astype(F32)

    w_fin = jnp.pad(wb["fox_w_in"][0], ((0, 0), (0, LANES - wb["fox_w_in"].shape[-1] % LANES)))
    w_fout, w_hin, w_hout = wb["fox_w_out"][0], wb["hgrn_w_in"][0], wb["hgrn_w_out"][0]
    w_uin, w_uout = wb["ffn_w_in"], wb["ffn_w_out"]
    n_heads = wb["fox_w_in"].shape[-1] - 4 * D
    bf = jnp.pad(row(small["fox_b_f"]), ((0, 0), (0, LANES - small["fox_b_f"].size)))
    qg = jnp.tile(row(small["fox_q_norm"]), (1, 2))
    kg = jnp.tile(row(small["fox_k_norm"]), (1, 2))

    h0 = jnp.concatenate([jnp.zeros((N_PAD, D), F32), meta, x], axis=0)

    hn0 = _rms_fwd(h0, row(small["attn_norm"][0]), T, "rms0_fwd")
    proj0 = _mm(hn0, w_fin, "nn", F32, "fox_in_fwd")
    qn, kn, vb, c = _fox_prep_fwd(proj0, bf, qg, kg, T, D, "fox_prep_fwd")
    cT = c.T[:2 * P]
    o, og0, mshift, linv = _fox_attn_fwd(qn, kn, vb, c, cT, proj0, T, D, "fox_attn_fwd")
    h1 = _mm(og0, w_fout, "nn", F32, "fox_out_fwd", res=h0)
    hf0 = _rms_fwd(h1, row(small["ffn_norm"][0]), T, "rmsf0_fwd")
    gu0 = _mm(hf0, w_uin[0], "nn", F32, "ffn0_in_fwd")
    act0 = _swiglu_fwd(gu0, "swiglu0_fwd")
    h2 = _mm(act0, w_uout[0], "nn", F32, "ffn0_out_fwd", res=h1)
    hn1 = _rms_fwd(h2, row(small["attn_norm"][1]), T, "rms1_fwd")
    proj1 = _mm(hn1, w_hin, "nn", F32, "hgrn_in_fwd")
    lbp = small["hgrn_lower_bounds"].astype(F32)
    ggn = row(small["hgrn_g_norm"])
    og1, ss = _hgrn_fwd(proj1, lbp, ggn, T, "hgrn_fwd")
    h3 = _mm(og1, w_hout, "nn", F32, "hgrn_out_fwd", res=h2)
    hf1 = _rms_fwd(h3, row(small["ffn_norm"][1]), T, "rmsf1_fwd")
    gu1 = _mm(hf1, w_uin[1], "nn", F32, "ffn1_in_fwd")
    act1 = _swiglu_fwd(gu1, "swiglu1_fwd")
    h4 = _mm(act1, w_uout[1], "nn", F32, "ffn1_out_fwd", res=h3)
    loss_blk, dh4, dh4b, d_final = _final_loss(h4, row(small["final_norm"]), target, "final_loss")

    grads = {}

    def ffn_bwd(i, dh, dhb, h_in, hf, gu, act, tag):
        grads_out = _mm(act, dhb, "tn", F32, f"ffn{i}_out_dw")
        dact = _mm(dhb, w_uout[i], "nt", F32, f"ffn{i}_out_dx")
        dgu = _swiglu_bwd(gu, dact, f"swiglu{i}_bwd")
        grads_in = _mm(hf, dgu, "tn", F32, f"ffn{i}_in_dw")
        dhf = _mm(dgu, w_uin[i], "nt", F32, f"ffn{i}_in_dx")
        dh_new, dh_newb, dgain = _rms_bwd(h_in, row(small["ffn_norm"][i]), dhf, dh, T, f"rmsf{i}_bwd")
        return dh_new, dh_newb, grads_in, grads_out, dgain

    dh3, dh3b, g_uin1, g_uout1, d_fn1 = ffn_bwd(1, dh4, dh4b, h3, hf1, gu1, act1, "1")
    grads["hgrn_w_out"] = _mm(og1, dh3b, "tn", F32, "hgrn_out_dw")[None]
    dog1 = _mm(dh3b, w_hout, "nt", F32, "hgrn_out_dx")
    dq1, dz1, di1, dgo1, d_lb, d_gg = _hgrn_bwd(proj1, lbp, ggn, dog1, ss, T, "hgrn_bwd")
    dproj1 = jnp.concatenate([dq1, dz1, di1, dgo1], axis=1)
    grads["hgrn_w_in"] = _mm(hn1, dproj1, "tn", F32, "hgrn_in_dw")[None]
    dhn1 = _mm(dproj1, w_hin, "nt", F32, "hgrn_in_dx")
    dh2, dh2b, d_an1 = _rms_bwd(h2, row(small["attn_norm"][1]), dhn1, dh3, T, "rms1_bwd")
    dh1, dh1b, g_uin0, g_uout0, d_fn0 = ffn_bwd(0, dh2, dh2b, h1, hf0, gu0, act0, "0")
    grads["ffn_w_in"] = jnp.stack([g_uin0, g_uin1])
    grads["ffn_w_out"] = jnp.stack([g_uout0, g_uout1])
    grads["fox_w_out"] = _mm(og0, dh1b, "tn", F32, "fox_out_dw")[None]
    dog0 = _mm(dh1b, w_fout, "nt", F32, "fox_out_dx")
    do, dgate, delta = _fox_gate_bwd(dog0, o, proj0, linv, T, D, "fox_gate_bwd")
    dqn, dkn, dv, dcr = _fox_attn_bwd(qn, kn, vb, c, cT, do, mshift, delta, T, D, "fox_attn_bwd")
    dc = jnp.pad(dcr[:, :2, :].reshape(2 * P, Lp).T, ((0, 0), (0, LANES - 2 * P)))
    dq0, dk0, dfl, sm = _fox_prep_bwd(proj0, bf, qg, kg, dqn, dkn, dc, T, D, "fox_prep_bwd")
    dproj0 = jnp.concatenate([dq0, dk0, dv.astype(BF16), dgate, dfl], axis=1)
    grads["fox_w_in"] = _mm(hn0, dproj0, "tn", F32, "fox_in_dw")[:, :4 * D + n_heads][None]
    dhn0 = _mm(dproj0, w_fin, "nt", F32, "fox_in_dx")
    dh0, _, d_an0 = _rms_bwd(h0, row(small["attn_norm"][0]), dhn0, dh1, T, "rms0_bwd")

    grads["meta_tokens"] = dh0[N_PAD:HEAD_ROWS]
    grads["attn_norm"] = jnp.concatenate([d_an0, d_an1], axis=0)
    grads["ffn_norm"] = jnp.concatenate([d_fn0, d_fn1], axis=0)
    grads["final_norm"] = d_final[0]
    grads["fox_b_f"] = sm[0:1, :n_heads]
    grads["fox_q_norm"] = sm[1:2, :FOX_DH] + sm[1:2, FOX_DH:]
    grads["fox_k_norm"] = sm[2:3, :FOX_DH] + sm[2:3, FOX_DH:]
    grads["hgrn_lower_bounds"] = d_lb[0:2]
    grads["hgrn_g_norm"] = d_gg[0:1]
    return loss_blk[0, 0], dh0[HEAD_ROWS:], grads


def kernel(x, meta_tokens, attn_norm, ffn_norm, final_norm, fox_w_in, fox_b_f, fox_q_norm, fox_k_norm, fox_w_out, hgrn_w_in, hgrn_lower_bounds, hgrn_g_norm, hgrn_w_out, ffn_w_in, ffn_w_out, loss_target, m_meta_tokens, m_attn_norm, m_ffn_norm, m_final_norm, m_fox_w_in, m_fox_b_f, m_fox_q_norm, m_fox_k_norm, m_fox_w_out, m_hgrn_w_in, m_hgrn_lower_bounds, m_hgrn_g_norm, m_hgrn_w_out, m_ffn_w_in, m_ffn_w_out, v_meta_tokens, v_attn_norm, v_ffn_norm, v_final_norm, v_fox_w_in, v_fox_b_f, v_fox_q_norm, v_fox_k_norm, v_fox_w_out, v_hgrn_w_in, v_hgrn_lower_bounds, v_hgrn_g_norm, v_hgrn_w_out, v_ffn_w_in, v_ffn_w_out):
    w = dict(meta_tokens=meta_tokens, attn_norm=attn_norm, ffn_norm=ffn_norm, final_norm=final_norm,
             fox_w_in=fox_w_in, fox_b_f=fox_b_f, fox_q_norm=fox_q_norm, fox_k_norm=fox_k_norm,
             fox_w_out=fox_w_out, hgrn_w_in=hgrn_w_in, hgrn_lower_bounds=hgrn_lower_bounds,
             hgrn_g_norm=hgrn_g_norm, hgrn_w_out=hgrn_w_out, ffn_w_in=ffn_w_in, ffn_w_out=ffn_w_out)
    m = dict(meta_tokens=m_meta_tokens, attn_norm=m_attn_norm, ffn_norm=m_ffn_norm, final_norm=m_final_norm,
             fox_w_in=m_fox_w_in, fox_b_f=m_fox_b_f, fox_q_norm=m_fox_q_norm, fox_k_norm=m_fox_k_norm,
             fox_w_out=m_fox_w_out, hgrn_w_in=m_hgrn_w_in, hgrn_lower_bounds=m_hgrn_lower_bounds,
             hgrn_g_norm=m_hgrn_g_norm, hgrn_w_out=m_hgrn_w_out, ffn_w_in=m_ffn_w_in, ffn_w_out=m_ffn_w_out)
    v = dict(meta_tokens=v_meta_tokens, attn_norm=v_attn_norm, ffn_norm=v_ffn_norm, final_norm=v_final_norm,
             fox_w_in=v_fox_w_in, fox_b_f=v_fox_b_f, fox_q_norm=v_fox_q_norm, fox_k_norm=v_fox_k_norm,
             fox_w_out=v_fox_w_out, hgrn_w_in=v_hgrn_w_in, hgrn_lower_bounds=v_hgrn_lower_bounds,
             hgrn_g_norm=v_hgrn_g_norm, hgrn_w_out=v_hgrn_w_out, ffn_w_in=v_ffn_w_in, ffn_w_out=v_ffn_w_out)
    axes = ("x", "y", "c")
    mats = BIG[1:]
    mat_shapes = [w[n].shape for n in mats]
    big_shapes = [w[n].shape for n in BIG]
    small_shapes = [w[n].shape for n in SMALL]

    wpack = _pack_rows([w[n] for n in mats], BF16)
    mpack = w["meta_tokens"].astype(F32)
    g_w, g_meta = _exchange([wpack, mpack], [False, False], "gather_weights")
    wb = {n: _gathered_to_full(g, n) for n, g in zip(mats, _unpack_rows(g_w, mat_shapes))}
    meta_full = _gathered_to_full(g_meta, "meta_tokens")
    small = {n: w[n] for n in SMALL}

    loss_local, grad_x, grads = _local_step(x[0], loss_target[0], meta_full, wb, small)
    loss = lax.psum(loss_local, axes)

    gpack = _pad_rows(jnp.concatenate([_full_to_slabs(grads[n], n).reshape(N_DEV, -1, LANES) for n in BIG],
                                      axis=1).astype(BF16))
    spack = _pack_small([grads[n] for n in SMALL])
    r_big, r_small = _exchange([gpack, spack], [True, False], "scatter_grads")

    g_b, d_b, m_b, v_b = _adamw(r_big, _pack_rows([w[n] for n in BIG], F32), _pack_rows([m[n] for n in BIG], F32),
                                _pack_rows([v[n] for n in BIG], F32), "adamw_big")
    g_s, d_s, m_s, v_s = _adamw(r_small, _pack_small([w[n] for n in SMALL]), _pack_small([m[n] for n in SMALL]),
                                _pack_small([v[n] for n in SMALL]), "adamw_small")
    outs = []
    for big, sml in ((g_b, g_s), (d_b, d_s), (m_b, m_s), (v_b, v_s)):
        d = dict(zip(BIG, _unpack_rows(big, big_shapes)))
        d.update(zip(SMALL, _unpack_small(sml, small_shapes)))
        outs.extend(d[n] for n in WEIGHTS)
    return (loss, grad_x[None], *outs)
```

```python
import functools
import math

import jax
import jax.numpy as jnp
from jax import lax
from jax.experimental import pallas as pl
from jax.experimental.pallas import tpu as pltpu

F32 = jnp.float32
BF16 = jnp.bfloat16
EPS = 1e-6
N_META = 16
LANES = 128
HEAD_ROWS = 128
N_PAD = HEAD_ROWS - N_META
FOX_DH = 64
HGRN_CHUNK = 64
N_DEV = 8
NEG = -1e30
VMEM_LIMIT = 56 * 1024 * 1024
HI = lax.Precision.HIGHEST

ADAM_LR = 0.001
ADAM_B1 = 0.9
ADAM_B2 = 0.999
ADAM_EPS = 1e-08
ADAM_WD = 0.01
ADAM_STEP = 10

BIG = ("meta_tokens", "fox_w_in", "fox_w_out", "hgrn_w_in", "hgrn_w_out", "ffn_w_in", "ffn_w_out")
SMALL = ("attn_norm", "ffn_norm", "final_norm", "fox_b_f", "fox_q_norm", "fox_k_norm",
         "hgrn_lower_bounds", "hgrn_g_norm")
WEIGHTS = ("meta_tokens", "attn_norm", "ffn_norm", "final_norm", "fox_w_in", "fox_b_f", "fox_q_norm",
           "fox_k_norm", "fox_w_out", "hgrn_w_in", "hgrn_lower_bounds", "hgrn_g_norm", "hgrn_w_out",
           "ffn_w_in", "ffn_w_out")
COL_SHARDED = ("meta_tokens", "fox_w_in", "hgrn_w_in", "ffn_w_in")
LATE = ("hgrn_w_in", "hgrn_w_out", "ffn_w_in", "ffn_w_out")


def _params(sem=None):
    return pltpu.CompilerParams(dimension_semantics=sem, vmem_limit_bytes=VMEM_LIMIT)


def _tile(n, cap):
    best = None
    for t in range(LANES, min(n, cap) + 1, LANES):
        if n % t == 0:
            best = t
    assert best is not None, (n, cap)
    return best


def _row_chunk(n, cap, mult=8):
    best = n
    for t in range(mult, min(n, cap) + 1, mult):
        if n % t == 0:
            best = t
    return best


def _dg(a, b, ca, cb):
    return lax.dot_general(a.astype(BF16), b.astype(BF16), (((ca,), (cb,)), ((), ())),
                           preferred_element_type=F32)


@jax.custom_vjp
def _d_nn(a, b):
    return _dg(a, b, 1, 0)


@jax.custom_vjp
def _d_nt(a, b):
    return _dg(a, b, 1, 1)


@jax.custom_vjp
def _d_tn(a, b):
    return _dg(a, b, 0, 0)


_d_nn.defvjp(lambda a, b: (_d_nn(a, b), (a, b)), lambda r, g: (_d_nt(g, r[1]), _d_tn(r[0], g)))
_d_nt.defvjp(lambda a, b: (_d_nt(a, b), (a, b)), lambda r, g: (_d_nn(g, r[1]), _d_tn(g, r[0])))
_d_tn.defvjp(lambda a, b: (_d_tn(a, b), (a, b)), lambda r, g: (_d_nt(r[1], g), _d_nn(r[0], g)))


def _log_sigmoid(x):
    return jnp.minimum(x, 0.0) - jnp.log1p(jnp.exp(-jnp.abs(x)))


def _rms(x, g):
    return x * lax.rsqrt(jnp.mean(x * x, axis=-1, keepdims=True) + EPS) * g


def _mm(a, b, mode, out_dtype, name, res=None, tm=None, tn=None, tk=None):
    assert a.dtype == BF16 and b.dtype == BF16, (name, a.dtype, b.dtype)
    if mode == "nn":
        (M, K), N = a.shape, b.shape[1]
    elif mode == "nt":
        (M, K), N = a.shape, b.shape[0]
    else:
        (K, M), N = a.shape, b.shape[1]
    tm = tm or _tile(M, 1408 if mode == "tn" else 640)
    tn = tn or _tile(N, 1408)
    tk = tk or _tile(K, 640 if mode == "tn" else 1408)
    nk = K // tk
    if mode == "tn":
        a_spec = pl.BlockSpec((tk, tm), lambda i, j, k: (k, i))
        dims = (((0,), (0,)), ((), ()))
    else:
        a_spec = pl.BlockSpec((tm, tk), lambda i, j, k: (i, k))
        dims = (((1,), (1 if mode == "nt" else 0,)), ((), ()))
    if mode == "nt":
        b_spec = pl.BlockSpec((tn, tk), lambda i, j, k: (j, k))
    else:
        b_spec = pl.BlockSpec((tk, tn), lambda i, j, k: (k, j))

    o_spec = pl.BlockSpec((tm, tn), lambda i, j, k: (i, j))

    def body(a_ref, b_ref, *rest):
        r_ref = rest[0] if res is not None else None
        o_ref, acc_ref = rest[-2:]
        k = pl.program_id(2)

        @pl.when(k == 0)
        def _():
            acc_ref[...] = jnp.zeros_like(acc_ref)

        acc_ref[...] += lax.dot_general(a_ref[...], b_ref[...], dims, preferred_element_type=F32)

        @pl.when(k == nk - 1)
        def _():
            out = acc_ref[...] if r_ref is None else acc_ref[...] + r_ref[...]
            o_ref[...] = out.astype(out_dtype)

    return pl.pallas_call(
        body, name=name, grid=(M // tm, N // tn, nk),
        in_specs=[a_spec, b_spec] + ([o_spec] if res is not None else []),
        out_specs=o_spec,
        out_shape=jax.ShapeDtypeStruct((M, N), out_dtype),
        scratch_shapes=[pltpu.VMEM((tm, tn), F32)],
        compiler_params=_params(("parallel", "parallel", "arbitrary")),
    )(a, b, *([res] if res is not None else []))


def _rms_fwd(x, g, T, name):
    Lp, D = x.shape

    def body(x_ref, g_ref, o_ref):
        o_ref[...] = _rms(x_ref[...], g_ref[...]).astype(BF16)

    return pl.pallas_call(
        body, name=name, grid=(Lp // T,),
        in_specs=[pl.BlockSpec((T, D), lambda i: (i, 0)), pl.BlockSpec((1, D), lambda i: (0, 0))],
        out_specs=pl.BlockSpec((T, D), lambda i: (i, 0)),
        out_shape=jax.ShapeDtypeStruct((Lp, D), BF16),
        compiler_params=_params(("parallel",)),
    )(x, g)


def _rms_bwd(x, g, dy, dres, T, name):
    Lp, D = x.shape

    def body(x_ref, g_ref, dy_ref, dr_ref, dx_ref, dxb_ref, dg_ref):
        @pl.when(pl.program_id(0) == 0)
        def _():
            dg_ref[...] = jnp.zeros_like(dg_ref)

        _, vjp = jax.vjp(_rms, x_ref[...], g_ref[...])
        dx, dg = vjp(dy_ref[...])
        dx = dx + dr_ref[...]
        dx_ref[...] = dx
        dxb_ref[...] = dx.astype(BF16)
        dg_ref[...] += dg

    row = pl.BlockSpec((T, D), lambda i: (i, 0))
    vec = pl.BlockSpec((1, D), lambda i: (0, 0))
    return pl.pallas_call(
        body, name=name, grid=(Lp // T,),
        in_specs=[row, vec, row, row],
        out_specs=[row, row, vec],
        out_shape=[jax.ShapeDtypeStruct((Lp, D), F32), jax.ShapeDtypeStruct((Lp, D), BF16),
                   jax.ShapeDtypeStruct((1, D), F32)],
        compiler_params=_params(("arbitrary",)),
    )(x, g, dy, dres)


def _swiglu(gate, up):
    return gate * jax.nn.sigmoid(gate) * up


def _swiglu_fwd(gu, name):
    Lp, F2 = gu.shape
    F = F2 // 2
    TR = 128

    def body(gu_ref, o_ref):
        o_ref[...] = _swiglu(gu_ref[:, :F], gu_ref[:, F:]).astype(BF16)

    return pl.pallas_call(
        body, name=name, grid=(Lp // TR,),
        in_specs=[pl.BlockSpec((TR, F2), lambda i: (i, 0))],
        out_specs=pl.BlockSpec((TR, F), lambda i: (i, 0)),
        out_shape=jax.ShapeDtypeStruct((Lp, F), BF16),
        compiler_params=_params(("parallel",)),
    )(gu)


def _swiglu_bwd(gu, dact, name):
    Lp, F2 = gu.shape
    F = F2 // 2
    TR = 128

    def body(gu_ref, da_ref, o_ref):
        _, vjp = jax.vjp(_swiglu, gu_ref[:, :F], gu_ref[:, F:])
        dg, du = vjp(da_ref[...])
        o_ref[:, :F] = dg.astype(BF16)
        o_ref[:, F:] = du.astype(BF16)

    return pl.pallas_call(
        body, name=name, grid=(Lp // TR,),
        in_specs=[pl.BlockSpec((TR, F2), lambda i: (i, 0)), pl.BlockSpec((TR, F), lambda i: (i, 0))],
        out_specs=pl.BlockSpec((TR, F2), lambda i: (i, 0)),
        out_shape=jax.ShapeDtypeStruct((Lp, F2), BF16),
        compiler_params=_params(("parallel",)),
    )(gu, dact)


def _final_loss(h, g, target, name):
    Lp, D = h.shape
    TR = HEAD_ROWS

    def loss_fn(hh, gg, tt):
        err = _rms(hh, gg) - tt
        return 0.5 * jnp.sum(jnp.mean(err * err, axis=-1))

    def body(h_ref, g_ref, t_ref, loss_ref, dh_ref, dhb_ref, dg_ref):
        i = pl.program_id(0)

        @pl.when(i == 0)
        def _():
            loss_ref[...] = jnp.zeros_like(loss_ref)
            dg_ref[...] = jnp.zeros_like(dg_ref)
            dh_ref[...] = jnp.zeros_like(dh_ref)
            dhb_ref[...] = jnp.zeros_like(dhb_ref)

        @pl.when(i > 0)
        def _():
            val, vjp = jax.vjp(lambda hh, gg: loss_fn(hh, gg, t_ref[...]), h_ref[...], g_ref[...])
            dh, dg = vjp(jnp.ones((), F32))
            dh_ref[...] = dh
            dhb_ref[...] = dh.astype(BF16)
            dg_ref[...] += dg
            loss_ref[...] += val

    row = pl.BlockSpec((TR, D), lambda i: (i, 0))
    return pl.pallas_call(
        body, name=name, grid=(Lp // TR,),
        in_specs=[row, pl.BlockSpec((1, D), lambda i: (0, 0)),
                  pl.BlockSpec((TR, D), lambda i: (jnp.maximum(i - 1, 0), 0))],
        out_specs=[pl.BlockSpec((8, LANES), lambda i: (0, 0)), row, row, pl.BlockSpec((1, D), lambda i: (0, 0))],
        out_shape=[jax.ShapeDtypeStruct((8, LANES), F32), jax.ShapeDtypeStruct((Lp, D), F32),
                   jax.ShapeDtypeStruct((Lp, D), BF16), jax.ShapeDtypeStruct((1, D), F32)],
        compiler_params=_params(("arbitrary",)),
    )(h, g, target)


def _lane_lo():
    return lax.broadcasted_iota(jnp.int32, (1, LANES), 1) < FOX_DH


def _headnorm(x, g, scale):
    lo = _lane_lo()
    x2 = x * x
    s0 = jnp.sum(jnp.where(lo, x2, 0.0), axis=-1, keepdims=True)
    s1 = jnp.sum(jnp.where(lo, 0.0, x2), axis=-1, keepdims=True)
    r = jnp.where(lo, lax.rsqrt(s0 / FOX_DH + EPS), lax.rsqrt(s1 / FOX_DH + EPS))
    return x * r * g * scale


def _fox_prep_fwd(proj, bf, qg, kg, T, D, name):
    Lp = proj.shape[0]
    nb = D // LANES
    scale = FOX_DH ** -0.5

    def body(q_ref, k_ref, v_ref, fl_ref, bf_ref, qg_ref, kg_ref, qn_ref, kn_ref, vb_ref, c_ref, carry_ref):
        @pl.when(pl.program_id(0) == 0)
        def _():
            carry_ref[...] = jnp.zeros_like(carry_ref)

        for b in range(nb):
            sl = slice(b * LANES, (b + 1) * LANES)
            qn_ref[:, sl] = _headnorm(q_ref[:, sl], qg_ref[...], scale).astype(BF16)
            kn_ref[:, sl] = _headnorm(k_ref[:, sl], kg_ref[...], 1.0).astype(BF16)
        vb_ref[...] = v_ref[...].astype(BF16)
        log_f = _log_sigmoid(fl_ref[...] + bf_ref[...])
        row = lax.broadcasted_iota(jnp.int32, (T, T), 0)
        col = lax.broadcasted_iota(jnp.int32, (T, T), 1)
        tri = (col <= row).astype(F32)
        c = jnp.dot(tri, log_f, precision=HI, preferred_element_type=F32) + carry_ref[...]
        c_ref[...] = c
        last = lax.broadcasted_iota(jnp.int32, (T, 1), 0) == T - 1
        carry_ref[...] = jnp.sum(jnp.where(last, c, 0.0), axis=0, keepdims=True)

    wide = lambda j: pl.BlockSpec((T, D), lambda i: (i, j))
    vec = pl.BlockSpec((1, LANES), lambda i: (0, 0))
    return pl.pallas_call(
        body, name=name, grid=(Lp // T,),
        in_specs=[wide(0), wide(1), wide(2), pl.BlockSpec((T, LANES), lambda i: (i, 4 * nb)), vec, vec, vec],
        out_specs=[wide(0), wide(0), wide(0), pl.BlockSpec((T, LANES), lambda i: (i, 0))],
        out_shape=[jax.ShapeDtypeStruct((Lp, D), BF16)] * 3 + [jax.ShapeDtypeStruct((Lp, LANES), F32)],
        scratch_shapes=[pltpu.VMEM((1, LANES), F32)],
        compiler_params=_params(("arbitrary",)),
    )(proj, proj, proj, proj, bf, qg, kg)


def _fox_prep_bwd(proj, bf, qg, kg, dqn, dkn, dc, T, D, name):
    Lp = proj.shape[0]
    nb = D // LANES
    nt = Lp // T
    scale = FOX_DH ** -0.5

    def body(q_ref, k_ref, fl_ref, bf_ref, qg_ref, kg_ref, dqn_ref, dkn_ref, dc_ref,
             dq_ref, dk_ref, dfl_ref, sm_ref, carry_ref):
        @pl.when(pl.program_id(0) == 0)
        def _():
            carry_ref[...] = jnp.zeros_like(carry_ref)
            sm_ref[...] = jnp.zeros_like(sm_ref)

        dqg = jnp.zeros((1, LANES), F32)
        dkg = jnp.zeros((1, LANES), F32)
        for b in range(nb):
            sl = slice(b * LANES, (b + 1) * LANES)
            _, vjp = jax.vjp(lambda x, g: _headnorm(x, g, scale), q_ref[:, sl], qg_ref[...])
            dx, dg = vjp(dqn_ref[:, sl])
            dq_ref[:, sl] = dx.astype(BF16)
            dqg = dqg + dg
            _, vjp = jax.vjp(lambda x, g: _headnorm(x, g, 1.0), k_ref[:, sl], kg_ref[...])
            dx, dg = vjp(dkn_ref[:, sl])
            dk_ref[:, sl] = dx.astype(BF16)
            dkg = dkg + dg
        dcv = dc_ref[...]
        row = lax.broadcasted_iota(jnp.int32, (T, T), 0)
        col = lax.broadcasted_iota(jnp.int32, (T, T), 1)
        triu = (col >= row).astype(F32)
        dlogf = jnp.dot(triu, dcv, precision=HI, preferred_element_type=F32) + carry_ref[...]
        carry_ref[...] += jnp.sum(dcv, axis=0, keepdims=True)
        _, vjp = jax.vjp(_log_sigmoid, fl_ref[...] + bf_ref[...])
        (dfl,) = vjp(dlogf)
        dfl_ref[...] = dfl.astype(BF16)
        sm_ref[0:1, :] += jnp.sum(dfl, axis=0, keepdims=True)
        sm_ref[1:2, :] += dqg
        sm_ref[2:3, :] += dkg

    wide = lambda j: pl.BlockSpec((T, D), lambda i: (nt - 1 - i, j))
    narrow = lambda j: pl.BlockSpec((T, LANES), lambda i: (nt - 1 - i, j))
    vec = pl.BlockSpec((1, LANES), lambda i: (0, 0))
    return pl.pallas_call(
        body, name=name, grid=(nt,),
        in_specs=[wide(0), wide(1), narrow(4 * nb), vec, vec, vec, wide(0), wide(0), narrow(0)],
        out_specs=[wide(0), wide(0), narrow(0), pl.BlockSpec((8, LANES), lambda i: (0, 0))],
        out_shape=[jax.ShapeDtypeStruct((Lp, D), BF16)] * 2 + [jax.ShapeDtypeStruct((Lp, LANES), BF16),
                                                                 jax.ShapeDtypeStruct((8, LANES), F32)],
        scratch_shapes=[pltpu.VMEM((1, LANES), F32)],
        compiler_params=_params(("arbitrary",)),
    )(proj, proj, proj, bf, qg, kg, dqn, dkn, dc)


def _ln2_ceil(m):
    return jnp.ceil(m * (1.0 / math.log(2.0))) * math.log(2.0)


def _fox_mask(i, k0, T):
    qpos = i * T + lax.broadcasted_iota(jnp.int32, (T, 1), 0)
    kpos = k0 + lax.broadcasted_iota(jnp.int32, (1, T), 1)
    return (kpos <= qpos) & ((kpos >= N_PAD) | (qpos < N_PAD))


def _pick_col(blk, idx):
    lane = lax.broadcasted_iota(jnp.int32, (1, LANES), 1)
    return jnp.sum(jnp.where(lane == idx, blk, 0.0), axis=1, keepdims=True)


def _split_halves(blk):
    lo = _lane_lo()
    return (jnp.max(jnp.where(lo, blk, -jnp.inf), axis=1, keepdims=True),
            jnp.max(jnp.where(lo, -jnp.inf, blk), axis=1, keepdims=True))


def _fox_attn_fwd(qn, kn, vb, c, cT, proj, xchg, T, D, name):
    Lp = qn.shape[0]
    P = D // LANES
    nt = Lp // T
    H = cT.shape[0]
    nx = len(xchg)

    def body(q_ref, k_ref, v_ref, c_ref, cT_ref, g_ref, *rest):
        x_in, (o_ref, og_ref, m_ref, li_ref), x_out, sems = rest[:nx], rest[nx:nx + 4], rest[nx + 4:2 * nx + 4], rest[2 * nx + 4:]
        p = pl.program_id(0)
        i = pl.program_id(1)

        @pl.when((p == 0) & (i == 0))
        def _():
            for cp in _xchg_copies(x_in, x_out, [False] * nx, sems):
                cp.start()

        lo = _lane_lo()
        q = q_ref[...]
        zero = jnp.zeros_like(q)
        qh = (jnp.where(lo, q, zero), jnp.where(lo, zero, q))
        cblk = c_ref[...]
        cq = tuple(_pick_col(cblk, 2 * p + h) for h in (0, 1))
        one = jnp.ones_like(q)

        def step(j, carry, masked):
            k0 = pl.multiple_of(j * T, LANES)
            kj = k_ref[pl.ds(k0, T), :]
            vj = v_ref[pl.ds(k0, T), :]
            vh = (jnp.where(lo, vj, one), jnp.where(lo, one, vj))
            mask = _fox_mask(i, k0, T) if masked else None
            out = []
            for h in (0, 1):
                m, acc = carry[h]
                ck = cT_ref[pl.ds(2 * p + h, 1), pl.ds(k0, T)]
                t = lax.dot_general(qh[h], kj, (((1,), (1,)), ((), ())), preferred_element_type=F32) - ck
                if masked:
                    t = jnp.where(mask, t, NEG)
                m_new = _ln2_ceil(jnp.maximum(m, cq[h] + jnp.max(t, axis=1, keepdims=True)))
                pr = jnp.exp(t + (cq[h] - m_new)).astype(BF16)
                acc = jnp.exp(m - m_new) * acc + jnp.dot(pr, vh[h], preferred_element_type=F32)
                out.append((m_new, acc))
            return tuple(out)

        init = tuple((jnp.full((T, 1), NEG, F32), jnp.zeros((T, LANES), F32)) for _ in (0, 1))
        carry = step(0, init, True)
        carry = lax.fori_loop(1, i, lambda j, cr: step(j, cr, False), carry)
        (m0, a0), (m1, a1) = lax.cond(i > 0, lambda cr: step(i, cr, True), lambda cr: cr, carry)
        l0 = pltpu.roll(a0, FOX_DH, 1)
        l1 = pltpu.roll(a1, FOX_DH, 1)
        o = jnp.where(lo, a0 / l0, a1 / l1)
        o_ref[...] = o
        m_ref[...] = jnp.where(lo, m0, m1)
        li_ref[...] = jnp.where(lo, 1.0 / l0, 1.0 / l1)
        og_ref[...] = (o * jax.nn.sigmoid(g_ref[...])).astype(BF16)

        @pl.when((p == P - 1) & (i == nt - 1))
        def _():
            for cp in _xchg_copies(x_in, x_out, [False] * nx, sems):
                cp.wait()

    tile = pl.BlockSpec((T, LANES), lambda p, i: (i, p))
    full = pl.BlockSpec((Lp, LANES), lambda p, i: (0, p))
    HBM = pl.BlockSpec(memory_space=pltpu.HBM)
    return pl.pallas_call(
        body, name=name, grid=(P, nt),
        in_specs=[tile, full, full, pl.BlockSpec((T, LANES), lambda p, i: (i, 0)),
                  pl.BlockSpec((H, Lp), lambda p, i: (0, 0)),
                  pl.BlockSpec((T, LANES), lambda p, i: (i, 3 * P + p))] + [HBM] * nx,
        out_specs=[tile, tile, tile, tile] + [HBM] * nx,
        out_shape=[jax.ShapeDtypeStruct((Lp, D), F32), jax.ShapeDtypeStruct((Lp, D), BF16),
                   jax.ShapeDtypeStruct((Lp, D), F32), jax.ShapeDtypeStruct((Lp, D), F32)]
        + _xchg_out_shapes(xchg, [False] * nx),
        scratch_shapes=_xchg_sems(nx),
        compiler_params=_params(("arbitrary", "arbitrary")),
    )(qn, kn, vb, c, cT, proj, *xchg)


def _fox_gate_bwd(dog, o, proj, linv, T, D, name):
    Lp = o.shape[0]
    P = D // LANES

    def body(dog_ref, o_ref, g_ref, li_ref, do_ref, dg_ref, dl_ref):
        lo = _lane_lo()
        sig = jax.nn.sigmoid(g_ref[...])
        ov = o_ref[...]
        do = (dog_ref[...] * sig * li_ref[...]).astype(BF16)
        do_ref[...] = do
        dg_ref[...] = (dog_ref[...] * ov * sig * (1.0 - sig)).astype(BF16)
        t = do.astype(F32) * ov
        d0 = jnp.sum(jnp.where(lo, t, 0.0), axis=1, keepdims=True)
        d1 = jnp.sum(jnp.where(lo, 0.0, t), axis=1, keepdims=True)
        dl_ref[...] = jnp.where(lo, d0, d1)

    tile = pl.BlockSpec((T, LANES), lambda i, p: (i, p))
    return pl.pallas_call(
        body, name=name, grid=(Lp // T, P),
        in_specs=[tile, tile, pl.BlockSpec((T, LANES), lambda i, p: (i, 3 * P + p)), tile],
        out_specs=[tile, tile, tile],
        out_shape=[jax.ShapeDtypeStruct((Lp, D), BF16), jax.ShapeDtypeStruct((Lp, D), BF16),
                   jax.ShapeDtypeStruct((Lp, D), F32)],
        compiler_params=_params(("parallel", "parallel")),
    )(dog, o, proj, linv)


def _fox_attn_bwd(qn, kn, vb, c, cT, do, mshift, delta, xchg, T, D, name):
    Lp = qn.shape[0]
    P = D // LANES
    nt = Lp // T
    H = cT.shape[0]
    nx = len(xchg)

    def body(q_ref, do_ref, m_ref, dl_ref, c_ref, k_ref, v_ref, cT_ref, *rest):
        x_in, (dq_ref, dk_ref, dv_ref, dc_ref), x_out, sems = rest[:nx], rest[nx:nx + 4], rest[nx + 4:2 * nx + 4], rest[2 * nx + 4:]
        p = pl.program_id(0)
        i = pl.program_id(1)

        @pl.when((p == 0) & (i == 0))
        def _():
            for cp in _xchg_copies(x_in, x_out, [True] * nx, sems):
                cp.start()

        @pl.when(i == 0)
        def _():
            dk_ref[...] = jnp.zeros_like(dk_ref)
            dv_ref[...] = jnp.zeros_like(dv_ref)
            dc_ref[...] = jnp.zeros_like(dc_ref)

        lo = _lane_lo()
        q = q_ref[...]
        do = do_ref[...]
        zero = jnp.zeros_like(q)
        qh = (jnp.where(lo, q, zero), jnp.where(lo, zero, q))
        doh = (jnp.where(lo, do, zero), jnp.where(lo, zero, do))
        msh = _split_halves(m_ref[...])
        dlt = _split_halves(dl_ref[...])
        cblk = c_ref[...]
        shift = tuple(_pick_col(cblk, 2 * p + h) - msh[h] for h in (0, 1))

        def step(j, carry, masked):
            k0 = pl.multiple_of(j * T, LANES)
            kj = k_ref[pl.ds(k0, T), :]
            vj = v_ref[pl.ds(k0, T), :]
            mask = _fox_mask(i, k0, T) if masked else None
            dqs, dks, dvs = [], [], []
            for h in (0, 1):
                ck = cT_ref[pl.ds(2 * p + h, 1), pl.ds(k0, T)]
                t = lax.dot_general(qh[h], kj, (((1,), (1,)), ((), ())), preferred_element_type=F32) - ck
                if masked:
                    t = jnp.where(mask, t, NEG)
                pb = jnp.exp(t + shift[h]).astype(BF16)
                dp = lax.dot_general(doh[h], vj, (((1,), (1,)), ((), ())), preferred_element_type=F32)
                ds = pb.astype(F32) * (dp - dlt[h])
                dsb = ds.astype(BF16)
                dqs.append(carry[h] + jnp.dot(dsb, kj, preferred_element_type=F32))
                dks.append(lax.dot_general(dsb, q, (((0,), (0,)), ((), ())), preferred_element_type=F32))
                dvs.append(lax.dot_general(pb, do, (((0,), (0,)), ((), ())), preferred_element_type=F32))
                dc_ref[0, h:h + 1, pl.ds(k0, T)] += -jnp.sum(ds, axis=0, keepdims=True)
            dk_ref[pl.ds(k0, T), :] += jnp.where(lo, dks[0], dks[1])
            dv_ref[pl.ds(k0, T), :] += jnp.where(lo, dvs[0], dvs[1])
            return tuple(dqs)

        init = (jnp.zeros((T, LANES), F32), jnp.zeros((T, LANES), F32))
        carry = step(0, init, True)
        carry = lax.fori_loop(1, i, lambda j, cr: step(j, cr, False), carry)
        dq0, dq1 = lax.cond(i > 0, lambda cr: step(i, cr, True), lambda cr: cr, carry)
        dq_ref[...] = jnp.where(lo, dq0, dq1)

        @pl.when((p == P - 1) & (i == nt - 1))
        def _():
            for cp in _xchg_copies(x_in, x_out, [True] * nx, sems):
                cp.wait()

    tile = pl.BlockSpec((T, LANES), lambda p, i: (i, p))
    full = pl.BlockSpec((Lp, LANES), lambda p, i: (0, p))
    HBM = pl.BlockSpec(memory_space=pltpu.HBM)
    return pl.pallas_call(
        body, name=name, grid=(P, nt),
        in_specs=[tile, tile, tile, tile, pl.BlockSpec((T, LANES), lambda p, i: (i, 0)), full, full,
                  pl.BlockSpec((H, Lp), lambda p, i: (0, 0))] + [HBM] * nx,
        out_specs=[tile, full, full, pl.BlockSpec((1, 8, Lp), lambda p, i: (p, 0, 0))] + [HBM] * nx,
        out_shape=[jax.ShapeDtypeStruct((Lp, D), F32)] * 3 + [jax.ShapeDtypeStruct((P, 8, Lp), F32)]
        + _xchg_out_shapes(xchg, [True] * nx),
        scratch_shapes=_xchg_sems(nx),
        compiler_params=_params(("arbitrary", "arbitrary")),
    )(qn, do, mshift, delta, c, kn, vb, cT, *xchg)


def _hgrn_chunk(St, qr, z, vi, go, p0, p1, gg):
    C = qr.shape[0]
    lb = jax.nn.sigmoid(p1 - p0)
    a = jnp.log(lb)
    cc = jnp.log1p(-lb) + _log_sigmoid(z)
    log_f = jnp.maximum(a, cc) + jnp.log1p(jnp.exp(-jnp.abs(a - cc)))
    k = (1.0 - lb) * jax.nn.sigmoid(-z)
    q = qr * jax.nn.sigmoid(qr)
    row = lax.broadcasted_iota(jnp.int32, (C, C), 0)
    col = lax.broadcasted_iota(jnp.int32, (C, C), 1)
    causal = col <= row
    b = jnp.dot(causal.astype(F32), log_f, precision=HI, preferred_element_type=F32)
    mid = lax.broadcasted_iota(jnp.int32, (C, 1), 0) == C // 2 - 1
    r = jnp.sum(jnp.where(mid, b, 0.0), axis=0, keepdims=True)
    b_last = jnp.sum(log_f, axis=0, keepdims=True)
    attn = jnp.where(causal, _d_nt(q * jnp.exp(b - r), k * jnp.exp(r - b)), 0.0)
    o = _d_nn(attn, vi) + _d_nt(q * jnp.exp(b), St)
    St_new = St * jnp.exp(b_last) + _d_tn(vi, k * jnp.exp(b_last - b))
    og = _rms(o, gg) * (go * jax.nn.sigmoid(go))
    return St_new, og


def _hgrn_heads_per_step(H):
    return 4 if H % 4 == 0 else 1


def _hgrn_specs(T, W, nhb, rev_nt=None):
    if rev_nt is None:
        return [pl.BlockSpec((T, W), functools.partial(lambda hb, t, g: (t, g * nhb + hb), g=g)) for g in range(4)]
    return [pl.BlockSpec((T, W), functools.partial(lambda hb, t, g: (rev_nt - 1 - t, g * nhb + hb), g=g))
            for g in range(4)]


def _hgrn_fwd(proj, lbp, gg, T, name):
    Lp = proj.shape[0]
    D = proj.shape[1] // 4
    H = D // LANES
    hps = _hgrn_heads_per_step(H)
    W = hps * LANES
    nhb = H // hps
    nt = Lp // T
    ncc = T // HGRN_CHUNK

    def body(q_ref, z_ref, i_ref, go_ref, p_ref, gg_ref, og_ref, ss_ref, st_ref):
        @pl.when(pl.program_id(1) == 0)
        def _():
            st_ref[...] = jnp.zeros_like(st_ref)

        gain = gg_ref[...]

        def chunk(cidx, carry):
            sl = pl.ds(pl.multiple_of(cidx * HGRN_CHUNK, HGRN_CHUNK), HGRN_CHUNK)
            for hh in range(hps):
                ln = slice(hh * LANES, (hh + 1) * LANES)
                St = st_ref[hh]
                ss_ref[hh, cidx] = St
                St_new, og = _hgrn_chunk(St, q_ref[sl, ln], z_ref[sl, ln], i_ref[sl, ln], go_ref[sl, ln],
                                         p_ref[0:1, ln], p_ref[1:2, ln], gain)
                st_ref[hh] = St_new
                og_ref[sl, ln] = og.astype(BF16)
            return carry

        lax.fori_loop(0, ncc, chunk, 0)

    return pl.pallas_call(
        body, name=name, grid=(nhb, nt),
        in_specs=_hgrn_specs(T, W, nhb) + [pl.BlockSpec((2, W), lambda hb, t: (0, hb)),
                                           pl.BlockSpec((1, LANES), lambda hb, t: (0, 0))],
        out_specs=[pl.BlockSpec((T, W), lambda hb, t: (t, hb)),
                   pl.BlockSpec((hps, ncc, LANES, LANES), lambda hb, t: (hb, t, 0, 0))],
        out_shape=[jax.ShapeDtypeStruct((Lp, D), BF16),
                   jax.ShapeDtypeStruct((H, Lp // HGRN_CHUNK, LANES, LANES), F32)],
        scratch_shapes=[pltpu.VMEM((hps, LANES, LANES), F32)],
        compiler_params=_params(("parallel", "arbitrary")),
    )(proj, proj, proj, proj, lbp, gg)


def _hgrn_bwd(proj, lbp, gg, dog, ss, T, name):
    Lp = proj.shape[0]
    D = proj.shape[1] // 4
    H = D // LANES
    hps = _hgrn_heads_per_step(H)
    W = hps * LANES
    nhb = H // hps
    nt = Lp // T
    ncc = T // HGRN_CHUNK

    def body(q_ref, z_ref, i_ref, go_ref, p_ref, gg_ref, dog_ref, ss_ref,
             dq_ref, dz_ref, di_ref, dgo_ref, dp_ref, dgg_ref, dst_ref):
        hb = pl.program_id(0)
        t = pl.program_id(1)

        @pl.when(t == 0)
        def _():
            dst_ref[...] = jnp.zeros_like(dst_ref)
            dp_ref[...] = jnp.zeros_like(dp_ref)

        @pl.when((t == 0) & (hb == 0))
        def _():
            dgg_ref[...] = jnp.zeros_like(dgg_ref)

        gain = gg_ref[...]
        row0 = (nt - 1 - t) * T

        def chunk(cc, carry):
            cidx = ncc - 1 - cc
            r0 = pl.multiple_of(cidx * HGRN_CHUNK, HGRN_CHUNK)
            sl = pl.ds(r0, HGRN_CHUNK)
            real = (row0 + r0 + lax.broadcasted_iota(jnp.int32, (HGRN_CHUNK, 1), 0)) >= N_PAD
            for hh in range(hps):
                ln = slice(hh * LANES, (hh + 1) * LANES)
                _, vjp = jax.vjp(_hgrn_chunk, ss_ref[hh, cidx], q_ref[sl, ln], z_ref[sl, ln], i_ref[sl, ln],
                                 go_ref[sl, ln], p_ref[0:1, ln], p_ref[1:2, ln], gain)
                dSt, dq, dz, di, dgo, dp0, dp1, dgain = vjp((dst_ref[hh], dog_ref[sl, ln]))
                dq_ref[sl, ln] = jnp.where(real, dq, 0.0).astype(BF16)
                dz_ref[sl, ln] = jnp.where(real, dz, 0.0).astype(BF16)
                di_ref[sl, ln] = jnp.where(real, di, 0.0).astype(BF16)
                dgo_ref[sl, ln] = jnp.where(real, dgo, 0.0).astype(BF16)
                dst_ref[hh] = dSt
                dp_ref[0:1, ln] += dp0
                dp_ref[1:2, ln] += dp1
                dgg_ref[0:1, :] += dgain
            return carry

        lax.fori_loop(0, ncc, chunk, 0)

    rev = pl.BlockSpec((T, W), lambda hb, t: (nt - 1 - t, hb))
    return pl.pallas_call(
        body, name=name, grid=(nhb, nt),
        in_specs=_hgrn_specs(T, W, nhb, nt) + [pl.BlockSpec((2, W), lambda hb, t: (0, hb)),
                                               pl.BlockSpec((1, LANES), lambda hb, t: (0, 0)), rev,
                                               pl.BlockSpec((hps, ncc, LANES, LANES),
                                                            lambda hb, t: (hb, nt - 1 - t, 0, 0))],
        out_specs=[rev, rev, rev, rev, pl.BlockSpec((8, W), lambda hb, t: (0, hb)),
                   pl.BlockSpec((8, LANES), lambda hb, t: (0, 0))],
        out_shape=[jax.ShapeDtypeStruct((Lp, D), BF16)] * 4 + [jax.ShapeDtypeStruct((8, D), F32),
                                                                 jax.ShapeDtypeStruct((8, LANES), F32)],
        scratch_shapes=[pltpu.VMEM((hps, LANES, LANES), F32)],
        compiler_params=_params(("arbitrary", "arbitrary")),
    )(proj, proj, proj, proj, lbp, gg, dog, ss)


def _xchg_sems(n_arr):
    return [pltpu.SemaphoreType.DMA((n_arr * (N_DEV - 1),)), pltpu.SemaphoreType.DMA((n_arr * (N_DEV - 1),)),
            pltpu.SemaphoreType.DMA((n_arr,))]


def _xchg_copies(ins, outs, per_peer, sems):
    send_sems, recv_sems, local_sems = sems
    x, y, c = lax.axis_index("x"), lax.axis_index("y"), lax.axis_index("c")
    me = 4 * x + 2 * y + c
    copies = []
    for n in range(len(ins)):
        src = ins[n].at[me] if per_peer[n] else ins[n]
        copies.append(pltpu.make_async_copy(src, outs[n].at[me], local_sems.at[n]))
    for rel in range(1, N_DEV):
        fx, fy, fc = (rel >> 2) & 1, (rel >> 1) & 1, rel & 1
        px = 1 - x if fx else x
        py = 1 - y if fy else y
        pc = 1 - c if fc else c
        peer = 4 * px + 2 * py + pc
        for n in range(len(ins)):
            src = ins[n].at[peer] if per_peer[n] else ins[n]
            copies.append(pltpu.make_async_remote_copy(
                src_ref=src, dst_ref=outs[n].at[me],
                send_sem=send_sems.at[n * (N_DEV - 1) + rel - 1],
                recv_sem=recv_sems.at[n * (N_DEV - 1) + rel - 1],
                device_id=(px, py, pc), device_id_type=pl.DeviceIdType.MESH))
    return copies


def _xchg_out_shapes(arrays, per_peer):
    return [jax.ShapeDtypeStruct(a.shape if pp else (N_DEV,) + a.shape, a.dtype) for a, pp in zip(arrays, per_peer)]


def _exchange(arrays, per_peer, name):
    n_arr = len(arrays)
    HBM = pl.BlockSpec(memory_space=pltpu.HBM)

    def body(*refs):
        copies = _xchg_copies(refs[:n_arr], refs[n_arr:2 * n_arr], per_peer, refs[2 * n_arr:])
        for cp in copies:
            cp.start()
        for cp in copies:
            cp.wait()

    return pl.pallas_call(
        body, name=name,
        in_specs=[HBM] * n_arr, out_specs=[HBM] * n_arr, out_shape=_xchg_out_shapes(arrays, per_peer),
        scratch_shapes=_xchg_sems(n_arr),
    )(*arrays)


ADAMW_VMEM_BUDGET = 36 * 1024 * 1024


def _adamw(recv, w, m, v, name):
    shape = w.shape
    C = shape[-1]
    R = math.prod(shape[:-1])
    lanes = -(-C // LANES) * LANES
    row_bytes = 2 * lanes * (N_DEV * recv.dtype.itemsize + 7 * 4)
    rc = _row_chunk(R, max(16, ADAMW_VMEM_BUDGET // row_bytes), 16 if recv.dtype == BF16 else 8)

    def body(r_ref, w_ref, m_ref, v_ref, g_ref, d_ref, mo_ref, vo_ref):
        g = r_ref[0].astype(F32)
        for s in range(1, N_DEV):
            g = g + r_ref[s].astype(F32)
        mn = ADAM_B1 * m_ref[...] + (1.0 - ADAM_B1) * g
        vn = ADAM_B2 * v_ref[...] + (1.0 - ADAM_B2) * (g * g)
        m_hat = mn / (1.0 - ADAM_B1 ** ADAM_STEP)
        v_hat = vn / (1.0 - ADAM_B2 ** ADAM_STEP)
        g_ref[...] = g
        d_ref[...] = -ADAM_LR * (m_hat / (jnp.sqrt(v_hat) + ADAM_EPS) + ADAM_WD * w_ref[...])
        mo_ref[...] = mn
        vo_ref[...] = vn

    row = pl.BlockSpec((rc, C), lambda i: (i, 0))
    outs = pl.pallas_call(
        body, name=name, grid=(R // rc,),
        in_specs=[pl.BlockSpec((N_DEV, rc, C), lambda i: (0, i, 0)), row, row, row],
        out_specs=[row] * 4,
        out_shape=[jax.ShapeDtypeStruct((R, C), F32)] * 4,
        compiler_params=_params(("parallel",)),
    )(recv.reshape(N_DEV, R, C), w.reshape(R, C), m.reshape(R, C), v.reshape(R, C))
    return [o.reshape(shape) for o in outs]


def _gathered_to_full(g, name):
    if name in COL_SHARDED:
        g = jnp.moveaxis(g, 0, -2)
        return g.reshape(g.shape[:-2] + (g.shape[-2] * g.shape[-1],))
    g = jnp.moveaxis(g, 0, -3)
    return g.reshape(g.shape[:-3] + (g.shape[-3] * g.shape[-2], g.shape[-1]))


def _full_to_slabs(full, name):
    if name in COL_SHARDED:
        f = full.reshape(full.shape[:-1] + (N_DEV, full.shape[-1] // N_DEV))
        return jnp.moveaxis(f, -2, 0)
    f = full.reshape(full.shape[:-2] + (N_DEV, full.shape[-2] // N_DEV, full.shape[-1]))
    return jnp.moveaxis(f, -3, 0)


def _pack_small(arrs):
    rows = []
    for a in arrs:
        flat = a.astype(F32).reshape(-1)
        pad = (-flat.shape[0]) % LANES
        rows.append(jnp.pad(flat, (0, pad)).reshape(-1, LANES))
    p = jnp.concatenate(rows, axis=0)
    return jnp.pad(p, ((0, (-p.shape[0]) % 8), (0, 0)))


def _unpack_small(packed, shapes):
    out, off = [], 0
    for shp in shapes:
        n = math.prod(shp)
        nr = -(-n // LANES)
        out.append(packed[off:off + nr].reshape(-1)[:n].reshape(shp))
        off += nr
    return out


def _local_step(x, target, meta, w_fox_in, w_fox_out, late, small):
    S, D = x.shape
    Lp = S + HEAD_ROWS
    T = 640 if Lp % 640 == 0 else 128
    P = D // LANES
    row = lambda v: v.reshape(1, -1).astype(F32)

    w_fin = jnp.pad(w_fox_in[0], ((0, 0), (0, LANES - w_fox_in.shape[-1] % LANES)))
    w_fout = w_fox_out[0]
    n_heads = w_fox_in.shape[-1] - 4 * D
    bf = jnp.pad(row(small["fox_b_f"]), ((0, 0), (0, LANES - small["fox_b_f"].size)))
    qg = jnp.tile(row(small["fox_q_norm"]), (1, 2))
    kg = jnp.tile(row(small["fox_k_norm"]), (1, 2))

    h0 = jnp.concatenate([jnp.zeros((N_PAD, D), F32), meta, x], axis=0)

    hn0 = _rms_fwd(h0, row(small["attn_norm"][0]), T, "rms0_fwd")
    proj0 = _mm(hn0, w_fin, "nn", F32, "fox_in_fwd")
    qn, kn, vb, c = _fox_prep_fwd(proj0, bf, qg, kg, T, D, "fox_prep_fwd")
    cT = c.T[:2 * P]
    o, og0, mshift, linv, *gathered = _fox_attn_fwd(qn, kn, vb, c, cT, proj0, [late[n] for n in LATE], T, D,
                                                    "fox_attn_fwd")
    wl = {n: _gathered_to_full(g, n) for n, g in zip(LATE, gathered)}
    w_hin, w_hout, w_uin, w_uout = wl["hgrn_w_in"][0], wl["hgrn_w_out"][0], wl["ffn_w_in"], wl["ffn_w_out"]
    h1 = _mm(og0, w_fout, "nn", F32, "fox_out_fwd", res=h0)
    hf0 = _rms_fwd(h1, row(small["ffn_norm"][0]), T, "rmsf0_fwd")
    gu0 = _mm(hf0, w_uin[0], "nn", F32, "ffn0_in_fwd")
    act0 = _swiglu_fwd(gu0, "swiglu0_fwd")
    h2 = _mm(act0, w_uout[0], "nn", F32, "ffn0_out_fwd", res=h1)
    hn1 = _rms_fwd(h2, row(small["attn_norm"][1]), T, "rms1_fwd")
    proj1 = _mm(hn1, w_hin, "nn", F32, "hgrn_in_fwd")
    lbp = small["hgrn_lower_bounds"].astype(F32)
    ggn = row(small["hgrn_g_norm"])
    og1, ss = _hgrn_fwd(proj1, lbp, ggn, T, "hgrn_fwd")
    h3 = _mm(og1, w_hout, "nn", F32, "hgrn_out_fwd", res=h2)
    hf1 = _rms_fwd(h3, row(small["ffn_norm"][1]), T, "rmsf1_fwd")
    gu1 = _mm(hf1, w_uin[1], "nn", F32, "ffn1_in_fwd")
    act1 = _swiglu_fwd(gu1, "swiglu1_fwd")
    h4 = _mm(act1, w_uout[1], "nn", F32, "ffn1_out_fwd", res=h3)
    loss_blk, dh4, dh4b, d_final = _final_loss(h4, row(small["final_norm"]), target, "final_loss")

    grads = {}

    def ffn_bwd(i, dh, dhb, h_in, hf, gu, act, tag):
        grads_out = _mm(act, dhb, "tn", F32, f"ffn{i}_out_dw")
        dact = _mm(dhb, w_uout[i], "nt", F32, f"ffn{i}_out_dx")
        dgu = _swiglu_bwd(gu, dact, f"swiglu{i}_bwd")
        grads_in = _mm(hf, dgu, "tn", F32, f"ffn{i}_in_dw")
        dhf = _mm(dgu, w_uin[i], "nt", F32, f"ffn{i}_in_dx")
        dh_new, dh_newb, dgain = _rms_bwd(h_in, row(small["ffn_norm"][i]), dhf, dh, T, f"rmsf{i}_bwd")
        return dh_new, dh_newb, grads_in, grads_out, dgain

    dh3, dh3b, g_uin1, g_uout1, d_fn1 = ffn_bwd(1, dh4, dh4b, h3, hf1, gu1, act1, "1")
    grads["hgrn_w_out"] = _mm(og1, dh3b, "tn", F32, "hgrn_out_dw")[None]
    dog1 = _mm(dh3b, w_hout, "nt", F32, "hgrn_out_dx")
    dq1, dz1, di1, dgo1, d_lb, d_gg = _hgrn_bwd(proj1, lbp, ggn, dog1, ss, T, "hgrn_bwd")
    dproj1 = jnp.concatenate([dq1, dz1, di1, dgo1], axis=1)
    grads["hgrn_w_in"] = _mm(hn1, dproj1, "tn", F32, "hgrn_in_dw")[None]
    dhn1 = _mm(dproj1, w_hin, "nt", F32, "hgrn_in_dx")
    dh2, dh2b, d_an1 = _rms_bwd(h2, row(small["attn_norm"][1]), dhn1, dh3, T, "rms1_bwd")
    dh1, dh1b, g_uin0, g_uout0, d_fn0 = ffn_bwd(0, dh2, dh2b, h1, hf0, gu0, act0, "0")
    grads["ffn_w_in"] = jnp.stack([g_uin0, g_uin1])
    grads["ffn_w_out"] = jnp.stack([g_uout0, g_uout1])
    grads["fox_w_out"] = _mm(og0, dh1b, "tn", F32, "fox_out_dw")[None]
    dog0 = _mm(dh1b, w_fout, "nt", F32, "fox_out_dx")
    do, dgate, delta = _fox_gate_bwd(dog0, o, proj0, linv, T, D, "fox_gate_bwd")
    slabs = [_full_to_slabs(grads[n], n).astype(BF16) for n in LATE]
    dqn, dkn, dv, dcr, *recv = _fox_attn_bwd(qn, kn, vb, c, cT, do, mshift, delta, slabs, T, D, "fox_attn_bwd")
    for n in LATE:
        del grads[n]
    dc = jnp.pad(dcr[:, :2, :].reshape(2 * P, Lp).T, ((0, 0), (0, LANES - 2 * P)))
    dq0, dk0, dfl, sm = _fox_prep_bwd(proj0, bf, qg, kg, dqn, dkn, dc, T, D, "fox_prep_bwd")
    dproj0 = jnp.concatenate([dq0, dk0, dv.astype(BF16), dgate, dfl], axis=1)
    grads["fox_w_in"] = _mm(hn0, dproj0, "tn", F32, "fox_in_dw")[:, :4 * D + n_heads][None]
    dhn0 = _mm(dproj0, w_fin, "nt", F32, "fox_in_dx")
    dh0, _, d_an0 = _rms_bwd(h0, row(small["attn_norm"][0]), dhn0, dh1, T, "rms0_bwd")

    grads["meta_tokens"] = dh0[N_PAD:HEAD_ROWS]
    grads["attn_norm"] = jnp.concatenate([d_an0, d_an1], axis=0)
    grads["ffn_norm"] = jnp.concatenate([d_fn0, d_fn1], axis=0)
    grads["final_norm"] = d_final[0]
    grads["fox_b_f"] = sm[0:1, :n_heads]
    grads["fox_q_norm"] = sm[1:2, :FOX_DH] + sm[1:2, FOX_DH:]
    grads["fox_k_norm"] = sm[2:3, :FOX_DH] + sm[2:3, FOX_DH:]
    grads["hgrn_lower_bounds"] = d_lb[0:2]
    grads["hgrn_g_norm"] = d_gg[0:1]
    return loss_blk[0, 0], dh0[HEAD_ROWS:], grads, dict(zip(LATE, recv))


def kernel(x, meta_tokens, attn_norm, ffn_norm, final_norm, fox_w_in, fox_b_f, fox_q_norm, fox_k_norm, fox_w_out, hgrn_w_in, hgrn_lower_bounds, hgrn_g_norm, hgrn_w_out, ffn_w_in, ffn_w_out, loss_target, m_meta_tokens, m_attn_norm, m_ffn_norm, m_final_norm, m_fox_w_in, m_fox_b_f, m_fox_q_norm, m_fox_k_norm, m_fox_w_out, m_hgrn_w_in, m_hgrn_lower_bounds, m_hgrn_g_norm, m_hgrn_w_out, m_ffn_w_in, m_ffn_w_out, v_meta_tokens, v_attn_norm, v_ffn_norm, v_final_norm, v_fox_w_in, v_fox_b_f, v_fox_q_norm, v_fox_k_norm, v_fox_w_out, v_hgrn_w_in, v_hgrn_lower_bounds, v_hgrn_g_norm, v_hgrn_w_out, v_ffn_w_in, v_ffn_w_out):
    w = dict(meta_tokens=meta_tokens, attn_norm=attn_norm, ffn_norm=ffn_norm, final_norm=final_norm,
             fox_w_in=fox_w_in, fox_b_f=fox_b_f, fox_q_norm=fox_q_norm, fox_k_norm=fox_k_norm,
             fox_w_out=fox_w_out, hgrn_w_in=hgrn_w_in, hgrn_lower_bounds=hgrn_lower_bounds,
             hgrn_g_norm=hgrn_g_norm, hgrn_w_out=hgrn_w_out, ffn_w_in=ffn_w_in, ffn_w_out=ffn_w_out)
    m = dict(meta_tokens=m_meta_tokens, attn_norm=m_attn_norm, ffn_norm=m_ffn_norm, final_norm=m_final_norm,
             fox_w_in=m_fox_w_in, fox_b_f=m_fox_b_f, fox_q_norm=m_fox_q_norm, fox_k_norm=m_fox_k_norm,
             fox_w_out=m_fox_w_out, hgrn_w_in=m_hgrn_w_in, hgrn_lower_bounds=m_hgrn_lower_bounds,
             hgrn_g_norm=m_hgrn_g_norm, hgrn_w_out=m_hgrn_w_out, ffn_w_in=m_ffn_w_in, ffn_w_out=m_ffn_w_out)
    v = dict(meta_tokens=v_meta_tokens, attn_norm=v_attn_norm, ffn_norm=v_ffn_norm, final_norm=v_final_norm,
             fox_w_in=v_fox_w_in, fox_b_f=v_fox_b_f, fox_q_norm=v_fox_q_norm, fox_k_norm=v_fox_k_norm,
             fox_w_out=v_fox_w_out, hgrn_w_in=v_hgrn_w_in, hgrn_lower_bounds=v_hgrn_lower_bounds,
             hgrn_g_norm=v_hgrn_g_norm, hgrn_w_out=v_hgrn_w_out, ffn_w_in=v_ffn_w_in, ffn_w_out=v_ffn_w_out)
    axes = ("x", "y", "c")
    small_shapes = [w[n].shape for n in SMALL]

    g_meta, g_fin, g_fout = _exchange([w["meta_tokens"].astype(F32), w["fox_w_in"].astype(BF16),
                                       w["fox_w_out"].astype(BF16)], [False] * 3, "gather_weights")
    loss_local, grad_x, grads, recv = _local_step(
        x[0], loss_target[0], _gathered_to_full(g_meta, "meta_tokens"), _gathered_to_full(g_fin, "fox_w_in"),
        _gathered_to_full(g_fout, "fox_w_out"), {n: w[n].astype(BF16) for n in LATE}, {n: w[n] for n in SMALL})
    loss = lax.psum(loss_local, axes)

    r_meta, r_fin, r_fout, r_small = _exchange(
        [_full_to_slabs(grads["meta_tokens"], "meta_tokens"), _full_to_slabs(grads["fox_w_in"], "fox_w_in").astype(BF16),
         _full_to_slabs(grads["fox_w_out"], "fox_w_out").astype(BF16), _pack_small([grads[n] for n in SMALL])],
        [True, True, True, False], "scatter_grads")
    recv.update(meta_tokens=r_meta, fox_w_in=r_fin, fox_w_out=r_fout)

    res = {n: _adamw(recv[n], w[n], m[n], v[n], "adamw_" + n) for n in BIG}
    sml = _adamw(r_small, _pack_small([w[n] for n in SMALL]), _pack_small([m[n] for n in SMALL]),
                 _pack_small([v[n] for n in SMALL]), "adamw_small")
    outs = []
    for k in range(4):
        d = {n: res[n][k] for n in BIG}
        d.update(zip(SMALL, _unpack_small(sml[k], small_shapes)))
        outs.extend(d[n] for n in WEIGHTS)
    return (loss, grad_x[None], *outs)
```

```python
import functools
import math

import jax
import jax.numpy as jnp
from jax import lax
from jax.experimental import pallas as pl
from jax.experimental.pallas import tpu as pltpu

F32 = jnp.float32
BF16 = jnp.bfloat16
EPS = 1e-6
N_META = 16
LANES = 128
HEAD_ROWS = 256
ROW_TILE = 768
N_PAD = HEAD_ROWS - N_META
FOX_DH = 64
HGRN_CHUNK = 64
N_DEV = 8
NEG = -1e30
VMEM_LIMIT = 56 * 1024 * 1024
HI = lax.Precision.HIGHEST

ADAM_LR = 0.001
ADAM_B1 = 0.9
ADAM_B2 = 0.999
ADAM_EPS = 1e-08
ADAM_WD = 0.01
ADAM_STEP = 10

BIG = ("meta_tokens", "fox_w_in", "fox_w_out", "hgrn_w_in", "hgrn_w_out", "ffn_w_in", "ffn_w_out")
SMALL = ("attn_norm", "ffn_norm", "final_norm", "fox_b_f", "fox_q_norm", "fox_k_norm",
         "hgrn_lower_bounds", "hgrn_g_norm")
WEIGHTS = ("meta_tokens", "attn_norm", "ffn_norm", "final_norm", "fox_w_in", "fox_b_f", "fox_q_norm",
           "fox_k_norm", "fox_w_out", "hgrn_w_in", "hgrn_lower_bounds", "hgrn_g_norm", "hgrn_w_out",
           "ffn_w_in", "ffn_w_out")
COL_SHARDED = ("meta_tokens", "fox_w_in", "hgrn_w_in", "ffn_w_in")
LATE = ("hgrn_w_in", "hgrn_w_out", "ffn_w_in", "ffn_w_out")


def _params(sem=None):
    return pltpu.CompilerParams(dimension_semantics=sem, vmem_limit_bytes=VMEM_LIMIT)


def _tile(n, cap):
    best = None
    for t in range(LANES, min(n, cap) + 1, LANES):
        if n % t == 0:
            best = t
    assert best is not None, (n, cap)
    return best


def _row_chunk(n, cap, mult=8):
    best = n
    for t in range(mult, min(n, cap) + 1, mult):
        if n % t == 0:
            best = t
    return best


def _dg(a, b, ca, cb):
    return lax.dot_general(a.astype(BF16), b.astype(BF16), (((ca,), (cb,)), ((), ())),
                           preferred_element_type=F32)


@jax.custom_vjp
def _d_nn(a, b):
    return _dg(a, b, 1, 0)


@jax.custom_vjp
def _d_nt(a, b):
    return _dg(a, b, 1, 1)


@jax.custom_vjp
def _d_tn(a, b):
    return _dg(a, b, 0, 0)


_d_nn.defvjp(lambda a, b: (_d_nn(a, b), (a, b)), lambda r, g: (_d_nt(g, r[1]), _d_tn(r[0], g)))
_d_nt.defvjp(lambda a, b: (_d_nt(a, b), (a, b)), lambda r, g: (_d_nn(g, r[1]), _d_tn(g, r[0])))
_d_tn.defvjp(lambda a, b: (_d_tn(a, b), (a, b)), lambda r, g: (_d_nt(r[1], g), _d_nn(r[0], g)))


def _log_sigmoid(x):
    return jnp.minimum(x, 0.0) - jnp.log1p(jnp.exp(-jnp.abs(x)))


def _rms(x, g):
    return x * lax.rsqrt(jnp.mean(x * x, axis=-1, keepdims=True) + EPS) * g


def _mm(a, b, mode, out_dtype, name, res=None, tm=None, tn=None, tk=None):
    assert a.dtype == BF16 and b.dtype == BF16, (name, a.dtype, b.dtype)
    if mode == "nn":
        (M, K), N = a.shape, b.shape[1]
    elif mode == "nt":
        (M, K), N = a.shape, b.shape[0]
    else:
        (K, M), N = a.shape, b.shape[1]
    tm = tm or _tile(M, 1408 if mode == "tn" else ROW_TILE)
    tn = tn or _tile(N, 1408)
    tk = tk or _tile(K, ROW_TILE if mode == "tn" else 1408)
    nk = K // tk
    if mode == "tn":
        a_spec = pl.BlockSpec((tk, tm), lambda i, j, k: (k, i))
        dims = (((0,), (0,)), ((), ()))
    else:
        a_spec = pl.BlockSpec((tm, tk), lambda i, j, k: (i, k))
        dims = (((1,), (1 if mode == "nt" else 0,)), ((), ()))
    if mode == "nt":
        b_spec = pl.BlockSpec((tn, tk), lambda i, j, k: (j, k))
    else:
        b_spec = pl.BlockSpec((tk, tn), lambda i, j, k: (k, j))

    o_spec = pl.BlockSpec((tm, tn), lambda i, j, k: (i, j))

    def body(a_ref, b_ref, *rest):
        r_ref = rest[0] if res is not None else None
        o_ref, acc_ref = rest[-2:]
        k = pl.program_id(2)

        @pl.when(k == 0)
        def _():
            acc_ref[...] = jnp.zeros_like(acc_ref)

        acc_ref[...] += lax.dot_general(a_ref[...], b_ref[...], dims, preferred_element_type=F32)

        @pl.when(k == nk - 1)
        def _():
            out = acc_ref[...] if r_ref is None else acc_ref[...] + r_ref[...]
            o_ref[...] = out.astype(out_dtype)

    return pl.pallas_call(
        body, name=name, grid=(M // tm, N // tn, nk),
        in_specs=[a_spec, b_spec] + ([o_spec] if res is not None else []),
        out_specs=o_spec,
        out_shape=jax.ShapeDtypeStruct((M, N), out_dtype),
        scratch_shapes=[pltpu.VMEM((tm, tn), F32)],
        compiler_params=_params(("parallel", "parallel", "arbitrary")),
    )(a, b, *([res] if res is not None else []))


def _rms_fwd(x, g, T, name):
    Lp, D = x.shape

    def body(x_ref, g_ref, o_ref):
        o_ref[...] = _rms(x_ref[...], g_ref[...]).astype(BF16)

    return pl.pallas_call(
        body, name=name, grid=(Lp // T,),
        in_specs=[pl.BlockSpec((T, D), lambda i: (i, 0)), pl.BlockSpec((1, D), lambda i: (0, 0))],
        out_specs=pl.BlockSpec((T, D), lambda i: (i, 0)),
        out_shape=jax.ShapeDtypeStruct((Lp, D), BF16),
        compiler_params=_params(("parallel",)),
    )(x, g)


def _rms_bwd(x, g, dy, dres, T, name):
    Lp, D = x.shape

    def body(x_ref, g_ref, dy_ref, dr_ref, dx_ref, dxb_ref, dg_ref):
        @pl.when(pl.program_id(0) == 0)
        def _():
            dg_ref[...] = jnp.zeros_like(dg_ref)

        _, vjp = jax.vjp(_rms, x_ref[...], g_ref[...])
        dx, dg = vjp(dy_ref[...])
        dx = dx + dr_ref[...]
        dx_ref[...] = dx
        dxb_ref[...] = dx.astype(BF16)
        dg_ref[...] += dg

    row = pl.BlockSpec((T, D), lambda i: (i, 0))
    vec = pl.BlockSpec((1, D), lambda i: (0, 0))
    return pl.pallas_call(
        body, name=name, grid=(Lp // T,),
        in_specs=[row, vec, row, row],
        out_specs=[row, row, vec],
        out_shape=[jax.ShapeDtypeStruct((Lp, D), F32), jax.ShapeDtypeStruct((Lp, D), BF16),
                   jax.ShapeDtypeStruct((1, D), F32)],
        compiler_params=_params(("arbitrary",)),
    )(x, g, dy, dres)


def _swiglu(gate, up):
    return gate * jax.nn.sigmoid(gate) * up


def _ffn_in_fwd(hf, w_in, name):
    Lp, D = hf.shape
    F = w_in.shape[1] // 2
    tm = _tile(Lp, ROW_TILE)
    tn = _tile(F, 1408)
    nj = F // tn

    def body(a_ref, bg_ref, bu_ref, g_ref, u_ref, act_ref):
        a = a_ref[...]
        g = jnp.dot(a, bg_ref[...], preferred_element_type=F32)
        u = jnp.dot(a, bu_ref[...], preferred_element_type=F32)
        g_ref[...] = g
        u_ref[...] = u
        act_ref[...] = _swiglu(g, u).astype(BF16)

    tile = pl.BlockSpec((tm, tn), lambda i, j: (i, j))
    return pl.pallas_call(
        body, name=name, grid=(Lp // tm, nj),
        in_specs=[pl.BlockSpec((tm, D), lambda i, j: (i, 0)), pl.BlockSpec((D, tn), lambda i, j: (0, j)),
                  pl.BlockSpec((D, tn), lambda i, j: (0, nj + j))],
        out_specs=[tile, tile, tile],
        out_shape=[jax.ShapeDtypeStruct((Lp, F), F32), jax.ShapeDtypeStruct((Lp, F), F32),
                   jax.ShapeDtypeStruct((Lp, F), BF16)],
        compiler_params=_params(("parallel", "parallel")),
    )(hf, w_in, w_in)


def _ffn_out_dx(dhb, w_out, g, u, name):
    Lp, D = dhb.shape
    F = w_out.shape[0]
    tm = HEAD_ROWS

    def body(a_ref, b_ref, g_ref, u_ref, o_ref):
        dact = lax.dot_general(a_ref[...], b_ref[...], (((1,), (1,)), ((), ())), preferred_element_type=F32)
        _, vjp = jax.vjp(_swiglu, g_ref[...], u_ref[...])
        dg, du = vjp(dact)
        o_ref[:, :F] = dg.astype(BF16)
        o_ref[:, F:] = du.astype(BF16)

    wide = pl.BlockSpec((tm, F), lambda i: (i, 0))
    return pl.pallas_call(
        body, name=name, grid=(Lp // tm,),
        in_specs=[pl.BlockSpec((tm, D), lambda i: (i, 0)), pl.BlockSpec((F, D), lambda i: (0, 0)), wide, wide],
        out_specs=pl.BlockSpec((tm, 2 * F), lambda i: (i, 0)),
        out_shape=jax.ShapeDtypeStruct((Lp, 2 * F), BF16),
        compiler_params=_params(("parallel",)),
    )(dhb, w_out, g, u)


def _final_loss(h, g, target, name):
    Lp, D = h.shape
    TR = HEAD_ROWS

    def loss_fn(hh, gg, tt):
        err = _rms(hh, gg) - tt
        return 0.5 * jnp.sum(jnp.mean(err * err, axis=-1))

    def body(h_ref, g_ref, t_ref, loss_ref, dh_ref, dhb_ref, dg_ref):
        i = pl.program_id(0)

        @pl.when(i == 0)
        def _():
            loss_ref[...] = jnp.zeros_like(loss_ref)
            dg_ref[...] = jnp.zeros_like(dg_ref)
            dh_ref[...] = jnp.zeros_like(dh_ref)
            dhb_ref[...] = jnp.zeros_like(dhb_ref)

        @pl.when(i > 0)
        def _():
            val, vjp = jax.vjp(lambda hh, gg: loss_fn(hh, gg, t_ref[...]), h_ref[...], g_ref[...])
            dh, dg = vjp(jnp.ones((), F32))
            dh_ref[...] = dh
            dhb_ref[...] = dh.astype(BF16)
            dg_ref[...] += dg
            loss_ref[...] += val

    row = pl.BlockSpec((TR, D), lambda i: (i, 0))
    return pl.pallas_call(
        body, name=name, grid=(Lp // TR,),
        in_specs=[row, pl.BlockSpec((1, D), lambda i: (0, 0)),
                  pl.BlockSpec((TR, D), lambda i: (jnp.maximum(i - 1, 0), 0))],
        out_specs=[pl.BlockSpec((8, LANES), lambda i: (0, 0)), row, row, pl.BlockSpec((1, D), lambda i: (0, 0))],
        out_shape=[jax.ShapeDtypeStruct((8, LANES), F32), jax.ShapeDtypeStruct((Lp, D), F32),
                   jax.ShapeDtypeStruct((Lp, D), BF16), jax.ShapeDtypeStruct((1, D), F32)],
        compiler_params=_params(("arbitrary",)),
    )(h, g, target)


def _lane_lo():
    return lax.broadcasted_iota(jnp.int32, (1, LANES), 1) < FOX_DH


def _headnorm(x, g, scale):
    lo = _lane_lo()
    x2 = x * x
    s0 = jnp.sum(jnp.where(lo, x2, 0.0), axis=-1, keepdims=True)
    s1 = jnp.sum(jnp.where(lo, 0.0, x2), axis=-1, keepdims=True)
    r = jnp.where(lo, lax.rsqrt(s0 / FOX_DH + EPS), lax.rsqrt(s1 / FOX_DH + EPS))
    return x * r * g * scale


def _fox_prep_fwd(proj, bf, qg, kg, T, D, name):
    Lp = proj.shape[0]
    nb = D // LANES
    scale = FOX_DH ** -0.5

    def body(q_ref, k_ref, v_ref, fl_ref, bf_ref, qg_ref, kg_ref, qn_ref, kn_ref, vb_ref, c_ref, carry_ref):
        @pl.when(pl.program_id(0) == 0)
        def _():
            carry_ref[...] = jnp.zeros_like(carry_ref)

        for b in range(nb):
            sl = slice(b * LANES, (b + 1) * LANES)
            qn_ref[:, sl] = _headnorm(q_ref[:, sl], qg_ref[...], scale).astype(BF16)
            kn_ref[:, sl] = _headnorm(k_ref[:, sl], kg_ref[...], 1.0).astype(BF16)
        vb_ref[...] = v_ref[...].astype(BF16)
        log_f = _log_sigmoid(fl_ref[...] + bf_ref[...])
        row = lax.broadcasted_iota(jnp.int32, (T, T), 0)
        col = lax.broadcasted_iota(jnp.int32, (T, T), 1)
        tri = (col <= row).astype(F32)
        c = jnp.dot(tri, log_f, precision=HI, preferred_element_type=F32) + carry_ref[...]
        c_ref[...] = c
        last = lax.broadcasted_iota(jnp.int32, (T, 1), 0) == T - 1
        carry_ref[...] = jnp.sum(jnp.where(last, c, 0.0), axis=0, keepdims=True)

    wide = lambda j: pl.BlockSpec((T, D), lambda i: (i, j))
    vec = pl.BlockSpec((1, LANES), lambda i: (0, 0))
    return pl.pallas_call(
        body, name=name, grid=(Lp // T,),
        in_specs=[wide(0), wide(1), wide(2), pl.BlockSpec((T, LANES), lambda i: (i, 4 * nb)), vec, vec, vec],
        out_specs=[wide(0), wide(0), wide(0), pl.BlockSpec((T, LANES), lambda i: (i, 0))],
        out_shape=[jax.ShapeDtypeStruct((Lp, D), BF16)] * 3 + [jax.ShapeDtypeStruct((Lp, LANES), F32)],
        scratch_shapes=[pltpu.VMEM((1, LANES), F32)],
        compiler_params=_params(("arbitrary",)),
    )(proj, proj, proj, proj, bf, qg, kg)


def _fox_prep_bwd(proj, bf, qg, kg, dqn, dkn, dc, T, D, name):
    Lp = proj.shape[0]
    nb = D // LANES
    nt = Lp // T
    scale = FOX_DH ** -0.5

    def body(q_ref, k_ref, fl_ref, bf_ref, qg_ref, kg_ref, dqn_ref, dkn_ref, dc_ref,
             dq_ref, dk_ref, dfl_ref, sm_ref, carry_ref):
        @pl.when(pl.program_id(0) == 0)
        def _():
            carry_ref[...] = jnp.zeros_like(carry_ref)
            sm_ref[...] = jnp.zeros_like(sm_ref)

        dqg = jnp.zeros((1, LANES), F32)
        dkg = jnp.zeros((1, LANES), F32)
        for b in range(nb):
            sl = slice(b * LANES, (b + 1) * LANES)
            _, vjp = jax.vjp(lambda x, g: _headnorm(x, g, scale), q_ref[:, sl], qg_ref[...])
            dx, dg = vjp(dqn_ref[:, sl])
            dq_ref[:, sl] = dx.astype(BF16)
            dqg = dqg + dg
            _, vjp = jax.vjp(lambda x, g: _headnorm(x, g, 1.0), k_ref[:, sl], kg_ref[...])
            dx, dg = vjp(dkn_ref[:, sl])
            dk_ref[:, sl] = dx.astype(BF16)
            dkg = dkg + dg
        dcv = dc_ref[...]
        row = lax.broadcasted_iota(jnp.int32, (T, T), 0)
        col = lax.broadcasted_iota(jnp.int32, (T, T), 1)
        triu = (col >= row).astype(F32)
        dlogf = jnp.dot(triu, dcv, precision=HI, preferred_element_type=F32) + carry_ref[...]
        carry_ref[...] += jnp.sum(dcv, axis=0, keepdims=True)
        _, vjp = jax.vjp(_log_sigmoid, fl_ref[...] + bf_ref[...])
        (dfl,) = vjp(dlogf)
        dfl_ref[...] = dfl.astype(BF16)
        sm_ref[0:1, :] += jnp.sum(dfl, axis=0, keepdims=True)
        sm_ref[1:2, :] += dqg
        sm_ref[2:3, :] += dkg

    wide = lambda j: pl.BlockSpec((T, D), lambda i: (nt - 1 - i, j))
    narrow = lambda j: pl.BlockSpec((T, LANES), lambda i: (nt - 1 - i, j))
    vec = pl.BlockSpec((1, LANES), lambda i: (0, 0))
    return pl.pallas_call(
        body, name=name, grid=(nt,),
        in_specs=[wide(0), wide(1), narrow(4 * nb), vec, vec, vec, wide(0), wide(0), narrow(0)],
        out_specs=[wide(0), wide(0), narrow(0), pl.BlockSpec((8, LANES), lambda i: (0, 0))],
        out_shape=[jax.ShapeDtypeStruct((Lp, D), BF16)] * 2 + [jax.ShapeDtypeStruct((Lp, LANES), BF16),
                                                                 jax.ShapeDtypeStruct((8, LANES), F32)],
        scratch_shapes=[pltpu.VMEM((1, LANES), F32)],
        compiler_params=_params(("arbitrary",)),
    )(proj, proj, proj, bf, qg, kg, dqn, dkn, dc)


def _ln2_ceil(m):
    return jnp.ceil(m * (1.0 / math.log(2.0))) * math.log(2.0)


def _fox_mask(i, k0, T):
    qpos = i * T + lax.broadcasted_iota(jnp.int32, (T, 1), 0)
    kpos = k0 + lax.broadcasted_iota(jnp.int32, (1, T), 1)
    return (kpos <= qpos) & ((kpos >= N_PAD) | (qpos < N_PAD))


def _pick_col(blk, idx):
    lane = lax.broadcasted_iota(jnp.int32, (1, LANES), 1)
    return jnp.sum(jnp.where(lane == idx, blk, 0.0), axis=1, keepdims=True)


def _split_halves(blk):
    lo = _lane_lo()
    return (jnp.max(jnp.where(lo, blk, -jnp.inf), axis=1, keepdims=True),
            jnp.max(jnp.where(lo, -jnp.inf, blk), axis=1, keepdims=True))


def _fox_attn_fwd(qn, kn, vb, c, cT, proj, xchg, T, D, name):
    Lp = qn.shape[0]
    P = D // LANES
    nt = Lp // T
    H = cT.shape[0]
    nx = len(xchg)

    def body(q_ref, k_ref, v_ref, c_ref, cT_ref, g_ref, *rest):
        x_in, (o_ref, og_ref, m_ref, li_ref), x_out, sems = rest[:nx], rest[nx:nx + 4], rest[nx + 4:2 * nx + 4], rest[2 * nx + 4:]
        p = pl.program_id(0)
        i = pl.program_id(1)

        @pl.when((p == 0) & (i == 0))
        def _():
            for cp in _xchg_copies(x_in, x_out, [False] * nx, sems):
                cp.start()

        lo = _lane_lo()
        q = q_ref[...]
        zero = jnp.zeros_like(q)
        qh = (jnp.where(lo, q, zero), jnp.where(lo, zero, q))
        cblk = c_ref[...]
        cq = tuple(_pick_col(cblk, 2 * p + h) for h in (0, 1))
        one = jnp.ones_like(q)

        def step(j, carry, masked):
            k0 = pl.multiple_of(j * T, LANES)
            kj = k_ref[pl.ds(k0, T), :]
            vj = v_ref[pl.ds(k0, T), :]
            vh = (jnp.where(lo, vj, one), jnp.where(lo, one, vj))
            mask = _fox_mask(i, k0, T) if masked else None
            out = []
            for h in (0, 1):
                m, acc = carry[h]
                ck = cT_ref[pl.ds(2 * p + h, 1), pl.ds(k0, T)]
                t = lax.dot_general(qh[h], kj, (((1,), (1,)), ((), ())), preferred_element_type=F32) - ck
                if masked:
                    t = jnp.where(mask, t, NEG)
                m_new = _ln2_ceil(jnp.maximum(m, cq[h] + jnp.max(t, axis=1, keepdims=True)))
                pr = jnp.exp(t + (cq[h] - m_new)).astype(BF16)
                acc = jnp.exp(m - m_new) * acc + jnp.dot(pr, vh[h], preferred_element_type=F32)
                out.append((m_new, acc))
            return tuple(out)

        init = tuple((jnp.full((T, 1), NEG, F32), jnp.zeros((T, LANES), F32)) for _ in (0, 1))
        carry = step(0, init, True)
        carry = lax.fori_loop(1, i, lambda j, cr: step(j, cr, False), carry)
        (m0, a0), (m1, a1) = lax.cond(i > 0, lambda cr: step(i, cr, True), lambda cr: cr, carry)
        l0 = pltpu.roll(a0, FOX_DH, 1)
        l1 = pltpu.roll(a1, FOX_DH, 1)
        o = jnp.where(lo, a0 / l0, a1 / l1)
        o_ref[...] = o
        m_ref[...] = jnp.where(lo, m0, m1)
        li_ref[...] = jnp.where(lo, 1.0 / l0, 1.0 / l1)
        og_ref[...] = (o * jax.nn.sigmoid(g_ref[...])).astype(BF16)

        @pl.when((p == P - 1) & (i == nt - 1))
        def _():
            for cp in _xchg_copies(x_in, x_out, [False] * nx, sems):
                cp.wait()

    tile = pl.BlockSpec((T, LANES), lambda p, i: (i, p))
    full = pl.BlockSpec((Lp, LANES), lambda p, i: (0, p))
    HBM = pl.BlockSpec(memory_space=pltpu.HBM)
    return pl.pallas_call(
        body, name=name, grid=(P, nt),
        in_specs=[tile, full, full, pl.BlockSpec((T, LANES), lambda p, i: (i, 0)),
                  pl.BlockSpec((H, Lp), lambda p, i: (0, 0)),
                  pl.BlockSpec((T, LANES), lambda p, i: (i, 3 * P + p))] + [HBM] * nx,
        out_specs=[tile, tile, tile, tile] + [HBM] * nx,
        out_shape=[jax.ShapeDtypeStruct((Lp, D), F32), jax.ShapeDtypeStruct((Lp, D), BF16),
                   jax.ShapeDtypeStruct((Lp, D), F32), jax.ShapeDtypeStruct((Lp, D), F32)]
        + _xchg_out_shapes(xchg, [False] * nx),
        scratch_shapes=_xchg_sems(nx),
        compiler_params=_params(("arbitrary", "arbitrary")),
    )(qn, kn, vb, c, cT, proj, *xchg)


def _fox_gate_bwd(dog, o, proj, linv, T, D, name):
    Lp = o.shape[0]
    P = D // LANES

    def body(dog_ref, o_ref, g_ref, li_ref, do_ref, dg_ref, dl_ref):
        lo = _lane_lo()
        sig = jax.nn.sigmoid(g_ref[...])
        ov = o_ref[...]
        do = (dog_ref[...] * sig * li_ref[...]).astype(BF16)
        do_ref[...] = do
        dg_ref[...] = (dog_ref[...] * ov * sig * (1.0 - sig)).astype(BF16)
        t = do.astype(F32) * ov
        d0 = jnp.sum(jnp.where(lo, t, 0.0), axis=1, keepdims=True)
        d1 = jnp.sum(jnp.where(lo, 0.0, t), axis=1, keepdims=True)
        dl_ref[...] = jnp.where(lo, d0, d1)

    tile = pl.BlockSpec((T, LANES), lambda i, p: (i, p))
    return pl.pallas_call(
        body, name=name, grid=(Lp // T, P),
        in_specs=[tile, tile, pl.BlockSpec((T, LANES), lambda i, p: (i, 3 * P + p)), tile],
        out_specs=[tile, tile, tile],
        out_shape=[jax.ShapeDtypeStruct((Lp, D), BF16), jax.ShapeDtypeStruct((Lp, D), BF16),
                   jax.ShapeDtypeStruct((Lp, D), F32)],
        compiler_params=_params(("parallel", "parallel")),
    )(dog, o, proj, linv)


def _fox_attn_bwd(qn, kn, vb, c, cT, do, mshift, delta, xchg, T, D, name):
    Lp = qn.shape[0]
    P = D // LANES
    nt = Lp // T
    H = cT.shape[0]
    nx = len(xchg)

    def body(q_ref, do_ref, m_ref, dl_ref, c_ref, k_ref, v_ref, cT_ref, *rest):
        x_in, (dq_ref, dk_ref, dv_ref, dc_ref), x_out, sems = rest[:nx], rest[nx:nx + 4], rest[nx + 4:2 * nx + 4], rest[2 * nx + 4:]
        p = pl.program_id(0)
        i = pl.program_id(1)

        @pl.when((p == 0) & (i == 0))
        def _():
            for cp in _xchg_copies(x_in, x_out, [True] * nx, sems):
                cp.start()

        @pl.when(i == 0)
        def _():
            dk_ref[...] = jnp.zeros_like(dk_ref)
            dv_ref[...] = jnp.zeros_like(dv_ref)
            dc_ref[...] = jnp.zeros_like(dc_ref)

        lo = _lane_lo()
        q = q_ref[...]
        do = do_ref[...]
        zero = jnp.zeros_like(q)
        qh = (jnp.where(lo, q, zero), jnp.where(lo, zero, q))
        doh = (jnp.where(lo, do, zero), jnp.where(lo, zero, do))
        msh = _split_halves(m_ref[...])
        dlt = _split_halves(dl_ref[...])
        cblk = c_ref[...]
        shift = tuple(_pick_col(cblk, 2 * p + h) - msh[h] for h in (0, 1))

        def step(j, carry, masked):
            k0 = pl.multiple_of(j * T, LANES)
            kj = k_ref[pl.ds(k0, T), :]
            vj = v_ref[pl.ds(k0, T), :]
            mask = _fox_mask(i, k0, T) if masked else None
            dqs, dks, dvs = [], [], []
            for h in (0, 1):
                ck = cT_ref[pl.ds(2 * p + h, 1), pl.ds(k0, T)]
                t = lax.dot_general(qh[h], kj, (((1,), (1,)), ((), ())), preferred_element_type=F32) - ck
                if masked:
                    t = jnp.where(mask, t, NEG)
                pb = jnp.exp(t + shift[h]).astype(BF16)
                dp = lax.dot_general(doh[h], vj, (((1,), (1,)), ((), ())), preferred_element_type=F32)
                ds = pb.astype(F32) * (dp - dlt[h])
                dsb = ds.astype(BF16)
                dqs.append(carry[h] + jnp.dot(dsb, kj, preferred_element_type=F32))
                dks.append(lax.dot_general(dsb, q, (((0,), (0,)), ((), ())), preferred_element_type=F32))
                dvs.append(lax.dot_general(pb, do, (((0,), (0,)), ((), ())), preferred_element_type=F32))
                dc_ref[0, h:h + 1, pl.ds(k0, T)] += -jnp.sum(ds, axis=0, keepdims=True)
            dk_ref[pl.ds(k0, T), :] += jnp.where(lo, dks[0], dks[1])
            dv_ref[pl.ds(k0, T), :] += jnp.where(lo, dvs[0], dvs[1])
            return tuple(dqs)

        init = (jnp.zeros((T, LANES), F32), jnp.zeros((T, LANES), F32))
        carry = step(0, init, True)
        carry = lax.fori_loop(1, i, lambda j, cr: step(j, cr, False), carry)
        dq0, dq1 = lax.cond(i > 0, lambda cr: step(i, cr, True), lambda cr: cr, carry)
        dq_ref[...] = jnp.where(lo, dq0, dq1)

        @pl.when((p == P - 1) & (i == nt - 1))
        def _():
            for cp in _xchg_copies(x_in, x_out, [True] * nx, sems):
                cp.wait()

    tile = pl.BlockSpec((T, LANES), lambda p, i: (i, p))
    full = pl.BlockSpec((Lp, LANES), lambda p, i: (0, p))
    HBM = pl.BlockSpec(memory_space=pltpu.HBM)
    return pl.pallas_call(
        body, name=name, grid=(P, nt),
        in_specs=[tile, tile, tile, tile, pl.BlockSpec((T, LANES), lambda p, i: (i, 0)), full, full,
                  pl.BlockSpec((H, Lp), lambda p, i: (0, 0))] + [HBM] * nx,
        out_specs=[tile, full, full, pl.BlockSpec((1, 8, Lp), lambda p, i: (p, 0, 0))] + [HBM] * nx,
        out_shape=[jax.ShapeDtypeStruct((Lp, D), F32)] * 3 + [jax.ShapeDtypeStruct((P, 8, Lp), F32)]
        + _xchg_out_shapes(xchg, [True] * nx),
        scratch_shapes=_xchg_sems(nx),
        compiler_params=_params(("arbitrary", "arbitrary")),
    )(qn, do, mshift, delta, c, kn, vb, cT, *xchg)


def _hgrn_chunk(St, qr, z, vi, go, p0, p1, gg):
    C = qr.shape[0]
    lb = jax.nn.sigmoid(p1 - p0)
    a = jnp.log(lb)
    cc = jnp.log1p(-lb) + _log_sigmoid(z)
    log_f = jnp.maximum(a, cc) + jnp.log1p(jnp.exp(-jnp.abs(a - cc)))
    k = (1.0 - lb) * jax.nn.sigmoid(-z)
    q = qr * jax.nn.sigmoid(qr)
    row = lax.broadcasted_iota(jnp.int32, (C, C), 0)
    col = lax.broadcasted_iota(jnp.int32, (C, C), 1)
    causal = col <= row
    b = jnp.dot(causal.astype(F32), log_f, precision=HI, preferred_element_type=F32)
    mid = lax.broadcasted_iota(jnp.int32, (C, 1), 0) == C // 2 - 1
    r = jnp.sum(jnp.where(mid, b, 0.0), axis=0, keepdims=True)
    b_last = jnp.sum(log_f, axis=0, keepdims=True)
    attn = jnp.where(causal, _d_nt(q * jnp.exp(b - r), k * jnp.exp(r - b)), 0.0)
    o = _d_nn(attn, vi) + _d_nt(q * jnp.exp(b), St)
    St_new = St * jnp.exp(b_last) + _d_tn(vi, k * jnp.exp(b_last - b))
    og = _rms(o, gg) * (go * jax.nn.sigmoid(go))
    return St_new, og


def _hgrn_heads_per_step(H):
    return 4 if H % 4 == 0 else 1


def _hgrn_specs(T, W, nhb, rev_nt=None):
    if rev_nt is None:
        return [pl.BlockSpec((T, W), functools.partial(lambda hb, t, g: (t, g * nhb + hb), g=g)) for g in range(4)]
    return [pl.BlockSpec((T, W), functools.partial(lambda hb, t, g: (rev_nt - 1 - t, g * nhb + hb), g=g))
            for g in range(4)]


def _hgrn_fwd(proj, lbp, gg, T, name):
    Lp = proj.shape[0]
    D = proj.shape[1] // 4
    H = D // LANES
    hps = _hgrn_heads_per_step(H)
    W = hps * LANES
    nhb = H // hps
    nt = Lp // T
    ncc = T // HGRN_CHUNK

    def body(q_ref, z_ref, i_ref, go_ref, p_ref, gg_ref, og_ref, ss_ref, st_ref):
        @pl.when(pl.program_id(1) == 0)
        def _():
            st_ref[...] = jnp.zeros_like(st_ref)

        gain = gg_ref[...]

        def chunk(cidx, states):
            sl = pl.ds(pl.multiple_of(cidx * HGRN_CHUNK, HGRN_CHUNK), HGRN_CHUNK)
            new = []
            for hh in range(hps):
                ln = slice(hh * LANES, (hh + 1) * LANES)
                ss_ref[hh, cidx] = states[hh]
                St_new, og = _hgrn_chunk(states[hh], q_ref[sl, ln], z_ref[sl, ln], i_ref[sl, ln], go_ref[sl, ln],
                                         p_ref[0:1, ln], p_ref[1:2, ln], gain)
                og_ref[sl, ln] = og.astype(BF16)
                new.append(St_new)
            return tuple(new)

        states = lax.fori_loop(0, ncc, chunk, tuple(st_ref[hh] for hh in range(hps)))
        for hh in range(hps):
            st_ref[hh] = states[hh]

    return pl.pallas_call(
        body, name=name, grid=(nhb, nt),
        in_specs=_hgrn_specs(T, W, nhb) + [pl.BlockSpec((2, W), lambda hb, t: (0, hb)),
                                           pl.BlockSpec((1, LANES), lambda hb, t: (0, 0))],
        out_specs=[pl.BlockSpec((T, W), lambda hb, t: (t, hb)),
                   pl.BlockSpec((hps, ncc, LANES, LANES), lambda hb, t: (hb, t, 0, 0))],
        out_shape=[jax.ShapeDtypeStruct((Lp, D), BF16),
                   jax.ShapeDtypeStruct((H, Lp // HGRN_CHUNK, LANES, LANES), F32)],
        scratch_shapes=[pltpu.VMEM((hps, LANES, LANES), F32)],
        compiler_params=_params(("parallel", "arbitrary")),
    )(proj, proj, proj, proj, lbp, gg)


def _hgrn_bwd(proj, lbp, gg, dog, ss, T, name):
    Lp = proj.shape[0]
    D = proj.shape[1] // 4
    H = D // LANES
    hps = _hgrn_heads_per_step(H)
    W = hps * LANES
    nhb = H // hps
    nt = Lp // T
    ncc = T // HGRN_CHUNK

    def body(q_ref, z_ref, i_ref, go_ref, p_ref, gg_ref, dog_ref, ss_ref,
             dq_ref, dz_ref, di_ref, dgo_ref, dp_ref, dgg_ref, dst_ref):
        hb = pl.program_id(0)
        t = pl.program_id(1)

        @pl.when(t == 0)
        def _():
            dst_ref[...] = jnp.zeros_like(dst_ref)
            dp_ref[...] = jnp.zeros_like(dp_ref)

        @pl.when((t == 0) & (hb == 0))
        def _():
            dgg_ref[...] = jnp.zeros_like(dgg_ref)

        gain = gg_ref[...]
        row0 = (nt - 1 - t) * T

        def chunk(cc, carry):
            dstates, dps, dgain_sum = carry
            cidx = ncc - 1 - cc
            r0 = pl.multiple_of(cidx * HGRN_CHUNK, HGRN_CHUNK)
            sl = pl.ds(r0, HGRN_CHUNK)
            real = (row0 + r0 + lax.broadcasted_iota(jnp.int32, (HGRN_CHUNK, 1), 0)) >= N_PAD
            new_d, new_p = [], []
            for hh in range(hps):
                ln = slice(hh * LANES, (hh + 1) * LANES)
                _, vjp = jax.vjp(_hgrn_chunk, ss_ref[hh, cidx], q_ref[sl, ln], z_ref[sl, ln], i_ref[sl, ln],
                                 go_ref[sl, ln], p_ref[0:1, ln], p_ref[1:2, ln], gain)
                dSt, dq, dz, di, dgo, dp0, dp1, dgain = vjp((dstates[hh], dog_ref[sl, ln]))
                dq_ref[sl, ln] = jnp.where(real, dq, 0.0).astype(BF16)
                dz_ref[sl, ln] = jnp.where(real, dz, 0.0).astype(BF16)
                di_ref[sl, ln] = jnp.where(real, di, 0.0).astype(BF16)
                dgo_ref[sl, ln] = jnp.where(real, dgo, 0.0).astype(BF16)
                new_d.append(dSt)
                new_p.append((dps[hh][0] + dp0, dps[hh][1] + dp1))
                dgain_sum = dgain_sum + dgain
            return tuple(new_d), tuple(new_p), dgain_sum

        zero_row = jnp.zeros((1, LANES), F32)
        init = (tuple(dst_ref[hh] for hh in range(hps)), tuple((zero_row, zero_row) for _ in range(hps)), zero_row)
        dstates, dps, dgain_sum = lax.fori_loop(0, ncc, chunk, init)
        for hh in range(hps):
            ln = slice(hh * LANES, (hh + 1) * LANES)
            dst_ref[hh] = dstates[hh]
            dp_ref[0:1, ln] += dps[hh][0]
            dp_ref[1:2, ln] += dps[hh][1]
        dgg_ref[0:1, :] += dgain_sum

    rev = pl.BlockSpec((T, W), lambda hb, t: (nt - 1 - t, hb))
    return pl.pallas_call(
        body, name=name, grid=(nhb, nt),
        in_specs=_hgrn_specs(T, W, nhb, nt) + [pl.BlockSpec((2, W), lambda hb, t: (0, hb)),
                                               pl.BlockSpec((1, LANES), lambda hb, t: (0, 0)), rev,
                                               pl.BlockSpec((hps, ncc, LANES, LANES),
                                                            lambda hb, t: (hb, nt - 1 - t, 0, 0))],
        out_specs=[rev, rev, rev, rev, pl.BlockSpec((8, W), lambda hb, t: (0, hb)),
                   pl.BlockSpec((8, LANES), lambda hb, t: (0, 0))],
        out_shape=[jax.ShapeDtypeStruct((Lp, D), BF16)] * 4 + [jax.ShapeDtypeStruct((8, D), F32),
                                                                 jax.ShapeDtypeStruct((8, LANES), F32)],
        scratch_shapes=[pltpu.VMEM((hps, LANES, LANES), F32)],
        compiler_params=_params(("arbitrary", "arbitrary")),
    )(proj, proj, proj, proj, lbp, gg, dog, ss)


def _xchg_sems(n_arr):
    return [pltpu.SemaphoreType.DMA((n_arr * (N_DEV - 1),)), pltpu.SemaphoreType.DMA((n_arr * (N_DEV - 1),)),
            pltpu.SemaphoreType.DMA((n_arr,))]


def _xchg_copies(ins, outs, per_peer, sems):
    send_sems, recv_sems, local_sems = sems
    x, y, c = lax.axis_index("x"), lax.axis_index("y"), lax.axis_index("c")
    me = 4 * x + 2 * y + c
    copies = []
    for n in range(len(ins)):
        src = ins[n].at[me] if per_peer[n] else ins[n]
        copies.append(pltpu.make_async_copy(src, outs[n].at[me], local_sems.at[n]))
    for rel in range(1, N_DEV):
        fx, fy, fc = (rel >> 2) & 1, (rel >> 1) & 1, rel & 1
        px = 1 - x if fx else x
        py = 1 - y if fy else y
        pc = 1 - c if fc else c
        peer = 4 * px + 2 * py + pc
        for n in range(len(ins)):
            src = ins[n].at[peer] if per_peer[n] else ins[n]
            copies.append(pltpu.make_async_remote_copy(
                src_ref=src, dst_ref=outs[n].at[me],
                send_sem=send_sems.at[n * (N_DEV - 1) + rel - 1],
                recv_sem=recv_sems.at[n * (N_DEV - 1) + rel - 1],
                device_id=(px, py, pc), device_id_type=pl.DeviceIdType.MESH))
    return copies


def _xchg_out_shapes(arrays, per_peer):
    return [jax.ShapeDtypeStruct(a.shape if pp else (N_DEV,) + a.shape, a.dtype) for a, pp in zip(arrays, per_peer)]


def _exchange(arrays, per_peer, name):
    n_arr = len(arrays)
    HBM = pl.BlockSpec(memory_space=pltpu.HBM)

    def body(*refs):
        copies = _xchg_copies(refs[:n_arr], refs[n_arr:2 * n_arr], per_peer, refs[2 * n_arr:])
        for cp in copies:
            cp.start()
        for cp in copies:
            cp.wait()

    return pl.pallas_call(
        body, name=name,
        in_specs=[HBM] * n_arr, out_specs=[HBM] * n_arr, out_shape=_xchg_out_shapes(arrays, per_peer),
        scratch_shapes=_xchg_sems(n_arr),
    )(*arrays)


ADAMW_VMEM_BUDGET = 36 * 1024 * 1024


def _adamw(recv, w, m, v, name):
    shape = w.shape
    C = shape[-1]
    R = math.prod(shape[:-1])
    lanes = -(-C // LANES) * LANES
    row_bytes = 2 * lanes * (N_DEV * recv.dtype.itemsize + 7 * 4)
    rc = _row_chunk(R, max(16, ADAMW_VMEM_BUDGET // row_bytes), 16 if recv.dtype == BF16 else 8)

    def body(r_ref, w_ref, m_ref, v_ref, g_ref, d_ref, mo_ref, vo_ref):
        g = r_ref[0].astype(F32)
        for s in range(1, N_DEV):
            g = g + r_ref[s].astype(F32)
        mn = ADAM_B1 * m_ref[...] + (1.0 - ADAM_B1) * g
        vn = ADAM_B2 * v_ref[...] + (1.0 - ADAM_B2) * (g * g)
        m_hat = mn / (1.0 - ADAM_B1 ** ADAM_STEP)
        v_hat = vn / (1.0 - ADAM_B2 ** ADAM_STEP)
        g_ref[...] = g
        d_ref[...] = -ADAM_LR * (m_hat / (jnp.sqrt(v_hat) + ADAM_EPS) + ADAM_WD * w_ref[...])
        mo_ref[...] = mn
        vo_ref[...] = vn

    row = pl.BlockSpec((rc, C), lambda i: (i, 0))
    outs = pl.pallas_call(
        body, name=name, grid=(R // rc,),
        in_specs=[pl.BlockSpec((N_DEV, rc, C), lambda i: (0, i, 0)), row, row, row],
        out_specs=[row] * 4,
        out_shape=[jax.ShapeDtypeStruct((R, C), F32)] * 4,
        compiler_params=_params(("parallel",)),
    )(recv.reshape(N_DEV, R, C), w.reshape(R, C), m.reshape(R, C), v.reshape(R, C))
    return [o.reshape(shape) for o in outs]


def _gathered_to_full(g, name):
    if name in COL_SHARDED:
        g = jnp.moveaxis(g, 0, -2)
        return g.reshape(g.shape[:-2] + (g.shape[-2] * g.shape[-1],))
    g = jnp.moveaxis(g, 0, -3)
    return g.reshape(g.shape[:-3] + (g.shape[-3] * g.shape[-2], g.shape[-1]))


def _full_to_slabs(full, name):
    if name in COL_SHARDED:
        f = full.reshape(full.shape[:-1] + (N_DEV, full.shape[-1] // N_DEV))
        return jnp.moveaxis(f, -2, 0)
    f = full.reshape(full.shape[:-2] + (N_DEV, full.shape[-2] // N_DEV, full.shape[-1]))
    return jnp.moveaxis(f, -3, 0)


def _pack_small(arrs):
    rows = []
    for a in arrs:
        flat = a.astype(F32).reshape(-1)
        pad = (-flat.shape[0]) % LANES
        rows.append(jnp.pad(flat, (0, pad)).reshape(-1, LANES))
    p = jnp.concatenate(rows, axis=0)
    return jnp.pad(p, ((0, (-p.shape[0]) % 8), (0, 0)))


def _unpack_small(packed, shapes):
    out, off = [], 0
    for shp in shapes:
        n = math.prod(shp)
        nr = -(-n // LANES)
        out.append(packed[off:off + nr].reshape(-1)[:n].reshape(shp))
        off += nr
    return out


def _local_step(x, target, meta, w_fox_in, w_fox_out, late, small):
    S, D = x.shape
    Lp = S + HEAD_ROWS
    T = ROW_TILE if Lp % ROW_TILE == 0 else HEAD_ROWS
    P = D // LANES
    row = lambda v: v.reshape(1, -1).astype(F32)

    w_fin = jnp.pad(w_fox_in[0], ((0, 0), (0, LANES - w_fox_in.shape[-1] % LANES)))
    w_fout = w_fox_out[0]
    n_heads = w_fox_in.shape[-1] - 4 * D
    bf = jnp.pad(row(small["fox_b_f"]), ((0, 0), (0, LANES - small["fox_b_f"].size)))
    qg = jnp.tile(row(small["fox_q_norm"]), (1, 2))
    kg = jnp.tile(row(small["fox_k_norm"]), (1, 2))

    h0 = jnp.concatenate([jnp.zeros((N_PAD, D), F32), meta, x], axis=0)

    hn0 = _rms_fwd(h0, row(small["attn_norm"][0]), T, "rms0_fwd")
    proj0 = _mm(hn0, w_fin, "nn", F32, "fox_in_fwd")
    qn, kn, vb, c = _fox_prep_fwd(proj0, bf, qg, kg, T, D, "fox_prep_fwd")
    cT = c.T[:2 * P]
    o, og0, mshift, linv, *gathered = _fox_attn_fwd(qn, kn, vb, c, cT, proj0, [late[n] for n in LATE], T, D,
                                                    "fox_attn_fwd")
    wl = {n: _gathered_to_full(g, n) for n, g in zip(LATE, gathered)}
    w_hin, w_hout, w_uin, w_uout = wl["hgrn_w_in"][0], wl["hgrn_w_out"][0], wl["ffn_w_in"], wl["ffn_w_out"]
    h1 = _mm(og0, w_fout, "nn", F32, "fox_out_fwd", res=h0)
    hf0 = _rms_fwd(h1, row(small["ffn_norm"][0]), T, "rmsf0_fwd")
    gu0 = _ffn_in_fwd(hf0, w_uin[0], "ffn0_in_fwd")
    act0 = gu0[2]
    h2 = _mm(act0, w_uout[0], "nn", F32, "ffn0_out_fwd", res=h1)
    hn1 = _rms_fwd(h2, row(small["attn_norm"][1]), T, "rms1_fwd")
    proj1 = _mm(hn1, w_hin, "nn", F32, "hgrn_in_fwd")
    lbp = small["hgrn_lower_bounds"].astype(F32)
    ggn = row(small["hgrn_g_norm"])
    og1, ss = _hgrn_fwd(proj1, lbp, ggn, T, "hgrn_fwd")
    h3 = _mm(og1, w_hout, "nn", F32, "hgrn_out_fwd", res=h2)
    hf1 = _rms_fwd(h3, row(small["ffn_norm"][1]), T, "rmsf1_fwd")
    gu1 = _ffn_in_fwd(hf1, w_uin[1], "ffn1_in_fwd")
    act1 = gu1[2]
    h4 = _mm(act1, w_uout[1], "nn", F32, "ffn1_out_fwd", res=h3)
    loss_blk, dh4, dh4b, d_final = _final_loss(h4, row(small["final_norm"]), target, "final_loss")

    grads = {}

    def ffn_bwd(i, dh, dhb, h_in, hf, gu, act, tag):
        grads_out = _mm(act, dhb, "tn", F32, f"ffn{i}_out_dw")
        dgu = _ffn_out_dx(dhb, w_uout[i], gu[0], gu[1], f"ffn{i}_out_dx")
        grads_in = _mm(hf, dgu, "tn", F32, f"ffn{i}_in_dw")
        dhf = _mm(dgu, w_uin[i], "nt", F32, f"ffn{i}_in_dx")
        dh_new, dh_newb, dgain = _rms_bwd(h_in, row(small["ffn_norm"][i]), dhf, dh, T, f"rmsf{i}_bwd")
        return dh_new, dh_newb, grads_in, grads_out, dgain

    dh3, dh3b, g_uin1, g_uout1, d_fn1 = ffn_bwd(1, dh4, dh4b, h3, hf1, gu1, act1, "1")
    grads["hgrn_w_out"] = _mm(og1, dh3b, "tn", F32, "hgrn_out_dw")[None]
    dog1 = _mm(dh3b, w_hout, "nt", F32, "hgrn_out_dx")
    dq1, dz1, di1, dgo1, d_lb, d_gg = _hgrn_bwd(proj1, lbp, ggn, dog1, ss, T, "hgrn_bwd")
    dproj1 = jnp.concatenate([dq1, dz1, di1, dgo1], axis=1)
    grads["hgrn_w_in"] = _mm(hn1, dproj1, "tn", F32, "hgrn_in_dw")[None]
    dhn1 = _mm(dproj1, w_hin, "nt", F32, "hgrn_in_dx")
    dh2, dh2b, d_an1 = _rms_bwd(h2, row(small["attn_norm"][1]), dhn1, dh3, T, "rms1_bwd")
    dh1, dh1b, g_uin0, g_uout0, d_fn0 = ffn_bwd(0, dh2, dh2b, h1, hf0, gu0, act0, "0")
    grads["ffn_w_in"] = jnp.stack([g_uin0, g_uin1])
    grads["ffn_w_out"] = jnp.stack([g_uout0, g_uout1])
    grads["fox_w_out"] = _mm(og0, dh1b, "tn", F32, "fox_out_dw")[None]
    dog0 = _mm(dh1b, w_fout, "nt", F32, "fox_out_dx")
    do, dgate, delta = _fox_gate_bwd(dog0, o, proj0, linv, T, D, "fox_gate_bwd")
    slabs = [_full_to_slabs(grads[n], n).astype(BF16) for n in LATE]
    dqn, dkn, dv, dcr, *recv = _fox_attn_bwd(qn, kn, vb, c, cT, do, mshift, delta, slabs, T, D, "fox_attn_bwd")
    for n in LATE:
        del grads[n]
    dc = jnp.pad(dcr[:, :2, :].reshape(2 * P, Lp).T, ((0, 0), (0, LANES - 2 * P)))
    dq0, dk0, dfl, sm = _fox_prep_bwd(proj0, bf, qg, kg, dqn, dkn, dc, T, D, "fox_prep_bwd")
    dproj0 = jnp.concatenate([dq0, dk0, dv.astype(BF16), dgate, dfl], axis=1)
    grads["fox_w_in"] = _mm(hn0, dproj0, "tn", F32, "fox_in_dw")[:, :4 * D + n_heads][None]
    dhn0 = _mm(dproj0, w_fin, "nt", F32, "fox_in_dx")
    dh0, _, d_an0 = _rms_bwd(h0, row(small["attn_norm"][0]), dhn0, dh1, T, "rms0_bwd")

    grads["meta_tokens"] = dh0[N_PAD:HEAD_ROWS]
    grads["attn_norm"] = jnp.concatenate([d_an0, d_an1], axis=0)
    grads["ffn_norm"] = jnp.concatenate([d_fn0, d_fn1], axis=0)
    grads["final_norm"] = d_final[0]
    grads["fox_b_f"] = sm[0:1, :n_heads]
    grads["fox_q_norm"] = sm[1:2, :FOX_DH] + sm[1:2, FOX_DH:]
    grads["fox_k_norm"] = sm[2:3, :FOX_DH] + sm[2:3, FOX_DH:]
    grads["hgrn_lower_bounds"] = d_lb[0:2]
    grads["hgrn_g_norm"] = d_gg[0:1]
    return loss_blk[0, 0], dh0[HEAD_ROWS:], grads, dict(zip(LATE, recv))


def kernel(x, meta_tokens, attn_norm, ffn_norm, final_norm, fox_w_in, fox_b_f, fox_q_norm, fox_k_norm, fox_w_out, hgrn_w_in, hgrn_lower_bounds, hgrn_g_norm, hgrn_w_out, ffn_w_in, ffn_w_out, loss_target, m_meta_tokens, m_attn_norm, m_ffn_norm, m_final_norm, m_fox_w_in, m_fox_b_f, m_fox_q_norm, m_fox_k_norm, m_fox_w_out, m_hgrn_w_in, m_hgrn_lower_bounds, m_hgrn_g_norm, m_hgrn_w_out, m_ffn_w_in, m_ffn_w_out, v_meta_tokens, v_attn_norm, v_ffn_norm, v_final_norm, v_fox_w_in, v_fox_b_f, v_fox_q_norm, v_fox_k_norm, v_fox_w_out, v_hgrn_w_in, v_hgrn_lower_bounds, v_hgrn_g_norm, v_hgrn_w_out, v_ffn_w_in, v_ffn_w_out):
    w = dict(meta_tokens=meta_tokens, attn_norm=attn_norm, ffn_norm=ffn_norm, final_norm=final_norm,
             fox_w_in=fox_w_in, fox_b_f=fox_b_f, fox_q_norm=fox_q_norm, fox_k_norm=fox_k_norm,
             fox_w_out=fox_w_out, hgrn_w_in=hgrn_w_in, hgrn_lower_bounds=hgrn_lower_bounds,
             hgrn_g_norm=hgrn_g_norm, hgrn_w_out=hgrn_w_out, ffn_w_in=ffn_w_in, ffn_w_out=ffn_w_out)
    m = dict(meta_tokens=m_meta_tokens, attn_norm=m_attn_norm, ffn_norm=m_ffn_norm, final_norm=m_final_norm,
             fox_w_in=m_fox_w_in, fox_b_f=m_fox_b_f, fox_q_norm=m_fox_q_norm, fox_k_norm=m_fox_k_norm,
             fox_w_out=m_fox_w_out, hgrn_w_in=m_hgrn_w_in, hgrn_lower_bounds=m_hgrn_lower_bounds,
             hgrn_g_norm=m_hgrn_g_norm, hgrn_w_out=m_hgrn_w_out, ffn_w_in=m_ffn_w_in, ffn_w_out=m_ffn_w_out)
    v = dict(meta_tokens=v_meta_tokens, attn_norm=v_attn_norm, ffn_norm=v_ffn_norm, final_norm=v_final_norm,
             fox_w_in=v_fox_w_in, fox_b_f=v_fox_b_f, fox_q_norm=v_fox_q_norm, fox_k_norm=v_fox_k_norm,
             fox_w_out=v_fox_w_out, hgrn_w_in=v_hgrn_w_in, hgrn_lower_bounds=v_hgrn_lower_bounds,
             hgrn_g_norm=v_hgrn_g_norm, hgrn_w_out=v_hgrn_w_out, ffn_w_in=v_ffn_w_in, ffn_w_out=v_ffn_w_out)
    axes = ("x", "y", "c")
    small_shapes = [w[n].shape for n in SMALL]

    g_meta, g_fin, g_fout = _exchange([w["meta_tokens"].astype(F32), w["fox_w_in"].astype(BF16),
                                       w["fox_w_out"].astype(BF16)], [False] * 3, "gather_weights")
    loss_local, grad_x, grads, recv = _local_step(
        x[0], loss_target[0], _gathered_to_full(g_meta, "meta_tokens"), _gathered_to_full(g_fin, "fox_w_in"),
        _gathered_to_full(g_fout, "fox_w_out"), {n: w[n].astype(BF16) for n in LATE}, {n: w[n] for n in SMALL})
    loss = lax.psum(loss_local, axes)

    r_meta, r_fin, r_fout, r_small = _exchange(
        [_full_to_slabs(grads["meta_tokens"], "meta_tokens"), _full_to_slabs(grads["fox_w_in"], "fox_w_in").astype(BF16),
         _full_to_slabs(grads["fox_w_out"], "fox_w_out").astype(BF16), _pack_small([grads[n] for n in SMALL])],
        [True, True, True, False], "scatter_grads")
    recv.update(meta_tokens=r_meta, fox_w_in=r_fin, fox_w_out=r_fout)

    res = {n: _adamw(recv[n], w[n], m[n], v[n], "adamw_" + n) for n in BIG}
    sml = _adamw(r_small, _pack_small([w[n] for n in SMALL]), _pack_small([m[n] for n in SMALL]),
                 _pack_small([v[n] for n in SMALL]), "adamw_small")
    outs = []
    for k in range(4):
        d = {n: res[n][k] for n in BIG}
        d.update(zip(SMALL, _unpack_small(sml[k], small_shapes)))
        outs.extend(d[n] for n in WEIGHTS)
    return (loss, grad_x[None], *outs)
```

```python
import functools
import math

import jax
import jax.numpy as jnp
from jax import lax
from jax.experimental import pallas as pl
from jax.experimental.pallas import tpu as pltpu

F32 = jnp.float32
BF16 = jnp.bfloat16
EPS = 1e-6
N_META = 16
LANES = 128
HEAD_ROWS = 256
ROW_TILE = 768
N_PAD = HEAD_ROWS - N_META
FOX_DH = 64
HGRN_CHUNK = 64
N_DEV = 8
NEG = -1e30
VMEM_LIMIT = 56 * 1024 * 1024
HI = lax.Precision.HIGHEST

ADAM_LR = 0.001
ADAM_B1 = 0.9
ADAM_B2 = 0.999
ADAM_EPS = 1e-08
ADAM_WD = 0.01
ADAM_STEP = 10

BIG = ("meta_tokens", "fox_w_in", "fox_w_out", "hgrn_w_in", "hgrn_w_out", "ffn_w_in", "ffn_w_out")
SMALL = ("attn_norm", "ffn_norm", "final_norm", "fox_b_f", "fox_q_norm", "fox_k_norm",
         "hgrn_lower_bounds", "hgrn_g_norm")
WEIGHTS = ("meta_tokens", "attn_norm", "ffn_norm", "final_norm", "fox_w_in", "fox_b_f", "fox_q_norm",
           "fox_k_norm", "fox_w_out", "hgrn_w_in", "hgrn_lower_bounds", "hgrn_g_norm", "hgrn_w_out",
           "ffn_w_in", "ffn_w_out")
COL_SHARDED = ("meta_tokens", "fox_w_in", "hgrn_w_in", "ffn_w_in")
LATE = ("hgrn_w_in", "hgrn_w_out", "ffn_w_in", "ffn_w_out")


def _params(sem=None):
    return pltpu.CompilerParams(dimension_semantics=sem, vmem_limit_bytes=VMEM_LIMIT)


def _tile(n, cap):
    best = None
    for t in range(LANES, min(n, cap) + 1, LANES):
        if n % t == 0:
            best = t
    assert best is not None, (n, cap)
    return best


def _row_chunk(n, cap, mult=8):
    best = n
    for t in range(mult, min(n, cap) + 1, mult):
        if n % t == 0:
            best = t
    return best


def _dg(a, b, ca, cb):
    return lax.dot_general(a.astype(BF16), b.astype(BF16), (((ca,), (cb,)), ((), ())),
                           preferred_element_type=F32)


@jax.custom_vjp
def _d_nn(a, b):
    return _dg(a, b, 1, 0)


@jax.custom_vjp
def _d_nt(a, b):
    return _dg(a, b, 1, 1)


@jax.custom_vjp
def _d_tn(a, b):
    return _dg(a, b, 0, 0)


_d_nn.defvjp(lambda a, b: (_d_nn(a, b), (a, b)), lambda r, g: (_d_nt(g, r[1]), _d_tn(r[0], g)))
_d_nt.defvjp(lambda a, b: (_d_nt(a, b), (a, b)), lambda r, g: (_d_nn(g, r[1]), _d_tn(g, r[0])))
_d_tn.defvjp(lambda a, b: (_d_tn(a, b), (a, b)), lambda r, g: (_d_nt(r[1], g), _d_nn(r[0], g)))


def _log_sigmoid(x):
    return jnp.minimum(x, 0.0) - jnp.log1p(jnp.exp(-jnp.abs(x)))


def _rms(x, g):
    return x * lax.rsqrt(jnp.mean(x * x, axis=-1, keepdims=True) + EPS) * g


def _mm(a, b, mode, out_dtype, name, res=None, tm=None, tn=None, tk=None):
    assert a.dtype == BF16 and b.dtype == BF16, (name, a.dtype, b.dtype)
    if mode == "nn":
        (M, K), N = a.shape, b.shape[1]
    elif mode == "nt":
        (M, K), N = a.shape, b.shape[0]
    else:
        (K, M), N = a.shape, b.shape[1]
    if mode == "nn":
        tm, tn, tk = tm or _tile(M, ROW_TILE), tn or _tile(N, 1408), tk or _tile(K, 2816)
    elif mode == "nt":
        tm, tn, tk = tm or _tile(M, ROW_TILE if K <= 2048 else ROW_TILE // 2), tn or N, tk or K
    else:
        tm, tn, tk = tm or _tile(M, 1408), tn or _tile(N, 1408), tk or _tile(K, ROW_TILE)
    nk = K // tk
    if mode == "tn":
        a_spec = pl.BlockSpec((tk, tm), lambda j, i, k: (k, i))
        dims = (((0,), (0,)), ((), ()))
    else:
        a_spec = pl.BlockSpec((tm, tk), lambda j, i, k: (i, k))
        dims = (((1,), (1 if mode == "nt" else 0,)), ((), ()))
    if mode == "nt":
        b_spec = pl.BlockSpec((tn, tk), lambda j, i, k: (j, k))
    else:
        b_spec = pl.BlockSpec((tk, tn), lambda j, i, k: (k, j))

    o_spec = pl.BlockSpec((tm, tn), lambda j, i, k: (i, j))

    def body(a_ref, b_ref, *rest):
        r_ref = rest[0] if res is not None else None
        o_ref, acc_ref = rest[-2:]
        k = pl.program_id(2)

        @pl.when(k == 0)
        def _():
            acc_ref[...] = jnp.zeros_like(acc_ref)

        acc_ref[...] += lax.dot_general(a_ref[...], b_ref[...], dims, preferred_element_type=F32)

        @pl.when(k == nk - 1)
        def _():
            out = acc_ref[...] if r_ref is None else acc_ref[...] + r_ref[...]
            o_ref[...] = out.astype(out_dtype)

    return pl.pallas_call(
        body, name=name, grid=(N // tn, M // tm, nk),
        in_specs=[a_spec, b_spec] + ([o_spec] if res is not None else []),
        out_specs=o_spec,
        out_shape=jax.ShapeDtypeStruct((M, N), out_dtype),
        scratch_shapes=[pltpu.VMEM((tm, tn), F32)],
        compiler_params=_params(("parallel", "parallel", "arbitrary")),
    )(a, b, *([res] if res is not None else []))


def _rms_fwd(x, g, T, name):
    Lp, D = x.shape

    def body(x_ref, g_ref, o_ref):
        o_ref[...] = _rms(x_ref[...], g_ref[...]).astype(BF16)

    return pl.pallas_call(
        body, name=name, grid=(Lp // T,),
        in_specs=[pl.BlockSpec((T, D), lambda i: (i, 0)), pl.BlockSpec((1, D), lambda i: (0, 0))],
        out_specs=pl.BlockSpec((T, D), lambda i: (i, 0)),
        out_shape=jax.ShapeDtypeStruct((Lp, D), BF16),
        compiler_params=_params(("parallel",)),
    )(x, g)


def _rms_bwd(x, g, dy, dres, T, name):
    Lp, D = x.shape

    def body(x_ref, g_ref, dy_ref, dr_ref, dx_ref, dxb_ref, dg_ref):
        @pl.when(pl.program_id(0) == 0)
        def _():
            dg_ref[...] = jnp.zeros_like(dg_ref)

        _, vjp = jax.vjp(_rms, x_ref[...], g_ref[...])
        dx, dg = vjp(dy_ref[...])
        dx = dx + dr_ref[...]
        dx_ref[...] = dx
        dxb_ref[...] = dx.astype(BF16)
        dg_ref[...] += dg

    row = pl.BlockSpec((T, D), lambda i: (i, 0))
    vec = pl.BlockSpec((1, D), lambda i: (0, 0))
    return pl.pallas_call(
        body, name=name, grid=(Lp // T,),
        in_specs=[row, vec, row, row],
        out_specs=[row, row, vec],
        out_shape=[jax.ShapeDtypeStruct((Lp, D), F32), jax.ShapeDtypeStruct((Lp, D), BF16),
                   jax.ShapeDtypeStruct((1, D), F32)],
        compiler_params=_params(("arbitrary",)),
    )(x, g, dy, dres)


def _swiglu(gate, up):
    return gate * jax.nn.sigmoid(gate) * up


def _ffn_in_fwd(hf, w_in, name):
    Lp, D = hf.shape
    F = w_in.shape[1] // 2
    tm = _tile(Lp, ROW_TILE)
    tn = _tile(F, 1408)
    nj = F // tn

    def body(a_ref, bg_ref, bu_ref, g_ref, u_ref, act_ref):
        a = a_ref[...]
        g = jnp.dot(a, bg_ref[...], preferred_element_type=F32)
        u = jnp.dot(a, bu_ref[...], preferred_element_type=F32)
        g_ref[...] = g
        u_ref[...] = u
        act_ref[...] = _swiglu(g, u).astype(BF16)

    tile = pl.BlockSpec((tm, tn), lambda j, i: (i, j))
    return pl.pallas_call(
        body, name=name, grid=(nj, Lp // tm),
        in_specs=[pl.BlockSpec((tm, D), lambda j, i: (i, 0)), pl.BlockSpec((D, tn), lambda j, i: (0, j)),
                  pl.BlockSpec((D, tn), lambda j, i: (0, nj + j))],
        out_specs=[tile, tile, tile],
        out_shape=[jax.ShapeDtypeStruct((Lp, F), F32), jax.ShapeDtypeStruct((Lp, F), F32),
                   jax.ShapeDtypeStruct((Lp, F), BF16)],
        compiler_params=_params(("parallel", "parallel")),
    )(hf, w_in, w_in)


def _ffn_out_dx(dhb, w_out, g, u, name):
    Lp, D = dhb.shape
    F = w_out.shape[0]
    tm = HEAD_ROWS

    def body(a_ref, b_ref, g_ref, u_ref, o_ref):
        dact = lax.dot_general(a_ref[...], b_ref[...], (((1,), (1,)), ((), ())), preferred_element_type=F32)
        _, vjp = jax.vjp(_swiglu, g_ref[...], u_ref[...])
        dg, du = vjp(dact)
        o_ref[:, :F] = dg.astype(BF16)
        o_ref[:, F:] = du.astype(BF16)

    wide = pl.BlockSpec((tm, F), lambda i: (i, 0))
    return pl.pallas_call(
        body, name=name, grid=(Lp // tm,),
        in_specs=[pl.BlockSpec((tm, D), lambda i: (i, 0)), pl.BlockSpec((F, D), lambda i: (0, 0)), wide, wide],
        out_specs=pl.BlockSpec((tm, 2 * F), lambda i: (i, 0)),
        out_shape=jax.ShapeDtypeStruct((Lp, 2 * F), BF16),
        compiler_params=_params(("parallel",)),
    )(dhb, w_out, g, u)


def _final_loss(h, g, target, name):
    Lp, D = h.shape
    TR = HEAD_ROWS

    def loss_fn(hh, gg, tt):
        err = _rms(hh, gg) - tt
        return 0.5 * jnp.sum(jnp.mean(err * err, axis=-1))

    def body(h_ref, g_ref, t_ref, loss_ref, dh_ref, dhb_ref, dg_ref):
        i = pl.program_id(0)

        @pl.when(i == 0)
        def _():
            loss_ref[...] = jnp.zeros_like(loss_ref)
            dg_ref[...] = jnp.zeros_like(dg_ref)
            dh_ref[...] = jnp.zeros_like(dh_ref)
            dhb_ref[...] = jnp.zeros_like(dhb_ref)

        @pl.when(i > 0)
        def _():
            val, vjp = jax.vjp(lambda hh, gg: loss_fn(hh, gg, t_ref[...]), h_ref[...], g_ref[...])
            dh, dg = vjp(jnp.ones((), F32))
            dh_ref[...] = dh
            dhb_ref[...] = dh.astype(BF16)
            dg_ref[...] += dg
            loss_ref[...] += val

    row = pl.BlockSpec((TR, D), lambda i: (i, 0))
    return pl.pallas_call(
        body, name=name, grid=(Lp // TR,),
        in_specs=[row, pl.BlockSpec((1, D), lambda i: (0, 0)),
                  pl.BlockSpec((TR, D), lambda i: (jnp.maximum(i - 1, 0), 0))],
        out_specs=[pl.BlockSpec((8, LANES), lambda i: (0, 0)), row, row, pl.BlockSpec((1, D), lambda i: (0, 0))],
        out_shape=[jax.ShapeDtypeStruct((8, LANES), F32), jax.ShapeDtypeStruct((Lp, D), F32),
                   jax.ShapeDtypeStruct((Lp, D), BF16), jax.ShapeDtypeStruct((1, D), F32)],
        compiler_params=_params(("arbitrary",)),
    )(h, g, target)


def _lane_lo():
    return lax.broadcasted_iota(jnp.int32, (1, LANES), 1) < FOX_DH


def _headnorm(x, g, scale):
    lo = _lane_lo()
    x2 = x * x
    s0 = jnp.sum(jnp.where(lo, x2, 0.0), axis=-1, keepdims=True)
    s1 = jnp.sum(jnp.where(lo, 0.0, x2), axis=-1, keepdims=True)
    r = jnp.where(lo, lax.rsqrt(s0 / FOX_DH + EPS), lax.rsqrt(s1 / FOX_DH + EPS))
    return x * r * g * scale


def _fox_prep_fwd(proj, bf, qg, kg, T, D, name):
    Lp = proj.shape[0]
    nb = D // LANES
    scale = FOX_DH ** -0.5

    def body(q_ref, k_ref, v_ref, fl_ref, bf_ref, qg_ref, kg_ref, qn_ref, kn_ref, vb_ref, c_ref, carry_ref):
        @pl.when(pl.program_id(0) == 0)
        def _():
            carry_ref[...] = jnp.zeros_like(carry_ref)

        for b in range(nb):
            sl = slice(b * LANES, (b + 1) * LANES)
            qn_ref[:, sl] = _headnorm(q_ref[:, sl], qg_ref[...], scale).astype(BF16)
            kn_ref[:, sl] = _headnorm(k_ref[:, sl], kg_ref[...], 1.0).astype(BF16)
        vb_ref[...] = v_ref[...].astype(BF16)
        log_f = _log_sigmoid(fl_ref[...] + bf_ref[...])
        row = lax.broadcasted_iota(jnp.int32, (T, T), 0)
        col = lax.broadcasted_iota(jnp.int32, (T, T), 1)
        tri = (col <= row).astype(F32)
        c = jnp.dot(tri, log_f, precision=HI, preferred_element_type=F32) + carry_ref[...]
        c_ref[...] = c
        last = lax.broadcasted_iota(jnp.int32, (T, 1), 0) == T - 1
        carry_ref[...] = jnp.sum(jnp.where(last, c, 0.0), axis=0, keepdims=True)

    wide = lambda j: pl.BlockSpec((T, D), lambda i: (i, j))
    vec = pl.BlockSpec((1, LANES), lambda i: (0, 0))
    return pl.pallas_call(
        body, name=name, grid=(Lp // T,),
        in_specs=[wide(0), wide(1), wide(2), pl.BlockSpec((T, LANES), lambda i: (i, 4 * nb)), vec, vec, vec],
        out_specs=[wide(0), wide(0), wide(0), pl.BlockSpec((T, LANES), lambda i: (i, 0))],
        out_shape=[jax.ShapeDtypeStruct((Lp, D), BF16)] * 3 + [jax.ShapeDtypeStruct((Lp, LANES), F32)],
        scratch_shapes=[pltpu.VMEM((1, LANES), F32)],
        compiler_params=_params(("arbitrary",)),
    )(proj, proj, proj, proj, bf, qg, kg)


def _fox_prep_bwd(proj, bf, qg, kg, dqn, dkn, dc, T, D, name):
    Lp = proj.shape[0]
    nb = D // LANES
    nt = Lp // T
    scale = FOX_DH ** -0.5

    def body(q_ref, k_ref, fl_ref, bf_ref, qg_ref, kg_ref, dqn_ref, dkn_ref, dc_ref,
             dq_ref, dk_ref, dfl_ref, sm_ref, carry_ref):
        @pl.when(pl.program_id(0) == 0)
        def _():
            carry_ref[...] = jnp.zeros_like(carry_ref)
            sm_ref[...] = jnp.zeros_like(sm_ref)

        dqg = jnp.zeros((1, LANES), F32)
        dkg = jnp.zeros((1, LANES), F32)
        for b in range(nb):
            sl = slice(b * LANES, (b + 1) * LANES)
            _, vjp = jax.vjp(lambda x, g: _headnorm(x, g, scale), q_ref[:, sl], qg_ref[...])
            dx, dg = vjp(dqn_ref[:, sl])
            dq_ref[:, sl] = dx.astype(BF16)
            dqg = dqg + dg
            _, vjp = jax.vjp(lambda x, g: _headnorm(x, g, 1.0), k_ref[:, sl], kg_ref[...])
            dx, dg = vjp(dkn_ref[:, sl])
            dk_ref[:, sl] = dx.astype(BF16)
            dkg = dkg + dg
        dcv = dc_ref[...]
        row = lax.broadcasted_iota(jnp.int32, (T, T), 0)
        col = lax.broadcasted_iota(jnp.int32, (T, T), 1)
        triu = (col >= row).astype(F32)
        dlogf = jnp.dot(triu, dcv, precision=HI, preferred_element_type=F32) + carry_ref[...]
        carry_ref[...] += jnp.sum(dcv, axis=0, keepdims=True)
        _, vjp = jax.vjp(_log_sigmoid, fl_ref[...] + bf_ref[...])
        (dfl,) = vjp(dlogf)
        dfl_ref[...] = dfl.astype(BF16)
        sm_ref[0:1, :] += jnp.sum(dfl, axis=0, keepdims=True)
        sm_ref[1:2, :] += dqg
        sm_ref[2:3, :] += dkg

    wide = lambda j: pl.BlockSpec((T, D), lambda i: (nt - 1 - i, j))
    narrow = lambda j: pl.BlockSpec((T, LANES), lambda i: (nt - 1 - i, j))
    vec = pl.BlockSpec((1, LANES), lambda i: (0, 0))
    return pl.pallas_call(
        body, name=name, grid=(nt,),
        in_specs=[wide(0), wide(1), narrow(4 * nb), vec, vec, vec, wide(0), wide(0), narrow(0)],
        out_specs=[wide(0), wide(0), narrow(0), pl.BlockSpec((8, LANES), lambda i: (0, 0))],
        out_shape=[jax.ShapeDtypeStruct((Lp, D), BF16)] * 2 + [jax.ShapeDtypeStruct((Lp, LANES), BF16),
                                                                 jax.ShapeDtypeStruct((8, LANES), F32)],
        scratch_shapes=[pltpu.VMEM((1, LANES), F32)],
        compiler_params=_params(("arbitrary",)),
    )(proj, proj, proj, bf, qg, kg, dqn, dkn, dc)


def _ln2_ceil(m):
    return jnp.ceil(m * (1.0 / math.log(2.0))) * math.log(2.0)


def _fox_mask(i, k0, T):
    qpos = i * T + lax.broadcasted_iota(jnp.int32, (T, 1), 0)
    kpos = k0 + lax.broadcasted_iota(jnp.int32, (1, T), 1)
    return (kpos <= qpos) & ((kpos >= N_PAD) | (qpos < N_PAD))


def _pick_col(blk, idx):
    lane = lax.broadcasted_iota(jnp.int32, (1, LANES), 1)
    return jnp.sum(jnp.where(lane == idx, blk, 0.0), axis=1, keepdims=True)


def _split_halves(blk):
    lo = _lane_lo()
    return (jnp.max(jnp.where(lo, blk, -jnp.inf), axis=1, keepdims=True),
            jnp.max(jnp.where(lo, -jnp.inf, blk), axis=1, keepdims=True))


def _fox_attn_fwd(qn, kn, vb, c, cT, proj, xchg, T, D, name):
    Lp = qn.shape[0]
    P = D // LANES
    nt = Lp // T
    H = cT.shape[0]
    nx = len(xchg)

    def body(q_ref, k_ref, v_ref, c_ref, cT_ref, g_ref, *rest):
        x_in, (o_ref, og_ref, m_ref, li_ref), x_out, sems = rest[:nx], rest[nx:nx + 4], rest[nx + 4:2 * nx + 4], rest[2 * nx + 4:]
        p = pl.program_id(0)
        i = pl.program_id(1)

        @pl.when((p == 0) & (i == 0))
        def _():
            for cp in _xchg_copies(x_in, x_out, [False] * nx, sems):
                cp.start()

        lo = _lane_lo()
        q = q_ref[...]
        zero = jnp.zeros_like(q)
        qh = (jnp.where(lo, q, zero), jnp.where(lo, zero, q))
        cblk = c_ref[...]
        cq = tuple(_pick_col(cblk, 2 * p + h) for h in (0, 1))
        one = jnp.ones_like(q)

        def step(j, carry, masked):
            k0 = pl.multiple_of(j * T, LANES)
            kj = k_ref[pl.ds(k0, T), :]
            vj = v_ref[pl.ds(k0, T), :]
            vh = (jnp.where(lo, vj, one), jnp.where(lo, one, vj))
            mask = _fox_mask(i, k0, T) if masked else None
            out = []
            for h in (0, 1):
                m, acc = carry[h]
                ck = cT_ref[pl.ds(2 * p + h, 1), pl.ds(k0, T)]
                t = lax.dot_general(qh[h], kj, (((1,), (1,)), ((), ())), preferred_element_type=F32) - ck
                if masked:
                    t = jnp.where(mask, t, NEG)
                m_new = _ln2_ceil(jnp.maximum(m, cq[h] + jnp.max(t, axis=1, keepdims=True)))
                pr = jnp.exp(t + (cq[h] - m_new)).astype(BF16)
                acc = jnp.exp(m - m_new) * acc + jnp.dot(pr, vh[h], preferred_element_type=F32)
                out.append((m_new, acc))
            return tuple(out)

        init = tuple((jnp.full((T, 1), NEG, F32), jnp.zeros((T, LANES), F32)) for _ in (0, 1))
        carry = step(0, init, True)
        carry = lax.fori_loop(1, i, lambda j, cr: step(j, cr, False), carry)
        (m0, a0), (m1, a1) = lax.cond(i > 0, lambda cr: step(i, cr, True), lambda cr: cr, carry)
        l0 = pltpu.roll(a0, FOX_DH, 1)
        l1 = pltpu.roll(a1, FOX_DH, 1)
        o = jnp.where(lo, a0 / l0, a1 / l1)
        o_ref[...] = o
        m_ref[...] = jnp.where(lo, m0, m1)
        li_ref[...] = jnp.where(lo, 1.0 / l0, 1.0 / l1)
        og_ref[...] = (o * jax.nn.sigmoid(g_ref[...])).astype(BF16)

        @pl.when((p == P - 1) & (i == nt - 1))
        def _():
            for cp in _xchg_copies(x_in, x_out, [False] * nx, sems):
                cp.wait()

    tile = pl.BlockSpec((T, LANES), lambda p, i: (i, p))
    full = pl.BlockSpec((Lp, LANES), lambda p, i: (0, p))
    HBM = pl.BlockSpec(memory_space=pltpu.HBM)
    return pl.pallas_call(
        body, name=name, grid=(P, nt),
        in_specs=[tile, full, full, pl.BlockSpec((T, LANES), lambda p, i: (i, 0)),
                  pl.BlockSpec((H, Lp), lambda p, i: (0, 0)),
                  pl.BlockSpec((T, LANES), lambda p, i: (i, 3 * P + p))] + [HBM] * nx,
        out_specs=[tile, tile, tile, tile] + [HBM] * nx,
        out_shape=[jax.ShapeDtypeStruct((Lp, D), F32), jax.ShapeDtypeStruct((Lp, D), BF16),
                   jax.ShapeDtypeStruct((Lp, D), F32), jax.ShapeDtypeStruct((Lp, D), F32)]
        + _xchg_out_shapes(xchg, [False] * nx),
        scratch_shapes=_xchg_sems(nx),
        compiler_params=_params(("arbitrary", "arbitrary")),
    )(qn, kn, vb, c, cT, proj, *xchg)


def _fox_gate_bwd(dog, o, proj, linv, T, D, name):
    Lp = o.shape[0]
    P = D // LANES

    def body(dog_ref, o_ref, g_ref, li_ref, do_ref, dg_ref, dl_ref):
        lo = _lane_lo()
        sig = jax.nn.sigmoid(g_ref[...])
        ov = o_ref[...]
        do = (dog_ref[...] * sig * li_ref[...]).astype(BF16)
        do_ref[...] = do
        dg_ref[...] = (dog_ref[...] * ov * sig * (1.0 - sig)).astype(BF16)
        t = do.astype(F32) * ov
        d0 = jnp.sum(jnp.where(lo, t, 0.0), axis=1, keepdims=True)
        d1 = jnp.sum(jnp.where(lo, 0.0, t), axis=1, keepdims=True)
        dl_ref[...] = jnp.where(lo, d0, d1)

    tile = pl.BlockSpec((T, LANES), lambda i, p: (i, p))
    return pl.pallas_call(
        body, name=name, grid=(Lp // T, P),
        in_specs=[tile, tile, pl.BlockSpec((T, LANES), lambda i, p: (i, 3 * P + p)), tile],
        out_specs=[tile, tile, tile],
        out_shape=[jax.ShapeDtypeStruct((Lp, D), BF16), jax.ShapeDtypeStruct((Lp, D), BF16),
                   jax.ShapeDtypeStruct((Lp, D), F32)],
        compiler_params=_params(("parallel", "parallel")),
    )(dog, o, proj, linv)


def _fox_attn_bwd(qn, kn, vb, c, cT, do, mshift, delta, xchg, T, D, name):
    Lp = qn.shape[0]
    P = D // LANES
    nt = Lp // T
    H = cT.shape[0]
    nx = len(xchg)

    def body(q_ref, do_ref, m_ref, dl_ref, c_ref, k_ref, v_ref, cT_ref, *rest):
        x_in, (dq_ref, dk_ref, dv_ref, dc_ref), x_out, sems = rest[:nx], rest[nx:nx + 4], rest[nx + 4:2 * nx + 4], rest[2 * nx + 4:]
        p = pl.program_id(0)
        i = pl.program_id(1)

        @pl.when((p == 0) & (i == 0))
        def _():
            for cp in _xchg_copies(x_in, x_out, [True] * nx, sems):
                cp.start()

        @pl.when(i == 0)
        def _():
            dk_ref[...] = jnp.zeros_like(dk_ref)
            dv_ref[...] = jnp.zeros_like(dv_ref)
            dc_ref[...] = jnp.zeros_like(dc_ref)

        lo = _lane_lo()
        q = q_ref[...]
        do = do_ref[...]
        zero = jnp.zeros_like(q)
        qh = (jnp.where(lo, q, zero), jnp.where(lo, zero, q))
        doh = (jnp.where(lo, do, zero), jnp.where(lo, zero, do))
        msh = _split_halves(m_ref[...])
        dlt = _split_halves(dl_ref[...])
        cblk = c_ref[...]
        shift = tuple(_pick_col(cblk, 2 * p + h) - msh[h] for h in (0, 1))

        def step(j, carry, masked):
            k0 = pl.multiple_of(j * T, LANES)
            kj = k_ref[pl.ds(k0, T), :]
            vj = v_ref[pl.ds(k0, T), :]
            mask = _fox_mask(i, k0, T) if masked else None
            dqs, dks, dvs = [], [], []
            for h in (0, 1):
                ck = cT_ref[pl.ds(2 * p + h, 1), pl.ds(k0, T)]
                t = lax.dot_general(qh[h], kj, (((1,), (1,)), ((), ())), preferred_element_type=F32) - ck
                if masked:
                    t = jnp.where(mask, t, NEG)
                pb = jnp.exp(t + shift[h]).astype(BF16)
                dp = lax.dot_general(doh[h], vj, (((1,), (1,)), ((), ())), preferred_element_type=F32)
                ds = pb.astype(F32) * (dp - dlt[h])
                dsb = ds.astype(BF16)
                dqs.append(carry[h] + jnp.dot(dsb, kj, preferred_element_type=F32))
                dks.append(lax.dot_general(dsb, q, (((0,), (0,)), ((), ())), preferred_element_type=F32))
                dvs.append(lax.dot_general(pb, do, (((0,), (0,)), ((), ())), preferred_element_type=F32))
                dc_ref[0, h:h + 1, pl.ds(k0, T)] += -jnp.sum(ds, axis=0, keepdims=True)
            dk_ref[pl.ds(k0, T), :] += jnp.where(lo, dks[0], dks[1])
            dv_ref[pl.ds(k0, T), :] += jnp.where(lo, dvs[0], dvs[1])
            return tuple(dqs)

        init = (jnp.zeros((T, LANES), F32), jnp.zeros((T, LANES), F32))
        carry = step(0, init, True)
        carry = lax.fori_loop(1, i, lambda j, cr: step(j, cr, False), carry)
        dq0, dq1 = lax.cond(i > 0, lambda cr: step(i, cr, True), lambda cr: cr, carry)
        dq_ref[...] = jnp.where(lo, dq0, dq1)

        @pl.when((p == P - 1) & (i == nt - 1))
        def _():
            for cp in _xchg_copies(x_in, x_out, [True] * nx, sems):
                cp.wait()

    tile = pl.BlockSpec((T, LANES), lambda p, i: (i, p))
    full = pl.BlockSpec((Lp, LANES), lambda p, i: (0, p))
    HBM = pl.BlockSpec(memory_space=pltpu.HBM)
    return pl.pallas_call(
        body, name=name, grid=(P, nt),
        in_specs=[tile, tile, tile, tile, pl.BlockSpec((T, LANES), lambda p, i: (i, 0)), full, full,
                  pl.BlockSpec((H, Lp), lambda p, i: (0, 0))] + [HBM] * nx,
        out_specs=[tile, full, full, pl.BlockSpec((1, 8, Lp), lambda p, i: (p, 0, 0))] + [HBM] * nx,
        out_shape=[jax.ShapeDtypeStruct((Lp, D), F32)] * 3 + [jax.ShapeDtypeStruct((P, 8, Lp), F32)]
        + _xchg_out_shapes(xchg, [True] * nx),
        scratch_shapes=_xchg_sems(nx),
        compiler_params=_params(("arbitrary", "arbitrary")),
    )(qn, do, mshift, delta, c, kn, vb, cT, *xchg)


def _scan_rows(x, reverse):
    C = x.shape[0]
    row = lax.broadcasted_iota(jnp.int32, (C, 1), 0)
    step = 1
    while step < C:
        if reverse:
            x = x + jnp.where(row < C - step, pltpu.roll(x, C - step, 0), 0.0)
        else:
            x = x + jnp.where(row >= step, pltpu.roll(x, step, 0), 0.0)
        step *= 2
    return x


@jax.custom_vjp
def _cumsum_rows(x):
    return _scan_rows(x, False)


_cumsum_rows.defvjp(lambda x: (_scan_rows(x, False), None), lambda _, g: (_scan_rows(g, True),))


def _hgrn_chunk(St, qr, z, vi, go, p0, p1, gg):
    C = qr.shape[0]
    lb = jax.nn.sigmoid(p1 - p0)
    a = jnp.log(lb)
    cc = jnp.log1p(-lb) + _log_sigmoid(z)
    log_f = jnp.maximum(a, cc) + jnp.log1p(jnp.exp(-jnp.abs(a - cc)))
    k = (1.0 - lb) * jax.nn.sigmoid(-z)
    q = qr * jax.nn.sigmoid(qr)
    row = lax.broadcasted_iota(jnp.int32, (C, C), 0)
    col = lax.broadcasted_iota(jnp.int32, (C, C), 1)
    causal = col <= row
    b = _cumsum_rows(log_f)
    mid = lax.broadcasted_iota(jnp.int32, (C, 1), 0) == C // 2 - 1
    r = jnp.sum(jnp.where(mid, b, 0.0), axis=0, keepdims=True)
    b_last = jnp.sum(log_f, axis=0, keepdims=True)
    attn = jnp.where(causal, _d_nt(q * jnp.exp(b - r), k * jnp.exp(r - b)), 0.0)
    o = _d_nn(attn, vi) + _d_nt(q * jnp.exp(b), St)
    St_new = St * jnp.exp(b_last) + _d_tn(vi, k * jnp.exp(b_last - b))
    og = _rms(o, gg) * (go * jax.nn.sigmoid(go))
    return St_new, og


def _hgrn_heads_per_step(H):
    return 4 if H % 4 == 0 else 1


def _hgrn_specs(T, W, nhb, rev_nt=None):
    if rev_nt is None:
        return [pl.BlockSpec((T, W), functools.partial(lambda hb, t, g: (t, g * nhb + hb), g=g)) for g in range(4)]
    return [pl.BlockSpec((T, W), functools.partial(lambda hb, t, g: (rev_nt - 1 - t, g * nhb + hb), g=g))
            for g in range(4)]


def _hgrn_fwd(proj, lbp, gg, T, name):
    Lp = proj.shape[0]
    D = proj.shape[1] // 4
    H = D // LANES
    hps = _hgrn_heads_per_step(H)
    W = hps * LANES
    nhb = H // hps
    nt = Lp // T
    ncc = T // HGRN_CHUNK

    def body(q_ref, z_ref, i_ref, go_ref, p_ref, gg_ref, og_ref, ss_ref, st_ref):
        @pl.when(pl.program_id(1) == 0)
        def _():
            st_ref[...] = jnp.zeros_like(st_ref)

        gain = gg_ref[...]

        def chunk(cidx, states):
            sl = pl.ds(pl.multiple_of(cidx * HGRN_CHUNK, HGRN_CHUNK), HGRN_CHUNK)
            new = []
            for hh in range(hps):
                ln = slice(hh * LANES, (hh + 1) * LANES)
                ss_ref[hh, cidx] = states[hh]
                St_new, og = _hgrn_chunk(states[hh], q_ref[sl, ln], z_ref[sl, ln], i_ref[sl, ln], go_ref[sl, ln],
                                         p_ref[0:1, ln], p_ref[1:2, ln], gain)
                og_ref[sl, ln] = og.astype(BF16)
                new.append(St_new)
            return tuple(new)

        states = lax.fori_loop(0, ncc, chunk, tuple(st_ref[hh] for hh in range(hps)))
        for hh in range(hps):
            st_ref[hh] = states[hh]

    return pl.pallas_call(
        body, name=name, grid=(nhb, nt),
        in_specs=_hgrn_specs(T, W, nhb) + [pl.BlockSpec((2, W), lambda hb, t: (0, hb)),
                                           pl.BlockSpec((1, LANES), lambda hb, t: (0, 0))],
        out_specs=[pl.BlockSpec((T, W), lambda hb, t: (t, hb)),
                   pl.BlockSpec((hps, ncc, LANES, LANES), lambda hb, t: (hb, t, 0, 0))],
        out_shape=[jax.ShapeDtypeStruct((Lp, D), BF16),
                   jax.ShapeDtypeStruct((H, Lp // HGRN_CHUNK, LANES, LANES), F32)],
        scratch_shapes=[pltpu.VMEM((hps, LANES, LANES), F32)],
        compiler_params=_params(("parallel", "arbitrary")),
    )(proj, proj, proj, proj, lbp, gg)


def _hgrn_bwd(proj, lbp, gg, dog, ss, T, name):
    Lp = proj.shape[0]
    D = proj.shape[1] // 4
    H = D // LANES
    hps = _hgrn_heads_per_step(H)
    W = hps * LANES
    nhb = H // hps
    nt = Lp // T
    ncc = T // HGRN_CHUNK

    def body(q_ref, z_ref, i_ref, go_ref, p_ref, gg_ref, dog_ref, ss_ref,
             dq_ref, dz_ref, di_ref, dgo_ref, dp_ref, dgg_ref, dst_ref):
        hb = pl.program_id(0)
        t = pl.program_id(1)

        @pl.when(t == 0)
        def _():
            dst_ref[...] = jnp.zeros_like(dst_ref)
            dp_ref[...] = jnp.zeros_like(dp_ref)

        @pl.when((t == 0) & (hb == 0))
        def _():
            dgg_ref[...] = jnp.zeros_like(dgg_ref)

        gain = gg_ref[...]
        row0 = (nt - 1 - t) * T

        def chunk(cc, carry):
            dstates, dps, dgain_sum = carry
            cidx = ncc - 1 - cc
            r0 = pl.multiple_of(cidx * HGRN_CHUNK, HGRN_CHUNK)
            sl = pl.ds(r0, HGRN_CHUNK)
            real = (row0 + r0 + lax.broadcasted_iota(jnp.int32, (HGRN_CHUNK, 1), 0)) >= N_PAD
            new_d, new_p = [], []
            for hh in range(hps):
                ln = slice(hh * LANES, (hh + 1) * LANES)
                _, vjp = jax.vjp(_hgrn_chunk, ss_ref[hh, cidx], q_ref[sl, ln], z_ref[sl, ln], i_ref[sl, ln],
                                 go_ref[sl, ln], p_ref[0:1, ln], p_ref[1:2, ln], gain)
                dSt, dq, dz, di, dgo, dp0, dp1, dgain = vjp((dstates[hh], dog_ref[sl, ln]))
                dq_ref[sl, ln] = jnp.where(real, dq, 0.0).astype(BF16)
                dz_ref[sl, ln] = jnp.where(real, dz, 0.0).astype(BF16)
                di_ref[sl, ln] = jnp.where(real, di, 0.0).astype(BF16)
                dgo_ref[sl, ln] = jnp.where(real, dgo, 0.0).astype(BF16)
                new_d.append(dSt)
                new_p.append((dps[hh][0] + dp0, dps[hh][1] + dp1))
                dgain_sum = dgain_sum + dgain
            return tuple(new_d), tuple(new_p), dgain_sum

        zero_row = jnp.zeros((1, LANES), F32)
        init = (tuple(dst_ref[hh] for hh in range(hps)), tuple((zero_row, zero_row) for _ in range(hps)), zero_row)
        dstates, dps, dgain_sum = lax.fori_loop(0, ncc, chunk, init)
        for hh in range(hps):
            ln = slice(hh * LANES, (hh + 1) * LANES)
            dst_ref[hh] = dstates[hh]
            dp_ref[0:1, ln] += dps[hh][0]
            dp_ref[1:2, ln] += dps[hh][1]
        dgg_ref[0:1, :] += dgain_sum

    rev = pl.BlockSpec((T, W), lambda hb, t: (nt - 1 - t, hb))
    return pl.pallas_call(
        body, name=name, grid=(nhb, nt),
        in_specs=_hgrn_specs(T, W, nhb, nt) + [pl.BlockSpec((2, W), lambda hb, t: (0, hb)),
                                               pl.BlockSpec((1, LANES), lambda hb, t: (0, 0)), rev,
                                               pl.BlockSpec((hps, ncc, LANES, LANES),
                                                            lambda hb, t: (hb, nt - 1 - t, 0, 0))],
        out_specs=[rev, rev, rev, rev, pl.BlockSpec((8, W), lambda hb, t: (0, hb)),
                   pl.BlockSpec((8, LANES), lambda hb, t: (0, 0))],
        out_shape=[jax.ShapeDtypeStruct((Lp, D), BF16)] * 4 + [jax.ShapeDtypeStruct((8, D), F32),
                                                                 jax.ShapeDtypeStruct((8, LANES), F32)],
        scratch_shapes=[pltpu.VMEM((hps, LANES, LANES), F32)],
        compiler_params=_params(("arbitrary", "arbitrary")),
    )(proj, proj, proj, proj, lbp, gg, dog, ss)


def _xchg_sems(n_arr):
    return [pltpu.SemaphoreType.DMA((n_arr * (N_DEV - 1),)), pltpu.SemaphoreType.DMA((n_arr * (N_DEV - 1),)),
            pltpu.SemaphoreType.DMA((n_arr,))]


def _xchg_copies(ins, outs, per_peer, sems):
    send_sems, recv_sems, local_sems = sems
    x, y, c = lax.axis_index("x"), lax.axis_index("y"), lax.axis_index("c")
    me = 4 * x + 2 * y + c
    copies = []
    for n in range(len(ins)):
        src = ins[n].at[me] if per_peer[n] else ins[n]
        copies.append(pltpu.make_async_copy(src, outs[n].at[me], local_sems.at[n]))
    for rel in range(1, N_DEV):
        fx, fy, fc = (rel >> 2) & 1, (rel >> 1) & 1, rel & 1
        px = 1 - x if fx else x
        py = 1 - y if fy else y
        pc = 1 - c if fc else c
        peer = 4 * px + 2 * py + pc
        for n in range(len(ins)):
            src = ins[n].at[peer] if per_peer[n] else ins[n]
            copies.append(pltpu.make_async_remote_copy(
                src_ref=src, dst_ref=outs[n].at[me],
                send_sem=send_sems.at[n * (N_DEV - 1) + rel - 1],
                recv_sem=recv_sems.at[n * (N_DEV - 1) + rel - 1],
                device_id=(px, py, pc), device_id_type=pl.DeviceIdType.MESH))
    return copies


def _xchg_out_shapes(arrays, per_peer):
    return [jax.ShapeDtypeStruct(a.shape if pp else (N_DEV,) + a.shape, a.dtype) for a, pp in zip(arrays, per_peer)]


def _exchange(arrays, per_peer, name):
    n_arr = len(arrays)
    HBM = pl.BlockSpec(memory_space=pltpu.HBM)

    def body(*refs):
        copies = _xchg_copies(refs[:n_arr], refs[n_arr:2 * n_arr], per_peer, refs[2 * n_arr:])
        for cp in copies:
            cp.start()
        for cp in copies:
            cp.wait()

    return pl.pallas_call(
        body, name=name,
        in_specs=[HBM] * n_arr, out_specs=[HBM] * n_arr, out_shape=_xchg_out_shapes(arrays, per_peer),
        scratch_shapes=_xchg_sems(n_arr),
    )(*arrays)


ADAMW_VMEM_BUDGET = 36 * 1024 * 1024


def _adamw(recv, w, m, v, name):
    shape = w.shape
    C = shape[-1]
    R = math.prod(shape[:-1])
    lanes = -(-C // LANES) * LANES
    row_bytes = 2 * lanes * (N_DEV * recv.dtype.itemsize + 7 * 4)
    rc = _row_chunk(R, max(16, ADAMW_VMEM_BUDGET // row_bytes), 16 if recv.dtype == BF16 else 8)

    def body(r_ref, w_ref, m_ref, v_ref, g_ref, d_ref, mo_ref, vo_ref):
        g = r_ref[0].astype(F32)
        for s in range(1, N_DEV):
            g = g + r_ref[s].astype(F32)
        mn = ADAM_B1 * m_ref[...] + (1.0 - ADAM_B1) * g
        vn = ADAM_B2 * v_ref[...] + (1.0 - ADAM_B2) * (g * g)
        m_hat = mn / (1.0 - ADAM_B1 ** ADAM_STEP)
        v_hat = vn / (1.0 - ADAM_B2 ** ADAM_STEP)
        g_ref[...] = g
        d_ref[...] = -ADAM_LR * (m_hat / (jnp.sqrt(v_hat) + ADAM_EPS) + ADAM_WD * w_ref[...])
        mo_ref[...] = mn
        vo_ref[...] = vn

    row = pl.BlockSpec((rc, C), lambda i: (i, 0))
    outs = pl.pallas_call(
        body, name=name, grid=(R // rc,),
        in_specs=[pl.BlockSpec((N_DEV, rc, C), lambda i: (0, i, 0)), row, row, row],
        out_specs=[row] * 4,
        out_shape=[jax.ShapeDtypeStruct((R, C), F32)] * 4,
        compiler_params=_params(("parallel",)),
    )(recv.reshape(N_DEV, R, C), w.reshape(R, C), m.reshape(R, C), v.reshape(R, C))
    return [o.reshape(shape) for o in outs]


def _gathered_to_full(g, name):
    if name in COL_SHARDED:
        g = jnp.moveaxis(g, 0, -2)
        return g.reshape(g.shape[:-2] + (g.shape[-2] * g.shape[-1],))
    g = jnp.moveaxis(g, 0, -3)
    return g.reshape(g.shape[:-3] + (g.shape[-3] * g.shape[-2], g.shape[-1]))


def _full_to_slabs(full, name):
    if name in COL_SHARDED:
        f = full.reshape(full.shape[:-1] + (N_DEV, full.shape[-1] // N_DEV))
        return jnp.moveaxis(f, -2, 0)
    f = full.reshape(full.shape[:-2] + (N_DEV, full.shape[-2] // N_DEV, full.shape[-1]))
    return jnp.moveaxis(f, -3, 0)


def _pack_small(arrs):
    rows = []
    for a in arrs:
        flat = a.astype(F32).reshape(-1)
        pad = (-flat.shape[0]) % LANES
        rows.append(jnp.pad(flat, (0, pad)).reshape(-1, LANES))
    p = jnp.concatenate(rows, axis=0)
    return jnp.pad(p, ((0, (-p.shape[0]) % 8), (0, 0)))


def _unpack_small(packed, shapes):
    out, off = [], 0
    for shp in shapes:
        n = math.prod(shp)
        nr = -(-n // LANES)
        out.append(packed[off:off + nr].reshape(-1)[:n].reshape(shp))
        off += nr
    return out


def _local_step(x, target, meta, w_fox_in, w_fox_out, late, small):
    S, D = x.shape
    Lp = S + HEAD_ROWS
    T = ROW_TILE if Lp % ROW_TILE == 0 else HEAD_ROWS
    P = D // LANES
    row = lambda v: v.reshape(1, -1).astype(F32)

    w_fin = jnp.pad(w_fox_in[0], ((0, 0), (0, LANES - w_fox_in.shape[-1] % LANES)))
    w_fout = w_fox_out[0]
    n_heads = w_fox_in.shape[-1] - 4 * D
    bf = jnp.pad(row(small["fox_b_f"]), ((0, 0), (0, LANES - small["fox_b_f"].size)))
    qg = jnp.tile(row(small["fox_q_norm"]), (1, 2))
    kg = jnp.tile(row(small["fox_k_norm"]), (1, 2))

    h0 = jnp.concatenate([jnp.zeros((N_PAD, D), F32), meta, x], axis=0)

    hn0 = _rms_fwd(h0, row(small["attn_norm"][0]), T, "rms0_fwd")
    proj0 = _mm(hn0, w_fin, "nn", F32, "fox_in_fwd")
    qn, kn, vb, c = _fox_prep_fwd(proj0, bf, qg, kg, T, D, "fox_prep_fwd")
    cT = c.T[:2 * P]
    o, og0, mshift, linv, *gathered = _fox_attn_fwd(qn, kn, vb, c, cT, proj0, [late[n] for n in LATE], T, D,
                                                    "fox_attn_fwd")
    wl = {n: _gathered_to_full(g, n) for n, g in zip(LATE, gathered)}
    w_hin, w_hout, w_uin, w_uout = wl["hgrn_w_in"][0], wl["hgrn_w_out"][0], wl["ffn_w_in"], wl["ffn_w_out"]
    h1 = _mm(og0, w_fout, "nn", F32, "fox_out_fwd", res=h0)
    hf0 = _rms_fwd(h1, row(small["ffn_norm"][0]), T, "rmsf0_fwd")
    gu0 = _ffn_in_fwd(hf0, w_uin[0], "ffn0_in_fwd")
    act0 = gu0[2]
    h2 = _mm(act0, w_uout[0], "nn", F32, "ffn0_out_fwd", res=h1)
    hn1 = _rms_fwd(h2, row(small["attn_norm"][1]), T, "rms1_fwd")
    proj1 = _mm(hn1, w_hin, "nn", F32, "hgrn_in_fwd")
    lbp = small["hgrn_lower_bounds"].astype(F32)
    ggn = row(small["hgrn_g_norm"])
    og1, ss = _hgrn_fwd(proj1, lbp, ggn, T, "hgrn_fwd")
    h3 = _mm(og1, w_hout, "nn", F32, "hgrn_out_fwd", res=h2)
    hf1 = _rms_fwd(h3, row(small["ffn_norm"][1]), T, "rmsf1_fwd")
    gu1 = _ffn_in_fwd(hf1, w_uin[1], "ffn1_in_fwd")
    act1 = gu1[2]
    h4 = _mm(act1, w_uout[1], "nn", F32, "ffn1_out_fwd", res=h3)
    loss_blk, dh4, dh4b, d_final = _final_loss(h4, row(small["final_norm"]), target, "final_loss")

    grads = {}

    def ffn_bwd(i, dh, dhb, h_in, hf, gu, act, tag):
        grads_out = _mm(act, dhb, "tn", F32, f"ffn{i}_out_dw")
        dgu = _ffn_out_dx(dhb, w_uout[i], gu[0], gu[1], f"ffn{i}_out_dx")
        grads_in = _mm(hf, dgu, "tn", F32, f"ffn{i}_in_dw")
        dhf = _mm(dgu, w_uin[i], "nt", F32, f"ffn{i}_in_dx")
        dh_new, dh_newb, dgain = _rms_bwd(h_in, row(small["ffn_norm"][i]), dhf, dh, T, f"rmsf{i}_bwd")
        return dh_new, dh_newb, grads_in, grads_out, dgain

    dh3, dh3b, g_uin1, g_uout1, d_fn1 = ffn_bwd(1, dh4, dh4b, h3, hf1, gu1, act1, "1")
    grads["hgrn_w_out"] = _mm(og1, dh3b, "tn", F32, "hgrn_out_dw")[None]
    dog1 = _mm(dh3b, w_hout, "nt", F32, "hgrn_out_dx")
    dq1, dz1, di1, dgo1, d_lb, d_gg = _hgrn_bwd(proj1, lbp, ggn, dog1, ss, T, "hgrn_bwd")
    dproj1 = jnp.concatenate([dq1, dz1, di1, dgo1], axis=1)
    grads["hgrn_w_in"] = _mm(hn1, dproj1, "tn", F32, "hgrn_in_dw")[None]
    dhn1 = _mm(dproj1, w_hin, "nt", F32, "hgrn_in_dx")
    dh2, dh2b, d_an1 = _rms_bwd(h2, row(small["attn_norm"][1]), dhn1, dh3, T, "rms1_bwd")
    dh1, dh1b, g_uin0, g_uout0, d_fn0 = ffn_bwd(0, dh2, dh2b, h1, hf0, gu0, act0, "0")
    grads["ffn_w_in"] = jnp.stack([g_uin0, g_uin1])
    grads["ffn_w_out"] = jnp.stack([g_uout0, g_uout1])
    grads["fox_w_out"] = _mm(og0, dh1b, "tn", F32, "fox_out_dw")[None]
    dog0 = _mm(dh1b, w_fout, "nt", F32, "fox_out_dx")
    do, dgate, delta = _fox_gate_bwd(dog0, o, proj0, linv, T, D, "fox_gate_bwd")
    slabs = [_full_to_slabs(grads[n], n).astype(BF16) for n in LATE]
    dqn, dkn, dv, dcr, *recv = _fox_attn_bwd(qn, kn, vb, c, cT, do, mshift, delta, slabs, T, D, "fox_attn_bwd")
    for n in LATE:
        del grads[n]
    dc = jnp.pad(dcr[:, :2, :].reshape(2 * P, Lp).T, ((0, 0), (0, LANES - 2 * P)))
    dq0, dk0, dfl, sm = _fox_prep_bwd(proj0, bf, qg, kg, dqn, dkn, dc, T, D, "fox_prep_bwd")
    dproj0 = jnp.concatenate([dq0, dk0, dv.astype(BF16), dgate, dfl], axis=1)
    grads["fox_w_in"] = _mm(hn0, dproj0, "tn", F32, "fox_in_dw")[:, :4 * D + n_heads][None]
    dhn0 = _mm(dproj0, w_fin, "nt", F32, "fox_in_dx")
    dh0, _, d_an0 = _rms_bwd(h0, row(small["attn_norm"][0]), dhn0, dh1, T, "rms0_bwd")

    grads["meta_tokens"] = dh0[N_PAD:HEAD_ROWS]
    grads["attn_norm"] = jnp.concatenate([d_an0, d_an1], axis=0)
    grads["ffn_norm"] = jnp.concatenate([d_fn0, d_fn1], axis=0)
    grads["final_norm"] = d_final[0]
    grads["fox_b_f"] = sm[0:1, :n_heads]
    grads["fox_q_norm"] = sm[1:2, :FOX_DH] + sm[1:2, FOX_DH:]
    grads["fox_k_norm"] = sm[2:3, :FOX_DH] + sm[2:3, FOX_DH:]
    grads["hgrn_lower_bounds"] = d_lb[0:2]
    grads["hgrn_g_norm"] = d_gg[0:1]
    return loss_blk[0, 0], dh0[HEAD_ROWS:], grads, dict(zip(LATE, recv))


def kernel(x, meta_tokens, attn_norm, ffn_norm, final_norm, fox_w_in, fox_b_f, fox_q_norm, fox_k_norm, fox_w_out, hgrn_w_in, hgrn_lower_bounds, hgrn_g_norm, hgrn_w_out, ffn_w_in, ffn_w_out, loss_target, m_meta_tokens, m_attn_norm, m_ffn_norm, m_final_norm, m_fox_w_in, m_fox_b_f, m_fox_q_norm, m_fox_k_norm, m_fox_w_out, m_hgrn_w_in, m_hgrn_lower_bounds, m_hgrn_g_norm, m_hgrn_w_out, m_ffn_w_in, m_ffn_w_out, v_meta_tokens, v_attn_norm, v_ffn_norm, v_final_norm, v_fox_w_in, v_fox_b_f, v_fox_q_norm, v_fox_k_norm, v_fox_w_out, v_hgrn_w_in, v_hgrn_lower_bounds, v_hgrn_g_norm, v_hgrn_w_out, v_ffn_w_in, v_ffn_w_out):
    w = dict(meta_tokens=meta_tokens, attn_norm=attn_norm, ffn_norm=ffn_norm, final_norm=final_norm,
             fox_w_in=fox_w_in, fox_b_f=fox_b_f, fox_q_norm=fox_q_norm, fox_k_norm=fox_k_norm,
             fox_w_out=fox_w_out, hgrn_w_in=hgrn_w_in, hgrn_lower_bounds=hgrn_lower_bounds,
             hgrn_g_norm=hgrn_g_norm, hgrn_w_out=hgrn_w_out, ffn_w_in=ffn_w_in, ffn_w_out=ffn_w_out)
    m = dict(meta_tokens=m_meta_tokens, attn_norm=m_attn_norm, ffn_norm=m_ffn_norm, final_norm=m_final_norm,
             fox_w_in=m_fox_w_in, fox_b_f=m_fox_b_f, fox_q_norm=m_fox_q_norm, fox_k_norm=m_fox_k_norm,
             fox_w_out=m_fox_w_out, hgrn_w_in=m_hgrn_w_in, hgrn_lower_bounds=m_hgrn_lower_bounds,
             hgrn_g_norm=m_hgrn_g_norm, hgrn_w_out=m_hgrn_w_out, ffn_w_in=m_ffn_w_in, ffn_w_out=m_ffn_w_out)
    v = dict(meta_tokens=v_meta_tokens, attn_norm=v_attn_norm, ffn_norm=v_ffn_norm, final_norm=v_final_norm,
             fox_w_in=v_fox_w_in, fox_b_f=v_fox_b_f, fox_q_norm=v_fox_q_norm, fox_k_norm=v_fox_k_norm,
             fox_w_out=v_fox_w_out, hgrn_w_in=v_hgrn_w_in, hgrn_lower_bounds=v_hgrn_lower_bounds,
             hgrn_g_norm=v_hgrn_g_norm, hgrn_w_out=v_hgrn_w_out, ffn_w_in=v_ffn_w_in, ffn_w_out=v_ffn_w_out)
    axes = ("x", "y", "c")
    small_shapes = [w[n].shape for n in SMALL]

    g_meta, g_fin, g_fout = _exchange([w["meta_tokens"].astype(F32), w["fox_w_in"].astype(BF16),
                                       w["fox_w_out"].astype(BF16)], [False] * 3, "gather_weights")
    loss_local, grad_x, grads, recv = _local_step(
        x[0], loss_target[0], _gathered_to_full(g_meta, "meta_tokens"), _gathered_to_full(g_fin, "fox_w_in"),
        _gathered_to_full(g_fout, "fox_w_out"), {n: w[n].astype(BF16) for n in LATE}, {n: w[n] for n in SMALL})
    loss = lax.psum(loss_local, axes)

    r_meta, r_fin, r_fout, r_small = _exchange(
        [_full_to_slabs(grads["meta_tokens"], "meta_tokens"), _full_to_slabs(grads["fox_w_in"], "fox_w_in").astype(BF16),
         _full_to_slabs(grads["fox_w_out"], "fox_w_out").astype(BF16), _pack_small([grads[n] for n in SMALL])],
        [True, True, True, False], "scatter_grads")
    recv.update(meta_tokens=r_meta, fox_w_in=r_fin, fox_w_out=r_fout)

    res = {n: _adamw(recv[n], w[n], m[n], v[n], "adamw_" + n) for n in BIG}
    sml = _adamw(r_small, _pack_small([w[n] for n in SMALL]), _pack_small([m[n] for n in SMALL]),
                 _pack_small([v[n] for n in SMALL]), "adamw_small")
    outs = []
    for k in range(4):
        d = {n: res[n][k] for n in BIG}
        d.update(zip(SMALL, _unpack_small(sml[k], small_shapes)))
        outs.extend(d[n] for n in WEIGHTS)
    return (loss, grad_x[None], *outs)
```

```python
import functools
import math

import jax
import jax.numpy as jnp
from jax import lax
from jax.experimental import pallas as pl
from jax.experimental.pallas import tpu as pltpu

F32 = jnp.float32
BF16 = jnp.bfloat16
EPS = 1e-6
N_META = 16
LANES = 128
HEAD_ROWS = 256
ROW_TILE = 768
N_PAD = HEAD_ROWS - N_META
FOX_DH = 64
HGRN_CHUNK = 64
HGRN_TILE = 384
N_DEV = 8
NEG = -1e30
VMEM_LIMIT = 56 * 1024 * 1024
HI = lax.Precision.HIGHEST

ADAM_LR = 0.001
ADAM_B1 = 0.9
ADAM_B2 = 0.999
ADAM_EPS = 1e-08
ADAM_WD = 0.01
ADAM_STEP = 10

BIG = ("meta_tokens", "fox_w_in", "fox_w_out", "hgrn_w_in", "hgrn_w_out", "ffn_w_in", "ffn_w_out")
SMALL = ("attn_norm", "ffn_norm", "final_norm", "fox_b_f", "fox_q_norm", "fox_k_norm",
         "hgrn_lower_bounds", "hgrn_g_norm")
WEIGHTS = ("meta_tokens", "attn_norm", "ffn_norm", "final_norm", "fox_w_in", "fox_b_f", "fox_q_norm",
           "fox_k_norm", "fox_w_out", "hgrn_w_in", "hgrn_lower_bounds", "hgrn_g_norm", "hgrn_w_out",
           "ffn_w_in", "ffn_w_out")
COL_SHARDED = ("meta_tokens", "fox_w_in", "hgrn_w_in", "ffn_w_in")
LATE = ("hgrn_w_in", "hgrn_w_out", "ffn_w_in", "ffn_w_out")


def _params(sem=None):
    return pltpu.CompilerParams(dimension_semantics=sem, vmem_limit_bytes=VMEM_LIMIT)


def _tile(n, cap):
    best = None
    for t in range(LANES, min(n, cap) + 1, LANES):
        if n % t == 0:
            best = t
    assert best is not None, (n, cap)
    return best


def _row_chunk(n, cap, mult=8):
    best = n
    for t in range(mult, min(n, cap) + 1, mult):
        if n % t == 0:
            best = t
    return best


def _dg(a, b, ca, cb):
    return lax.dot_general(a.astype(BF16), b.astype(BF16), (((ca,), (cb,)), ((), ())),
                           preferred_element_type=F32)


@jax.custom_vjp
def _d_nn(a, b):
    return _dg(a, b, 1, 0)


@jax.custom_vjp
def _d_nt(a, b):
    return _dg(a, b, 1, 1)


@jax.custom_vjp
def _d_tn(a, b):
    return _dg(a, b, 0, 0)


_d_nn.defvjp(lambda a, b: (_d_nn(a, b), (a, b)), lambda r, g: (_d_nt(g, r[1]), _d_tn(r[0], g)))
_d_nt.defvjp(lambda a, b: (_d_nt(a, b), (a, b)), lambda r, g: (_d_nn(g, r[1]), _d_tn(g, r[0])))
_d_tn.defvjp(lambda a, b: (_d_tn(a, b), (a, b)), lambda r, g: (_d_nt(r[1], g), _d_nn(r[0], g)))


def _log_sigmoid(x):
    return jnp.minimum(x, 0.0) - jnp.log1p(jnp.exp(-jnp.abs(x)))


def _rms(x, g):
    return x * lax.rsqrt(jnp.mean(x * x, axis=-1, keepdims=True) + EPS) * g


def _mm(a, b, mode, out_dtype, name, res=None, tm=None, tn=None, tk=None):
    assert a.dtype == BF16 and b.dtype == BF16, (name, a.dtype, b.dtype)
    if mode == "nn":
        (M, K), N = a.shape, b.shape[1]
    elif mode == "nt":
        (M, K), N = a.shape, b.shape[0]
    else:
        (K, M), N = a.shape, b.shape[1]
    if mode == "nn":
        tm, tn, tk = tm or _tile(M, ROW_TILE), tn or _tile(N, 1408), tk or _tile(K, 2816)
    elif mode == "nt":
        tm, tn, tk = tm or _tile(M, ROW_TILE if K <= 2048 else ROW_TILE // 2), tn or N, tk or K
    else:
        tm, tn, tk = tm or _tile(M, 1408), tn or _tile(N, 1408), tk or _tile(K, ROW_TILE)
    nk = K // tk
    if mode == "tn":
        a_spec = pl.BlockSpec((tk, tm), lambda j, i, k: (k, i))
        dims = (((0,), (0,)), ((), ()))
    else:
        a_spec = pl.BlockSpec((tm, tk), lambda j, i, k: (i, k))
        dims = (((1,), (1 if mode == "nt" else 0,)), ((), ()))
    if mode == "nt":
        b_spec = pl.BlockSpec((tn, tk), lambda j, i, k: (j, k))
    else:
        b_spec = pl.BlockSpec((tk, tn), lambda j, i, k: (k, j))

    o_spec = pl.BlockSpec((tm, tn), lambda j, i, k: (i, j))

    def body(a_ref, b_ref, *rest):
        r_ref = rest[0] if res is not None else None
        o_ref, acc_ref = rest[-2:]
        k = pl.program_id(2)

        @pl.when(k == 0)
        def _():
            acc_ref[...] = jnp.zeros_like(acc_ref)

        acc_ref[...] += lax.dot_general(a_ref[...], b_ref[...], dims, preferred_element_type=F32)

        @pl.when(k == nk - 1)
        def _():
            out = acc_ref[...] if r_ref is None else acc_ref[...] + r_ref[...]
            o_ref[...] = out.astype(out_dtype)

    return pl.pallas_call(
        body, name=name, grid=(N // tn, M // tm, nk),
        in_specs=[a_spec, b_spec] + ([o_spec] if res is not None else []),
        out_specs=o_spec,
        out_shape=jax.ShapeDtypeStruct((M, N), out_dtype),
        scratch_shapes=[pltpu.VMEM((tm, tn), F32)],
        compiler_params=_params(("parallel", "parallel", "arbitrary")),
    )(a, b, *([res] if res is not None else []))


def _out_proj_fwd(a, w, res, gain, name):
    Lp, K = a.shape
    D = w.shape[1]
    tm = _tile(Lp, ROW_TILE)

    def body(a_ref, w_ref, r_ref, g_ref, h_ref, hn_ref):
        h = jnp.dot(a_ref[...], w_ref[...], preferred_element_type=F32) + r_ref[...]
        h_ref[...] = h
        hn_ref[...] = _rms(h, g_ref[...]).astype(BF16)

    row = pl.BlockSpec((tm, D), lambda i: (i, 0))
    return pl.pallas_call(
        body, name=name, grid=(Lp // tm,),
        in_specs=[pl.BlockSpec((tm, K), lambda i: (i, 0)), pl.BlockSpec((K, D), lambda i: (0, 0)), row,
                  pl.BlockSpec((1, D), lambda i: (0, 0))],
        out_specs=[row, row],
        out_shape=[jax.ShapeDtypeStruct((Lp, D), F32), jax.ShapeDtypeStruct((Lp, D), BF16)],
        compiler_params=_params(("parallel",)),
    )(a, w, res, gain)


def _in_proj_dx(dy, w, x, gain, dres, name):
    Lp, N = dy.shape
    D = w.shape[0]
    tm = _tile(Lp, ROW_TILE // 2)

    def body(dy_ref, w_ref, x_ref, g_ref, dr_ref, dx_ref, dxb_ref, dg_ref):
        @pl.when(pl.program_id(0) == 0)
        def _():
            dg_ref[...] = jnp.zeros_like(dg_ref)

        dhn = lax.dot_general(dy_ref[...], w_ref[...], (((1,), (1,)), ((), ())), preferred_element_type=F32)
        _, vjp = jax.vjp(_rms, x_ref[...], g_ref[...])
        dx, dg = vjp(dhn)
        dx = dx + dr_ref[...]
        dx_ref[...] = dx
        dxb_ref[...] = dx.astype(BF16)
        dg_ref[...] += dg

    row = pl.BlockSpec((tm, D), lambda i: (i, 0))
    vec = pl.BlockSpec((1, D), lambda i: (0, 0))
    return pl.pallas_call(
        body, name=name, grid=(Lp // tm,),
        in_specs=[pl.BlockSpec((tm, N), lambda i: (i, 0)), pl.BlockSpec((D, N), lambda i: (0, 0)), row, vec, row],
        out_specs=[row, row, vec],
        out_shape=[jax.ShapeDtypeStruct((Lp, D), F32), jax.ShapeDtypeStruct((Lp, D), BF16),
                   jax.ShapeDtypeStruct((1, D), F32)],
        compiler_params=_params(("arbitrary",)),
    )(dy, w, x, gain, dres)


def _rms_fwd(x, g, T, name):
    Lp, D = x.shape

    def body(x_ref, g_ref, o_ref):
        o_ref[...] = _rms(x_ref[...], g_ref[...]).astype(BF16)

    return pl.pallas_call(
        body, name=name, grid=(Lp // T,),
        in_specs=[pl.BlockSpec((T, D), lambda i: (i, 0)), pl.BlockSpec((1, D), lambda i: (0, 0))],
        out_specs=pl.BlockSpec((T, D), lambda i: (i, 0)),
        out_shape=jax.ShapeDtypeStruct((Lp, D), BF16),
        compiler_params=_params(("parallel",)),
    )(x, g)


def _swiglu(gate, up):
    return gate * jax.nn.sigmoid(gate) * up


def _ffn_in_fwd(hf, w_in, name):
    Lp, D = hf.shape
    F = w_in.shape[1] // 2
    tm = _tile(Lp, ROW_TILE)
    tn = _tile(F, 1408)
    nj = F // tn

    def body(a_ref, bg_ref, bu_ref, g_ref, u_ref, act_ref):
        a = a_ref[...]
        g = jnp.dot(a, bg_ref[...], preferred_element_type=F32)
        u = jnp.dot(a, bu_ref[...], preferred_element_type=F32)
        g_ref[...] = g
        u_ref[...] = u
        act_ref[...] = _swiglu(g, u).astype(BF16)

    tile = pl.BlockSpec((tm, tn), lambda j, i: (i, j))
    return pl.pallas_call(
        body, name=name, grid=(nj, Lp // tm),
        in_specs=[pl.BlockSpec((tm, D), lambda j, i: (i, 0)), pl.BlockSpec((D, tn), lambda j, i: (0, j)),
                  pl.BlockSpec((D, tn), lambda j, i: (0, nj + j))],
        out_specs=[tile, tile, tile],
        out_shape=[jax.ShapeDtypeStruct((Lp, F), F32), jax.ShapeDtypeStruct((Lp, F), F32),
                   jax.ShapeDtypeStruct((Lp, F), BF16)],
        compiler_params=_params(("parallel", "parallel")),
    )(hf, w_in, w_in)


def _ffn_out_dx(dhb, w_out, g, u, name):
    Lp, D = dhb.shape
    F = w_out.shape[0]
    tm = HEAD_ROWS

    def body(a_ref, b_ref, g_ref, u_ref, o_ref):
        dact = lax.dot_general(a_ref[...], b_ref[...], (((1,), (1,)), ((), ())), preferred_element_type=F32)
        _, vjp = jax.vjp(_swiglu, g_ref[...], u_ref[...])
        dg, du = vjp(dact)
        o_ref[:, :F] = dg.astype(BF16)
        o_ref[:, F:] = du.astype(BF16)

    wide = pl.BlockSpec((tm, F), lambda i: (i, 0))
    return pl.pallas_call(
        body, name=name, grid=(Lp // tm,),
        in_specs=[pl.BlockSpec((tm, D), lambda i: (i, 0)), pl.BlockSpec((F, D), lambda i: (0, 0)), wide, wide],
        out_specs=pl.BlockSpec((tm, 2 * F), lambda i: (i, 0)),
        out_shape=jax.ShapeDtypeStruct((Lp, 2 * F), BF16),
        compiler_params=_params(("parallel",)),
    )(dhb, w_out, g, u)


def _final_loss(h, g, target, name):
    Lp, D = h.shape
    TR = HEAD_ROWS

    def loss_fn(hh, gg, tt):
        err = _rms(hh, gg) - tt
        return 0.5 * jnp.sum(jnp.mean(err * err, axis=-1))

    def body(h_ref, g_ref, t_ref, loss_ref, dh_ref, dhb_ref, dg_ref):
        i = pl.program_id(0)

        @pl.when(i == 0)
        def _():
            loss_ref[...] = jnp.zeros_like(loss_ref)
            dg_ref[...] = jnp.zeros_like(dg_ref)
            dh_ref[...] = jnp.zeros_like(dh_ref)
            dhb_ref[...] = jnp.zeros_like(dhb_ref)

        @pl.when(i > 0)
        def _():
            val, vjp = jax.vjp(lambda hh, gg: loss_fn(hh, gg, t_ref[...]), h_ref[...], g_ref[...])
            dh, dg = vjp(jnp.ones((), F32))
            dh_ref[...] = dh
            dhb_ref[...] = dh.astype(BF16)
            dg_ref[...] += dg
            loss_ref[...] += val

    row = pl.BlockSpec((TR, D), lambda i: (i, 0))
    return pl.pallas_call(
        body, name=name, grid=(Lp // TR,),
        in_specs=[row, pl.BlockSpec((1, D), lambda i: (0, 0)),
                  pl.BlockSpec((TR, D), lambda i: (jnp.maximum(i - 1, 0), 0))],
        out_specs=[pl.BlockSpec((8, LANES), lambda i: (0, 0)), row, row, pl.BlockSpec((1, D), lambda i: (0, 0))],
        out_shape=[jax.ShapeDtypeStruct((8, LANES), F32), jax.ShapeDtypeStruct((Lp, D), F32),
                   jax.ShapeDtypeStruct((Lp, D), BF16), jax.ShapeDtypeStruct((1, D), F32)],
        compiler_params=_params(("arbitrary",)),
    )(h, g, target)


def _lane_lo():
    return lax.broadcasted_iota(jnp.int32, (1, LANES), 1) < FOX_DH


def _headnorm(x, g, scale):
    lo = _lane_lo()
    x2 = x * x
    s0 = jnp.sum(jnp.where(lo, x2, 0.0), axis=-1, keepdims=True)
    s1 = jnp.sum(jnp.where(lo, 0.0, x2), axis=-1, keepdims=True)
    r = jnp.where(lo, lax.rsqrt(s0 / FOX_DH + EPS), lax.rsqrt(s1 / FOX_DH + EPS))
    return x * r * g * scale


def _fox_prep_fwd(proj, bf, qg, kg, T, D, name):
    Lp = proj.shape[0]
    nb = D // LANES
    scale = FOX_DH ** -0.5

    def body(q_ref, k_ref, v_ref, fl_ref, bf_ref, qg_ref, kg_ref, qn_ref, kn_ref, vb_ref, c_ref, carry_ref):
        @pl.when(pl.program_id(0) == 0)
        def _():
            carry_ref[...] = jnp.zeros_like(carry_ref)

        for b in range(nb):
            sl = slice(b * LANES, (b + 1) * LANES)
            qn_ref[:, sl] = _headnorm(q_ref[:, sl], qg_ref[...], scale).astype(BF16)
            kn_ref[:, sl] = _headnorm(k_ref[:, sl], kg_ref[...], 1.0).astype(BF16)
        vb_ref[...] = v_ref[...].astype(BF16)
        log_f = _log_sigmoid(fl_ref[...] + bf_ref[...])
        row = lax.broadcasted_iota(jnp.int32, (T, T), 0)
        col = lax.broadcasted_iota(jnp.int32, (T, T), 1)
        tri = (col <= row).astype(F32)
        c = jnp.dot(tri, log_f, precision=HI, preferred_element_type=F32) + carry_ref[...]
        c_ref[...] = c
        last = lax.broadcasted_iota(jnp.int32, (T, 1), 0) == T - 1
        carry_ref[...] = jnp.sum(jnp.where(last, c, 0.0), axis=0, keepdims=True)

    wide = lambda j: pl.BlockSpec((T, D), lambda i: (i, j))
    vec = pl.BlockSpec((1, LANES), lambda i: (0, 0))
    return pl.pallas_call(
        body, name=name, grid=(Lp // T,),
        in_specs=[wide(0), wide(1), wide(2), pl.BlockSpec((T, LANES), lambda i: (i, 4 * nb)), vec, vec, vec],
        out_specs=[wide(0), wide(0), wide(0), pl.BlockSpec((T, LANES), lambda i: (i, 0))],
        out_shape=[jax.ShapeDtypeStruct((Lp, D), BF16)] * 3 + [jax.ShapeDtypeStruct((Lp, LANES), F32)],
        scratch_shapes=[pltpu.VMEM((1, LANES), F32)],
        compiler_params=_params(("arbitrary",)),
    )(proj, proj, proj, proj, bf, qg, kg)


def _fox_prep_bwd(proj, bf, qg, kg, dqn, dkn, dc, T, D, name):
    Lp = proj.shape[0]
    nb = D // LANES
    nt = Lp // T
    scale = FOX_DH ** -0.5

    def body(q_ref, k_ref, fl_ref, bf_ref, qg_ref, kg_ref, dqn_ref, dkn_ref, dc_ref,
             dq_ref, dk_ref, dfl_ref, sm_ref, carry_ref):
        @pl.when(pl.program_id(0) == 0)
        def _():
            carry_ref[...] = jnp.zeros_like(carry_ref)
            sm_ref[...] = jnp.zeros_like(sm_ref)

        dqg = jnp.zeros((1, LANES), F32)
        dkg = jnp.zeros((1, LANES), F32)
        for b in range(nb):
            sl = slice(b * LANES, (b + 1) * LANES)
            _, vjp = jax.vjp(lambda x, g: _headnorm(x, g, scale), q_ref[:, sl], qg_ref[...])
            dx, dg = vjp(dqn_ref[:, sl])
            dq_ref[:, sl] = dx.astype(BF16)
            dqg = dqg + dg
            _, vjp = jax.vjp(lambda x, g: _headnorm(x, g, 1.0), k_ref[:, sl], kg_ref[...])
            dx, dg = vjp(dkn_ref[:, sl])
            dk_ref[:, sl] = dx.astype(BF16)
            dkg = dkg + dg
        dcv = dc_ref[...]
        row = lax.broadcasted_iota(jnp.int32, (T, T), 0)
        col = lax.broadcasted_iota(jnp.int32, (T, T), 1)
        triu = (col >= row).astype(F32)
        dlogf = jnp.dot(triu, dcv, precision=HI, preferred_element_type=F32) + carry_ref[...]
        carry_ref[...] += jnp.sum(dcv, axis=0, keepdims=True)
        _, vjp = jax.vjp(_log_sigmoid, fl_ref[...] + bf_ref[...])
        (dfl,) = vjp(dlogf)
        dfl_ref[...] = dfl.astype(BF16)
        sm_ref[0:1, :] += jnp.sum(dfl, axis=0, keepdims=True)
        sm_ref[1:2, :] += dqg
        sm_ref[2:3, :] += dkg

    wide = lambda j: pl.BlockSpec((T, D), lambda i: (nt - 1 - i, j))
    narrow = lambda j: pl.BlockSpec((T, LANES), lambda i: (nt - 1 - i, j))
    vec = pl.BlockSpec((1, LANES), lambda i: (0, 0))
    return pl.pallas_call(
        body, name=name, grid=(nt,),
        in_specs=[wide(0), wide(1), narrow(4 * nb), vec, vec, vec, wide(0), wide(0), narrow(0)],
        out_specs=[wide(0), wide(0), narrow(0), pl.BlockSpec((8, LANES), lambda i: (0, 0))],
        out_shape=[jax.ShapeDtypeStruct((Lp, D), BF16)] * 2 + [jax.ShapeDtypeStruct((Lp, LANES), BF16),
                                                                 jax.ShapeDtypeStruct((8, LANES), F32)],
        scratch_shapes=[pltpu.VMEM((1, LANES), F32)],
        compiler_params=_params(("arbitrary",)),
    )(proj, proj, proj, bf, qg, kg, dqn, dkn, dc)


def _ln2_ceil(m):
    return jnp.ceil(m * (1.0 / math.log(2.0))) * math.log(2.0)


def _fox_mask(i, k0, T):
    qpos = i * T + lax.broadcasted_iota(jnp.int32, (T, 1), 0)
    kpos = k0 + lax.broadcasted_iota(jnp.int32, (1, T), 1)
    return (kpos <= qpos) & ((kpos >= N_PAD) | (qpos < N_PAD))


def _pick_col(blk, idx):
    lane = lax.broadcasted_iota(jnp.int32, (1, LANES), 1)
    return jnp.sum(jnp.where(lane == idx, blk, 0.0), axis=1, keepdims=True)


def _split_halves(blk):
    lo = _lane_lo()
    return (jnp.max(jnp.where(lo, blk, -jnp.inf), axis=1, keepdims=True),
            jnp.max(jnp.where(lo, -jnp.inf, blk), axis=1, keepdims=True))


def _fox_attn_fwd(qn, kn, vb, c, cT, proj, xchg, T, D, name):
    Lp = qn.shape[0]
    P = D // LANES
    nt = Lp // T
    H = cT.shape[0]
    nx = len(xchg)

    def body(q_ref, k_ref, v_ref, c_ref, cT_ref, g_ref, *rest):
        x_in, (o_ref, og_ref, m_ref, li_ref), x_out, sems = rest[:nx], rest[nx:nx + 4], rest[nx + 4:2 * nx + 4], rest[2 * nx + 4:]
        p = pl.program_id(0)
        i = pl.program_id(1)

        @pl.when((p == 0) & (i == 0))
        def _():
            for cp in _xchg_copies(x_in, x_out, [False] * nx, sems):
                cp.start()

        lo = _lane_lo()
        q = q_ref[...]
        zero = jnp.zeros_like(q)
        qh = (jnp.where(lo, q, zero), jnp.where(lo, zero, q))
        cblk = c_ref[...]
        cq = tuple(_pick_col(cblk, 2 * p + h) for h in (0, 1))
        one = jnp.ones_like(q)

        def step(j, carry, masked):
            k0 = pl.multiple_of(j * T, LANES)
            kj = k_ref[pl.ds(k0, T), :]
            vj = v_ref[pl.ds(k0, T), :]
            vh = (jnp.where(lo, vj, one), jnp.where(lo, one, vj))
            mask = _fox_mask(i, k0, T) if masked else None
            out = []
            for h in (0, 1):
                m, acc = carry[h]
                ck = cT_ref[pl.ds(2 * p + h, 1), pl.ds(k0, T)]
                t = lax.dot_general(qh[h], kj, (((1,), (1,)), ((), ())), preferred_element_type=F32) - ck
                if masked:
                    t = jnp.where(mask, t, NEG)
                m_new = _ln2_ceil(jnp.maximum(m, cq[h] + jnp.max(t, axis=1, keepdims=True)))
                pr = jnp.exp(t + (cq[h] - m_new)).astype(BF16)
                acc = jnp.exp(m - m_new) * acc + jnp.dot(pr, vh[h], preferred_element_type=F32)
                out.append((m_new, acc))
            return tuple(out)

        init = tuple((jnp.full((T, 1), NEG, F32), jnp.zeros((T, LANES), F32)) for _ in (0, 1))
        carry = step(0, init, True)
        carry = lax.fori_loop(1, i, lambda j, cr: step(j, cr, False), carry)
        (m0, a0), (m1, a1) = lax.cond(i > 0, lambda cr: step(i, cr, True), lambda cr: cr, carry)
        l0 = pltpu.roll(a0, FOX_DH, 1)
        l1 = pltpu.roll(a1, FOX_DH, 1)
        o = jnp.where(lo, a0 / l0, a1 / l1)
        o_ref[...] = o
        m_ref[...] = jnp.where(lo, m0, m1)
        li_ref[...] = jnp.where(lo, 1.0 / l0, 1.0 / l1)
        og_ref[...] = (o * jax.nn.sigmoid(g_ref[...])).astype(BF16)

        @pl.when((p == P - 1) & (i == nt - 1))
        def _():
            for cp in _xchg_copies(x_in, x_out, [False] * nx, sems):
                cp.wait()

    tile = pl.BlockSpec((T, LANES), lambda p, i: (i, p))
    full = pl.BlockSpec((Lp, LANES), lambda p, i: (0, p))
    HBM = pl.BlockSpec(memory_space=pltpu.HBM)
    return pl.pallas_call(
        body, name=name, grid=(P, nt),
        in_specs=[tile, full, full, pl.BlockSpec((T, LANES), lambda p, i: (i, 0)),
                  pl.BlockSpec((H, Lp), lambda p, i: (0, 0)),
                  pl.BlockSpec((T, LANES), lambda p, i: (i, 3 * P + p))] + [HBM] * nx,
        out_specs=[tile, tile, tile, tile] + [HBM] * nx,
        out_shape=[jax.ShapeDtypeStruct((Lp, D), F32), jax.ShapeDtypeStruct((Lp, D), BF16),
                   jax.ShapeDtypeStruct((Lp, D), F32), jax.ShapeDtypeStruct((Lp, D), F32)]
        + _xchg_out_shapes(xchg, [False] * nx),
        scratch_shapes=_xchg_sems(nx),
        compiler_params=_params(("arbitrary", "arbitrary")),
    )(qn, kn, vb, c, cT, proj, *xchg)


def _fox_out_dx(dhb, w_out, o, proj, linv, D, name):
    Lp = o.shape[0]
    tm = HEAD_ROWS

    def body(a_ref, w_ref, o_ref, g_ref, li_ref, do_ref, dg_ref, dl_ref):
        lo = _lane_lo()
        dog_all = lax.dot_general(a_ref[...], w_ref[...], (((1,), (1,)), ((), ())), preferred_element_type=F32)
        for b in range(D // LANES):
            sl = slice(b * LANES, (b + 1) * LANES)
            dog = dog_all[:, sl]
            sig = jax.nn.sigmoid(g_ref[:, sl])
            ov = o_ref[:, sl]
            do = (dog * sig * li_ref[:, sl]).astype(BF16)
            do_ref[:, sl] = do
            dg_ref[:, sl] = (dog * ov * sig * (1.0 - sig)).astype(BF16)
            t = do.astype(F32) * ov
            d0 = jnp.sum(jnp.where(lo, t, 0.0), axis=1, keepdims=True)
            d1 = jnp.sum(jnp.where(lo, 0.0, t), axis=1, keepdims=True)
            dl_ref[:, sl] = jnp.where(lo, d0, d1)

    row = pl.BlockSpec((tm, D), lambda i: (i, 0))
    return pl.pallas_call(
        body, name=name, grid=(Lp // tm,),
        in_specs=[row, pl.BlockSpec((D, D), lambda i: (0, 0)), row, pl.BlockSpec((tm, D), lambda i: (i, 3)), row],
        out_specs=[row, row, row],
        out_shape=[jax.ShapeDtypeStruct((Lp, D), BF16), jax.ShapeDtypeStruct((Lp, D), BF16),
                   jax.ShapeDtypeStruct((Lp, D), F32)],
        compiler_params=_params(("parallel",)),
    )(dhb, w_out, o, proj, linv)


def _fox_attn_bwd(qn, kn, vb, c, cT, do, mshift, delta, xchg, T, D, name):
    Lp = qn.shape[0]
    P = D // LANES
    nt = Lp // T
    H = cT.shape[0]
    nx = len(xchg)

    def body(q_ref, do_ref, m_ref, dl_ref, c_ref, k_ref, v_ref, cT_ref, *rest):
        x_in, (dq_ref, dk_ref, dv_ref, dc_ref), x_out, sems = rest[:nx], rest[nx:nx + 4], rest[nx + 4:2 * nx + 4], rest[2 * nx + 4:]
        p = pl.program_id(0)
        i = pl.program_id(1)

        @pl.when((p == 0) & (i == 0))
        def _():
            for cp in _xchg_copies(x_in, x_out, [True] * nx, sems):
                cp.start()

        @pl.when(i == 0)
        def _():
            dk_ref[...] = jnp.zeros_like(dk_ref)
            dv_ref[...] = jnp.zeros_like(dv_ref)
            dc_ref[...] = jnp.zeros_like(dc_ref)

        lo = _lane_lo()
        q = q_ref[...]
        do = do_ref[...]
        zero = jnp.zeros_like(q)
        qh = (jnp.where(lo, q, zero), jnp.where(lo, zero, q))
        doh = (jnp.where(lo, do, zero), jnp.where(lo, zero, do))
        msh = _split_halves(m_ref[...])
        dlt = _split_halves(dl_ref[...])
        cblk = c_ref[...]
        shift = tuple(_pick_col(cblk, 2 * p + h) - msh[h] for h in (0, 1))

        def step(j, carry, masked):
            k0 = pl.multiple_of(j * T, LANES)
            kj = k_ref[pl.ds(k0, T), :]
            vj = v_ref[pl.ds(k0, T), :]
            mask = _fox_mask(i, k0, T) if masked else None
            dqs, dks, dvs = [], [], []
            for h in (0, 1):
                ck = cT_ref[pl.ds(2 * p + h, 1), pl.ds(k0, T)]
                t = lax.dot_general(qh[h], kj, (((1,), (1,)), ((), ())), preferred_element_type=F32) - ck
                if masked:
                    t = jnp.where(mask, t, NEG)
                pb = jnp.exp(t + shift[h]).astype(BF16)
                dp = lax.dot_general(doh[h], vj, (((1,), (1,)), ((), ())), preferred_element_type=F32)
                ds = pb.astype(F32) * (dp - dlt[h])
                dsb = ds.astype(BF16)
                dqs.append(carry[h] + jnp.dot(dsb, kj, preferred_element_type=F32))
                dks.append(lax.dot_general(dsb, q, (((0,), (0,)), ((), ())), preferred_element_type=F32))
                dvs.append(lax.dot_general(pb, do, (((0,), (0,)), ((), ())), preferred_element_type=F32))
                dc_ref[0, h:h + 1, pl.ds(k0, T)] += -jnp.sum(ds, axis=0, keepdims=True)
            dk_ref[pl.ds(k0, T), :] += jnp.where(lo, dks[0], dks[1])
            dv_ref[pl.ds(k0, T), :] += jnp.where(lo, dvs[0], dvs[1])
            return tuple(dqs)

        init = (jnp.zeros((T, LANES), F32), jnp.zeros((T, LANES), F32))
        carry = step(0, init, True)
        carry = lax.fori_loop(1, i, lambda j, cr: step(j, cr, False), carry)
        dq0, dq1 = lax.cond(i > 0, lambda cr: step(i, cr, True), lambda cr: cr, carry)
        dq_ref[...] = jnp.where(lo, dq0, dq1)

        @pl.when((p == P - 1) & (i == nt - 1))
        def _():
            for cp in _xchg_copies(x_in, x_out, [True] * nx, sems):
                cp.wait()

    tile = pl.BlockSpec((T, LANES), lambda p, i: (i, p))
    full = pl.BlockSpec((Lp, LANES), lambda p, i: (0, p))
    HBM = pl.BlockSpec(memory_space=pltpu.HBM)
    return pl.pallas_call(
        body, name=name, grid=(P, nt),
        in_specs=[tile, tile, tile, tile, pl.BlockSpec((T, LANES), lambda p, i: (i, 0)), full, full,
                  pl.BlockSpec((H, Lp), lambda p, i: (0, 0))] + [HBM] * nx,
        out_specs=[tile, full, full, pl.BlockSpec((1, 8, Lp), lambda p, i: (p, 0, 0))] + [HBM] * nx,
        out_shape=[jax.ShapeDtypeStruct((Lp, D), F32)] * 3 + [jax.ShapeDtypeStruct((P, 8, Lp), F32)]
        + _xchg_out_shapes(xchg, [True] * nx),
        scratch_shapes=_xchg_sems(nx),
        compiler_params=_params(("arbitrary", "arbitrary")),
    )(qn, do, mshift, delta, c, kn, vb, cT, *xchg)


def _scan_rows(x, reverse):
    C = x.shape[0]
    row = lax.broadcasted_iota(jnp.int32, (C, 1), 0)
    step = 1
    while step < C:
        if reverse:
            x = x + jnp.where(row < C - step, pltpu.roll(x, C - step, 0), 0.0)
        else:
            x = x + jnp.where(row >= step, pltpu.roll(x, step, 0), 0.0)
        step *= 2
    return x


@jax.custom_vjp
def _cumsum_rows(x):
    return _scan_rows(x, False)


_cumsum_rows.defvjp(lambda x: (_scan_rows(x, False), None), lambda _, g: (_scan_rows(g, True),))


def _hgrn_chunk(St, qr, z, vi, go, p0, p1, gg):
    C = qr.shape[0]
    lb = jax.nn.sigmoid(p1 - p0)
    a = jnp.log(lb)
    cc = jnp.log1p(-lb) + _log_sigmoid(z)
    log_f = jnp.maximum(a, cc) + jnp.log1p(jnp.exp(-jnp.abs(a - cc)))
    k = (1.0 - lb) * jax.nn.sigmoid(-z)
    q = qr * jax.nn.sigmoid(qr)
    row = lax.broadcasted_iota(jnp.int32, (C, C), 0)
    col = lax.broadcasted_iota(jnp.int32, (C, C), 1)
    causal = col <= row
    b = _cumsum_rows(log_f)
    mid = lax.broadcasted_iota(jnp.int32, (C, 1), 0) == C // 2 - 1
    r = jnp.sum(jnp.where(mid, b, 0.0), axis=0, keepdims=True)
    b_last = jnp.sum(log_f, axis=0, keepdims=True)
    attn = jnp.where(causal, _d_nt(q * jnp.exp(b - r), k * jnp.exp(r - b)), 0.0)
    o = _d_nn(attn, vi) + _d_nt(q * jnp.exp(b), St)
    St_new = St * jnp.exp(b_last) + _d_tn(vi, k * jnp.exp(b_last - b))
    og = _rms(o, gg) * (go * jax.nn.sigmoid(go))
    return St_new, og


def _hgrn_heads_per_step(H):
    return 8 if H % 8 == 0 else 4 if H % 4 == 0 else 1


def _hgrn_specs(T, W, nhb, rev_nt=None):
    if rev_nt is None:
        return [pl.BlockSpec((T, W), functools.partial(lambda hb, t, g: (t, g * nhb + hb), g=g)) for g in range(4)]
    return [pl.BlockSpec((T, W), functools.partial(lambda hb, t, g: (rev_nt - 1 - t, g * nhb + hb), g=g))
            for g in range(4)]


def _hgrn_fwd(proj, lbp, gg, T, name):
    Lp = proj.shape[0]
    D = proj.shape[1] // 4
    H = D // LANES
    hps = _hgrn_heads_per_step(H)
    W = hps * LANES
    nhb = H // hps
    nt = Lp // T
    ncc = T // HGRN_CHUNK

    def body(q_ref, z_ref, i_ref, go_ref, p_ref, gg_ref, og_ref, ss_ref, st_ref):
        @pl.when(pl.program_id(1) == 0)
        def _():
            st_ref[...] = jnp.zeros_like(st_ref)

        gain = gg_ref[...]

        def chunk(cidx, states):
            sl = pl.ds(pl.multiple_of(cidx * HGRN_CHUNK, HGRN_CHUNK), HGRN_CHUNK)
            new = []
            for hh in range(hps):
                ln = slice(hh * LANES, (hh + 1) * LANES)
                ss_ref[hh, cidx] = states[hh]
                St_new, og = _hgrn_chunk(states[hh], q_ref[sl, ln], z_ref[sl, ln], i_ref[sl, ln], go_ref[sl, ln],
                                         p_ref[0:1, ln], p_ref[1:2, ln], gain)
                og_ref[sl, ln] = og.astype(BF16)
                new.append(St_new)
            return tuple(new)

        states = lax.fori_loop(0, ncc, chunk, tuple(st_ref[hh] for hh in range(hps)))
        for hh in range(hps):
            st_ref[hh] = states[hh]

    return pl.pallas_call(
        body, name=name, grid=(nhb, nt),
        in_specs=_hgrn_specs(T, W, nhb) + [pl.BlockSpec((2, W), lambda hb, t: (0, hb)),
                                           pl.BlockSpec((1, LANES), lambda hb, t: (0, 0))],
        out_specs=[pl.BlockSpec((T, W), lambda hb, t: (t, hb)),
                   pl.BlockSpec((hps, ncc, LANES, LANES), lambda hb, t: (hb, t, 0, 0))],
        out_shape=[jax.ShapeDtypeStruct((Lp, D), BF16),
                   jax.ShapeDtypeStruct((H, Lp // HGRN_CHUNK, LANES, LANES), F32)],
        scratch_shapes=[pltpu.VMEM((hps, LANES, LANES), F32)],
        compiler_params=_params(("parallel", "arbitrary")),
    )(proj, proj, proj, proj, lbp, gg)


def _hgrn_bwd(proj, lbp, gg, dog, ss, T, name):
    Lp = proj.shape[0]
    D = proj.shape[1] // 4
    H = D // LANES
    hps = _hgrn_heads_per_step(H)
    W = hps * LANES
    nhb = H // hps
    nt = Lp // T
    ncc = T // HGRN_CHUNK

    def body(q_ref, z_ref, i_ref, go_ref, p_ref, gg_ref, dog_ref, ss_ref,
             dq_ref, dz_ref, di_ref, dgo_ref, dp_ref, dgg_ref, dst_ref):
        hb = pl.program_id(0)
        t = pl.program_id(1)

        @pl.when(t == 0)
        def _():
            dst_ref[...] = jnp.zeros_like(dst_ref)
            dp_ref[...] = jnp.zeros_like(dp_ref)

        @pl.when((t == 0) & (hb == 0))
        def _():
            dgg_ref[...] = jnp.zeros_like(dgg_ref)

        gain = gg_ref[...]
        row0 = (nt - 1 - t) * T

        def chunk(cc, carry):
            dstates, dps, dgain_sum = carry
            cidx = ncc - 1 - cc
            r0 = pl.multiple_of(cidx * HGRN_CHUNK, HGRN_CHUNK)
            sl = pl.ds(r0, HGRN_CHUNK)
            real = (row0 + r0 + lax.broadcasted_iota(jnp.int32, (HGRN_CHUNK, 1), 0)) >= N_PAD
            new_d, new_p = [], []
            for hh in range(hps):
                ln = slice(hh * LANES, (hh + 1) * LANES)
                _, vjp = jax.vjp(_hgrn_chunk, ss_ref[hh, cidx], q_ref[sl, ln], z_ref[sl, ln], i_ref[sl, ln],
                                 go_ref[sl, ln], p_ref[0:1, ln], p_ref[1:2, ln], gain)
                dSt, dq, dz, di, dgo, dp0, dp1, dgain = vjp((dstates[hh], dog_ref[sl, ln]))
                dq_ref[sl, ln] = jnp.where(real, dq, 0.0).astype(BF16)
                dz_ref[sl, ln] = jnp.where(real, dz, 0.0).astype(BF16)
                di_ref[sl, ln] = jnp.where(real, di, 0.0).astype(BF16)
                dgo_ref[sl, ln] = jnp.where(real, dgo, 0.0).astype(BF16)
                new_d.append(dSt)
                new_p.append((dps[hh][0] + dp0, dps[hh][1] + dp1))
                dgain_sum = dgain_sum + dgain
            return tuple(new_d), tuple(new_p), dgain_sum

        zero_row = jnp.zeros((1, LANES), F32)
        init = (tuple(dst_ref[hh] for hh in range(hps)), tuple((zero_row, zero_row) for _ in range(hps)), zero_row)
        dstates, dps, dgain_sum = lax.fori_loop(0, ncc, chunk, init)
        for hh in range(hps):
            ln = slice(hh * LANES, (hh + 1) * LANES)
            dst_ref[hh] = dstates[hh]
            dp_ref[0:1, ln] += dps[hh][0]
            dp_ref[1:2, ln] += dps[hh][1]
        dgg_ref[0:1, :] += dgain_sum

    rev = pl.BlockSpec((T, W), lambda hb, t: (nt - 1 - t, hb))
    return pl.pallas_call(
        body, name=name, grid=(nhb, nt),
        in_specs=_hgrn_specs(T, W, nhb, nt) + [pl.BlockSpec((2, W), lambda hb, t: (0, hb)),
                                               pl.BlockSpec((1, LANES), lambda hb, t: (0, 0)), rev,
                                               pl.BlockSpec((hps, ncc, LANES, LANES),
                                                            lambda hb, t: (hb, nt - 1 - t, 0, 0))],
        out_specs=[rev, rev, rev, rev, pl.BlockSpec((8, W), lambda hb, t: (0, hb)),
                   pl.BlockSpec((8, LANES), lambda hb, t: (0, 0))],
        out_shape=[jax.ShapeDtypeStruct((Lp, D), BF16)] * 4 + [jax.ShapeDtypeStruct((8, D), F32),
                                                                 jax.ShapeDtypeStruct((8, LANES), F32)],
        scratch_shapes=[pltpu.VMEM((hps, LANES, LANES), F32)],
        compiler_params=_params(("arbitrary", "arbitrary")),
    )(proj, proj, proj, proj, lbp, gg, dog, ss)


def _xchg_sems(n_arr):
    return [pltpu.SemaphoreType.DMA((n_arr * (N_DEV - 1),)), pltpu.SemaphoreType.DMA((n_arr * (N_DEV - 1),)),
            pltpu.SemaphoreType.DMA((n_arr,))]


def _xchg_copies(ins, outs, per_peer, sems):
    send_sems, recv_sems, local_sems = sems
    x, y, c = lax.axis_index("x"), lax.axis_index("y"), lax.axis_index("c")
    me = 4 * x + 2 * y + c
    copies = []
    for n in range(len(ins)):
        src = ins[n].at[me] if per_peer[n] else ins[n]
        copies.append(pltpu.make_async_copy(src, outs[n].at[me], local_sems.at[n]))
    for rel in range(1, N_DEV):
        fx, fy, fc = (rel >> 2) & 1, (rel >> 1) & 1, rel & 1
        px = 1 - x if fx else x
        py = 1 - y if fy else y
        pc = 1 - c if fc else c
        peer = 4 * px + 2 * py + pc
        for n in range(len(ins)):
            src = ins[n].at[peer] if per_peer[n] else ins[n]
            copies.append(pltpu.make_async_remote_copy(
                src_ref=src, dst_ref=outs[n].at[me],
                send_sem=send_sems.at[n * (N_DEV - 1) + rel - 1],
                recv_sem=recv_sems.at[n * (N_DEV - 1) + rel - 1],
                device_id=(px, py, pc), device_id_type=pl.DeviceIdType.MESH))
    return copies


def _xchg_out_shapes(arrays, per_peer):
    return [jax.ShapeDtypeStruct(a.shape if pp else (N_DEV,) + a.shape, a.dtype) for a, pp in zip(arrays, per_peer)]


def _exchange(arrays, per_peer, name):
    n_arr = len(arrays)
    HBM = pl.BlockSpec(memory_space=pltpu.HBM)

    def body(*refs):
        copies = _xchg_copies(refs[:n_arr], refs[n_arr:2 * n_arr], per_peer, refs[2 * n_arr:])
        for cp in copies:
            cp.start()
        for cp in copies:
            cp.wait()

    return pl.pallas_call(
        body, name=name,
        in_specs=[HBM] * n_arr, out_specs=[HBM] * n_arr, out_shape=_xchg_out_shapes(arrays, per_peer),
        scratch_shapes=_xchg_sems(n_arr),
    )(*arrays)


ADAMW_VMEM_BUDGET = 36 * 1024 * 1024


def _adamw(recv, w, m, v, name):
    shape = w.shape
    C = shape[-1]
    R = math.prod(shape[:-1])
    lanes = -(-C // LANES) * LANES
    row_bytes = 2 * lanes * (N_DEV * recv.dtype.itemsize + 7 * 4)
    rc = _row_chunk(R, max(16, ADAMW_VMEM_BUDGET // row_bytes), 16 if recv.dtype == BF16 else 8)

    def body(r_ref, w_ref, m_ref, v_ref, g_ref, d_ref, mo_ref, vo_ref):
        g = r_ref[0].astype(F32)
        for s in range(1, N_DEV):
            g = g + r_ref[s].astype(F32)
        mn = ADAM_B1 * m_ref[...] + (1.0 - ADAM_B1) * g
        vn = ADAM_B2 * v_ref[...] + (1.0 - ADAM_B2) * (g * g)
        m_hat = mn / (1.0 - ADAM_B1 ** ADAM_STEP)
        v_hat = vn / (1.0 - ADAM_B2 ** ADAM_STEP)
        g_ref[...] = g
        d_ref[...] = -ADAM_LR * (m_hat / (jnp.sqrt(v_hat) + ADAM_EPS) + ADAM_WD * w_ref[...])
        mo_ref[...] = mn
        vo_ref[...] = vn

    row = pl.BlockSpec((rc, C), lambda i: (i, 0))
    outs = pl.pallas_call(
        body, name=name, grid=(R // rc,),
        in_specs=[pl.BlockSpec((N_DEV, rc, C), lambda i: (0, i, 0)), row, row, row],
        out_specs=[row] * 4,
        out_shape=[jax.ShapeDtypeStruct((R, C), F32)] * 4,
        compiler_params=_params(("parallel",)),
    )(recv.reshape(N_DEV, R, C), w.reshape(R, C), m.reshape(R, C), v.reshape(R, C))
    return [o.reshape(shape) for o in outs]


def _gathered_to_full(g, name):
    if name in COL_SHARDED:
        g = jnp.moveaxis(g, 0, -2)
        return g.reshape(g.shape[:-2] + (g.shape[-2] * g.shape[-1],))
    g = jnp.moveaxis(g, 0, -3)
    return g.reshape(g.shape[:-3] + (g.shape[-3] * g.shape[-2], g.shape[-1]))


def _full_to_slabs(full, name):
    if name in COL_SHARDED:
        f = full.reshape(full.shape[:-1] + (N_DEV, full.shape[-1] // N_DEV))
        return jnp.moveaxis(f, -2, 0)
    f = full.reshape(full.shape[:-2] + (N_DEV, full.shape[-2] // N_DEV, full.shape[-1]))
    return jnp.moveaxis(f, -3, 0)


def _pack_small(arrs):
    rows = []
    for a in arrs:
        flat = a.astype(F32).reshape(-1)
        pad = (-flat.shape[0]) % LANES
        rows.append(jnp.pad(flat, (0, pad)).reshape(-1, LANES))
    p = jnp.concatenate(rows, axis=0)
    return jnp.pad(p, ((0, (-p.shape[0]) % 8), (0, 0)))


def _unpack_small(packed, shapes):
    out, off = [], 0
    for shp in shapes:
        n = math.prod(shp)
        nr = -(-n // LANES)
        out.append(packed[off:off + nr].reshape(-1)[:n].reshape(shp))
        off += nr
    return out


def _local_step(x, target, meta, w_fox_in, w_fox_out, late, small):
    S, D = x.shape
    Lp = S + HEAD_ROWS
    T = ROW_TILE if Lp % ROW_TILE == 0 else HEAD_ROWS
    P = D // LANES
    row = lambda v: v.reshape(1, -1).astype(F32)

    w_fin = jnp.pad(w_fox_in[0], ((0, 0), (0, LANES - w_fox_in.shape[-1] % LANES)))
    w_fout = w_fox_out[0]
    n_heads = w_fox_in.shape[-1] - 4 * D
    bf = jnp.pad(row(small["fox_b_f"]), ((0, 0), (0, LANES - small["fox_b_f"].size)))
    qg = jnp.tile(row(small["fox_q_norm"]), (1, 2))
    kg = jnp.tile(row(small["fox_k_norm"]), (1, 2))

    h0 = jnp.concatenate([jnp.zeros((N_PAD, D), F32), meta, x], axis=0)

    hn0 = _rms_fwd(h0, row(small["attn_norm"][0]), T, "rms0_fwd")
    proj0 = _mm(hn0, w_fin, "nn", F32, "fox_in_fwd")
    qn, kn, vb, c = _fox_prep_fwd(proj0, bf, qg, kg, T, D, "fox_prep_fwd")
    cT = c.T[:2 * P]
    o, og0, mshift, linv, *gathered = _fox_attn_fwd(qn, kn, vb, c, cT, proj0, [late[n] for n in LATE], T, D,
                                                    "fox_attn_fwd")
    wl = {n: _gathered_to_full(g, n) for n, g in zip(LATE, gathered)}
    w_hin, w_hout, w_uin, w_uout = wl["hgrn_w_in"][0], wl["hgrn_w_out"][0], wl["ffn_w_in"], wl["ffn_w_out"]
    h1, hf0 = _out_proj_fwd(og0, w_fout, h0, row(small["ffn_norm"][0]), "fox_out_fwd")
    gu0 = _ffn_in_fwd(hf0, w_uin[0], "ffn0_in_fwd")
    act0 = gu0[2]
    h2, hn1 = _out_proj_fwd(act0, w_uout[0], h1, row(small["attn_norm"][1]), "ffn0_out_fwd")
    proj1 = _mm(hn1, w_hin, "nn", F32, "hgrn_in_fwd")
    lbp = small["hgrn_lower_bounds"].astype(F32)
    ggn = row(small["hgrn_g_norm"])
    Th = HGRN_TILE if Lp % HGRN_TILE == 0 else HEAD_ROWS
    og1, ss = _hgrn_fwd(proj1, lbp, ggn, Th, "hgrn_fwd")
    h3, hf1 = _out_proj_fwd(og1, w_hout, h2, row(small["ffn_norm"][1]), "hgrn_out_fwd")
    gu1 = _ffn_in_fwd(hf1, w_uin[1], "ffn1_in_fwd")
    act1 = gu1[2]
    h4 = _mm(act1, w_uout[1], "nn", F32, "ffn1_out_fwd", res=h3)
    loss_blk, dh4, dh4b, d_final = _final_loss(h4, row(small["final_norm"]), target, "final_loss")

    grads = {}

    def ffn_bwd(i, dh, dhb, h_in, hf, gu, act, tag):
        grads_out = _mm(act, dhb, "tn", F32, f"ffn{i}_out_dw")
        dgu = _ffn_out_dx(dhb, w_uout[i], gu[0], gu[1], f"ffn{i}_out_dx")
        grads_in = _mm(hf, dgu, "tn", F32, f"ffn{i}_in_dw")
        dh_new, dh_newb, dgain = _in_proj_dx(dgu, w_uin[i], h_in, row(small["ffn_norm"][i]), dh, f"ffn{i}_in_dx")
        return dh_new, dh_newb, grads_in, grads_out, dgain

    dh3, dh3b, g_uin1, g_uout1, d_fn1 = ffn_bwd(1, dh4, dh4b, h3, hf1, gu1, act1, "1")
    grads["hgrn_w_out"] = _mm(og1, dh3b, "tn", F32, "hgrn_out_dw")[None]
    dog1 = _mm(dh3b, w_hout, "nt", F32, "hgrn_out_dx")
    dq1, dz1, di1, dgo1, d_lb, d_gg = _hgrn_bwd(proj1, lbp, ggn, dog1, ss, Th, "hgrn_bwd")
    dproj1 = jnp.concatenate([dq1, dz1, di1, dgo1], axis=1)
    grads["hgrn_w_in"] = _mm(hn1, dproj1, "tn", F32, "hgrn_in_dw")[None]
    dh2, dh2b, d_an1 = _in_proj_dx(dproj1, w_hin, h2, row(small["attn_norm"][1]), dh3, "hgrn_in_dx")
    dh1, dh1b, g_uin0, g_uout0, d_fn0 = ffn_bwd(0, dh2, dh2b, h1, hf0, gu0, act0, "0")
    grads["ffn_w_in"] = jnp.stack([g_uin0, g_uin1])
    grads["ffn_w_out"] = jnp.stack([g_uout0, g_uout1])
    grads["fox_w_out"] = _mm(og0, dh1b, "tn", F32, "fox_out_dw")[None]
    do, dgate, delta = _fox_out_dx(dh1b, w_fout, o, proj0, linv, D, "fox_out_dx")
    slabs = [_full_to_slabs(grads[n], n).astype(BF16) for n in LATE]
    dqn, dkn, dv, dcr, *recv = _fox_attn_bwd(qn, kn, vb, c, cT, do, mshift, delta, slabs, T, D, "fox_attn_bwd")
    for n in LATE:
        del grads[n]
    dc = jnp.pad(dcr[:, :2, :].reshape(2 * P, Lp).T, ((0, 0), (0, LANES - 2 * P)))
    dq0, dk0, dfl, sm = _fox_prep_bwd(proj0, bf, qg, kg, dqn, dkn, dc, T, D, "fox_prep_bwd")
    dproj0 = jnp.concatenate([dq0, dk0, dv.astype(BF16), dgate, dfl], axis=1)
    grads["fox_w_in"] = _mm(hn0, dproj0, "tn", F32, "fox_in_dw")[:, :4 * D + n_heads][None]
    dh0, _, d_an0 = _in_proj_dx(dproj0, w_fin, h0, row(small["attn_norm"][0]), dh1, "fox_in_dx")

    grads["meta_tokens"] = dh0[N_PAD:HEAD_ROWS]
    grads["attn_norm"] = jnp.concatenate([d_an0, d_an1], axis=0)
    grads["ffn_norm"] = jnp.concatenate([d_fn0, d_fn1], axis=0)
    grads["final_norm"] = d_final[0]
    grads["fox_b_f"] = sm[0:1, :n_heads]
    grads["fox_q_norm"] = sm[1:2, :FOX_DH] + sm[1:2, FOX_DH:]
    grads["fox_k_norm"] = sm[2:3, :FOX_DH] + sm[2:3, FOX_DH:]
    grads["hgrn_lower_bounds"] = d_lb[0:2]
    grads["hgrn_g_norm"] = d_gg[0:1]
    return loss_blk[0, 0], dh0[HEAD_ROWS:], grads, dict(zip(LATE, recv))


def kernel(x, meta_tokens, attn_norm, ffn_norm, final_norm, fox_w_in, fox_b_f, fox_q_norm, fox_k_norm, fox_w_out, hgrn_w_in, hgrn_lower_bounds, hgrn_g_norm, hgrn_w_out, ffn_w_in, ffn_w_out, loss_target, m_meta_tokens, m_attn_norm, m_ffn_norm, m_final_norm, m_fox_w_in, m_fox_b_f, m_fox_q_norm, m_fox_k_norm, m_fox_w_out, m_hgrn_w_in, m_hgrn_lower_bounds, m_hgrn_g_norm, m_hgrn_w_out, m_ffn_w_in, m_ffn_w_out, v_meta_tokens, v_attn_norm, v_ffn_norm, v_final_norm, v_fox_w_in, v_fox_b_f, v_fox_q_norm, v_fox_k_norm, v_fox_w_out, v_hgrn_w_in, v_hgrn_lower_bounds, v_hgrn_g_norm, v_hgrn_w_out, v_ffn_w_in, v_ffn_w_out):
    w = dict(meta_tokens=meta_tokens, attn_norm=attn_norm, ffn_norm=ffn_norm, final_norm=final_norm,
             fox_w_in=fox_w_in, fox_b_f=fox_b_f, fox_q_norm=fox_q_norm, fox_k_norm=fox_k_norm,
             fox_w_out=fox_w_out, hgrn_w_in=hgrn_w_in, hgrn_lower_bounds=hgrn_lower_bounds,
             hgrn_g_norm=hgrn_g_norm, hgrn_w_out=hgrn_w_out, ffn_w_in=ffn_w_in, ffn_w_out=ffn_w_out)
    m = dict(meta_tokens=m_meta_tokens, attn_norm=m_attn_norm, ffn_norm=m_ffn_norm, final_norm=m_final_norm,
             fox_w_in=m_fox_w_in, fox_b_f=m_fox_b_f, fox_q_norm=m_fox_q_norm, fox_k_norm=m_fox_k_norm,
             fox_w_out=m_fox_w_out, hgrn_w_in=m_hgrn_w_in, hgrn_lower_bounds=m_hgrn_lower_bounds,
             hgrn_g_norm=m_hgrn_g_norm, hgrn_w_out=m_hgrn_w_out, ffn_w_in=m_ffn_w_in, ffn_w_out=m_ffn_w_out)
    v = dict(meta_tokens=v_meta_tokens, attn_norm=v_attn_norm, ffn_norm=v_ffn_norm, final_norm=v_final_norm,
             fox_w_in=v_fox_w_in, fox_b_f=v_fox_b_f, fox_q_norm=v_fox_q_norm, fox_k_norm=v_fox_k_norm,
             fox_w_out=v_fox_w_out, hgrn_w_in=v_hgrn_w_in, hgrn_lower_bounds=v_hgrn_lower_bounds,
             hgrn_g_norm=v_hgrn_g_norm, hgrn_w_out=v_hgrn_w_out, ffn_w_in=v_ffn_w_in, ffn_w_out=v_ffn_w_out)
    axes = ("x", "y", "c")
    small_shapes = [w[n].shape for n in SMALL]

    g_meta, g_fin, g_fout = _exchange([w["meta_tokens"].astype(F32), w["fox_w_in"].astype(BF16),
                                       w["fox_w_out"].astype(BF16)], [False] * 3, "gather_weights")
    loss_local, grad_x, grads, recv = _local_step(
        x[0], loss_target[0], _gathered_to_full(g_meta, "meta_tokens"), _gathered_to_full(g_fin, "fox_w_in"),
        _gathered_to_full(g_fout, "fox_w_out"), {n: w[n].astype(BF16) for n in LATE}, {n: w[n] for n in SMALL})
    loss = lax.psum(loss_local, axes)

    r_meta, r_fin, r_fout, r_small = _exchange(
        [_full_to_slabs(grads["meta_tokens"], "meta_tokens"), _full_to_slabs(grads["fox_w_in"], "fox_w_in").astype(BF16),
         _full_to_slabs(grads["fox_w_out"], "fox_w_out").astype(BF16), _pack_small([grads[n] for n in SMALL])],
        [True, True, True, False], "scatter_grads")
    recv.update(meta_tokens=r_meta, fox_w_in=r_fin, fox_w_out=r_fout)

    res = {n: _adamw(recv[n], w[n], m[n], v[n], "adamw_" + n) for n in BIG}
    sml = _adamw(r_small, _pack_small([w[n] for n in SMALL]), _pack_small([m[n] for n in SMALL]),
                 _pack_small([v[n] for n in SMALL]), "adamw_small")
    outs = []
    for k in range(4):
        d = {n: res[n][k] for n in BIG}
        d.update(zip(SMALL, _unpack_small(sml[k], small_shapes)))
        outs.extend(d[n] for n in WEIGHTS)
    return (loss, grad_x[None], *outs)
```

```python
import functools
import math

import jax
import jax.numpy as jnp
from jax import lax
from jax.experimental import pallas as pl
from jax.experimental.pallas import tpu as pltpu

F32 = jnp.float32
BF16 = jnp.bfloat16
EPS = 1e-6
N_META = 16
LANES = 128
HEAD_ROWS = 256
ROW_TILE = 768
N_PAD = HEAD_ROWS - N_META
FOX_DH = 64
HGRN_CHUNK = 64
HGRN_TILE = 384
N_DEV = 8
NEG = -1e30
VMEM_LIMIT = 56 * 1024 * 1024
HI = lax.Precision.HIGHEST

ADAM_LR = 0.001
ADAM_B1 = 0.9
ADAM_B2 = 0.999
ADAM_EPS = 1e-08
ADAM_WD = 0.01
ADAM_STEP = 10

BIG = ("meta_tokens", "fox_w_in", "fox_w_out", "hgrn_w_in", "hgrn_w_out", "ffn_w_in", "ffn_w_out")
SMALL = ("attn_norm", "ffn_norm", "final_norm", "fox_b_f", "fox_q_norm", "fox_k_norm",
         "hgrn_lower_bounds", "hgrn_g_norm")
WEIGHTS = ("meta_tokens", "attn_norm", "ffn_norm", "final_norm", "fox_w_in", "fox_b_f", "fox_q_norm",
           "fox_k_norm", "fox_w_out", "hgrn_w_in", "hgrn_lower_bounds", "hgrn_g_norm", "hgrn_w_out",
           "ffn_w_in", "ffn_w_out")
COL_SHARDED = ("meta_tokens", "fox_w_in", "hgrn_w_in", "ffn_w_in")
LATE = ("hgrn_w_in", "hgrn_w_out", "ffn_w_in", "ffn_w_out")


def _params(sem=None):
    return pltpu.CompilerParams(dimension_semantics=sem, vmem_limit_bytes=VMEM_LIMIT)


def _tile(n, cap):
    best = None
    for t in range(LANES, min(n, cap) + 1, LANES):
        if n % t == 0:
            best = t
    assert best is not None, (n, cap)
    return best


def _row_chunk(n, cap, mult=8):
    best = n
    for t in range(mult, min(n, cap) + 1, mult):
        if n % t == 0:
            best = t
    return best


def _dg(a, b, ca, cb):
    return lax.dot_general(a.astype(BF16), b.astype(BF16), (((ca,), (cb,)), ((), ())),
                           preferred_element_type=F32)


@jax.custom_vjp
def _d_nn(a, b):
    return _dg(a, b, 1, 0)


@jax.custom_vjp
def _d_nt(a, b):
    return _dg(a, b, 1, 1)


@jax.custom_vjp
def _d_tn(a, b):
    return _dg(a, b, 0, 0)


_d_nn.defvjp(lambda a, b: (_d_nn(a, b), (a, b)), lambda r, g: (_d_nt(g, r[1]), _d_tn(r[0], g)))
_d_nt.defvjp(lambda a, b: (_d_nt(a, b), (a, b)), lambda r, g: (_d_nn(g, r[1]), _d_tn(g, r[0])))
_d_tn.defvjp(lambda a, b: (_d_tn(a, b), (a, b)), lambda r, g: (_d_nt(r[1], g), _d_nn(r[0], g)))


def _log_sigmoid(x):
    return jnp.minimum(x, 0.0) - jnp.log1p(jnp.exp(-jnp.abs(x)))


def _rms(x, g):
    return x * lax.rsqrt(jnp.mean(x * x, axis=-1, keepdims=True) + EPS) * g


def _mm(a, b, mode, out_dtype, name, res=None, tm=None, tn=None, tk=None):
    assert a.dtype == BF16 and b.dtype == BF16, (name, a.dtype, b.dtype)
    if mode == "nn":
        (M, K), N = a.shape, b.shape[1]
    elif mode == "nt":
        (M, K), N = a.shape, b.shape[0]
    else:
        (K, M), N = a.shape, b.shape[1]
    if mode == "nn":
        tm, tn, tk = tm or _tile(M, ROW_TILE), tn or _tile(N, 1408), tk or _tile(K, 2816)
    elif mode == "nt":
        tm, tn, tk = tm or _tile(M, ROW_TILE if K <= 2048 else ROW_TILE // 2), tn or N, tk or K
    else:
        tm, tn, tk = tm or _tile(M, 1408), tn or _tile(N, 1408), tk or _tile(K, ROW_TILE)
    nk = K // tk
    if mode == "tn":
        a_spec = pl.BlockSpec((tk, tm), lambda j, i, k: (k, i))
        dims = (((0,), (0,)), ((), ()))
    else:
        a_spec = pl.BlockSpec((tm, tk), lambda j, i, k: (i, k))
        dims = (((1,), (1 if mode == "nt" else 0,)), ((), ()))
    if mode == "nt":
        b_spec = pl.BlockSpec((tn, tk), lambda j, i, k: (j, k))
    else:
        b_spec = pl.BlockSpec((tk, tn), lambda j, i, k: (k, j))

    o_spec = pl.BlockSpec((tm, tn), lambda j, i, k: (i, j))

    def body(a_ref, b_ref, *rest):
        r_ref = rest[0] if res is not None else None
        o_ref, acc_ref = rest[-2:]
        k = pl.program_id(2)

        @pl.when(k == 0)
        def _():
            acc_ref[...] = jnp.zeros_like(acc_ref)

        acc_ref[...] += lax.dot_general(a_ref[...], b_ref[...], dims, preferred_element_type=F32)

        @pl.when(k == nk - 1)
        def _():
            out = acc_ref[...] if r_ref is None else acc_ref[...] + r_ref[...]
            o_ref[...] = out.astype(out_dtype)

    return pl.pallas_call(
        body, name=name, grid=(N // tn, M // tm, nk),
        in_specs=[a_spec, b_spec] + ([o_spec] if res is not None else []),
        out_specs=o_spec,
        out_shape=jax.ShapeDtypeStruct((M, N), out_dtype),
        scratch_shapes=[pltpu.VMEM((tm, tn), F32)],
        compiler_params=_params(("parallel", "parallel", "arbitrary")),
    )(a, b, *([res] if res is not None else []))


def _out_proj_fwd(a, w, res, gain, name):
    Lp, K = a.shape
    D = w.shape[1]
    tm = _tile(Lp, ROW_TILE)

    def body(a_ref, w_ref, r_ref, g_ref, h_ref, hn_ref):
        h = jnp.dot(a_ref[...], w_ref[...], preferred_element_type=F32) + r_ref[...]
        h_ref[...] = h
        hn_ref[...] = _rms(h, g_ref[...]).astype(BF16)

    row = pl.BlockSpec((tm, D), lambda i: (i, 0))
    return pl.pallas_call(
        body, name=name, grid=(Lp // tm,),
        in_specs=[pl.BlockSpec((tm, K), lambda i: (i, 0)), pl.BlockSpec((K, D), lambda i: (0, 0)), row,
                  pl.BlockSpec((1, D), lambda i: (0, 0))],
        out_specs=[row, row],
        out_shape=[jax.ShapeDtypeStruct((Lp, D), F32), jax.ShapeDtypeStruct((Lp, D), BF16)],
        compiler_params=_params(("parallel",)),
    )(a, w, res, gain)


def _in_proj_dx(dy, w, x, gain, dres, name):
    Lp, N = dy.shape
    D = w.shape[0]
    tm = _tile(Lp, ROW_TILE // 2)

    def body(dy_ref, w_ref, x_ref, g_ref, dr_ref, dx_ref, dxb_ref, dg_ref):
        @pl.when(pl.program_id(0) == 0)
        def _():
            dg_ref[...] = jnp.zeros_like(dg_ref)

        dhn = lax.dot_general(dy_ref[...], w_ref[...], (((1,), (1,)), ((), ())), preferred_element_type=F32)
        _, vjp = jax.vjp(_rms, x_ref[...], g_ref[...])
        dx, dg = vjp(dhn)
        dx = dx + dr_ref[...]
        dx_ref[...] = dx
        dxb_ref[...] = dx.astype(BF16)
        dg_ref[...] += dg

    row = pl.BlockSpec((tm, D), lambda i: (i, 0))
    vec = pl.BlockSpec((1, D), lambda i: (0, 0))
    return pl.pallas_call(
        body, name=name, grid=(Lp // tm,),
        in_specs=[pl.BlockSpec((tm, N), lambda i: (i, 0)), pl.BlockSpec((D, N), lambda i: (0, 0)), row, vec, row],
        out_specs=[row, row, vec],
        out_shape=[jax.ShapeDtypeStruct((Lp, D), F32), jax.ShapeDtypeStruct((Lp, D), BF16),
                   jax.ShapeDtypeStruct((1, D), F32)],
        compiler_params=_params(("arbitrary",)),
    )(dy, w, x, gain, dres)


def _rms_fwd(x, g, T, name):
    Lp, D = x.shape

    def body(x_ref, g_ref, o_ref):
        o_ref[...] = _rms(x_ref[...], g_ref[...]).astype(BF16)

    return pl.pallas_call(
        body, name=name, grid=(Lp // T,),
        in_specs=[pl.BlockSpec((T, D), lambda i: (i, 0)), pl.BlockSpec((1, D), lambda i: (0, 0))],
        out_specs=pl.BlockSpec((T, D), lambda i: (i, 0)),
        out_shape=jax.ShapeDtypeStruct((Lp, D), BF16),
        compiler_params=_params(("parallel",)),
    )(x, g)


def _swiglu(gate, up):
    return gate * jax.nn.sigmoid(gate) * up


def _ffn_in_fwd(hf, w_in, name):
    Lp, D = hf.shape
    F = w_in.shape[1] // 2
    tm = _tile(Lp, ROW_TILE)
    tn = _tile(F, 1408)
    nj = F // tn

    def body(a_ref, bg_ref, bu_ref, g_ref, u_ref, act_ref):
        a = a_ref[...]
        g = jnp.dot(a, bg_ref[...], preferred_element_type=F32)
        u = jnp.dot(a, bu_ref[...], preferred_element_type=F32)
        g_ref[...] = g.astype(BF16)
        u_ref[...] = u.astype(BF16)
        act_ref[...] = _swiglu(g, u).astype(BF16)

    tile = pl.BlockSpec((tm, tn), lambda j, i: (i, j))
    return pl.pallas_call(
        body, name=name, grid=(nj, Lp // tm),
        in_specs=[pl.BlockSpec((tm, D), lambda j, i: (i, 0)), pl.BlockSpec((D, tn), lambda j, i: (0, j)),
                  pl.BlockSpec((D, tn), lambda j, i: (0, nj + j))],
        out_specs=[tile, tile, tile],
        out_shape=[jax.ShapeDtypeStruct((Lp, F), BF16)] * 3,
        compiler_params=_params(("parallel", "parallel")),
    )(hf, w_in, w_in)


def _ffn_out_dx(dhb, w_out, g, u, name):
    Lp, D = dhb.shape
    F = w_out.shape[0]
    tm = HEAD_ROWS

    def body(a_ref, b_ref, g_ref, u_ref, o_ref):
        dact = lax.dot_general(a_ref[...], b_ref[...], (((1,), (1,)), ((), ())), preferred_element_type=F32)
        _, vjp = jax.vjp(_swiglu, g_ref[...].astype(F32), u_ref[...].astype(F32))
        dg, du = vjp(dact)
        o_ref[:, :F] = dg.astype(BF16)
        o_ref[:, F:] = du.astype(BF16)

    wide = pl.BlockSpec((tm, F), lambda i: (i, 0))
    return pl.pallas_call(
        body, name=name, grid=(Lp // tm,),
        in_specs=[pl.BlockSpec((tm, D), lambda i: (i, 0)), pl.BlockSpec((F, D), lambda i: (0, 0)), wide, wide],
        out_specs=pl.BlockSpec((tm, 2 * F), lambda i: (i, 0)),
        out_shape=jax.ShapeDtypeStruct((Lp, 2 * F), BF16),
        compiler_params=_params(("parallel",)),
    )(dhb, w_out, g, u)


def _final_loss(h, g, target, name):
    Lp, D = h.shape
    TR = HEAD_ROWS

    def loss_fn(hh, gg, tt):
        err = _rms(hh, gg) - tt
        return 0.5 * jnp.sum(jnp.mean(err * err, axis=-1))

    def body(h_ref, g_ref, t_ref, loss_ref, dh_ref, dhb_ref, dg_ref):
        i = pl.program_id(0)

        @pl.when(i == 0)
        def _():
            loss_ref[...] = jnp.zeros_like(loss_ref)
            dg_ref[...] = jnp.zeros_like(dg_ref)
            dh_ref[...] = jnp.zeros_like(dh_ref)
            dhb_ref[...] = jnp.zeros_like(dhb_ref)

        @pl.when(i > 0)
        def _():
            val, vjp = jax.vjp(lambda hh, gg: loss_fn(hh, gg, t_ref[...]), h_ref[...], g_ref[...])
            dh, dg = vjp(jnp.ones((), F32))
            dh_ref[...] = dh
            dhb_ref[...] = dh.astype(BF16)
            dg_ref[...] += dg
            loss_ref[...] += val

    row = pl.BlockSpec((TR, D), lambda i: (i, 0))
    return pl.pallas_call(
        body, name=name, grid=(Lp // TR,),
        in_specs=[row, pl.BlockSpec((1, D), lambda i: (0, 0)),
                  pl.BlockSpec((TR, D), lambda i: (jnp.maximum(i - 1, 0), 0))],
        out_specs=[pl.BlockSpec((8, LANES), lambda i: (0, 0)), row, row, pl.BlockSpec((1, D), lambda i: (0, 0))],
        out_shape=[jax.ShapeDtypeStruct((8, LANES), F32), jax.ShapeDtypeStruct((Lp, D), F32),
                   jax.ShapeDtypeStruct((Lp, D), BF16), jax.ShapeDtypeStruct((1, D), F32)],
        compiler_params=_params(("arbitrary",)),
    )(h, g, target)


def _lane_lo():
    return lax.broadcasted_iota(jnp.int32, (1, LANES), 1) < FOX_DH


def _headnorm(x, g, scale):
    lo = _lane_lo()
    x2 = x * x
    s0 = jnp.sum(jnp.where(lo, x2, 0.0), axis=-1, keepdims=True)
    s1 = jnp.sum(jnp.where(lo, 0.0, x2), axis=-1, keepdims=True)
    r = jnp.where(lo, lax.rsqrt(s0 / FOX_DH + EPS), lax.rsqrt(s1 / FOX_DH + EPS))
    return x * r * g * scale


def _fox_prep_fwd(proj, bf, qg, kg, T, D, name):
    Lp = proj.shape[0]
    nb = D // LANES
    scale = FOX_DH ** -0.5

    def body(q_ref, k_ref, v_ref, fl_ref, bf_ref, qg_ref, kg_ref, qn_ref, kn_ref, vb_ref, c_ref, carry_ref):
        @pl.when(pl.program_id(0) == 0)
        def _():
            carry_ref[...] = jnp.zeros_like(carry_ref)

        for b in range(nb):
            sl = slice(b * LANES, (b + 1) * LANES)
            qn_ref[:, sl] = _headnorm(q_ref[:, sl], qg_ref[...], scale).astype(BF16)
            kn_ref[:, sl] = _headnorm(k_ref[:, sl], kg_ref[...], 1.0).astype(BF16)
        vb_ref[...] = v_ref[...].astype(BF16)
        log_f = _log_sigmoid(fl_ref[...] + bf_ref[...])
        row = lax.broadcasted_iota(jnp.int32, (T, T), 0)
        col = lax.broadcasted_iota(jnp.int32, (T, T), 1)
        tri = (col <= row).astype(F32)
        c = jnp.dot(tri, log_f, precision=HI, preferred_element_type=F32) + carry_ref[...]
        c_ref[...] = c
        last = lax.broadcasted_iota(jnp.int32, (T, 1), 0) == T - 1
        carry_ref[...] = jnp.sum(jnp.where(last, c, 0.0), axis=0, keepdims=True)

    wide = lambda j: pl.BlockSpec((T, D), lambda i: (i, j))
    vec = pl.BlockSpec((1, LANES), lambda i: (0, 0))
    return pl.pallas_call(
        body, name=name, grid=(Lp // T,),
        in_specs=[wide(0), wide(1), wide(2), pl.BlockSpec((T, LANES), lambda i: (i, 4 * nb)), vec, vec, vec],
        out_specs=[wide(0), wide(0), wide(0), pl.BlockSpec((T, LANES), lambda i: (i, 0))],
        out_shape=[jax.ShapeDtypeStruct((Lp, D), BF16)] * 3 + [jax.ShapeDtypeStruct((Lp, LANES), F32)],
        scratch_shapes=[pltpu.VMEM((1, LANES), F32)],
        compiler_params=_params(("arbitrary",)),
    )(proj, proj, proj, proj, bf, qg, kg)


def _fox_prep_bwd(proj, bf, qg, kg, dqn, dkn, dv, dgate, dc, T, D, name):
    Lp = proj.shape[0]
    nb = D // LANES
    nt = Lp // T
    scale = FOX_DH ** -0.5

    def body(q_ref, k_ref, fl_ref, bf_ref, qg_ref, kg_ref, dqn_ref, dkn_ref, dv_ref, dgate_ref, dc_ref,
             dproj_ref, sm_ref, carry_ref):
        @pl.when(pl.program_id(0) == 0)
        def _():
            carry_ref[...] = jnp.zeros_like(carry_ref)
            sm_ref[...] = jnp.zeros_like(sm_ref)

        dqg = jnp.zeros((1, LANES), F32)
        dkg = jnp.zeros((1, LANES), F32)
        for b in range(nb):
            sl = slice(b * LANES, (b + 1) * LANES)
            _, vjp = jax.vjp(lambda x, g: _headnorm(x, g, scale), q_ref[:, sl], qg_ref[...])
            dx, dg = vjp(dqn_ref[:, sl])
            dproj_ref[:, sl] = dx.astype(BF16)
            dqg = dqg + dg
            _, vjp = jax.vjp(lambda x, g: _headnorm(x, g, 1.0), k_ref[:, sl], kg_ref[...])
            dx, dg = vjp(dkn_ref[:, sl])
            dproj_ref[:, D + b * LANES:D + (b + 1) * LANES] = dx.astype(BF16)
            dkg = dkg + dg
        dproj_ref[:, 2 * D:3 * D] = dv_ref[...].astype(BF16)
        dproj_ref[:, 3 * D:4 * D] = dgate_ref[...]
        dcv = dc_ref[...]
        row = lax.broadcasted_iota(jnp.int32, (T, T), 0)
        col = lax.broadcasted_iota(jnp.int32, (T, T), 1)
        triu = (col >= row).astype(F32)
        dlogf = jnp.dot(triu, dcv, precision=HI, preferred_element_type=F32) + carry_ref[...]
        carry_ref[...] += jnp.sum(dcv, axis=0, keepdims=True)
        _, vjp = jax.vjp(_log_sigmoid, fl_ref[...] + bf_ref[...])
        (dfl,) = vjp(dlogf)
        dproj_ref[:, 4 * D:] = dfl.astype(BF16)
        sm_ref[0:1, :] += jnp.sum(dfl, axis=0, keepdims=True)
        sm_ref[1:2, :] += dqg
        sm_ref[2:3, :] += dkg

    wide = lambda j: pl.BlockSpec((T, D), lambda i: (nt - 1 - i, j))
    narrow = lambda j: pl.BlockSpec((T, LANES), lambda i: (nt - 1 - i, j))
    vec = pl.BlockSpec((1, LANES), lambda i: (0, 0))
    return pl.pallas_call(
        body, name=name, grid=(nt,),
        in_specs=[wide(0), wide(1), narrow(4 * nb), vec, vec, vec, wide(0), wide(0), wide(0), wide(0), narrow(0)],
        out_specs=[pl.BlockSpec((T, 4 * D + LANES), lambda i: (nt - 1 - i, 0)), pl.BlockSpec((8, LANES), lambda i: (0, 0))],
        out_shape=[jax.ShapeDtypeStruct((Lp, 4 * D + LANES), BF16), jax.ShapeDtypeStruct((8, LANES), F32)],
        scratch_shapes=[pltpu.VMEM((1, LANES), F32)],
        compiler_params=_params(("arbitrary",)),
    )(proj, proj, proj, bf, qg, kg, dqn, dkn, dv, dgate, dc)


def _ln2_ceil(m):
    return jnp.ceil(m * (1.0 / math.log(2.0))) * math.log(2.0)


def _fox_mask(i, k0, T):
    qpos = i * T + lax.broadcasted_iota(jnp.int32, (T, 1), 0)
    kpos = k0 + lax.broadcasted_iota(jnp.int32, (1, T), 1)
    return (kpos <= qpos) & ((kpos >= N_PAD) | (qpos < N_PAD))


def _pick_col(blk, idx):
    lane = lax.broadcasted_iota(jnp.int32, (1, LANES), 1)
    return jnp.sum(jnp.where(lane == idx, blk, 0.0), axis=1, keepdims=True)


def _split_halves(blk):
    lo = _lane_lo()
    return (jnp.max(jnp.where(lo, blk, -jnp.inf), axis=1, keepdims=True),
            jnp.max(jnp.where(lo, -jnp.inf, blk), axis=1, keepdims=True))


def _fox_attn_fwd(qn, kn, vb, c, cT, proj, xchg, T, D, name):
    Lp = qn.shape[0]
    P = D // LANES
    nt = Lp // T
    H = cT.shape[0]
    nx = len(xchg)

    def body(q_ref, k_ref, v_ref, c_ref, cT_ref, g_ref, *rest):
        x_in, (o_ref, og_ref, m_ref, li_ref), x_out, sems = rest[:nx], rest[nx:nx + 4], rest[nx + 4:2 * nx + 4], rest[2 * nx + 4:]
        p = pl.program_id(0)
        i = pl.program_id(1)

        @pl.when((p == 0) & (i == 0))
        def _():
            for cp in _xchg_copies(x_in, x_out, [False] * nx, sems):
                cp.start()

        lo = _lane_lo()
        q = q_ref[...]
        zero = jnp.zeros_like(q)
        qh = (jnp.where(lo, q, zero), jnp.where(lo, zero, q))
        cblk = c_ref[...]
        cq = tuple(_pick_col(cblk, 2 * p + h) for h in (0, 1))
        one = jnp.ones_like(q)

        def step(j, carry, masked):
            k0 = pl.multiple_of(j * T, LANES)
            kj = k_ref[pl.ds(k0, T), :]
            vj = v_ref[pl.ds(k0, T), :]
            vh = (jnp.where(lo, vj, one), jnp.where(lo, one, vj))
            mask = _fox_mask(i, k0, T) if masked else None
            out = []
            for h in (0, 1):
                m, acc = carry[h]
                ck = cT_ref[pl.ds(2 * p + h, 1), pl.ds(k0, T)]
                t = lax.dot_general(qh[h], kj, (((1,), (1,)), ((), ())), preferred_element_type=F32) - ck
                if masked:
                    t = jnp.where(mask, t, NEG)
                m_new = _ln2_ceil(jnp.maximum(m, cq[h] + jnp.max(t, axis=1, keepdims=True)))
                pr = jnp.exp(t + (cq[h] - m_new)).astype(BF16)
                acc = jnp.exp(m - m_new) * acc + jnp.dot(pr, vh[h], preferred_element_type=F32)
                out.append((m_new, acc))
            return tuple(out)

        init = tuple((jnp.full((T, 1), NEG, F32), jnp.zeros((T, LANES), F32)) for _ in (0, 1))
        carry = step(0, init, True)
        carry = lax.fori_loop(1, i, lambda j, cr: step(j, cr, False), carry)
        (m0, a0), (m1, a1) = lax.cond(i > 0, lambda cr: step(i, cr, True), lambda cr: cr, carry)
        l0 = pltpu.roll(a0, FOX_DH, 1)
        l1 = pltpu.roll(a1, FOX_DH, 1)
        o = jnp.where(lo, a0 / l0, a1 / l1)
        o_ref[...] = o
        m_ref[...] = jnp.where(lo, m0, m1)
        li_ref[...] = jnp.where(lo, 1.0 / l0, 1.0 / l1)
        og_ref[...] = (o * jax.nn.sigmoid(g_ref[...])).astype(BF16)

        @pl.when((p == P - 1) & (i == nt - 1))
        def _():
            for cp in _xchg_copies(x_in, x_out, [False] * nx, sems):
                cp.wait()

    tile = pl.BlockSpec((T, LANES), lambda p, i: (i, p))
    full = pl.BlockSpec((Lp, LANES), lambda p, i: (0, p))
    HBM = pl.BlockSpec(memory_space=pltpu.HBM)
    return pl.pallas_call(
        body, name=name, grid=(P, nt),
        in_specs=[tile, full, full, pl.BlockSpec((T, LANES), lambda p, i: (i, 0)),
                  pl.BlockSpec((H, Lp), lambda p, i: (0, 0)),
                  pl.BlockSpec((T, LANES), lambda p, i: (i, 3 * P + p))] + [HBM] * nx,
        out_specs=[tile, tile, tile, tile] + [HBM] * nx,
        out_shape=[jax.ShapeDtypeStruct((Lp, D), F32), jax.ShapeDtypeStruct((Lp, D), BF16),
                   jax.ShapeDtypeStruct((Lp, D), F32), jax.ShapeDtypeStruct((Lp, D), F32)]
        + _xchg_out_shapes(xchg, [False] * nx),
        scratch_shapes=_xchg_sems(nx),
        compiler_params=_params(("arbitrary", "arbitrary")),
    )(qn, kn, vb, c, cT, proj, *xchg)


def _fox_out_dx(dhb, w_out, o, proj, linv, D, name):
    Lp = o.shape[0]
    tm = HEAD_ROWS

    def body(a_ref, w_ref, o_ref, g_ref, li_ref, do_ref, dg_ref, dl_ref):
        lo = _lane_lo()
        dog_all = lax.dot_general(a_ref[...], w_ref[...], (((1,), (1,)), ((), ())), preferred_element_type=F32)
        for b in range(D // LANES):
            sl = slice(b * LANES, (b + 1) * LANES)
            dog = dog_all[:, sl]
            sig = jax.nn.sigmoid(g_ref[:, sl])
            ov = o_ref[:, sl]
            do = (dog * sig * li_ref[:, sl]).astype(BF16)
            do_ref[:, sl] = do
            dg_ref[:, sl] = (dog * ov * sig * (1.0 - sig)).astype(BF16)
            t = do.astype(F32) * ov
            d0 = jnp.sum(jnp.where(lo, t, 0.0), axis=1, keepdims=True)
            d1 = jnp.sum(jnp.where(lo, 0.0, t), axis=1, keepdims=True)
            dl_ref[:, sl] = jnp.where(lo, d0, d1)

    row = pl.BlockSpec((tm, D), lambda i: (i, 0))
    return pl.pallas_call(
        body, name=name, grid=(Lp // tm,),
        in_specs=[row, pl.BlockSpec((D, D), lambda i: (0, 0)), row, pl.BlockSpec((tm, D), lambda i: (i, 3)), row],
        out_specs=[row, row, row],
        out_shape=[jax.ShapeDtypeStruct((Lp, D), BF16), jax.ShapeDtypeStruct((Lp, D), BF16),
                   jax.ShapeDtypeStruct((Lp, D), F32)],
        compiler_params=_params(("parallel",)),
    )(dhb, w_out, o, proj, linv)


def _fox_attn_bwd(qn, kn, vb, c, cT, do, mshift, delta, xchg, T, D, name):
    Lp = qn.shape[0]
    P = D // LANES
    nt = Lp // T
    H = cT.shape[0]
    nx = len(xchg)

    def body(q_ref, do_ref, m_ref, dl_ref, c_ref, k_ref, v_ref, cT_ref, *rest):
        x_in, (dq_ref, dk_ref, dv_ref, dc_ref), x_out, sems = rest[:nx], rest[nx:nx + 4], rest[nx + 4:2 * nx + 4], rest[2 * nx + 4:]
        p = pl.program_id(0)
        i = pl.program_id(1)

        @pl.when((p == 0) & (i == 0))
        def _():
            for cp in _xchg_copies(x_in, x_out, [True] * nx, sems):
                cp.start()

        @pl.when(i == 0)
        def _():
            dk_ref[...] = jnp.zeros_like(dk_ref)
            dv_ref[...] = jnp.zeros_like(dv_ref)
            dc_ref[...] = jnp.zeros_like(dc_ref)

        lo = _lane_lo()
        q = q_ref[...]
        do = do_ref[...]
        zero = jnp.zeros_like(q)
        qh = (jnp.where(lo, q, zero), jnp.where(lo, zero, q))
        doh = (jnp.where(lo, do, zero), jnp.where(lo, zero, do))
        msh = _split_halves(m_ref[...])
        dlt = _split_halves(dl_ref[...])
        cblk = c_ref[...]
        shift = tuple(_pick_col(cblk, 2 * p + h) - msh[h] for h in (0, 1))

        def step(j, carry, masked):
            k0 = pl.multiple_of(j * T, LANES)
            kj = k_ref[pl.ds(k0, T), :]
            vj = v_ref[pl.ds(k0, T), :]
            mask = _fox_mask(i, k0, T) if masked else None
            dqs, dks, dvs = [], [], []
            for h in (0, 1):
                ck = cT_ref[pl.ds(2 * p + h, 1), pl.ds(k0, T)]
                t = lax.dot_general(qh[h], kj, (((1,), (1,)), ((), ())), preferred_element_type=F32) - ck
                if masked:
                    t = jnp.where(mask, t, NEG)
                pb = jnp.exp(t + shift[h]).astype(BF16)
                dp = lax.dot_general(doh[h], vj, (((1,), (1,)), ((), ())), preferred_element_type=F32)
                ds = pb.astype(F32) * (dp - dlt[h])
                dsb = ds.astype(BF16)
                dqs.append(carry[h] + jnp.dot(dsb, kj, preferred_element_type=F32))
                dks.append(lax.dot_general(dsb, q, (((0,), (0,)), ((), ())), preferred_element_type=F32))
                dvs.append(lax.dot_general(pb, do, (((0,), (0,)), ((), ())), preferred_element_type=F32))
                dc_ref[0, h:h + 1, pl.ds(k0, T)] += -jnp.sum(ds, axis=0, keepdims=True)
            dk_ref[pl.ds(k0, T), :] += jnp.where(lo, dks[0], dks[1])
            dv_ref[pl.ds(k0, T), :] += jnp.where(lo, dvs[0], dvs[1])
            return tuple(dqs)

        init = (jnp.zeros((T, LANES), F32), jnp.zeros((T, LANES), F32))
        carry = step(0, init, True)
        carry = lax.fori_loop(1, i, lambda j, cr: step(j, cr, False), carry)
        dq0, dq1 = lax.cond(i > 0, lambda cr: step(i, cr, True), lambda cr: cr, carry)
        dq_ref[...] = jnp.where(lo, dq0, dq1)

        @pl.when((p == P - 1) & (i == nt - 1))
        def _():
            for cp in _xchg_copies(x_in, x_out, [True] * nx, sems):
                cp.wait()

    tile = pl.BlockSpec((T, LANES), lambda p, i: (i, p))
    full = pl.BlockSpec((Lp, LANES), lambda p, i: (0, p))
    HBM = pl.BlockSpec(memory_space=pltpu.HBM)
    return pl.pallas_call(
        body, name=name, grid=(P, nt),
        in_specs=[tile, tile, tile, tile, pl.BlockSpec((T, LANES), lambda p, i: (i, 0)), full, full,
                  pl.BlockSpec((H, Lp), lambda p, i: (0, 0))] + [HBM] * nx,
        out_specs=[tile, full, full, pl.BlockSpec((1, 8, Lp), lambda p, i: (p, 0, 0))] + [HBM] * nx,
        out_shape=[jax.ShapeDtypeStruct((Lp, D), F32)] * 3 + [jax.ShapeDtypeStruct((P, 8, Lp), F32)]
        + _xchg_out_shapes(xchg, [True] * nx),
        scratch_shapes=_xchg_sems(nx),
        compiler_params=_params(("arbitrary", "arbitrary")),
    )(qn, do, mshift, delta, c, kn, vb, cT, *xchg)


def _scan_rows(x, reverse):
    C = x.shape[0]
    row = lax.broadcasted_iota(jnp.int32, (C, 1), 0)
    step = 1
    while step < C:
        if reverse:
            x = x + jnp.where(row < C - step, pltpu.roll(x, C - step, 0), 0.0)
        else:
            x = x + jnp.where(row >= step, pltpu.roll(x, step, 0), 0.0)
        step *= 2
    return x


@jax.custom_vjp
def _cumsum_rows(x):
    return _scan_rows(x, False)


_cumsum_rows.defvjp(lambda x: (_scan_rows(x, False), None), lambda _, g: (_scan_rows(g, True),))


def _hgrn_chunk(St, qr, z, vi, go, p0, p1, gg):
    C = qr.shape[0]
    lb = jax.nn.sigmoid(p1 - p0)
    a = jnp.log(lb)
    cc = jnp.log1p(-lb) + _log_sigmoid(z)
    log_f = jnp.maximum(a, cc) + jnp.log1p(jnp.exp(-jnp.abs(a - cc)))
    k = (1.0 - lb) * jax.nn.sigmoid(-z)
    q = qr * jax.nn.sigmoid(qr)
    row = lax.broadcasted_iota(jnp.int32, (C, C), 0)
    col = lax.broadcasted_iota(jnp.int32, (C, C), 1)
    causal = col <= row
    b = _cumsum_rows(log_f)
    mid = lax.broadcasted_iota(jnp.int32, (C, 1), 0) == C // 2 - 1
    r = jnp.sum(jnp.where(mid, b, 0.0), axis=0, keepdims=True)
    b_last = jnp.sum(log_f, axis=0, keepdims=True)
    attn = jnp.where(causal, _d_nt(q * jnp.exp(b - r), k * jnp.exp(r - b)), 0.0)
    o = _d_nn(attn, vi) + _d_nt(q * jnp.exp(b), St)
    St_new = St * jnp.exp(b_last) + _d_tn(vi, k * jnp.exp(b_last - b))
    og = _rms(o, gg) * (go * jax.nn.sigmoid(go))
    return St_new, og


def _hgrn_heads_per_step(H):
    return 8 if H % 8 == 0 else 4 if H % 4 == 0 else 1


def _hgrn_specs(T, W, nhb, rev_nt=None):
    if rev_nt is None:
        return [pl.BlockSpec((T, W), functools.partial(lambda hb, t, g: (t, g * nhb + hb), g=g)) for g in range(4)]
    return [pl.BlockSpec((T, W), functools.partial(lambda hb, t, g: (rev_nt - 1 - t, g * nhb + hb), g=g))
            for g in range(4)]


def _hgrn_fwd(proj, lbp, gg, T, name):
    Lp = proj.shape[0]
    D = proj.shape[1] // 4
    H = D // LANES
    hps = _hgrn_heads_per_step(H)
    W = hps * LANES
    nhb = H // hps
    nt = Lp // T
    ncc = T // HGRN_CHUNK

    def body(q_ref, z_ref, i_ref, go_ref, p_ref, gg_ref, og_ref, ss_ref, st_ref):
        @pl.when(pl.program_id(1) == 0)
        def _():
            st_ref[...] = jnp.zeros_like(st_ref)

        gain = gg_ref[...]

        def chunk(cidx, states):
            sl = pl.ds(pl.multiple_of(cidx * HGRN_CHUNK, HGRN_CHUNK), HGRN_CHUNK)
            new = []
            for hh in range(hps):
                ln = slice(hh * LANES, (hh + 1) * LANES)
                ss_ref[hh, cidx] = states[hh]
                St_new, og = _hgrn_chunk(states[hh], q_ref[sl, ln], z_ref[sl, ln], i_ref[sl, ln], go_ref[sl, ln],
                                         p_ref[0:1, ln], p_ref[1:2, ln], gain)
                og_ref[sl, ln] = og.astype(BF16)
                new.append(St_new)
            return tuple(new)

        states = lax.fori_loop(0, ncc, chunk, tuple(st_ref[hh] for hh in range(hps)))
        for hh in range(hps):
            st_ref[hh] = states[hh]

    return pl.pallas_call(
        body, name=name, grid=(nhb, nt),
        in_specs=_hgrn_specs(T, W, nhb) + [pl.BlockSpec((2, W), lambda hb, t: (0, hb)),
                                           pl.BlockSpec((1, LANES), lambda hb, t: (0, 0))],
        out_specs=[pl.BlockSpec((T, W), lambda hb, t: (t, hb)),
                   pl.BlockSpec((hps, ncc, LANES, LANES), lambda hb, t: (hb, t, 0, 0))],
        out_shape=[jax.ShapeDtypeStruct((Lp, D), BF16),
                   jax.ShapeDtypeStruct((H, Lp // HGRN_CHUNK, LANES, LANES), F32)],
        scratch_shapes=[pltpu.VMEM((hps, LANES, LANES), F32)],
        compiler_params=_params(("parallel", "arbitrary")),
    )(proj, proj, proj, proj, lbp, gg)


def _hgrn_bwd(proj, lbp, gg, dog, ss, T, name):
    Lp = proj.shape[0]
    D = proj.shape[1] // 4
    H = D // LANES
    hps = _hgrn_heads_per_step(H)
    W = hps * LANES
    nhb = H // hps
    assert nhb == 1, "d proj is written as whole rows: every head in one grid step"
    nt = Lp // T
    ncc = T // HGRN_CHUNK

    def body(q_ref, z_ref, i_ref, go_ref, p_ref, gg_ref, dog_ref, ss_ref, dproj_ref, dp_ref, dgg_ref, dst_ref):
        hb = pl.program_id(0)
        t = pl.program_id(1)

        @pl.when(t == 0)
        def _():
            dst_ref[...] = jnp.zeros_like(dst_ref)
            dp_ref[...] = jnp.zeros_like(dp_ref)

        @pl.when((t == 0) & (hb == 0))
        def _():
            dgg_ref[...] = jnp.zeros_like(dgg_ref)

        gain = gg_ref[...]
        row0 = (nt - 1 - t) * T

        def chunk(cc, carry):
            dstates, dps, dgain_sum = carry
            cidx = ncc - 1 - cc
            r0 = pl.multiple_of(cidx * HGRN_CHUNK, HGRN_CHUNK)
            sl = pl.ds(r0, HGRN_CHUNK)
            real = (row0 + r0 + lax.broadcasted_iota(jnp.int32, (HGRN_CHUNK, 1), 0)) >= N_PAD
            new_d, new_p = [], []
            for hh in range(hps):
                ln = slice(hh * LANES, (hh + 1) * LANES)
                _, vjp = jax.vjp(_hgrn_chunk, ss_ref[hh, cidx], q_ref[sl, ln], z_ref[sl, ln], i_ref[sl, ln],
                                 go_ref[sl, ln], p_ref[0:1, ln], p_ref[1:2, ln], gain)
                dSt, dq, dz, di, dgo, dp0, dp1, dgain = vjp((dstates[hh], dog_ref[sl, ln]))
                for grp, dval in enumerate((dq, dz, di, dgo)):
                    dproj_ref[sl, grp * D + hh * LANES:grp * D + (hh + 1) * LANES] = (
                        jnp.where(real, dval, 0.0).astype(BF16))
                new_d.append(dSt)
                new_p.append((dps[hh][0] + dp0, dps[hh][1] + dp1))
                dgain_sum = dgain_sum + dgain
            return tuple(new_d), tuple(new_p), dgain_sum

        zero_row = jnp.zeros((1, LANES), F32)
        init = (tuple(dst_ref[hh] for hh in range(hps)), tuple((zero_row, zero_row) for _ in range(hps)), zero_row)
        dstates, dps, dgain_sum = lax.fori_loop(0, ncc, chunk, init)
        for hh in range(hps):
            ln = slice(hh * LANES, (hh + 1) * LANES)
            dst_ref[hh] = dstates[hh]
            dp_ref[0:1, ln] += dps[hh][0]
            dp_ref[1:2, ln] += dps[hh][1]
        dgg_ref[0:1, :] += dgain_sum

    rev = pl.BlockSpec((T, W), lambda hb, t: (nt - 1 - t, hb))
    return pl.pallas_call(
        body, name=name, grid=(nhb, nt),
        in_specs=_hgrn_specs(T, W, nhb, nt) + [pl.BlockSpec((2, W), lambda hb, t: (0, hb)),
                                               pl.BlockSpec((1, LANES), lambda hb, t: (0, 0)), rev,
                                               pl.BlockSpec((hps, ncc, LANES, LANES),
                                                            lambda hb, t: (hb, nt - 1 - t, 0, 0))],
        out_specs=[pl.BlockSpec((T, 4 * D), lambda hb, t: (nt - 1 - t, 0)), pl.BlockSpec((8, W), lambda hb, t: (0, hb)),
                   pl.BlockSpec((8, LANES), lambda hb, t: (0, 0))],
        out_shape=[jax.ShapeDtypeStruct((Lp, 4 * D), BF16), jax.ShapeDtypeStruct((8, D), F32),
                   jax.ShapeDtypeStruct((8, LANES), F32)],
        scratch_shapes=[pltpu.VMEM((hps, LANES, LANES), F32)],
        compiler_params=_params(("arbitrary", "arbitrary")),
    )(proj, proj, proj, proj, lbp, gg, dog, ss)


def _xchg_sems(n_arr):
    return [pltpu.SemaphoreType.DMA((n_arr * (N_DEV - 1),)), pltpu.SemaphoreType.DMA((n_arr * (N_DEV - 1),)),
            pltpu.SemaphoreType.DMA((n_arr,))]


def _xchg_copies(ins, outs, per_peer, sems):
    send_sems, recv_sems, local_sems = sems
    x, y, c = lax.axis_index("x"), lax.axis_index("y"), lax.axis_index("c")
    me = 4 * x + 2 * y + c
    copies = []
    for n in range(len(ins)):
        src = ins[n].at[me] if per_peer[n] else ins[n]
        copies.append(pltpu.make_async_copy(src, outs[n].at[me], local_sems.at[n]))
    for rel in range(1, N_DEV):
        fx, fy, fc = (rel >> 2) & 1, (rel >> 1) & 1, rel & 1
        px = 1 - x if fx else x
        py = 1 - y if fy else y
        pc = 1 - c if fc else c
        peer = 4 * px + 2 * py + pc
        for n in range(len(ins)):
            src = ins[n].at[peer] if per_peer[n] else ins[n]
            copies.append(pltpu.make_async_remote_copy(
                src_ref=src, dst_ref=outs[n].at[me],
                send_sem=send_sems.at[n * (N_DEV - 1) + rel - 1],
                recv_sem=recv_sems.at[n * (N_DEV - 1) + rel - 1],
                device_id=(px, py, pc), device_id_type=pl.DeviceIdType.MESH))
    return copies


def _xchg_out_shapes(arrays, per_peer):
    return [jax.ShapeDtypeStruct(a.shape if pp else (N_DEV,) + a.shape, a.dtype) for a, pp in zip(arrays, per_peer)]


def _exchange(arrays, per_peer, name):
    n_arr = len(arrays)
    HBM = pl.BlockSpec(memory_space=pltpu.HBM)

    def body(*refs):
        copies = _xchg_copies(refs[:n_arr], refs[n_arr:2 * n_arr], per_peer, refs[2 * n_arr:])
        for cp in copies:
            cp.start()
        for cp in copies:
            cp.wait()

    return pl.pallas_call(
        body, name=name,
        in_specs=[HBM] * n_arr, out_specs=[HBM] * n_arr, out_shape=_xchg_out_shapes(arrays, per_peer),
        scratch_shapes=_xchg_sems(n_arr),
    )(*arrays)


ADAMW_VMEM_BUDGET = 36 * 1024 * 1024


def _adamw(recv, w, m, v, name):
    shape = w.shape
    C = shape[-1]
    R = math.prod(shape[:-1])
    lanes = -(-C // LANES) * LANES
    row_bytes = 2 * lanes * (N_DEV * recv.dtype.itemsize + 7 * 4)
    rc = _row_chunk(R, max(16, ADAMW_VMEM_BUDGET // row_bytes), 16 if recv.dtype == BF16 else 8)

    def body(r_ref, w_ref, m_ref, v_ref, g_ref, d_ref, mo_ref, vo_ref):
        g = r_ref[0].astype(F32)
        for s in range(1, N_DEV):
            g = g + r_ref[s].astype(F32)
        mn = ADAM_B1 * m_ref[...] + (1.0 - ADAM_B1) * g
        vn = ADAM_B2 * v_ref[...] + (1.0 - ADAM_B2) * (g * g)
        m_hat = mn / (1.0 - ADAM_B1 ** ADAM_STEP)
        v_hat = vn / (1.0 - ADAM_B2 ** ADAM_STEP)
        g_ref[...] = g
        d_ref[...] = -ADAM_LR * (m_hat / (jnp.sqrt(v_hat) + ADAM_EPS) + ADAM_WD * w_ref[...])
        mo_ref[...] = mn
        vo_ref[...] = vn

    row = pl.BlockSpec((rc, C), lambda i: (i, 0))
    outs = pl.pallas_call(
        body, name=name, grid=(R // rc,),
        in_specs=[pl.BlockSpec((N_DEV, rc, C), lambda i: (0, i, 0)), row, row, row],
        out_specs=[row] * 4,
        out_shape=[jax.ShapeDtypeStruct((R, C), F32)] * 4,
        compiler_params=_params(("parallel",)),
    )(recv.reshape(N_DEV, R, C), w.reshape(R, C), m.reshape(R, C), v.reshape(R, C))
    return [o.reshape(shape) for o in outs]


def _gathered_to_full(g, name):
    if name in COL_SHARDED:
        g = jnp.moveaxis(g, 0, -2)
        return g.reshape(g.shape[:-2] + (g.shape[-2] * g.shape[-1],))
    g = jnp.moveaxis(g, 0, -3)
    return g.reshape(g.shape[:-3] + (g.shape[-3] * g.shape[-2], g.shape[-1]))


def _full_to_slabs(full, name):
    if name in COL_SHARDED:
        f = full.reshape(full.shape[:-1] + (N_DEV, full.shape[-1] // N_DEV))
        return jnp.moveaxis(f, -2, 0)
    f = full.reshape(full.shape[:-2] + (N_DEV, full.shape[-2] // N_DEV, full.shape[-1]))
    return jnp.moveaxis(f, -3, 0)


def _pack_small(arrs):
    rows = []
    for a in arrs:
        flat = a.astype(F32).reshape(-1)
        pad = (-flat.shape[0]) % LANES
        rows.append(jnp.pad(flat, (0, pad)).reshape(-1, LANES))
    p = jnp.concatenate(rows, axis=0)
    return jnp.pad(p, ((0, (-p.shape[0]) % 8), (0, 0)))


def _unpack_small(packed, shapes):
    out, off = [], 0
    for shp in shapes:
        n = math.prod(shp)
        nr = -(-n // LANES)
        out.append(packed[off:off + nr].reshape(-1)[:n].reshape(shp))
        off += nr
    return out


def _local_step(x, target, meta, w_fox_in, w_fox_out, late, small):
    S, D = x.shape
    Lp = S + HEAD_ROWS
    T = ROW_TILE if Lp % ROW_TILE == 0 else HEAD_ROWS
    P = D // LANES
    row = lambda v: v.reshape(1, -1).astype(F32)

    w_fin = jnp.pad(w_fox_in[0], ((0, 0), (0, LANES - w_fox_in.shape[-1] % LANES)))
    w_fout = w_fox_out[0]
    n_heads = w_fox_in.shape[-1] - 4 * D
    bf = jnp.pad(row(small["fox_b_f"]), ((0, 0), (0, LANES - small["fox_b_f"].size)))
    qg = jnp.tile(row(small["fox_q_norm"]), (1, 2))
    kg = jnp.tile(row(small["fox_k_norm"]), (1, 2))

    h0 = jnp.concatenate([jnp.zeros((N_PAD, D), F32), meta, x], axis=0)

    hn0 = _rms_fwd(h0, row(small["attn_norm"][0]), T, "rms0_fwd")
    proj0 = _mm(hn0, w_fin, "nn", F32, "fox_in_fwd")
    qn, kn, vb, c = _fox_prep_fwd(proj0, bf, qg, kg, T, D, "fox_prep_fwd")
    cT = c.T[:2 * P]
    o, og0, mshift, linv, *gathered = _fox_attn_fwd(qn, kn, vb, c, cT, proj0, [late[n] for n in LATE], T, D,
                                                    "fox_attn_fwd")
    wl = {n: _gathered_to_full(g, n) for n, g in zip(LATE, gathered)}
    w_hin, w_hout, w_uin, w_uout = wl["hgrn_w_in"][0], wl["hgrn_w_out"][0], wl["ffn_w_in"], wl["ffn_w_out"]
    h1, hf0 = _out_proj_fwd(og0, w_fout, h0, row(small["ffn_norm"][0]), "fox_out_fwd")
    gu0 = _ffn_in_fwd(hf0, w_uin[0], "ffn0_in_fwd")
    act0 = gu0[2]
    h2, hn1 = _out_proj_fwd(act0, w_uout[0], h1, row(small["attn_norm"][1]), "ffn0_out_fwd")
    proj1 = _mm(hn1, w_hin, "nn", F32, "hgrn_in_fwd")
    lbp = small["hgrn_lower_bounds"].astype(F32)
    ggn = row(small["hgrn_g_norm"])
    Th = HGRN_TILE if Lp % HGRN_TILE == 0 else HEAD_ROWS
    og1, ss = _hgrn_fwd(proj1, lbp, ggn, Th, "hgrn_fwd")
    h3, hf1 = _out_proj_fwd(og1, w_hout, h2, row(small["ffn_norm"][1]), "hgrn_out_fwd")
    gu1 = _ffn_in_fwd(hf1, w_uin[1], "ffn1_in_fwd")
    act1 = gu1[2]
    h4 = _mm(act1, w_uout[1], "nn", F32, "ffn1_out_fwd", res=h3)
    loss_blk, dh4, dh4b, d_final = _final_loss(h4, row(small["final_norm"]), target, "final_loss")

    grads = {}

    def ffn_bwd(i, dh, dhb, h_in, hf, gu, act, tag):
        grads_out = _mm(act, dhb, "tn", F32, f"ffn{i}_out_dw")
        dgu = _ffn_out_dx(dhb, w_uout[i], gu[0], gu[1], f"ffn{i}_out_dx")
        grads_in = _mm(hf, dgu, "tn", F32, f"ffn{i}_in_dw")
        dh_new, dh_newb, dgain = _in_proj_dx(dgu, w_uin[i], h_in, row(small["ffn_norm"][i]), dh, f"ffn{i}_in_dx")
        return dh_new, dh_newb, grads_in, grads_out, dgain

    dh3, dh3b, g_uin1, g_uout1, d_fn1 = ffn_bwd(1, dh4, dh4b, h3, hf1, gu1, act1, "1")
    grads["hgrn_w_out"] = _mm(og1, dh3b, "tn", F32, "hgrn_out_dw")[None]
    dog1 = _mm(dh3b, w_hout, "nt", F32, "hgrn_out_dx")
    dproj1, d_lb, d_gg = _hgrn_bwd(proj1, lbp, ggn, dog1, ss, Th, "hgrn_bwd")
    grads["hgrn_w_in"] = _mm(hn1, dproj1, "tn", F32, "hgrn_in_dw")[None]
    dh2, dh2b, d_an1 = _in_proj_dx(dproj1, w_hin, h2, row(small["attn_norm"][1]), dh3, "hgrn_in_dx")
    dh1, dh1b, g_uin0, g_uout0, d_fn0 = ffn_bwd(0, dh2, dh2b, h1, hf0, gu0, act0, "0")
    grads["ffn_w_in"] = jnp.stack([g_uin0, g_uin1])
    grads["ffn_w_out"] = jnp.stack([g_uout0, g_uout1])
    grads["fox_w_out"] = _mm(og0, dh1b, "tn", F32, "fox_out_dw")[None]
    do, dgate, delta = _fox_out_dx(dh1b, w_fout, o, proj0, linv, D, "fox_out_dx")
    slabs = [_full_to_slabs(grads[n], n).astype(BF16) for n in LATE]
    dqn, dkn, dv, dcr, *recv = _fox_attn_bwd(qn, kn, vb, c, cT, do, mshift, delta, slabs, T, D, "fox_attn_bwd")
    for n in LATE:
        del grads[n]
    dc = jnp.pad(dcr[:, :2, :].reshape(2 * P, Lp).T, ((0, 0), (0, LANES - 2 * P)))
    Tp = T // 2 if T == ROW_TILE else T
    dproj0, sm = _fox_prep_bwd(proj0, bf, qg, kg, dqn, dkn, dv, dgate, dc, Tp, D, "fox_prep_bwd")
    grads["fox_w_in"] = _mm(hn0, dproj0, "tn", F32, "fox_in_dw")[:, :4 * D + n_heads][None]
    dh0, _, d_an0 = _in_proj_dx(dproj0, w_fin, h0, row(small["attn_norm"][0]), dh1, "fox_in_dx")

    grads["meta_tokens"] = dh0[N_PAD:HEAD_ROWS]
    grads["attn_norm"] = jnp.concatenate([d_an0, d_an1], axis=0)
    grads["ffn_norm"] = jnp.concatenate([d_fn0, d_fn1], axis=0)
    grads["final_norm"] = d_final[0]
    grads["fox_b_f"] = sm[0:1, :n_heads]
    grads["fox_q_norm"] = sm[1:2, :FOX_DH] + sm[1:2, FOX_DH:]
    grads["fox_k_norm"] = sm[2:3, :FOX_DH] + sm[2:3, FOX_DH:]
    grads["hgrn_lower_bounds"] = d_lb[0:2]
    grads["hgrn_g_norm"] = d_gg[0:1]
    return loss_blk[0, 0], dh0[HEAD_ROWS:], grads, dict(zip(LATE, recv))


def kernel(x, meta_tokens, attn_norm, ffn_norm, final_norm, fox_w_in, fox_b_f, fox_q_norm, fox_k_norm, fox_w_out, hgrn_w_in, hgrn_lower_bounds, hgrn_g_norm, hgrn_w_out, ffn_w_in, ffn_w_out, loss_target, m_meta_tokens, m_attn_norm, m_ffn_norm, m_final_norm, m_fox_w_in, m_fox_b_f, m_fox_q_norm, m_fox_k_norm, m_fox_w_out, m_hgrn_w_in, m_hgrn_lower_bounds, m_hgrn_g_norm, m_hgrn_w_out, m_ffn_w_in, m_ffn_w_out, v_meta_tokens, v_attn_norm, v_ffn_norm, v_final_norm, v_fox_w_in, v_fox_b_f, v_fox_q_norm, v_fox_k_norm, v_fox_w_out, v_hgrn_w_in, v_hgrn_lower_bounds, v_hgrn_g_norm, v_hgrn_w_out, v_ffn_w_in, v_ffn_w_out):
    w = dict(meta_tokens=meta_tokens, attn_norm=attn_norm, ffn_norm=ffn_norm, final_norm=final_norm,
             fox_w_in=fox_w_in, fox_b_f=fox_b_f, fox_q_norm=fox_q_norm, fox_k_norm=fox_k_norm,
             fox_w_out=fox_w_out, hgrn_w_in=hgrn_w_in, hgrn_lower_bounds=hgrn_lower_bounds,
             hgrn_g_norm=hgrn_g_norm, hgrn_w_out=hgrn_w_out, ffn_w_in=ffn_w_in, ffn_w_out=ffn_w_out)
    m = dict(meta_tokens=m_meta_tokens, attn_norm=m_attn_norm, ffn_norm=m_ffn_norm, final_norm=m_final_norm,
             fox_w_in=m_fox_w_in, fox_b_f=m_fox_b_f, fox_q_norm=m_fox_q_norm, fox_k_norm=m_fox_k_norm,
             fox_w_out=m_fox_w_out, hgrn_w_in=m_hgrn_w_in, hgrn_lower_bounds=m_hgrn_lower_bounds,
             hgrn_g_norm=m_hgrn_g_norm, hgrn_w_out=m_hgrn_w_out, ffn_w_in=m_ffn_w_in, ffn_w_out=m_ffn_w_out)
    v = dict(meta_tokens=v_meta_tokens, attn_norm=v_attn_norm, ffn_norm=v_ffn_norm, final_norm=v_final_norm,
             fox_w_in=v_fox_w_in, fox_b_f=v_fox_b_f, fox_q_norm=v_fox_q_norm, fox_k_norm=v_fox_k_norm,
             fox_w_out=v_fox_w_out, hgrn_w_in=v_hgrn_w_in, hgrn_lower_bounds=v_hgrn_lower_bounds,
             hgrn_g_norm=v_hgrn_g_norm, hgrn_w_out=v_hgrn_w_out, ffn_w_in=v_ffn_w_in, ffn_w_out=v_ffn_w_out)
    axes = ("x", "y", "c")
    small_shapes = [w[n].shape for n in SMALL]

    g_meta, g_fin, g_fout = _exchange([w["meta_tokens"].astype(F32), w["fox_w_in"].astype(BF16),
                                       w["fox_w_out"].astype(BF16)], [False] * 3, "gather_weights")
    loss_local, grad_x, grads, recv = _local_step(
        x[0], loss_target[0], _gathered_to_full(g_meta, "meta_tokens"), _gathered_to_full(g_fin, "fox_w_in"),
        _gathered_to_full(g_fout, "fox_w_out"), {n: w[n].astype(BF16) for n in LATE}, {n: w[n] for n in SMALL})
    loss = lax.psum(loss_local, axes)

    r_meta, r_fin, r_fout, r_small = _exchange(
        [_full_to_slabs(grads["meta_tokens"], "meta_tokens"), _full_to_slabs(grads["fox_w_in"], "fox_w_in").astype(BF16),
         _full_to_slabs(grads["fox_w_out"], "fox_w_out").astype(BF16), _pack_small([grads[n] for n in SMALL])],
        [True, True, True, False], "scatter_grads")
    recv.update(meta_tokens=r_meta, fox_w_in=r_fin, fox_w_out=r_fout)

    res = {n: _adamw(recv[n], w[n], m[n], v[n], "adamw_" + n) for n in BIG}
    sml = _adamw(r_small, _pack_small([w[n] for n in SMALL]), _pack_small([m[n] for n in SMALL]),
                 _pack_small([v[n] for n in SMALL]), "adamw_small")
    outs = []
    for k in range(4):
        d = {n: res[n][k] for n in BIG}
        d.update(zip(SMALL, _unpack_small(sml[k], small_shapes)))
        outs.extend(d[n] for n in WEIGHTS)
    return (loss, grad_x[None], *outs)
```

```python
import functools
import math

import jax
import jax.numpy as jnp
from jax import lax
from jax.experimental import pallas as pl
from jax.experimental.pallas import tpu as pltpu

F32 = jnp.float32
BF16 = jnp.bfloat16
EPS = 1e-6
N_META = 16
LANES = 128
HEAD_ROWS = 256
ROW_TILE = 768
N_PAD = HEAD_ROWS - N_META
FOX_DH = 64
HGRN_CHUNK = 64
HGRN_TILE = 384
N_DEV = 8
NEG = -1e30
PAD_SHIFT = 1e4
VMEM_LIMIT = 56 * 1024 * 1024
HI = lax.Precision.HIGHEST

ADAM_LR = 0.001
ADAM_B1 = 0.9
ADAM_B2 = 0.999
ADAM_EPS = 1e-08
ADAM_WD = 0.01
ADAM_STEP = 10

BIG = ("meta_tokens", "fox_w_in", "fox_w_out", "hgrn_w_in", "hgrn_w_out", "ffn_w_in", "ffn_w_out")
SMALL = ("attn_norm", "ffn_norm", "final_norm", "fox_b_f", "fox_q_norm", "fox_k_norm",
         "hgrn_lower_bounds", "hgrn_g_norm")
WEIGHTS = ("meta_tokens", "attn_norm", "ffn_norm", "final_norm", "fox_w_in", "fox_b_f", "fox_q_norm",
           "fox_k_norm", "fox_w_out", "hgrn_w_in", "hgrn_lower_bounds", "hgrn_g_norm", "hgrn_w_out",
           "ffn_w_in", "ffn_w_out")
COL_SHARDED = ("meta_tokens", "fox_w_in", "hgrn_w_in", "ffn_w_in")
LATE = ("hgrn_w_in", "hgrn_w_out", "ffn_w_in", "ffn_w_out")


def _params(sem=None):
    return pltpu.CompilerParams(dimension_semantics=sem, vmem_limit_bytes=VMEM_LIMIT)


def _tile(n, cap):
    best = None
    for t in range(LANES, min(n, cap) + 1, LANES):
        if n % t == 0:
            best = t
    assert best is not None, (n, cap)
    return best


def _row_chunk(n, cap, mult=8):
    best = n
    for t in range(mult, min(n, cap) + 1, mult):
        if n % t == 0:
            best = t
    return best


def _dg(a, b, ca, cb):
    return lax.dot_general(a.astype(BF16), b.astype(BF16), (((ca,), (cb,)), ((), ())),
                           preferred_element_type=F32)


@jax.custom_vjp
def _d_nn(a, b):
    return _dg(a, b, 1, 0)


@jax.custom_vjp
def _d_nt(a, b):
    return _dg(a, b, 1, 1)


@jax.custom_vjp
def _d_tn(a, b):
    return _dg(a, b, 0, 0)


_d_nn.defvjp(lambda a, b: (_d_nn(a, b), (a, b)), lambda r, g: (_d_nt(g, r[1]), _d_tn(r[0], g)))
_d_nt.defvjp(lambda a, b: (_d_nt(a, b), (a, b)), lambda r, g: (_d_nn(g, r[1]), _d_tn(g, r[0])))
_d_tn.defvjp(lambda a, b: (_d_tn(a, b), (a, b)), lambda r, g: (_d_nt(r[1], g), _d_nn(r[0], g)))


def _log_sigmoid(x):
    return jnp.minimum(x, 0.0) - jnp.log1p(jnp.exp(-jnp.abs(x)))


def _rms(x, g):
    return x * lax.rsqrt(jnp.mean(x * x, axis=-1, keepdims=True) + EPS) * g


def _mm(a, b, mode, out_dtype, name, res=None, tm=None, tn=None, tk=None):
    assert a.dtype == BF16 and b.dtype == BF16, (name, a.dtype, b.dtype)
    if mode == "nn":
        (M, K), N = a.shape, b.shape[1]
    elif mode == "nt":
        (M, K), N = a.shape, b.shape[0]
    else:
        (K, M), N = a.shape, b.shape[1]
    if mode == "nn":
        tm, tn, tk = tm or _tile(M, ROW_TILE), tn or _tile(N, 1408), tk or _tile(K, 2816)
    elif mode == "nt":
        tm, tn, tk = tm or _tile(M, ROW_TILE if K <= 2048 else ROW_TILE // 2), tn or N, tk or K
    else:
        tn = tn or _tile(N, 2816)
        tm, tk = tm or _tile(M, 1408 if tn == N else 512), tk or _tile(K, ROW_TILE)
    nk = K // tk
    if mode == "tn":
        a_spec = pl.BlockSpec((tk, tm), lambda j, i, k: (k, i))
        dims = (((0,), (0,)), ((), ()))
    else:
        a_spec = pl.BlockSpec((tm, tk), lambda j, i, k: (i, k))
        dims = (((1,), (1 if mode == "nt" else 0,)), ((), ()))
    if mode == "nt":
        b_spec = pl.BlockSpec((tn, tk), lambda j, i, k: (j, k))
    else:
        b_spec = pl.BlockSpec((tk, tn), lambda j, i, k: (k, j))

    o_spec = pl.BlockSpec((tm, tn), lambda j, i, k: (i, j))

    def body(a_ref, b_ref, *rest):
        r_ref = rest[0] if res is not None else None
        o_ref, acc_ref = rest[-2:]
        k = pl.program_id(2)

        @pl.when(k == 0)
        def _():
            acc_ref[...] = jnp.zeros_like(acc_ref)

        acc_ref[...] += lax.dot_general(a_ref[...], b_ref[...], dims, preferred_element_type=F32)

        @pl.when(k == nk - 1)
        def _():
            out = acc_ref[...] if r_ref is None else acc_ref[...] + r_ref[...]
            o_ref[...] = out.astype(out_dtype)

    return pl.pallas_call(
        body, name=name, grid=(N // tn, M // tm, nk),
        in_specs=[a_spec, b_spec] + ([o_spec] if res is not None else []),
        out_specs=o_spec,
        out_shape=jax.ShapeDtypeStruct((M, N), out_dtype),
        scratch_shapes=[pltpu.VMEM((tm, tn), F32)],
        compiler_params=_params(("parallel", "parallel", "arbitrary")),
    )(a, b, *([res] if res is not None else []))


def _out_proj_fwd(a, w, res, gain, name):
    Lp, K = a.shape
    D = w.shape[1]
    tm = _tile(Lp, ROW_TILE)

    def body(a_ref, w_ref, r_ref, g_ref, h_ref, hn_ref):
        h = jnp.dot(a_ref[...], w_ref[...], preferred_element_type=F32) + r_ref[...]
        h_ref[...] = h
        hn_ref[...] = _rms(h, g_ref[...]).astype(BF16)

    row = pl.BlockSpec((tm, D), lambda i: (i, 0))
    return pl.pallas_call(
        body, name=name, grid=(Lp // tm,),
        in_specs=[pl.BlockSpec((tm, K), lambda i: (i, 0)), pl.BlockSpec((K, D), lambda i: (0, 0)), row,
                  pl.BlockSpec((1, D), lambda i: (0, 0))],
        out_specs=[row, row],
        out_shape=[jax.ShapeDtypeStruct((Lp, D), F32), jax.ShapeDtypeStruct((Lp, D), BF16)],
        compiler_params=_params(("parallel",)),
    )(a, w, res, gain)


def _in_proj_dx(dy, w, x, gain, dres, name):
    Lp, N = dy.shape
    D = w.shape[0]
    tm = _tile(Lp, ROW_TILE // 2)

    def body(dy_ref, w_ref, x_ref, g_ref, dr_ref, dx_ref, dxb_ref, dg_ref):
        @pl.when(pl.program_id(0) == 0)
        def _():
            dg_ref[...] = jnp.zeros_like(dg_ref)

        dhn = lax.dot_general(dy_ref[...], w_ref[...], (((1,), (1,)), ((), ())), preferred_element_type=F32)
        _, vjp = jax.vjp(_rms, x_ref[...], g_ref[...])
        dx, dg = vjp(dhn)
        dx = dx + dr_ref[...]
        dx_ref[...] = dx
        dxb_ref[...] = dx.astype(BF16)
        dg_ref[...] += dg

    row = pl.BlockSpec((tm, D), lambda i: (i, 0))
    vec = pl.BlockSpec((1, D), lambda i: (0, 0))
    return pl.pallas_call(
        body, name=name, grid=(Lp // tm,),
        in_specs=[pl.BlockSpec((tm, N), lambda i: (i, 0)), pl.BlockSpec((D, N), lambda i: (0, 0)), row, vec, row],
        out_specs=[row, row, vec],
        out_shape=[jax.ShapeDtypeStruct((Lp, D), F32), jax.ShapeDtypeStruct((Lp, D), BF16),
                   jax.ShapeDtypeStruct((1, D), F32)],
        compiler_params=_params(("arbitrary",)),
    )(dy, w, x, gain, dres)


def _rms_fwd(x, g, T, name):
    Lp, D = x.shape

    def body(x_ref, g_ref, o_ref):
        o_ref[...] = _rms(x_ref[...], g_ref[...]).astype(BF16)

    return pl.pallas_call(
        body, name=name, grid=(Lp // T,),
        in_specs=[pl.BlockSpec((T, D), lambda i: (i, 0)), pl.BlockSpec((1, D), lambda i: (0, 0))],
        out_specs=pl.BlockSpec((T, D), lambda i: (i, 0)),
        out_shape=jax.ShapeDtypeStruct((Lp, D), BF16),
        compiler_params=_params(("parallel",)),
    )(x, g)


def _swiglu(gate, up):
    return gate * jax.nn.sigmoid(gate) * up


def _ffn_in_fwd(hf, w_in, name):
    Lp, D = hf.shape
    F = w_in.shape[1] // 2
    tm = _tile(Lp, ROW_TILE)
    tn = _tile(F, 1408)
    nj = F // tn

    def body(a_ref, bg_ref, bu_ref, g_ref, u_ref, act_ref):
        a = a_ref[...]
        g = jnp.dot(a, bg_ref[...], preferred_element_type=F32)
        u = jnp.dot(a, bu_ref[...], preferred_element_type=F32)
        g_ref[...] = g.astype(BF16)
        u_ref[...] = u.astype(BF16)
        act_ref[...] = _swiglu(g, u).astype(BF16)

    tile = pl.BlockSpec((tm, tn), lambda j, i: (i, j))
    return pl.pallas_call(
        body, name=name, grid=(nj, Lp // tm),
        in_specs=[pl.BlockSpec((tm, D), lambda j, i: (i, 0)), pl.BlockSpec((D, tn), lambda j, i: (0, j)),
                  pl.BlockSpec((D, tn), lambda j, i: (0, nj + j))],
        out_specs=[tile, tile, tile],
        out_shape=[jax.ShapeDtypeStruct((Lp, F), BF16)] * 3,
        compiler_params=_params(("parallel", "parallel")),
    )(hf, w_in, w_in)


def _ffn_out_dx(dhb, w_out, g, u, name):
    Lp, D = dhb.shape
    F = w_out.shape[0]
    tm = HEAD_ROWS

    def body(a_ref, b_ref, g_ref, u_ref, o_ref):
        dact = lax.dot_general(a_ref[...], b_ref[...], (((1,), (1,)), ((), ())), preferred_element_type=F32)
        _, vjp = jax.vjp(_swiglu, g_ref[...].astype(F32), u_ref[...].astype(F32))
        dg, du = vjp(dact)
        o_ref[:, :F] = dg.astype(BF16)
        o_ref[:, F:] = du.astype(BF16)

    wide = pl.BlockSpec((tm, F), lambda i: (i, 0))
    return pl.pallas_call(
        body, name=name, grid=(Lp // tm,),
        in_specs=[pl.BlockSpec((tm, D), lambda i: (i, 0)), pl.BlockSpec((F, D), lambda i: (0, 0)), wide, wide],
        out_specs=pl.BlockSpec((tm, 2 * F), lambda i: (i, 0)),
        out_shape=jax.ShapeDtypeStruct((Lp, 2 * F), BF16),
        compiler_params=_params(("parallel",)),
    )(dhb, w_out, g, u)


def _final_loss(h, g, target, name):
    Lp, D = h.shape
    TR = HEAD_ROWS

    def loss_fn(hh, gg, tt):
        err = _rms(hh, gg) - tt
        return 0.5 * jnp.sum(jnp.mean(err * err, axis=-1))

    def body(h_ref, g_ref, t_ref, loss_ref, dh_ref, dhb_ref, dg_ref):
        i = pl.program_id(0)

        @pl.when(i == 0)
        def _():
            loss_ref[...] = jnp.zeros_like(loss_ref)
            dg_ref[...] = jnp.zeros_like(dg_ref)
            dh_ref[...] = jnp.zeros_like(dh_ref)
            dhb_ref[...] = jnp.zeros_like(dhb_ref)

        @pl.when(i > 0)
        def _():
            val, vjp = jax.vjp(lambda hh, gg: loss_fn(hh, gg, t_ref[...]), h_ref[...], g_ref[...])
            dh, dg = vjp(jnp.ones((), F32))
            dh_ref[...] = dh
            dhb_ref[...] = dh.astype(BF16)
            dg_ref[...] += dg
            loss_ref[...] += val

    row = pl.BlockSpec((TR, D), lambda i: (i, 0))
    return pl.pallas_call(
        body, name=name, grid=(Lp // TR,),
        in_specs=[row, pl.BlockSpec((1, D), lambda i: (0, 0)),
                  pl.BlockSpec((TR, D), lambda i: (jnp.maximum(i - 1, 0), 0))],
        out_specs=[pl.BlockSpec((8, LANES), lambda i: (0, 0)), row, row, pl.BlockSpec((1, D), lambda i: (0, 0))],
        out_shape=[jax.ShapeDtypeStruct((8, LANES), F32), jax.ShapeDtypeStruct((Lp, D), F32),
                   jax.ShapeDtypeStruct((Lp, D), BF16), jax.ShapeDtypeStruct((1, D), F32)],
        compiler_params=_params(("arbitrary",)),
    )(h, g, target)


def _lane_lo():
    return lax.broadcasted_iota(jnp.int32, (1, LANES), 1) < FOX_DH


def _headnorm(x, g, scale):
    lo = _lane_lo()
    x2 = x * x
    s0 = jnp.sum(jnp.where(lo, x2, 0.0), axis=-1, keepdims=True)
    s1 = jnp.sum(jnp.where(lo, 0.0, x2), axis=-1, keepdims=True)
    r = jnp.where(lo, lax.rsqrt(s0 / FOX_DH + EPS), lax.rsqrt(s1 / FOX_DH + EPS))
    return x * r * g * scale


def _fox_prep_fwd(proj, bf, qg, kg, T, D, name):
    Lp = proj.shape[0]
    nb = D // LANES
    scale = FOX_DH ** -0.5

    def body(q_ref, k_ref, v_ref, fl_ref, bf_ref, qg_ref, kg_ref, qn_ref, kn_ref, vb_ref, c_ref, carry_ref):
        @pl.when(pl.program_id(0) == 0)
        def _():
            carry_ref[...] = jnp.zeros_like(carry_ref)

        for b in range(nb):
            sl = slice(b * LANES, (b + 1) * LANES)
            qn_ref[:, sl] = _headnorm(q_ref[:, sl], qg_ref[...], scale).astype(BF16)
            kn_ref[:, sl] = _headnorm(k_ref[:, sl], kg_ref[...], 1.0).astype(BF16)
        vb_ref[...] = v_ref[...].astype(BF16)
        log_f = _log_sigmoid(fl_ref[...] + bf_ref[...])
        row = lax.broadcasted_iota(jnp.int32, (T, T), 0)
        col = lax.broadcasted_iota(jnp.int32, (T, T), 1)
        tri = (col <= row).astype(F32)
        c = jnp.dot(tri, log_f, precision=HI, preferred_element_type=F32) + carry_ref[...]
        c_ref[...] = c
        last = lax.broadcasted_iota(jnp.int32, (T, 1), 0) == T - 1
        carry_ref[...] = jnp.sum(jnp.where(last, c, 0.0), axis=0, keepdims=True)

    wide = lambda j: pl.BlockSpec((T, D), lambda i: (i, j))
    vec = pl.BlockSpec((1, LANES), lambda i: (0, 0))
    return pl.pallas_call(
        body, name=name, grid=(Lp // T,),
        in_specs=[wide(0), wide(1), wide(2), pl.BlockSpec((T, LANES), lambda i: (i, 4 * nb)), vec, vec, vec],
        out_specs=[wide(0), wide(0), wide(0), pl.BlockSpec((T, LANES), lambda i: (i, 0))],
        out_shape=[jax.ShapeDtypeStruct((Lp, D), BF16)] * 3 + [jax.ShapeDtypeStruct((Lp, LANES), F32)],
        scratch_shapes=[pltpu.VMEM((1, LANES), F32)],
        compiler_params=_params(("arbitrary",)),
    )(proj, proj, proj, proj, bf, qg, kg)


def _fox_prep_bwd(proj, bf, qg, kg, dqn, dkn, dv, dgate, dc, T, D, name):
    Lp = proj.shape[0]
    nb = D // LANES
    nt = Lp // T
    scale = FOX_DH ** -0.5

    def body(q_ref, k_ref, fl_ref, bf_ref, qg_ref, kg_ref, dqn_ref, dkn_ref, dv_ref, dgate_ref, dc_ref,
             dproj_ref, sm_ref, carry_ref):
        @pl.when(pl.program_id(0) == 0)
        def _():
            carry_ref[...] = jnp.zeros_like(carry_ref)
            sm_ref[...] = jnp.zeros_like(sm_ref)

        dqg = jnp.zeros((1, LANES), F32)
        dkg = jnp.zeros((1, LANES), F32)
        for b in range(nb):
            sl = slice(b * LANES, (b + 1) * LANES)
            _, vjp = jax.vjp(lambda x, g: _headnorm(x, g, scale), q_ref[:, sl], qg_ref[...])
            dx, dg = vjp(dqn_ref[:, sl])
            dproj_ref[:, sl] = dx.astype(BF16)
            dqg = dqg + dg
            _, vjp = jax.vjp(lambda x, g: _headnorm(x, g, 1.0), k_ref[:, sl], kg_ref[...])
            dx, dg = vjp(dkn_ref[:, sl])
            dproj_ref[:, D + b * LANES:D + (b + 1) * LANES] = dx.astype(BF16)
            dkg = dkg + dg
        dproj_ref[:, 2 * D:3 * D] = dv_ref[...].astype(BF16)
        dproj_ref[:, 3 * D:4 * D] = dgate_ref[...]
        dcv = dc_ref[...]
        row = lax.broadcasted_iota(jnp.int32, (T, T), 0)
        col = lax.broadcasted_iota(jnp.int32, (T, T), 1)
        triu = (col >= row).astype(F32)
        dlogf = jnp.dot(triu, dcv, precision=HI, preferred_element_type=F32) + carry_ref[...]
        carry_ref[...] += jnp.sum(dcv, axis=0, keepdims=True)
        _, vjp = jax.vjp(_log_sigmoid, fl_ref[...] + bf_ref[...])
        (dfl,) = vjp(dlogf)
        dproj_ref[:, 4 * D:] = dfl.astype(BF16)
        sm_ref[0:1, :] += jnp.sum(dfl, axis=0, keepdims=True)
        sm_ref[1:2, :] += dqg
        sm_ref[2:3, :] += dkg

    wide = lambda j: pl.BlockSpec((T, D), lambda i: (nt - 1 - i, j))
    narrow = lambda j: pl.BlockSpec((T, LANES), lambda i: (nt - 1 - i, j))
    vec = pl.BlockSpec((1, LANES), lambda i: (0, 0))
    return pl.pallas_call(
        body, name=name, grid=(nt,),
        in_specs=[wide(0), wide(1), narrow(4 * nb), vec, vec, vec, wide(0), wide(0), wide(0), wide(0), narrow(0)],
        out_specs=[pl.BlockSpec((T, 4 * D + LANES), lambda i: (nt - 1 - i, 0)), pl.BlockSpec((8, LANES), lambda i: (0, 0))],
        out_shape=[jax.ShapeDtypeStruct((Lp, 4 * D + LANES), BF16), jax.ShapeDtypeStruct((8, LANES), F32)],
        scratch_shapes=[pltpu.VMEM((1, LANES), F32)],
        compiler_params=_params(("arbitrary",)),
    )(proj, proj, proj, bf, qg, kg, dqn, dkn, dv, dgate, dc)


def _ln2_ceil(m):
    return jnp.ceil(m * (1.0 / math.log(2.0))) * math.log(2.0)


def _fox_mask(i, k0, T):
    qpos = i * T + lax.broadcasted_iota(jnp.int32, (T, 1), 0)
    kpos = k0 + lax.broadcasted_iota(jnp.int32, (1, T), 1)
    return (kpos <= qpos) & ((kpos >= N_PAD) | (qpos < N_PAD))


def _pick_col(blk, idx):
    lane = lax.broadcasted_iota(jnp.int32, (1, LANES), 1)
    return jnp.sum(jnp.where(lane == idx, blk, 0.0), axis=1, keepdims=True)


def _split_halves(blk):
    lo = _lane_lo()
    return (jnp.max(jnp.where(lo, blk, -jnp.inf), axis=1, keepdims=True),
            jnp.max(jnp.where(lo, -jnp.inf, blk), axis=1, keepdims=True))


def _fox_attn_fwd(qn, kn, vb, c, cT, proj, xchg, T, D, name):
    Lp = qn.shape[0]
    P = D // LANES
    nt = Lp // T
    H = cT.shape[0]
    nx = len(xchg)

    def body(q_ref, k_ref, v_ref, c_ref, cT_ref, g_ref, *rest):
        x_in, (o_ref, og_ref, m_ref, li_ref), x_out, sems = rest[:nx], rest[nx:nx + 4], rest[nx + 4:2 * nx + 4], rest[2 * nx + 4:]
        p = pl.program_id(0)
        i = pl.program_id(1)

        @pl.when((p == 0) & (i == 0))
        def _():
            for cp in _xchg_copies(x_in, x_out, [False] * nx, sems):
                cp.start()

        lo = _lane_lo()
        q = q_ref[...]
        zero = jnp.zeros_like(q)
        qh = (jnp.where(lo, q, zero), jnp.where(lo, zero, q))
        cblk = c_ref[...]
        cq = tuple(_pick_col(cblk, 2 * p + h) for h in (0, 1))
        one = jnp.ones_like(q)

        def step(j, carry, masked):
            k0 = pl.multiple_of(j * T, LANES)
            kj = k_ref[pl.ds(k0, T), :]
            vj = v_ref[pl.ds(k0, T), :]
            vh = (jnp.where(lo, vj, one), jnp.where(lo, one, vj))
            mask = _fox_mask(i, k0, T) if masked else None
            out = []
            for h in (0, 1):
                m, acc = carry[h]
                ck = cT_ref[pl.ds(2 * p + h, 1), pl.ds(k0, T)]
                t = lax.dot_general(qh[h], kj, (((1,), (1,)), ((), ())), preferred_element_type=F32) - ck
                if masked:
                    t = jnp.where(mask, t, NEG)
                m_new = _ln2_ceil(jnp.maximum(m, cq[h] + jnp.max(t, axis=1, keepdims=True)))
                pr = jnp.exp(t + (cq[h] - m_new)).astype(BF16)
                acc = jnp.exp(m - m_new) * acc + jnp.dot(pr, vh[h], preferred_element_type=F32)
                out.append((m_new, acc))
            return tuple(out)

        init = tuple((jnp.full((T, 1), NEG, F32), jnp.zeros((T, LANES), F32)) for _ in (0, 1))
        carry = lax.fori_loop(0, i, lambda j, cr: step(j, cr, False), init)
        (m0, a0), (m1, a1) = step(i, carry, True)
        l0 = pltpu.roll(a0, FOX_DH, 1)
        l1 = pltpu.roll(a1, FOX_DH, 1)
        o = jnp.where(lo, a0 / l0, a1 / l1)
        o_ref[...] = o
        m_ref[...] = jnp.where(lo, m0, m1)
        li_ref[...] = jnp.where(lo, 1.0 / l0, 1.0 / l1)
        og_ref[...] = (o * jax.nn.sigmoid(g_ref[...])).astype(BF16)

        @pl.when((p == P - 1) & (i == nt - 1))
        def _():
            for cp in _xchg_copies(x_in, x_out, [False] * nx, sems):
                cp.wait()

    tile = pl.BlockSpec((T, LANES), lambda p, i: (i, p))
    full = pl.BlockSpec((Lp, LANES), lambda p, i: (0, p))
    HBM = pl.BlockSpec(memory_space=pltpu.HBM)
    return pl.pallas_call(
        body, name=name, grid=(P, nt),
        in_specs=[tile, full, full, pl.BlockSpec((T, LANES), lambda p, i: (i, 0)),
                  pl.BlockSpec((H, Lp), lambda p, i: (0, 0)),
                  pl.BlockSpec((T, LANES), lambda p, i: (i, 3 * P + p))] + [HBM] * nx,
        out_specs=[tile, tile, tile, tile] + [HBM] * nx,
        out_shape=[jax.ShapeDtypeStruct((Lp, D), F32), jax.ShapeDtypeStruct((Lp, D), BF16),
                   jax.ShapeDtypeStruct((Lp, D), F32), jax.ShapeDtypeStruct((Lp, D), F32)]
        + _xchg_out_shapes(xchg, [False] * nx),
        scratch_shapes=_xchg_sems(nx),
        compiler_params=_params(("arbitrary", "arbitrary")),
    )(qn, kn, vb, c, cT, proj, *xchg)


def _fox_out_dx(dhb, w_out, o, proj, linv, D, name):
    Lp = o.shape[0]
    tm = HEAD_ROWS

    def body(a_ref, w_ref, o_ref, g_ref, li_ref, do_ref, dg_ref, dl_ref):
        lo = _lane_lo()
        dog_all = lax.dot_general(a_ref[...], w_ref[...], (((1,), (1,)), ((), ())), preferred_element_type=F32)
        for b in range(D // LANES):
            sl = slice(b * LANES, (b + 1) * LANES)
            dog = dog_all[:, sl]
            sig = jax.nn.sigmoid(g_ref[:, sl])
            ov = o_ref[:, sl]
            do = (dog * sig * li_ref[:, sl]).astype(BF16)
            do_ref[:, sl] = do
            dg_ref[:, sl] = (dog * ov * sig * (1.0 - sig)).astype(BF16)
            t = do.astype(F32) * ov
            d0 = jnp.sum(jnp.where(lo, t, 0.0), axis=1, keepdims=True)
            d1 = jnp.sum(jnp.where(lo, 0.0, t), axis=1, keepdims=True)
            dl_ref[:, sl] = jnp.where(lo, d0, d1)

    row = pl.BlockSpec((tm, D), lambda i: (i, 0))
    return pl.pallas_call(
        body, name=name, grid=(Lp // tm,),
        in_specs=[row, pl.BlockSpec((D, D), lambda i: (0, 0)), row, pl.BlockSpec((tm, D), lambda i: (i, 3)), row],
        out_specs=[row, row, row],
        out_shape=[jax.ShapeDtypeStruct((Lp, D), BF16), jax.ShapeDtypeStruct((Lp, D), BF16),
                   jax.ShapeDtypeStruct((Lp, D), F32)],
        compiler_params=_params(("parallel",)),
    )(dhb, w_out, o, proj, linv)


def _fox_attn_bwd(qn, kn, vb, c, cT, do, mshift, delta, xchg, T, D, name):
    Lp = qn.shape[0]
    P = D // LANES
    nt = Lp // T
    H = cT.shape[0]
    nx = len(xchg)

    def body(q_ref, do_ref, m_ref, dl_ref, c_ref, k_ref, v_ref, cT_ref, *rest):
        x_in, (dq_ref, dk_ref, dv_ref, dc_ref), x_out, sems = rest[:nx], rest[nx:nx + 4], rest[nx + 4:2 * nx + 4], rest[2 * nx + 4:]
        p = pl.program_id(0)
        i = pl.program_id(1)

        @pl.when((p == 0) & (i == 0))
        def _():
            for cp in _xchg_copies(x_in, x_out, [True] * nx, sems):
                cp.start()

        @pl.when(i == 0)
        def _():
            dk_ref[...] = jnp.zeros_like(dk_ref)
            dv_ref[...] = jnp.zeros_like(dv_ref)
            dc_ref[...] = jnp.zeros_like(dc_ref)

        lo = _lane_lo()
        q = q_ref[...]
        do = do_ref[...]
        zero = jnp.zeros_like(q)
        qh = (jnp.where(lo, q, zero), jnp.where(lo, zero, q))
        doh = (jnp.where(lo, do, zero), jnp.where(lo, zero, do))
        msh = _split_halves(m_ref[...])
        dlt = _split_halves(dl_ref[...])
        cblk = c_ref[...]
        shift = tuple(_pick_col(cblk, 2 * p + h) - msh[h] for h in (0, 1))

        def step(j, carry, masked):
            k0 = pl.multiple_of(j * T, LANES)
            kj = k_ref[pl.ds(k0, T), :]
            vj = v_ref[pl.ds(k0, T), :]
            mask = _fox_mask(i, k0, T) if masked else None
            dqs, dks, dvs = [], [], []
            for h in (0, 1):
                ck = cT_ref[pl.ds(2 * p + h, 1), pl.ds(k0, T)]
                t = lax.dot_general(qh[h], kj, (((1,), (1,)), ((), ())), preferred_element_type=F32) - ck
                if masked:
                    t = jnp.where(mask, t, NEG)
                pb = jnp.exp(t + shift[h]).astype(BF16)
                dp = lax.dot_general(doh[h], vj, (((1,), (1,)), ((), ())), preferred_element_type=F32)
                ds = pb.astype(F32) * (dp - dlt[h])
                dsb = ds.astype(BF16)
                dqs.append(carry[h] + jnp.dot(dsb, kj, preferred_element_type=F32))
                dks.append(lax.dot_general(dsb, q, (((0,), (0,)), ((), ())), preferred_element_type=F32))
                dvs.append(lax.dot_general(pb, do, (((0,), (0,)), ((), ())), preferred_element_type=F32))
                dc_ref[0, h:h + 1, pl.ds(k0, T)] += -jnp.sum(ds, axis=0, keepdims=True)
            dk_ref[pl.ds(k0, T), :] += jnp.where(lo, dks[0], dks[1])
            dv_ref[pl.ds(k0, T), :] += jnp.where(lo, dvs[0], dvs[1])
            return tuple(dqs)

        init = (jnp.zeros((T, LANES), F32), jnp.zeros((T, LANES), F32))
        carry = lax.fori_loop(0, i, lambda j, cr: step(j, cr, False), init)
        dq0, dq1 = step(i, carry, True)
        dq_ref[...] = jnp.where(lo, dq0, dq1)

        @pl.when((p == P - 1) & (i == nt - 1))
        def _():
            for cp in _xchg_copies(x_in, x_out, [True] * nx, sems):
                cp.wait()

    tile = pl.BlockSpec((T, LANES), lambda p, i: (i, p))
    full = pl.BlockSpec((Lp, LANES), lambda p, i: (0, p))
    HBM = pl.BlockSpec(memory_space=pltpu.HBM)
    return pl.pallas_call(
        body, name=name, grid=(P, nt),
        in_specs=[tile, tile, tile, tile, pl.BlockSpec((T, LANES), lambda p, i: (i, 0)), full, full,
                  pl.BlockSpec((H, Lp), lambda p, i: (0, 0))] + [HBM] * nx,
        out_specs=[tile, full, full, pl.BlockSpec((1, 8, Lp), lambda p, i: (p, 0, 0))] + [HBM] * nx,
        out_shape=[jax.ShapeDtypeStruct((Lp, D), F32)] * 3 + [jax.ShapeDtypeStruct((P, 8, Lp), F32)]
        + _xchg_out_shapes(xchg, [True] * nx),
        scratch_shapes=_xchg_sems(nx),
        compiler_params=_params(("arbitrary", "arbitrary")),
    )(qn, do, mshift, delta, c, kn, vb, cT, *xchg)


def _scan_rows(x, reverse):
    C = x.shape[0]
    row = lax.broadcasted_iota(jnp.int32, (C, 1), 0)
    step = 1
    while step < C:
        if reverse:
            x = x + jnp.where(row < C - step, pltpu.roll(x, C - step, 0), 0.0)
        else:
            x = x + jnp.where(row >= step, pltpu.roll(x, step, 0), 0.0)
        step *= 2
    return x


@jax.custom_vjp
def _cumsum_rows(x):
    return _scan_rows(x, False)


_cumsum_rows.defvjp(lambda x: (_scan_rows(x, False), None), lambda _, g: (_scan_rows(g, True),))


def _hgrn_chunk(St, qr, z, vi, go, p0, p1, gg):
    C = qr.shape[0]
    lb = jax.nn.sigmoid(p1 - p0)
    a = jnp.log(lb)
    cc = jnp.log1p(-lb) + _log_sigmoid(z)
    log_f = jnp.maximum(a, cc) + jnp.log1p(jnp.exp(-jnp.abs(a - cc)))
    k = (1.0 - lb) * jax.nn.sigmoid(-z)
    q = qr * jax.nn.sigmoid(qr)
    row = lax.broadcasted_iota(jnp.int32, (C, C), 0)
    col = lax.broadcasted_iota(jnp.int32, (C, C), 1)
    causal = col <= row
    b = _cumsum_rows(log_f)
    mid = lax.broadcasted_iota(jnp.int32, (C, 1), 0) == C // 2 - 1
    r = jnp.sum(jnp.where(mid, b, 0.0), axis=0, keepdims=True)
    b_last = jnp.sum(log_f, axis=0, keepdims=True)
    attn = jnp.where(causal, _d_nt(q * jnp.exp(b - r), k * jnp.exp(r - b)), 0.0)
    o = _d_nn(attn, vi) + _d_nt(q * jnp.exp(b), St)
    St_new = St * jnp.exp(b_last) + _d_tn(vi, k * jnp.exp(b_last - b))
    og = _rms(o, gg) * (go * jax.nn.sigmoid(go))
    return St_new, og


def _hgrn_heads_per_step(H):
    return 8 if H % 8 == 0 else 4 if H % 4 == 0 else 1


def _hgrn_specs(T, W, nhb, rev_nt=None):
    if rev_nt is None:
        return [pl.BlockSpec((T, W), functools.partial(lambda hb, t, g: (t, g * nhb + hb), g=g)) for g in range(4)]
    return [pl.BlockSpec((T, W), functools.partial(lambda hb, t, g: (rev_nt - 1 - t, g * nhb + hb), g=g))
            for g in range(4)]


def _hgrn_fwd(proj, lbp, gg, T, name):
    Lp = proj.shape[0]
    D = proj.shape[1] // 4
    H = D // LANES
    hps = _hgrn_heads_per_step(H)
    W = hps * LANES
    nhb = H // hps
    nt = Lp // T
    ncc = T // HGRN_CHUNK

    def body(q_ref, z_ref, i_ref, go_ref, p_ref, gg_ref, og_ref, ss_ref, st_ref):
        @pl.when(pl.program_id(1) == 0)
        def _():
            st_ref[...] = jnp.zeros_like(st_ref)

        gain = gg_ref[...]

        def chunk(cidx, states):
            sl = pl.ds(pl.multiple_of(cidx * HGRN_CHUNK, HGRN_CHUNK), HGRN_CHUNK)
            new = []
            for hh in range(hps):
                ln = slice(hh * LANES, (hh + 1) * LANES)
                ss_ref[hh, cidx] = states[hh]
                St_new, og = _hgrn_chunk(states[hh], q_ref[sl, ln], z_ref[sl, ln], i_ref[sl, ln], go_ref[sl, ln],
                                         p_ref[0:1, ln], p_ref[1:2, ln], gain)
                og_ref[sl, ln] = og.astype(BF16)
                new.append(St_new)
            return tuple(new)

        states = lax.fori_loop(0, ncc, chunk, tuple(st_ref[hh] for hh in range(hps)))
        for hh in range(hps):
            st_ref[hh] = states[hh]

    return pl.pallas_call(
        body, name=name, grid=(nhb, nt),
        in_specs=_hgrn_specs(T, W, nhb) + [pl.BlockSpec((2, W), lambda hb, t: (0, hb)),
                                           pl.BlockSpec((1, LANES), lambda hb, t: (0, 0))],
        out_specs=[pl.BlockSpec((T, W), lambda hb, t: (t, hb)),
                   pl.BlockSpec((hps, ncc, LANES, LANES), lambda hb, t: (hb, t, 0, 0))],
        out_shape=[jax.ShapeDtypeStruct((Lp, D), BF16),
                   jax.ShapeDtypeStruct((H, Lp // HGRN_CHUNK, LANES, LANES), F32)],
        scratch_shapes=[pltpu.VMEM((hps, LANES, LANES), F32)],
        compiler_params=_params(("parallel", "arbitrary")),
    )(proj, proj, proj, proj, lbp, gg)


def _hgrn_bwd(proj, lbp, gg, dog, ss, T, name):
    Lp = proj.shape[0]
    D = proj.shape[1] // 4
    H = D // LANES
    hps = _hgrn_heads_per_step(H)
    W = hps * LANES
    nhb = H // hps
    assert nhb == 1, "d proj is written as whole rows: every head in one grid step"
    nt = Lp // T
    ncc = T // HGRN_CHUNK

    def body(q_ref, z_ref, i_ref, go_ref, p_ref, gg_ref, dog_ref, ss_ref, dproj_ref, dp_ref, dgg_ref, dst_ref):
        hb = pl.program_id(0)
        t = pl.program_id(1)

        @pl.when(t == 0)
        def _():
            dst_ref[...] = jnp.zeros_like(dst_ref)
            dp_ref[...] = jnp.zeros_like(dp_ref)

        @pl.when((t == 0) & (hb == 0))
        def _():
            dgg_ref[...] = jnp.zeros_like(dgg_ref)

        gain = gg_ref[...]
        row0 = (nt - 1 - t) * T

        def chunk(cc, carry):
            dstates, dps, dgain_sum = carry
            cidx = ncc - 1 - cc
            r0 = pl.multiple_of(cidx * HGRN_CHUNK, HGRN_CHUNK)
            sl = pl.ds(r0, HGRN_CHUNK)
            real = (row0 + r0 + lax.broadcasted_iota(jnp.int32, (HGRN_CHUNK, 1), 0)) >= N_PAD
            new_d, new_p = [], []
            for hh in range(hps):
                ln = slice(hh * LANES, (hh + 1) * LANES)
                _, vjp = jax.vjp(_hgrn_chunk, ss_ref[hh, cidx], q_ref[sl, ln], z_ref[sl, ln], i_ref[sl, ln],
                                 go_ref[sl, ln], p_ref[0:1, ln], p_ref[1:2, ln], gain)
                dSt, dq, dz, di, dgo, dp0, dp1, dgain = vjp((dstates[hh], dog_ref[sl, ln]))
                for grp, dval in enumerate((dq, dz, di, dgo)):
                    dproj_ref[sl, grp * D + hh * LANES:grp * D + (hh + 1) * LANES] = (
                        jnp.where(real, dval, 0.0).astype(BF16))
                new_d.append(dSt)
                new_p.append((dps[hh][0] + dp0, dps[hh][1] + dp1))
                dgain_sum = dgain_sum + dgain
            return tuple(new_d), tuple(new_p), dgain_sum

        zero_row = jnp.zeros((1, LANES), F32)
        init = (tuple(dst_ref[hh] for hh in range(hps)), tuple((zero_row, zero_row) for _ in range(hps)), zero_row)
        dstates, dps, dgain_sum = lax.fori_loop(0, ncc, chunk, init)
        for hh in range(hps):
            ln = slice(hh * LANES, (hh + 1) * LANES)
            dst_ref[hh] = dstates[hh]
            dp_ref[0:1, ln] += dps[hh][0]
            dp_ref[1:2, ln] += dps[hh][1]
        dgg_ref[0:1, :] += dgain_sum

    rev = pl.BlockSpec((T, W), lambda hb, t: (nt - 1 - t, hb))
    return pl.pallas_call(
        body, name=name, grid=(nhb, nt),
        in_specs=_hgrn_specs(T, W, nhb, nt) + [pl.BlockSpec((2, W), lambda hb, t: (0, hb)),
                                               pl.BlockSpec((1, LANES), lambda hb, t: (0, 0)), rev,
                                               pl.BlockSpec((hps, ncc, LANES, LANES),
                                                            lambda hb, t: (hb, nt - 1 - t, 0, 0))],
        out_specs=[pl.BlockSpec((T, 4 * D), lambda hb, t: (nt - 1 - t, 0)), pl.BlockSpec((8, W), lambda hb, t: (0, hb)),
                   pl.BlockSpec((8, LANES), lambda hb, t: (0, 0))],
        out_shape=[jax.ShapeDtypeStruct((Lp, 4 * D), BF16), jax.ShapeDtypeStruct((8, D), F32),
                   jax.ShapeDtypeStruct((8, LANES), F32)],
        scratch_shapes=[pltpu.VMEM((hps, LANES, LANES), F32)],
        compiler_params=_params(("arbitrary", "arbitrary")),
    )(proj, proj, proj, proj, lbp, gg, dog, ss)


def _xchg_sems(n_arr):
    return [pltpu.SemaphoreType.DMA((n_arr * (N_DEV - 1),)), pltpu.SemaphoreType.DMA((n_arr * (N_DEV - 1),)),
            pltpu.SemaphoreType.DMA((n_arr,))]


def _xchg_copies(ins, outs, per_peer, sems):
    send_sems, recv_sems, local_sems = sems
    x, y, c = lax.axis_index("x"), lax.axis_index("y"), lax.axis_index("c")
    me = 4 * x + 2 * y + c
    copies = []
    for n in range(len(ins)):
        src = ins[n].at[me] if per_peer[n] else ins[n]
        copies.append(pltpu.make_async_copy(src, outs[n].at[me], local_sems.at[n]))
    for rel in range(1, N_DEV):
        fx, fy, fc = (rel >> 2) & 1, (rel >> 1) & 1, rel & 1
        px = 1 - x if fx else x
        py = 1 - y if fy else y
        pc = 1 - c if fc else c
        peer = 4 * px + 2 * py + pc
        for n in range(len(ins)):
            src = ins[n].at[peer] if per_peer[n] else ins[n]
            copies.append(pltpu.make_async_remote_copy(
                src_ref=src, dst_ref=outs[n].at[me],
                send_sem=send_sems.at[n * (N_DEV - 1) + rel - 1],
                recv_sem=recv_sems.at[n * (N_DEV - 1) + rel - 1],
                device_id=(px, py, pc), device_id_type=pl.DeviceIdType.MESH))
    return copies


def _xchg_out_shapes(arrays, per_peer):
    return [jax.ShapeDtypeStruct(a.shape if pp else (N_DEV,) + a.shape, a.dtype) for a, pp in zip(arrays, per_peer)]


def _exchange(arrays, per_peer, name):
    n_arr = len(arrays)
    HBM = pl.BlockSpec(memory_space=pltpu.HBM)

    def body(*refs):
        copies = _xchg_copies(refs[:n_arr], refs[n_arr:2 * n_arr], per_peer, refs[2 * n_arr:])
        for cp in copies:
            cp.start()
        for cp in copies:
            cp.wait()

    return pl.pallas_call(
        body, name=name,
        in_specs=[HBM] * n_arr, out_specs=[HBM] * n_arr, out_shape=_xchg_out_shapes(arrays, per_peer),
        scratch_shapes=_xchg_sems(n_arr),
    )(*arrays)


ADAMW_VMEM_BUDGET = 36 * 1024 * 1024


def _adamw(recv, w, m, v, name):
    shape = w.shape
    C = shape[-1]
    R = math.prod(shape[:-1])
    lanes = -(-C // LANES) * LANES
    row_bytes = 2 * lanes * (N_DEV * recv.dtype.itemsize + 7 * 4)
    rc = _row_chunk(R, max(16, ADAMW_VMEM_BUDGET // row_bytes), 16 if recv.dtype == BF16 else 8)

    def body(r_ref, w_ref, m_ref, v_ref, g_ref, d_ref, mo_ref, vo_ref):
        g = r_ref[0].astype(F32)
        for s in range(1, N_DEV):
            g = g + r_ref[s].astype(F32)
        mn = ADAM_B1 * m_ref[...] + (1.0 - ADAM_B1) * g
        vn = ADAM_B2 * v_ref[...] + (1.0 - ADAM_B2) * (g * g)
        m_hat = mn / (1.0 - ADAM_B1 ** ADAM_STEP)
        v_hat = vn / (1.0 - ADAM_B2 ** ADAM_STEP)
        g_ref[...] = g
        d_ref[...] = -ADAM_LR * (m_hat / (jnp.sqrt(v_hat) + ADAM_EPS) + ADAM_WD * w_ref[...])
        mo_ref[...] = mn
        vo_ref[...] = vn

    row = pl.BlockSpec((rc, C), lambda i: (i, 0))
    outs = pl.pallas_call(
        body, name=name, grid=(R // rc,),
        in_specs=[pl.BlockSpec((N_DEV, rc, C), lambda i: (0, i, 0)), row, row, row],
        out_specs=[row] * 4,
        out_shape=[jax.ShapeDtypeStruct((R, C), F32)] * 4,
        compiler_params=_params(("parallel",)),
    )(recv.reshape(N_DEV, R, C), w.reshape(R, C), m.reshape(R, C), v.reshape(R, C))
    return [o.reshape(shape) for o in outs]


def _gathered_to_full(g, name):
    if name in COL_SHARDED:
        g = jnp.moveaxis(g, 0, -2)
        return g.reshape(g.shape[:-2] + (g.shape[-2] * g.shape[-1],))
    g = jnp.moveaxis(g, 0, -3)
    return g.reshape(g.shape[:-3] + (g.shape[-3] * g.shape[-2], g.shape[-1]))


def _full_to_slabs(full, name):
    if name in COL_SHARDED:
        f = full.reshape(full.shape[:-1] + (N_DEV, full.shape[-1] // N_DEV))
        return jnp.moveaxis(f, -2, 0)
    f = full.reshape(full.shape[:-2] + (N_DEV, full.shape[-2] // N_DEV, full.shape[-1]))
    return jnp.moveaxis(f, -3, 0)


def _pack_small(arrs):
    rows = []
    for a in arrs:
        flat = a.astype(F32).reshape(-1)
        pad = (-flat.shape[0]) % LANES
        rows.append(jnp.pad(flat, (0, pad)).reshape(-1, LANES))
    p = jnp.concatenate(rows, axis=0)
    return jnp.pad(p, ((0, (-p.shape[0]) % 8), (0, 0)))


def _unpack_small(packed, shapes):
    out, off = [], 0
    for shp in shapes:
        n = math.prod(shp)
        nr = -(-n // LANES)
        out.append(packed[off:off + nr].reshape(-1)[:n].reshape(shp))
        off += nr
    return out


def _local_step(x, target, meta, w_fox_in, w_fox_out, late, small):
    S, D = x.shape
    Lp = S + HEAD_ROWS
    T = ROW_TILE if Lp % ROW_TILE == 0 else HEAD_ROWS
    P = D // LANES
    row = lambda v: v.reshape(1, -1).astype(F32)

    w_fin = jnp.pad(w_fox_in[0], ((0, 0), (0, LANES - w_fox_in.shape[-1] % LANES)))
    w_fout = w_fox_out[0]
    n_heads = w_fox_in.shape[-1] - 4 * D
    bf = jnp.pad(row(small["fox_b_f"]), ((0, 0), (0, LANES - small["fox_b_f"].size)))
    qg = jnp.tile(row(small["fox_q_norm"]), (1, 2))
    kg = jnp.tile(row(small["fox_k_norm"]), (1, 2))

    h0 = jnp.concatenate([jnp.zeros((N_PAD, D), F32), meta, x], axis=0)

    hn0 = _rms_fwd(h0, row(small["attn_norm"][0]), T, "rms0_fwd")
    proj0 = _mm(hn0, w_fin, "nn", F32, "fox_in_fwd")
    qn, kn, vb, c = _fox_prep_fwd(proj0, bf, qg, kg, T, D, "fox_prep_fwd")
    cT = c.T[:2 * P].at[:, :N_PAD].set(PAD_SHIFT)
    o, og0, mshift, linv, *gathered = _fox_attn_fwd(qn, kn, vb, c, cT, proj0, [late[n] for n in LATE], T, D,
                                                    "fox_attn_fwd")
    wl = {n: _gathered_to_full(g, n) for n, g in zip(LATE, gathered)}
    w_hin, w_hout, w_uin, w_uout = wl["hgrn_w_in"][0], wl["hgrn_w_out"][0], wl["ffn_w_in"], wl["ffn_w_out"]
    h1, hf0 = _out_proj_fwd(og0, w_fout, h0, row(small["ffn_norm"][0]), "fox_out_fwd")
    gu0 = _ffn_in_fwd(hf0, w_uin[0], "ffn0_in_fwd")
    act0 = gu0[2]
    h2, hn1 = _out_proj_fwd(act0, w_uout[0], h1, row(small["attn_norm"][1]), "ffn0_out_fwd")
    proj1 = _mm(hn1, w_hin, "nn", F32, "hgrn_in_fwd")
    lbp = small["hgrn_lower_bounds"].astype(F32)
    ggn = row(small["hgrn_g_norm"])
    Th = HGRN_TILE if Lp % HGRN_TILE == 0 else HEAD_ROWS
    og1, ss = _hgrn_fwd(proj1, lbp, ggn, Th, "hgrn_fwd")
    h3, hf1 = _out_proj_fwd(og1, w_hout, h2, row(small["ffn_norm"][1]), "hgrn_out_fwd")
    gu1 = _ffn_in_fwd(hf1, w_uin[1], "ffn1_in_fwd")
    act1 = gu1[2]
    h4 = _mm(act1, w_uout[1], "nn", F32, "ffn1_out_fwd", res=h3)
    loss_blk, dh4, dh4b, d_final = _final_loss(h4, row(small["final_norm"]), target, "final_loss")

    grads = {}

    def ffn_bwd(i, dh, dhb, h_in, hf, gu, act, tag):
        grads_out = _mm(act, dhb, "tn", F32, f"ffn{i}_out_dw")
        dgu = _ffn_out_dx(dhb, w_uout[i], gu[0], gu[1], f"ffn{i}_out_dx")
        grads_in = _mm(hf, dgu, "tn", F32, f"ffn{i}_in_dw")
        dh_new, dh_newb, dgain = _in_proj_dx(dgu, w_uin[i], h_in, row(small["ffn_norm"][i]), dh, f"ffn{i}_in_dx")
        return dh_new, dh_newb, grads_in, grads_out, dgain

    dh3, dh3b, g_uin1, g_uout1, d_fn1 = ffn_bwd(1, dh4, dh4b, h3, hf1, gu1, act1, "1")
    grads["hgrn_w_out"] = _mm(og1, dh3b, "tn", F32, "hgrn_out_dw")[None]
    dog1 = _mm(dh3b, w_hout, "nt", F32, "hgrn_out_dx")
    dproj1, d_lb, d_gg = _hgrn_bwd(proj1, lbp, ggn, dog1, ss, Th, "hgrn_bwd")
    grads["hgrn_w_in"] = _mm(hn1, dproj1, "tn", F32, "hgrn_in_dw")[None]
    dh2, dh2b, d_an1 = _in_proj_dx(dproj1, w_hin, h2, row(small["attn_norm"][1]), dh3, "hgrn_in_dx")
    dh1, dh1b, g_uin0, g_uout0, d_fn0 = ffn_bwd(0, dh2, dh2b, h1, hf0, gu0, act0, "0")
    grads["ffn_w_in"] = jnp.stack([g_uin0, g_uin1])
    grads["ffn_w_out"] = jnp.stack([g_uout0, g_uout1])
    grads["fox_w_out"] = _mm(og0, dh1b, "tn", F32, "fox_out_dw")[None]
    do, dgate, delta = _fox_out_dx(dh1b, w_fout, o, proj0, linv, D, "fox_out_dx")
    slabs = [_full_to_slabs(grads[n], n).astype(BF16) for n in LATE]
    dqn, dkn, dv, dcr, *recv = _fox_attn_bwd(qn, kn, vb, c, cT, do, mshift, delta, slabs, T, D, "fox_attn_bwd")
    for n in LATE:
        del grads[n]
    dc = jnp.pad(dcr[:, :2, :].reshape(2 * P, Lp).T, ((0, 0), (0, LANES - 2 * P)))
    Tp = T // 2 if T == ROW_TILE else T
    dproj0, sm = _fox_prep_bwd(proj0, bf, qg, kg, dqn, dkn, dv, dgate, dc, Tp, D, "fox_prep_bwd")
    grads["fox_w_in"] = _mm(hn0, dproj0, "tn", F32, "fox_in_dw")[:, :4 * D + n_heads][None]
    dh0, _, d_an0 = _in_proj_dx(dproj0, w_fin, h0, row(small["attn_norm"][0]), dh1, "fox_in_dx")

    grads["meta_tokens"] = dh0[N_PAD:HEAD_ROWS]
    grads["attn_norm"] = jnp.concatenate([d_an0, d_an1], axis=0)
    grads["ffn_norm"] = jnp.concatenate([d_fn0, d_fn1], axis=0)
    grads["final_norm"] = d_final[0]
    grads["fox_b_f"] = sm[0:1, :n_heads]
    grads["fox_q_norm"] = sm[1:2, :FOX_DH] + sm[1:2, FOX_DH:]
    grads["fox_k_norm"] = sm[2:3, :FOX_DH] + sm[2:3, FOX_DH:]
    grads["hgrn_lower_bounds"] = d_lb[0:2]
    grads["hgrn_g_norm"] = d_gg[0:1]
    return loss_blk[0, 0], dh0[HEAD_ROWS:], grads, dict(zip(LATE, recv))


def kernel(x, meta_tokens, attn_norm, ffn_norm, final_norm, fox_w_in, fox_b_f, fox_q_norm, fox_k_norm, fox_w_out, hgrn_w_in, hgrn_lower_bounds, hgrn_g_norm, hgrn_w_out, ffn_w_in, ffn_w_out, loss_target, m_meta_tokens, m_attn_norm, m_ffn_norm, m_final_norm, m_fox_w_in, m_fox_b_f, m_fox_q_norm, m_fox_k_norm, m_fox_w_out, m_hgrn_w_in, m_hgrn_lower_bounds, m_hgrn_g_norm, m_hgrn_w_out, m_ffn_w_in, m_ffn_w_out, v_meta_tokens, v_attn_norm, v_ffn_norm, v_final_norm, v_fox_w_in, v_fox_b_f, v_fox_q_norm, v_fox_k_norm, v_fox_w_out, v_hgrn_w_in, v_hgrn_lower_bounds, v_hgrn_g_norm, v_hgrn_w_out, v_ffn_w_in, v_ffn_w_out):
    w = dict(meta_tokens=meta_tokens, attn_norm=attn_norm, ffn_norm=ffn_norm, final_norm=final_norm,
             fox_w_in=fox_w_in, fox_b_f=fox_b_f, fox_q_norm=fox_q_norm, fox_k_norm=fox_k_norm,
             fox_w_out=fox_w_out, hgrn_w_in=hgrn_w_in, hgrn_lower_bounds=hgrn_lower_bounds,
             hgrn_g_norm=hgrn_g_norm, hgrn_w_out=hgrn_w_out, ffn_w_in=ffn_w_in, ffn_w_out=ffn_w_out)
    m = dict(meta_tokens=m_meta_tokens, attn_norm=m_attn_norm, ffn_norm=m_ffn_norm, final_norm=m_final_norm,
             fox_w_in=m_fox_w_in, fox_b_f=m_fox_b_f, fox_q_norm=m_fox_q_norm, fox_k_norm=m_fox_k_norm,
             fox_w_out=m_fox_w_out, hgrn_w_in=m_hgrn_w_in, hgrn_lower_bounds=m_hgrn_lower_bounds,
             hgrn_g_norm=m_hgrn_g_norm, hgrn_w_out=m_hgrn_w_out, ffn_w_in=m_ffn_w_in, ffn_w_out=m_ffn_w_out)
    v = dict(meta_tokens=v_meta_tokens, attn_norm=v_attn_norm, ffn_norm=v_ffn_norm, final_norm=v_final_norm,
             fox_w_in=v_fox_w_in, fox_b_f=v_fox_b_f, fox_q_norm=v_fox_q_norm, fox_k_norm=v_fox_k_norm,
             fox_w_out=v_fox_w_out, hgrn_w_in=v_hgrn_w_in, hgrn_lower_bounds=v_hgrn_lower_bounds,
             hgrn_g_norm=v_hgrn_g_norm, hgrn_w_out=v_hgrn_w_out, ffn_w_in=v_ffn_w_in, ffn_w_out=v_ffn_w_out)
    axes = ("x", "y", "c")
    small_shapes = [w[n].shape for n in SMALL]

    g_meta, g_fin, g_fout = _exchange([w["meta_tokens"].astype(F32), w["fox_w_in"].astype(BF16),
                                       w["fox_w_out"].astype(BF16)], [False] * 3, "gather_weights")
    loss_local, grad_x, grads, recv = _local_step(
        x[0], loss_target[0], _gathered_to_full(g_meta, "meta_tokens"), _gathered_to_full(g_fin, "fox_w_in"),
        _gathered_to_full(g_fout, "fox_w_out"), {n: w[n].astype(BF16) for n in LATE}, {n: w[n] for n in SMALL})
    loss = lax.psum(loss_local, axes)

    r_meta, r_fin, r_fout, r_small = _exchange(
        [_full_to_slabs(grads["meta_tokens"], "meta_tokens"), _full_to_slabs(grads["fox_w_in"], "fox_w_in").astype(BF16),
         _full_to_slabs(grads["fox_w_out"], "fox_w_out").astype(BF16), _pack_small([grads[n] for n in SMALL])],
        [True, True, True, False], "scatter_grads")
    recv.update(meta_tokens=r_meta, fox_w_in=r_fin, fox_w_out=r_fout)

    res = {n: _adamw(recv[n], w[n], m[n], v[n], "adamw_" + n) for n in BIG}
    sml = _adamw(r_small, _pack_small([w[n] for n in SMALL]), _pack_small([m[n] for n in SMALL]),
                 _pack_small([v[n] for n in SMALL]), "adamw_small")
    outs = []
    for k in range(4):
        d = {n: res[n][k] for n in BIG}
        d.update(zip(SMALL, _unpack_small(sml[k], small_shapes)))
        outs.extend(d[n] for n in WEIGHTS)
    return (loss, grad_x[None], *outs)
```

```python
import functools
import math

import jax
import jax.numpy as jnp
from jax import lax
from jax.experimental import pallas as pl
from jax.experimental.pallas import tpu as pltpu

F32 = jnp.float32
BF16 = jnp.bfloat16
EPS = 1e-6
N_META = 16
LANES = 128
HEAD_ROWS = 256
ROW_TILE = 768
N_PAD = HEAD_ROWS - N_META
FOX_DH = 64
HGRN_CHUNK = 64
HGRN_TILE = 384
N_DEV = 8
NEG = -1e30
PAD_SHIFT = 1e4
VMEM_LIMIT = 56 * 1024 * 1024
HI = lax.Precision.HIGHEST

ADAM_LR = 0.001
ADAM_B1 = 0.9
ADAM_B2 = 0.999
ADAM_EPS = 1e-08
ADAM_WD = 0.01
ADAM_STEP = 10

BIG = ("meta_tokens", "fox_w_in", "fox_w_out", "hgrn_w_in", "hgrn_w_out", "ffn_w_in", "ffn_w_out")
SMALL = ("attn_norm", "ffn_norm", "final_norm", "fox_b_f", "fox_q_norm", "fox_k_norm",
         "hgrn_lower_bounds", "hgrn_g_norm")
WEIGHTS = ("meta_tokens", "attn_norm", "ffn_norm", "final_norm", "fox_w_in", "fox_b_f", "fox_q_norm",
           "fox_k_norm", "fox_w_out", "hgrn_w_in", "hgrn_lower_bounds", "hgrn_g_norm", "hgrn_w_out",
           "ffn_w_in", "ffn_w_out")
COL_SHARDED = ("meta_tokens", "fox_w_in", "hgrn_w_in", "ffn_w_in")
LATE = ("fox_w_out", "hgrn_w_in", "hgrn_w_out", "ffn_w_in", "ffn_w_out")


def _params(sem=None):
    return pltpu.CompilerParams(dimension_semantics=sem, vmem_limit_bytes=VMEM_LIMIT)


def _tile(n, cap):
    best = None
    for t in range(LANES, min(n, cap) + 1, LANES):
        if n % t == 0:
            best = t
    assert best is not None, (n, cap)
    return best


def _row_chunk(n, cap, mult=8):
    best = n
    for t in range(mult, min(n, cap) + 1, mult):
        if n % t == 0:
            best = t
    return best


def _dg(a, b, ca, cb):
    return lax.dot_general(a.astype(BF16), b.astype(BF16), (((ca,), (cb,)), ((), ())),
                           preferred_element_type=F32)


@jax.custom_vjp
def _d_nn(a, b):
    return _dg(a, b, 1, 0)


@jax.custom_vjp
def _d_nt(a, b):
    return _dg(a, b, 1, 1)


@jax.custom_vjp
def _d_tn(a, b):
    return _dg(a, b, 0, 0)


_d_nn.defvjp(lambda a, b: (_d_nn(a, b), (a, b)), lambda r, g: (_d_nt(g, r[1]), _d_tn(r[0], g)))
_d_nt.defvjp(lambda a, b: (_d_nt(a, b), (a, b)), lambda r, g: (_d_nn(g, r[1]), _d_tn(g, r[0])))
_d_tn.defvjp(lambda a, b: (_d_tn(a, b), (a, b)), lambda r, g: (_d_nt(r[1], g), _d_nn(r[0], g)))


def _log_sigmoid(x):
    return jnp.minimum(x, 0.0) - jnp.log1p(jnp.exp(-jnp.abs(x)))


def _rms(x, g):
    return x * lax.rsqrt(jnp.mean(x * x, axis=-1, keepdims=True) + EPS) * g


def _mm(a, b, mode, out_dtype, name, res=None, tm=None, tn=None, tk=None):
    assert a.dtype == BF16 and b.dtype == BF16, (name, a.dtype, b.dtype)
    if mode == "nn":
        (M, K), N = a.shape, b.shape[1]
    elif mode == "nt":
        (M, K), N = a.shape, b.shape[0]
    else:
        (K, M), N = a.shape, b.shape[1]
    if mode == "nn":
        tm, tn, tk = tm or _tile(M, ROW_TILE), tn or _tile(N, 1408), tk or _tile(K, 2816)
    elif mode == "nt":
        tm, tn, tk = tm or _tile(M, ROW_TILE if K <= 2048 else ROW_TILE // 2), tn or N, tk or K
    else:
        tm, tn, tk = tm or _tile(M, 1408), tn or _tile(N, 1408), tk or _tile(K, ROW_TILE)
    nk = K // tk
    if mode == "tn":
        a_spec = pl.BlockSpec((tk, tm), lambda j, i, k: (k, i))
        dims = (((0,), (0,)), ((), ()))
    else:
        a_spec = pl.BlockSpec((tm, tk), lambda j, i, k: (i, k))
        dims = (((1,), (1 if mode == "nt" else 0,)), ((), ()))
    if mode == "nt":
        b_spec = pl.BlockSpec((tn, tk), lambda j, i, k: (j, k))
    else:
        b_spec = pl.BlockSpec((tk, tn), lambda j, i, k: (k, j))

    o_spec = pl.BlockSpec((tm, tn), lambda j, i, k: (i, j))

    def body(a_ref, b_ref, *rest):
        r_ref = rest[0] if res is not None else None
        o_ref, acc_ref = rest[-2:]
        k = pl.program_id(2)

        @pl.when(k == 0)
        def _():
            acc_ref[...] = jnp.zeros_like(acc_ref)

        acc_ref[...] += lax.dot_general(a_ref[...], b_ref[...], dims, preferred_element_type=F32)

        @pl.when(k == nk - 1)
        def _():
            out = acc_ref[...] if r_ref is None else acc_ref[...] + r_ref[...]
            o_ref[...] = out.astype(out_dtype)

    return pl.pallas_call(
        body, name=name, grid=(N // tn, M // tm, nk),
        in_specs=[a_spec, b_spec] + ([o_spec] if res is not None else []),
        out_specs=o_spec,
        out_shape=jax.ShapeDtypeStruct((M, N), out_dtype),
        scratch_shapes=[pltpu.VMEM((tm, tn), F32)],
        compiler_params=_params(("parallel", "parallel", "arbitrary")),
    )(a, b, *([res] if res is not None else []))


def _out_proj_fwd(a, w, res, gain, name):
    Lp, K = a.shape
    D = w.shape[1]
    tm = _tile(Lp, ROW_TILE)

    def body(a_ref, w_ref, r_ref, g_ref, h_ref, hn_ref):
        h = jnp.dot(a_ref[...], w_ref[...], preferred_element_type=F32) + r_ref[...]
        h_ref[...] = h
        hn_ref[...] = _rms(h, g_ref[...]).astype(BF16)

    row = pl.BlockSpec((tm, D), lambda i: (i, 0))
    return pl.pallas_call(
        body, name=name, grid=(Lp // tm,),
        in_specs=[pl.BlockSpec((tm, K), lambda i: (i, 0)), pl.BlockSpec((K, D), lambda i: (0, 0)), row,
                  pl.BlockSpec((1, D), lambda i: (0, 0))],
        out_specs=[row, row],
        out_shape=[jax.ShapeDtypeStruct((Lp, D), F32), jax.ShapeDtypeStruct((Lp, D), BF16)],
        compiler_params=_params(("parallel",)),
    )(a, w, res, gain)


def _in_proj_dx(dy, w, x, gain, dres, name, xchg=()):
    Lp, N = dy.shape
    D = w.shape[0]
    tm = _tile(Lp, ROW_TILE // 2)
    nt = Lp // tm
    nx = len(xchg)

    def body(dy_ref, w_ref, x_ref, g_ref, dr_ref, *rest):
        x_in, (dx_ref, dxb_ref, dg_ref), x_out, sems = rest[:nx], rest[nx:nx + 3], rest[nx + 3:2 * nx + 3], rest[2 * nx + 3:]

        @pl.when(pl.program_id(0) == 0)
        def _():
            dg_ref[...] = jnp.zeros_like(dg_ref)
            if nx:
                for cp in _xchg_copies(x_in, x_out, [True] * nx, sems):
                    cp.start()

        dhn = lax.dot_general(dy_ref[...], w_ref[...], (((1,), (1,)), ((), ())), preferred_element_type=F32)
        _, vjp = jax.vjp(_rms, x_ref[...], g_ref[...])
        dx, dg = vjp(dhn)
        dx = dx + dr_ref[...]
        dx_ref[...] = dx
        dxb_ref[...] = dx.astype(BF16)
        dg_ref[...] += dg

        if nx:
            @pl.when(pl.program_id(0) == nt - 1)
            def _():
                for cp in _xchg_copies(x_in, x_out, [True] * nx, sems):
                    cp.wait()

    row = pl.BlockSpec((tm, D), lambda i: (i, 0))
    vec = pl.BlockSpec((1, D), lambda i: (0, 0))
    HBM = pl.BlockSpec(memory_space=pltpu.HBM)
    return pl.pallas_call(
        body, name=name, grid=(nt,),
        in_specs=[pl.BlockSpec((tm, N), lambda i: (i, 0)), pl.BlockSpec((D, N), lambda i: (0, 0)), row, vec, row]
        + [HBM] * nx,
        out_specs=[row, row, vec] + [HBM] * nx,
        out_shape=[jax.ShapeDtypeStruct((Lp, D), F32), jax.ShapeDtypeStruct((Lp, D), BF16),
                   jax.ShapeDtypeStruct((1, D), F32)] + _xchg_out_shapes(xchg, [True] * nx),
        scratch_shapes=_xchg_sems(nx) if nx else [],
        compiler_params=_params(("arbitrary",)),
    )(dy, w, x, gain, dres, *xchg)


def _rms_fwd(x, g, T, name):
    Lp, D = x.shape

    def body(x_ref, g_ref, o_ref):
        o_ref[...] = _rms(x_ref[...], g_ref[...]).astype(BF16)

    return pl.pallas_call(
        body, name=name, grid=(Lp // T,),
        in_specs=[pl.BlockSpec((T, D), lambda i: (i, 0)), pl.BlockSpec((1, D), lambda i: (0, 0))],
        out_specs=pl.BlockSpec((T, D), lambda i: (i, 0)),
        out_shape=jax.ShapeDtypeStruct((Lp, D), BF16),
        compiler_params=_params(("parallel",)),
    )(x, g)


def _swiglu(gate, up):
    return gate * jax.nn.sigmoid(gate) * up


def _ffn_in_fwd(hf, w_in, name):
    Lp, D = hf.shape
    F = w_in.shape[1] // 2
    tm = _tile(Lp, ROW_TILE)
    tn = _tile(F, 1408)
    nj = F // tn

    def body(a_ref, bg_ref, bu_ref, g_ref, u_ref, act_ref):
        a = a_ref[...]
        g = jnp.dot(a, bg_ref[...], preferred_element_type=F32)
        u = jnp.dot(a, bu_ref[...], preferred_element_type=F32)
        g_ref[...] = g.astype(BF16)
        u_ref[...] = u.astype(BF16)
        act_ref[...] = _swiglu(g, u).astype(BF16)

    tile = pl.BlockSpec((tm, tn), lambda j, i: (i, j))
    return pl.pallas_call(
        body, name=name, grid=(nj, Lp // tm),
        in_specs=[pl.BlockSpec((tm, D), lambda j, i: (i, 0)), pl.BlockSpec((D, tn), lambda j, i: (0, j)),
                  pl.BlockSpec((D, tn), lambda j, i: (0, nj + j))],
        out_specs=[tile, tile, tile],
        out_shape=[jax.ShapeDtypeStruct((Lp, F), BF16)] * 3,
        compiler_params=_params(("parallel", "parallel")),
    )(hf, w_in, w_in)


def _ffn_out_dx(dhb, w_out, g, u, name):
    Lp, D = dhb.shape
    F = w_out.shape[0]
    tm = HEAD_ROWS

    def body(a_ref, b_ref, g_ref, u_ref, o_ref):
        dact = lax.dot_general(a_ref[...], b_ref[...], (((1,), (1,)), ((), ())), preferred_element_type=F32)
        _, vjp = jax.vjp(_swiglu, g_ref[...].astype(F32), u_ref[...].astype(F32))
        dg, du = vjp(dact)
        o_ref[:, :F] = dg.astype(BF16)
        o_ref[:, F:] = du.astype(BF16)

    wide = pl.BlockSpec((tm, F), lambda i: (i, 0))
    return pl.pallas_call(
        body, name=name, grid=(Lp // tm,),
        in_specs=[pl.BlockSpec((tm, D), lambda i: (i, 0)), pl.BlockSpec((F, D), lambda i: (0, 0)), wide, wide],
        out_specs=pl.BlockSpec((tm, 2 * F), lambda i: (i, 0)),
        out_shape=jax.ShapeDtypeStruct((Lp, 2 * F), BF16),
        compiler_params=_params(("parallel",)),
    )(dhb, w_out, g, u)


def _final_loss(h, g, target, name):
    Lp, D = h.shape
    TR = HEAD_ROWS

    def loss_fn(hh, gg, tt):
        err = _rms(hh, gg) - tt
        return 0.5 * jnp.sum(jnp.mean(err * err, axis=-1))

    def body(h_ref, g_ref, t_ref, loss_ref, dh_ref, dhb_ref, dg_ref):
        i = pl.program_id(0)

        @pl.when(i == 0)
        def _():
            loss_ref[...] = jnp.zeros_like(loss_ref)
            dg_ref[...] = jnp.zeros_like(dg_ref)
            dh_ref[...] = jnp.zeros_like(dh_ref)
            dhb_ref[...] = jnp.zeros_like(dhb_ref)

        @pl.when(i > 0)
        def _():
            val, vjp = jax.vjp(lambda hh, gg: loss_fn(hh, gg, t_ref[...]), h_ref[...], g_ref[...])
            dh, dg = vjp(jnp.ones((), F32))
            dh_ref[...] = dh
            dhb_ref[...] = dh.astype(BF16)
            dg_ref[...] += dg
            loss_ref[...] += val

    row = pl.BlockSpec((TR, D), lambda i: (i, 0))
    return pl.pallas_call(
        body, name=name, grid=(Lp // TR,),
        in_specs=[row, pl.BlockSpec((1, D), lambda i: (0, 0)),
                  pl.BlockSpec((TR, D), lambda i: (jnp.maximum(i - 1, 0), 0))],
        out_specs=[pl.BlockSpec((8, LANES), lambda i: (0, 0)), row, row, pl.BlockSpec((1, D), lambda i: (0, 0))],
        out_shape=[jax.ShapeDtypeStruct((8, LANES), F32), jax.ShapeDtypeStruct((Lp, D), F32),
                   jax.ShapeDtypeStruct((Lp, D), BF16), jax.ShapeDtypeStruct((1, D), F32)],
        compiler_params=_params(("arbitrary",)),
    )(h, g, target)


def _lane_lo():
    return lax.broadcasted_iota(jnp.int32, (1, LANES), 1) < FOX_DH


def _headnorm(x, g, scale):
    lo = _lane_lo()
    x2 = x * x
    s0 = jnp.sum(jnp.where(lo, x2, 0.0), axis=-1, keepdims=True)
    s1 = jnp.sum(jnp.where(lo, 0.0, x2), axis=-1, keepdims=True)
    r = jnp.where(lo, lax.rsqrt(s0 / FOX_DH + EPS), lax.rsqrt(s1 / FOX_DH + EPS))
    return x * r * g * scale


def _fox_prep_fwd(proj, bf, qg, kg, T, D, name):
    Lp = proj.shape[0]
    nb = D // LANES
    scale = FOX_DH ** -0.5

    def body(q_ref, k_ref, v_ref, fl_ref, bf_ref, qg_ref, kg_ref, qn_ref, kn_ref, vb_ref, c_ref, carry_ref):
        @pl.when(pl.program_id(0) == 0)
        def _():
            carry_ref[...] = jnp.zeros_like(carry_ref)

        for b in range(nb):
            sl = slice(b * LANES, (b + 1) * LANES)
            qn_ref[:, sl] = _headnorm(q_ref[:, sl], qg_ref[...], scale).astype(BF16)
            kn_ref[:, sl] = _headnorm(k_ref[:, sl], kg_ref[...], 1.0).astype(BF16)
        vb_ref[...] = v_ref[...].astype(BF16)
        log_f = _log_sigmoid(fl_ref[...] + bf_ref[...])
        row = lax.broadcasted_iota(jnp.int32, (T, T), 0)
        col = lax.broadcasted_iota(jnp.int32, (T, T), 1)
        tri = (col <= row).astype(F32)
        c = jnp.dot(tri, log_f, precision=HI, preferred_element_type=F32) + carry_ref[...]
        c_ref[...] = c
        last = lax.broadcasted_iota(jnp.int32, (T, 1), 0) == T - 1
        carry_ref[...] = jnp.sum(jnp.where(last, c, 0.0), axis=0, keepdims=True)

    wide = lambda j: pl.BlockSpec((T, D), lambda i: (i, j))
    vec = pl.BlockSpec((1, LANES), lambda i: (0, 0))
    return pl.pallas_call(
        body, name=name, grid=(Lp // T,),
        in_specs=[wide(0), wide(1), wide(2), pl.BlockSpec((T, LANES), lambda i: (i, 4 * nb)), vec, vec, vec],
        out_specs=[wide(0), wide(0), wide(0), pl.BlockSpec((T, LANES), lambda i: (i, 0))],
        out_shape=[jax.ShapeDtypeStruct((Lp, D), BF16)] * 3 + [jax.ShapeDtypeStruct((Lp, LANES), F32)],
        scratch_shapes=[pltpu.VMEM((1, LANES), F32)],
        compiler_params=_params(("arbitrary",)),
    )(proj, proj, proj, proj, bf, qg, kg)


def _fox_prep_bwd(proj, bf, qg, kg, dqn, dkn, dv, dgate, dc, T, D, name):
    Lp = proj.shape[0]
    nb = D // LANES
    nt = Lp // T
    scale = FOX_DH ** -0.5

    def body(q_ref, k_ref, fl_ref, bf_ref, qg_ref, kg_ref, dqn_ref, dkn_ref, dv_ref, dgate_ref, dc_ref,
             dproj_ref, sm_ref, carry_ref):
        @pl.when(pl.program_id(0) == 0)
        def _():
            carry_ref[...] = jnp.zeros_like(carry_ref)
            sm_ref[...] = jnp.zeros_like(sm_ref)

        dqg = jnp.zeros((1, LANES), F32)
        dkg = jnp.zeros((1, LANES), F32)
        for b in range(nb):
            sl = slice(b * LANES, (b + 1) * LANES)
            _, vjp = jax.vjp(lambda x, g: _headnorm(x, g, scale), q_ref[:, sl], qg_ref[...])
            dx, dg = vjp(dqn_ref[:, sl])
            dproj_ref[:, sl] = dx.astype(BF16)
            dqg = dqg + dg
            _, vjp = jax.vjp(lambda x, g: _headnorm(x, g, 1.0), k_ref[:, sl], kg_ref[...])
            dx, dg = vjp(dkn_ref[:, sl])
            dproj_ref[:, D + b * LANES:D + (b + 1) * LANES] = dx.astype(BF16)
            dkg = dkg + dg
        dproj_ref[:, 2 * D:3 * D] = dv_ref[...].astype(BF16)
        dproj_ref[:, 3 * D:4 * D] = dgate_ref[...]
        dcv = dc_ref[...]
        row = lax.broadcasted_iota(jnp.int32, (T, T), 0)
        col = lax.broadcasted_iota(jnp.int32, (T, T), 1)
        triu = (col >= row).astype(F32)
        dlogf = jnp.dot(triu, dcv, precision=HI, preferred_element_type=F32) + carry_ref[...]
        carry_ref[...] += jnp.sum(dcv, axis=0, keepdims=True)
        _, vjp = jax.vjp(_log_sigmoid, fl_ref[...] + bf_ref[...])
        (dfl,) = vjp(dlogf)
        dproj_ref[:, 4 * D:] = dfl.astype(BF16)
        sm_ref[0:1, :] += jnp.sum(dfl, axis=0, keepdims=True)
        sm_ref[1:2, :] += dqg
        sm_ref[2:3, :] += dkg

    wide = lambda j: pl.BlockSpec((T, D), lambda i: (nt - 1 - i, j))
    narrow = lambda j: pl.BlockSpec((T, LANES), lambda i: (nt - 1 - i, j))
    vec = pl.BlockSpec((1, LANES), lambda i: (0, 0))
    return pl.pallas_call(
        body, name=name, grid=(nt,),
        in_specs=[wide(0), wide(1), narrow(4 * nb), vec, vec, vec, wide(0), wide(0), wide(0), wide(0), narrow(0)],
        out_specs=[pl.BlockSpec((T, 4 * D + LANES), lambda i: (nt - 1 - i, 0)), pl.BlockSpec((8, LANES), lambda i: (0, 0))],
        out_shape=[jax.ShapeDtypeStruct((Lp, 4 * D + LANES), BF16), jax.ShapeDtypeStruct((8, LANES), F32)],
        scratch_shapes=[pltpu.VMEM((1, LANES), F32)],
        compiler_params=_params(("arbitrary",)),
    )(proj, proj, proj, bf, qg, kg, dqn, dkn, dv, dgate, dc)


def _ln2_ceil(m):
    return jnp.ceil(m * (1.0 / math.log(2.0))) * math.log(2.0)


def _fox_mask(i, k0, T):
    qpos = i * T + lax.broadcasted_iota(jnp.int32, (T, 1), 0)
    kpos = k0 + lax.broadcasted_iota(jnp.int32, (1, T), 1)
    return (kpos <= qpos) & ((kpos >= N_PAD) | (qpos < N_PAD))


def _pick_col(blk, idx):
    lane = lax.broadcasted_iota(jnp.int32, (1, LANES), 1)
    return jnp.sum(jnp.where(lane == idx, blk, 0.0), axis=1, keepdims=True)


def _split_halves(blk):
    lo = _lane_lo()
    return (jnp.max(jnp.where(lo, blk, -jnp.inf), axis=1, keepdims=True),
            jnp.max(jnp.where(lo, -jnp.inf, blk), axis=1, keepdims=True))


def _fox_attn_fwd(qn, kn, vb, c, cT, proj, xchg, T, D, name):
    Lp = qn.shape[0]
    P = D // LANES
    nt = Lp // T
    H = cT.shape[0]
    nx = len(xchg)

    def body(q_ref, k_ref, v_ref, c_ref, cT_ref, g_ref, *rest):
        x_in, (o_ref, og_ref, m_ref, li_ref), x_out, sems = rest[:nx], rest[nx:nx + 4], rest[nx + 4:2 * nx + 4], rest[2 * nx + 4:]
        p = pl.program_id(0)
        i = pl.program_id(1)

        @pl.when((p == 0) & (i == 0))
        def _():
            for cp in _xchg_copies(x_in, x_out, [False] * nx, sems):
                cp.start()

        lo = _lane_lo()
        q = q_ref[...]
        zero = jnp.zeros_like(q)
        qh = (jnp.where(lo, q, zero), jnp.where(lo, zero, q))
        cblk = c_ref[...]
        cq = tuple(_pick_col(cblk, 2 * p + h) for h in (0, 1))
        one = jnp.ones_like(q)

        def step(j, carry, masked):
            k0 = pl.multiple_of(j * T, LANES)
            kj = k_ref[pl.ds(k0, T), :]
            vj = v_ref[pl.ds(k0, T), :]
            vh = (jnp.where(lo, vj, one), jnp.where(lo, one, vj))
            mask = _fox_mask(i, k0, T) if masked else None
            out = []
            for h in (0, 1):
                m, acc = carry[h]
                ck = cT_ref[pl.ds(2 * p + h, 1), pl.ds(k0, T)]
                t = lax.dot_general(qh[h], kj, (((1,), (1,)), ((), ())), preferred_element_type=F32) - ck
                if masked:
                    t = jnp.where(mask, t, NEG)
                m_new = _ln2_ceil(jnp.maximum(m, cq[h] + jnp.max(t, axis=1, keepdims=True)))
                pr = jnp.exp(t + (cq[h] - m_new)).astype(BF16)
                acc = jnp.exp(m - m_new) * acc + jnp.dot(pr, vh[h], preferred_element_type=F32)
                out.append((m_new, acc))
            return tuple(out)

        init = tuple((jnp.full((T, 1), NEG, F32), jnp.zeros((T, LANES), F32)) for _ in (0, 1))
        carry = lax.fori_loop(0, i, lambda j, cr: step(j, cr, False), init)
        (m0, a0), (m1, a1) = step(i, carry, True)
        l0 = pltpu.roll(a0, FOX_DH, 1)
        l1 = pltpu.roll(a1, FOX_DH, 1)
        o = jnp.where(lo, a0 / l0, a1 / l1)
        o_ref[...] = o
        m_ref[...] = jnp.where(lo, m0, m1)
        li_ref[...] = jnp.where(lo, 1.0 / l0, 1.0 / l1)
        og_ref[...] = (o * jax.nn.sigmoid(g_ref[...])).astype(BF16)

        @pl.when((p == P - 1) & (i == nt - 1))
        def _():
            for cp in _xchg_copies(x_in, x_out, [False] * nx, sems):
                cp.wait()

    tile = pl.BlockSpec((T, LANES), lambda p, i: (i, p))
    full = pl.BlockSpec((Lp, LANES), lambda p, i: (0, p))
    HBM = pl.BlockSpec(memory_space=pltpu.HBM)
    return pl.pallas_call(
        body, name=name, grid=(P, nt),
        in_specs=[tile, full, full, pl.BlockSpec((T, LANES), lambda p, i: (i, 0)),
                  pl.BlockSpec((H, Lp), lambda p, i: (0, 0)),
                  pl.BlockSpec((T, LANES), lambda p, i: (i, 3 * P + p))] + [HBM] * nx,
        out_specs=[tile, tile, tile, tile] + [HBM] * nx,
        out_shape=[jax.ShapeDtypeStruct((Lp, D), F32), jax.ShapeDtypeStruct((Lp, D), BF16),
                   jax.ShapeDtypeStruct((Lp, D), F32), jax.ShapeDtypeStruct((Lp, D), F32)]
        + _xchg_out_shapes(xchg, [False] * nx),
        scratch_shapes=_xchg_sems(nx),
        compiler_params=_params(("arbitrary", "arbitrary")),
    )(qn, kn, vb, c, cT, proj, *xchg)


def _fox_out_dx(dhb, w_out, o, proj, linv, D, name):
    Lp = o.shape[0]
    tm = HEAD_ROWS

    def body(a_ref, w_ref, o_ref, g_ref, li_ref, do_ref, dg_ref, dl_ref):
        lo = _lane_lo()
        dog_all = lax.dot_general(a_ref[...], w_ref[...], (((1,), (1,)), ((), ())), preferred_element_type=F32)
        for b in range(D // LANES):
            sl = slice(b * LANES, (b + 1) * LANES)
            dog = dog_all[:, sl]
            sig = jax.nn.sigmoid(g_ref[:, sl])
            ov = o_ref[:, sl]
            do = (dog * sig * li_ref[:, sl]).astype(BF16)
            do_ref[:, sl] = do
            dg_ref[:, sl] = (dog * ov * sig * (1.0 - sig)).astype(BF16)
            t = do.astype(F32) * ov
            d0 = jnp.sum(jnp.where(lo, t, 0.0), axis=1, keepdims=True)
            d1 = jnp.sum(jnp.where(lo, 0.0, t), axis=1, keepdims=True)
            dl_ref[:, sl] = jnp.where(lo, d0, d1)

    row = pl.BlockSpec((tm, D), lambda i: (i, 0))
    return pl.pallas_call(
        body, name=name, grid=(Lp // tm,),
        in_specs=[row, pl.BlockSpec((D, D), lambda i: (0, 0)), row, pl.BlockSpec((tm, D), lambda i: (i, 3)), row],
        out_specs=[row, row, row],
        out_shape=[jax.ShapeDtypeStruct((Lp, D), BF16), jax.ShapeDtypeStruct((Lp, D), BF16),
                   jax.ShapeDtypeStruct((Lp, D), F32)],
        compiler_params=_params(("parallel",)),
    )(dhb, w_out, o, proj, linv)


def _fox_attn_bwd(qn, kn, vb, c, cT, do, mshift, delta, xchg, T, D, name):
    Lp = qn.shape[0]
    P = D // LANES
    nt = Lp // T
    H = cT.shape[0]
    nx = len(xchg)

    def body(q_ref, do_ref, m_ref, dl_ref, c_ref, k_ref, v_ref, cT_ref, *rest):
        x_in, (dq_ref, dk_ref, dv_ref, dc_ref), x_out, sems = rest[:nx], rest[nx:nx + 4], rest[nx + 4:2 * nx + 4], rest[2 * nx + 4:]
        p = pl.program_id(0)
        i = pl.program_id(1)

        @pl.when((p == 0) & (i == 0))
        def _():
            for cp in _xchg_copies(x_in, x_out, [True] * nx, sems):
                cp.start()

        @pl.when(i == 0)
        def _():
            dk_ref[...] = jnp.zeros_like(dk_ref)
            dv_ref[...] = jnp.zeros_like(dv_ref)
            dc_ref[...] = jnp.zeros_like(dc_ref)

        lo = _lane_lo()
        q = q_ref[...]
        do = do_ref[...]
        zero = jnp.zeros_like(q)
        qh = (jnp.where(lo, q, zero), jnp.where(lo, zero, q))
        doh = (jnp.where(lo, do, zero), jnp.where(lo, zero, do))
        msh = _split_halves(m_ref[...])
        dlt = _split_halves(dl_ref[...])
        cblk = c_ref[...]
        shift = tuple(_pick_col(cblk, 2 * p + h) - msh[h] for h in (0, 1))

        def step(j, carry, masked):
            k0 = pl.multiple_of(j * T, LANES)
            kj = k_ref[pl.ds(k0, T), :]
            vj = v_ref[pl.ds(k0, T), :]
            mask = _fox_mask(i, k0, T) if masked else None
            dqs, dks, dvs = [], [], []
            for h in (0, 1):
                ck = cT_ref[pl.ds(2 * p + h, 1), pl.ds(k0, T)]
                t = lax.dot_general(qh[h], kj, (((1,), (1,)), ((), ())), preferred_element_type=F32) - ck
                if masked:
                    t = jnp.where(mask, t, NEG)
                pb = jnp.exp(t + shift[h]).astype(BF16)
                dp = lax.dot_general(doh[h], vj, (((1,), (1,)), ((), ())), preferred_element_type=F32)
                ds = pb.astype(F32) * (dp - dlt[h])
                dsb = ds.astype(BF16)
                dqs.append(carry[h] + jnp.dot(dsb, kj, preferred_element_type=F32))
                dks.append(lax.dot_general(dsb, q, (((0,), (0,)), ((), ())), preferred_element_type=F32))
                dvs.append(lax.dot_general(pb, do, (((0,), (0,)), ((), ())), preferred_element_type=F32))
                dc_ref[0, h:h + 1, pl.ds(k0, T)] += -jnp.sum(ds, axis=0, keepdims=True)
            dk_ref[pl.ds(k0, T), :] += jnp.where(lo, dks[0], dks[1])
            dv_ref[pl.ds(k0, T), :] += jnp.where(lo, dvs[0], dvs[1])
            return tuple(dqs)

        init = (jnp.zeros((T, LANES), F32), jnp.zeros((T, LANES), F32))
        carry = lax.fori_loop(0, i, lambda j, cr: step(j, cr, False), init)
        dq0, dq1 = step(i, carry, True)
        dq_ref[...] = jnp.where(lo, dq0, dq1)

        @pl.when((p == P - 1) & (i == nt - 1))
        def _():
            for cp in _xchg_copies(x_in, x_out, [True] * nx, sems):
                cp.wait()

    tile = pl.BlockSpec((T, LANES), lambda p, i: (i, p))
    full = pl.BlockSpec((Lp, LANES), lambda p, i: (0, p))
    HBM = pl.BlockSpec(memory_space=pltpu.HBM)
    return pl.pallas_call(
        body, name=name, grid=(P, nt),
        in_specs=[tile, tile, tile, tile, pl.BlockSpec((T, LANES), lambda p, i: (i, 0)), full, full,
                  pl.BlockSpec((H, Lp), lambda p, i: (0, 0))] + [HBM] * nx,
        out_specs=[tile, full, full, pl.BlockSpec((1, 8, Lp), lambda p, i: (p, 0, 0))] + [HBM] * nx,
        out_shape=[jax.ShapeDtypeStruct((Lp, D), F32)] * 3 + [jax.ShapeDtypeStruct((P, 8, Lp), F32)]
        + _xchg_out_shapes(xchg, [True] * nx),
        scratch_shapes=_xchg_sems(nx),
        compiler_params=_params(("arbitrary", "arbitrary")),
    )(qn, do, mshift, delta, c, kn, vb, cT, *xchg)


def _scan_rows(x, reverse):
    C = x.shape[0]
    row = lax.broadcasted_iota(jnp.int32, (C, 1), 0)
    step = 1
    while step < C:
        if reverse:
            x = x + jnp.where(row < C - step, pltpu.roll(x, C - step, 0), 0.0)
        else:
            x = x + jnp.where(row >= step, pltpu.roll(x, step, 0), 0.0)
        step *= 2
    return x


@jax.custom_vjp
def _cumsum_rows(x):
    return _scan_rows(x, False)


_cumsum_rows.defvjp(lambda x: (_scan_rows(x, False), None), lambda _, g: (_scan_rows(g, True),))


def _hgrn_chunk(St, qr, z, vi, go, p0, p1, gg):
    C = qr.shape[0]
    lb = jax.nn.sigmoid(p1 - p0)
    a = jnp.log(lb)
    cc = jnp.log1p(-lb) + _log_sigmoid(z)
    log_f = jnp.maximum(a, cc) + jnp.log1p(jnp.exp(-jnp.abs(a - cc)))
    k = (1.0 - lb) * jax.nn.sigmoid(-z)
    q = qr * jax.nn.sigmoid(qr)
    row = lax.broadcasted_iota(jnp.int32, (C, C), 0)
    col = lax.broadcasted_iota(jnp.int32, (C, C), 1)
    causal = col <= row
    b = _cumsum_rows(log_f)
    mid = lax.broadcasted_iota(jnp.int32, (C, 1), 0) == C // 2 - 1
    r = jnp.sum(jnp.where(mid, b, 0.0), axis=0, keepdims=True)
    b_last = jnp.sum(log_f, axis=0, keepdims=True)
    attn = jnp.where(causal, _d_nt(q * jnp.exp(b - r), k * jnp.exp(r - b)), 0.0)
    o = _d_nn(attn, vi) + _d_nt(q * jnp.exp(b), St)
    St_new = St * jnp.exp(b_last) + _d_tn(vi, k * jnp.exp(b_last - b))
    og = _rms(o, gg) * (go * jax.nn.sigmoid(go))
    return St_new, og


def _hgrn_heads_per_step(H):
    return 8 if H % 8 == 0 else 4 if H % 4 == 0 else 1


def _hgrn_specs(T, W, nhb, rev_nt=None):
    if rev_nt is None:
        return [pl.BlockSpec((T, W), functools.partial(lambda hb, t, g: (t, g * nhb + hb), g=g)) for g in range(4)]
    return [pl.BlockSpec((T, W), functools.partial(lambda hb, t, g: (rev_nt - 1 - t, g * nhb + hb), g=g))
            for g in range(4)]


def _hgrn_fwd(proj, lbp, gg, T, name):
    Lp = proj.shape[0]
    D = proj.shape[1] // 4
    H = D // LANES
    hps = _hgrn_heads_per_step(H)
    W = hps * LANES
    nhb = H // hps
    nt = Lp // T
    ncc = T // HGRN_CHUNK

    def body(q_ref, z_ref, i_ref, go_ref, p_ref, gg_ref, og_ref, ss_ref, st_ref):
        @pl.when(pl.program_id(1) == 0)
        def _():
            st_ref[...] = jnp.zeros_like(st_ref)

        gain = gg_ref[...]

        def chunk(cidx, states):
            sl = pl.ds(pl.multiple_of(cidx * HGRN_CHUNK, HGRN_CHUNK), HGRN_CHUNK)
            new = []
            for hh in range(hps):
                ln = slice(hh * LANES, (hh + 1) * LANES)
                ss_ref[hh, cidx] = states[hh]
                St_new, og = _hgrn_chunk(states[hh], q_ref[sl, ln], z_ref[sl, ln], i_ref[sl, ln], go_ref[sl, ln],
                                         p_ref[0:1, ln], p_ref[1:2, ln], gain)
                og_ref[sl, ln] = og.astype(BF16)
                new.append(St_new)
            return tuple(new)

        states = lax.fori_loop(0, ncc, chunk, tuple(st_ref[hh] for hh in range(hps)))
        for hh in range(hps):
            st_ref[hh] = states[hh]

    return pl.pallas_call(
        body, name=name, grid=(nhb, nt),
        in_specs=_hgrn_specs(T, W, nhb) + [pl.BlockSpec((2, W), lambda hb, t: (0, hb)),
                                           pl.BlockSpec((1, LANES), lambda hb, t: (0, 0))],
        out_specs=[pl.BlockSpec((T, W), lambda hb, t: (t, hb)),
                   pl.BlockSpec((hps, ncc, LANES, LANES), lambda hb, t: (hb, t, 0, 0))],
        out_shape=[jax.ShapeDtypeStruct((Lp, D), BF16),
                   jax.ShapeDtypeStruct((H, Lp // HGRN_CHUNK, LANES, LANES), F32)],
        scratch_shapes=[pltpu.VMEM((hps, LANES, LANES), F32)],
        compiler_params=_params(("parallel", "arbitrary")),
    )(proj, proj, proj, proj, lbp, gg)


def _hgrn_bwd(proj, lbp, gg, dog, ss, T, name):
    Lp = proj.shape[0]
    D = proj.shape[1] // 4
    H = D // LANES
    hps = _hgrn_heads_per_step(H)
    W = hps * LANES
    nhb = H // hps
    assert nhb == 1, "d proj is written as whole rows: every head in one grid step"
    nt = Lp // T
    ncc = T // HGRN_CHUNK

    def body(q_ref, z_ref, i_ref, go_ref, p_ref, gg_ref, dog_ref, ss_ref, dproj_ref, dp_ref, dgg_ref, dst_ref):
        hb = pl.program_id(0)
        t = pl.program_id(1)

        @pl.when(t == 0)
        def _():
            dst_ref[...] = jnp.zeros_like(dst_ref)
            dp_ref[...] = jnp.zeros_like(dp_ref)

        @pl.when((t == 0) & (hb == 0))
        def _():
            dgg_ref[...] = jnp.zeros_like(dgg_ref)

        gain = gg_ref[...]
        row0 = (nt - 1 - t) * T

        def chunk(cc, carry):
            dstates, dps, dgain_sum = carry
            cidx = ncc - 1 - cc
            r0 = pl.multiple_of(cidx * HGRN_CHUNK, HGRN_CHUNK)
            sl = pl.ds(r0, HGRN_CHUNK)
            real = (row0 + r0 + lax.broadcasted_iota(jnp.int32, (HGRN_CHUNK, 1), 0)) >= N_PAD
            new_d, new_p = [], []
            for hh in range(hps):
                ln = slice(hh * LANES, (hh + 1) * LANES)
                _, vjp = jax.vjp(_hgrn_chunk, ss_ref[hh, cidx], q_ref[sl, ln], z_ref[sl, ln], i_ref[sl, ln],
                                 go_ref[sl, ln], p_ref[0:1, ln], p_ref[1:2, ln], gain)
                dSt, dq, dz, di, dgo, dp0, dp1, dgain = vjp((dstates[hh], dog_ref[sl, ln]))
                for grp, dval in enumerate((dq, dz, di, dgo)):
                    dproj_ref[sl, grp * D + hh * LANES:grp * D + (hh + 1) * LANES] = (
                        jnp.where(real, dval, 0.0).astype(BF16))
                new_d.append(dSt)
                new_p.append((dps[hh][0] + dp0, dps[hh][1] + dp1))
                dgain_sum = dgain_sum + dgain
            return tuple(new_d), tuple(new_p), dgain_sum

        zero_row = jnp.zeros((1, LANES), F32)
        init = (tuple(dst_ref[hh] for hh in range(hps)), tuple((zero_row, zero_row) for _ in range(hps)), zero_row)
        dstates, dps, dgain_sum = lax.fori_loop(0, ncc, chunk, init)
        for hh in range(hps):
            ln = slice(hh * LANES, (hh + 1) * LANES)
            dst_ref[hh] = dstates[hh]
            dp_ref[0:1, ln] += dps[hh][0]
            dp_ref[1:2, ln] += dps[hh][1]
        dgg_ref[0:1, :] += dgain_sum

    rev = pl.BlockSpec((T, W), lambda hb, t: (nt - 1 - t, hb))
    return pl.pallas_call(
        body, name=name, grid=(nhb, nt),
        in_specs=_hgrn_specs(T, W, nhb, nt) + [pl.BlockSpec((2, W), lambda hb, t: (0, hb)),
                                               pl.BlockSpec((1, LANES), lambda hb, t: (0, 0)), rev,
                                               pl.BlockSpec((hps, ncc, LANES, LANES),
                                                            lambda hb, t: (hb, nt - 1 - t, 0, 0))],
        out_specs=[pl.BlockSpec((T, 4 * D), lambda hb, t: (nt - 1 - t, 0)), pl.BlockSpec((8, W), lambda hb, t: (0, hb)),
                   pl.BlockSpec((8, LANES), lambda hb, t: (0, 0))],
        out_shape=[jax.ShapeDtypeStruct((Lp, 4 * D), BF16), jax.ShapeDtypeStruct((8, D), F32),
                   jax.ShapeDtypeStruct((8, LANES), F32)],
        scratch_shapes=[pltpu.VMEM((hps, LANES, LANES), F32)],
        compiler_params=_params(("arbitrary", "arbitrary")),
    )(proj, proj, proj, proj, lbp, gg, dog, ss)


def _xchg_sems(n_arr):
    return [pltpu.SemaphoreType.DMA((n_arr * (N_DEV - 1),)), pltpu.SemaphoreType.DMA((n_arr * (N_DEV - 1),)),
            pltpu.SemaphoreType.DMA((n_arr,))]


def _xchg_copies(ins, outs, per_peer, sems):
    send_sems, recv_sems, local_sems = sems
    x, y, c = lax.axis_index("x"), lax.axis_index("y"), lax.axis_index("c")
    me = 4 * x + 2 * y + c
    copies = []
    for n in range(len(ins)):
        src = ins[n].at[me] if per_peer[n] else ins[n]
        copies.append(pltpu.make_async_copy(src, outs[n].at[me], local_sems.at[n]))
    for rel in range(1, N_DEV):
        fx, fy, fc = (rel >> 2) & 1, (rel >> 1) & 1, rel & 1
        px = 1 - x if fx else x
        py = 1 - y if fy else y
        pc = 1 - c if fc else c
        peer = 4 * px + 2 * py + pc
        for n in range(len(ins)):
            src = ins[n].at[peer] if per_peer[n] else ins[n]
            copies.append(pltpu.make_async_remote_copy(
                src_ref=src, dst_ref=outs[n].at[me],
                send_sem=send_sems.at[n * (N_DEV - 1) + rel - 1],
                recv_sem=recv_sems.at[n * (N_DEV - 1) + rel - 1],
                device_id=(px, py, pc), device_id_type=pl.DeviceIdType.MESH))
    return copies


def _xchg_out_shapes(arrays, per_peer):
    return [jax.ShapeDtypeStruct(a.shape if pp else (N_DEV,) + a.shape, a.dtype) for a, pp in zip(arrays, per_peer)]


def _exchange(arrays, per_peer, name):
    n_arr = len(arrays)
    HBM = pl.BlockSpec(memory_space=pltpu.HBM)

    def body(*refs):
        copies = _xchg_copies(refs[:n_arr], refs[n_arr:2 * n_arr], per_peer, refs[2 * n_arr:])
        for cp in copies:
            cp.start()
        for cp in copies:
            cp.wait()

    return pl.pallas_call(
        body, name=name,
        in_specs=[HBM] * n_arr, out_specs=[HBM] * n_arr, out_shape=_xchg_out_shapes(arrays, per_peer),
        scratch_shapes=_xchg_sems(n_arr),
    )(*arrays)


ADAMW_VMEM_BUDGET = 36 * 1024 * 1024


def _adamw(recv, w, m, v, name):
    shape = w.shape
    C = shape[-1]
    R = math.prod(shape[:-1])
    lanes = -(-C // LANES) * LANES
    row_bytes = 2 * lanes * (N_DEV * recv.dtype.itemsize + 7 * 4)
    rc = _row_chunk(R, max(16, ADAMW_VMEM_BUDGET // row_bytes), 16 if recv.dtype == BF16 else 8)

    def body(r_ref, w_ref, m_ref, v_ref, g_ref, d_ref, mo_ref, vo_ref):
        g = r_ref[0].astype(F32)
        for s in range(1, N_DEV):
            g = g + r_ref[s].astype(F32)
        mn = ADAM_B1 * m_ref[...] + (1.0 - ADAM_B1) * g
        vn = ADAM_B2 * v_ref[...] + (1.0 - ADAM_B2) * (g * g)
        m_hat = mn / (1.0 - ADAM_B1 ** ADAM_STEP)
        v_hat = vn / (1.0 - ADAM_B2 ** ADAM_STEP)
        g_ref[...] = g
        d_ref[...] = -ADAM_LR * (m_hat / (jnp.sqrt(v_hat) + ADAM_EPS) + ADAM_WD * w_ref[...])
        mo_ref[...] = mn
        vo_ref[...] = vn

    row = pl.BlockSpec((rc, C), lambda i: (i, 0))
    outs = pl.pallas_call(
        body, name=name, grid=(R // rc,),
        in_specs=[pl.BlockSpec((N_DEV, rc, C), lambda i: (0, i, 0)), row, row, row],
        out_specs=[row] * 4,
        out_shape=[jax.ShapeDtypeStruct((R, C), F32)] * 4,
        compiler_params=_params(("parallel",)),
    )(recv.reshape(N_DEV, R, C), w.reshape(R, C), m.reshape(R, C), v.reshape(R, C))
    return [o.reshape(shape) for o in outs]


def _gathered_to_full(g, name):
    if name in COL_SHARDED:
        g = jnp.moveaxis(g, 0, -2)
        return g.reshape(g.shape[:-2] + (g.shape[-2] * g.shape[-1],))
    g = jnp.moveaxis(g, 0, -3)
    return g.reshape(g.shape[:-3] + (g.shape[-3] * g.shape[-2], g.shape[-1]))


def _full_to_slabs(full, name):
    if name in COL_SHARDED:
        f = full.reshape(full.shape[:-1] + (N_DEV, full.shape[-1] // N_DEV))
        return jnp.moveaxis(f, -2, 0)
    f = full.reshape(full.shape[:-2] + (N_DEV, full.shape[-2] // N_DEV, full.shape[-1]))
    return jnp.moveaxis(f, -3, 0)


def _pack_small(arrs):
    rows = []
    for a in arrs:
        flat = a.astype(F32).reshape(-1)
        pad = (-flat.shape[0]) % LANES
        rows.append(jnp.pad(flat, (0, pad)).reshape(-1, LANES))
    p = jnp.concatenate(rows, axis=0)
    return jnp.pad(p, ((0, (-p.shape[0]) % 8), (0, 0)))


def _unpack_small(packed, shapes):
    out, off = [], 0
    for shp in shapes:
        n = math.prod(shp)
        nr = -(-n // LANES)
        out.append(packed[off:off + nr].reshape(-1)[:n].reshape(shp))
        off += nr
    return out


def _local_step(x, target, meta, w_fox_in, late, small):
    S, D = x.shape
    Lp = S + HEAD_ROWS
    T = ROW_TILE if Lp % ROW_TILE == 0 else HEAD_ROWS
    P = D // LANES
    row = lambda v: v.reshape(1, -1).astype(F32)

    w_fin = jnp.pad(w_fox_in[0], ((0, 0), (0, LANES - w_fox_in.shape[-1] % LANES)))
    n_heads = w_fox_in.shape[-1] - 4 * D
    bf = jnp.pad(row(small["fox_b_f"]), ((0, 0), (0, LANES - small["fox_b_f"].size)))
    qg = jnp.tile(row(small["fox_q_norm"]), (1, 2))
    kg = jnp.tile(row(small["fox_k_norm"]), (1, 2))

    h0 = jnp.concatenate([jnp.zeros((N_PAD, D), F32), meta, x], axis=0)

    hn0 = _rms_fwd(h0, row(small["attn_norm"][0]), T, "rms0_fwd")
    proj0 = _mm(hn0, w_fin, "nn", F32, "fox_in_fwd")
    qn, kn, vb, c = _fox_prep_fwd(proj0, bf, qg, kg, T, D, "fox_prep_fwd")
    cT = c.T[:2 * P].at[:, :N_PAD].set(PAD_SHIFT)
    o, og0, mshift, linv, *gathered = _fox_attn_fwd(qn, kn, vb, c, cT, proj0, [late[n] for n in LATE], T, D,
                                                    "fox_attn_fwd")
    wl = {n: _gathered_to_full(g, n) for n, g in zip(LATE, gathered)}
    w_fout, w_hin, w_hout = wl["fox_w_out"][0], wl["hgrn_w_in"][0], wl["hgrn_w_out"][0]
    w_uin, w_uout = wl["ffn_w_in"], wl["ffn_w_out"]
    h1, hf0 = _out_proj_fwd(og0, w_fout, h0, row(small["ffn_norm"][0]), "fox_out_fwd")
    gu0 = _ffn_in_fwd(hf0, w_uin[0], "ffn0_in_fwd")
    act0 = gu0[2]
    h2, hn1 = _out_proj_fwd(act0, w_uout[0], h1, row(small["attn_norm"][1]), "ffn0_out_fwd")
    proj1 = _mm(hn1, w_hin, "nn", F32, "hgrn_in_fwd")
    lbp = small["hgrn_lower_bounds"].astype(F32)
    ggn = row(small["hgrn_g_norm"])
    Th = HGRN_TILE if Lp % HGRN_TILE == 0 else HEAD_ROWS
    og1, ss = _hgrn_fwd(proj1, lbp, ggn, Th, "hgrn_fwd")
    h3, hf1 = _out_proj_fwd(og1, w_hout, h2, row(small["ffn_norm"][1]), "hgrn_out_fwd")
    gu1 = _ffn_in_fwd(hf1, w_uin[1], "ffn1_in_fwd")
    act1 = gu1[2]
    h4 = _mm(act1, w_uout[1], "nn", F32, "ffn1_out_fwd", res=h3)
    loss_blk, dh4, dh4b, d_final = _final_loss(h4, row(small["final_norm"]), target, "final_loss")

    grads = {}

    def ffn_bwd(i, dh, dhb, h_in, hf, gu, act, tag):
        grads_out = _mm(act, dhb, "tn", F32, f"ffn{i}_out_dw")
        dgu = _ffn_out_dx(dhb, w_uout[i], gu[0], gu[1], f"ffn{i}_out_dx")
        grads_in = _mm(hf, dgu, "tn", F32, f"ffn{i}_in_dw")
        dh_new, dh_newb, dgain = _in_proj_dx(dgu, w_uin[i], h_in, row(small["ffn_norm"][i]), dh, f"ffn{i}_in_dx")
        return dh_new, dh_newb, grads_in, grads_out, dgain

    dh3, dh3b, g_uin1, g_uout1, d_fn1 = ffn_bwd(1, dh4, dh4b, h3, hf1, gu1, act1, "1")
    grads["hgrn_w_out"] = _mm(og1, dh3b, "tn", F32, "hgrn_out_dw")[None]
    dog1 = _mm(dh3b, w_hout, "nt", F32, "hgrn_out_dx")
    dproj1, d_lb, d_gg = _hgrn_bwd(proj1, lbp, ggn, dog1, ss, Th, "hgrn_bwd")
    grads["hgrn_w_in"] = _mm(hn1, dproj1, "tn", F32, "hgrn_in_dw")[None]
    dh2, dh2b, d_an1 = _in_proj_dx(dproj1, w_hin, h2, row(small["attn_norm"][1]), dh3, "hgrn_in_dx")
    dh1, dh1b, g_uin0, g_uout0, d_fn0 = ffn_bwd(0, dh2, dh2b, h1, hf0, gu0, act0, "0")
    grads["ffn_w_in"] = jnp.stack([g_uin0, g_uin1])
    grads["ffn_w_out"] = jnp.stack([g_uout0, g_uout1])
    grads["fox_w_out"] = _mm(og0, dh1b, "tn", F32, "fox_out_dw")[None]
    do, dgate, delta = _fox_out_dx(dh1b, w_fout, o, proj0, linv, D, "fox_out_dx")
    slabs = [_full_to_slabs(grads[n], n).astype(BF16) for n in LATE]
    dqn, dkn, dv, dcr, *recv = _fox_attn_bwd(qn, kn, vb, c, cT, do, mshift, delta, slabs, T, D, "fox_attn_bwd")
    for n in LATE:
        del grads[n]
    dc = jnp.pad(dcr[:, :2, :].reshape(2 * P, Lp).T, ((0, 0), (0, LANES - 2 * P)))
    Tp = T // 2 if T == ROW_TILE else T
    dproj0, sm = _fox_prep_bwd(proj0, bf, qg, kg, dqn, dkn, dv, dgate, dc, Tp, D, "fox_prep_bwd")
    g_fin = _mm(hn0, dproj0, "tn", F32, "fox_in_dw")[:, :4 * D + n_heads][None]
    dh0, _, d_an0, r_fin = _in_proj_dx(dproj0, w_fin, h0, row(small["attn_norm"][0]), dh1, "fox_in_dx",
                                       xchg=[_full_to_slabs(g_fin, "fox_w_in").astype(BF16)])

    grads["meta_tokens"] = dh0[N_PAD:HEAD_ROWS]
    grads["attn_norm"] = jnp.concatenate([d_an0, d_an1], axis=0)
    grads["ffn_norm"] = jnp.concatenate([d_fn0, d_fn1], axis=0)
    grads["final_norm"] = d_final[0]
    grads["fox_b_f"] = sm[0:1, :n_heads]
    grads["fox_q_norm"] = sm[1:2, :FOX_DH] + sm[1:2, FOX_DH:]
    grads["fox_k_norm"] = sm[2:3, :FOX_DH] + sm[2:3, FOX_DH:]
    grads["hgrn_lower_bounds"] = d_lb[0:2]
    grads["hgrn_g_norm"] = d_gg[0:1]
    return loss_blk[0, 0], dh0[HEAD_ROWS:], grads, dict(zip(LATE, recv), fox_w_in=r_fin)


def kernel(x, meta_tokens, attn_norm, ffn_norm, final_norm, fox_w_in, fox_b_f, fox_q_norm, fox_k_norm, fox_w_out, hgrn_w_in, hgrn_lower_bounds, hgrn_g_norm, hgrn_w_out, ffn_w_in, ffn_w_out, loss_target, m_meta_tokens, m_attn_norm, m_ffn_norm, m_final_norm, m_fox_w_in, m_fox_b_f, m_fox_q_norm, m_fox_k_norm, m_fox_w_out, m_hgrn_w_in, m_hgrn_lower_bounds, m_hgrn_g_norm, m_hgrn_w_out, m_ffn_w_in, m_ffn_w_out, v_meta_tokens, v_attn_norm, v_ffn_norm, v_final_norm, v_fox_w_in, v_fox_b_f, v_fox_q_norm, v_fox_k_norm, v_fox_w_out, v_hgrn_w_in, v_hgrn_lower_bounds, v_hgrn_g_norm, v_hgrn_w_out, v_ffn_w_in, v_ffn_w_out):
    w = dict(meta_tokens=meta_tokens, attn_norm=attn_norm, ffn_norm=ffn_norm, final_norm=final_norm,
             fox_w_in=fox_w_in, fox_b_f=fox_b_f, fox_q_norm=fox_q_norm, fox_k_norm=fox_k_norm,
             fox_w_out=fox_w_out, hgrn_w_in=hgrn_w_in, hgrn_lower_bounds=hgrn_lower_bounds,
             hgrn_g_norm=hgrn_g_norm, hgrn_w_out=hgrn_w_out, ffn_w_in=ffn_w_in, ffn_w_out=ffn_w_out)
    m = dict(meta_tokens=m_meta_tokens, attn_norm=m_attn_norm, ffn_norm=m_ffn_norm, final_norm=m_final_norm,
             fox_w_in=m_fox_w_in, fox_b_f=m_fox_b_f, fox_q_norm=m_fox_q_norm, fox_k_norm=m_fox_k_norm,
             fox_w_out=m_fox_w_out, hgrn_w_in=m_hgrn_w_in, hgrn_lower_bounds=m_hgrn_lower_bounds,
             hgrn_g_norm=m_hgrn_g_norm, hgrn_w_out=m_hgrn_w_out, ffn_w_in=m_ffn_w_in, ffn_w_out=m_ffn_w_out)
    v = dict(meta_tokens=v_meta_tokens, attn_norm=v_attn_norm, ffn_norm=v_ffn_norm, final_norm=v_final_norm,
             fox_w_in=v_fox_w_in, fox_b_f=v_fox_b_f, fox_q_norm=v_fox_q_norm, fox_k_norm=v_fox_k_norm,
             fox_w_out=v_fox_w_out, hgrn_w_in=v_hgrn_w_in, hgrn_lower_bounds=v_hgrn_lower_bounds,
             hgrn_g_norm=v_hgrn_g_norm, hgrn_w_out=v_hgrn_w_out, ffn_w_in=v_ffn_w_in, ffn_w_out=v_ffn_w_out)
    axes = ("x", "y", "c")
    small_shapes = [w[n].shape for n in SMALL]

    g_meta, g_fin = _exchange([w["meta_tokens"].astype(F32), w["fox_w_in"].astype(BF16)], [False] * 2,
                              "gather_weights")
    loss_local, grad_x, grads, recv = _local_step(
        x[0], loss_target[0], _gathered_to_full(g_meta, "meta_tokens"), _gathered_to_full(g_fin, "fox_w_in"),
        {n: w[n].astype(BF16) for n in LATE}, {n: w[n] for n in SMALL})
    loss = lax.psum(loss_local, axes)

    r_meta, r_small = _exchange([_full_to_slabs(grads["meta_tokens"], "meta_tokens"),
                                 _pack_small([grads[n] for n in SMALL])], [True, False], "scatter_grads")
    recv.update(meta_tokens=r_meta)

    res = {n: _adamw(recv[n], w[n], m[n], v[n], "adamw_" + n) for n in BIG}
    sml = _adamw(r_small, _pack_small([w[n] for n in SMALL]), _pack_small([m[n] for n in SMALL]),
                 _pack_small([v[n] for n in SMALL]), "adamw_small")
    outs = []
    for k in range(4):
        d = {n: res[n][k] for n in BIG}
        d.update(zip(SMALL, _unpack_small(sml[k], small_shapes)))
        outs.extend(d[n] for n in WEIGHTS)
    return (loss, grad_x[None], *outs)
```

```python
import functools
import math

import jax
import jax.numpy as jnp
from jax import lax
from jax.experimental import pallas as pl
from jax.experimental.pallas import tpu as pltpu

F32 = jnp.float32
BF16 = jnp.bfloat16
EPS = 1e-6
N_META = 16
LANES = 128
HEAD_ROWS = 256
ROW_TILE = 768
N_PAD = HEAD_ROWS - N_META
FOX_DH = 64
HGRN_CHUNK = 64
HGRN_TILE = 384
N_DEV = 8
NEG = -1e30
PAD_SHIFT = 1e4
VMEM_LIMIT = 56 * 1024 * 1024
HI = lax.Precision.HIGHEST

ADAM_LR = 0.001
ADAM_B1 = 0.9
ADAM_B2 = 0.999
ADAM_EPS = 1e-08
ADAM_WD = 0.01
ADAM_STEP = 10

BIG = ("meta_tokens", "fox_w_in", "fox_w_out", "hgrn_w_in", "hgrn_w_out", "ffn_w_in", "ffn_w_out")
SMALL = ("attn_norm", "ffn_norm", "final_norm", "fox_b_f", "fox_q_norm", "fox_k_norm",
         "hgrn_lower_bounds", "hgrn_g_norm")
WEIGHTS = ("meta_tokens", "attn_norm", "ffn_norm", "final_norm", "fox_w_in", "fox_b_f", "fox_q_norm",
           "fox_k_norm", "fox_w_out", "hgrn_w_in", "hgrn_lower_bounds", "hgrn_g_norm", "hgrn_w_out",
           "ffn_w_in", "ffn_w_out")
COL_SHARDED = ("meta_tokens", "fox_w_in", "hgrn_w_in", "ffn_w_in")
LATE = ("fox_w_out", "hgrn_w_in", "hgrn_w_out", "ffn_w_in", "ffn_w_out")


def _params(sem=None):
    return pltpu.CompilerParams(dimension_semantics=sem, vmem_limit_bytes=VMEM_LIMIT)


def _tile(n, cap):
    best = None
    for t in range(LANES, min(n, cap) + 1, LANES):
        if n % t == 0:
            best = t
    assert best is not None, (n, cap)
    return best


def _row_chunk(n, cap, mult=8):
    best = n
    for t in range(mult, min(n, cap) + 1, mult):
        if n % t == 0:
            best = t
    return best


def _dg(a, b, ca, cb):
    return lax.dot_general(a.astype(BF16), b.astype(BF16), (((ca,), (cb,)), ((), ())),
                           preferred_element_type=F32)


@jax.custom_vjp
def _d_nn(a, b):
    return _dg(a, b, 1, 0)


@jax.custom_vjp
def _d_nt(a, b):
    return _dg(a, b, 1, 1)


@jax.custom_vjp
def _d_tn(a, b):
    return _dg(a, b, 0, 0)


_d_nn.defvjp(lambda a, b: (_d_nn(a, b), (a, b)), lambda r, g: (_d_nt(g, r[1]), _d_tn(r[0], g)))
_d_nt.defvjp(lambda a, b: (_d_nt(a, b), (a, b)), lambda r, g: (_d_nn(g, r[1]), _d_tn(g, r[0])))
_d_tn.defvjp(lambda a, b: (_d_tn(a, b), (a, b)), lambda r, g: (_d_nt(r[1], g), _d_nn(r[0], g)))


def _log_sigmoid(x):
    return jnp.minimum(x, 0.0) - jnp.log1p(jnp.exp(-jnp.abs(x)))


def _rms(x, g):
    return x * lax.rsqrt(jnp.mean(x * x, axis=-1, keepdims=True) + EPS) * g


def _mm(a, b, mode, out_dtype, name, res=None, tm=None, tn=None, tk=None):
    assert a.dtype == BF16 and b.dtype == BF16, (name, a.dtype, b.dtype)
    if mode == "nn":
        (M, K), N = a.shape, b.shape[1]
    elif mode == "nt":
        (M, K), N = a.shape, b.shape[0]
    else:
        (K, M), N = a.shape, b.shape[1]
    if mode == "nn":
        tm, tn, tk = tm or _tile(M, ROW_TILE), tn or _tile(N, 1408), tk or _tile(K, 2816)
    elif mode == "nt":
        tm, tn, tk = tm or _tile(M, ROW_TILE if K <= 2048 else ROW_TILE // 2), tn or N, tk or K
    else:
        tm, tn, tk = tm or _tile(M, 1408), tn or _tile(N, 1408), tk or _tile(K, ROW_TILE)
    nk = K // tk
    if mode == "tn":
        a_spec = pl.BlockSpec((tk, tm), lambda j, i, k: (k, i))
        dims = (((0,), (0,)), ((), ()))
    else:
        a_spec = pl.BlockSpec((tm, tk), lambda j, i, k: (i, k))
        dims = (((1,), (1 if mode == "nt" else 0,)), ((), ()))
    if mode == "nt":
        b_spec = pl.BlockSpec((tn, tk), lambda j, i, k: (j, k))
    else:
        b_spec = pl.BlockSpec((tk, tn), lambda j, i, k: (k, j))

    o_spec = pl.BlockSpec((tm, tn), lambda j, i, k: (i, j))

    def body(a_ref, b_ref, *rest):
        r_ref = rest[0] if res is not None else None
        o_ref, acc_ref = rest[-2:]
        k = pl.program_id(2)

        @pl.when(k == 0)
        def _():
            acc_ref[...] = jnp.zeros_like(acc_ref)

        acc_ref[...] += lax.dot_general(a_ref[...], b_ref[...], dims, preferred_element_type=F32)

        @pl.when(k == nk - 1)
        def _():
            out = acc_ref[...] if r_ref is None else acc_ref[...] + r_ref[...]
            o_ref[...] = out.astype(out_dtype)

    return pl.pallas_call(
        body, name=name, grid=(N // tn, M // tm, nk),
        in_specs=[a_spec, b_spec] + ([o_spec] if res is not None else []),
        out_specs=o_spec,
        out_shape=jax.ShapeDtypeStruct((M, N), out_dtype),
        scratch_shapes=[pltpu.VMEM((tm, tn), F32)],
        compiler_params=_params(("parallel", "parallel", "arbitrary")),
    )(a, b, *([res] if res is not None else []))


def _out_proj_fwd(a, w, res, gain, name):
    Lp, K = a.shape
    D = w.shape[1]
    tm = _tile(Lp, ROW_TILE)

    def body(a_ref, w_ref, r_ref, g_ref, h_ref, hn_ref, hnt_ref):
        h = jnp.dot(a_ref[...], w_ref[...], preferred_element_type=F32) + r_ref[...]
        h_ref[...] = h
        hn = _rms(h, g_ref[...])
        hn_ref[...] = hn.astype(BF16)
        hnt_ref[...] = hn.T.astype(BF16)

    row = pl.BlockSpec((tm, D), lambda i: (i, 0))
    return pl.pallas_call(
        body, name=name, grid=(Lp // tm,),
        in_specs=[pl.BlockSpec((tm, K), lambda i: (i, 0)), pl.BlockSpec((K, D), lambda i: (0, 0)), row,
                  pl.BlockSpec((1, D), lambda i: (0, 0))],
        out_specs=[row, row, pl.BlockSpec((D, tm), lambda i: (0, i))],
        out_shape=[jax.ShapeDtypeStruct((Lp, D), F32), jax.ShapeDtypeStruct((Lp, D), BF16),
                   jax.ShapeDtypeStruct((D, Lp), BF16)],
        compiler_params=_params(("parallel",)),
    )(a, w, res, gain)


def _in_proj_dx(dy, w, x, gain, dres, name, xchg=()):
    Lp, N = dy.shape
    D = w.shape[0]
    tm = _tile(Lp, ROW_TILE // 2)
    nt = Lp // tm
    nx = len(xchg)

    def body(dy_ref, w_ref, x_ref, g_ref, dr_ref, *rest):
        x_in, (dx_ref, dxb_ref, dg_ref), x_out, sems = rest[:nx], rest[nx:nx + 3], rest[nx + 3:2 * nx + 3], rest[2 * nx + 3:]

        @pl.when(pl.program_id(0) == 0)
        def _():
            dg_ref[...] = jnp.zeros_like(dg_ref)
            if nx:
                for cp in _xchg_copies(x_in, x_out, [True] * nx, sems):
                    cp.start()

        dhn = lax.dot_general(dy_ref[...], w_ref[...], (((1,), (1,)), ((), ())), preferred_element_type=F32)
        _, vjp = jax.vjp(_rms, x_ref[...], g_ref[...])
        dx, dg = vjp(dhn)
        dx = dx + dr_ref[...]
        dx_ref[...] = dx
        dxb_ref[...] = dx.astype(BF16)
        dg_ref[...] += dg

        if nx:
            @pl.when(pl.program_id(0) == nt - 1)
            def _():
                for cp in _xchg_copies(x_in, x_out, [True] * nx, sems):
                    cp.wait()

    row = pl.BlockSpec((tm, D), lambda i: (i, 0))
    vec = pl.BlockSpec((1, D), lambda i: (0, 0))
    HBM = pl.BlockSpec(memory_space=pltpu.HBM)
    return pl.pallas_call(
        body, name=name, grid=(nt,),
        in_specs=[pl.BlockSpec((tm, N), lambda i: (i, 0)), pl.BlockSpec((D, N), lambda i: (0, 0)), row, vec, row]
        + [HBM] * nx,
        out_specs=[row, row, vec] + [HBM] * nx,
        out_shape=[jax.ShapeDtypeStruct((Lp, D), F32), jax.ShapeDtypeStruct((Lp, D), BF16),
                   jax.ShapeDtypeStruct((1, D), F32)] + _xchg_out_shapes(xchg, [True] * nx),
        scratch_shapes=_xchg_sems(nx) if nx else [],
        compiler_params=_params(("arbitrary",)),
    )(dy, w, x, gain, dres, *xchg)


def _rms_fwd(x, g, T, name):
    Lp, D = x.shape

    def body(x_ref, g_ref, o_ref, ot_ref):
        y = _rms(x_ref[...], g_ref[...])
        o_ref[...] = y.astype(BF16)
        ot_ref[...] = y.T.astype(BF16)

    return pl.pallas_call(
        body, name=name, grid=(Lp // T,),
        in_specs=[pl.BlockSpec((T, D), lambda i: (i, 0)), pl.BlockSpec((1, D), lambda i: (0, 0))],
        out_specs=[pl.BlockSpec((T, D), lambda i: (i, 0)), pl.BlockSpec((D, T), lambda i: (0, i))],
        out_shape=[jax.ShapeDtypeStruct((Lp, D), BF16), jax.ShapeDtypeStruct((D, Lp), BF16)],
        compiler_params=_params(("parallel",)),
    )(x, g)


def _swiglu(gate, up):
    return gate * jax.nn.sigmoid(gate) * up


def _ffn_in_fwd(hf, w_in, name):
    Lp, D = hf.shape
    F = w_in.shape[1] // 2
    tm = _tile(Lp, ROW_TILE)
    tn = _tile(F, 1408)
    nj = F // tn

    def body(a_ref, bg_ref, bu_ref, g_ref, u_ref, act_ref, actt_ref):
        a = a_ref[...]
        g = jnp.dot(a, bg_ref[...], preferred_element_type=F32)
        u = jnp.dot(a, bu_ref[...], preferred_element_type=F32)
        g_ref[...] = g.astype(BF16)
        u_ref[...] = u.astype(BF16)
        act = _swiglu(g, u)
        act_ref[...] = act.astype(BF16)
        actt_ref[...] = act.T.astype(BF16)

    tile = pl.BlockSpec((tm, tn), lambda j, i: (i, j))
    return pl.pallas_call(
        body, name=name, grid=(nj, Lp // tm),
        in_specs=[pl.BlockSpec((tm, D), lambda j, i: (i, 0)), pl.BlockSpec((D, tn), lambda j, i: (0, j)),
                  pl.BlockSpec((D, tn), lambda j, i: (0, nj + j))],
        out_specs=[tile, tile, tile, pl.BlockSpec((tn, tm), lambda j, i: (j, i))],
        out_shape=[jax.ShapeDtypeStruct((Lp, F), BF16)] * 3 + [jax.ShapeDtypeStruct((F, Lp), BF16)],
        compiler_params=_params(("parallel", "parallel")),
    )(hf, w_in, w_in)


def _ffn_out_dx(dhb, w_out, g, u, name):
    Lp, D = dhb.shape
    F = w_out.shape[0]
    tm = HEAD_ROWS

    def body(a_ref, b_ref, g_ref, u_ref, o_ref):
        dact = lax.dot_general(a_ref[...], b_ref[...], (((1,), (1,)), ((), ())), preferred_element_type=F32)
        _, vjp = jax.vjp(_swiglu, g_ref[...].astype(F32), u_ref[...].astype(F32))
        dg, du = vjp(dact)
        o_ref[:, :F] = dg.astype(BF16)
        o_ref[:, F:] = du.astype(BF16)

    wide = pl.BlockSpec((tm, F), lambda i: (i, 0))
    return pl.pallas_call(
        body, name=name, grid=(Lp // tm,),
        in_specs=[pl.BlockSpec((tm, D), lambda i: (i, 0)), pl.BlockSpec((F, D), lambda i: (0, 0)), wide, wide],
        out_specs=pl.BlockSpec((tm, 2 * F), lambda i: (i, 0)),
        out_shape=jax.ShapeDtypeStruct((Lp, 2 * F), BF16),
        compiler_params=_params(("parallel",)),
    )(dhb, w_out, g, u)


def _final_loss(h, g, target, name):
    Lp, D = h.shape
    TR = HEAD_ROWS

    def loss_fn(hh, gg, tt):
        err = _rms(hh, gg) - tt
        return 0.5 * jnp.sum(jnp.mean(err * err, axis=-1))

    def body(h_ref, g_ref, t_ref, loss_ref, dh_ref, dhb_ref, dg_ref):
        i = pl.program_id(0)

        @pl.when(i == 0)
        def _():
            loss_ref[...] = jnp.zeros_like(loss_ref)
            dg_ref[...] = jnp.zeros_like(dg_ref)
            dh_ref[...] = jnp.zeros_like(dh_ref)
            dhb_ref[...] = jnp.zeros_like(dhb_ref)

        @pl.when(i > 0)
        def _():
            val, vjp = jax.vjp(lambda hh, gg: loss_fn(hh, gg, t_ref[...]), h_ref[...], g_ref[...])
            dh, dg = vjp(jnp.ones((), F32))
            dh_ref[...] = dh
            dhb_ref[...] = dh.astype(BF16)
            dg_ref[...] += dg
            loss_ref[...] += val

    row = pl.BlockSpec((TR, D), lambda i: (i, 0))
    return pl.pallas_call(
        body, name=name, grid=(Lp // TR,),
        in_specs=[row, pl.BlockSpec((1, D), lambda i: (0, 0)),
                  pl.BlockSpec((TR, D), lambda i: (jnp.maximum(i - 1, 0), 0))],
        out_specs=[pl.BlockSpec((8, LANES), lambda i: (0, 0)), row, row, pl.BlockSpec((1, D), lambda i: (0, 0))],
        out_shape=[jax.ShapeDtypeStruct((8, LANES), F32), jax.ShapeDtypeStruct((Lp, D), F32),
                   jax.ShapeDtypeStruct((Lp, D), BF16), jax.ShapeDtypeStruct((1, D), F32)],
        compiler_params=_params(("arbitrary",)),
    )(h, g, target)


def _lane_lo():
    return lax.broadcasted_iota(jnp.int32, (1, LANES), 1) < FOX_DH


def _headnorm(x, g, scale):
    lo = _lane_lo()
    x2 = x * x
    s0 = jnp.sum(jnp.where(lo, x2, 0.0), axis=-1, keepdims=True)
    s1 = jnp.sum(jnp.where(lo, 0.0, x2), axis=-1, keepdims=True)
    r = jnp.where(lo, lax.rsqrt(s0 / FOX_DH + EPS), lax.rsqrt(s1 / FOX_DH + EPS))
    return x * r * g * scale


def _fox_prep_fwd(proj, bf, qg, kg, T, D, name):
    Lp = proj.shape[0]
    nb = D // LANES
    scale = FOX_DH ** -0.5

    def body(q_ref, k_ref, v_ref, fl_ref, bf_ref, qg_ref, kg_ref, qn_ref, kn_ref, vb_ref, c_ref, carry_ref):
        @pl.when(pl.program_id(0) == 0)
        def _():
            carry_ref[...] = jnp.zeros_like(carry_ref)

        for b in range(nb):
            sl = slice(b * LANES, (b + 1) * LANES)
            qn_ref[:, sl] = _headnorm(q_ref[:, sl], qg_ref[...], scale).astype(BF16)
            kn_ref[:, sl] = _headnorm(k_ref[:, sl], kg_ref[...], 1.0).astype(BF16)
        vb_ref[...] = v_ref[...].astype(BF16)
        log_f = _log_sigmoid(fl_ref[...] + bf_ref[...])
        row = lax.broadcasted_iota(jnp.int32, (T, T), 0)
        col = lax.broadcasted_iota(jnp.int32, (T, T), 1)
        tri = (col <= row).astype(F32)
        c = jnp.dot(tri, log_f, precision=HI, preferred_element_type=F32) + carry_ref[...]
        c_ref[...] = c
        last = lax.broadcasted_iota(jnp.int32, (T, 1), 0) == T - 1
        carry_ref[...] = jnp.sum(jnp.where(last, c, 0.0), axis=0, keepdims=True)

    wide = lambda j: pl.BlockSpec((T, D), lambda i: (i, j))
    vec = pl.BlockSpec((1, LANES), lambda i: (0, 0))
    return pl.pallas_call(
        body, name=name, grid=(Lp // T,),
        in_specs=[wide(0), wide(1), wide(2), pl.BlockSpec((T, LANES), lambda i: (i, 4 * nb)), vec, vec, vec],
        out_specs=[wide(0), wide(0), wide(0), pl.BlockSpec((T, LANES), lambda i: (i, 0))],
        out_shape=[jax.ShapeDtypeStruct((Lp, D), BF16)] * 3 + [jax.ShapeDtypeStruct((Lp, LANES), F32)],
        scratch_shapes=[pltpu.VMEM((1, LANES), F32)],
        compiler_params=_params(("arbitrary",)),
    )(proj, proj, proj, proj, bf, qg, kg)


def _fox_prep_bwd(proj, bf, qg, kg, dqn, dkn, dv, dgate, dc, T, D, name):
    Lp = proj.shape[0]
    nb = D // LANES
    nt = Lp // T
    scale = FOX_DH ** -0.5

    def body(q_ref, k_ref, fl_ref, bf_ref, qg_ref, kg_ref, dqn_ref, dkn_ref, dv_ref, dgate_ref, dc_ref,
             dproj_ref, sm_ref, carry_ref):
        @pl.when(pl.program_id(0) == 0)
        def _():
            carry_ref[...] = jnp.zeros_like(carry_ref)
            sm_ref[...] = jnp.zeros_like(sm_ref)

        dqg = jnp.zeros((1, LANES), F32)
        dkg = jnp.zeros((1, LANES), F32)
        for b in range(nb):
            sl = slice(b * LANES, (b + 1) * LANES)
            _, vjp = jax.vjp(lambda x, g: _headnorm(x, g, scale), q_ref[:, sl], qg_ref[...])
            dx, dg = vjp(dqn_ref[:, sl])
            dproj_ref[:, sl] = dx.astype(BF16)
            dqg = dqg + dg
            _, vjp = jax.vjp(lambda x, g: _headnorm(x, g, 1.0), k_ref[:, sl], kg_ref[...])
            dx, dg = vjp(dkn_ref[:, sl])
            dproj_ref[:, D + b * LANES:D + (b + 1) * LANES] = dx.astype(BF16)
            dkg = dkg + dg
        dproj_ref[:, 2 * D:3 * D] = dv_ref[...].astype(BF16)
        dproj_ref[:, 3 * D:4 * D] = dgate_ref[...]
        dcv = dc_ref[...]
        row = lax.broadcasted_iota(jnp.int32, (T, T), 0)
        col = lax.broadcasted_iota(jnp.int32, (T, T), 1)
        triu = (col >= row).astype(F32)
        dlogf = jnp.dot(triu, dcv, precision=HI, preferred_element_type=F32) + carry_ref[...]
        carry_ref[...] += jnp.sum(dcv, axis=0, keepdims=True)
        _, vjp = jax.vjp(_log_sigmoid, fl_ref[...] + bf_ref[...])
        (dfl,) = vjp(dlogf)
        dproj_ref[:, 4 * D:] = dfl.astype(BF16)
        sm_ref[0:1, :] += jnp.sum(dfl, axis=0, keepdims=True)
        sm_ref[1:2, :] += dqg
        sm_ref[2:3, :] += dkg

    wide = lambda j: pl.BlockSpec((T, D), lambda i: (nt - 1 - i, j))
    narrow = lambda j: pl.BlockSpec((T, LANES), lambda i: (nt - 1 - i, j))
    vec = pl.BlockSpec((1, LANES), lambda i: (0, 0))
    return pl.pallas_call(
        body, name=name, grid=(nt,),
        in_specs=[wide(0), wide(1), narrow(4 * nb), vec, vec, vec, wide(0), wide(0), wide(0), wide(0), narrow(0)],
        out_specs=[pl.BlockSpec((T, 4 * D + LANES), lambda i: (nt - 1 - i, 0)), pl.BlockSpec((8, LANES), lambda i: (0, 0))],
        out_shape=[jax.ShapeDtypeStruct((Lp, 4 * D + LANES), BF16), jax.ShapeDtypeStruct((8, LANES), F32)],
        scratch_shapes=[pltpu.VMEM((1, LANES), F32)],
        compiler_params=_params(("arbitrary",)),
    )(proj, proj, proj, bf, qg, kg, dqn, dkn, dv, dgate, dc)


def _ln2_ceil(m):
    return jnp.ceil(m * (1.0 / math.log(2.0))) * math.log(2.0)


def _fox_mask(i, k0, T):
    qpos = i * T + lax.broadcasted_iota(jnp.int32, (T, 1), 0)
    kpos = k0 + lax.broadcasted_iota(jnp.int32, (1, T), 1)
    return (kpos <= qpos) & ((kpos >= N_PAD) | (qpos < N_PAD))


def _pick_col(blk, idx):
    lane = lax.broadcasted_iota(jnp.int32, (1, LANES), 1)
    return jnp.sum(jnp.where(lane == idx, blk, 0.0), axis=1, keepdims=True)


def _split_halves(blk):
    lo = _lane_lo()
    return (jnp.max(jnp.where(lo, blk, -jnp.inf), axis=1, keepdims=True),
            jnp.max(jnp.where(lo, -jnp.inf, blk), axis=1, keepdims=True))


def _fox_attn_fwd(qn, kn, vb, c, cT, proj, xchg, T, D, name):
    Lp = qn.shape[0]
    P = D // LANES
    nt = Lp // T
    H = cT.shape[0]
    nx = len(xchg)

    def body(q_ref, k_ref, v_ref, c_ref, cT_ref, g_ref, *rest):
        x_in, (o_ref, og_ref, m_ref, li_ref), x_out, sems = rest[:nx], rest[nx:nx + 4], rest[nx + 4:2 * nx + 4], rest[2 * nx + 4:]
        p = pl.program_id(0)
        i = pl.program_id(1)

        @pl.when((p == 0) & (i == 0))
        def _():
            for cp in _xchg_copies(x_in, x_out, [False] * nx, sems):
                cp.start()

        lo = _lane_lo()
        q = q_ref[...]
        zero = jnp.zeros_like(q)
        qh = (jnp.where(lo, q, zero), jnp.where(lo, zero, q))
        cblk = c_ref[...]
        cq = tuple(_pick_col(cblk, 2 * p + h) for h in (0, 1))
        one = jnp.ones_like(q)

        def step(j, carry, masked):
            k0 = pl.multiple_of(j * T, LANES)
            kj = k_ref[pl.ds(k0, T), :]
            vj = v_ref[pl.ds(k0, T), :]
            vh = (jnp.where(lo, vj, one), jnp.where(lo, one, vj))
            mask = _fox_mask(i, k0, T) if masked else None
            out = []
            for h in (0, 1):
                m, acc = carry[h]
                ck = cT_ref[pl.ds(2 * p + h, 1), pl.ds(k0, T)]
                t = lax.dot_general(qh[h], kj, (((1,), (1,)), ((), ())), preferred_element_type=F32) - ck
                if masked:
                    t = jnp.where(mask, t, NEG)
                m_new = _ln2_ceil(jnp.maximum(m, cq[h] + jnp.max(t, axis=1, keepdims=True)))
                pr = jnp.exp(t + (cq[h] - m_new)).astype(BF16)
                acc = jnp.exp(m - m_new) * acc + jnp.dot(pr, vh[h], preferred_element_type=F32)
                out.append((m_new, acc))
            return tuple(out)

        init = tuple((jnp.full((T, 1), NEG, F32), jnp.zeros((T, LANES), F32)) for _ in (0, 1))
        carry = lax.fori_loop(0, i, lambda j, cr: step(j, cr, False), init)
        (m0, a0), (m1, a1) = step(i, carry, True)
        l0 = pltpu.roll(a0, FOX_DH, 1)
        l1 = pltpu.roll(a1, FOX_DH, 1)
        o = jnp.where(lo, a0 / l0, a1 / l1)
        o_ref[...] = o
        m_ref[...] = jnp.where(lo, m0, m1)
        li_ref[...] = jnp.where(lo, 1.0 / l0, 1.0 / l1)
        og_ref[...] = (o * jax.nn.sigmoid(g_ref[...])).astype(BF16)

        @pl.when((p == P - 1) & (i == nt - 1))
        def _():
            for cp in _xchg_copies(x_in, x_out, [False] * nx, sems):
                cp.wait()

    tile = pl.BlockSpec((T, LANES), lambda p, i: (i, p))
    full = pl.BlockSpec((Lp, LANES), lambda p, i: (0, p))
    HBM = pl.BlockSpec(memory_space=pltpu.HBM)
    return pl.pallas_call(
        body, name=name, grid=(P, nt),
        in_specs=[tile, full, full, pl.BlockSpec((T, LANES), lambda p, i: (i, 0)),
                  pl.BlockSpec((H, Lp), lambda p, i: (0, 0)),
                  pl.BlockSpec((T, LANES), lambda p, i: (i, 3 * P + p))] + [HBM] * nx,
        out_specs=[tile, tile, tile, tile] + [HBM] * nx,
        out_shape=[jax.ShapeDtypeStruct((Lp, D), F32), jax.ShapeDtypeStruct((Lp, D), BF16),
                   jax.ShapeDtypeStruct((Lp, D), F32), jax.ShapeDtypeStruct((Lp, D), F32)]
        + _xchg_out_shapes(xchg, [False] * nx),
        scratch_shapes=_xchg_sems(nx),
        compiler_params=_params(("arbitrary", "arbitrary")),
    )(qn, kn, vb, c, cT, proj, *xchg)


def _fox_out_dx(dhb, w_out, o, proj, linv, D, name):
    Lp = o.shape[0]
    tm = HEAD_ROWS

    def body(a_ref, w_ref, o_ref, g_ref, li_ref, do_ref, dg_ref, dl_ref):
        lo = _lane_lo()
        dog_all = lax.dot_general(a_ref[...], w_ref[...], (((1,), (1,)), ((), ())), preferred_element_type=F32)
        for b in range(D // LANES):
            sl = slice(b * LANES, (b + 1) * LANES)
            dog = dog_all[:, sl]
            sig = jax.nn.sigmoid(g_ref[:, sl])
            ov = o_ref[:, sl]
            do = (dog * sig * li_ref[:, sl]).astype(BF16)
            do_ref[:, sl] = do
            dg_ref[:, sl] = (dog * ov * sig * (1.0 - sig)).astype(BF16)
            t = do.astype(F32) * ov
            d0 = jnp.sum(jnp.where(lo, t, 0.0), axis=1, keepdims=True)
            d1 = jnp.sum(jnp.where(lo, 0.0, t), axis=1, keepdims=True)
            dl_ref[:, sl] = jnp.where(lo, d0, d1)

    row = pl.BlockSpec((tm, D), lambda i: (i, 0))
    return pl.pallas_call(
        body, name=name, grid=(Lp // tm,),
        in_specs=[row, pl.BlockSpec((D, D), lambda i: (0, 0)), row, pl.BlockSpec((tm, D), lambda i: (i, 3)), row],
        out_specs=[row, row, row],
        out_shape=[jax.ShapeDtypeStruct((Lp, D), BF16), jax.ShapeDtypeStruct((Lp, D), BF16),
                   jax.ShapeDtypeStruct((Lp, D), F32)],
        compiler_params=_params(("parallel",)),
    )(dhb, w_out, o, proj, linv)


def _fox_attn_bwd(qn, kn, vb, c, cT, do, mshift, delta, xchg, T, D, name):
    Lp = qn.shape[0]
    P = D // LANES
    nt = Lp // T
    H = cT.shape[0]
    nx = len(xchg)

    def body(q_ref, do_ref, m_ref, dl_ref, c_ref, k_ref, v_ref, cT_ref, *rest):
        x_in, (dq_ref, dk_ref, dv_ref, dc_ref), x_out, sems = rest[:nx], rest[nx:nx + 4], rest[nx + 4:2 * nx + 4], rest[2 * nx + 4:]
        p = pl.program_id(0)
        i = pl.program_id(1)

        @pl.when((p == 0) & (i == 0))
        def _():
            for cp in _xchg_copies(x_in, x_out, [True] * nx, sems):
                cp.start()

        @pl.when(i == 0)
        def _():
            dk_ref[...] = jnp.zeros_like(dk_ref)
            dv_ref[...] = jnp.zeros_like(dv_ref)
            dc_ref[...] = jnp.zeros_like(dc_ref)

        lo = _lane_lo()
        q = q_ref[...]
        do = do_ref[...]
        zero = jnp.zeros_like(q)
        qh = (jnp.where(lo, q, zero), jnp.where(lo, zero, q))
        doh = (jnp.where(lo, do, zero), jnp.where(lo, zero, do))
        msh = _split_halves(m_ref[...])
        dlt = _split_halves(dl_ref[...])
        cblk = c_ref[...]
        shift = tuple(_pick_col(cblk, 2 * p + h) - msh[h] for h in (0, 1))

        def step(j, carry, masked):
            k0 = pl.multiple_of(j * T, LANES)
            kj = k_ref[pl.ds(k0, T), :]
            vj = v_ref[pl.ds(k0, T), :]
            mask = _fox_mask(i, k0, T) if masked else None
            dqs, dks, dvs = [], [], []
            for h in (0, 1):
                ck = cT_ref[pl.ds(2 * p + h, 1), pl.ds(k0, T)]
                t = lax.dot_general(qh[h], kj, (((1,), (1,)), ((), ())), preferred_element_type=F32) - ck
                if masked:
                    t = jnp.where(mask, t, NEG)
                pb = jnp.exp(t + shift[h]).astype(BF16)
                dp = lax.dot_general(doh[h], vj, (((1,), (1,)), ((), ())), preferred_element_type=F32)
                ds = pb.astype(F32) * (dp - dlt[h])
                dsb = ds.astype(BF16)
                dqs.append(carry[h] + jnp.dot(dsb, kj, preferred_element_type=F32))
                dks.append(lax.dot_general(dsb, q, (((0,), (0,)), ((), ())), preferred_element_type=F32))
                dvs.append(lax.dot_general(pb, do, (((0,), (0,)), ((), ())), preferred_element_type=F32))
                dc_ref[0, h:h + 1, pl.ds(k0, T)] += -jnp.sum(ds, axis=0, keepdims=True)
            dk_ref[pl.ds(k0, T), :] += jnp.where(lo, dks[0], dks[1])
            dv_ref[pl.ds(k0, T), :] += jnp.where(lo, dvs[0], dvs[1])
            return tuple(dqs)

        init = (jnp.zeros((T, LANES), F32), jnp.zeros((T, LANES), F32))
        carry = lax.fori_loop(0, i, lambda j, cr: step(j, cr, False), init)
        dq0, dq1 = step(i, carry, True)
        dq_ref[...] = jnp.where(lo, dq0, dq1)

        @pl.when((p == P - 1) & (i == nt - 1))
        def _():
            for cp in _xchg_copies(x_in, x_out, [True] * nx, sems):
                cp.wait()

    tile = pl.BlockSpec((T, LANES), lambda p, i: (i, p))
    full = pl.BlockSpec((Lp, LANES), lambda p, i: (0, p))
    HBM = pl.BlockSpec(memory_space=pltpu.HBM)
    return pl.pallas_call(
        body, name=name, grid=(P, nt),
        in_specs=[tile, tile, tile, tile, pl.BlockSpec((T, LANES), lambda p, i: (i, 0)), full, full,
                  pl.BlockSpec((H, Lp), lambda p, i: (0, 0))] + [HBM] * nx,
        out_specs=[tile, full, full, pl.BlockSpec((1, 8, Lp), lambda p, i: (p, 0, 0))] + [HBM] * nx,
        out_shape=[jax.ShapeDtypeStruct((Lp, D), F32)] * 3 + [jax.ShapeDtypeStruct((P, 8, Lp), F32)]
        + _xchg_out_shapes(xchg, [True] * nx),
        scratch_shapes=_xchg_sems(nx),
        compiler_params=_params(("arbitrary", "arbitrary")),
    )(qn, do, mshift, delta, c, kn, vb, cT, *xchg)


def _scan_rows(x, reverse):
    C = x.shape[0]
    row = lax.broadcasted_iota(jnp.int32, (C, 1), 0)
    step = 1
    while step < C:
        if reverse:
            x = x + jnp.where(row < C - step, pltpu.roll(x, C - step, 0), 0.0)
        else:
            x = x + jnp.where(row >= step, pltpu.roll(x, step, 0), 0.0)
        step *= 2
    return x


@jax.custom_vjp
def _cumsum_rows(x):
    return _scan_rows(x, False)


_cumsum_rows.defvjp(lambda x: (_scan_rows(x, False), None), lambda _, g: (_scan_rows(g, True),))


def _hgrn_chunk(St, qr, z, vi, go, p0, p1, gg):
    C = qr.shape[0]
    lb = jax.nn.sigmoid(p1 - p0)
    a = jnp.log(lb)
    cc = jnp.log1p(-lb) + _log_sigmoid(z)
    log_f = jnp.maximum(a, cc) + jnp.log1p(jnp.exp(-jnp.abs(a - cc)))
    k = (1.0 - lb) * jax.nn.sigmoid(-z)
    q = qr * jax.nn.sigmoid(qr)
    row = lax.broadcasted_iota(jnp.int32, (C, C), 0)
    col = lax.broadcasted_iota(jnp.int32, (C, C), 1)
    causal = col <= row
    b = _cumsum_rows(log_f)
    mid = lax.broadcasted_iota(jnp.int32, (C, 1), 0) == C // 2 - 1
    r = jnp.sum(jnp.where(mid, b, 0.0), axis=0, keepdims=True)
    b_last = jnp.sum(log_f, axis=0, keepdims=True)
    attn = jnp.where(causal, _d_nt(q * jnp.exp(b - r), k * jnp.exp(r - b)), 0.0)
    o = _d_nn(attn, vi) + _d_nt(q * jnp.exp(b), St)
    St_new = St * jnp.exp(b_last) + _d_tn(vi, k * jnp.exp(b_last - b))
    og = _rms(o, gg) * (go * jax.nn.sigmoid(go))
    return St_new, og


def _hgrn_heads_per_step(H):
    return 8 if H % 8 == 0 else 4 if H % 4 == 0 else 1


def _hgrn_specs(T, W, nhb, rev_nt=None):
    if rev_nt is None:
        return [pl.BlockSpec((T, W), functools.partial(lambda hb, t, g: (t, g * nhb + hb), g=g)) for g in range(4)]
    return [pl.BlockSpec((T, W), functools.partial(lambda hb, t, g: (rev_nt - 1 - t, g * nhb + hb), g=g))
            for g in range(4)]


def _hgrn_fwd(proj, lbp, gg, T, name):
    Lp = proj.shape[0]
    D = proj.shape[1] // 4
    H = D // LANES
    hps = _hgrn_heads_per_step(H)
    W = hps * LANES
    nhb = H // hps
    nt = Lp // T
    ncc = T // HGRN_CHUNK

    def body(q_ref, z_ref, i_ref, go_ref, p_ref, gg_ref, og_ref, ss_ref, st_ref):
        @pl.when(pl.program_id(1) == 0)
        def _():
            st_ref[...] = jnp.zeros_like(st_ref)

        gain = gg_ref[...]

        def chunk(cidx, states):
            sl = pl.ds(pl.multiple_of(cidx * HGRN_CHUNK, HGRN_CHUNK), HGRN_CHUNK)
            new = []
            for hh in range(hps):
                ln = slice(hh * LANES, (hh + 1) * LANES)
                ss_ref[hh, cidx] = states[hh]
                St_new, og = _hgrn_chunk(states[hh], q_ref[sl, ln], z_ref[sl, ln], i_ref[sl, ln], go_ref[sl, ln],
                                         p_ref[0:1, ln], p_ref[1:2, ln], gain)
                og_ref[sl, ln] = og.astype(BF16)
                new.append(St_new)
            return tuple(new)

        states = lax.fori_loop(0, ncc, chunk, tuple(st_ref[hh] for hh in range(hps)))
        for hh in range(hps):
            st_ref[hh] = states[hh]

    return pl.pallas_call(
        body, name=name, grid=(nhb, nt),
        in_specs=_hgrn_specs(T, W, nhb) + [pl.BlockSpec((2, W), lambda hb, t: (0, hb)),
                                           pl.BlockSpec((1, LANES), lambda hb, t: (0, 0))],
        out_specs=[pl.BlockSpec((T, W), lambda hb, t: (t, hb)),
                   pl.BlockSpec((hps, ncc, LANES, LANES), lambda hb, t: (hb, t, 0, 0))],
        out_shape=[jax.ShapeDtypeStruct((Lp, D), BF16),
                   jax.ShapeDtypeStruct((H, Lp // HGRN_CHUNK, LANES, LANES), F32)],
        scratch_shapes=[pltpu.VMEM((hps, LANES, LANES), F32)],
        compiler_params=_params(("parallel", "arbitrary")),
    )(proj, proj, proj, proj, lbp, gg)


def _hgrn_bwd(proj, lbp, gg, dog, ss, T, name):
    Lp = proj.shape[0]
    D = proj.shape[1] // 4
    H = D // LANES
    hps = _hgrn_heads_per_step(H)
    W = hps * LANES
    nhb = H // hps
    assert nhb == 1, "d proj is written as whole rows: every head in one grid step"
    nt = Lp // T
    ncc = T // HGRN_CHUNK

    def body(q_ref, z_ref, i_ref, go_ref, p_ref, gg_ref, dog_ref, ss_ref, dproj_ref, dp_ref, dgg_ref, dst_ref):
        hb = pl.program_id(0)
        t = pl.program_id(1)

        @pl.when(t == 0)
        def _():
            dst_ref[...] = jnp.zeros_like(dst_ref)
            dp_ref[...] = jnp.zeros_like(dp_ref)

        @pl.when((t == 0) & (hb == 0))
        def _():
            dgg_ref[...] = jnp.zeros_like(dgg_ref)

        gain = gg_ref[...]
        row0 = (nt - 1 - t) * T

        def chunk(cc, carry):
            dstates, dps, dgain_sum = carry
            cidx = ncc - 1 - cc
            r0 = pl.multiple_of(cidx * HGRN_CHUNK, HGRN_CHUNK)
            sl = pl.ds(r0, HGRN_CHUNK)
            real = (row0 + r0 + lax.broadcasted_iota(jnp.int32, (HGRN_CHUNK, 1), 0)) >= N_PAD
            new_d, new_p = [], []
            for hh in range(hps):
                ln = slice(hh * LANES, (hh + 1) * LANES)
                _, vjp = jax.vjp(_hgrn_chunk, ss_ref[hh, cidx], q_ref[sl, ln], z_ref[sl, ln], i_ref[sl, ln],
                                 go_ref[sl, ln], p_ref[0:1, ln], p_ref[1:2, ln], gain)
                dSt, dq, dz, di, dgo, dp0, dp1, dgain = vjp((dstates[hh], dog_ref[sl, ln]))
                for grp, dval in enumerate((dq, dz, di, dgo)):
                    dproj_ref[sl, grp * D + hh * LANES:grp * D + (hh + 1) * LANES] = (
                        jnp.where(real, dval, 0.0).astype(BF16))
                new_d.append(dSt)
                new_p.append((dps[hh][0] + dp0, dps[hh][1] + dp1))
                dgain_sum = dgain_sum + dgain
            return tuple(new_d), tuple(new_p), dgain_sum

        zero_row = jnp.zeros((1, LANES), F32)
        init = (tuple(dst_ref[hh] for hh in range(hps)), tuple((zero_row, zero_row) for _ in range(hps)), zero_row)
        dstates, dps, dgain_sum = lax.fori_loop(0, ncc, chunk, init)
        for hh in range(hps):
            ln = slice(hh * LANES, (hh + 1) * LANES)
            dst_ref[hh] = dstates[hh]
            dp_ref[0:1, ln] += dps[hh][0]
            dp_ref[1:2, ln] += dps[hh][1]
        dgg_ref[0:1, :] += dgain_sum

    rev = pl.BlockSpec((T, W), lambda hb, t: (nt - 1 - t, hb))
    return pl.pallas_call(
        body, name=name, grid=(nhb, nt),
        in_specs=_hgrn_specs(T, W, nhb, nt) + [pl.BlockSpec((2, W), lambda hb, t: (0, hb)),
                                               pl.BlockSpec((1, LANES), lambda hb, t: (0, 0)), rev,
                                               pl.BlockSpec((hps, ncc, LANES, LANES),
                                                            lambda hb, t: (hb, nt - 1 - t, 0, 0))],
        out_specs=[pl.BlockSpec((T, 4 * D), lambda hb, t: (nt - 1 - t, 0)), pl.BlockSpec((8, W), lambda hb, t: (0, hb)),
                   pl.BlockSpec((8, LANES), lambda hb, t: (0, 0))],
        out_shape=[jax.ShapeDtypeStruct((Lp, 4 * D), BF16), jax.ShapeDtypeStruct((8, D), F32),
                   jax.ShapeDtypeStruct((8, LANES), F32)],
        scratch_shapes=[pltpu.VMEM((hps, LANES, LANES), F32)],
        compiler_params=_params(("arbitrary", "arbitrary")),
    )(proj, proj, proj, proj, lbp, gg, dog, ss)


def _xchg_sems(n_arr):
    return [pltpu.SemaphoreType.DMA((n_arr * (N_DEV - 1),)), pltpu.SemaphoreType.DMA((n_arr * (N_DEV - 1),)),
            pltpu.SemaphoreType.DMA((n_arr,))]


def _xchg_copies(ins, outs, per_peer, sems):
    send_sems, recv_sems, local_sems = sems
    x, y, c = lax.axis_index("x"), lax.axis_index("y"), lax.axis_index("c")
    me = 4 * x + 2 * y + c
    copies = []
    for n in range(len(ins)):
        src = ins[n].at[me] if per_peer[n] else ins[n]
        copies.append(pltpu.make_async_copy(src, outs[n].at[me], local_sems.at[n]))
    for rel in range(1, N_DEV):
        fx, fy, fc = (rel >> 2) & 1, (rel >> 1) & 1, rel & 1
        px = 1 - x if fx else x
        py = 1 - y if fy else y
        pc = 1 - c if fc else c
        peer = 4 * px + 2 * py + pc
        for n in range(len(ins)):
            src = ins[n].at[peer] if per_peer[n] else ins[n]
            copies.append(pltpu.make_async_remote_copy(
                src_ref=src, dst_ref=outs[n].at[me],
                send_sem=send_sems.at[n * (N_DEV - 1) + rel - 1],
                recv_sem=recv_sems.at[n * (N_DEV - 1) + rel - 1],
                device_id=(px, py, pc), device_id_type=pl.DeviceIdType.MESH))
    return copies


def _xchg_out_shapes(arrays, per_peer):
    return [jax.ShapeDtypeStruct(a.shape if pp else (N_DEV,) + a.shape, a.dtype) for a, pp in zip(arrays, per_peer)]


def _exchange(arrays, per_peer, name):
    n_arr = len(arrays)
    HBM = pl.BlockSpec(memory_space=pltpu.HBM)

    def body(*refs):
        copies = _xchg_copies(refs[:n_arr], refs[n_arr:2 * n_arr], per_peer, refs[2 * n_arr:])
        for cp in copies:
            cp.start()
        for cp in copies:
            cp.wait()

    return pl.pallas_call(
        body, name=name,
        in_specs=[HBM] * n_arr, out_specs=[HBM] * n_arr, out_shape=_xchg_out_shapes(arrays, per_peer),
        scratch_shapes=_xchg_sems(n_arr),
    )(*arrays)


ADAMW_VMEM_BUDGET = 36 * 1024 * 1024


def _adamw(recv, w, m, v, name):
    shape = w.shape
    C = shape[-1]
    R = math.prod(shape[:-1])
    lanes = -(-C // LANES) * LANES
    row_bytes = 2 * lanes * (N_DEV * recv.dtype.itemsize + 7 * 4)
    rc = _row_chunk(R, max(16, ADAMW_VMEM_BUDGET // row_bytes), 16 if recv.dtype == BF16 else 8)

    def body(r_ref, w_ref, m_ref, v_ref, g_ref, d_ref, mo_ref, vo_ref):
        g = r_ref[0].astype(F32)
        for s in range(1, N_DEV):
            g = g + r_ref[s].astype(F32)
        mn = ADAM_B1 * m_ref[...] + (1.0 - ADAM_B1) * g
        vn = ADAM_B2 * v_ref[...] + (1.0 - ADAM_B2) * (g * g)
        m_hat = mn / (1.0 - ADAM_B1 ** ADAM_STEP)
        v_hat = vn / (1.0 - ADAM_B2 ** ADAM_STEP)
        g_ref[...] = g
        d_ref[...] = -ADAM_LR * (m_hat / (jnp.sqrt(v_hat) + ADAM_EPS) + ADAM_WD * w_ref[...])
        mo_ref[...] = mn
        vo_ref[...] = vn

    row = pl.BlockSpec((rc, C), lambda i: (i, 0))
    outs = pl.pallas_call(
        body, name=name, grid=(R // rc,),
        in_specs=[pl.BlockSpec((N_DEV, rc, C), lambda i: (0, i, 0)), row, row, row],
        out_specs=[row] * 4,
        out_shape=[jax.ShapeDtypeStruct((R, C), F32)] * 4,
        compiler_params=_params(("parallel",)),
    )(recv.reshape(N_DEV, R, C), w.reshape(R, C), m.reshape(R, C), v.reshape(R, C))
    return [o.reshape(shape) for o in outs]


def _gathered_to_full(g, name):
    if name in COL_SHARDED:
        g = jnp.moveaxis(g, 0, -2)
        return g.reshape(g.shape[:-2] + (g.shape[-2] * g.shape[-1],))
    g = jnp.moveaxis(g, 0, -3)
    return g.reshape(g.shape[:-3] + (g.shape[-3] * g.shape[-2], g.shape[-1]))


def _full_to_slabs(full, name):
    if name in COL_SHARDED:
        f = full.reshape(full.shape[:-1] + (N_DEV, full.shape[-1] // N_DEV))
        return jnp.moveaxis(f, -2, 0)
    f = full.reshape(full.shape[:-2] + (N_DEV, full.shape[-2] // N_DEV, full.shape[-1]))
    return jnp.moveaxis(f, -3, 0)


def _pack_small(arrs):
    rows = []
    for a in arrs:
        flat = a.astype(F32).reshape(-1)
        pad = (-flat.shape[0]) % LANES
        rows.append(jnp.pad(flat, (0, pad)).reshape(-1, LANES))
    p = jnp.concatenate(rows, axis=0)
    return jnp.pad(p, ((0, (-p.shape[0]) % 8), (0, 0)))


def _unpack_small(packed, shapes):
    out, off = [], 0
    for shp in shapes:
        n = math.prod(shp)
        nr = -(-n // LANES)
        out.append(packed[off:off + nr].reshape(-1)[:n].reshape(shp))
        off += nr
    return out


def _local_step(x, target, meta, w_fox_in, late, small):
    S, D = x.shape
    Lp = S + HEAD_ROWS
    T = ROW_TILE if Lp % ROW_TILE == 0 else HEAD_ROWS
    P = D // LANES
    row = lambda v: v.reshape(1, -1).astype(F32)

    w_fin = jnp.pad(w_fox_in[0], ((0, 0), (0, LANES - w_fox_in.shape[-1] % LANES)))
    n_heads = w_fox_in.shape[-1] - 4 * D
    bf = jnp.pad(row(small["fox_b_f"]), ((0, 0), (0, LANES - small["fox_b_f"].size)))
    qg = jnp.tile(row(small["fox_q_norm"]), (1, 2))
    kg = jnp.tile(row(small["fox_k_norm"]), (1, 2))

    h0 = jnp.concatenate([jnp.zeros((N_PAD, D), F32), meta, x], axis=0)

    hn0, hn0t = _rms_fwd(h0, row(small["attn_norm"][0]), T, "rms0_fwd")
    proj0 = _mm(hn0, w_fin, "nn", F32, "fox_in_fwd")
    qn, kn, vb, c = _fox_prep_fwd(proj0, bf, qg, kg, T, D, "fox_prep_fwd")
    cT = c.T[:2 * P].at[:, :N_PAD].set(PAD_SHIFT)
    o, og0, mshift, linv, *gathered = _fox_attn_fwd(qn, kn, vb, c, cT, proj0, [late[n] for n in LATE], T, D,
                                                    "fox_attn_fwd")
    wl = {n: _gathered_to_full(g, n) for n, g in zip(LATE, gathered)}
    w_fout, w_hin, w_hout = wl["fox_w_out"][0], wl["hgrn_w_in"][0], wl["hgrn_w_out"][0]
    w_uin, w_uout = wl["ffn_w_in"], wl["ffn_w_out"]
    h1, hf0, hf0t = _out_proj_fwd(og0, w_fout, h0, row(small["ffn_norm"][0]), "fox_out_fwd")
    gu0 = _ffn_in_fwd(hf0, w_uin[0], "ffn0_in_fwd")
    act0 = gu0[2]
    h2, hn1, hn1t = _out_proj_fwd(act0, w_uout[0], h1, row(small["attn_norm"][1]), "ffn0_out_fwd")
    proj1 = _mm(hn1, w_hin, "nn", F32, "hgrn_in_fwd")
    lbp = small["hgrn_lower_bounds"].astype(F32)
    ggn = row(small["hgrn_g_norm"])
    Th = HGRN_TILE if Lp % HGRN_TILE == 0 else HEAD_ROWS
    og1, ss = _hgrn_fwd(proj1, lbp, ggn, Th, "hgrn_fwd")
    h3, hf1, hf1t = _out_proj_fwd(og1, w_hout, h2, row(small["ffn_norm"][1]), "hgrn_out_fwd")
    gu1 = _ffn_in_fwd(hf1, w_uin[1], "ffn1_in_fwd")
    act1 = gu1[2]
    h4 = _mm(act1, w_uout[1], "nn", F32, "ffn1_out_fwd", res=h3)
    loss_blk, dh4, dh4b, d_final = _final_loss(h4, row(small["final_norm"]), target, "final_loss")

    grads = {}

    def ffn_bwd(i, dh, dhb, h_in, hft, gu):
        grads_out = _mm(gu[3], dhb, "nn", F32, f"ffn{i}_out_dw", tm=_tile(gu[3].shape[0], 1408), tk=_tile(Lp, 1408))
        dgu = _ffn_out_dx(dhb, w_uout[i], gu[0], gu[1], f"ffn{i}_out_dx")
        grads_in = _mm(hft, dgu, "nn", F32, f"ffn{i}_in_dw", tm=D, tk=_tile(Lp, 1408))
        dh_new, dh_newb, dgain = _in_proj_dx(dgu, w_uin[i], h_in, row(small["ffn_norm"][i]), dh, f"ffn{i}_in_dx")
        return dh_new, dh_newb, grads_in, grads_out, dgain

    dh3, dh3b, g_uin1, g_uout1, d_fn1 = ffn_bwd(1, dh4, dh4b, h3, hf1t, gu1)
    grads["hgrn_w_out"] = _mm(og1, dh3b, "tn", F32, "hgrn_out_dw")[None]
    dog1 = _mm(dh3b, w_hout, "nt", F32, "hgrn_out_dx")
    dproj1, d_lb, d_gg = _hgrn_bwd(proj1, lbp, ggn, dog1, ss, Th, "hgrn_bwd")
    grads["hgrn_w_in"] = _mm(hn1t, dproj1, "nn", F32, "hgrn_in_dw", tm=D, tk=_tile(Lp, 1408))[None]
    dh2, dh2b, d_an1 = _in_proj_dx(dproj1, w_hin, h2, row(small["attn_norm"][1]), dh3, "hgrn_in_dx")
    dh1, dh1b, g_uin0, g_uout0, d_fn0 = ffn_bwd(0, dh2, dh2b, h1, hf0t, gu0)
    grads["ffn_w_in"] = jnp.stack([g_uin0, g_uin1])
    grads["ffn_w_out"] = jnp.stack([g_uout0, g_uout1])
    grads["fox_w_out"] = _mm(og0, dh1b, "tn", F32, "fox_out_dw")[None]
    do, dgate, delta = _fox_out_dx(dh1b, w_fout, o, proj0, linv, D, "fox_out_dx")
    slabs = [_full_to_slabs(grads[n], n).astype(BF16) for n in LATE]
    dqn, dkn, dv, dcr, *recv = _fox_attn_bwd(qn, kn, vb, c, cT, do, mshift, delta, slabs, T, D, "fox_attn_bwd")
    for n in LATE:
        del grads[n]
    dc = jnp.pad(dcr[:, :2, :].reshape(2 * P, Lp).T, ((0, 0), (0, LANES - 2 * P)))
    Tp = T // 2 if T == ROW_TILE else T
    dproj0, sm = _fox_prep_bwd(proj0, bf, qg, kg, dqn, dkn, dv, dgate, dc, Tp, D, "fox_prep_bwd")
    g_fin = _mm(hn0t, dproj0, "nn", F32, "fox_in_dw", tm=D, tk=_tile(Lp, 1408))[:, :4 * D + n_heads][None]
    dh0, _, d_an0, r_fin = _in_proj_dx(dproj0, w_fin, h0, row(small["attn_norm"][0]), dh1, "fox_in_dx",
                                       xchg=[_full_to_slabs(g_fin, "fox_w_in").astype(BF16)])

    grads["meta_tokens"] = dh0[N_PAD:HEAD_ROWS]
    grads["attn_norm"] = jnp.concatenate([d_an0, d_an1], axis=0)
    grads["ffn_norm"] = jnp.concatenate([d_fn0, d_fn1], axis=0)
    grads["final_norm"] = d_final[0]
    grads["fox_b_f"] = sm[0:1, :n_heads]
    grads["fox_q_norm"] = sm[1:2, :FOX_DH] + sm[1:2, FOX_DH:]
    grads["fox_k_norm"] = sm[2:3, :FOX_DH] + sm[2:3, FOX_DH:]
    grads["hgrn_lower_bounds"] = d_lb[0:2]
    grads["hgrn_g_norm"] = d_gg[0:1]
    return loss_blk[0, 0], dh0[HEAD_ROWS:], grads, dict(zip(LATE, recv), fox_w_in=r_fin)


def kernel(x, meta_tokens, attn_norm, ffn_norm, final_norm, fox_w_in, fox_b_f, fox_q_norm, fox_k_norm, fox_w_out, hgrn_w_in, hgrn_lower_bounds, hgrn_g_norm, hgrn_w_out, ffn_w_in, ffn_w_out, loss_target, m_meta_tokens, m_attn_norm, m_ffn_norm, m_final_norm, m_fox_w_in, m_fox_b_f, m_fox_q_norm, m_fox_k_norm, m_fox_w_out, m_hgrn_w_in, m_hgrn_lower_bounds, m_hgrn_g_norm, m_hgrn_w_out, m_ffn_w_in, m_ffn_w_out, v_meta_tokens, v_attn_norm, v_ffn_norm, v_final_norm, v_fox_w_in, v_fox_b_f, v_fox_q_norm, v_fox_k_norm, v_fox_w_out, v_hgrn_w_in, v_hgrn_lower_bounds, v_hgrn_g_norm, v_hgrn_w_out, v_ffn_w_in, v_ffn_w_out):
    w = dict(meta_tokens=meta_tokens, attn_norm=attn_norm, ffn_norm=ffn_norm, final_norm=final_norm,
             fox_w_in=fox_w_in, fox_b_f=fox_b_f, fox_q_norm=fox_q_norm, fox_k_norm=fox_k_norm,
             fox_w_out=fox_w_out, hgrn_w_in=hgrn_w_in, hgrn_lower_bounds=hgrn_lower_bounds,
             hgrn_g_norm=hgrn_g_norm, hgrn_w_out=hgrn_w_out, ffn_w_in=ffn_w_in, ffn_w_out=ffn_w_out)
    m = dict(meta_tokens=m_meta_tokens, attn_norm=m_attn_norm, ffn_norm=m_ffn_norm, final_norm=m_final_norm,
             fox_w_in=m_fox_w_in, fox_b_f=m_fox_b_f, fox_q_norm=m_fox_q_norm, fox_k_norm=m_fox_k_norm,
             fox_w_out=m_fox_w_out, hgrn_w_in=m_hgrn_w_in, hgrn_lower_bounds=m_hgrn_lower_bounds,
             hgrn_g_norm=m_hgrn_g_norm, hgrn_w_out=m_hgrn_w_out, ffn_w_in=m_ffn_w_in, ffn_w_out=m_ffn_w_out)
    v = dict(meta_tokens=v_meta_tokens, attn_norm=v_attn_norm, ffn_norm=v_ffn_norm, final_norm=v_final_norm,
             fox_w_in=v_fox_w_in, fox_b_f=v_fox_b_f, fox_q_norm=v_fox_q_norm, fox_k_norm=v_fox_k_norm,
             fox_w_out=v_fox_w_out, hgrn_w_in=v_hgrn_w_in, hgrn_lower_bounds=v_hgrn_lower_bounds,
             hgrn_g_norm=v_hgrn_g_norm, hgrn_w_out=v_hgrn_w_out, ffn_w_in=v_ffn_w_in, ffn_w_out=v_ffn_w_out)
    axes = ("x", "y", "c")
    small_shapes = [w[n].shape for n in SMALL]

    g_meta, g_fin = _exchange([w["meta_tokens"].astype(F32), w["fox_w_in"].astype(BF16)], [False] * 2,
                              "gather_weights")
    loss_local, grad_x, grads, recv = _local_step(
        x[0], loss_target[0], _gathered_to_full(g_meta, "meta_tokens"), _gathered_to_full(g_fin, "fox_w_in"),
        {n: w[n].astype(BF16) for n in LATE}, {n: w[n] for n in SMALL})
    loss = lax.psum(loss_local, axes)

    r_meta, r_small = _exchange([_full_to_slabs(grads["meta_tokens"], "meta_tokens"),
                                 _pack_small([grads[n] for n in SMALL])], [True, False], "scatter_grads")
    recv.update(meta_tokens=r_meta)

    res = {n: _adamw(recv[n], w[n], m[n], v[n], "adamw_" + n) for n in BIG}
    sml = _adamw(r_small, _pack_small([w[n] for n in SMALL]), _pack_small([m[n] for n in SMALL]),
                 _pack_small([v[n] for n in SMALL]), "adamw_small")
    outs = []
    for k in range(4):
        d = {n: res[n][k] for n in BIG}
        d.update(zip(SMALL, _unpack_small(sml[k], small_shapes)))
        outs.extend(d[n] for n in WEIGHTS)
    return (loss, grad_x[None], *outs)
```

```python
import functools
import math

import jax
import jax.numpy as jnp
from jax import lax
from jax.experimental import pallas as pl
from jax.experimental.pallas import tpu as pltpu

F32 = jnp.float32
BF16 = jnp.bfloat16
EPS = 1e-6
N_META = 16
LANES = 128
HEAD_ROWS = 256
ROW_TILE = 768
N_PAD = HEAD_ROWS - N_META
FOX_DH = 64
HGRN_CHUNK = 64
HGRN_TILE = 384
N_DEV = 8
NEG = -1e30
PAD_SHIFT = 1e4
VMEM_LIMIT = 56 * 1024 * 1024
HI = lax.Precision.HIGHEST
LOG2E = 1.0 / math.log(2.0)
LN2 = math.log(2.0)

ADAM_LR = 0.001
ADAM_B1 = 0.9
ADAM_B2 = 0.999
ADAM_EPS = 1e-08
ADAM_WD = 0.01
ADAM_STEP = 10

BIG = ("meta_tokens", "fox_w_in", "fox_w_out", "hgrn_w_in", "hgrn_w_out", "ffn_w_in", "ffn_w_out")
SMALL = ("attn_norm", "ffn_norm", "final_norm", "fox_b_f", "fox_q_norm", "fox_k_norm",
         "hgrn_lower_bounds", "hgrn_g_norm")
WEIGHTS = ("meta_tokens", "attn_norm", "ffn_norm", "final_norm", "fox_w_in", "fox_b_f", "fox_q_norm",
           "fox_k_norm", "fox_w_out", "hgrn_w_in", "hgrn_lower_bounds", "hgrn_g_norm", "hgrn_w_out",
           "ffn_w_in", "ffn_w_out")
COL_SHARDED = ("meta_tokens", "fox_w_in", "hgrn_w_in", "ffn_w_in")
LATE = ("fox_w_out", "hgrn_w_in", "hgrn_w_out", "ffn_w_in", "ffn_w_out")


def _params(sem=None):
    return pltpu.CompilerParams(dimension_semantics=sem, vmem_limit_bytes=VMEM_LIMIT)


def _tile(n, cap):
    best = None
    for t in range(LANES, min(n, cap) + 1, LANES):
        if n % t == 0:
            best = t
    assert best is not None, (n, cap)
    return best


def _row_chunk(n, cap, mult=8):
    best = n
    for t in range(mult, min(n, cap) + 1, mult):
        if n % t == 0:
            best = t
    return best


def _dg(a, b, ca, cb):
    return lax.dot_general(a.astype(BF16), b.astype(BF16), (((ca,), (cb,)), ((), ())),
                           preferred_element_type=F32)


@jax.custom_vjp
def _d_nn(a, b):
    return _dg(a, b, 1, 0)


@jax.custom_vjp
def _d_nt(a, b):
    return _dg(a, b, 1, 1)


@jax.custom_vjp
def _d_tn(a, b):
    return _dg(a, b, 0, 0)


_d_nn.defvjp(lambda a, b: (_d_nn(a, b), (a, b)), lambda r, g: (_d_nt(g, r[1]), _d_tn(r[0], g)))
_d_nt.defvjp(lambda a, b: (_d_nt(a, b), (a, b)), lambda r, g: (_d_nn(g, r[1]), _d_tn(g, r[0])))
_d_tn.defvjp(lambda a, b: (_d_tn(a, b), (a, b)), lambda r, g: (_d_nt(r[1], g), _d_nn(r[0], g)))


def _log_sigmoid(x):
    return jnp.minimum(x, 0.0) - jnp.log1p(jnp.exp(-jnp.abs(x)))


def _rms(x, g):
    return x * lax.rsqrt(jnp.mean(x * x, axis=-1, keepdims=True) + EPS) * g


def _mm(a, b, mode, out_dtype, name, res=None, tm=None, tn=None, tk=None):
    assert a.dtype == BF16 and b.dtype == BF16, (name, a.dtype, b.dtype)
    if mode == "nn":
        (M, K), N = a.shape, b.shape[1]
    elif mode == "nt":
        (M, K), N = a.shape, b.shape[0]
    else:
        (K, M), N = a.shape, b.shape[1]
    if mode == "nn":
        tm, tn, tk = tm or _tile(M, ROW_TILE), tn or _tile(N, 1408), tk or _tile(K, 2816)
    elif mode == "nt":
        tm, tn, tk = tm or _tile(M, ROW_TILE if K <= 2048 else ROW_TILE // 2), tn or N, tk or K
    else:
        tm, tn, tk = tm or _tile(M, 1408), tn or _tile(N, 1408), tk or _tile(K, ROW_TILE)
    nk = K // tk
    if mode == "tn":
        a_spec = pl.BlockSpec((tk, tm), lambda j, i, k: (k, i))
        dims = (((0,), (0,)), ((), ()))
    else:
        a_spec = pl.BlockSpec((tm, tk), lambda j, i, k: (i, k))
        dims = (((1,), (1 if mode == "nt" else 0,)), ((), ()))
    if mode == "nt":
        b_spec = pl.BlockSpec((tn, tk), lambda j, i, k: (j, k))
    else:
        b_spec = pl.BlockSpec((tk, tn), lambda j, i, k: (k, j))

    o_spec = pl.BlockSpec((tm, tn), lambda j, i, k: (i, j))

    def body(a_ref, b_ref, *rest):
        r_ref = rest[0] if res is not None else None
        o_ref, acc_ref = rest[-2:]
        k = pl.program_id(2)

        @pl.when(k == 0)
        def _():
            acc_ref[...] = jnp.zeros_like(acc_ref)

        acc_ref[...] += lax.dot_general(a_ref[...], b_ref[...], dims, preferred_element_type=F32)

        @pl.when(k == nk - 1)
        def _():
            out = acc_ref[...] if r_ref is None else acc_ref[...] + r_ref[...]
            o_ref[...] = out.astype(out_dtype)

    return pl.pallas_call(
        body, name=name, grid=(N // tn, M // tm, nk),
        in_specs=[a_spec, b_spec] + ([o_spec] if res is not None else []),
        out_specs=o_spec,
        out_shape=jax.ShapeDtypeStruct((M, N), out_dtype),
        scratch_shapes=[pltpu.VMEM((tm, tn), F32)],
        compiler_params=_params(("parallel", "parallel", "arbitrary")),
    )(a, b, *([res] if res is not None else []))


def _out_proj_fwd(a, w, res, gain, name):
    Lp, K = a.shape
    D = w.shape[1]
    tm = _tile(Lp, ROW_TILE)

    def body(a_ref, w_ref, r_ref, g_ref, h_ref, hn_ref, hnt_ref):
        h = jnp.dot(a_ref[...], w_ref[...], preferred_element_type=F32) + r_ref[...]
        h_ref[...] = h
        hn = _rms(h, g_ref[...])
        hn_ref[...] = hn.astype(BF16)
        hnt_ref[...] = hn.T.astype(BF16)

    row = pl.BlockSpec((tm, D), lambda i: (i, 0))
    return pl.pallas_call(
        body, name=name, grid=(Lp // tm,),
        in_specs=[pl.BlockSpec((tm, K), lambda i: (i, 0)), pl.BlockSpec((K, D), lambda i: (0, 0)), row,
                  pl.BlockSpec((1, D), lambda i: (0, 0))],
        out_specs=[row, row, pl.BlockSpec((D, tm), lambda i: (0, i))],
        out_shape=[jax.ShapeDtypeStruct((Lp, D), F32), jax.ShapeDtypeStruct((Lp, D), BF16),
                   jax.ShapeDtypeStruct((D, Lp), BF16)],
        compiler_params=_params(("parallel",)),
    )(a, w, res, gain)


def _in_proj_dx(dy, w, x, gain, dres, name, xchg=()):
    Lp, N = dy.shape
    D = w.shape[0]
    tm = _tile(Lp, ROW_TILE // 2)
    nt = Lp // tm
    nx = len(xchg)

    def body(dy_ref, w_ref, x_ref, g_ref, dr_ref, *rest):
        x_in, (dx_ref, dxb_ref, dg_ref), x_out, sems = rest[:nx], rest[nx:nx + 3], rest[nx + 3:2 * nx + 3], rest[2 * nx + 3:]

        @pl.when(pl.program_id(0) == 0)
        def _():
            dg_ref[...] = jnp.zeros_like(dg_ref)
            if nx:
                for cp in _xchg_copies(x_in, x_out, [True] * nx, sems):
                    cp.start()

        dhn = lax.dot_general(dy_ref[...], w_ref[...], (((1,), (1,)), ((), ())), preferred_element_type=F32)
        _, vjp = jax.vjp(_rms, x_ref[...], g_ref[...])
        dx, dg = vjp(dhn)
        dx = dx + dr_ref[...]
        dx_ref[...] = dx
        dxb_ref[...] = dx.astype(BF16)
        dg_ref[...] += dg

        if nx:
            @pl.when(pl.program_id(0) == nt - 1)
            def _():
                for cp in _xchg_copies(x_in, x_out, [True] * nx, sems):
                    cp.wait()

    row = pl.BlockSpec((tm, D), lambda i: (i, 0))
    vec = pl.BlockSpec((1, D), lambda i: (0, 0))
    HBM = pl.BlockSpec(memory_space=pltpu.HBM)
    return pl.pallas_call(
        body, name=name, grid=(nt,),
        in_specs=[pl.BlockSpec((tm, N), lambda i: (i, 0)), pl.BlockSpec((D, N), lambda i: (0, 0)), row, vec, row]
        + [HBM] * nx,
        out_specs=[row, row, vec] + [HBM] * nx,
        out_shape=[jax.ShapeDtypeStruct((Lp, D), F32), jax.ShapeDtypeStruct((Lp, D), BF16),
                   jax.ShapeDtypeStruct((1, D), F32)] + _xchg_out_shapes(xchg, [True] * nx),
        scratch_shapes=_xchg_sems(nx) if nx else [],
        compiler_params=_params(("arbitrary",)),
    )(dy, w, x, gain, dres, *xchg)


def _rms_fwd(x, g, T, name):
    Lp, D = x.shape

    def body(x_ref, g_ref, o_ref, ot_ref):
        y = _rms(x_ref[...], g_ref[...])
        o_ref[...] = y.astype(BF16)
        ot_ref[...] = y.T.astype(BF16)

    return pl.pallas_call(
        body, name=name, grid=(Lp // T,),
        in_specs=[pl.BlockSpec((T, D), lambda i: (i, 0)), pl.BlockSpec((1, D), lambda i: (0, 0))],
        out_specs=[pl.BlockSpec((T, D), lambda i: (i, 0)), pl.BlockSpec((D, T), lambda i: (0, i))],
        out_shape=[jax.ShapeDtypeStruct((Lp, D), BF16), jax.ShapeDtypeStruct((D, Lp), BF16)],
        compiler_params=_params(("parallel",)),
    )(x, g)


def _swiglu(gate, up):
    return gate * jax.nn.sigmoid(gate) * up


def _ffn_in_fwd(hf, w_in, name):
    Lp, D = hf.shape
    F = w_in.shape[1] // 2
    tm = _tile(Lp, ROW_TILE)
    tn = _tile(F, 1408)
    nj = F // tn

    def body(a_ref, bg_ref, bu_ref, g_ref, u_ref, act_ref, actt_ref):
        a = a_ref[...]
        g = jnp.dot(a, bg_ref[...], preferred_element_type=F32)
        u = jnp.dot(a, bu_ref[...], preferred_element_type=F32)
        g_ref[...] = g.astype(BF16)
        u_ref[...] = u.astype(BF16)
        act = _swiglu(g, u)
        act_ref[...] = act.astype(BF16)
        actt_ref[...] = act.T.astype(BF16)

    tile = pl.BlockSpec((tm, tn), lambda j, i: (i, j))
    return pl.pallas_call(
        body, name=name, grid=(nj, Lp // tm),
        in_specs=[pl.BlockSpec((tm, D), lambda j, i: (i, 0)), pl.BlockSpec((D, tn), lambda j, i: (0, j)),
                  pl.BlockSpec((D, tn), lambda j, i: (0, nj + j))],
        out_specs=[tile, tile, tile, pl.BlockSpec((tn, tm), lambda j, i: (j, i))],
        out_shape=[jax.ShapeDtypeStruct((Lp, F), BF16)] * 3 + [jax.ShapeDtypeStruct((F, Lp), BF16)],
        compiler_params=_params(("parallel", "parallel")),
    )(hf, w_in, w_in)


def _ffn_out_dx(dhb, w_out, g, u, name):
    Lp, D = dhb.shape
    F = w_out.shape[0]
    tm = HEAD_ROWS

    def body(a_ref, b_ref, g_ref, u_ref, o_ref):
        dact = lax.dot_general(a_ref[...], b_ref[...], (((1,), (1,)), ((), ())), preferred_element_type=F32)
        _, vjp = jax.vjp(_swiglu, g_ref[...].astype(F32), u_ref[...].astype(F32))
        dg, du = vjp(dact)
        o_ref[:, :F] = dg.astype(BF16)
        o_ref[:, F:] = du.astype(BF16)

    wide = pl.BlockSpec((tm, F), lambda i: (i, 0))
    return pl.pallas_call(
        body, name=name, grid=(Lp // tm,),
        in_specs=[pl.BlockSpec((tm, D), lambda i: (i, 0)), pl.BlockSpec((F, D), lambda i: (0, 0)), wide, wide],
        out_specs=pl.BlockSpec((tm, 2 * F), lambda i: (i, 0)),
        out_shape=jax.ShapeDtypeStruct((Lp, 2 * F), BF16),
        compiler_params=_params(("parallel",)),
    )(dhb, w_out, g, u)


def _final_loss(h, g, target, name):
    Lp, D = h.shape
    TR = HEAD_ROWS

    def loss_fn(hh, gg, tt):
        err = _rms(hh, gg) - tt
        return 0.5 * jnp.sum(jnp.mean(err * err, axis=-1))

    def body(h_ref, g_ref, t_ref, loss_ref, dh_ref, dhb_ref, dg_ref):
        i = pl.program_id(0)

        @pl.when(i == 0)
        def _():
            loss_ref[...] = jnp.zeros_like(loss_ref)
            dg_ref[...] = jnp.zeros_like(dg_ref)
            dh_ref[...] = jnp.zeros_like(dh_ref)
            dhb_ref[...] = jnp.zeros_like(dhb_ref)

        @pl.when(i > 0)
        def _():
            val, vjp = jax.vjp(lambda hh, gg: loss_fn(hh, gg, t_ref[...]), h_ref[...], g_ref[...])
            dh, dg = vjp(jnp.ones((), F32))
            dh_ref[...] = dh
            dhb_ref[...] = dh.astype(BF16)
            dg_ref[...] += dg
            loss_ref[...] += val

    row = pl.BlockSpec((TR, D), lambda i: (i, 0))
    return pl.pallas_call(
        body, name=name, grid=(Lp // TR,),
        in_specs=[row, pl.BlockSpec((1, D), lambda i: (0, 0)),
                  pl.BlockSpec((TR, D), lambda i: (jnp.maximum(i - 1, 0), 0))],
        out_specs=[pl.BlockSpec((8, LANES), lambda i: (0, 0)), row, row, pl.BlockSpec((1, D), lambda i: (0, 0))],
        out_shape=[jax.ShapeDtypeStruct((8, LANES), F32), jax.ShapeDtypeStruct((Lp, D), F32),
                   jax.ShapeDtypeStruct((Lp, D), BF16), jax.ShapeDtypeStruct((1, D), F32)],
        compiler_params=_params(("arbitrary",)),
    )(h, g, target)


def _lane_lo():
    return lax.broadcasted_iota(jnp.int32, (1, LANES), 1) < FOX_DH


def _headnorm(x, g, scale):
    lo = _lane_lo()
    x2 = x * x
    s0 = jnp.sum(jnp.where(lo, x2, 0.0), axis=-1, keepdims=True)
    s1 = jnp.sum(jnp.where(lo, 0.0, x2), axis=-1, keepdims=True)
    r = jnp.where(lo, lax.rsqrt(s0 / FOX_DH + EPS), lax.rsqrt(s1 / FOX_DH + EPS))
    return x * r * g * scale


def _fox_prep_fwd(proj, bf, qg, kg, T, D, name):
    Lp = proj.shape[0]
    nb = D // LANES
    scale = FOX_DH ** -0.5 * LOG2E

    def body(q_ref, k_ref, v_ref, fl_ref, bf_ref, qg_ref, kg_ref, qn_ref, kn_ref, vb_ref, c_ref, carry_ref):
        @pl.when(pl.program_id(0) == 0)
        def _():
            carry_ref[...] = jnp.zeros_like(carry_ref)

        for b in range(nb):
            sl = slice(b * LANES, (b + 1) * LANES)
            qn_ref[:, sl] = _headnorm(q_ref[:, sl], qg_ref[...], scale).astype(BF16)
            kn_ref[:, sl] = _headnorm(k_ref[:, sl], kg_ref[...], 1.0).astype(BF16)
        vb_ref[...] = v_ref[...].astype(BF16)
        log_f = _log_sigmoid(fl_ref[...] + bf_ref[...])
        row = lax.broadcasted_iota(jnp.int32, (T, T), 0)
        col = lax.broadcasted_iota(jnp.int32, (T, T), 1)
        tri = (col <= row).astype(F32)
        c = jnp.dot(tri, log_f, precision=HI, preferred_element_type=F32) + carry_ref[...]
        c_ref[...] = c * LOG2E
        last = lax.broadcasted_iota(jnp.int32, (T, 1), 0) == T - 1
        carry_ref[...] = jnp.sum(jnp.where(last, c, 0.0), axis=0, keepdims=True)

    wide = lambda j: pl.BlockSpec((T, D), lambda i: (i, j))
    vec = pl.BlockSpec((1, LANES), lambda i: (0, 0))
    return pl.pallas_call(
        body, name=name, grid=(Lp // T,),
        in_specs=[wide(0), wide(1), wide(2), pl.BlockSpec((T, LANES), lambda i: (i, 4 * nb)), vec, vec, vec],
        out_specs=[wide(0), wide(0), wide(0), pl.BlockSpec((T, LANES), lambda i: (i, 0))],
        out_shape=[jax.ShapeDtypeStruct((Lp, D), BF16)] * 3 + [jax.ShapeDtypeStruct((Lp, LANES), F32)],
        scratch_shapes=[pltpu.VMEM((1, LANES), F32)],
        compiler_params=_params(("arbitrary",)),
    )(proj, proj, proj, proj, bf, qg, kg)


def _fox_prep_bwd(proj, bf, qg, kg, dqn, dkn, dv, dgate, dc, T, D, name):
    Lp = proj.shape[0]
    nb = D // LANES
    nt = Lp // T
    scale = FOX_DH ** -0.5 * LOG2E

    def body(q_ref, k_ref, fl_ref, bf_ref, qg_ref, kg_ref, dqn_ref, dkn_ref, dv_ref, dgate_ref, dc_ref,
             dproj_ref, sm_ref, carry_ref):
        @pl.when(pl.program_id(0) == 0)
        def _():
            carry_ref[...] = jnp.zeros_like(carry_ref)
            sm_ref[...] = jnp.zeros_like(sm_ref)

        dqg = jnp.zeros((1, LANES), F32)
        dkg = jnp.zeros((1, LANES), F32)
        for b in range(nb):
            sl = slice(b * LANES, (b + 1) * LANES)
            _, vjp = jax.vjp(lambda x, g: _headnorm(x, g, scale), q_ref[:, sl], qg_ref[...])
            dx, dg = vjp(dqn_ref[:, sl] * LN2)
            dproj_ref[:, sl] = dx.astype(BF16)
            dqg = dqg + dg
            _, vjp = jax.vjp(lambda x, g: _headnorm(x, g, 1.0), k_ref[:, sl], kg_ref[...])
            dx, dg = vjp(dkn_ref[:, sl] * LN2)
            dproj_ref[:, D + b * LANES:D + (b + 1) * LANES] = dx.astype(BF16)
            dkg = dkg + dg
        dproj_ref[:, 2 * D:3 * D] = dv_ref[...].astype(BF16)
        dproj_ref[:, 3 * D:4 * D] = dgate_ref[...]
        dcv = dc_ref[...]
        row = lax.broadcasted_iota(jnp.int32, (T, T), 0)
        col = lax.broadcasted_iota(jnp.int32, (T, T), 1)
        triu = (col >= row).astype(F32)
        dlogf = jnp.dot(triu, dcv, precision=HI, preferred_element_type=F32) + carry_ref[...]
        carry_ref[...] += jnp.sum(dcv, axis=0, keepdims=True)
        _, vjp = jax.vjp(_log_sigmoid, fl_ref[...] + bf_ref[...])
        (dfl,) = vjp(dlogf)
        dproj_ref[:, 4 * D:] = dfl.astype(BF16)
        sm_ref[0:1, :] += jnp.sum(dfl, axis=0, keepdims=True)
        sm_ref[1:2, :] += dqg
        sm_ref[2:3, :] += dkg

    wide = lambda j: pl.BlockSpec((T, D), lambda i: (nt - 1 - i, j))
    narrow = lambda j: pl.BlockSpec((T, LANES), lambda i: (nt - 1 - i, j))
    vec = pl.BlockSpec((1, LANES), lambda i: (0, 0))
    return pl.pallas_call(
        body, name=name, grid=(nt,),
        in_specs=[wide(0), wide(1), narrow(4 * nb), vec, vec, vec, wide(0), wide(0), wide(0), wide(0), narrow(0)],
        out_specs=[pl.BlockSpec((T, 4 * D + LANES), lambda i: (nt - 1 - i, 0)), pl.BlockSpec((8, LANES), lambda i: (0, 0))],
        out_shape=[jax.ShapeDtypeStruct((Lp, 4 * D + LANES), BF16), jax.ShapeDtypeStruct((8, LANES), F32)],
        scratch_shapes=[pltpu.VMEM((1, LANES), F32)],
        compiler_params=_params(("arbitrary",)),
    )(proj, proj, proj, bf, qg, kg, dqn, dkn, dv, dgate, dc)


def _fox_mask(i, k0, T):
    qpos = i * T + lax.broadcasted_iota(jnp.int32, (T, 1), 0)
    kpos = k0 + lax.broadcasted_iota(jnp.int32, (1, T), 1)
    return (kpos <= qpos) & ((kpos >= N_PAD) | (qpos < N_PAD))


def _pick_col(blk, idx):
    lane = lax.broadcasted_iota(jnp.int32, (1, LANES), 1)
    return jnp.sum(jnp.where(lane == idx, blk, 0.0), axis=1, keepdims=True)


def _split_halves(blk):
    lo = _lane_lo()
    return (jnp.max(jnp.where(lo, blk, -jnp.inf), axis=1, keepdims=True),
            jnp.max(jnp.where(lo, -jnp.inf, blk), axis=1, keepdims=True))


def _fox_attn_fwd(qn, kn, vb, c, cT, proj, xchg, T, D, name):
    Lp = qn.shape[0]
    P = D // LANES
    nt = Lp // T
    H = cT.shape[0]
    nx = len(xchg)

    def body(q_ref, k_ref, v_ref, c_ref, cT_ref, g_ref, *rest):
        x_in, (o_ref, og_ref, m_ref, li_ref), x_out, sems = rest[:nx], rest[nx:nx + 4], rest[nx + 4:2 * nx + 4], rest[2 * nx + 4:]
        p = pl.program_id(0)
        i = pl.program_id(1)

        @pl.when((p == 0) & (i == 0))
        def _():
            for cp in _xchg_copies(x_in, x_out, [False] * nx, sems):
                cp.start()

        lo = _lane_lo()
        q = q_ref[...]
        zero = jnp.zeros_like(q)
        qh = (jnp.where(lo, q, zero), jnp.where(lo, zero, q))
        cblk = c_ref[...]
        cq = tuple(_pick_col(cblk, 2 * p + h) for h in (0, 1))
        one = jnp.ones_like(q)

        def step(j, carry, masked):
            k0 = pl.multiple_of(j * T, LANES)
            kj = k_ref[pl.ds(k0, T), :]
            vj = v_ref[pl.ds(k0, T), :]
            vh = (jnp.where(lo, vj, one), jnp.where(lo, one, vj))
            mask = _fox_mask(i, k0, T) if masked else None
            out = []
            for h in (0, 1):
                m, acc = carry[h]
                ck = cT_ref[pl.ds(2 * p + h, 1), pl.ds(k0, T)]
                t = lax.dot_general(qh[h], kj, (((1,), (1,)), ((), ())), preferred_element_type=F32) - ck
                if masked:
                    t = jnp.where(mask, t, NEG)
                m_new = jnp.ceil(jnp.maximum(m, cq[h] + jnp.max(t, axis=1, keepdims=True)))
                pr = jnp.exp2(t + (cq[h] - m_new)).astype(BF16)
                acc = jnp.exp2(m - m_new) * acc + jnp.dot(pr, vh[h], preferred_element_type=F32)
                out.append((m_new, acc))
            return tuple(out)

        init = tuple((jnp.full((T, 1), NEG, F32), jnp.zeros((T, LANES), F32)) for _ in (0, 1))
        carry = lax.fori_loop(0, i, lambda j, cr: step(j, cr, False), init)
        (m0, a0), (m1, a1) = step(i, carry, True)
        l0 = pltpu.roll(a0, FOX_DH, 1)
        l1 = pltpu.roll(a1, FOX_DH, 1)
        o = jnp.where(lo, a0 / l0, a1 / l1)
        o_ref[...] = o
        m_ref[...] = jnp.where(lo, m0, m1)
        li_ref[...] = jnp.where(lo, 1.0 / l0, 1.0 / l1)
        og_ref[...] = (o * jax.nn.sigmoid(g_ref[...])).astype(BF16)

        @pl.when((p == P - 1) & (i == nt - 1))
        def _():
            for cp in _xchg_copies(x_in, x_out, [False] * nx, sems):
                cp.wait()

    tile = pl.BlockSpec((T, LANES), lambda p, i: (i, p))
    full = pl.BlockSpec((Lp, LANES), lambda p, i: (0, p))
    HBM = pl.BlockSpec(memory_space=pltpu.HBM)
    return pl.pallas_call(
        body, name=name, grid=(P, nt),
        in_specs=[tile, full, full, pl.BlockSpec((T, LANES), lambda p, i: (i, 0)),
                  pl.BlockSpec((H, Lp), lambda p, i: (0, 0)),
                  pl.BlockSpec((T, LANES), lambda p, i: (i, 3 * P + p))] + [HBM] * nx,
        out_specs=[tile, tile, tile, tile] + [HBM] * nx,
        out_shape=[jax.ShapeDtypeStruct((Lp, D), F32), jax.ShapeDtypeStruct((Lp, D), BF16),
                   jax.ShapeDtypeStruct((Lp, D), F32), jax.ShapeDtypeStruct((Lp, D), F32)]
        + _xchg_out_shapes(xchg, [False] * nx),
        scratch_shapes=_xchg_sems(nx),
        compiler_params=_params(("arbitrary", "arbitrary")),
    )(qn, kn, vb, c, cT, proj, *xchg)


def _fox_out_dx(dhb, w_out, o, proj, linv, D, name):
    Lp = o.shape[0]
    tm = HEAD_ROWS

    def body(a_ref, w_ref, o_ref, g_ref, li_ref, do_ref, dg_ref, dl_ref):
        lo = _lane_lo()
        dog_all = lax.dot_general(a_ref[...], w_ref[...], (((1,), (1,)), ((), ())), preferred_element_type=F32)
        for b in range(D // LANES):
            sl = slice(b * LANES, (b + 1) * LANES)
            dog = dog_all[:, sl]
            sig = jax.nn.sigmoid(g_ref[:, sl])
            ov = o_ref[:, sl]
            do = (dog * sig * li_ref[:, sl]).astype(BF16)
            do_ref[:, sl] = do
            dg_ref[:, sl] = (dog * ov * sig * (1.0 - sig)).astype(BF16)
            t = do.astype(F32) * ov
            d0 = jnp.sum(jnp.where(lo, t, 0.0), axis=1, keepdims=True)
            d1 = jnp.sum(jnp.where(lo, 0.0, t), axis=1, keepdims=True)
            dl_ref[:, sl] = jnp.where(lo, d0, d1)

    row = pl.BlockSpec((tm, D), lambda i: (i, 0))
    return pl.pallas_call(
        body, name=name, grid=(Lp // tm,),
        in_specs=[row, pl.BlockSpec((D, D), lambda i: (0, 0)), row, pl.BlockSpec((tm, D), lambda i: (i, 3)), row],
        out_specs=[row, row, row],
        out_shape=[jax.ShapeDtypeStruct((Lp, D), BF16), jax.ShapeDtypeStruct((Lp, D), BF16),
                   jax.ShapeDtypeStruct((Lp, D), F32)],
        compiler_params=_params(("parallel",)),
    )(dhb, w_out, o, proj, linv)


def _fox_attn_bwd(qn, kn, vb, c, cT, do, mshift, delta, xchg, T, D, name):
    Lp = qn.shape[0]
    P = D // LANES
    nt = Lp // T
    H = cT.shape[0]
    nx = len(xchg)

    def body(q_ref, do_ref, m_ref, dl_ref, c_ref, k_ref, v_ref, cT_ref, *rest):
        x_in, (dq_ref, dk_ref, dv_ref, dc_ref), x_out, sems = rest[:nx], rest[nx:nx + 4], rest[nx + 4:2 * nx + 4], rest[2 * nx + 4:]
        p = pl.program_id(0)
        i = pl.program_id(1)

        @pl.when((p == 0) & (i == 0))
        def _():
            for cp in _xchg_copies(x_in, x_out, [True] * nx, sems):
                cp.start()

        @pl.when(i == 0)
        def _():
            dk_ref[...] = jnp.zeros_like(dk_ref)
            dv_ref[...] = jnp.zeros_like(dv_ref)
            dc_ref[...] = jnp.zeros_like(dc_ref)

        lo = _lane_lo()
        q = q_ref[...]
        do = do_ref[...]
        zero = jnp.zeros_like(q)
        qh = (jnp.where(lo, q, zero), jnp.where(lo, zero, q))
        doh = (jnp.where(lo, do, zero), jnp.where(lo, zero, do))
        msh = _split_halves(m_ref[...])
        dlt = _split_halves(dl_ref[...])
        cblk = c_ref[...]
        shift = tuple(_pick_col(cblk, 2 * p + h) - msh[h] for h in (0, 1))

        def step(j, carry, masked):
            k0 = pl.multiple_of(j * T, LANES)
            kj = k_ref[pl.ds(k0, T), :]
            vj = v_ref[pl.ds(k0, T), :]
            mask = _fox_mask(i, k0, T) if masked else None
            dqs, dks, dvs = [], [], []
            for h in (0, 1):
                ck = cT_ref[pl.ds(2 * p + h, 1), pl.ds(k0, T)]
                t = lax.dot_general(qh[h], kj, (((1,), (1,)), ((), ())), preferred_element_type=F32) - ck
                if masked:
                    t = jnp.where(mask, t, NEG)
                pb = jnp.exp2(t + shift[h]).astype(BF16)
                dp = lax.dot_general(doh[h], vj, (((1,), (1,)), ((), ())), preferred_element_type=F32)
                ds = pb.astype(F32) * (dp - dlt[h])
                dsb = ds.astype(BF16)
                dqs.append(carry[h] + jnp.dot(dsb, kj, preferred_element_type=F32))
                dks.append(lax.dot_general(dsb, q, (((0,), (0,)), ((), ())), preferred_element_type=F32))
                dvs.append(lax.dot_general(pb, do, (((0,), (0,)), ((), ())), preferred_element_type=F32))
                dc_ref[0, h:h + 1, pl.ds(k0, T)] += -jnp.sum(ds, axis=0, keepdims=True)
            dk_ref[pl.ds(k0, T), :] += jnp.where(lo, dks[0], dks[1])
            dv_ref[pl.ds(k0, T), :] += jnp.where(lo, dvs[0], dvs[1])
            return tuple(dqs)

        init = (jnp.zeros((T, LANES), F32), jnp.zeros((T, LANES), F32))
        carry = lax.fori_loop(0, i, lambda j, cr: step(j, cr, False), init)
        dq0, dq1 = step(i, carry, True)
        dq_ref[...] = jnp.where(lo, dq0, dq1)

        @pl.when((p == P - 1) & (i == nt - 1))
        def _():
            for cp in _xchg_copies(x_in, x_out, [True] * nx, sems):
                cp.wait()

    tile = pl.BlockSpec((T, LANES), lambda p, i: (i, p))
    full = pl.BlockSpec((Lp, LANES), lambda p, i: (0, p))
    HBM = pl.BlockSpec(memory_space=pltpu.HBM)
    return pl.pallas_call(
        body, name=name, grid=(P, nt),
        in_specs=[tile, tile, tile, tile, pl.BlockSpec((T, LANES), lambda p, i: (i, 0)), full, full,
                  pl.BlockSpec((H, Lp), lambda p, i: (0, 0))] + [HBM] * nx,
        out_specs=[tile, full, full, pl.BlockSpec((1, 8, Lp), lambda p, i: (p, 0, 0))] + [HBM] * nx,
        out_shape=[jax.ShapeDtypeStruct((Lp, D), F32)] * 3 + [jax.ShapeDtypeStruct((P, 8, Lp), F32)]
        + _xchg_out_shapes(xchg, [True] * nx),
        scratch_shapes=_xchg_sems(nx),
        compiler_params=_params(("arbitrary", "arbitrary")),
    )(qn, do, mshift, delta, c, kn, vb, cT, *xchg)


def _scan_rows(x, reverse):
    C = x.shape[0]
    row = lax.broadcasted_iota(jnp.int32, (C, 1), 0)
    step = 1
    while step < C:
        if reverse:
            x = x + jnp.where(row < C - step, pltpu.roll(x, C - step, 0), 0.0)
        else:
            x = x + jnp.where(row >= step, pltpu.roll(x, step, 0), 0.0)
        step *= 2
    return x


@jax.custom_vjp
def _cumsum_rows(x):
    return _scan_rows(x, False)


_cumsum_rows.defvjp(lambda x: (_scan_rows(x, False), None), lambda _, g: (_scan_rows(g, True),))


def _hgrn_chunk(St, qr, z, vi, go, p0, p1, gg):
    C = qr.shape[0]
    lb = jax.nn.sigmoid(p1 - p0)
    a = jnp.log(lb)
    cc = jnp.log1p(-lb) + _log_sigmoid(z)
    log_f = jnp.maximum(a, cc) + jnp.log1p(jnp.exp(-jnp.abs(a - cc)))
    k = (1.0 - lb) * jax.nn.sigmoid(-z)
    q = qr * jax.nn.sigmoid(qr)
    row = lax.broadcasted_iota(jnp.int32, (C, C), 0)
    col = lax.broadcasted_iota(jnp.int32, (C, C), 1)
    causal = col <= row
    b = _cumsum_rows(log_f)
    mid = lax.broadcasted_iota(jnp.int32, (C, 1), 0) == C // 2 - 1
    r = jnp.sum(jnp.where(mid, b, 0.0), axis=0, keepdims=True)
    b_last = jnp.sum(log_f, axis=0, keepdims=True)
    attn = jnp.where(causal, _d_nt(q * jnp.exp(b - r), k * jnp.exp(r - b)), 0.0)
    o = _d_nn(attn, vi) + _d_nt(q * jnp.exp(b), St)
    St_new = St * jnp.exp(b_last) + _d_tn(vi, k * jnp.exp(b_last - b))
    og = _rms(o, gg) * (go * jax.nn.sigmoid(go))
    return St_new, og


def _hgrn_heads_per_step(H):
    return 8 if H % 8 == 0 else 4 if H % 4 == 0 else 1


def _hgrn_specs(T, W, nhb, rev_nt=None):
    if rev_nt is None:
        return [pl.BlockSpec((T, W), functools.partial(lambda hb, t, g: (t, g * nhb + hb), g=g)) for g in range(4)]
    return [pl.BlockSpec((T, W), functools.partial(lambda hb, t, g: (rev_nt - 1 - t, g * nhb + hb), g=g))
            for g in range(4)]


def _hgrn_fwd(proj, lbp, gg, T, name):
    Lp = proj.shape[0]
    D = proj.shape[1] // 4
    H = D // LANES
    hps = _hgrn_heads_per_step(H)
    W = hps * LANES
    nhb = H // hps
    nt = Lp // T
    ncc = T // HGRN_CHUNK

    def body(q_ref, z_ref, i_ref, go_ref, p_ref, gg_ref, og_ref, ss_ref, st_ref):
        @pl.when(pl.program_id(1) == 0)
        def _():
            st_ref[...] = jnp.zeros_like(st_ref)

        gain = gg_ref[...]

        def chunk(cidx, states):
            sl = pl.ds(pl.multiple_of(cidx * HGRN_CHUNK, HGRN_CHUNK), HGRN_CHUNK)
            new = []
            for hh in range(hps):
                ln = slice(hh * LANES, (hh + 1) * LANES)
                ss_ref[hh, cidx] = states[hh]
                St_new, og = _hgrn_chunk(states[hh], q_ref[sl, ln], z_ref[sl, ln], i_ref[sl, ln], go_ref[sl, ln],
                                         p_ref[0:1, ln], p_ref[1:2, ln], gain)
                og_ref[sl, ln] = og.astype(BF16)
                new.append(St_new)
            return tuple(new)

        states = lax.fori_loop(0, ncc, chunk, tuple(st_ref[hh] for hh in range(hps)))
        for hh in range(hps):
            st_ref[hh] = states[hh]

    return pl.pallas_call(
        body, name=name, grid=(nhb, nt),
        in_specs=_hgrn_specs(T, W, nhb) + [pl.BlockSpec((2, W), lambda hb, t: (0, hb)),
                                           pl.BlockSpec((1, LANES), lambda hb, t: (0, 0))],
        out_specs=[pl.BlockSpec((T, W), lambda hb, t: (t, hb)),
                   pl.BlockSpec((hps, ncc, LANES, LANES), lambda hb, t: (hb, t, 0, 0))],
        out_shape=[jax.ShapeDtypeStruct((Lp, D), BF16),
                   jax.ShapeDtypeStruct((H, Lp // HGRN_CHUNK, LANES, LANES), F32)],
        scratch_shapes=[pltpu.VMEM((hps, LANES, LANES), F32)],
        compiler_params=_params(("parallel", "arbitrary")),
    )(proj, proj, proj, proj, lbp, gg)


def _hgrn_bwd(proj, lbp, gg, dog, ss, T, name):
    Lp = proj.shape[0]
    D = proj.shape[1] // 4
    H = D // LANES
    hps = _hgrn_heads_per_step(H)
    W = hps * LANES
    nhb = H // hps
    assert nhb == 1, "d proj is written as whole rows: every head in one grid step"
    nt = Lp // T
    ncc = T // HGRN_CHUNK

    def body(q_ref, z_ref, i_ref, go_ref, p_ref, gg_ref, dog_ref, ss_ref, dproj_ref, dp_ref, dgg_ref, dst_ref):
        hb = pl.program_id(0)
        t = pl.program_id(1)

        @pl.when(t == 0)
        def _():
            dst_ref[...] = jnp.zeros_like(dst_ref)
            dp_ref[...] = jnp.zeros_like(dp_ref)

        @pl.when((t == 0) & (hb == 0))
        def _():
            dgg_ref[...] = jnp.zeros_like(dgg_ref)

        gain = gg_ref[...]
        row0 = (nt - 1 - t) * T

        def chunk(cc, carry):
            dstates, dps, dgain_sum = carry
            cidx = ncc - 1 - cc
            r0 = pl.multiple_of(cidx * HGRN_CHUNK, HGRN_CHUNK)
            sl = pl.ds(r0, HGRN_CHUNK)
            real = (row0 + r0 + lax.broadcasted_iota(jnp.int32, (HGRN_CHUNK, 1), 0)) >= N_PAD
            new_d, new_p = [], []
            for hh in range(hps):
                ln = slice(hh * LANES, (hh + 1) * LANES)
                _, vjp = jax.vjp(_hgrn_chunk, ss_ref[hh, cidx], q_ref[sl, ln], z_ref[sl, ln], i_ref[sl, ln],
                                 go_ref[sl, ln], p_ref[0:1, ln], p_ref[1:2, ln], gain)
                dSt, dq, dz, di, dgo, dp0, dp1, dgain = vjp((dstates[hh], dog_ref[sl, ln]))
                for grp, dval in enumerate((dq, dz, di, dgo)):
                    dproj_ref[sl, grp * D + hh * LANES:grp * D + (hh + 1) * LANES] = (
                        jnp.where(real, dval, 0.0).astype(BF16))
                new_d.append(dSt)
                new_p.append((dps[hh][0] + dp0, dps[hh][1] + dp1))
                dgain_sum = dgain_sum + dgain
            return tuple(new_d), tuple(new_p), dgain_sum

        zero_row = jnp.zeros((1, LANES), F32)
        init = (tuple(dst_ref[hh] for hh in range(hps)), tuple((zero_row, zero_row) for _ in range(hps)), zero_row)
        dstates, dps, dgain_sum = lax.fori_loop(0, ncc, chunk, init)
        for hh in range(hps):
            ln = slice(hh * LANES, (hh + 1) * LANES)
            dst_ref[hh] = dstates[hh]
            dp_ref[0:1, ln] += dps[hh][0]
            dp_ref[1:2, ln] += dps[hh][1]
        dgg_ref[0:1, :] += dgain_sum

    rev = pl.BlockSpec((T, W), lambda hb, t: (nt - 1 - t, hb))
    return pl.pallas_call(
        body, name=name, grid=(nhb, nt),
        in_specs=_hgrn_specs(T, W, nhb, nt) + [pl.BlockSpec((2, W), lambda hb, t: (0, hb)),
                                               pl.BlockSpec((1, LANES), lambda hb, t: (0, 0)), rev,
                                               pl.BlockSpec((hps, ncc, LANES, LANES),
                                                            lambda hb, t: (hb, nt - 1 - t, 0, 0))],
        out_specs=[pl.BlockSpec((T, 4 * D), lambda hb, t: (nt - 1 - t, 0)), pl.BlockSpec((8, W), lambda hb, t: (0, hb)),
                   pl.BlockSpec((8, LANES), lambda hb, t: (0, 0))],
        out_shape=[jax.ShapeDtypeStruct((Lp, 4 * D), BF16), jax.ShapeDtypeStruct((8, D), F32),
                   jax.ShapeDtypeStruct((8, LANES), F32)],
        scratch_shapes=[pltpu.VMEM((hps, LANES, LANES), F32)],
        compiler_params=_params(("arbitrary", "arbitrary")),
    )(proj, proj, proj, proj, lbp, gg, dog, ss)


def _xchg_sems(n_arr):
    return [pltpu.SemaphoreType.DMA((n_arr * (N_DEV - 1),)), pltpu.SemaphoreType.DMA((n_arr * (N_DEV - 1),)),
            pltpu.SemaphoreType.DMA((n_arr,))]


def _xchg_copies(ins, outs, per_peer, sems):
    send_sems, recv_sems, local_sems = sems
    x, y, c = lax.axis_index("x"), lax.axis_index("y"), lax.axis_index("c")
    me = 4 * x + 2 * y + c
    copies = []
    for n in range(len(ins)):
        src = ins[n].at[me] if per_peer[n] else ins[n]
        copies.append(pltpu.make_async_copy(src, outs[n].at[me], local_sems.at[n]))
    for rel in range(1, N_DEV):
        fx, fy, fc = (rel >> 2) & 1, (rel >> 1) & 1, rel & 1
        px = 1 - x if fx else x
        py = 1 - y if fy else y
        pc = 1 - c if fc else c
        peer = 4 * px + 2 * py + pc
        for n in range(len(ins)):
            src = ins[n].at[peer] if per_peer[n] else ins[n]
            copies.append(pltpu.make_async_remote_copy(
                src_ref=src, dst_ref=outs[n].at[me],
                send_sem=send_sems.at[n * (N_DEV - 1) + rel - 1],
                recv_sem=recv_sems.at[n * (N_DEV - 1) + rel - 1],
                device_id=(px, py, pc), device_id_type=pl.DeviceIdType.MESH))
    return copies


def _xchg_out_shapes(arrays, per_peer):
    return [jax.ShapeDtypeStruct(a.shape if pp else (N_DEV,) + a.shape, a.dtype) for a, pp in zip(arrays, per_peer)]


def _exchange(arrays, per_peer, name):
    n_arr = len(arrays)
    HBM = pl.BlockSpec(memory_space=pltpu.HBM)

    def body(*refs):
        copies = _xchg_copies(refs[:n_arr], refs[n_arr:2 * n_arr], per_peer, refs[2 * n_arr:])
        for cp in copies:
            cp.start()
        for cp in copies:
            cp.wait()

    return pl.pallas_call(
        body, name=name,
        in_specs=[HBM] * n_arr, out_specs=[HBM] * n_arr, out_shape=_xchg_out_shapes(arrays, per_peer),
        scratch_shapes=_xchg_sems(n_arr),
    )(*arrays)


ADAMW_VMEM_BUDGET = 36 * 1024 * 1024


def _adamw(recv, w, m, v, name):
    shape = w.shape
    C = shape[-1]
    R = math.prod(shape[:-1])
    lanes = -(-C // LANES) * LANES
    row_bytes = 2 * lanes * (N_DEV * recv.dtype.itemsize + 7 * 4)
    rc = _row_chunk(R, max(16, ADAMW_VMEM_BUDGET // row_bytes), 16 if recv.dtype == BF16 else 8)

    def body(r_ref, w_ref, m_ref, v_ref, g_ref, d_ref, mo_ref, vo_ref):
        g = r_ref[0].astype(F32)
        for s in range(1, N_DEV):
            g = g + r_ref[s].astype(F32)
        mn = ADAM_B1 * m_ref[...] + (1.0 - ADAM_B1) * g
        vn = ADAM_B2 * v_ref[...] + (1.0 - ADAM_B2) * (g * g)
        m_hat = mn / (1.0 - ADAM_B1 ** ADAM_STEP)
        v_hat = vn / (1.0 - ADAM_B2 ** ADAM_STEP)
        g_ref[...] = g
        d_ref[...] = -ADAM_LR * (m_hat / (jnp.sqrt(v_hat) + ADAM_EPS) + ADAM_WD * w_ref[...])
        mo_ref[...] = mn
        vo_ref[...] = vn

    row = pl.BlockSpec((rc, C), lambda i: (i, 0))
    outs = pl.pallas_call(
        body, name=name, grid=(R // rc,),
        in_specs=[pl.BlockSpec((N_DEV, rc, C), lambda i: (0, i, 0)), row, row, row],
        out_specs=[row] * 4,
        out_shape=[jax.ShapeDtypeStruct((R, C), F32)] * 4,
        compiler_params=_params(("parallel",)),
    )(recv.reshape(N_DEV, R, C), w.reshape(R, C), m.reshape(R, C), v.reshape(R, C))
    return [o.reshape(shape) for o in outs]


def _gathered_to_full(g, name):
    if name in COL_SHARDED:
        g = jnp.moveaxis(g, 0, -2)
        return g.reshape(g.shape[:-2] + (g.shape[-2] * g.shape[-1],))
    g = jnp.moveaxis(g, 0, -3)
    return g.reshape(g.shape[:-3] + (g.shape[-3] * g.shape[-2], g.shape[-1]))


def _full_to_slabs(full, name):
    if name in COL_SHARDED:
        f = full.reshape(full.shape[:-1] + (N_DEV, full.shape[-1] // N_DEV))
        return jnp.moveaxis(f, -2, 0)
    f = full.reshape(full.shape[:-2] + (N_DEV, full.shape[-2] // N_DEV, full.shape[-1]))
    return jnp.moveaxis(f, -3, 0)


def _pack_small(arrs):
    rows = []
    for a in arrs:
        flat = a.astype(F32).reshape(-1)
        pad = (-flat.shape[0]) % LANES
        rows.append(jnp.pad(flat, (0, pad)).reshape(-1, LANES))
    p = jnp.concatenate(rows, axis=0)
    return jnp.pad(p, ((0, (-p.shape[0]) % 8), (0, 0)))


def _unpack_small(packed, shapes):
    out, off = [], 0
    for shp in shapes:
        n = math.prod(shp)
        nr = -(-n // LANES)
        out.append(packed[off:off + nr].reshape(-1)[:n].reshape(shp))
        off += nr
    return out


def _local_step(x, target, meta, w_fox_in, late, small):
    S, D = x.shape
    Lp = S + HEAD_ROWS
    T = ROW_TILE if Lp % ROW_TILE == 0 else HEAD_ROWS
    P = D // LANES
    row = lambda v: v.reshape(1, -1).astype(F32)

    w_fin = jnp.pad(w_fox_in[0], ((0, 0), (0, LANES - w_fox_in.shape[-1] % LANES)))
    n_heads = w_fox_in.shape[-1] - 4 * D
    bf = jnp.pad(row(small["fox_b_f"]), ((0, 0), (0, LANES - small["fox_b_f"].size)))
    qg = jnp.tile(row(small["fox_q_norm"]), (1, 2))
    kg = jnp.tile(row(small["fox_k_norm"]), (1, 2))

    h0 = jnp.concatenate([jnp.zeros((N_PAD, D), F32), meta, x], axis=0)

    hn0, hn0t = _rms_fwd(h0, row(small["attn_norm"][0]), T, "rms0_fwd")
    proj0 = _mm(hn0, w_fin, "nn", F32, "fox_in_fwd")
    qn, kn, vb, c = _fox_prep_fwd(proj0, bf, qg, kg, T, D, "fox_prep_fwd")
    cT = c.T[:2 * P].at[:, :N_PAD].set(PAD_SHIFT)
    o, og0, mshift, linv, *gathered = _fox_attn_fwd(qn, kn, vb, c, cT, proj0, [late[n] for n in LATE], T, D,
                                                    "fox_attn_fwd")
    wl = {n: _gathered_to_full(g, n) for n, g in zip(LATE, gathered)}
    w_fout, w_hin, w_hout = wl["fox_w_out"][0], wl["hgrn_w_in"][0], wl["hgrn_w_out"][0]
    w_uin, w_uout = wl["ffn_w_in"], wl["ffn_w_out"]
    h1, hf0, hf0t = _out_proj_fwd(og0, w_fout, h0, row(small["ffn_norm"][0]), "fox_out_fwd")
    gu0 = _ffn_in_fwd(hf0, w_uin[0], "ffn0_in_fwd")
    act0 = gu0[2]
    h2, hn1, hn1t = _out_proj_fwd(act0, w_uout[0], h1, row(small["attn_norm"][1]), "ffn0_out_fwd")
    proj1 = _mm(hn1, w_hin, "nn", F32, "hgrn_in_fwd")
    lbp = small["hgrn_lower_bounds"].astype(F32)
    ggn = row(small["hgrn_g_norm"])
    Th = HGRN_TILE if Lp % HGRN_TILE == 0 else HEAD_ROWS
    og1, ss = _hgrn_fwd(proj1, lbp, ggn, Th, "hgrn_fwd")
    h3, hf1, hf1t = _out_proj_fwd(og1, w_hout, h2, row(small["ffn_norm"][1]), "hgrn_out_fwd")
    gu1 = _ffn_in_fwd(hf1, w_uin[1], "ffn1_in_fwd")
    act1 = gu1[2]
    h4 = _mm(act1, w_uout[1], "nn", F32, "ffn1_out_fwd", res=h3)
    loss_blk, dh4, dh4b, d_final = _final_loss(h4, row(small["final_norm"]), target, "final_loss")

    grads = {}

    def ffn_bwd(i, dh, dhb, h_in, hft, gu):
        grads_out = _mm(gu[3], dhb, "nn", F32, f"ffn{i}_out_dw", tm=_tile(gu[3].shape[0], 1408), tk=_tile(Lp, 1408))
        dgu = _ffn_out_dx(dhb, w_uout[i], gu[0], gu[1], f"ffn{i}_out_dx")
        grads_in = _mm(hft, dgu, "nn", F32, f"ffn{i}_in_dw", tm=D, tk=_tile(Lp, 1408))
        dh_new, dh_newb, dgain = _in_proj_dx(dgu, w_uin[i], h_in, row(small["ffn_norm"][i]), dh, f"ffn{i}_in_dx")
        return dh_new, dh_newb, grads_in, grads_out, dgain

    dh3, dh3b, g_uin1, g_uout1, d_fn1 = ffn_bwd(1, dh4, dh4b, h3, hf1t, gu1)
    grads["hgrn_w_out"] = _mm(og1, dh3b, "tn", F32, "hgrn_out_dw")[None]
    dog1 = _mm(dh3b, w_hout, "nt", F32, "hgrn_out_dx")
    dproj1, d_lb, d_gg = _hgrn_bwd(proj1, lbp, ggn, dog1, ss, Th, "hgrn_bwd")
    grads["hgrn_w_in"] = _mm(hn1t, dproj1, "nn", F32, "hgrn_in_dw", tm=D, tk=_tile(Lp, 1408))[None]
    dh2, dh2b, d_an1 = _in_proj_dx(dproj1, w_hin, h2, row(small["attn_norm"][1]), dh3, "hgrn_in_dx")
    dh1, dh1b, g_uin0, g_uout0, d_fn0 = ffn_bwd(0, dh2, dh2b, h1, hf0t, gu0)
    grads["ffn_w_in"] = jnp.stack([g_uin0, g_uin1])
    grads["ffn_w_out"] = jnp.stack([g_uout0, g_uout1])
    grads["fox_w_out"] = _mm(og0, dh1b, "tn", F32, "fox_out_dw")[None]
    do, dgate, delta = _fox_out_dx(dh1b, w_fout, o, proj0, linv, D, "fox_out_dx")
    slabs = [_full_to_slabs(grads[n], n).astype(BF16) for n in LATE]
    dqn, dkn, dv, dcr, *recv = _fox_attn_bwd(qn, kn, vb, c, cT, do, mshift, delta, slabs, T, D, "fox_attn_bwd")
    for n in LATE:
        del grads[n]
    dc = jnp.pad(dcr[:, :2, :].reshape(2 * P, Lp).T, ((0, 0), (0, LANES - 2 * P)))
    Tp = T // 2 if T == ROW_TILE else T
    dproj0, sm = _fox_prep_bwd(proj0, bf, qg, kg, dqn, dkn, dv, dgate, dc, Tp, D, "fox_prep_bwd")
    g_fin = _mm(hn0t, dproj0, "nn", F32, "fox_in_dw", tm=D, tk=_tile(Lp, 1408))[:, :4 * D + n_heads][None]
    dh0, _, d_an0, r_fin = _in_proj_dx(dproj0, w_fin, h0, row(small["attn_norm"][0]), dh1, "fox_in_dx",
                                       xchg=[_full_to_slabs(g_fin, "fox_w_in").astype(BF16)])

    grads["meta_tokens"] = dh0[N_PAD:HEAD_ROWS]
    grads["attn_norm"] = jnp.concatenate([d_an0, d_an1], axis=0)
    grads["ffn_norm"] = jnp.concatenate([d_fn0, d_fn1], axis=0)
    grads["final_norm"] = d_final[0]
    grads["fox_b_f"] = sm[0:1, :n_heads]
    grads["fox_q_norm"] = sm[1:2, :FOX_DH] + sm[1:2, FOX_DH:]
    grads["fox_k_norm"] = sm[2:3, :FOX_DH] + sm[2:3, FOX_DH:]
    grads["hgrn_lower_bounds"] = d_lb[0:2]
    grads["hgrn_g_norm"] = d_gg[0:1]
    return loss_blk[0, 0], dh0[HEAD_ROWS:], grads, dict(zip(LATE, recv), fox_w_in=r_fin)


def kernel(x, meta_tokens, attn_norm, ffn_norm, final_norm, fox_w_in, fox_b_f, fox_q_norm, fox_k_norm, fox_w_out, hgrn_w_in, hgrn_lower_bounds, hgrn_g_norm, hgrn_w_out, ffn_w_in, ffn_w_out, loss_target, m_meta_tokens, m_attn_norm, m_ffn_norm, m_final_norm, m_fox_w_in, m_fox_b_f, m_fox_q_norm, m_fox_k_norm, m_fox_w_out, m_hgrn_w_in, m_hgrn_lower_bounds, m_hgrn_g_norm, m_hgrn_w_out, m_ffn_w_in, m_ffn_w_out, v_meta_tokens, v_attn_norm, v_ffn_norm, v_final_norm, v_fox_w_in, v_fox_b_f, v_fox_q_norm, v_fox_k_norm, v_fox_w_out, v_hgrn_w_in, v_hgrn_lower_bounds, v_hgrn_g_norm, v_hgrn_w_out, v_ffn_w_in, v_ffn_w_out):
    w = dict(meta_tokens=meta_tokens, attn_norm=attn_norm, ffn_norm=ffn_norm, final_norm=final_norm,
             fox_w_in=fox_w_in, fox_b_f=fox_b_f, fox_q_norm=fox_q_norm, fox_k_norm=fox_k_norm,
             fox_w_out=fox_w_out, hgrn_w_in=hgrn_w_in, hgrn_lower_bounds=hgrn_lower_bounds,
             hgrn_g_norm=hgrn_g_norm, hgrn_w_out=hgrn_w_out, ffn_w_in=ffn_w_in, ffn_w_out=ffn_w_out)
    m = dict(meta_tokens=m_meta_tokens, attn_norm=m_attn_norm, ffn_norm=m_ffn_norm, final_norm=m_final_norm,
             fox_w_in=m_fox_w_in, fox_b_f=m_fox_b_f, fox_q_norm=m_fox_q_norm, fox_k_norm=m_fox_k_norm,
             fox_w_out=m_fox_w_out, hgrn_w_in=m_hgrn_w_in, hgrn_lower_bounds=m_hgrn_lower_bounds,
             hgrn_g_norm=m_hgrn_g_norm, hgrn_w_out=m_hgrn_w_out, ffn_w_in=m_ffn_w_in, ffn_w_out=m_ffn_w_out)
    v = dict(meta_tokens=v_meta_tokens, attn_norm=v_attn_norm, ffn_norm=v_ffn_norm, final_norm=v_final_norm,
             fox_w_in=v_fox_w_in, fox_b_f=v_fox_b_f, fox_q_norm=v_fox_q_norm, fox_k_norm=v_fox_k_norm,
             fox_w_out=v_fox_w_out, hgrn_w_in=v_hgrn_w_in, hgrn_lower_bounds=v_hgrn_lower_bounds,
             hgrn_g_norm=v_hgrn_g_norm, hgrn_w_out=v_hgrn_w_out, ffn_w_in=v_ffn_w_in, ffn_w_out=v_ffn_w_out)
    axes = ("x", "y", "c")
    small_shapes = [w[n].shape for n in SMALL]

    g_meta, g_fin = _exchange([w["meta_tokens"].astype(F32), w["fox_w_in"].astype(BF16)], [False] * 2,
                              "gather_weights")
    loss_local, grad_x, grads, recv = _local_step(
        x[0], loss_target[0], _gathered_to_full(g_meta, "meta_tokens"), _gathered_to_full(g_fin, "fox_w_in"),
        {n: w[n].astype(BF16) for n in LATE}, {n: w[n] for n in SMALL})
    loss = lax.psum(loss_local, axes)

    r_meta, r_small = _exchange([_full_to_slabs(grads["meta_tokens"], "meta_tokens"),
                                 _pack_small([grads[n] for n in SMALL])], [True, False], "scatter_grads")
    recv.update(meta_tokens=r_meta)

    res = {n: _adamw(recv[n], w[n], m[n], v[n], "adamw_" + n) for n in BIG}
    sml = _adamw(r_small, _pack_small([w[n] for n in SMALL]), _pack_small([m[n] for n in SMALL]),
                 _pack_small([v[n] for n in SMALL]), "adamw_small")
    outs = []
    for k in range(4):
        d = {n: res[n][k] for n in BIG}
        d.update(zip(SMALL, _unpack_small(sml[k], small_shapes)))
        outs.extend(d[n] for n in WEIGHTS)
    return (loss, grad_x[None], *outs)
```

```python
import functools
import math

import jax
import jax.numpy as jnp
from jax import lax
from jax.experimental import pallas as pl
from jax.experimental.pallas import tpu as pltpu

F32 = jnp.float32
BF16 = jnp.bfloat16
EPS = 1e-6
N_META = 16
LANES = 128
HEAD_ROWS = 256
ROW_TILE = 768
N_PAD = HEAD_ROWS - N_META
FOX_DH = 64
HGRN_CHUNK = 64
HGRN_TILE = 384
N_DEV = 8
NEG = -1e30
PAD_SHIFT = 1e4
VMEM_LIMIT = 56 * 1024 * 1024
HI = lax.Precision.HIGHEST
LOG2E = 1.0 / math.log(2.0)
LN2 = math.log(2.0)

ADAM_LR = 0.001
ADAM_B1 = 0.9
ADAM_B2 = 0.999
ADAM_EPS = 1e-08
ADAM_WD = 0.01
ADAM_STEP = 10

BIG = ("meta_tokens", "fox_w_in", "fox_w_out", "hgrn_w_in", "hgrn_w_out", "ffn_w_in", "ffn_w_out")
SMALL = ("attn_norm", "ffn_norm", "final_norm", "fox_b_f", "fox_q_norm", "fox_k_norm",
         "hgrn_lower_bounds", "hgrn_g_norm")
WEIGHTS = ("meta_tokens", "attn_norm", "ffn_norm", "final_norm", "fox_w_in", "fox_b_f", "fox_q_norm",
           "fox_k_norm", "fox_w_out", "hgrn_w_in", "hgrn_lower_bounds", "hgrn_g_norm", "hgrn_w_out",
           "ffn_w_in", "ffn_w_out")
COL_SHARDED = ("meta_tokens", "fox_w_in", "hgrn_w_in", "ffn_w_in")
LATE = ("fox_w_out", "hgrn_w_in", "hgrn_w_out", "ffn_w_in", "ffn_w_out")


def _params(sem=None):
    return pltpu.CompilerParams(dimension_semantics=sem, vmem_limit_bytes=VMEM_LIMIT)


def _tile(n, cap):
    best = None
    for t in range(LANES, min(n, cap) + 1, LANES):
        if n % t == 0:
            best = t
    assert best is not None, (n, cap)
    return best


def _row_chunk(n, cap, mult=8):
    best = n
    for t in range(mult, min(n, cap) + 1, mult):
        if n % t == 0:
            best = t
    return best


def _dg(a, b, ca, cb):
    return lax.dot_general(a.astype(BF16), b.astype(BF16), (((ca,), (cb,)), ((), ())),
                           preferred_element_type=F32)


@jax.custom_vjp
def _d_nn(a, b):
    return _dg(a, b, 1, 0)


@jax.custom_vjp
def _d_nt(a, b):
    return _dg(a, b, 1, 1)


@jax.custom_vjp
def _d_tn(a, b):
    return _dg(a, b, 0, 0)


_d_nn.defvjp(lambda a, b: (_d_nn(a, b), (a, b)), lambda r, g: (_d_nt(g, r[1]), _d_tn(r[0], g)))
_d_nt.defvjp(lambda a, b: (_d_nt(a, b), (a, b)), lambda r, g: (_d_nn(g, r[1]), _d_tn(g, r[0])))
_d_tn.defvjp(lambda a, b: (_d_tn(a, b), (a, b)), lambda r, g: (_d_nt(r[1], g), _d_nn(r[0], g)))


def _log_sigmoid(x):
    return jnp.minimum(x, 0.0) - jnp.log1p(jnp.exp(-jnp.abs(x)))


def _rms(x, g):
    return x * lax.rsqrt(jnp.mean(x * x, axis=-1, keepdims=True) + EPS) * g


def _mm(a, b, mode, out_dtype, name, res=None, tm=None, tn=None, tk=None):
    assert a.dtype == BF16 and b.dtype == BF16, (name, a.dtype, b.dtype)
    if mode == "nn":
        (M, K), N = a.shape, b.shape[1]
    elif mode == "nt":
        (M, K), N = a.shape, b.shape[0]
    else:
        (K, M), N = a.shape, b.shape[1]
    if mode == "nn":
        tm, tn, tk = tm or _tile(M, ROW_TILE), tn or _tile(N, 1408), tk or _tile(K, 2816)
    elif mode == "nt":
        tm, tn, tk = tm or _tile(M, ROW_TILE if K <= 2048 else ROW_TILE // 2), tn or N, tk or K
    else:
        tm, tn, tk = tm or _tile(M, 1408), tn or _tile(N, 1408), tk or _tile(K, ROW_TILE)
    nk = K // tk
    if mode == "tn":
        a_spec = pl.BlockSpec((tk, tm), lambda j, i, k: (k, i))
        dims = (((0,), (0,)), ((), ()))
    else:
        a_spec = pl.BlockSpec((tm, tk), lambda j, i, k: (i, k))
        dims = (((1,), (1 if mode == "nt" else 0,)), ((), ()))
    if mode == "nt":
        b_spec = pl.BlockSpec((tn, tk), lambda j, i, k: (j, k))
    else:
        b_spec = pl.BlockSpec((tk, tn), lambda j, i, k: (k, j))

    o_spec = pl.BlockSpec((tm, tn), lambda j, i, k: (i, j))

    def body(a_ref, b_ref, *rest):
        r_ref = rest[0] if res is not None else None
        o_ref, acc_ref = rest[-2:]
        k = pl.program_id(2)

        @pl.when(k == 0)
        def _():
            acc_ref[...] = jnp.zeros_like(acc_ref)

        acc_ref[...] += lax.dot_general(a_ref[...], b_ref[...], dims, preferred_element_type=F32)

        @pl.when(k == nk - 1)
        def _():
            out = acc_ref[...] if r_ref is None else acc_ref[...] + r_ref[...]
            o_ref[...] = out.astype(out_dtype)

    return pl.pallas_call(
        body, name=name, grid=(N // tn, M // tm, nk),
        in_specs=[a_spec, b_spec] + ([o_spec] if res is not None else []),
        out_specs=o_spec,
        out_shape=jax.ShapeDtypeStruct((M, N), out_dtype),
        scratch_shapes=[pltpu.VMEM((tm, tn), F32)],
        compiler_params=_params(("parallel", "parallel", "arbitrary")),
    )(a, b, *([res] if res is not None else []))


def _out_proj_fwd(a, w, res, gain, name):
    Lp, K = a.shape
    D = w.shape[1]
    tm = _tile(Lp, ROW_TILE)

    def body(a_ref, w_ref, r_ref, g_ref, h_ref, hn_ref, hnt_ref):
        h = jnp.dot(a_ref[...], w_ref[...], preferred_element_type=F32) + r_ref[...]
        h_ref[...] = h
        hn = _rms(h, g_ref[...])
        hn_ref[...] = hn.astype(BF16)
        hnt_ref[...] = hn.T.astype(BF16)

    row = pl.BlockSpec((tm, D), lambda i: (i, 0))
    return pl.pallas_call(
        body, name=name, grid=(Lp // tm,),
        in_specs=[pl.BlockSpec((tm, K), lambda i: (i, 0)), pl.BlockSpec((K, D), lambda i: (0, 0)), row,
                  pl.BlockSpec((1, D), lambda i: (0, 0))],
        out_specs=[row, row, pl.BlockSpec((D, tm), lambda i: (0, i))],
        out_shape=[jax.ShapeDtypeStruct((Lp, D), F32), jax.ShapeDtypeStruct((Lp, D), BF16),
                   jax.ShapeDtypeStruct((D, Lp), BF16)],
        compiler_params=_params(("parallel",)),
    )(a, w, res, gain)


def _in_proj_dx(dy, w, x, gain, dres, name, xchg=()):
    Lp, N = dy.shape
    D = w.shape[0]
    tm = _tile(Lp, ROW_TILE // 2)
    nt = Lp // tm
    nx = len(xchg)

    def body(dy_ref, w_ref, x_ref, g_ref, dr_ref, *rest):
        x_in, (dx_ref, dxb_ref, dg_ref), x_out, sems = rest[:nx], rest[nx:nx + 3], rest[nx + 3:2 * nx + 3], rest[2 * nx + 3:]

        @pl.when(pl.program_id(0) == 0)
        def _():
            dg_ref[...] = jnp.zeros_like(dg_ref)
            if nx:
                for cp in _xchg_copies(x_in, x_out, [True] * nx, sems):
                    cp.start()

        dhn = lax.dot_general(dy_ref[...], w_ref[...], (((1,), (1,)), ((), ())), preferred_element_type=F32)
        _, vjp = jax.vjp(_rms, x_ref[...], g_ref[...])
        dx, dg = vjp(dhn)
        dx = dx + dr_ref[...]
        dx_ref[...] = dx
        dxb_ref[...] = dx.astype(BF16)
        dg_ref[...] += dg

        if nx:
            @pl.when(pl.program_id(0) == nt - 1)
            def _():
                for cp in _xchg_copies(x_in, x_out, [True] * nx, sems):
                    cp.wait()

    row = pl.BlockSpec((tm, D), lambda i: (i, 0))
    vec = pl.BlockSpec((1, D), lambda i: (0, 0))
    HBM = pl.BlockSpec(memory_space=pltpu.HBM)
    return pl.pallas_call(
        body, name=name, grid=(nt,),
        in_specs=[pl.BlockSpec((tm, N), lambda i: (i, 0)), pl.BlockSpec((D, N), lambda i: (0, 0)), row, vec, row]
        + [HBM] * nx,
        out_specs=[row, row, vec] + [HBM] * nx,
        out_shape=[jax.ShapeDtypeStruct((Lp, D), F32), jax.ShapeDtypeStruct((Lp, D), BF16),
                   jax.ShapeDtypeStruct((1, D), F32)] + _xchg_out_shapes(xchg, [True] * nx),
        scratch_shapes=_xchg_sems(nx) if nx else [],
        compiler_params=_params(("arbitrary",)),
    )(dy, w, x, gain, dres, *xchg)


def _rms_fwd(x, g, T, name):
    Lp, D = x.shape

    def body(x_ref, g_ref, o_ref, ot_ref):
        y = _rms(x_ref[...], g_ref[...])
        o_ref[...] = y.astype(BF16)
        ot_ref[...] = y.T.astype(BF16)

    return pl.pallas_call(
        body, name=name, grid=(Lp // T,),
        in_specs=[pl.BlockSpec((T, D), lambda i: (i, 0)), pl.BlockSpec((1, D), lambda i: (0, 0))],
        out_specs=[pl.BlockSpec((T, D), lambda i: (i, 0)), pl.BlockSpec((D, T), lambda i: (0, i))],
        out_shape=[jax.ShapeDtypeStruct((Lp, D), BF16), jax.ShapeDtypeStruct((D, Lp), BF16)],
        compiler_params=_params(("parallel",)),
    )(x, g)


def _swiglu(gate, up):
    return gate * jax.nn.sigmoid(gate) * up


def _ffn_in_fwd(hf, w_in, name):
    Lp, D = hf.shape
    F = w_in.shape[1] // 2
    tm = _tile(Lp, ROW_TILE)
    tn = _tile(F, 1408)
    nj = F // tn

    def body(a_ref, bg_ref, bu_ref, g_ref, u_ref, act_ref, actt_ref):
        a = a_ref[...]
        g = jnp.dot(a, bg_ref[...], preferred_element_type=F32)
        u = jnp.dot(a, bu_ref[...], preferred_element_type=F32)
        g_ref[...] = g.astype(BF16)
        u_ref[...] = u.astype(BF16)
        act = _swiglu(g, u)
        act_ref[...] = act.astype(BF16)
        actt_ref[...] = act.T.astype(BF16)

    tile = pl.BlockSpec((tm, tn), lambda j, i: (i, j))
    return pl.pallas_call(
        body, name=name, grid=(nj, Lp // tm),
        in_specs=[pl.BlockSpec((tm, D), lambda j, i: (i, 0)), pl.BlockSpec((D, tn), lambda j, i: (0, j)),
                  pl.BlockSpec((D, tn), lambda j, i: (0, nj + j))],
        out_specs=[tile, tile, tile, pl.BlockSpec((tn, tm), lambda j, i: (j, i))],
        out_shape=[jax.ShapeDtypeStruct((Lp, F), BF16)] * 3 + [jax.ShapeDtypeStruct((F, Lp), BF16)],
        compiler_params=_params(("parallel", "parallel")),
    )(hf, w_in, w_in)


def _ffn_out_dx(dhb, w_out, g, u, name):
    Lp, D = dhb.shape
    F = w_out.shape[0]
    tm = HEAD_ROWS

    def body(a_ref, b_ref, g_ref, u_ref, o_ref):
        dact = lax.dot_general(a_ref[...], b_ref[...], (((1,), (1,)), ((), ())), preferred_element_type=F32)
        _, vjp = jax.vjp(_swiglu, g_ref[...].astype(F32), u_ref[...].astype(F32))
        dg, du = vjp(dact)
        o_ref[:, :F] = dg.astype(BF16)
        o_ref[:, F:] = du.astype(BF16)

    wide = pl.BlockSpec((tm, F), lambda i: (i, 0))
    return pl.pallas_call(
        body, name=name, grid=(Lp // tm,),
        in_specs=[pl.BlockSpec((tm, D), lambda i: (i, 0)), pl.BlockSpec((F, D), lambda i: (0, 0)), wide, wide],
        out_specs=pl.BlockSpec((tm, 2 * F), lambda i: (i, 0)),
        out_shape=jax.ShapeDtypeStruct((Lp, 2 * F), BF16),
        compiler_params=_params(("parallel",)),
    )(dhb, w_out, g, u)


def _final_loss(h, g, target, name):
    Lp, D = h.shape
    TR = HEAD_ROWS

    def loss_fn(hh, gg, tt):
        err = _rms(hh, gg) - tt
        return 0.5 * jnp.sum(jnp.mean(err * err, axis=-1))

    def body(h_ref, g_ref, t_ref, loss_ref, dh_ref, dhb_ref, dg_ref):
        i = pl.program_id(0)

        @pl.when(i == 0)
        def _():
            loss_ref[...] = jnp.zeros_like(loss_ref)
            dg_ref[...] = jnp.zeros_like(dg_ref)
            dh_ref[...] = jnp.zeros_like(dh_ref)
            dhb_ref[...] = jnp.zeros_like(dhb_ref)

        @pl.when(i > 0)
        def _():
            val, vjp = jax.vjp(lambda hh, gg: loss_fn(hh, gg, t_ref[...]), h_ref[...], g_ref[...])
            dh, dg = vjp(jnp.ones((), F32))
            dh_ref[...] = dh
            dhb_ref[...] = dh.astype(BF16)
            dg_ref[...] += dg
            loss_ref[...] += val

    row = pl.BlockSpec((TR, D), lambda i: (i, 0))
    return pl.pallas_call(
        body, name=name, grid=(Lp // TR,),
        in_specs=[row, pl.BlockSpec((1, D), lambda i: (0, 0)),
                  pl.BlockSpec((TR, D), lambda i: (jnp.maximum(i - 1, 0), 0))],
        out_specs=[pl.BlockSpec((8, LANES), lambda i: (0, 0)), row, row, pl.BlockSpec((1, D), lambda i: (0, 0))],
        out_shape=[jax.ShapeDtypeStruct((8, LANES), F32), jax.ShapeDtypeStruct((Lp, D), F32),
                   jax.ShapeDtypeStruct((Lp, D), BF16), jax.ShapeDtypeStruct((1, D), F32)],
        compiler_params=_params(("arbitrary",)),
    )(h, g, target)


def _lane_lo():
    return lax.broadcasted_iota(jnp.int32, (1, LANES), 1) < FOX_DH


def _headnorm(x, g, scale):
    lo = _lane_lo()
    x2 = x * x
    s0 = jnp.sum(jnp.where(lo, x2, 0.0), axis=-1, keepdims=True)
    s1 = jnp.sum(jnp.where(lo, 0.0, x2), axis=-1, keepdims=True)
    r = jnp.where(lo, lax.rsqrt(s0 / FOX_DH + EPS), lax.rsqrt(s1 / FOX_DH + EPS))
    return x * r * g * scale


AUG = 3


def _split3(x):
    hi = x.astype(BF16).astype(F32)
    mid = (x - hi).astype(BF16).astype(F32)
    return hi, mid, x - hi - mid


def _fox_prep_fwd(proj, bf, qg, kg, T, D, name):
    Lp = proj.shape[0]
    nb = D // LANES
    scale = FOX_DH ** -0.5 * LOG2E

    def body(q_ref, k_ref, v_ref, fl_ref, bf_ref, qg_ref, kg_ref, qn_ref, ka_ref, kb_ref, vb_ref, c_ref, carry_ref):
        i = pl.program_id(0)

        @pl.when(i == 0)
        def _():
            carry_ref[...] = jnp.zeros_like(carry_ref)

        vb_ref[...] = v_ref[...].astype(BF16)
        log_f = _log_sigmoid(fl_ref[...] + bf_ref[...])
        row = lax.broadcasted_iota(jnp.int32, (T, T), 0)
        col = lax.broadcasted_iota(jnp.int32, (T, T), 1)
        tri = (col <= row).astype(F32)
        c = jnp.dot(tri, log_f, precision=HI, preferred_element_type=F32) + carry_ref[...]
        c2 = c * LOG2E
        c_ref[...] = c2
        last = lax.broadcasted_iota(jnp.int32, (T, 1), 0) == T - 1
        carry_ref[...] = jnp.sum(jnp.where(last, c, 0.0), axis=0, keepdims=True)

        is_pad = (i * T + lax.broadcasted_iota(jnp.int32, (T, 1), 0)) < N_PAD
        negc = jnp.where(is_pad, -PAD_SHIFT, -c2)
        lane = lax.broadcasted_iota(jnp.int32, (1, LANES), 1)
        for b in range(nb):
            sl = slice(b * LANES, (b + 1) * LANES)
            qn_ref[:, sl] = _headnorm(q_ref[:, sl], qg_ref[...], scale).astype(BF16)
            kn = _headnorm(k_ref[:, sl], kg_ref[...], 1.0)
            ka = jnp.where(lane < FOX_DH, kn, 0.0)
            kb = jnp.where(lane < FOX_DH, 0.0, kn)
            for n, (pa, pb) in enumerate(zip(_split3(_pick_col(negc, 2 * b)), _split3(_pick_col(negc, 2 * b + 1)))):
                ka = jnp.where(lane == FOX_DH + n, pa, ka)
                kb = jnp.where(lane == n, pb, kb)
            ka_ref[:, sl] = ka.astype(BF16)
            kb_ref[:, sl] = kb.astype(BF16)

    wide = lambda j: pl.BlockSpec((T, D), lambda i: (i, j))
    vec = pl.BlockSpec((1, LANES), lambda i: (0, 0))
    return pl.pallas_call(
        body, name=name, grid=(Lp // T,),
        in_specs=[wide(0), wide(1), wide(2), pl.BlockSpec((T, LANES), lambda i: (i, 4 * nb)), vec, vec, vec],
        out_specs=[wide(0), wide(0), wide(0), wide(0), pl.BlockSpec((T, LANES), lambda i: (i, 0))],
        out_shape=[jax.ShapeDtypeStruct((Lp, D), BF16)] * 4 + [jax.ShapeDtypeStruct((Lp, LANES), F32)],
        scratch_shapes=[pltpu.VMEM((1, LANES), F32)],
        compiler_params=_params(("arbitrary",)),
    )(proj, proj, proj, proj, bf, qg, kg)


def _fox_prep_bwd(proj, bf, qg, kg, dqn, dkn, dv, dgate, dc, T, D, name):
    Lp = proj.shape[0]
    nb = D // LANES
    nt = Lp // T
    scale = FOX_DH ** -0.5 * LOG2E

    def body(q_ref, k_ref, fl_ref, bf_ref, qg_ref, kg_ref, dqn_ref, dkn_ref, dv_ref, dgate_ref, dc_ref,
             dproj_ref, sm_ref, carry_ref):
        @pl.when(pl.program_id(0) == 0)
        def _():
            carry_ref[...] = jnp.zeros_like(carry_ref)
            sm_ref[...] = jnp.zeros_like(sm_ref)

        dqg = jnp.zeros((1, LANES), F32)
        dkg = jnp.zeros((1, LANES), F32)
        for b in range(nb):
            sl = slice(b * LANES, (b + 1) * LANES)
            _, vjp = jax.vjp(lambda x, g: _headnorm(x, g, scale), q_ref[:, sl], qg_ref[...])
            dx, dg = vjp(dqn_ref[:, sl] * LN2)
            dproj_ref[:, sl] = dx.astype(BF16)
            dqg = dqg + dg
            _, vjp = jax.vjp(lambda x, g: _headnorm(x, g, 1.0), k_ref[:, sl], kg_ref[...])
            dx, dg = vjp(dkn_ref[:, sl] * LN2)
            dproj_ref[:, D + b * LANES:D + (b + 1) * LANES] = dx.astype(BF16)
            dkg = dkg + dg
        dproj_ref[:, 2 * D:3 * D] = dv_ref[...].astype(BF16)
        dproj_ref[:, 3 * D:4 * D] = dgate_ref[...]
        dcv = dc_ref[...]
        row = lax.broadcasted_iota(jnp.int32, (T, T), 0)
        col = lax.broadcasted_iota(jnp.int32, (T, T), 1)
        triu = (col >= row).astype(F32)
        dlogf = jnp.dot(triu, dcv, precision=HI, preferred_element_type=F32) + carry_ref[...]
        carry_ref[...] += jnp.sum(dcv, axis=0, keepdims=True)
        _, vjp = jax.vjp(_log_sigmoid, fl_ref[...] + bf_ref[...])
        (dfl,) = vjp(dlogf)
        dproj_ref[:, 4 * D:] = dfl.astype(BF16)
        sm_ref[0:1, :] += jnp.sum(dfl, axis=0, keepdims=True)
        sm_ref[1:2, :] += dqg
        sm_ref[2:3, :] += dkg

    wide = lambda j: pl.BlockSpec((T, D), lambda i: (nt - 1 - i, j))
    narrow = lambda j: pl.BlockSpec((T, LANES), lambda i: (nt - 1 - i, j))
    vec = pl.BlockSpec((1, LANES), lambda i: (0, 0))
    return pl.pallas_call(
        body, name=name, grid=(nt,),
        in_specs=[wide(0), wide(1), narrow(4 * nb), vec, vec, vec, wide(0), wide(0), wide(0), wide(0), narrow(0)],
        out_specs=[pl.BlockSpec((T, 4 * D + LANES), lambda i: (nt - 1 - i, 0)), pl.BlockSpec((8, LANES), lambda i: (0, 0))],
        out_shape=[jax.ShapeDtypeStruct((Lp, 4 * D + LANES), BF16), jax.ShapeDtypeStruct((8, LANES), F32)],
        scratch_shapes=[pltpu.VMEM((1, LANES), F32)],
        compiler_params=_params(("arbitrary",)),
    )(proj, proj, proj, bf, qg, kg, dqn, dkn, dv, dgate, dc)


def _fox_q_operands(q):
    lane = lax.broadcasted_iota(jnp.int32, (1, LANES), 1)
    zero, one = jnp.zeros_like(q), jnp.ones_like(q)
    return (jnp.where(lane < FOX_DH, q, jnp.where(lane < FOX_DH + AUG, one, zero)),
            jnp.where(lane < FOX_DH, jnp.where(lane < AUG, one, zero), q))


def _fox_mask(i, k0, T):
    qpos = i * T + lax.broadcasted_iota(jnp.int32, (T, 1), 0)
    kpos = k0 + lax.broadcasted_iota(jnp.int32, (1, T), 1)
    return (kpos <= qpos) & ((kpos >= N_PAD) | (qpos < N_PAD))


def _pick_col(blk, idx):
    lane = lax.broadcasted_iota(jnp.int32, (1, LANES), 1)
    return jnp.sum(jnp.where(lane == idx, blk, 0.0), axis=1, keepdims=True)


def _split_halves(blk):
    lo = _lane_lo()
    return (jnp.max(jnp.where(lo, blk, -jnp.inf), axis=1, keepdims=True),
            jnp.max(jnp.where(lo, -jnp.inf, blk), axis=1, keepdims=True))


def _fox_attn_fwd(qn, ka, kb, vb, c, proj, xchg, T, D, name):
    Lp = qn.shape[0]
    P = D // LANES
    nt = Lp // T
    nx = len(xchg)

    def body(q_ref, ka_ref, kb_ref, v_ref, c_ref, g_ref, *rest):
        x_in, (o_ref, og_ref, m_ref, li_ref), x_out, sems = rest[:nx], rest[nx:nx + 4], rest[nx + 4:2 * nx + 4], rest[2 * nx + 4:]
        p = pl.program_id(0)
        i = pl.program_id(1)

        @pl.when((p == 0) & (i == 0))
        def _():
            for cp in _xchg_copies(x_in, x_out, [False] * nx, sems):
                cp.start()

        lo = _lane_lo()
        q = q_ref[...]
        qh = _fox_q_operands(q)
        cblk = c_ref[...]
        cq = tuple(_pick_col(cblk, 2 * p + h) for h in (0, 1))
        one = jnp.ones_like(q)

        def step(j, carry, masked):
            k0 = pl.multiple_of(j * T, LANES)
            kj = (ka_ref[pl.ds(k0, T), :], kb_ref[pl.ds(k0, T), :])
            vj = v_ref[pl.ds(k0, T), :]
            vh = (jnp.where(lo, vj, one), jnp.where(lo, one, vj))
            mask = _fox_mask(i, k0, T) if masked else None
            out = []
            for h in (0, 1):
                m, acc = carry[h]
                t = lax.dot_general(qh[h], kj[h], (((1,), (1,)), ((), ())), preferred_element_type=F32)
                if masked:
                    t = jnp.where(mask, t, NEG)
                m_new = jnp.ceil(jnp.maximum(m, cq[h] + jnp.max(t, axis=1, keepdims=True)))
                pr = jnp.exp2(t + (cq[h] - m_new)).astype(BF16)
                acc = jnp.exp2(m - m_new) * acc + jnp.dot(pr, vh[h], preferred_element_type=F32)
                out.append((m_new, acc))
            return tuple(out)

        init = tuple((jnp.full((T, 1), NEG, F32), jnp.zeros((T, LANES), F32)) for _ in (0, 1))
        carry = lax.fori_loop(0, i, lambda j, cr: step(j, cr, False), init)
        (m0, a0), (m1, a1) = step(i, carry, True)
        l0 = pltpu.roll(a0, FOX_DH, 1)
        l1 = pltpu.roll(a1, FOX_DH, 1)
        o = jnp.where(lo, a0 / l0, a1 / l1)
        o_ref[...] = o
        m_ref[...] = jnp.where(lo, m0, m1)
        li_ref[...] = jnp.where(lo, 1.0 / l0, 1.0 / l1)
        og_ref[...] = (o * jax.nn.sigmoid(g_ref[...])).astype(BF16)

        @pl.when((p == P - 1) & (i == nt - 1))
        def _():
            for cp in _xchg_copies(x_in, x_out, [False] * nx, sems):
                cp.wait()

    tile = pl.BlockSpec((T, LANES), lambda p, i: (i, p))
    full = pl.BlockSpec((Lp, LANES), lambda p, i: (0, p))
    HBM = pl.BlockSpec(memory_space=pltpu.HBM)
    return pl.pallas_call(
        body, name=name, grid=(P, nt),
        in_specs=[tile, full, full, full, pl.BlockSpec((T, LANES), lambda p, i: (i, 0)),
                  pl.BlockSpec((T, LANES), lambda p, i: (i, 3 * P + p))] + [HBM] * nx,
        out_specs=[tile, tile, tile, tile] + [HBM] * nx,
        out_shape=[jax.ShapeDtypeStruct((Lp, D), F32), jax.ShapeDtypeStruct((Lp, D), BF16),
                   jax.ShapeDtypeStruct((Lp, D), F32), jax.ShapeDtypeStruct((Lp, D), F32)]
        + _xchg_out_shapes(xchg, [False] * nx),
        scratch_shapes=_xchg_sems(nx),
        compiler_params=_params(("arbitrary", "arbitrary")),
    )(qn, ka, kb, vb, c, proj, *xchg)


def _fox_out_dx(dhb, w_out, o, proj, linv, D, name):
    Lp = o.shape[0]
    tm = HEAD_ROWS

    def body(a_ref, w_ref, o_ref, g_ref, li_ref, do_ref, dg_ref, dl_ref):
        lo = _lane_lo()
        dog_all = lax.dot_general(a_ref[...], w_ref[...], (((1,), (1,)), ((), ())), preferred_element_type=F32)
        for b in range(D // LANES):
            sl = slice(b * LANES, (b + 1) * LANES)
            dog = dog_all[:, sl]
            sig = jax.nn.sigmoid(g_ref[:, sl])
            ov = o_ref[:, sl]
            do = (dog * sig * li_ref[:, sl]).astype(BF16)
            do_ref[:, sl] = do
            dg_ref[:, sl] = (dog * ov * sig * (1.0 - sig)).astype(BF16)
            t = do.astype(F32) * ov
            d0 = jnp.sum(jnp.where(lo, t, 0.0), axis=1, keepdims=True)
            d1 = jnp.sum(jnp.where(lo, 0.0, t), axis=1, keepdims=True)
            dl_ref[:, sl] = jnp.where(lo, d0, d1)

    row = pl.BlockSpec((tm, D), lambda i: (i, 0))
    return pl.pallas_call(
        body, name=name, grid=(Lp // tm,),
        in_specs=[row, pl.BlockSpec((D, D), lambda i: (0, 0)), row, pl.BlockSpec((tm, D), lambda i: (i, 3)), row],
        out_specs=[row, row, row],
        out_shape=[jax.ShapeDtypeStruct((Lp, D), BF16), jax.ShapeDtypeStruct((Lp, D), BF16),
                   jax.ShapeDtypeStruct((Lp, D), F32)],
        compiler_params=_params(("parallel",)),
    )(dhb, w_out, o, proj, linv)


def _fox_attn_bwd(qn, ka, kb, vb, c, do, mshift, delta, xchg, T, D, name):
    Lp = qn.shape[0]
    P = D // LANES
    nt = Lp // T
    nx = len(xchg)

    def body(q_ref, do_ref, m_ref, dl_ref, c_ref, ka_ref, kb_ref, v_ref, *rest):
        x_in, (dq_ref, dk_ref, dv_ref, dc_ref), x_out, sems = rest[:nx], rest[nx:nx + 4], rest[nx + 4:2 * nx + 4], rest[2 * nx + 4:]
        p = pl.program_id(0)
        i = pl.program_id(1)

        @pl.when((p == 0) & (i == 0))
        def _():
            for cp in _xchg_copies(x_in, x_out, [True] * nx, sems):
                cp.start()

        @pl.when(i == 0)
        def _():
            dk_ref[...] = jnp.zeros_like(dk_ref)
            dv_ref[...] = jnp.zeros_like(dv_ref)
            dc_ref[...] = jnp.zeros_like(dc_ref)

        lo = _lane_lo()
        q = q_ref[...]
        do = do_ref[...]
        zero = jnp.zeros_like(q)
        qh = _fox_q_operands(q)
        doh = (jnp.where(lo, do, zero), jnp.where(lo, zero, do))
        msh = _split_halves(m_ref[...])
        dlt = _split_halves(dl_ref[...])
        cblk = c_ref[...]
        shift = tuple(_pick_col(cblk, 2 * p + h) - msh[h] for h in (0, 1))

        def step(j, carry, masked):
            k0 = pl.multiple_of(j * T, LANES)
            kj = (ka_ref[pl.ds(k0, T), :], kb_ref[pl.ds(k0, T), :])
            vj = v_ref[pl.ds(k0, T), :]
            mask = _fox_mask(i, k0, T) if masked else None
            dqs, dks, dvs = [], [], []
            for h in (0, 1):
                t = lax.dot_general(qh[h], kj[h], (((1,), (1,)), ((), ())), preferred_element_type=F32)
                if masked:
                    t = jnp.where(mask, t, NEG)
                pb = jnp.exp2(t + shift[h]).astype(BF16)
                dp = lax.dot_general(doh[h], vj, (((1,), (1,)), ((), ())), preferred_element_type=F32)
                ds = pb.astype(F32) * (dp - dlt[h])
                dsb = ds.astype(BF16)
                dqs.append(carry[h] + jnp.dot(dsb, kj[h], preferred_element_type=F32))
                dks.append(lax.dot_general(dsb, q, (((0,), (0,)), ((), ())), preferred_element_type=F32))
                dvs.append(lax.dot_general(pb, do, (((0,), (0,)), ((), ())), preferred_element_type=F32))
                dc_ref[0, h:h + 1, pl.ds(k0, T)] += -jnp.sum(ds, axis=0, keepdims=True)
            dk_ref[pl.ds(k0, T), :] += jnp.where(lo, dks[0], dks[1])
            dv_ref[pl.ds(k0, T), :] += jnp.where(lo, dvs[0], dvs[1])
            return tuple(dqs)

        init = (jnp.zeros((T, LANES), F32), jnp.zeros((T, LANES), F32))
        carry = lax.fori_loop(0, i, lambda j, cr: step(j, cr, False), init)
        dq0, dq1 = step(i, carry, True)
        dq_ref[...] = jnp.where(lo, dq0, dq1)

        @pl.when((p == P - 1) & (i == nt - 1))
        def _():
            for cp in _xchg_copies(x_in, x_out, [True] * nx, sems):
                cp.wait()

    tile = pl.BlockSpec((T, LANES), lambda p, i: (i, p))
    full = pl.BlockSpec((Lp, LANES), lambda p, i: (0, p))
    HBM = pl.BlockSpec(memory_space=pltpu.HBM)
    return pl.pallas_call(
        body, name=name, grid=(P, nt),
        in_specs=[tile, tile, tile, tile, pl.BlockSpec((T, LANES), lambda p, i: (i, 0)), full, full, full]
        + [HBM] * nx,
        out_specs=[tile, full, full, pl.BlockSpec((1, 8, Lp), lambda p, i: (p, 0, 0))] + [HBM] * nx,
        out_shape=[jax.ShapeDtypeStruct((Lp, D), F32)] * 3 + [jax.ShapeDtypeStruct((P, 8, Lp), F32)]
        + _xchg_out_shapes(xchg, [True] * nx),
        scratch_shapes=_xchg_sems(nx),
        compiler_params=_params(("arbitrary", "arbitrary")),
    )(qn, do, mshift, delta, c, ka, kb, vb, *xchg)


def _scan_rows(x, reverse):
    C = x.shape[0]
    row = lax.broadcasted_iota(jnp.int32, (C, 1), 0)
    step = 1
    while step < C:
        if reverse:
            x = x + jnp.where(row < C - step, pltpu.roll(x, C - step, 0), 0.0)
        else:
            x = x + jnp.where(row >= step, pltpu.roll(x, step, 0), 0.0)
        step *= 2
    return x


@jax.custom_vjp
def _cumsum_rows(x):
    return _scan_rows(x, False)


_cumsum_rows.defvjp(lambda x: (_scan_rows(x, False), None), lambda _, g: (_scan_rows(g, True),))


def _hgrn_chunk(St, qr, z, vi, go, p0, p1, gg):
    C = qr.shape[0]
    lb = jax.nn.sigmoid(p1 - p0)
    a = jnp.log(lb)
    cc = jnp.log1p(-lb) + _log_sigmoid(z)
    log_f = jnp.maximum(a, cc) + jnp.log1p(jnp.exp(-jnp.abs(a - cc)))
    k = (1.0 - lb) * jax.nn.sigmoid(-z)
    q = qr * jax.nn.sigmoid(qr)
    row = lax.broadcasted_iota(jnp.int32, (C, C), 0)
    col = lax.broadcasted_iota(jnp.int32, (C, C), 1)
    causal = col <= row
    b = _cumsum_rows(log_f)
    mid = lax.broadcasted_iota(jnp.int32, (C, 1), 0) == C // 2 - 1
    r = jnp.sum(jnp.where(mid, b, 0.0), axis=0, keepdims=True)
    b_last = jnp.sum(log_f, axis=0, keepdims=True)
    attn = jnp.where(causal, _d_nt(q * jnp.exp(b - r), k * jnp.exp(r - b)), 0.0)
    o = _d_nn(attn, vi) + _d_nt(q * jnp.exp(b), St)
    St_new = St * jnp.exp(b_last) + _d_tn(vi, k * jnp.exp(b_last - b))
    og = _rms(o, gg) * (go * jax.nn.sigmoid(go))
    return St_new, og


def _hgrn_heads_per_step(H):
    return 8 if H % 8 == 0 else 4 if H % 4 == 0 else 1


def _hgrn_specs(T, W, nhb, rev_nt=None):
    if rev_nt is None:
        return [pl.BlockSpec((T, W), functools.partial(lambda hb, t, g: (t, g * nhb + hb), g=g)) for g in range(4)]
    return [pl.BlockSpec((T, W), functools.partial(lambda hb, t, g: (rev_nt - 1 - t, g * nhb + hb), g=g))
            for g in range(4)]


def _hgrn_fwd(proj, lbp, gg, T, name):
    Lp = proj.shape[0]
    D = proj.shape[1] // 4
    H = D // LANES
    hps = _hgrn_heads_per_step(H)
    W = hps * LANES
    nhb = H // hps
    nt = Lp // T
    ncc = T // HGRN_CHUNK

    def body(q_ref, z_ref, i_ref, go_ref, p_ref, gg_ref, og_ref, ss_ref, st_ref):
        @pl.when(pl.program_id(1) == 0)
        def _():
            st_ref[...] = jnp.zeros_like(st_ref)

        gain = gg_ref[...]

        def chunk(cidx, states):
            sl = pl.ds(pl.multiple_of(cidx * HGRN_CHUNK, HGRN_CHUNK), HGRN_CHUNK)
            new = []
            for hh in range(hps):
                ln = slice(hh * LANES, (hh + 1) * LANES)
                ss_ref[hh, cidx] = states[hh]
                St_new, og = _hgrn_chunk(states[hh], q_ref[sl, ln], z_ref[sl, ln], i_ref[sl, ln], go_ref[sl, ln],
                                         p_ref[0:1, ln], p_ref[1:2, ln], gain)
                og_ref[sl, ln] = og.astype(BF16)
                new.append(St_new)
            return tuple(new)

        states = lax.fori_loop(0, ncc, chunk, tuple(st_ref[hh] for hh in range(hps)))
        for hh in range(hps):
            st_ref[hh] = states[hh]

    return pl.pallas_call(
        body, name=name, grid=(nhb, nt),
        in_specs=_hgrn_specs(T, W, nhb) + [pl.BlockSpec((2, W), lambda hb, t: (0, hb)),
                                           pl.BlockSpec((1, LANES), lambda hb, t: (0, 0))],
        out_specs=[pl.BlockSpec((T, W), lambda hb, t: (t, hb)),
                   pl.BlockSpec((hps, ncc, LANES, LANES), lambda hb, t: (hb, t, 0, 0))],
        out_shape=[jax.ShapeDtypeStruct((Lp, D), BF16),
                   jax.ShapeDtypeStruct((H, Lp // HGRN_CHUNK, LANES, LANES), F32)],
        scratch_shapes=[pltpu.VMEM((hps, LANES, LANES), F32)],
        compiler_params=_params(("parallel", "arbitrary")),
    )(proj, proj, proj, proj, lbp, gg)


def _hgrn_bwd(proj, lbp, gg, dog, ss, T, name):
    Lp = proj.shape[0]
    D = proj.shape[1] // 4
    H = D // LANES
    hps = _hgrn_heads_per_step(H)
    W = hps * LANES
    nhb = H // hps
    assert nhb == 1, "d proj is written as whole rows: every head in one grid step"
    nt = Lp // T
    ncc = T // HGRN_CHUNK

    def body(q_ref, z_ref, i_ref, go_ref, p_ref, gg_ref, dog_ref, ss_ref, dproj_ref, dp_ref, dgg_ref, dst_ref):
        hb = pl.program_id(0)
        t = pl.program_id(1)

        @pl.when(t == 0)
        def _():
            dst_ref[...] = jnp.zeros_like(dst_ref)
            dp_ref[...] = jnp.zeros_like(dp_ref)

        @pl.when((t == 0) & (hb == 0))
        def _():
            dgg_ref[...] = jnp.zeros_like(dgg_ref)

        gain = gg_ref[...]
        row0 = (nt - 1 - t) * T

        def chunk(cc, carry):
            dstates, dps, dgain_sum = carry
            cidx = ncc - 1 - cc
            r0 = pl.multiple_of(cidx * HGRN_CHUNK, HGRN_CHUNK)
            sl = pl.ds(r0, HGRN_CHUNK)
            real = (row0 + r0 + lax.broadcasted_iota(jnp.int32, (HGRN_CHUNK, 1), 0)) >= N_PAD
            new_d, new_p = [], []
            for hh in range(hps):
                ln = slice(hh * LANES, (hh + 1) * LANES)
                _, vjp = jax.vjp(_hgrn_chunk, ss_ref[hh, cidx], q_ref[sl, ln], z_ref[sl, ln], i_ref[sl, ln],
                                 go_ref[sl, ln], p_ref[0:1, ln], p_ref[1:2, ln], gain)
                dSt, dq, dz, di, dgo, dp0, dp1, dgain = vjp((dstates[hh], dog_ref[sl, ln]))
                for grp, dval in enumerate((dq, dz, di, dgo)):
                    dproj_ref[sl, grp * D + hh * LANES:grp * D + (hh + 1) * LANES] = (
                        jnp.where(real, dval, 0.0).astype(BF16))
                new_d.append(dSt)
                new_p.append((dps[hh][0] + dp0, dps[hh][1] + dp1))
                dgain_sum = dgain_sum + dgain
            return tuple(new_d), tuple(new_p), dgain_sum

        zero_row = jnp.zeros((1, LANES), F32)
        init = (tuple(dst_ref[hh] for hh in range(hps)), tuple((zero_row, zero_row) for _ in range(hps)), zero_row)
        dstates, dps, dgain_sum = lax.fori_loop(0, ncc, chunk, init)
        for hh in range(hps):
            ln = slice(hh * LANES, (hh + 1) * LANES)
            dst_ref[hh] = dstates[hh]
            dp_ref[0:1, ln] += dps[hh][0]
            dp_ref[1:2, ln] += dps[hh][1]
        dgg_ref[0:1, :] += dgain_sum

    rev = pl.BlockSpec((T, W), lambda hb, t: (nt - 1 - t, hb))
    return pl.pallas_call(
        body, name=name, grid=(nhb, nt),
        in_specs=_hgrn_specs(T, W, nhb, nt) + [pl.BlockSpec((2, W), lambda hb, t: (0, hb)),
                                               pl.BlockSpec((1, LANES), lambda hb, t: (0, 0)), rev,
                                               pl.BlockSpec((hps, ncc, LANES, LANES),
                                                            lambda hb, t: (hb, nt - 1 - t, 0, 0))],
        out_specs=[pl.BlockSpec((T, 4 * D), lambda hb, t: (nt - 1 - t, 0)), pl.BlockSpec((8, W), lambda hb, t: (0, hb)),
                   pl.BlockSpec((8, LANES), lambda hb, t: (0, 0))],
        out_shape=[jax.ShapeDtypeStruct((Lp, 4 * D), BF16), jax.ShapeDtypeStruct((8, D), F32),
                   jax.ShapeDtypeStruct((8, LANES), F32)],
        scratch_shapes=[pltpu.VMEM((hps, LANES, LANES), F32)],
        compiler_params=_params(("arbitrary", "arbitrary")),
    )(proj, proj, proj, proj, lbp, gg, dog, ss)


def _xchg_sems(n_arr):
    return [pltpu.SemaphoreType.DMA((n_arr * (N_DEV - 1),)), pltpu.SemaphoreType.DMA((n_arr * (N_DEV - 1),)),
            pltpu.SemaphoreType.DMA((n_arr,))]


def _xchg_copies(ins, outs, per_peer, sems):
    send_sems, recv_sems, local_sems = sems
    x, y, c = lax.axis_index("x"), lax.axis_index("y"), lax.axis_index("c")
    me = 4 * x + 2 * y + c
    copies = []
    for n in range(len(ins)):
        src = ins[n].at[me] if per_peer[n] else ins[n]
        copies.append(pltpu.make_async_copy(src, outs[n].at[me], local_sems.at[n]))
    for rel in range(1, N_DEV):
        fx, fy, fc = (rel >> 2) & 1, (rel >> 1) & 1, rel & 1
        px = 1 - x if fx else x
        py = 1 - y if fy else y
        pc = 1 - c if fc else c
        peer = 4 * px + 2 * py + pc
        for n in range(len(ins)):
            src = ins[n].at[peer] if per_peer[n] else ins[n]
            copies.append(pltpu.make_async_remote_copy(
                src_ref=src, dst_ref=outs[n].at[me],
                send_sem=send_sems.at[n * (N_DEV - 1) + rel - 1],
                recv_sem=recv_sems.at[n * (N_DEV - 1) + rel - 1],
                device_id=(px, py, pc), device_id_type=pl.DeviceIdType.MESH))
    return copies


def _xchg_out_shapes(arrays, per_peer):
    return [jax.ShapeDtypeStruct(a.shape if pp else (N_DEV,) + a.shape, a.dtype) for a, pp in zip(arrays, per_peer)]


def _exchange(arrays, per_peer, name):
    n_arr = len(arrays)
    HBM = pl.BlockSpec(memory_space=pltpu.HBM)

    def body(*refs):
        copies = _xchg_copies(refs[:n_arr], refs[n_arr:2 * n_arr], per_peer, refs[2 * n_arr:])
        for cp in copies:
            cp.start()
        for cp in copies:
            cp.wait()

    return pl.pallas_call(
        body, name=name,
        in_specs=[HBM] * n_arr, out_specs=[HBM] * n_arr, out_shape=_xchg_out_shapes(arrays, per_peer),
        scratch_shapes=_xchg_sems(n_arr),
    )(*arrays)


ADAMW_VMEM_BUDGET = 36 * 1024 * 1024


def _adamw(recv, w, m, v, name):
    shape = w.shape
    C = shape[-1]
    R = math.prod(shape[:-1])
    lanes = -(-C // LANES) * LANES
    row_bytes = 2 * lanes * (N_DEV * recv.dtype.itemsize + 7 * 4)
    rc = _row_chunk(R, max(16, ADAMW_VMEM_BUDGET // row_bytes), 16 if recv.dtype == BF16 else 8)

    def body(r_ref, w_ref, m_ref, v_ref, g_ref, d_ref, mo_ref, vo_ref):
        g = r_ref[0].astype(F32)
        for s in range(1, N_DEV):
            g = g + r_ref[s].astype(F32)
        mn = ADAM_B1 * m_ref[...] + (1.0 - ADAM_B1) * g
        vn = ADAM_B2 * v_ref[...] + (1.0 - ADAM_B2) * (g * g)
        m_hat = mn / (1.0 - ADAM_B1 ** ADAM_STEP)
        v_hat = vn / (1.0 - ADAM_B2 ** ADAM_STEP)
        g_ref[...] = g
        d_ref[...] = -ADAM_LR * (m_hat / (jnp.sqrt(v_hat) + ADAM_EPS) + ADAM_WD * w_ref[...])
        mo_ref[...] = mn
        vo_ref[...] = vn

    row = pl.BlockSpec((rc, C), lambda i: (i, 0))
    outs = pl.pallas_call(
        body, name=name, grid=(R // rc,),
        in_specs=[pl.BlockSpec((N_DEV, rc, C), lambda i: (0, i, 0)), row, row, row],
        out_specs=[row] * 4,
        out_shape=[jax.ShapeDtypeStruct((R, C), F32)] * 4,
        compiler_params=_params(("parallel",)),
    )(recv.reshape(N_DEV, R, C), w.reshape(R, C), m.reshape(R, C), v.reshape(R, C))
    return [o.reshape(shape) for o in outs]


def _gathered_to_full(g, name):
    if name in COL_SHARDED:
        g = jnp.moveaxis(g, 0, -2)
        return g.reshape(g.shape[:-2] + (g.shape[-2] * g.shape[-1],))
    g = jnp.moveaxis(g, 0, -3)
    return g.reshape(g.shape[:-3] + (g.shape[-3] * g.shape[-2], g.shape[-1]))


def _full_to_slabs(full, name):
    if name in COL_SHARDED:
        f = full.reshape(full.shape[:-1] + (N_DEV, full.shape[-1] // N_DEV))
        return jnp.moveaxis(f, -2, 0)
    f = full.reshape(full.shape[:-2] + (N_DEV, full.shape[-2] // N_DEV, full.shape[-1]))
    return jnp.moveaxis(f, -3, 0)


def _pack_small(arrs):
    rows = []
    for a in arrs:
        flat = a.astype(F32).reshape(-1)
        pad = (-flat.shape[0]) % LANES
        rows.append(jnp.pad(flat, (0, pad)).reshape(-1, LANES))
    p = jnp.concatenate(rows, axis=0)
    return jnp.pad(p, ((0, (-p.shape[0]) % 8), (0, 0)))


def _unpack_small(packed, shapes):
    out, off = [], 0
    for shp in shapes:
        n = math.prod(shp)
        nr = -(-n // LANES)
        out.append(packed[off:off + nr].reshape(-1)[:n].reshape(shp))
        off += nr
    return out


def _local_step(x, target, meta, w_fox_in, late, small):
    S, D = x.shape
    Lp = S + HEAD_ROWS
    T = ROW_TILE if Lp % ROW_TILE == 0 else HEAD_ROWS
    P = D // LANES
    row = lambda v: v.reshape(1, -1).astype(F32)

    w_fin = jnp.pad(w_fox_in[0], ((0, 0), (0, LANES - w_fox_in.shape[-1] % LANES)))
    n_heads = w_fox_in.shape[-1] - 4 * D
    bf = jnp.pad(row(small["fox_b_f"]), ((0, 0), (0, LANES - small["fox_b_f"].size)))
    qg = jnp.tile(row(small["fox_q_norm"]), (1, 2))
    kg = jnp.tile(row(small["fox_k_norm"]), (1, 2))

    h0 = jnp.concatenate([jnp.zeros((N_PAD, D), F32), meta, x], axis=0)

    hn0, hn0t = _rms_fwd(h0, row(small["attn_norm"][0]), T, "rms0_fwd")
    proj0 = _mm(hn0, w_fin, "nn", F32, "fox_in_fwd")
    qn, ka, kb, vb, c = _fox_prep_fwd(proj0, bf, qg, kg, T, D, "fox_prep_fwd")
    o, og0, mshift, linv, *gathered = _fox_attn_fwd(qn, ka, kb, vb, c, proj0, [late[n] for n in LATE], T, D,
                                                    "fox_attn_fwd")
    wl = {n: _gathered_to_full(g, n) for n, g in zip(LATE, gathered)}
    w_fout, w_hin, w_hout = wl["fox_w_out"][0], wl["hgrn_w_in"][0], wl["hgrn_w_out"][0]
    w_uin, w_uout = wl["ffn_w_in"], wl["ffn_w_out"]
    h1, hf0, hf0t = _out_proj_fwd(og0, w_fout, h0, row(small["ffn_norm"][0]), "fox_out_fwd")
    gu0 = _ffn_in_fwd(hf0, w_uin[0], "ffn0_in_fwd")
    act0 = gu0[2]
    h2, hn1, hn1t = _out_proj_fwd(act0, w_uout[0], h1, row(small["attn_norm"][1]), "ffn0_out_fwd")
    proj1 = _mm(hn1, w_hin, "nn", F32, "hgrn_in_fwd")
    lbp = small["hgrn_lower_bounds"].astype(F32)
    ggn = row(small["hgrn_g_norm"])
    Th = HGRN_TILE if Lp % HGRN_TILE == 0 else HEAD_ROWS
    og1, ss = _hgrn_fwd(proj1, lbp, ggn, Th, "hgrn_fwd")
    h3, hf1, hf1t = _out_proj_fwd(og1, w_hout, h2, row(small["ffn_norm"][1]), "hgrn_out_fwd")
    gu1 = _ffn_in_fwd(hf1, w_uin[1], "ffn1_in_fwd")
    act1 = gu1[2]
    h4 = _mm(act1, w_uout[1], "nn", F32, "ffn1_out_fwd", res=h3)
    loss_blk, dh4, dh4b, d_final = _final_loss(h4, row(small["final_norm"]), target, "final_loss")

    grads = {}

    def ffn_bwd(i, dh, dhb, h_in, hft, gu):
        grads_out = _mm(gu[3], dhb, "nn", F32, f"ffn{i}_out_dw", tm=_tile(gu[3].shape[0], 1408), tk=_tile(Lp, 1408))
        dgu = _ffn_out_dx(dhb, w_uout[i], gu[0], gu[1], f"ffn{i}_out_dx")
        grads_in = _mm(hft, dgu, "nn", F32, f"ffn{i}_in_dw", tm=D, tk=_tile(Lp, 1408))
        dh_new, dh_newb, dgain = _in_proj_dx(dgu, w_uin[i], h_in, row(small["ffn_norm"][i]), dh, f"ffn{i}_in_dx")
        return dh_new, dh_newb, grads_in, grads_out, dgain

    dh3, dh3b, g_uin1, g_uout1, d_fn1 = ffn_bwd(1, dh4, dh4b, h3, hf1t, gu1)
    grads["hgrn_w_out"] = _mm(og1, dh3b, "tn", F32, "hgrn_out_dw")[None]
    dog1 = _mm(dh3b, w_hout, "nt", F32, "hgrn_out_dx")
    dproj1, d_lb, d_gg = _hgrn_bwd(proj1, lbp, ggn, dog1, ss, Th, "hgrn_bwd")
    grads["hgrn_w_in"] = _mm(hn1t, dproj1, "nn", F32, "hgrn_in_dw", tm=D, tk=_tile(Lp, 1408))[None]
    dh2, dh2b, d_an1 = _in_proj_dx(dproj1, w_hin, h2, row(small["attn_norm"][1]), dh3, "hgrn_in_dx")
    dh1, dh1b, g_uin0, g_uout0, d_fn0 = ffn_bwd(0, dh2, dh2b, h1, hf0t, gu0)
    grads["ffn_w_in"] = jnp.stack([g_uin0, g_uin1])
    grads["ffn_w_out"] = jnp.stack([g_uout0, g_uout1])
    grads["fox_w_out"] = _mm(og0, dh1b, "tn", F32, "fox_out_dw")[None]
    do, dgate, delta = _fox_out_dx(dh1b, w_fout, o, proj0, linv, D, "fox_out_dx")
    slabs = [_full_to_slabs(grads[n], n).astype(BF16) for n in LATE]
    dqn, dkn, dv, dcr, *recv = _fox_attn_bwd(qn, ka, kb, vb, c, do, mshift, delta, slabs, T, D, "fox_attn_bwd")
    for n in LATE:
        del grads[n]
    dc = jnp.pad(dcr[:, :2, :].reshape(2 * P, Lp).T, ((0, 0), (0, LANES - 2 * P)))
    Tp = T // 2 if T == ROW_TILE else T
    dproj0, sm = _fox_prep_bwd(proj0, bf, qg, kg, dqn, dkn, dv, dgate, dc, Tp, D, "fox_prep_bwd")
    g_fin = _mm(hn0t, dproj0, "nn", F32, "fox_in_dw", tm=D, tk=_tile(Lp, 1408))[:, :4 * D + n_heads][None]
    dh0, _, d_an0, r_fin = _in_proj_dx(dproj0, w_fin, h0, row(small["attn_norm"][0]), dh1, "fox_in_dx",
                                       xchg=[_full_to_slabs(g_fin, "fox_w_in").astype(BF16)])

    grads["meta_tokens"] = dh0[N_PAD:HEAD_ROWS]
    grads["attn_norm"] = jnp.concatenate([d_an0, d_an1], axis=0)
    grads["ffn_norm"] = jnp.concatenate([d_fn0, d_fn1], axis=0)
    grads["final_norm"] = d_final[0]
    grads["fox_b_f"] = sm[0:1, :n_heads]
    grads["fox_q_norm"] = sm[1:2, :FOX_DH] + sm[1:2, FOX_DH:]
    grads["fox_k_norm"] = sm[2:3, :FOX_DH] + sm[2:3, FOX_DH:]
    grads["hgrn_lower_bounds"] = d_lb[0:2]
    grads["hgrn_g_norm"] = d_gg[0:1]
    return loss_blk[0, 0], dh0[HEAD_ROWS:], grads, dict(zip(LATE, recv), fox_w_in=r_fin)


def kernel(x, meta_tokens, attn_norm, ffn_norm, final_norm, fox_w_in, fox_b_f, fox_q_norm, fox_k_norm, fox_w_out, hgrn_w_in, hgrn_lower_bounds, hgrn_g_norm, hgrn_w_out, ffn_w_in, ffn_w_out, loss_target, m_meta_tokens, m_attn_norm, m_ffn_norm, m_final_norm, m_fox_w_in, m_fox_b_f, m_fox_q_norm, m_fox_k_norm, m_fox_w_out, m_hgrn_w_in, m_hgrn_lower_bounds, m_hgrn_g_norm, m_hgrn_w_out, m_ffn_w_in, m_ffn_w_out, v_meta_tokens, v_attn_norm, v_ffn_norm, v_final_norm, v_fox_w_in, v_fox_b_f, v_fox_q_norm, v_fox_k_norm, v_fox_w_out, v_hgrn_w_in, v_hgrn_lower_bounds, v_hgrn_g_norm, v_hgrn_w_out, v_ffn_w_in, v_ffn_w_out):
    w = dict(meta_tokens=meta_tokens, attn_norm=attn_norm, ffn_norm=ffn_norm, final_norm=final_norm,
             fox_w_in=fox_w_in, fox_b_f=fox_b_f, fox_q_norm=fox_q_norm, fox_k_norm=fox_k_norm,
             fox_w_out=fox_w_out, hgrn_w_in=hgrn_w_in, hgrn_lower_bounds=hgrn_lower_bounds,
             hgrn_g_norm=hgrn_g_norm, hgrn_w_out=hgrn_w_out, ffn_w_in=ffn_w_in, ffn_w_out=ffn_w_out)
    m = dict(meta_tokens=m_meta_tokens, attn_norm=m_attn_norm, ffn_norm=m_ffn_norm, final_norm=m_final_norm,
             fox_w_in=m_fox_w_in, fox_b_f=m_fox_b_f, fox_q_norm=m_fox_q_norm, fox_k_norm=m_fox_k_norm,
             fox_w_out=m_fox_w_out, hgrn_w_in=m_hgrn_w_in, hgrn_lower_bounds=m_hgrn_lower_bounds,
             hgrn_g_norm=m_hgrn_g_norm, hgrn_w_out=m_hgrn_w_out, ffn_w_in=m_ffn_w_in, ffn_w_out=m_ffn_w_out)
    v = dict(meta_tokens=v_meta_tokens, attn_norm=v_attn_norm, ffn_norm=v_ffn_norm, final_norm=v_final_norm,
             fox_w_in=v_fox_w_in, fox_b_f=v_fox_b_f, fox_q_norm=v_fox_q_norm, fox_k_norm=v_fox_k_norm,
             fox_w_out=v_fox_w_out, hgrn_w_in=v_hgrn_w_in, hgrn_lower_bounds=v_hgrn_lower_bounds,
             hgrn_g_norm=v_hgrn_g_norm, hgrn_w_out=v_hgrn_w_out, ffn_w_in=v_ffn_w_in, ffn_w_out=v_ffn_w_out)
    axes = ("x", "y", "c")
    small_shapes = [w[n].shape for n in SMALL]

    g_meta, g_fin = _exchange([w["meta_tokens"].astype(F32), w["fox_w_in"].astype(BF16)], [False] * 2,
                              "gather_weights")
    loss_local, grad_x, grads, recv = _local_step(
        x[0], loss_target[0], _gathered_to_full(g_meta, "meta_tokens"), _gathered_to_full(g_fin, "fox_w_in"),
        {n: w[n].astype(BF16) for n in LATE}, {n: w[n] for n in SMALL})
    loss = lax.psum(loss_local, axes)

    r_meta, r_small = _exchange([_full_to_slabs(grads["meta_tokens"], "meta_tokens"),
                                 _pack_small([grads[n] for n in SMALL])], [True, False], "scatter_grads")
    recv.update(meta_tokens=r_meta)

    res = {n: _adamw(recv[n], w[n], m[n], v[n], "adamw_" + n) for n in BIG}
    sml = _adamw(r_small, _pack_small([w[n] for n in SMALL]), _pack_small([m[n] for n in SMALL]),
                 _pack_small([v[n] for n in SMALL]), "adamw_small")
    outs = []
    for k in range(4):
        d = {n: res[n][k] for n in BIG}
        d.update(zip(SMALL, _unpack_small(sml[k], small_shapes)))
        outs.extend(d[n] for n in WEIGHTS)
    return (loss, grad_x[None], *outs)
```

```python
import functools
import math

import jax
import jax.numpy as jnp
from jax import lax
from jax.experimental import pallas as pl
from jax.experimental.pallas import tpu as pltpu

F32 = jnp.float32
BF16 = jnp.bfloat16
EPS = 1e-6
N_META = 16
LANES = 128
HEAD_ROWS = 256
ROW_TILE = 768
N_PAD = HEAD_ROWS - N_META
FOX_DH = 64
HGRN_CHUNK = 64
HGRN_TILE = 384
N_DEV = 8
NEG = -1e30
PAD_SHIFT = 1e4
VMEM_LIMIT = 56 * 1024 * 1024
HI = lax.Precision.HIGHEST
LOG2E = 1.0 / math.log(2.0)
LN2 = math.log(2.0)

ADAM_LR = 0.001
ADAM_B1 = 0.9
ADAM_B2 = 0.999
ADAM_EPS = 1e-08
ADAM_WD = 0.01
ADAM_STEP = 10

BIG = ("meta_tokens", "fox_w_in", "fox_w_out", "hgrn_w_in", "hgrn_w_out", "ffn_w_in", "ffn_w_out")
SMALL = ("attn_norm", "ffn_norm", "final_norm", "fox_b_f", "fox_q_norm", "fox_k_norm",
         "hgrn_lower_bounds", "hgrn_g_norm")
WEIGHTS = ("meta_tokens", "attn_norm", "ffn_norm", "final_norm", "fox_w_in", "fox_b_f", "fox_q_norm",
           "fox_k_norm", "fox_w_out", "hgrn_w_in", "hgrn_lower_bounds", "hgrn_g_norm", "hgrn_w_out",
           "ffn_w_in", "ffn_w_out")
COL_SHARDED = ("meta_tokens", "fox_w_in", "hgrn_w_in", "ffn_w_in")
LATE = ("fox_w_out", "hgrn_w_in", "hgrn_w_out", "ffn_w_in", "ffn_w_out")


def _params(sem=None):
    return pltpu.CompilerParams(dimension_semantics=sem, vmem_limit_bytes=VMEM_LIMIT)


def _tile(n, cap):
    best = None
    for t in range(LANES, min(n, cap) + 1, LANES):
        if n % t == 0:
            best = t
    assert best is not None, (n, cap)
    return best


def _row_chunk(n, cap, mult=8):
    best = n
    for t in range(mult, min(n, cap) + 1, mult):
        if n % t == 0:
            best = t
    return best


def _dg(a, b, ca, cb):
    return lax.dot_general(a.astype(BF16), b.astype(BF16), (((ca,), (cb,)), ((), ())),
                           preferred_element_type=F32)


@jax.custom_vjp
def _d_nn(a, b):
    return _dg(a, b, 1, 0)


@jax.custom_vjp
def _d_nt(a, b):
    return _dg(a, b, 1, 1)


@jax.custom_vjp
def _d_tn(a, b):
    return _dg(a, b, 0, 0)


_d_nn.defvjp(lambda a, b: (_d_nn(a, b), (a, b)), lambda r, g: (_d_nt(g, r[1]), _d_tn(r[0], g)))
_d_nt.defvjp(lambda a, b: (_d_nt(a, b), (a, b)), lambda r, g: (_d_nn(g, r[1]), _d_tn(g, r[0])))
_d_tn.defvjp(lambda a, b: (_d_tn(a, b), (a, b)), lambda r, g: (_d_nt(r[1], g), _d_nn(r[0], g)))


def _log_sigmoid(x):
    return jnp.minimum(x, 0.0) - jnp.log1p(jnp.exp(-jnp.abs(x)))


def _rms(x, g):
    return x * lax.rsqrt(jnp.mean(x * x, axis=-1, keepdims=True) + EPS) * g


def _mm(a, b, mode, out_dtype, name, res=None, tm=None, tn=None, tk=None):
    assert a.dtype == BF16 and b.dtype == BF16, (name, a.dtype, b.dtype)
    if mode == "nn":
        (M, K), N = a.shape, b.shape[1]
    elif mode == "nt":
        (M, K), N = a.shape, b.shape[0]
    else:
        (K, M), N = a.shape, b.shape[1]
    if mode == "nn":
        tm, tn, tk = tm or _tile(M, ROW_TILE), tn or _tile(N, 1408), tk or _tile(K, 2816)
    elif mode == "nt":
        tm, tn, tk = tm or _tile(M, ROW_TILE if K <= 2048 else ROW_TILE // 2), tn or N, tk or K
    else:
        tm, tn, tk = tm or _tile(M, 1408), tn or _tile(N, 1408), tk or _tile(K, ROW_TILE)
    nk = K // tk
    if mode == "tn":
        a_spec = pl.BlockSpec((tk, tm), lambda j, i, k: (k, i))
        dims = (((0,), (0,)), ((), ()))
    else:
        a_spec = pl.BlockSpec((tm, tk), lambda j, i, k: (i, k))
        dims = (((1,), (1 if mode == "nt" else 0,)), ((), ()))
    if mode == "nt":
        b_spec = pl.BlockSpec((tn, tk), lambda j, i, k: (j, k))
    else:
        b_spec = pl.BlockSpec((tk, tn), lambda j, i, k: (k, j))

    o_spec = pl.BlockSpec((tm, tn), lambda j, i, k: (i, j))

    def body(a_ref, b_ref, *rest):
        r_ref = rest[0] if res is not None else None
        o_ref, acc_ref = rest[-2:]
        k = pl.program_id(2)

        @pl.when(k == 0)
        def _():
            acc_ref[...] = jnp.zeros_like(acc_ref)

        acc_ref[...] += lax.dot_general(a_ref[...], b_ref[...], dims, preferred_element_type=F32)

        @pl.when(k == nk - 1)
        def _():
            out = acc_ref[...] if r_ref is None else acc_ref[...] + r_ref[...]
            o_ref[...] = out.astype(out_dtype)

    return pl.pallas_call(
        body, name=name, grid=(N // tn, M // tm, nk),
        in_specs=[a_spec, b_spec] + ([o_spec] if res is not None else []),
        out_specs=o_spec,
        out_shape=jax.ShapeDtypeStruct((M, N), out_dtype),
        scratch_shapes=[pltpu.VMEM((tm, tn), F32)],
        compiler_params=_params(("parallel", "parallel", "arbitrary")),
    )(a, b, *([res] if res is not None else []))


def _out_proj_fwd(a, w, res, gain, name):
    Lp, K = a.shape
    D = w.shape[1]
    tm = _tile(Lp, ROW_TILE)

    def body(a_ref, w_ref, r_ref, g_ref, h_ref, hn_ref, hnt_ref):
        h = jnp.dot(a_ref[...], w_ref[...], preferred_element_type=F32) + r_ref[...]
        h_ref[...] = h
        hn = _rms(h, g_ref[...])
        hn_ref[...] = hn.astype(BF16)
        hnt_ref[...] = hn.T.astype(BF16)

    row = pl.BlockSpec((tm, D), lambda i: (i, 0))
    return pl.pallas_call(
        body, name=name, grid=(Lp // tm,),
        in_specs=[pl.BlockSpec((tm, K), lambda i: (i, 0)), pl.BlockSpec((K, D), lambda i: (0, 0)), row,
                  pl.BlockSpec((1, D), lambda i: (0, 0))],
        out_specs=[row, row, pl.BlockSpec((D, tm), lambda i: (0, i))],
        out_shape=[jax.ShapeDtypeStruct((Lp, D), F32), jax.ShapeDtypeStruct((Lp, D), BF16),
                   jax.ShapeDtypeStruct((D, Lp), BF16)],
        compiler_params=_params(("parallel",)),
    )(a, w, res, gain)


def _in_proj_dx(dy, w, x, gain, dres, name, xchg=(), split_head=False):
    Lp, N = dy.shape
    D = w.shape[0]
    tm = HEAD_ROWS if split_head else _tile(Lp, ROW_TILE // 2)
    nt = Lp // tm
    nx = len(xchg)

    def body(dy_ref, w_ref, x_ref, g_ref, dr_ref, *rest):
        x_in, (dx_ref, dxb_ref, dg_ref), x_out, sems = rest[:nx], rest[nx:nx + 3], rest[nx + 3:2 * nx + 3], rest[2 * nx + 3:]

        @pl.when(pl.program_id(0) == 0)
        def _():
            dg_ref[...] = jnp.zeros_like(dg_ref)
            if nx:
                for cp in _xchg_copies(x_in, x_out, [True] * nx, sems):
                    cp.start()

        dhn = lax.dot_general(dy_ref[...], w_ref[...], (((1,), (1,)), ((), ())), preferred_element_type=F32)
        _, vjp = jax.vjp(_rms, x_ref[...], g_ref[...])
        dx, dg = vjp(dhn)
        dx = dx + dr_ref[...]
        if split_head:
            @pl.when(pl.program_id(0) == 0)
            def _():
                dx_ref[...] = dx

            @pl.when(pl.program_id(0) > 0)
            def _():
                dxb_ref[...] = dx
        else:
            dx_ref[...] = dx
            dxb_ref[...] = dx.astype(BF16)
        dg_ref[...] += dg

        if nx:
            @pl.when(pl.program_id(0) == nt - 1)
            def _():
                for cp in _xchg_copies(x_in, x_out, [True] * nx, sems):
                    cp.wait()

    row = pl.BlockSpec((tm, D), lambda i: (i, 0))
    vec = pl.BlockSpec((1, D), lambda i: (0, 0))
    HBM = pl.BlockSpec(memory_space=pltpu.HBM)
    return pl.pallas_call(
        body, name=name, grid=(nt,),
        in_specs=[pl.BlockSpec((tm, N), lambda i: (i, 0)), pl.BlockSpec((D, N), lambda i: (0, 0)), row, vec, row]
        + [HBM] * nx,
        out_specs=([pl.BlockSpec((tm, D), lambda i: (0, 0)), pl.BlockSpec((tm, D), lambda i: (jnp.maximum(i - 1, 0), 0))]
                   if split_head else [row, row]) + [vec] + [HBM] * nx,
        out_shape=([jax.ShapeDtypeStruct((tm, D), F32), jax.ShapeDtypeStruct((Lp - tm, D), F32)] if split_head else
                   [jax.ShapeDtypeStruct((Lp, D), F32), jax.ShapeDtypeStruct((Lp, D), BF16)])
        + [jax.ShapeDtypeStruct((1, D), F32)] + _xchg_out_shapes(xchg, [True] * nx),
        scratch_shapes=_xchg_sems(nx) if nx else [],
        compiler_params=_params(("arbitrary",)),
    )(dy, w, x, gain, dres, *xchg)


def _embed_rms_fwd(x, meta, g, name):
    S, D = x.shape
    TR = HEAD_ROWS
    Lp = S + TR

    def body(x_ref, m_ref, g_ref, h_ref, o_ref, ot_ref):
        i = pl.program_id(0)

        @pl.when(i == 0)
        def _():
            h_ref[...] = jnp.zeros_like(h_ref)
            h_ref[N_PAD:, :] = m_ref[...]

        @pl.when(i > 0)
        def _():
            h_ref[...] = x_ref[...]

        y = _rms(h_ref[...], g_ref[...])
        o_ref[...] = y.astype(BF16)
        ot_ref[...] = y.T.astype(BF16)

    row = pl.BlockSpec((TR, D), lambda i: (i, 0))
    return pl.pallas_call(
        body, name=name, grid=(Lp // TR,),
        in_specs=[pl.BlockSpec((TR, D), lambda i: (jnp.maximum(i - 1, 0), 0)),
                  pl.BlockSpec((N_META, D), lambda i: (0, 0)), pl.BlockSpec((1, D), lambda i: (0, 0))],
        out_specs=[row, row, pl.BlockSpec((D, TR), lambda i: (0, i))],
        out_shape=[jax.ShapeDtypeStruct((Lp, D), F32), jax.ShapeDtypeStruct((Lp, D), BF16),
                   jax.ShapeDtypeStruct((D, Lp), BF16)],
        compiler_params=_params(("parallel",)),
    )(x, meta, g)


def _swiglu(gate, up):
    return gate * jax.nn.sigmoid(gate) * up


def _ffn_in_fwd(hf, w_in, name):
    Lp, D = hf.shape
    F = w_in.shape[1] // 2
    tm = _tile(Lp, ROW_TILE)
    tn = _tile(F, 1408)
    nj = F // tn

    def body(a_ref, bg_ref, bu_ref, g_ref, u_ref, act_ref, actt_ref):
        a = a_ref[...]
        g = jnp.dot(a, bg_ref[...], preferred_element_type=F32)
        u = jnp.dot(a, bu_ref[...], preferred_element_type=F32)
        g_ref[...] = g.astype(BF16)
        u_ref[...] = u.astype(BF16)
        act = _swiglu(g, u)
        act_ref[...] = act.astype(BF16)
        actt_ref[...] = act.T.astype(BF16)

    tile = pl.BlockSpec((tm, tn), lambda j, i: (i, j))
    return pl.pallas_call(
        body, name=name, grid=(nj, Lp // tm),
        in_specs=[pl.BlockSpec((tm, D), lambda j, i: (i, 0)), pl.BlockSpec((D, tn), lambda j, i: (0, j)),
                  pl.BlockSpec((D, tn), lambda j, i: (0, nj + j))],
        out_specs=[tile, tile, tile, pl.BlockSpec((tn, tm), lambda j, i: (j, i))],
        out_shape=[jax.ShapeDtypeStruct((Lp, F), BF16)] * 3 + [jax.ShapeDtypeStruct((F, Lp), BF16)],
        compiler_params=_params(("parallel", "parallel")),
    )(hf, w_in, w_in)


def _ffn_out_dx(dhb, w_out, g, u, name):
    Lp, D = dhb.shape
    F = w_out.shape[0]
    tm = HEAD_ROWS

    def body(a_ref, b_ref, g_ref, u_ref, o_ref):
        dact = lax.dot_general(a_ref[...], b_ref[...], (((1,), (1,)), ((), ())), preferred_element_type=F32)
        _, vjp = jax.vjp(_swiglu, g_ref[...].astype(F32), u_ref[...].astype(F32))
        dg, du = vjp(dact)
        o_ref[:, :F] = dg.astype(BF16)
        o_ref[:, F:] = du.astype(BF16)

    wide = pl.BlockSpec((tm, F), lambda i: (i, 0))
    return pl.pallas_call(
        body, name=name, grid=(Lp // tm,),
        in_specs=[pl.BlockSpec((tm, D), lambda i: (i, 0)), pl.BlockSpec((F, D), lambda i: (0, 0)), wide, wide],
        out_specs=pl.BlockSpec((tm, 2 * F), lambda i: (i, 0)),
        out_shape=jax.ShapeDtypeStruct((Lp, 2 * F), BF16),
        compiler_params=_params(("parallel",)),
    )(dhb, w_out, g, u)


def _final_loss(h, g, target, name):
    Lp, D = h.shape
    TR = HEAD_ROWS

    def loss_fn(hh, gg, tt):
        err = _rms(hh, gg) - tt
        return 0.5 * jnp.sum(jnp.mean(err * err, axis=-1))

    def body(h_ref, g_ref, t_ref, loss_ref, dh_ref, dhb_ref, dg_ref):
        i = pl.program_id(0)

        @pl.when(i == 0)
        def _():
            loss_ref[...] = jnp.zeros_like(loss_ref)
            dg_ref[...] = jnp.zeros_like(dg_ref)
            dh_ref[...] = jnp.zeros_like(dh_ref)
            dhb_ref[...] = jnp.zeros_like(dhb_ref)

        @pl.when(i > 0)
        def _():
            val, vjp = jax.vjp(lambda hh, gg: loss_fn(hh, gg, t_ref[...]), h_ref[...], g_ref[...])
            dh, dg = vjp(jnp.ones((), F32))
            dh_ref[...] = dh
            dhb_ref[...] = dh.astype(BF16)
            dg_ref[...] += dg
            loss_ref[...] += val

    row = pl.BlockSpec((TR, D), lambda i: (i, 0))
    return pl.pallas_call(
        body, name=name, grid=(Lp // TR,),
        in_specs=[row, pl.BlockSpec((1, D), lambda i: (0, 0)),
                  pl.BlockSpec((TR, D), lambda i: (jnp.maximum(i - 1, 0), 0))],
        out_specs=[pl.BlockSpec((8, LANES), lambda i: (0, 0)), row, row, pl.BlockSpec((1, D), lambda i: (0, 0))],
        out_shape=[jax.ShapeDtypeStruct((8, LANES), F32), jax.ShapeDtypeStruct((Lp, D), F32),
                   jax.ShapeDtypeStruct((Lp, D), BF16), jax.ShapeDtypeStruct((1, D), F32)],
        compiler_params=_params(("arbitrary",)),
    )(h, g, target)


def _lane_lo():
    return lax.broadcasted_iota(jnp.int32, (1, LANES), 1) < FOX_DH


def _headnorm(x, g, scale):
    lo = _lane_lo()
    x2 = x * x
    s0 = jnp.sum(jnp.where(lo, x2, 0.0), axis=-1, keepdims=True)
    s1 = jnp.sum(jnp.where(lo, 0.0, x2), axis=-1, keepdims=True)
    r = jnp.where(lo, lax.rsqrt(s0 / FOX_DH + EPS), lax.rsqrt(s1 / FOX_DH + EPS))
    return x * r * g * scale


AUG = 3


def _split3(x):
    hi = x.astype(BF16).astype(F32)
    mid = (x - hi).astype(BF16).astype(F32)
    return hi, mid, x - hi - mid


def _fox_prep_fwd(proj, bf, qg, kg, T, D, name):
    Lp = proj.shape[0]
    nb = D // LANES
    scale = FOX_DH ** -0.5 * LOG2E

    def body(q_ref, k_ref, v_ref, fl_ref, bf_ref, qg_ref, kg_ref, qn_ref, ka_ref, kb_ref, vb_ref, c_ref, carry_ref):
        i = pl.program_id(0)

        @pl.when(i == 0)
        def _():
            carry_ref[...] = jnp.zeros_like(carry_ref)

        vb_ref[...] = v_ref[...].astype(BF16)
        log_f = _log_sigmoid(fl_ref[...] + bf_ref[...])
        row = lax.broadcasted_iota(jnp.int32, (T, T), 0)
        col = lax.broadcasted_iota(jnp.int32, (T, T), 1)
        tri = (col <= row).astype(F32)
        c = jnp.dot(tri, log_f, precision=HI, preferred_element_type=F32) + carry_ref[...]
        c2 = c * LOG2E
        c_ref[...] = c2
        last = lax.broadcasted_iota(jnp.int32, (T, 1), 0) == T - 1
        carry_ref[...] = jnp.sum(jnp.where(last, c, 0.0), axis=0, keepdims=True)

        is_pad = (i * T + lax.broadcasted_iota(jnp.int32, (T, 1), 0)) < N_PAD
        negc = jnp.where(is_pad, -PAD_SHIFT, -c2)
        lane = lax.broadcasted_iota(jnp.int32, (1, LANES), 1)
        for b in range(nb):
            sl = slice(b * LANES, (b + 1) * LANES)
            qn_ref[:, sl] = _headnorm(q_ref[:, sl], qg_ref[...], scale).astype(BF16)
            kn = _headnorm(k_ref[:, sl], kg_ref[...], 1.0)
            ka = jnp.where(lane < FOX_DH, kn, 0.0)
            kb = jnp.where(lane < FOX_DH, 0.0, kn)
            for n, (pa, pb) in enumerate(zip(_split3(_pick_col(negc, 2 * b)), _split3(_pick_col(negc, 2 * b + 1)))):
                ka = jnp.where(lane == FOX_DH + n, pa, ka)
                kb = jnp.where(lane == n, pb, kb)
            ka_ref[:, sl] = ka.astype(BF16)
            kb_ref[:, sl] = kb.astype(BF16)

    wide = lambda j: pl.BlockSpec((T, D), lambda i: (i, j))
    vec = pl.BlockSpec((1, LANES), lambda i: (0, 0))
    return pl.pallas_call(
        body, name=name, grid=(Lp // T,),
        in_specs=[wide(0), wide(1), wide(2), pl.BlockSpec((T, LANES), lambda i: (i, 4 * nb)), vec, vec, vec],
        out_specs=[wide(0), wide(0), wide(0), wide(0), pl.BlockSpec((T, LANES), lambda i: (i, 0))],
        out_shape=[jax.ShapeDtypeStruct((Lp, D), BF16)] * 4 + [jax.ShapeDtypeStruct((Lp, LANES), F32)],
        scratch_shapes=[pltpu.VMEM((1, LANES), F32)],
        compiler_params=_params(("arbitrary",)),
    )(proj, proj, proj, proj, bf, qg, kg)


def _fox_prep_bwd(proj, bf, qg, kg, dqn, dkn, dv, dgate, dc, T, D, name):
    Lp = proj.shape[0]
    nb = D // LANES
    nt = Lp // T
    scale = FOX_DH ** -0.5 * LOG2E

    def body(q_ref, k_ref, fl_ref, bf_ref, qg_ref, kg_ref, dqn_ref, dkn_ref, dv_ref, dgate_ref, dc_ref,
             dproj_ref, sm_ref, carry_ref):
        @pl.when(pl.program_id(0) == 0)
        def _():
            carry_ref[...] = jnp.zeros_like(carry_ref)
            sm_ref[...] = jnp.zeros_like(sm_ref)

        dqg = jnp.zeros((1, LANES), F32)
        dkg = jnp.zeros((1, LANES), F32)
        for b in range(nb):
            sl = slice(b * LANES, (b + 1) * LANES)
            _, vjp = jax.vjp(lambda x, g: _headnorm(x, g, scale), q_ref[:, sl], qg_ref[...])
            dx, dg = vjp(dqn_ref[:, sl] * LN2)
            dproj_ref[:, sl] = dx.astype(BF16)
            dqg = dqg + dg
            _, vjp = jax.vjp(lambda x, g: _headnorm(x, g, 1.0), k_ref[:, sl], kg_ref[...])
            dx, dg = vjp(dkn_ref[:, sl] * LN2)
            dproj_ref[:, D + b * LANES:D + (b + 1) * LANES] = dx.astype(BF16)
            dkg = dkg + dg
        dproj_ref[:, 2 * D:3 * D] = dv_ref[...].astype(BF16)
        dproj_ref[:, 3 * D:4 * D] = dgate_ref[...]
        dcv = dc_ref[...]
        row = lax.broadcasted_iota(jnp.int32, (T, T), 0)
        col = lax.broadcasted_iota(jnp.int32, (T, T), 1)
        triu = (col >= row).astype(F32)
        dlogf = jnp.dot(triu, dcv, precision=HI, preferred_element_type=F32) + carry_ref[...]
        carry_ref[...] += jnp.sum(dcv, axis=0, keepdims=True)
        _, vjp = jax.vjp(_log_sigmoid, fl_ref[...] + bf_ref[...])
        (dfl,) = vjp(dlogf)
        dproj_ref[:, 4 * D:] = dfl.astype(BF16)
        sm_ref[0:1, :] += jnp.sum(dfl, axis=0, keepdims=True)
        sm_ref[1:2, :] += dqg
        sm_ref[2:3, :] += dkg

    wide = lambda j: pl.BlockSpec((T, D), lambda i: (nt - 1 - i, j))
    narrow = lambda j: pl.BlockSpec((T, LANES), lambda i: (nt - 1 - i, j))
    vec = pl.BlockSpec((1, LANES), lambda i: (0, 0))
    return pl.pallas_call(
        body, name=name, grid=(nt,),
        in_specs=[wide(0), wide(1), narrow(4 * nb), vec, vec, vec, wide(0), wide(0), wide(0), wide(0), narrow(0)],
        out_specs=[pl.BlockSpec((T, 4 * D + LANES), lambda i: (nt - 1 - i, 0)), pl.BlockSpec((8, LANES), lambda i: (0, 0))],
        out_shape=[jax.ShapeDtypeStruct((Lp, 4 * D + LANES), BF16), jax.ShapeDtypeStruct((8, LANES), F32)],
        scratch_shapes=[pltpu.VMEM((1, LANES), F32)],
        compiler_params=_params(("arbitrary",)),
    )(proj, proj, proj, bf, qg, kg, dqn, dkn, dv, dgate, dc)


def _fox_q_operands(q):
    lane = lax.broadcasted_iota(jnp.int32, (1, LANES), 1)
    zero, one = jnp.zeros_like(q), jnp.ones_like(q)
    return (jnp.where(lane < FOX_DH, q, jnp.where(lane < FOX_DH + AUG, one, zero)),
            jnp.where(lane < FOX_DH, jnp.where(lane < AUG, one, zero), q))


def _fox_mask(i, k0, T):
    qpos = i * T + lax.broadcasted_iota(jnp.int32, (T, 1), 0)
    kpos = k0 + lax.broadcasted_iota(jnp.int32, (1, T), 1)
    return (kpos <= qpos) & ((kpos >= N_PAD) | (qpos < N_PAD))


def _pick_col(blk, idx):
    lane = lax.broadcasted_iota(jnp.int32, (1, LANES), 1)
    return jnp.sum(jnp.where(lane == idx, blk, 0.0), axis=1, keepdims=True)


def _split_halves(blk):
    lo = _lane_lo()
    return (jnp.max(jnp.where(lo, blk, -jnp.inf), axis=1, keepdims=True),
            jnp.max(jnp.where(lo, -jnp.inf, blk), axis=1, keepdims=True))


def _fox_attn_fwd(qn, ka, kb, vb, c, proj, xchg, T, D, name):
    Lp = qn.shape[0]
    P = D // LANES
    nt = Lp // T
    nx = len(xchg)

    def body(q_ref, ka_ref, kb_ref, v_ref, c_ref, g_ref, *rest):
        x_in, (o_ref, og_ref, m_ref, li_ref), x_out, sems = rest[:nx], rest[nx:nx + 4], rest[nx + 4:2 * nx + 4], rest[2 * nx + 4:]
        p = pl.program_id(0)
        i = pl.program_id(1)

        @pl.when((p == 0) & (i == 0))
        def _():
            for cp in _xchg_copies(x_in, x_out, [False] * nx, sems):
                cp.start()

        lo = _lane_lo()
        q = q_ref[...]
        qh = _fox_q_operands(q)
        cblk = c_ref[...]
        cq = tuple(_pick_col(cblk, 2 * p + h) for h in (0, 1))
        one = jnp.ones_like(q)

        def step(j, carry, masked):
            k0 = pl.multiple_of(j * T, LANES)
            kj = (ka_ref[pl.ds(k0, T), :], kb_ref[pl.ds(k0, T), :])
            vj = v_ref[pl.ds(k0, T), :]
            vh = (jnp.where(lo, vj, one), jnp.where(lo, one, vj))
            mask = _fox_mask(i, k0, T) if masked else None
            out = []
            for h in (0, 1):
                m, acc = carry[h]
                t = lax.dot_general(qh[h], kj[h], (((1,), (1,)), ((), ())), preferred_element_type=F32)
                if masked:
                    t = jnp.where(mask, t, NEG)
                m_new = jnp.ceil(jnp.maximum(m, cq[h] + jnp.max(t, axis=1, keepdims=True)))
                pr = jnp.exp2(t + (cq[h] - m_new)).astype(BF16)
                acc = jnp.exp2(m - m_new) * acc + jnp.dot(pr, vh[h], preferred_element_type=F32)
                out.append((m_new, acc))
            return tuple(out)

        init = tuple((jnp.full((T, 1), NEG, F32), jnp.zeros((T, LANES), F32)) for _ in (0, 1))
        carry = lax.fori_loop(0, i, lambda j, cr: step(j, cr, False), init)
        (m0, a0), (m1, a1) = step(i, carry, True)
        l0 = pltpu.roll(a0, FOX_DH, 1)
        l1 = pltpu.roll(a1, FOX_DH, 1)
        o = jnp.where(lo, a0 / l0, a1 / l1)
        o_ref[...] = o
        m_ref[...] = jnp.where(lo, m0, m1)
        li_ref[...] = jnp.where(lo, 1.0 / l0, 1.0 / l1)
        og_ref[...] = (o * jax.nn.sigmoid(g_ref[...])).astype(BF16)

        @pl.when((p == P - 1) & (i == nt - 1))
        def _():
            for cp in _xchg_copies(x_in, x_out, [False] * nx, sems):
                cp.wait()

    tile = pl.BlockSpec((T, LANES), lambda p, i: (i, p))
    full = pl.BlockSpec((Lp, LANES), lambda p, i: (0, p))
    HBM = pl.BlockSpec(memory_space=pltpu.HBM)
    return pl.pallas_call(
        body, name=name, grid=(P, nt),
        in_specs=[tile, full, full, full, pl.BlockSpec((T, LANES), lambda p, i: (i, 0)),
                  pl.BlockSpec((T, LANES), lambda p, i: (i, 3 * P + p))] + [HBM] * nx,
        out_specs=[tile, tile, tile, tile] + [HBM] * nx,
        out_shape=[jax.ShapeDtypeStruct((Lp, D), F32), jax.ShapeDtypeStruct((Lp, D), BF16),
                   jax.ShapeDtypeStruct((Lp, D), F32), jax.ShapeDtypeStruct((Lp, D), F32)]
        + _xchg_out_shapes(xchg, [False] * nx),
        scratch_shapes=_xchg_sems(nx),
        compiler_params=_params(("arbitrary", "arbitrary")),
    )(qn, ka, kb, vb, c, proj, *xchg)


def _fox_out_dx(dhb, w_out, o, proj, linv, D, name):
    Lp = o.shape[0]
    tm = HEAD_ROWS

    def body(a_ref, w_ref, o_ref, g_ref, li_ref, do_ref, dg_ref, dl_ref):
        lo = _lane_lo()
        dog_all = lax.dot_general(a_ref[...], w_ref[...], (((1,), (1,)), ((), ())), preferred_element_type=F32)
        for b in range(D // LANES):
            sl = slice(b * LANES, (b + 1) * LANES)
            dog = dog_all[:, sl]
            sig = jax.nn.sigmoid(g_ref[:, sl])
            ov = o_ref[:, sl]
            do = (dog * sig * li_ref[:, sl]).astype(BF16)
            do_ref[:, sl] = do
            dg_ref[:, sl] = (dog * ov * sig * (1.0 - sig)).astype(BF16)
            t = do.astype(F32) * ov
            d0 = jnp.sum(jnp.where(lo, t, 0.0), axis=1, keepdims=True)
            d1 = jnp.sum(jnp.where(lo, 0.0, t), axis=1, keepdims=True)
            dl_ref[:, sl] = jnp.where(lo, d0, d1)

    row = pl.BlockSpec((tm, D), lambda i: (i, 0))
    return pl.pallas_call(
        body, name=name, grid=(Lp // tm,),
        in_specs=[row, pl.BlockSpec((D, D), lambda i: (0, 0)), row, pl.BlockSpec((tm, D), lambda i: (i, 3)), row],
        out_specs=[row, row, row],
        out_shape=[jax.ShapeDtypeStruct((Lp, D), BF16), jax.ShapeDtypeStruct((Lp, D), BF16),
                   jax.ShapeDtypeStruct((Lp, D), F32)],
        compiler_params=_params(("parallel",)),
    )(dhb, w_out, o, proj, linv)


def _fox_attn_bwd(qn, ka, kb, vb, c, do, mshift, delta, xchg, T, D, name):
    Lp = qn.shape[0]
    P = D // LANES
    nt = Lp // T
    nx = len(xchg)

    def body(q_ref, do_ref, m_ref, dl_ref, c_ref, ka_ref, kb_ref, v_ref, *rest):
        x_in, (dq_ref, dk_ref, dv_ref, dc_ref), x_out, sems = rest[:nx], rest[nx:nx + 4], rest[nx + 4:2 * nx + 4], rest[2 * nx + 4:]
        p = pl.program_id(0)
        i = pl.program_id(1)

        @pl.when((p == 0) & (i == 0))
        def _():
            for cp in _xchg_copies(x_in, x_out, [True] * nx, sems):
                cp.start()

        @pl.when(i == 0)
        def _():
            dk_ref[...] = jnp.zeros_like(dk_ref)
            dv_ref[...] = jnp.zeros_like(dv_ref)
            dc_ref[...] = jnp.zeros_like(dc_ref)

        lo = _lane_lo()
        q = q_ref[...]
        do = do_ref[...]
        zero = jnp.zeros_like(q)
        qh = _fox_q_operands(q)
        doh = (jnp.where(lo, do, zero), jnp.where(lo, zero, do))
        msh = _split_halves(m_ref[...])
        dlt = _split_halves(dl_ref[...])
        cblk = c_ref[...]
        shift = tuple(_pick_col(cblk, 2 * p + h) - msh[h] for h in (0, 1))

        def step(j, carry, masked):
            k0 = pl.multiple_of(j * T, LANES)
            kj = (ka_ref[pl.ds(k0, T), :], kb_ref[pl.ds(k0, T), :])
            vj = v_ref[pl.ds(k0, T), :]
            mask = _fox_mask(i, k0, T) if masked else None
            dqs, dks, dvs = [], [], []
            for h in (0, 1):
                t = lax.dot_general(qh[h], kj[h], (((1,), (1,)), ((), ())), preferred_element_type=F32)
                if masked:
                    t = jnp.where(mask, t, NEG)
                pb = jnp.exp2(t + shift[h]).astype(BF16)
                dp = lax.dot_general(doh[h], vj, (((1,), (1,)), ((), ())), preferred_element_type=F32)
                ds = pb.astype(F32) * (dp - dlt[h])
                dsb = ds.astype(BF16)
                dqs.append(carry[h] + jnp.dot(dsb, kj[h], preferred_element_type=F32))
                dks.append(lax.dot_general(dsb, q, (((0,), (0,)), ((), ())), preferred_element_type=F32))
                dvs.append(lax.dot_general(pb, do, (((0,), (0,)), ((), ())), preferred_element_type=F32))
                dc_ref[0, h:h + 1, pl.ds(k0, T)] += -jnp.sum(ds, axis=0, keepdims=True)
            dk_ref[pl.ds(k0, T), :] += jnp.where(lo, dks[0], dks[1])
            dv_ref[pl.ds(k0, T), :] += jnp.where(lo, dvs[0], dvs[1])
            return tuple(dqs)

        init = (jnp.zeros((T, LANES), F32), jnp.zeros((T, LANES), F32))
        carry = lax.fori_loop(0, i, lambda j, cr: step(j, cr, False), init)
        dq0, dq1 = step(i, carry, True)
        dq_ref[...] = jnp.where(lo, dq0, dq1)

        @pl.when((p == P - 1) & (i == nt - 1))
        def _():
            for cp in _xchg_copies(x_in, x_out, [True] * nx, sems):
                cp.wait()

    tile = pl.BlockSpec((T, LANES), lambda p, i: (i, p))
    full = pl.BlockSpec((Lp, LANES), lambda p, i: (0, p))
    HBM = pl.BlockSpec(memory_space=pltpu.HBM)
    return pl.pallas_call(
        body, name=name, grid=(P, nt),
        in_specs=[tile, tile, tile, tile, pl.BlockSpec((T, LANES), lambda p, i: (i, 0)), full, full, full]
        + [HBM] * nx,
        out_specs=[tile, full, full, pl.BlockSpec((1, 8, Lp), lambda p, i: (p, 0, 0))] + [HBM] * nx,
        out_shape=[jax.ShapeDtypeStruct((Lp, D), F32)] * 3 + [jax.ShapeDtypeStruct((P, 8, Lp), F32)]
        + _xchg_out_shapes(xchg, [True] * nx),
        scratch_shapes=_xchg_sems(nx),
        compiler_params=_params(("arbitrary", "arbitrary")),
    )(qn, do, mshift, delta, c, ka, kb, vb, *xchg)


def _scan_rows(x, reverse):
    C = x.shape[0]
    row = lax.broadcasted_iota(jnp.int32, (C, 1), 0)
    step = 1
    while step < C:
        if reverse:
            x = x + jnp.where(row < C - step, pltpu.roll(x, C - step, 0), 0.0)
        else:
            x = x + jnp.where(row >= step, pltpu.roll(x, step, 0), 0.0)
        step *= 2
    return x


@jax.custom_vjp
def _cumsum_rows(x):
    return _scan_rows(x, False)


_cumsum_rows.defvjp(lambda x: (_scan_rows(x, False), None), lambda _, g: (_scan_rows(g, True),))


def _hgrn_chunk(St, qr, z, vi, go, p0, p1, gg):
    C = qr.shape[0]
    lb = jax.nn.sigmoid(p1 - p0)
    a = jnp.log(lb)
    cc = jnp.log1p(-lb) + _log_sigmoid(z)
    log_f = jnp.maximum(a, cc) + jnp.log1p(jnp.exp(-jnp.abs(a - cc)))
    k = (1.0 - lb) * jax.nn.sigmoid(-z)
    q = qr * jax.nn.sigmoid(qr)
    row = lax.broadcasted_iota(jnp.int32, (C, C), 0)
    col = lax.broadcasted_iota(jnp.int32, (C, C), 1)
    causal = col <= row
    b = _cumsum_rows(log_f)
    mid = lax.broadcasted_iota(jnp.int32, (C, 1), 0) == C // 2 - 1
    r = jnp.sum(jnp.where(mid, b, 0.0), axis=0, keepdims=True)
    b_last = jnp.sum(log_f, axis=0, keepdims=True)
    attn = jnp.where(causal, _d_nt(q * jnp.exp(b - r), k * jnp.exp(r - b)), 0.0)
    o = _d_nn(attn, vi) + _d_nt(q * jnp.exp(b), St)
    St_new = St * jnp.exp(b_last) + _d_tn(vi, k * jnp.exp(b_last - b))
    og = _rms(o, gg) * (go * jax.nn.sigmoid(go))
    return St_new, og


def _hgrn_heads_per_step(H):
    return 8 if H % 8 == 0 else 4 if H % 4 == 0 else 1


def _hgrn_specs(T, W, nhb, rev_nt=None):
    if rev_nt is None:
        return [pl.BlockSpec((T, W), functools.partial(lambda hb, t, g: (t, g * nhb + hb), g=g)) for g in range(4)]
    return [pl.BlockSpec((T, W), functools.partial(lambda hb, t, g: (rev_nt - 1 - t, g * nhb + hb), g=g))
            for g in range(4)]


def _hgrn_fwd(proj, lbp, gg, T, name):
    Lp = proj.shape[0]
    D = proj.shape[1] // 4
    H = D // LANES
    hps = _hgrn_heads_per_step(H)
    W = hps * LANES
    nhb = H // hps
    nt = Lp // T
    ncc = T // HGRN_CHUNK

    def body(q_ref, z_ref, i_ref, go_ref, p_ref, gg_ref, og_ref, ss_ref, st_ref):
        @pl.when(pl.program_id(1) == 0)
        def _():
            st_ref[...] = jnp.zeros_like(st_ref)

        gain = gg_ref[...]

        def chunk(cidx, states):
            sl = pl.ds(pl.multiple_of(cidx * HGRN_CHUNK, HGRN_CHUNK), HGRN_CHUNK)
            new = []
            for hh in range(hps):
                ln = slice(hh * LANES, (hh + 1) * LANES)
                ss_ref[hh, cidx] = states[hh]
                St_new, og = _hgrn_chunk(states[hh], q_ref[sl, ln], z_ref[sl, ln], i_ref[sl, ln], go_ref[sl, ln],
                                         p_ref[0:1, ln], p_ref[1:2, ln], gain)
                og_ref[sl, ln] = og.astype(BF16)
                new.append(St_new)
            return tuple(new)

        assert ncc % 2 == 0
        states = lax.fori_loop(0, ncc // 2, lambda c2, st: chunk(2 * c2 + 1, chunk(2 * c2, st)),
                               tuple(st_ref[hh] for hh in range(hps)))
        for hh in range(hps):
            st_ref[hh] = states[hh]

    return pl.pallas_call(
        body, name=name, grid=(nhb, nt),
        in_specs=_hgrn_specs(T, W, nhb) + [pl.BlockSpec((2, W), lambda hb, t: (0, hb)),
                                           pl.BlockSpec((1, LANES), lambda hb, t: (0, 0))],
        out_specs=[pl.BlockSpec((T, W), lambda hb, t: (t, hb)),
                   pl.BlockSpec((hps, ncc, LANES, LANES), lambda hb, t: (hb, t, 0, 0))],
        out_shape=[jax.ShapeDtypeStruct((Lp, D), BF16),
                   jax.ShapeDtypeStruct((H, Lp // HGRN_CHUNK, LANES, LANES), F32)],
        scratch_shapes=[pltpu.VMEM((hps, LANES, LANES), F32)],
        compiler_params=_params(("parallel", "arbitrary")),
    )(proj, proj, proj, proj, lbp, gg)


def _hgrn_bwd(proj, lbp, gg, dog, ss, T, name):
    Lp = proj.shape[0]
    D = proj.shape[1] // 4
    H = D // LANES
    hps = _hgrn_heads_per_step(H)
    W = hps * LANES
    nhb = H // hps
    assert nhb == 1, "d proj is written as whole rows: every head in one grid step"
    nt = Lp // T
    ncc = T // HGRN_CHUNK

    def body(q_ref, z_ref, i_ref, go_ref, p_ref, gg_ref, dog_ref, ss_ref, dproj_ref, dp_ref, dgg_ref, dst_ref):
        hb = pl.program_id(0)
        t = pl.program_id(1)

        @pl.when(t == 0)
        def _():
            dst_ref[...] = jnp.zeros_like(dst_ref)
            dp_ref[...] = jnp.zeros_like(dp_ref)

        @pl.when((t == 0) & (hb == 0))
        def _():
            dgg_ref[...] = jnp.zeros_like(dgg_ref)

        gain = gg_ref[...]
        row0 = (nt - 1 - t) * T

        def chunk(cc, carry):
            dstates, dps, dgain_sum = carry
            cidx = ncc - 1 - cc
            r0 = pl.multiple_of(cidx * HGRN_CHUNK, HGRN_CHUNK)
            sl = pl.ds(r0, HGRN_CHUNK)
            real = (row0 + r0 + lax.broadcasted_iota(jnp.int32, (HGRN_CHUNK, 1), 0)) >= N_PAD
            new_d, new_p = [], []
            for hh in range(hps):
                ln = slice(hh * LANES, (hh + 1) * LANES)
                _, vjp = jax.vjp(_hgrn_chunk, ss_ref[hh, cidx], q_ref[sl, ln], z_ref[sl, ln], i_ref[sl, ln],
                                 go_ref[sl, ln], p_ref[0:1, ln], p_ref[1:2, ln], gain)
                dSt, dq, dz, di, dgo, dp0, dp1, dgain = vjp((dstates[hh], dog_ref[sl, ln]))
                for grp, dval in enumerate((dq, dz, di, dgo)):
                    dproj_ref[sl, grp * D + hh * LANES:grp * D + (hh + 1) * LANES] = (
                        jnp.where(real, dval, 0.0).astype(BF16))
                new_d.append(dSt)
                new_p.append((dps[hh][0] + dp0, dps[hh][1] + dp1))
                dgain_sum = dgain_sum + dgain
            return tuple(new_d), tuple(new_p), dgain_sum

        zero_row = jnp.zeros((1, LANES), F32)
        init = (tuple(dst_ref[hh] for hh in range(hps)), tuple((zero_row, zero_row) for _ in range(hps)), zero_row)
        dstates, dps, dgain_sum = lax.fori_loop(0, ncc, chunk, init)
        for hh in range(hps):
            ln = slice(hh * LANES, (hh + 1) * LANES)
            dst_ref[hh] = dstates[hh]
            dp_ref[0:1, ln] += dps[hh][0]
            dp_ref[1:2, ln] += dps[hh][1]
        dgg_ref[0:1, :] += dgain_sum

    rev = pl.BlockSpec((T, W), lambda hb, t: (nt - 1 - t, hb))
    return pl.pallas_call(
        body, name=name, grid=(nhb, nt),
        in_specs=_hgrn_specs(T, W, nhb, nt) + [pl.BlockSpec((2, W), lambda hb, t: (0, hb)),
                                               pl.BlockSpec((1, LANES), lambda hb, t: (0, 0)), rev,
                                               pl.BlockSpec((hps, ncc, LANES, LANES),
                                                            lambda hb, t: (hb, nt - 1 - t, 0, 0))],
        out_specs=[pl.BlockSpec((T, 4 * D), lambda hb, t: (nt - 1 - t, 0)), pl.BlockSpec((8, W), lambda hb, t: (0, hb)),
                   pl.BlockSpec((8, LANES), lambda hb, t: (0, 0))],
        out_shape=[jax.ShapeDtypeStruct((Lp, 4 * D), BF16), jax.ShapeDtypeStruct((8, D), F32),
                   jax.ShapeDtypeStruct((8, LANES), F32)],
        scratch_shapes=[pltpu.VMEM((hps, LANES, LANES), F32)],
        compiler_params=_params(("arbitrary", "arbitrary")),
    )(proj, proj, proj, proj, lbp, gg, dog, ss)


def _xchg_sems(n_arr):
    return [pltpu.SemaphoreType.DMA((n_arr * (N_DEV - 1),)), pltpu.SemaphoreType.DMA((n_arr * (N_DEV - 1),)),
            pltpu.SemaphoreType.DMA((n_arr,))]


def _xchg_copies(ins, outs, per_peer, sems):
    send_sems, recv_sems, local_sems = sems
    x, y, c = lax.axis_index("x"), lax.axis_index("y"), lax.axis_index("c")
    me = 4 * x + 2 * y + c
    copies = []
    for n in range(len(ins)):
        src = ins[n].at[me] if per_peer[n] else ins[n]
        copies.append(pltpu.make_async_copy(src, outs[n].at[me], local_sems.at[n]))
    for rel in range(1, N_DEV):
        fx, fy, fc = (rel >> 2) & 1, (rel >> 1) & 1, rel & 1
        px = 1 - x if fx else x
        py = 1 - y if fy else y
        pc = 1 - c if fc else c
        peer = 4 * px + 2 * py + pc
        for n in range(len(ins)):
            src = ins[n].at[peer] if per_peer[n] else ins[n]
            copies.append(pltpu.make_async_remote_copy(
                src_ref=src, dst_ref=outs[n].at[me],
                send_sem=send_sems.at[n * (N_DEV - 1) + rel - 1],
                recv_sem=recv_sems.at[n * (N_DEV - 1) + rel - 1],
                device_id=(px, py, pc), device_id_type=pl.DeviceIdType.MESH))
    return copies


def _xchg_out_shapes(arrays, per_peer):
    return [jax.ShapeDtypeStruct(a.shape if pp else (N_DEV,) + a.shape, a.dtype) for a, pp in zip(arrays, per_peer)]


def _exchange(arrays, per_peer, name):
    n_arr = len(arrays)
    HBM = pl.BlockSpec(memory_space=pltpu.HBM)

    def body(*refs):
        copies = _xchg_copies(refs[:n_arr], refs[n_arr:2 * n_arr], per_peer, refs[2 * n_arr:])
        for cp in copies:
            cp.start()
        for cp in copies:
            cp.wait()

    return pl.pallas_call(
        body, name=name,
        in_specs=[HBM] * n_arr, out_specs=[HBM] * n_arr, out_shape=_xchg_out_shapes(arrays, per_peer),
        scratch_shapes=_xchg_sems(n_arr),
    )(*arrays)


def _gather_two_level(arrays, name):
    n_arr = len(arrays)
    HBM = pl.BlockSpec(memory_space=pltpu.HBM)

    def body(*refs):
        ins, outs = refs[:n_arr], refs[n_arr:2 * n_arr]
        send_sems, recv_sems, local_sems = refs[2 * n_arr:]
        x, y, c = lax.axis_index("x"), lax.axis_index("y"), lax.axis_index("c")
        sibling = (x, y, 1 - c)
        chips = [(1 - x, y), (x, 1 - y), (1 - x, 1 - y)]
        slot = lambda px, py, pc: 4 * px + 2 * py + pc

        def copy(a, k, block, to, own=False):
            return pltpu.make_async_remote_copy(
                src_ref=ins[a] if own else outs[a].at[slot(*block)], dst_ref=outs[a].at[slot(*block)],
                send_sem=send_sems.at[a * (N_DEV - 1) + k], recv_sem=recv_sems.at[a * (N_DEV - 1) + k],
                device_id=to, device_id_type=pl.DeviceIdType.MESH)

        me = (x, y, c)
        local = [pltpu.make_async_copy(ins[a], outs[a].at[slot(*me)], local_sems.at[a]) for a in range(n_arr)]
        first = [copy(a, 0, me, sibling, own=True) for a in range(n_arr)]
        first += [copy(a, 1 + j, me, (*chip, c), own=True) for j, chip in enumerate(chips) for a in range(n_arr)]
        for cp in local + first:
            cp.start()
        passed = []
        for j, chip in enumerate(chips):
            for a in range(n_arr):
                copy(a, 1 + j, (*chip, c), me).wait_recv()
                cp = copy(a, 4 + j, (*chip, c), sibling)
                cp.start()
                passed.append(cp)
        for a in range(n_arr):
            copy(a, 0, sibling, me).wait_recv()
            for j, chip in enumerate(chips):
                copy(a, 4 + j, (*chip, 1 - c), me).wait_recv()
        for cp in first + passed:
            cp.wait_send()
        for cp in local:
            cp.wait()

    return pl.pallas_call(
        body, name=name,
        in_specs=[HBM] * n_arr, out_specs=[HBM] * n_arr, out_shape=_xchg_out_shapes(arrays, [False] * n_arr),
        scratch_shapes=_xchg_sems(n_arr),
    )(*arrays)


ADAMW_VMEM_BUDGET = 36 * 1024 * 1024


def _adamw(recv, w, m, v, name):
    shape = w.shape
    C = shape[-1]
    R = math.prod(shape[:-1])
    lanes = -(-C // LANES) * LANES
    row_bytes = 2 * lanes * (N_DEV * recv.dtype.itemsize + 7 * 4)
    rc = _row_chunk(R, max(16, ADAMW_VMEM_BUDGET // row_bytes), 16 if recv.dtype == BF16 else 8)

    def body(r_ref, w_ref, m_ref, v_ref, g_ref, d_ref, mo_ref, vo_ref):
        g = r_ref[0].astype(F32)
        for s in range(1, N_DEV):
            g = g + r_ref[s].astype(F32)
        mn = ADAM_B1 * m_ref[...] + (1.0 - ADAM_B1) * g
        vn = ADAM_B2 * v_ref[...] + (1.0 - ADAM_B2) * (g * g)
        m_hat = mn / (1.0 - ADAM_B1 ** ADAM_STEP)
        v_hat = vn / (1.0 - ADAM_B2 ** ADAM_STEP)
        g_ref[...] = g
        d_ref[...] = -ADAM_LR * (m_hat / (jnp.sqrt(v_hat) + ADAM_EPS) + ADAM_WD * w_ref[...])
        mo_ref[...] = mn
        vo_ref[...] = vn

    row = pl.BlockSpec((rc, C), lambda i: (i, 0))
    outs = pl.pallas_call(
        body, name=name, grid=(R // rc,),
        in_specs=[pl.BlockSpec((N_DEV, rc, C), lambda i: (0, i, 0)), row, row, row],
        out_specs=[row] * 4,
        out_shape=[jax.ShapeDtypeStruct((R, C), F32)] * 4,
        compiler_params=_params(("parallel",)),
    )(recv.reshape(N_DEV, R, C), w.reshape(R, C), m.reshape(R, C), v.reshape(R, C))
    return [o.reshape(shape) for o in outs]


def _gathered_to_full(g, name):
    if name in COL_SHARDED:
        g = jnp.moveaxis(g, 0, -2)
        return g.reshape(g.shape[:-2] + (g.shape[-2] * g.shape[-1],))
    g = jnp.moveaxis(g, 0, -3)
    return g.reshape(g.shape[:-3] + (g.shape[-3] * g.shape[-2], g.shape[-1]))


def _full_to_slabs(full, name):
    if name in COL_SHARDED:
        f = full.reshape(full.shape[:-1] + (N_DEV, full.shape[-1] // N_DEV))
        return jnp.moveaxis(f, -2, 0)
    f = full.reshape(full.shape[:-2] + (N_DEV, full.shape[-2] // N_DEV, full.shape[-1]))
    return jnp.moveaxis(f, -3, 0)


def _pack_small(arrs):
    rows = []
    for a in arrs:
        flat = a.astype(F32).reshape(-1)
        pad = (-flat.shape[0]) % LANES
        rows.append(jnp.pad(flat, (0, pad)).reshape(-1, LANES))
    p = jnp.concatenate(rows, axis=0)
    return jnp.pad(p, ((0, (-p.shape[0]) % 8), (0, 0)))


def _unpack_small(packed, shapes):
    out, off = [], 0
    for shp in shapes:
        n = math.prod(shp)
        nr = -(-n // LANES)
        out.append(packed[off:off + nr].reshape(-1)[:n].reshape(shp))
        off += nr
    return out


def _local_step(x, target, meta, w_fox_in, late, small):
    S, D = x.shape
    Lp = S + HEAD_ROWS
    T = ROW_TILE if Lp % ROW_TILE == 0 else HEAD_ROWS
    P = D // LANES
    row = lambda v: v.reshape(1, -1).astype(F32)

    w_fin = jnp.pad(w_fox_in[0], ((0, 0), (0, LANES - w_fox_in.shape[-1] % LANES)))
    n_heads = w_fox_in.shape[-1] - 4 * D
    bf = jnp.pad(row(small["fox_b_f"]), ((0, 0), (0, LANES - small["fox_b_f"].size)))
    qg = jnp.tile(row(small["fox_q_norm"]), (1, 2))
    kg = jnp.tile(row(small["fox_k_norm"]), (1, 2))


    h0, hn0, hn0t = _embed_rms_fwd(x, meta, row(small["attn_norm"][0]), "rms0_fwd")
    proj0 = _mm(hn0, w_fin, "nn", F32, "fox_in_fwd")
    qn, ka, kb, vb, c = _fox_prep_fwd(proj0, bf, qg, kg, T, D, "fox_prep_fwd")
    o, og0, mshift, linv, *gathered = _fox_attn_fwd(qn, ka, kb, vb, c, proj0, [late[n] for n in LATE], T, D,
                                                    "fox_attn_fwd")
    wl = {n: _gathered_to_full(g, n) for n, g in zip(LATE, gathered)}
    w_fout, w_hin, w_hout = wl["fox_w_out"][0], wl["hgrn_w_in"][0], wl["hgrn_w_out"][0]
    w_uin, w_uout = wl["ffn_w_in"], wl["ffn_w_out"]
    h1, hf0, hf0t = _out_proj_fwd(og0, w_fout, h0, row(small["ffn_norm"][0]), "fox_out_fwd")
    gu0 = _ffn_in_fwd(hf0, w_uin[0], "ffn0_in_fwd")
    act0 = gu0[2]
    h2, hn1, hn1t = _out_proj_fwd(act0, w_uout[0], h1, row(small["attn_norm"][1]), "ffn0_out_fwd")
    proj1 = _mm(hn1, w_hin, "nn", F32, "hgrn_in_fwd")
    lbp = small["hgrn_lower_bounds"].astype(F32)
    ggn = row(small["hgrn_g_norm"])
    Th = HGRN_TILE if Lp % HGRN_TILE == 0 else HEAD_ROWS
    og1, ss = _hgrn_fwd(proj1, lbp, ggn, Th, "hgrn_fwd")
    h3, hf1, hf1t = _out_proj_fwd(og1, w_hout, h2, row(small["ffn_norm"][1]), "hgrn_out_fwd")
    gu1 = _ffn_in_fwd(hf1, w_uin[1], "ffn1_in_fwd")
    act1 = gu1[2]
    h4 = _mm(act1, w_uout[1], "nn", F32, "ffn1_out_fwd", res=h3)
    loss_blk, dh4, dh4b, d_final = _final_loss(h4, row(small["final_norm"]), target, "final_loss")

    grads = {}

    def ffn_bwd(i, dh, dhb, h_in, hft, gu):
        grads_out = _mm(gu[3], dhb, "nn", F32, f"ffn{i}_out_dw", tm=_tile(gu[3].shape[0], 1408), tk=_tile(Lp, 1408))
        dgu = _ffn_out_dx(dhb, w_uout[i], gu[0], gu[1], f"ffn{i}_out_dx")
        grads_in = _mm(hft, dgu, "nn", F32, f"ffn{i}_in_dw", tm=D, tk=_tile(Lp, 1408))
        dh_new, dh_newb, dgain = _in_proj_dx(dgu, w_uin[i], h_in, row(small["ffn_norm"][i]), dh, f"ffn{i}_in_dx")
        return dh_new, dh_newb, grads_in, grads_out, dgain

    dh3, dh3b, g_uin1, g_uout1, d_fn1 = ffn_bwd(1, dh4, dh4b, h3, hf1t, gu1)
    grads["hgrn_w_out"] = _mm(og1, dh3b, "tn", F32, "hgrn_out_dw")[None]
    dog1 = _mm(dh3b, w_hout, "nt", F32, "hgrn_out_dx")
    dproj1, d_lb, d_gg = _hgrn_bwd(proj1, lbp, ggn, dog1, ss, Th, "hgrn_bwd")
    grads["hgrn_w_in"] = _mm(hn1t, dproj1, "nn", F32, "hgrn_in_dw", tm=D, tk=_tile(Lp, 1408))[None]
    dh2, dh2b, d_an1 = _in_proj_dx(dproj1, w_hin, h2, row(small["attn_norm"][1]), dh3, "hgrn_in_dx")
    dh1, dh1b, g_uin0, g_uout0, d_fn0 = ffn_bwd(0, dh2, dh2b, h1, hf0t, gu0)
    grads["ffn_w_in"] = jnp.stack([g_uin0, g_uin1])
    grads["ffn_w_out"] = jnp.stack([g_uout0, g_uout1])
    grads["fox_w_out"] = _mm(og0, dh1b, "tn", F32, "fox_out_dw")[None]
    do, dgate, delta = _fox_out_dx(dh1b, w_fout, o, proj0, linv, D, "fox_out_dx")
    slabs = [_full_to_slabs(grads[n], n).astype(BF16) for n in LATE]
    dqn, dkn, dv, dcr, *recv = _fox_attn_bwd(qn, ka, kb, vb, c, do, mshift, delta, slabs, T, D, "fox_attn_bwd")
    for n in LATE:
        del grads[n]
    dc = jnp.pad(dcr[:, :2, :].reshape(2 * P, Lp).T, ((0, 0), (0, LANES - 2 * P)))
    Tp = T // 2 if T == ROW_TILE else T
    dproj0, sm = _fox_prep_bwd(proj0, bf, qg, kg, dqn, dkn, dv, dgate, dc, Tp, D, "fox_prep_bwd")
    g_fin = _mm(hn0t, dproj0, "nn", F32, "fox_in_dw", tm=D, tk=_tile(Lp, 1408))[:, :4 * D + n_heads][None]
    dh_head, grad_x, d_an0, r_fin = _in_proj_dx(dproj0, w_fin, h0, row(small["attn_norm"][0]), dh1, "fox_in_dx",
                                                xchg=[_full_to_slabs(g_fin, "fox_w_in").astype(BF16)], split_head=True)

    grads["meta_tokens"] = dh_head[N_PAD:]
    grads["attn_norm"] = jnp.concatenate([d_an0, d_an1], axis=0)
    grads["ffn_norm"] = jnp.concatenate([d_fn0, d_fn1], axis=0)
    grads["final_norm"] = d_final[0]
    grads["fox_b_f"] = sm[0:1, :n_heads]
    grads["fox_q_norm"] = sm[1:2, :FOX_DH] + sm[1:2, FOX_DH:]
    grads["fox_k_norm"] = sm[2:3, :FOX_DH] + sm[2:3, FOX_DH:]
    grads["hgrn_lower_bounds"] = d_lb[0:2]
    grads["hgrn_g_norm"] = d_gg[0:1]
    return loss_blk[0, 0], grad_x, grads, dict(zip(LATE, recv), fox_w_in=r_fin)


def kernel(x, meta_tokens, attn_norm, ffn_norm, final_norm, fox_w_in, fox_b_f, fox_q_norm, fox_k_norm, fox_w_out, hgrn_w_in, hgrn_lower_bounds, hgrn_g_norm, hgrn_w_out, ffn_w_in, ffn_w_out, loss_target, m_meta_tokens, m_attn_norm, m_ffn_norm, m_final_norm, m_fox_w_in, m_fox_b_f, m_fox_q_norm, m_fox_k_norm, m_fox_w_out, m_hgrn_w_in, m_hgrn_lower_bounds, m_hgrn_g_norm, m_hgrn_w_out, m_ffn_w_in, m_ffn_w_out, v_meta_tokens, v_attn_norm, v_ffn_norm, v_final_norm, v_fox_w_in, v_fox_b_f, v_fox_q_norm, v_fox_k_norm, v_fox_w_out, v_hgrn_w_in, v_hgrn_lower_bounds, v_hgrn_g_norm, v_hgrn_w_out, v_ffn_w_in, v_ffn_w_out):
    w = dict(meta_tokens=meta_tokens, attn_norm=attn_norm, ffn_norm=ffn_norm, final_norm=final_norm,
             fox_w_in=fox_w_in, fox_b_f=fox_b_f, fox_q_norm=fox_q_norm, fox_k_norm=fox_k_norm,
             fox_w_out=fox_w_out, hgrn_w_in=hgrn_w_in, hgrn_lower_bounds=hgrn_lower_bounds,
             hgrn_g_norm=hgrn_g_norm, hgrn_w_out=hgrn_w_out, ffn_w_in=ffn_w_in, ffn_w_out=ffn_w_out)
    m = dict(meta_tokens=m_meta_tokens, attn_norm=m_attn_norm, ffn_norm=m_ffn_norm, final_norm=m_final_norm,
             fox_w_in=m_fox_w_in, fox_b_f=m_fox_b_f, fox_q_norm=m_fox_q_norm, fox_k_norm=m_fox_k_norm,
             fox_w_out=m_fox_w_out, hgrn_w_in=m_hgrn_w_in, hgrn_lower_bounds=m_hgrn_lower_bounds,
             hgrn_g_norm=m_hgrn_g_norm, hgrn_w_out=m_hgrn_w_out, ffn_w_in=m_ffn_w_in, ffn_w_out=m_ffn_w_out)
    v = dict(meta_tokens=v_meta_tokens, attn_norm=v_attn_norm, ffn_norm=v_ffn_norm, final_norm=v_final_norm,
             fox_w_in=v_fox_w_in, fox_b_f=v_fox_b_f, fox_q_norm=v_fox_q_norm, fox_k_norm=v_fox_k_norm,
             fox_w_out=v_fox_w_out, hgrn_w_in=v_hgrn_w_in, hgrn_lower_bounds=v_hgrn_lower_bounds,
             hgrn_g_norm=v_hgrn_g_norm, hgrn_w_out=v_hgrn_w_out, ffn_w_in=v_ffn_w_in, ffn_w_out=v_ffn_w_out)
    axes = ("x", "y", "c")
    small_shapes = [w[n].shape for n in SMALL]

    g_meta, g_fin = _gather_two_level([w["meta_tokens"].astype(F32), w["fox_w_in"].astype(BF16)], "gather_weights")
    loss_local, grad_x, grads, recv = _local_step(
        x[0], loss_target[0], _gathered_to_full(g_meta, "meta_tokens"), _gathered_to_full(g_fin, "fox_w_in"),
        {n: w[n].astype(BF16) for n in LATE}, {n: w[n] for n in SMALL})
    loss = lax.psum(loss_local, axes)

    r_meta, r_small = _exchange([_full_to_slabs(grads["meta_tokens"], "meta_tokens"),
                                 _pack_small([grads[n] for n in SMALL])], [True, False], "scatter_grads")
    recv.update(meta_tokens=r_meta)

    res = {n: _adamw(recv[n], w[n], m[n], v[n], "adamw_" + n) for n in BIG}
    sml = _adamw(r_small, _pack_small([w[n] for n in SMALL]), _pack_small([m[n] for n in SMALL]),
                 _pack_small([v[n] for n in SMALL]), "adamw_small")
    outs = []
    for k in range(4):
        d = {n: res[n][k] for n in BIG}
        d.update(zip(SMALL, _unpack_small(sml[k], small_shapes)))
        outs.extend(d[n] for n in WEIGHTS)
    return (loss, grad_x[None], *outs)
```

```python
import functools
import math

import jax
import jax.numpy as jnp
from jax import lax
from jax.experimental import pallas as pl
from jax.experimental.pallas import tpu as pltpu

F32 = jnp.float32
BF16 = jnp.bfloat16
EPS = 1e-6
N_META = 16
LANES = 128
HEAD_ROWS = 256
ROW_TILE = 768
N_PAD = HEAD_ROWS - N_META
FOX_DH = 64
HGRN_CHUNK = 64
HGRN_TILE = 384
N_DEV = 8
NEG = -1e30
PAD_SHIFT = 1e4
VMEM_LIMIT = 56 * 1024 * 1024
HI = lax.Precision.HIGHEST
LOG2E = 1.0 / math.log(2.0)
LN2 = math.log(2.0)

ADAM_LR = 0.001
ADAM_B1 = 0.9
ADAM_B2 = 0.999
ADAM_EPS = 1e-08
ADAM_WD = 0.01
ADAM_STEP = 10

BIG = ("meta_tokens", "fox_w_in", "fox_w_out", "hgrn_w_in", "hgrn_w_out", "ffn_w_in", "ffn_w_out")
SMALL = ("attn_norm", "ffn_norm", "final_norm", "fox_b_f", "fox_q_norm", "fox_k_norm",
         "hgrn_lower_bounds", "hgrn_g_norm")
WEIGHTS = ("meta_tokens", "attn_norm", "ffn_norm", "final_norm", "fox_w_in", "fox_b_f", "fox_q_norm",
           "fox_k_norm", "fox_w_out", "hgrn_w_in", "hgrn_lower_bounds", "hgrn_g_norm", "hgrn_w_out",
           "ffn_w_in", "ffn_w_out")
COL_SHARDED = ("meta_tokens", "fox_w_in", "hgrn_w_in", "ffn_w_in")
LATE = ("fox_w_out", "hgrn_w_in", "hgrn_w_out", "ffn_w_in", "ffn_w_out")


def _params(sem=None):
    return pltpu.CompilerParams(dimension_semantics=sem, vmem_limit_bytes=VMEM_LIMIT)


def _tile(n, cap):
    best = None
    for t in range(LANES, min(n, cap) + 1, LANES):
        if n % t == 0:
            best = t
    assert best is not None, (n, cap)
    return best


def _row_chunk(n, cap, mult=8):
    best = n
    for t in range(mult, min(n, cap) + 1, mult):
        if n % t == 0:
            best = t
    return best


def _dg(a, b, ca, cb):
    return lax.dot_general(a.astype(BF16), b.astype(BF16), (((ca,), (cb,)), ((), ())),
                           preferred_element_type=F32)


@jax.custom_vjp
def _d_nn(a, b):
    return _dg(a, b, 1, 0)


@jax.custom_vjp
def _d_nt(a, b):
    return _dg(a, b, 1, 1)


@jax.custom_vjp
def _d_tn(a, b):
    return _dg(a, b, 0, 0)


_d_nn.defvjp(lambda a, b: (_d_nn(a, b), (a, b)), lambda r, g: (_d_nt(g, r[1]), _d_tn(r[0], g)))
_d_nt.defvjp(lambda a, b: (_d_nt(a, b), (a, b)), lambda r, g: (_d_nn(g, r[1]), _d_tn(g, r[0])))
_d_tn.defvjp(lambda a, b: (_d_tn(a, b), (a, b)), lambda r, g: (_d_nt(r[1], g), _d_nn(r[0], g)))


def _log_sigmoid(x):
    return jnp.minimum(x, 0.0) - jnp.log1p(jnp.exp(-jnp.abs(x)))


def _rms(x, g):
    return x * lax.rsqrt(jnp.mean(x * x, axis=-1, keepdims=True) + EPS) * g


def _mm(a, b, mode, out_dtype, name, res=None, tm=None, tn=None, tk=None):
    assert a.dtype == BF16 and b.dtype == BF16, (name, a.dtype, b.dtype)
    if mode == "nn":
        (M, K), N = a.shape, b.shape[1]
    elif mode == "nt":
        (M, K), N = a.shape, b.shape[0]
    else:
        (K, M), N = a.shape, b.shape[1]
    if mode == "nn":
        tm, tn, tk = tm or _tile(M, ROW_TILE), tn or _tile(N, 1408), tk or _tile(K, 2816)
    elif mode == "nt":
        tm, tn, tk = tm or _tile(M, ROW_TILE if K <= 2048 else ROW_TILE // 2), tn or N, tk or K
    else:
        tm, tn, tk = tm or _tile(M, 1408), tn or _tile(N, 1408), tk or _tile(K, ROW_TILE)
    nk = K // tk
    if mode == "tn":
        a_spec = pl.BlockSpec((tk, tm), lambda j, i, k: (k, i))
        dims = (((0,), (0,)), ((), ()))
    else:
        a_spec = pl.BlockSpec((tm, tk), lambda j, i, k: (i, k))
        dims = (((1,), (1 if mode == "nt" else 0,)), ((), ()))
    if mode == "nt":
        b_spec = pl.BlockSpec((tn, tk), lambda j, i, k: (j, k))
    else:
        b_spec = pl.BlockSpec((tk, tn), lambda j, i, k: (k, j))

    o_spec = pl.BlockSpec((tm, tn), lambda j, i, k: (i, j))

    def body(a_ref, b_ref, *rest):
        r_ref = rest[0] if res is not None else None
        o_ref, acc_ref = rest[-2:]
        k = pl.program_id(2)

        @pl.when(k == 0)
        def _():
            acc_ref[...] = jnp.zeros_like(acc_ref)

        acc_ref[...] += lax.dot_general(a_ref[...], b_ref[...], dims, preferred_element_type=F32)

        @pl.when(k == nk - 1)
        def _():
            out = acc_ref[...] if r_ref is None else acc_ref[...] + r_ref[...]
            o_ref[...] = out.astype(out_dtype)

    return pl.pallas_call(
        body, name=name, grid=(N // tn, M // tm, nk),
        in_specs=[a_spec, b_spec] + ([o_spec] if res is not None else []),
        out_specs=o_spec,
        out_shape=jax.ShapeDtypeStruct((M, N), out_dtype),
        scratch_shapes=[pltpu.VMEM((tm, tn), F32)],
        compiler_params=_params(("parallel", "parallel", "arbitrary")),
    )(a, b, *([res] if res is not None else []))


def _out_proj_fwd(a, w, res, gain, name):
    Lp, K = a.shape
    D = w.shape[1]
    tm = _tile(Lp, ROW_TILE)

    def body(a_ref, w_ref, r_ref, g_ref, h_ref, hn_ref, hnt_ref):
        h = jnp.dot(a_ref[...], w_ref[...], preferred_element_type=F32) + r_ref[...]
        h_ref[...] = h
        hn = _rms(h, g_ref[...])
        hn_ref[...] = hn.astype(BF16)
        hnt_ref[...] = hn.T.astype(BF16)

    row = pl.BlockSpec((tm, D), lambda i: (i, 0))
    return pl.pallas_call(
        body, name=name, grid=(Lp // tm,),
        in_specs=[pl.BlockSpec((tm, K), lambda i: (i, 0)), pl.BlockSpec((K, D), lambda i: (0, 0)), row,
                  pl.BlockSpec((1, D), lambda i: (0, 0))],
        out_specs=[row, row, pl.BlockSpec((D, tm), lambda i: (0, i))],
        out_shape=[jax.ShapeDtypeStruct((Lp, D), F32), jax.ShapeDtypeStruct((Lp, D), BF16),
                   jax.ShapeDtypeStruct((D, Lp), BF16)],
        compiler_params=_params(("parallel",)),
    )(a, w, res, gain)


def _in_proj_dx(dy, w, x, gain, dres, name, xchg=(), split_head=False):
    Lp, N = dy.shape
    D = w.shape[0]
    tm = HEAD_ROWS if split_head else _tile(Lp, ROW_TILE // 2)
    nt = Lp // tm
    nx = len(xchg)

    def body(dy_ref, w_ref, x_ref, g_ref, dr_ref, *rest):
        x_in, (dx_ref, dxb_ref, dg_ref), x_out, sems = rest[:nx], rest[nx:nx + 3], rest[nx + 3:2 * nx + 3], rest[2 * nx + 3:]

        @pl.when(pl.program_id(0) == 0)
        def _():
            dg_ref[...] = jnp.zeros_like(dg_ref)
            if nx:
                for cp in _xchg_copies(x_in, x_out, [True] * nx, sems):
                    cp.start()

        dhn = lax.dot_general(dy_ref[...], w_ref[...], (((1,), (1,)), ((), ())), preferred_element_type=F32)
        _, vjp = jax.vjp(_rms, x_ref[...], g_ref[...])
        dx, dg = vjp(dhn)
        dx = dx + dr_ref[...]
        if split_head:
            @pl.when(pl.program_id(0) == 0)
            def _():
                dx_ref[...] = dx

            @pl.when(pl.program_id(0) > 0)
            def _():
                dxb_ref[...] = dx
        else:
            dx_ref[...] = dx
            dxb_ref[...] = dx.astype(BF16)
        dg_ref[...] += dg

        if nx:
            @pl.when(pl.program_id(0) == nt - 1)
            def _():
                for cp in _xchg_copies(x_in, x_out, [True] * nx, sems):
                    cp.wait()

    row = pl.BlockSpec((tm, D), lambda i: (i, 0))
    vec = pl.BlockSpec((1, D), lambda i: (0, 0))
    HBM = pl.BlockSpec(memory_space=pltpu.HBM)
    return pl.pallas_call(
        body, name=name, grid=(nt,),
        in_specs=[pl.BlockSpec((tm, N), lambda i: (i, 0)), pl.BlockSpec((D, N), lambda i: (0, 0)), row, vec, row]
        + [HBM] * nx,
        out_specs=([pl.BlockSpec((tm, D), lambda i: (0, 0)), pl.BlockSpec((tm, D), lambda i: (jnp.maximum(i - 1, 0), 0))]
                   if split_head else [row, row]) + [vec] + [HBM] * nx,
        out_shape=([jax.ShapeDtypeStruct((tm, D), F32), jax.ShapeDtypeStruct((Lp - tm, D), F32)] if split_head else
                   [jax.ShapeDtypeStruct((Lp, D), F32), jax.ShapeDtypeStruct((Lp, D), BF16)])
        + [jax.ShapeDtypeStruct((1, D), F32)] + _xchg_out_shapes(xchg, [True] * nx),
        scratch_shapes=_xchg_sems(nx) if nx else [],
        compiler_params=_params(("arbitrary",)),
    )(dy, w, x, gain, dres, *xchg)


def _embed_rms_fwd(x, meta, g, name):
    S, D = x.shape
    TR = HEAD_ROWS
    Lp = S + TR

    def body(x_ref, m_ref, g_ref, h_ref, o_ref, ot_ref):
        i = pl.program_id(0)

        @pl.when(i == 0)
        def _():
            h_ref[...] = jnp.zeros_like(h_ref)
            h_ref[N_PAD:, :] = m_ref[...]

        @pl.when(i > 0)
        def _():
            h_ref[...] = x_ref[...]

        y = _rms(h_ref[...], g_ref[...])
        o_ref[...] = y.astype(BF16)
        ot_ref[...] = y.T.astype(BF16)

    row = pl.BlockSpec((TR, D), lambda i: (i, 0))
    return pl.pallas_call(
        body, name=name, grid=(Lp // TR,),
        in_specs=[pl.BlockSpec((TR, D), lambda i: (jnp.maximum(i - 1, 0), 0)),
                  pl.BlockSpec((N_META, D), lambda i: (0, 0)), pl.BlockSpec((1, D), lambda i: (0, 0))],
        out_specs=[row, row, pl.BlockSpec((D, TR), lambda i: (0, i))],
        out_shape=[jax.ShapeDtypeStruct((Lp, D), F32), jax.ShapeDtypeStruct((Lp, D), BF16),
                   jax.ShapeDtypeStruct((D, Lp), BF16)],
        compiler_params=_params(("parallel",)),
    )(x, meta, g)


def _swiglu(gate, up):
    return gate * jax.nn.sigmoid(gate) * up


def _ffn_in_fwd(hf, w_in, name):
    Lp, D = hf.shape
    F = w_in.shape[1] // 2
    tm = _tile(Lp, ROW_TILE)
    tn = _tile(F, 1408)
    nj = F // tn

    def body(a_ref, bg_ref, bu_ref, g_ref, u_ref, act_ref, actt_ref):
        a = a_ref[...]
        g = jnp.dot(a, bg_ref[...], preferred_element_type=F32)
        u = jnp.dot(a, bu_ref[...], preferred_element_type=F32)
        g_ref[...] = g.astype(BF16)
        u_ref[...] = u.astype(BF16)
        act = _swiglu(g, u)
        act_ref[...] = act.astype(BF16)
        actt_ref[...] = act.T.astype(BF16)

    tile = pl.BlockSpec((tm, tn), lambda j, i: (i, j))
    return pl.pallas_call(
        body, name=name, grid=(nj, Lp // tm),
        in_specs=[pl.BlockSpec((tm, D), lambda j, i: (i, 0)), pl.BlockSpec((D, tn), lambda j, i: (0, j)),
                  pl.BlockSpec((D, tn), lambda j, i: (0, nj + j))],
        out_specs=[tile, tile, tile, pl.BlockSpec((tn, tm), lambda j, i: (j, i))],
        out_shape=[jax.ShapeDtypeStruct((Lp, F), BF16)] * 3 + [jax.ShapeDtypeStruct((F, Lp), BF16)],
        compiler_params=_params(("parallel", "parallel")),
    )(hf, w_in, w_in)


def _ffn_out_dx(dhb, w_out, g, u, name):
    Lp, D = dhb.shape
    F = w_out.shape[0]
    tm = HEAD_ROWS

    def body(a_ref, b_ref, g_ref, u_ref, o_ref):
        dact = lax.dot_general(a_ref[...], b_ref[...], (((1,), (1,)), ((), ())), preferred_element_type=F32)
        _, vjp = jax.vjp(_swiglu, g_ref[...].astype(F32), u_ref[...].astype(F32))
        dg, du = vjp(dact)
        o_ref[:, :F] = dg.astype(BF16)
        o_ref[:, F:] = du.astype(BF16)

    wide = pl.BlockSpec((tm, F), lambda i: (i, 0))
    return pl.pallas_call(
        body, name=name, grid=(Lp // tm,),
        in_specs=[pl.BlockSpec((tm, D), lambda i: (i, 0)), pl.BlockSpec((F, D), lambda i: (0, 0)), wide, wide],
        out_specs=pl.BlockSpec((tm, 2 * F), lambda i: (i, 0)),
        out_shape=jax.ShapeDtypeStruct((Lp, 2 * F), BF16),
        compiler_params=_params(("parallel",)),
    )(dhb, w_out, g, u)


def _final_loss(h, g, target, name):
    Lp, D = h.shape
    TR = HEAD_ROWS

    def loss_fn(hh, gg, tt):
        err = _rms(hh, gg) - tt
        return 0.5 * jnp.sum(jnp.mean(err * err, axis=-1))

    def body(h_ref, g_ref, t_ref, loss_ref, dh_ref, dhb_ref, dg_ref):
        i = pl.program_id(0)

        @pl.when(i == 0)
        def _():
            loss_ref[...] = jnp.zeros_like(loss_ref)
            dg_ref[...] = jnp.zeros_like(dg_ref)
            dh_ref[...] = jnp.zeros_like(dh_ref)
            dhb_ref[...] = jnp.zeros_like(dhb_ref)

        @pl.when(i > 0)
        def _():
            val, vjp = jax.vjp(lambda hh, gg: loss_fn(hh, gg, t_ref[...]), h_ref[...], g_ref[...])
            dh, dg = vjp(jnp.ones((), F32))
            dh_ref[...] = dh
            dhb_ref[...] = dh.astype(BF16)
            dg_ref[...] += dg
            loss_ref[...] += val

    row = pl.BlockSpec((TR, D), lambda i: (i, 0))
    return pl.pallas_call(
        body, name=name, grid=(Lp // TR,),
        in_specs=[row, pl.BlockSpec((1, D), lambda i: (0, 0)),
                  pl.BlockSpec((TR, D), lambda i: (jnp.maximum(i - 1, 0), 0))],
        out_specs=[pl.BlockSpec((8, LANES), lambda i: (0, 0)), row, row, pl.BlockSpec((1, D), lambda i: (0, 0))],
        out_shape=[jax.ShapeDtypeStruct((8, LANES), F32), jax.ShapeDtypeStruct((Lp, D), F32),
                   jax.ShapeDtypeStruct((Lp, D), BF16), jax.ShapeDtypeStruct((1, D), F32)],
        compiler_params=_params(("arbitrary",)),
    )(h, g, target)


def _lane_lo():
    return lax.broadcasted_iota(jnp.int32, (1, LANES), 1) < FOX_DH


def _headnorm(x, g, scale):
    lo = _lane_lo()
    x2 = x * x
    s0 = jnp.sum(jnp.where(lo, x2, 0.0), axis=-1, keepdims=True)
    s1 = jnp.sum(jnp.where(lo, 0.0, x2), axis=-1, keepdims=True)
    r = jnp.where(lo, lax.rsqrt(s0 / FOX_DH + EPS), lax.rsqrt(s1 / FOX_DH + EPS))
    return x * r * g * scale


AUG = 3


def _split3(x):
    hi = x.astype(BF16).astype(F32)
    mid = (x - hi).astype(BF16).astype(F32)
    return hi, mid, x - hi - mid


def _fox_prep_fwd(proj, bf, qg, kg, T, D, name):
    Lp = proj.shape[0]
    nb = D // LANES
    scale = FOX_DH ** -0.5 * LOG2E

    def body(q_ref, k_ref, v_ref, fl_ref, bf_ref, qg_ref, kg_ref, qn_ref, ka_ref, kb_ref, vb_ref, c_ref, carry_ref):
        i = pl.program_id(0)

        @pl.when(i == 0)
        def _():
            carry_ref[...] = jnp.zeros_like(carry_ref)

        vb_ref[...] = v_ref[...].astype(BF16)
        log_f = _log_sigmoid(fl_ref[...] + bf_ref[...])
        row = lax.broadcasted_iota(jnp.int32, (T, T), 0)
        col = lax.broadcasted_iota(jnp.int32, (T, T), 1)
        tri = (col <= row).astype(F32)
        c = jnp.dot(tri, log_f, precision=HI, preferred_element_type=F32) + carry_ref[...]
        c2 = c * LOG2E
        c_ref[...] = c2
        last = lax.broadcasted_iota(jnp.int32, (T, 1), 0) == T - 1
        carry_ref[...] = jnp.sum(jnp.where(last, c, 0.0), axis=0, keepdims=True)

        is_pad = (i * T + lax.broadcasted_iota(jnp.int32, (T, 1), 0)) < N_PAD
        negc = jnp.where(is_pad, -PAD_SHIFT, -c2)
        lane = lax.broadcasted_iota(jnp.int32, (1, LANES), 1)
        for b in range(nb):
            sl = slice(b * LANES, (b + 1) * LANES)
            qn_ref[:, sl] = _headnorm(q_ref[:, sl], qg_ref[...], scale).astype(BF16)
            kn = _headnorm(k_ref[:, sl], kg_ref[...], 1.0)
            ka = jnp.where(lane < FOX_DH, kn, 0.0)
            kb = jnp.where(lane < FOX_DH, 0.0, kn)
            for n, (pa, pb) in enumerate(zip(_split3(_pick_col(negc, 2 * b)), _split3(_pick_col(negc, 2 * b + 1)))):
                ka = jnp.where(lane == FOX_DH + n, pa, ka)
                kb = jnp.where(lane == n, pb, kb)
            ka_ref[:, sl] = ka.astype(BF16)
            kb_ref[:, sl] = kb.astype(BF16)

    wide = lambda j: pl.BlockSpec((T, D), lambda i: (i, j))
    vec = pl.BlockSpec((1, LANES), lambda i: (0, 0))
    return pl.pallas_call(
        body, name=name, grid=(Lp // T,),
        in_specs=[wide(0), wide(1), wide(2), pl.BlockSpec((T, LANES), lambda i: (i, 4 * nb)), vec, vec, vec],
        out_specs=[wide(0), wide(0), wide(0), wide(0), pl.BlockSpec((T, LANES), lambda i: (i, 0))],
        out_shape=[jax.ShapeDtypeStruct((Lp, D), BF16)] * 4 + [jax.ShapeDtypeStruct((Lp, LANES), F32)],
        scratch_shapes=[pltpu.VMEM((1, LANES), F32)],
        compiler_params=_params(("arbitrary",)),
    )(proj, proj, proj, proj, bf, qg, kg)


def _fox_prep_bwd(proj, bf, qg, kg, dqn, dkn, dv, dgate, dc, T, D, name):
    Lp = proj.shape[0]
    nb = D // LANES
    nt = Lp // T
    scale = FOX_DH ** -0.5 * LOG2E

    def body(q_ref, k_ref, fl_ref, bf_ref, qg_ref, kg_ref, dqn_ref, dkn_ref, dv_ref, dgate_ref, dc_ref,
             dproj_ref, sm_ref, carry_ref):
        @pl.when(pl.program_id(0) == 0)
        def _():
            carry_ref[...] = jnp.zeros_like(carry_ref)
            sm_ref[...] = jnp.zeros_like(sm_ref)

        dqg = jnp.zeros((1, LANES), F32)
        dkg = jnp.zeros((1, LANES), F32)
        for b in range(nb):
            sl = slice(b * LANES, (b + 1) * LANES)
            _, vjp = jax.vjp(lambda x, g: _headnorm(x, g, scale), q_ref[:, sl], qg_ref[...])
            dx, dg = vjp(dqn_ref[:, sl] * LN2)
            dproj_ref[:, sl] = dx.astype(BF16)
            dqg = dqg + dg
            _, vjp = jax.vjp(lambda x, g: _headnorm(x, g, 1.0), k_ref[:, sl], kg_ref[...])
            dx, dg = vjp(dkn_ref[:, sl] * LN2)
            dproj_ref[:, D + b * LANES:D + (b + 1) * LANES] = dx.astype(BF16)
            dkg = dkg + dg
        dproj_ref[:, 2 * D:3 * D] = dv_ref[...].astype(BF16)
        dproj_ref[:, 3 * D:4 * D] = dgate_ref[...]
        dcv = dc_ref[...]
        row = lax.broadcasted_iota(jnp.int32, (T, T), 0)
        col = lax.broadcasted_iota(jnp.int32, (T, T), 1)
        triu = (col >= row).astype(F32)
        dlogf = jnp.dot(triu, dcv, precision=HI, preferred_element_type=F32) + carry_ref[...]
        carry_ref[...] += jnp.sum(dcv, axis=0, keepdims=True)
        _, vjp = jax.vjp(_log_sigmoid, fl_ref[...] + bf_ref[...])
        (dfl,) = vjp(dlogf)
        dproj_ref[:, 4 * D:] = dfl.astype(BF16)
        sm_ref[0:1, :] += jnp.sum(dfl, axis=0, keepdims=True)
        sm_ref[1:2, :] += dqg
        sm_ref[2:3, :] += dkg

    wide = lambda j: pl.BlockSpec((T, D), lambda i: (nt - 1 - i, j))
    narrow = lambda j: pl.BlockSpec((T, LANES), lambda i: (nt - 1 - i, j))
    vec = pl.BlockSpec((1, LANES), lambda i: (0, 0))
    return pl.pallas_call(
        body, name=name, grid=(nt,),
        in_specs=[wide(0), wide(1), narrow(4 * nb), vec, vec, vec, wide(0), wide(0), wide(0), wide(0), narrow(0)],
        out_specs=[pl.BlockSpec((T, 4 * D + LANES), lambda i: (nt - 1 - i, 0)), pl.BlockSpec((8, LANES), lambda i: (0, 0))],
        out_shape=[jax.ShapeDtypeStruct((Lp, 4 * D + LANES), BF16), jax.ShapeDtypeStruct((8, LANES), F32)],
        scratch_shapes=[pltpu.VMEM((1, LANES), F32)],
        compiler_params=_params(("arbitrary",)),
    )(proj, proj, proj, bf, qg, kg, dqn, dkn, dv, dgate, dc)


def _fox_q_operands(q):
    lane = lax.broadcasted_iota(jnp.int32, (1, LANES), 1)
    zero, one = jnp.zeros_like(q), jnp.ones_like(q)
    return (jnp.where(lane < FOX_DH, q, jnp.where(lane < FOX_DH + AUG, one, zero)),
            jnp.where(lane < FOX_DH, jnp.where(lane < AUG, one, zero), q))


def _fox_mask(q0, nq, k0, nk):
    qpos = q0 + lax.broadcasted_iota(jnp.int32, (nq, 1), 0)
    kpos = k0 + lax.broadcasted_iota(jnp.int32, (1, nk), 1)
    return (kpos <= qpos) & ((kpos >= N_PAD) | (qpos < N_PAD))


def _diag_split(T):
    return 512 if T == 768 else T // 2


def _rows_joined(head_rows, tail_rows, n_head):
    return jnp.concatenate([head_rows[:n_head], tail_rows], axis=0)


def _pick_col(blk, idx):
    lane = lax.broadcasted_iota(jnp.int32, (1, LANES), 1)
    return jnp.sum(jnp.where(lane == idx, blk, 0.0), axis=1, keepdims=True)


def _split_halves(blk):
    lo = _lane_lo()
    return (jnp.max(jnp.where(lo, blk, -jnp.inf), axis=1, keepdims=True),
            jnp.max(jnp.where(lo, -jnp.inf, blk), axis=1, keepdims=True))


def _fox_attn_fwd(qn, ka, kb, vb, c, proj, xchg, T, D, name):
    Lp = qn.shape[0]
    P = D // LANES
    nt = Lp // T
    nx = len(xchg)

    def body(q_ref, ka_ref, kb_ref, v_ref, c_ref, g_ref, *rest):
        x_in, (o_ref, og_ref, m_ref, li_ref), x_out, sems = rest[:nx], rest[nx:nx + 4], rest[nx + 4:2 * nx + 4], rest[2 * nx + 4:]
        p = pl.program_id(0)
        i = pl.program_id(1)

        @pl.when((p == 0) & (i == 0))
        def _():
            for cp in _xchg_copies(x_in, x_out, [False] * nx, sems):
                cp.start()

        lo = _lane_lo()
        q = q_ref[...]
        qh = _fox_q_operands(q)
        cblk = c_ref[...]
        cq = tuple(_pick_col(cblk, 2 * p + h) for h in (0, 1))
        one = jnp.ones_like(q)

        def block(r0, nr, k0, nk, carry, masked):
            kj = (ka_ref[pl.ds(k0, nk), :], kb_ref[pl.ds(k0, nk), :])
            vj = v_ref[pl.ds(k0, nk), :]
            vh = (jnp.where(lo, vj, one[:nk]), jnp.where(lo, one[:nk], vj))
            mask = _fox_mask(i * T + r0, nr, k0, nk) if masked else None
            out = []
            for h in (0, 1):
                m, acc = carry[h]
                cqr = cq[h][r0:r0 + nr]
                t = lax.dot_general(qh[h][r0:r0 + nr], kj[h], (((1,), (1,)), ((), ())), preferred_element_type=F32)
                if masked:
                    t = jnp.where(mask, t, NEG)
                m_new = jnp.ceil(jnp.maximum(m, cqr + jnp.max(t, axis=1, keepdims=True)))
                pr = jnp.exp2(t + (cqr - m_new)).astype(BF16)
                acc = jnp.exp2(m - m_new) * acc + jnp.dot(pr, vh[h], preferred_element_type=F32)
                out.append((m_new, acc))
            return tuple(out)

        init = tuple((jnp.full((T, 1), NEG, F32), jnp.zeros((T, LANES), F32)) for _ in (0, 1))
        carry = lax.fori_loop(0, i, lambda j, cr: block(0, T, pl.multiple_of(j * T, LANES), T, cr, False), init)
        (m0, a0), (m1, a1) = block(0, T, pl.multiple_of(i * T, LANES), T, carry, True)
        l0 = pltpu.roll(a0, FOX_DH, 1)
        l1 = pltpu.roll(a1, FOX_DH, 1)
        o = jnp.where(lo, a0 / l0, a1 / l1)
        o_ref[...] = o
        m_ref[...] = jnp.where(lo, m0, m1)
        li_ref[...] = jnp.where(lo, 1.0 / l0, 1.0 / l1)
        og_ref[...] = (o * jax.nn.sigmoid(g_ref[...])).astype(BF16)

        @pl.when((p == P - 1) & (i == nt - 1))
        def _():
            for cp in _xchg_copies(x_in, x_out, [False] * nx, sems):
                cp.wait()

    tile = pl.BlockSpec((T, LANES), lambda p, i: (i, p))
    full = pl.BlockSpec((Lp, LANES), lambda p, i: (0, p))
    HBM = pl.BlockSpec(memory_space=pltpu.HBM)
    return pl.pallas_call(
        body, name=name, grid=(P, nt),
        in_specs=[tile, full, full, full, pl.BlockSpec((T, LANES), lambda p, i: (i, 0)),
                  pl.BlockSpec((T, LANES), lambda p, i: (i, 3 * P + p))] + [HBM] * nx,
        out_specs=[tile, tile, tile, tile] + [HBM] * nx,
        out_shape=[jax.ShapeDtypeStruct((Lp, D), F32), jax.ShapeDtypeStruct((Lp, D), BF16),
                   jax.ShapeDtypeStruct((Lp, D), F32), jax.ShapeDtypeStruct((Lp, D), F32)]
        + _xchg_out_shapes(xchg, [False] * nx),
        scratch_shapes=_xchg_sems(nx),
        compiler_params=_params(("arbitrary", "arbitrary")),
    )(qn, ka, kb, vb, c, proj, *xchg)


def _fox_out_dx(dhb, w_out, o, proj, linv, D, name):
    Lp = o.shape[0]
    tm = HEAD_ROWS

    def body(a_ref, w_ref, o_ref, g_ref, li_ref, do_ref, dg_ref, dl_ref):
        lo = _lane_lo()
        dog_all = lax.dot_general(a_ref[...], w_ref[...], (((1,), (1,)), ((), ())), preferred_element_type=F32)
        for b in range(D // LANES):
            sl = slice(b * LANES, (b + 1) * LANES)
            dog = dog_all[:, sl]
            sig = jax.nn.sigmoid(g_ref[:, sl])
            ov = o_ref[:, sl]
            do = (dog * sig * li_ref[:, sl]).astype(BF16)
            do_ref[:, sl] = do
            dg_ref[:, sl] = (dog * ov * sig * (1.0 - sig)).astype(BF16)
            t = do.astype(F32) * ov
            d0 = jnp.sum(jnp.where(lo, t, 0.0), axis=1, keepdims=True)
            d1 = jnp.sum(jnp.where(lo, 0.0, t), axis=1, keepdims=True)
            dl_ref[:, sl] = jnp.where(lo, d0, d1)

    row = pl.BlockSpec((tm, D), lambda i: (i, 0))
    return pl.pallas_call(
        body, name=name, grid=(Lp // tm,),
        in_specs=[row, pl.BlockSpec((D, D), lambda i: (0, 0)), row, pl.BlockSpec((tm, D), lambda i: (i, 3)), row],
        out_specs=[row, row, row],
        out_shape=[jax.ShapeDtypeStruct((Lp, D), BF16), jax.ShapeDtypeStruct((Lp, D), BF16),
                   jax.ShapeDtypeStruct((Lp, D), F32)],
        compiler_params=_params(("parallel",)),
    )(dhb, w_out, o, proj, linv)


def _fox_attn_bwd(qn, ka, kb, vb, c, do, mshift, delta, xchg, T, D, name):
    Lp = qn.shape[0]
    P = D // LANES
    nt = Lp // T
    nx = len(xchg)

    def body(q_ref, do_ref, m_ref, dl_ref, c_ref, ka_ref, kb_ref, v_ref, *rest):
        x_in, (dq_ref, dk_ref, dv_ref, dc_ref), x_out, sems = rest[:nx], rest[nx:nx + 4], rest[nx + 4:2 * nx + 4], rest[2 * nx + 4:]
        p = pl.program_id(0)
        i = pl.program_id(1)

        @pl.when((p == 0) & (i == 0))
        def _():
            for cp in _xchg_copies(x_in, x_out, [True] * nx, sems):
                cp.start()

        @pl.when(i == 0)
        def _():
            dk_ref[...] = jnp.zeros_like(dk_ref)
            dv_ref[...] = jnp.zeros_like(dv_ref)
            dc_ref[...] = jnp.zeros_like(dc_ref)

        lo = _lane_lo()
        q = q_ref[...]
        do = do_ref[...]
        zero = jnp.zeros_like(q)
        qh = _fox_q_operands(q)
        doh = (jnp.where(lo, do, zero), jnp.where(lo, zero, do))
        msh = _split_halves(m_ref[...])
        dlt = _split_halves(dl_ref[...])
        cblk = c_ref[...]
        shift = tuple(_pick_col(cblk, 2 * p + h) - msh[h] for h in (0, 1))

        def block(r0, nr, k0, nk, carry, masked):
            kj = (ka_ref[pl.ds(k0, nk), :], kb_ref[pl.ds(k0, nk), :])
            vj = v_ref[pl.ds(k0, nk), :]
            mask = _fox_mask(i * T + r0, nr, k0, nk) if masked else None
            qr, dor = q[r0:r0 + nr], do[r0:r0 + nr]
            dqs, dks, dvs = [], [], []
            for h in (0, 1):
                t = lax.dot_general(qh[h][r0:r0 + nr], kj[h], (((1,), (1,)), ((), ())), preferred_element_type=F32)
                if masked:
                    t = jnp.where(mask, t, NEG)
                pb = jnp.exp2(t + shift[h][r0:r0 + nr]).astype(BF16)
                dp = lax.dot_general(doh[h][r0:r0 + nr], vj, (((1,), (1,)), ((), ())), preferred_element_type=F32)
                ds = pb.astype(F32) * (dp - dlt[h][r0:r0 + nr])
                dsb = ds.astype(BF16)
                dqs.append(carry[h] + jnp.dot(dsb, kj[h], preferred_element_type=F32))
                dks.append(lax.dot_general(dsb, qr, (((0,), (0,)), ((), ())), preferred_element_type=F32))
                dvs.append(lax.dot_general(pb, dor, (((0,), (0,)), ((), ())), preferred_element_type=F32))
                dc_ref[0, h:h + 1, pl.ds(k0, nk)] += -jnp.sum(ds, axis=0, keepdims=True)
            dk_ref[pl.ds(k0, nk), :] += jnp.where(lo, dks[0], dks[1])
            dv_ref[pl.ds(k0, nk), :] += jnp.where(lo, dvs[0], dvs[1])
            return tuple(dqs)

        init = (jnp.zeros((T, LANES), F32), jnp.zeros((T, LANES), F32))
        carry = lax.fori_loop(0, i, lambda j, cr: block(0, T, pl.multiple_of(j * T, LANES), T, cr, False), init)
        kd = pl.multiple_of(i * T, LANES)
        ta = _diag_split(T)
        da = block(0, T, kd, ta, carry, True)
        db = block(ta, T - ta, pl.multiple_of(kd + ta, LANES), T - ta, tuple(d[ta:] for d in da), True)
        dq_ref[...] = jnp.where(lo, _rows_joined(da[0], db[0], ta), _rows_joined(da[1], db[1], ta))

        @pl.when((p == P - 1) & (i == nt - 1))
        def _():
            for cp in _xchg_copies(x_in, x_out, [True] * nx, sems):
                cp.wait()

    tile = pl.BlockSpec((T, LANES), lambda p, i: (i, p))
    full = pl.BlockSpec((Lp, LANES), lambda p, i: (0, p))
    HBM = pl.BlockSpec(memory_space=pltpu.HBM)
    return pl.pallas_call(
        body, name=name, grid=(P, nt),
        in_specs=[tile, tile, tile, tile, pl.BlockSpec((T, LANES), lambda p, i: (i, 0)), full, full, full]
        + [HBM] * nx,
        out_specs=[tile, full, full, pl.BlockSpec((1, 8, Lp), lambda p, i: (p, 0, 0))] + [HBM] * nx,
        out_shape=[jax.ShapeDtypeStruct((Lp, D), F32)] * 3 + [jax.ShapeDtypeStruct((P, 8, Lp), F32)]
        + _xchg_out_shapes(xchg, [True] * nx),
        scratch_shapes=_xchg_sems(nx),
        compiler_params=_params(("arbitrary", "arbitrary")),
    )(qn, do, mshift, delta, c, ka, kb, vb, *xchg)


def _scan_rows(x, reverse):
    C = x.shape[0]
    row = lax.broadcasted_iota(jnp.int32, (C, 1), 0)
    step = 1
    while step < C:
        if reverse:
            x = x + jnp.where(row < C - step, pltpu.roll(x, C - step, 0), 0.0)
        else:
            x = x + jnp.where(row >= step, pltpu.roll(x, step, 0), 0.0)
        step *= 2
    return x


@jax.custom_vjp
def _cumsum_rows(x):
    return _scan_rows(x, False)


_cumsum_rows.defvjp(lambda x: (_scan_rows(x, False), None), lambda _, g: (_scan_rows(g, True),))


def _hgrn_chunk(St, qr, z, vi, go, p0, p1, gg):
    C = qr.shape[0]
    lb = jax.nn.sigmoid(p1 - p0)
    a = jnp.log(lb)
    cc = jnp.log1p(-lb) + _log_sigmoid(z)
    log_f = jnp.maximum(a, cc) + jnp.log1p(jnp.exp(-jnp.abs(a - cc)))
    k = (1.0 - lb) * jax.nn.sigmoid(-z)
    q = qr * jax.nn.sigmoid(qr)
    row = lax.broadcasted_iota(jnp.int32, (C, C), 0)
    col = lax.broadcasted_iota(jnp.int32, (C, C), 1)
    causal = col <= row
    b = _cumsum_rows(log_f)
    mid = lax.broadcasted_iota(jnp.int32, (C, 1), 0) == C // 2 - 1
    r = jnp.sum(jnp.where(mid, b, 0.0), axis=0, keepdims=True)
    b_last = jnp.sum(log_f, axis=0, keepdims=True)
    attn = jnp.where(causal, _d_nt(q * jnp.exp(b - r), k * jnp.exp(r - b)), 0.0)
    o = _d_nn(attn, vi) + _d_nt(q * jnp.exp(b), St)
    St_new = St * jnp.exp(b_last) + _d_tn(vi, k * jnp.exp(b_last - b))
    og = _rms(o, gg) * (go * jax.nn.sigmoid(go))
    return St_new, og


def _hgrn_heads_per_step(H):
    return 8 if H % 8 == 0 else 4 if H % 4 == 0 else 1


def _hgrn_specs(T, W, nhb, rev_nt=None):
    if rev_nt is None:
        return [pl.BlockSpec((T, W), functools.partial(lambda hb, t, g: (t, g * nhb + hb), g=g)) for g in range(4)]
    return [pl.BlockSpec((T, W), functools.partial(lambda hb, t, g: (rev_nt - 1 - t, g * nhb + hb), g=g))
            for g in range(4)]


def _hgrn_fwd(proj, lbp, gg, T, name):
    Lp = proj.shape[0]
    D = proj.shape[1] // 4
    H = D // LANES
    hps = _hgrn_heads_per_step(H)
    W = hps * LANES
    nhb = H // hps
    nt = Lp // T
    ncc = T // HGRN_CHUNK

    def body(q_ref, z_ref, i_ref, go_ref, p_ref, gg_ref, og_ref, ss_ref, st_ref):
        @pl.when(pl.program_id(1) == 0)
        def _():
            st_ref[...] = jnp.zeros_like(st_ref)

        gain = gg_ref[...]

        def chunk(cidx, states):
            sl = pl.ds(pl.multiple_of(cidx * HGRN_CHUNK, HGRN_CHUNK), HGRN_CHUNK)
            new = []
            for hh in range(hps):
                ln = slice(hh * LANES, (hh + 1) * LANES)
                ss_ref[hh, cidx] = states[hh]
                St_new, og = _hgrn_chunk(states[hh], q_ref[sl, ln], z_ref[sl, ln], i_ref[sl, ln], go_ref[sl, ln],
                                         p_ref[0:1, ln], p_ref[1:2, ln], gain)
                og_ref[sl, ln] = og.astype(BF16)
                new.append(St_new)
            return tuple(new)

        assert ncc % 2 == 0
        states = lax.fori_loop(0, ncc // 2, lambda c2, st: chunk(2 * c2 + 1, chunk(2 * c2, st)),
                               tuple(st_ref[hh] for hh in range(hps)))
        for hh in range(hps):
            st_ref[hh] = states[hh]

    return pl.pallas_call(
        body, name=name, grid=(nhb, nt),
        in_specs=_hgrn_specs(T, W, nhb) + [pl.BlockSpec((2, W), lambda hb, t: (0, hb)),
                                           pl.BlockSpec((1, LANES), lambda hb, t: (0, 0))],
        out_specs=[pl.BlockSpec((T, W), lambda hb, t: (t, hb)),
                   pl.BlockSpec((hps, ncc, LANES, LANES), lambda hb, t: (hb, t, 0, 0))],
        out_shape=[jax.ShapeDtypeStruct((Lp, D), BF16),
                   jax.ShapeDtypeStruct((H, Lp // HGRN_CHUNK, LANES, LANES), F32)],
        scratch_shapes=[pltpu.VMEM((hps, LANES, LANES), F32)],
        compiler_params=_params(("parallel", "arbitrary")),
    )(proj, proj, proj, proj, lbp, gg)


def _hgrn_bwd(proj, lbp, gg, dog, ss, T, name):
    Lp = proj.shape[0]
    D = proj.shape[1] // 4
    H = D // LANES
    hps = _hgrn_heads_per_step(H)
    W = hps * LANES
    nhb = H // hps
    assert nhb == 1, "d proj is written as whole rows: every head in one grid step"
    nt = Lp // T
    ncc = T // HGRN_CHUNK

    def body(q_ref, z_ref, i_ref, go_ref, p_ref, gg_ref, dog_ref, ss_ref, dproj_ref, dp_ref, dgg_ref, dst_ref):
        hb = pl.program_id(0)
        t = pl.program_id(1)

        @pl.when(t == 0)
        def _():
            dst_ref[...] = jnp.zeros_like(dst_ref)
            dp_ref[...] = jnp.zeros_like(dp_ref)

        @pl.when((t == 0) & (hb == 0))
        def _():
            dgg_ref[...] = jnp.zeros_like(dgg_ref)

        gain = gg_ref[...]
        row0 = (nt - 1 - t) * T

        def chunk(cc, carry):
            dstates, dps, dgain_sum = carry
            cidx = ncc - 1 - cc
            r0 = pl.multiple_of(cidx * HGRN_CHUNK, HGRN_CHUNK)
            sl = pl.ds(r0, HGRN_CHUNK)
            real = (row0 + r0 + lax.broadcasted_iota(jnp.int32, (HGRN_CHUNK, 1), 0)) >= N_PAD
            new_d, new_p = [], []
            for hh in range(hps):
                ln = slice(hh * LANES, (hh + 1) * LANES)
                _, vjp = jax.vjp(_hgrn_chunk, ss_ref[hh, cidx], q_ref[sl, ln], z_ref[sl, ln], i_ref[sl, ln],
                                 go_ref[sl, ln], p_ref[0:1, ln], p_ref[1:2, ln], gain)
                dSt, dq, dz, di, dgo, dp0, dp1, dgain = vjp((dstates[hh], dog_ref[sl, ln]))
                for grp, dval in enumerate((dq, dz, di, dgo)):
                    dproj_ref[sl, grp * D + hh * LANES:grp * D + (hh + 1) * LANES] = (
                        jnp.where(real, dval, 0.0).astype(BF16))
                new_d.append(dSt)
                new_p.append((dps[hh][0] + dp0, dps[hh][1] + dp1))
                dgain_sum = dgain_sum + dgain
            return tuple(new_d), tuple(new_p), dgain_sum

        zero_row = jnp.zeros((1, LANES), F32)
        init = (tuple(dst_ref[hh] for hh in range(hps)), tuple((zero_row, zero_row) for _ in range(hps)), zero_row)
        dstates, dps, dgain_sum = lax.fori_loop(0, ncc, chunk, init)
        for hh in range(hps):
            ln = slice(hh * LANES, (hh + 1) * LANES)
            dst_ref[hh] = dstates[hh]
            dp_ref[0:1, ln] += dps[hh][0]
            dp_ref[1:2, ln] += dps[hh][1]
        dgg_ref[0:1, :] += dgain_sum

    rev = pl.BlockSpec((T, W), lambda hb, t: (nt - 1 - t, hb))
    return pl.pallas_call(
        body, name=name, grid=(nhb, nt),
        in_specs=_hgrn_specs(T, W, nhb, nt) + [pl.BlockSpec((2, W), lambda hb, t: (0, hb)),
                                               pl.BlockSpec((1, LANES), lambda hb, t: (0, 0)), rev,
                                               pl.BlockSpec((hps, ncc, LANES, LANES),
                                                            lambda hb, t: (hb, nt - 1 - t, 0, 0))],
        out_specs=[pl.BlockSpec((T, 4 * D), lambda hb, t: (nt - 1 - t, 0)), pl.BlockSpec((8, W), lambda hb, t: (0, hb)),
                   pl.BlockSpec((8, LANES), lambda hb, t: (0, 0))],
        out_shape=[jax.ShapeDtypeStruct((Lp, 4 * D), BF16), jax.ShapeDtypeStruct((8, D), F32),
                   jax.ShapeDtypeStruct((8, LANES), F32)],
        scratch_shapes=[pltpu.VMEM((hps, LANES, LANES), F32)],
        compiler_params=_params(("arbitrary", "arbitrary")),
    )(proj, proj, proj, proj, lbp, gg, dog, ss)


def _xchg_sems(n_arr):
    return [pltpu.SemaphoreType.DMA((n_arr * (N_DEV - 1),)), pltpu.SemaphoreType.DMA((n_arr * (N_DEV - 1),)),
            pltpu.SemaphoreType.DMA((n_arr,))]


def _xchg_copies(ins, outs, per_peer, sems):
    send_sems, recv_sems, local_sems = sems
    x, y, c = lax.axis_index("x"), lax.axis_index("y"), lax.axis_index("c")
    me = 4 * x + 2 * y + c
    copies = []
    for n in range(len(ins)):
        src = ins[n].at[me] if per_peer[n] else ins[n]
        copies.append(pltpu.make_async_copy(src, outs[n].at[me], local_sems.at[n]))
    for rel in range(1, N_DEV):
        fx, fy, fc = (rel >> 2) & 1, (rel >> 1) & 1, rel & 1
        px = 1 - x if fx else x
        py = 1 - y if fy else y
        pc = 1 - c if fc else c
        peer = 4 * px + 2 * py + pc
        for n in range(len(ins)):
            src = ins[n].at[peer] if per_peer[n] else ins[n]
            copies.append(pltpu.make_async_remote_copy(
                src_ref=src, dst_ref=outs[n].at[me],
                send_sem=send_sems.at[n * (N_DEV - 1) + rel - 1],
                recv_sem=recv_sems.at[n * (N_DEV - 1) + rel - 1],
                device_id=(px, py, pc), device_id_type=pl.DeviceIdType.MESH))
    return copies


def _xchg_out_shapes(arrays, per_peer):
    return [jax.ShapeDtypeStruct(a.shape if pp else (N_DEV,) + a.shape, a.dtype) for a, pp in zip(arrays, per_peer)]


def _exchange(arrays, per_peer, name):
    n_arr = len(arrays)
    HBM = pl.BlockSpec(memory_space=pltpu.HBM)

    def body(*refs):
        copies = _xchg_copies(refs[:n_arr], refs[n_arr:2 * n_arr], per_peer, refs[2 * n_arr:])
        for cp in copies:
            cp.start()
        for cp in copies:
            cp.wait()

    return pl.pallas_call(
        body, name=name,
        in_specs=[HBM] * n_arr, out_specs=[HBM] * n_arr, out_shape=_xchg_out_shapes(arrays, per_peer),
        scratch_shapes=_xchg_sems(n_arr),
    )(*arrays)


def _gather_two_level(arrays, name):
    n_arr = len(arrays)
    HBM = pl.BlockSpec(memory_space=pltpu.HBM)

    def body(*refs):
        ins, outs = refs[:n_arr], refs[n_arr:2 * n_arr]
        send_sems, recv_sems, local_sems = refs[2 * n_arr:]
        x, y, c = lax.axis_index("x"), lax.axis_index("y"), lax.axis_index("c")
        sibling = (x, y, 1 - c)
        chips = [(1 - x, y), (x, 1 - y), (1 - x, 1 - y)]
        slot = lambda px, py, pc: 4 * px + 2 * py + pc

        def copy(a, k, block, to, own=False):
            return pltpu.make_async_remote_copy(
                src_ref=ins[a] if own else outs[a].at[slot(*block)], dst_ref=outs[a].at[slot(*block)],
                send_sem=send_sems.at[a * (N_DEV - 1) + k], recv_sem=recv_sems.at[a * (N_DEV - 1) + k],
                device_id=to, device_id_type=pl.DeviceIdType.MESH)

        me = (x, y, c)
        local = [pltpu.make_async_copy(ins[a], outs[a].at[slot(*me)], local_sems.at[a]) for a in range(n_arr)]
        first = [copy(a, 0, me, sibling, own=True) for a in range(n_arr)]
        first += [copy(a, 1 + j, me, (*chip, c), own=True) for j, chip in enumerate(chips) for a in range(n_arr)]
        for cp in local + first:
            cp.start()
        passed = []
        for j, chip in enumerate(chips):
            for a in range(n_arr):
                copy(a, 1 + j, (*chip, c), me).wait_recv()
                cp = copy(a, 4 + j, (*chip, c), sibling)
                cp.start()
                passed.append(cp)
        for a in range(n_arr):
            copy(a, 0, sibling, me).wait_recv()
            for j, chip in enumerate(chips):
                copy(a, 4 + j, (*chip, 1 - c), me).wait_recv()
        for cp in first + passed:
            cp.wait_send()
        for cp in local:
            cp.wait()

    return pl.pallas_call(
        body, name=name,
        in_specs=[HBM] * n_arr, out_specs=[HBM] * n_arr, out_shape=_xchg_out_shapes(arrays, [False] * n_arr),
        scratch_shapes=_xchg_sems(n_arr),
    )(*arrays)


ADAMW_VMEM_BUDGET = 36 * 1024 * 1024


def _adamw(recv, w, m, v, name):
    shape = w.shape
    C = shape[-1]
    R = math.prod(shape[:-1])
    lanes = -(-C // LANES) * LANES
    row_bytes = 2 * lanes * (N_DEV * recv.dtype.itemsize + 7 * 4)
    rc = _row_chunk(R, max(16, ADAMW_VMEM_BUDGET // row_bytes), 16 if recv.dtype == BF16 else 8)

    def body(r_ref, w_ref, m_ref, v_ref, g_ref, d_ref, mo_ref, vo_ref):
        g = r_ref[0].astype(F32)
        for s in range(1, N_DEV):
            g = g + r_ref[s].astype(F32)
        mn = ADAM_B1 * m_ref[...] + (1.0 - ADAM_B1) * g
        vn = ADAM_B2 * v_ref[...] + (1.0 - ADAM_B2) * (g * g)
        m_hat = mn / (1.0 - ADAM_B1 ** ADAM_STEP)
        v_hat = vn / (1.0 - ADAM_B2 ** ADAM_STEP)
        g_ref[...] = g
        d_ref[...] = -ADAM_LR * (m_hat / (jnp.sqrt(v_hat) + ADAM_EPS) + ADAM_WD * w_ref[...])
        mo_ref[...] = mn
        vo_ref[...] = vn

    row = pl.BlockSpec((rc, C), lambda i: (i, 0))
    outs = pl.pallas_call(
        body, name=name, grid=(R // rc,),
        in_specs=[pl.BlockSpec((N_DEV, rc, C), lambda i: (0, i, 0)), row, row, row],
        out_specs=[row] * 4,
        out_shape=[jax.ShapeDtypeStruct((R, C), F32)] * 4,
        compiler_params=_params(("parallel",)),
    )(recv.reshape(N_DEV, R, C), w.reshape(R, C), m.reshape(R, C), v.reshape(R, C))
    return [o.reshape(shape) for o in outs]


def _gathered_to_full(g, name):
    if name in COL_SHARDED:
        g = jnp.moveaxis(g, 0, -2)
        return g.reshape(g.shape[:-2] + (g.shape[-2] * g.shape[-1],))
    g = jnp.moveaxis(g, 0, -3)
    return g.reshape(g.shape[:-3] + (g.shape[-3] * g.shape[-2], g.shape[-1]))


def _full_to_slabs(full, name):
    if name in COL_SHARDED:
        f = full.reshape(full.shape[:-1] + (N_DEV, full.shape[-1] // N_DEV))
        return jnp.moveaxis(f, -2, 0)
    f = full.reshape(full.shape[:-2] + (N_DEV, full.shape[-2] // N_DEV, full.shape[-1]))
    return jnp.moveaxis(f, -3, 0)


def _pack_small(arrs):
    rows = []
    for a in arrs:
        flat = a.astype(F32).reshape(-1)
        pad = (-flat.shape[0]) % LANES
        rows.append(jnp.pad(flat, (0, pad)).reshape(-1, LANES))
    p = jnp.concatenate(rows, axis=0)
    return jnp.pad(p, ((0, (-p.shape[0]) % 8), (0, 0)))


def _unpack_small(packed, shapes):
    out, off = [], 0
    for shp in shapes:
        n = math.prod(shp)
        nr = -(-n // LANES)
        out.append(packed[off:off + nr].reshape(-1)[:n].reshape(shp))
        off += nr
    return out


def _local_step(x, target, meta, w_fox_in, late, small):
    S, D = x.shape
    Lp = S + HEAD_ROWS
    T = ROW_TILE if Lp % ROW_TILE == 0 else HEAD_ROWS
    P = D // LANES
    row = lambda v: v.reshape(1, -1).astype(F32)

    w_fin = jnp.pad(w_fox_in[0], ((0, 0), (0, LANES - w_fox_in.shape[-1] % LANES)))
    n_heads = w_fox_in.shape[-1] - 4 * D
    bf = jnp.pad(row(small["fox_b_f"]), ((0, 0), (0, LANES - small["fox_b_f"].size)))
    qg = jnp.tile(row(small["fox_q_norm"]), (1, 2))
    kg = jnp.tile(row(small["fox_k_norm"]), (1, 2))


    h0, hn0, hn0t = _embed_rms_fwd(x, meta, row(small["attn_norm"][0]), "rms0_fwd")
    proj0 = _mm(hn0, w_fin, "nn", F32, "fox_in_fwd")
    qn, ka, kb, vb, c = _fox_prep_fwd(proj0, bf, qg, kg, T, D, "fox_prep_fwd")
    o, og0, mshift, linv, *gathered = _fox_attn_fwd(qn, ka, kb, vb, c, proj0, [late[n] for n in LATE], T, D,
                                                    "fox_attn_fwd")
    wl = {n: _gathered_to_full(g, n) for n, g in zip(LATE, gathered)}
    w_fout, w_hin, w_hout = wl["fox_w_out"][0], wl["hgrn_w_in"][0], wl["hgrn_w_out"][0]
    w_uin, w_uout = wl["ffn_w_in"], wl["ffn_w_out"]
    h1, hf0, hf0t = _out_proj_fwd(og0, w_fout, h0, row(small["ffn_norm"][0]), "fox_out_fwd")
    gu0 = _ffn_in_fwd(hf0, w_uin[0], "ffn0_in_fwd")
    act0 = gu0[2]
    h2, hn1, hn1t = _out_proj_fwd(act0, w_uout[0], h1, row(small["attn_norm"][1]), "ffn0_out_fwd")
    proj1 = _mm(hn1, w_hin, "nn", F32, "hgrn_in_fwd")
    lbp = small["hgrn_lower_bounds"].astype(F32)
    ggn = row(small["hgrn_g_norm"])
    Th = HGRN_TILE if Lp % HGRN_TILE == 0 else HEAD_ROWS
    og1, ss = _hgrn_fwd(proj1, lbp, ggn, Th, "hgrn_fwd")
    h3, hf1, hf1t = _out_proj_fwd(og1, w_hout, h2, row(small["ffn_norm"][1]), "hgrn_out_fwd")
    gu1 = _ffn_in_fwd(hf1, w_uin[1], "ffn1_in_fwd")
    act1 = gu1[2]
    h4 = _mm(act1, w_uout[1], "nn", F32, "ffn1_out_fwd", res=h3)
    loss_blk, dh4, dh4b, d_final = _final_loss(h4, row(small["final_norm"]), target, "final_loss")

    grads = {}

    def ffn_bwd(i, dh, dhb, h_in, hft, gu):
        grads_out = _mm(gu[3], dhb, "nn", F32, f"ffn{i}_out_dw", tm=_tile(gu[3].shape[0], 1408), tk=_tile(Lp, 1408))
        dgu = _ffn_out_dx(dhb, w_uout[i], gu[0], gu[1], f"ffn{i}_out_dx")
        grads_in = _mm(hft, dgu, "nn", F32, f"ffn{i}_in_dw", tm=D, tk=_tile(Lp, 1408))
        dh_new, dh_newb, dgain = _in_proj_dx(dgu, w_uin[i], h_in, row(small["ffn_norm"][i]), dh, f"ffn{i}_in_dx")
        return dh_new, dh_newb, grads_in, grads_out, dgain

    dh3, dh3b, g_uin1, g_uout1, d_fn1 = ffn_bwd(1, dh4, dh4b, h3, hf1t, gu1)
    grads["hgrn_w_out"] = _mm(og1, dh3b, "tn", F32, "hgrn_out_dw")[None]
    dog1 = _mm(dh3b, w_hout, "nt", F32, "hgrn_out_dx")
    dproj1, d_lb, d_gg = _hgrn_bwd(proj1, lbp, ggn, dog1, ss, Th, "hgrn_bwd")
    grads["hgrn_w_in"] = _mm(hn1t, dproj1, "nn", F32, "hgrn_in_dw", tm=D, tk=_tile(Lp, 1408))[None]
    dh2, dh2b, d_an1 = _in_proj_dx(dproj1, w_hin, h2, row(small["attn_norm"][1]), dh3, "hgrn_in_dx")
    dh1, dh1b, g_uin0, g_uout0, d_fn0 = ffn_bwd(0, dh2, dh2b, h1, hf0t, gu0)
    grads["ffn_w_in"] = jnp.stack([g_uin0, g_uin1])
    grads["ffn_w_out"] = jnp.stack([g_uout0, g_uout1])
    grads["fox_w_out"] = _mm(og0, dh1b, "tn", F32, "fox_out_dw")[None]
    do, dgate, delta = _fox_out_dx(dh1b, w_fout, o, proj0, linv, D, "fox_out_dx")
    slabs = [_full_to_slabs(grads[n], n).astype(BF16) for n in LATE]
    dqn, dkn, dv, dcr, *recv = _fox_attn_bwd(qn, ka, kb, vb, c, do, mshift, delta, slabs, T, D, "fox_attn_bwd")
    for n in LATE:
        del grads[n]
    dc = jnp.pad(dcr[:, :2, :].reshape(2 * P, Lp).T, ((0, 0), (0, LANES - 2 * P)))
    Tp = T // 2 if T == ROW_TILE else T
    dproj0, sm = _fox_prep_bwd(proj0, bf, qg, kg, dqn, dkn, dv, dgate, dc, Tp, D, "fox_prep_bwd")
    g_fin = _mm(hn0t, dproj0, "nn", F32, "fox_in_dw", tm=D, tk=_tile(Lp, 1408))[:, :4 * D + n_heads][None]
    dh_head, grad_x, d_an0, r_fin = _in_proj_dx(dproj0, w_fin, h0, row(small["attn_norm"][0]), dh1, "fox_in_dx",
                                                xchg=[_full_to_slabs(g_fin, "fox_w_in").astype(BF16)], split_head=True)

    grads["meta_tokens"] = dh_head[N_PAD:]
    grads["attn_norm"] = jnp.concatenate([d_an0, d_an1], axis=0)
    grads["ffn_norm"] = jnp.concatenate([d_fn0, d_fn1], axis=0)
    grads["final_norm"] = d_final[0]
    grads["fox_b_f"] = sm[0:1, :n_heads]
    grads["fox_q_norm"] = sm[1:2, :FOX_DH] + sm[1:2, FOX_DH:]
    grads["fox_k_norm"] = sm[2:3, :FOX_DH] + sm[2:3, FOX_DH:]
    grads["hgrn_lower_bounds"] = d_lb[0:2]
    grads["hgrn_g_norm"] = d_gg[0:1]
    return loss_blk[0, 0], grad_x, grads, dict(zip(LATE, recv), fox_w_in=r_fin)


def kernel(x, meta_tokens, attn_norm, ffn_norm, final_norm, fox_w_in, fox_b_f, fox_q_norm, fox_k_norm, fox_w_out, hgrn_w_in, hgrn_lower_bounds, hgrn_g_norm, hgrn_w_out, ffn_w_in, ffn_w_out, loss_target, m_meta_tokens, m_attn_norm, m_ffn_norm, m_final_norm, m_fox_w_in, m_fox_b_f, m_fox_q_norm, m_fox_k_norm, m_fox_w_out, m_hgrn_w_in, m_hgrn_lower_bounds, m_hgrn_g_norm, m_hgrn_w_out, m_ffn_w_in, m_ffn_w_out, v_meta_tokens, v_attn_norm, v_ffn_norm, v_final_norm, v_fox_w_in, v_fox_b_f, v_fox_q_norm, v_fox_k_norm, v_fox_w_out, v_hgrn_w_in, v_hgrn_lower_bounds, v_hgrn_g_norm, v_hgrn_w_out, v_ffn_w_in, v_ffn_w_out):
    w = dict(meta_tokens=meta_tokens, attn_norm=attn_norm, ffn_norm=ffn_norm, final_norm=final_norm,
             fox_w_in=fox_w_in, fox_b_f=fox_b_f, fox_q_norm=fox_q_norm, fox_k_norm=fox_k_norm,
             fox_w_out=fox_w_out, hgrn_w_in=hgrn_w_in, hgrn_lower_bounds=hgrn_lower_bounds,
             hgrn_g_norm=hgrn_g_norm, hgrn_w_out=hgrn_w_out, ffn_w_in=ffn_w_in, ffn_w_out=ffn_w_out)
    m = dict(meta_tokens=m_meta_tokens, attn_norm=m_attn_norm, ffn_norm=m_ffn_norm, final_norm=m_final_norm,
             fox_w_in=m_fox_w_in, fox_b_f=m_fox_b_f, fox_q_norm=m_fox_q_norm, fox_k_norm=m_fox_k_norm,
             fox_w_out=m_fox_w_out, hgrn_w_in=m_hgrn_w_in, hgrn_lower_bounds=m_hgrn_lower_bounds,
             hgrn_g_norm=m_hgrn_g_norm, hgrn_w_out=m_hgrn_w_out, ffn_w_in=m_ffn_w_in, ffn_w_out=m_ffn_w_out)
    v = dict(meta_tokens=v_meta_tokens, attn_norm=v_attn_norm, ffn_norm=v_ffn_norm, final_norm=v_final_norm,
             fox_w_in=v_fox_w_in, fox_b_f=v_fox_b_f, fox_q_norm=v_fox_q_norm, fox_k_norm=v_fox_k_norm,
             fox_w_out=v_fox_w_out, hgrn_w_in=v_hgrn_w_in, hgrn_lower_bounds=v_hgrn_lower_bounds,
             hgrn_g_norm=v_hgrn_g_norm, hgrn_w_out=v_hgrn_w_out, ffn_w_in=v_ffn_w_in, ffn_w_out=v_ffn_w_out)
    axes = ("x", "y", "c")
    small_shapes = [w[n].shape for n in SMALL]

    g_meta, g_fin = _gather_two_level([w["meta_tokens"].astype(F32), w["fox_w_in"].astype(BF16)], "gather_weights")
    loss_local, grad_x, grads, recv = _local_step(
        x[0], loss_target[0], _gathered_to_full(g_meta, "meta_tokens"), _gathered_to_full(g_fin, "fox_w_in"),
        {n: w[n].astype(BF16) for n in LATE}, {n: w[n] for n in SMALL})
    loss = lax.psum(loss_local, axes)

    r_meta, r_small = _exchange([_full_to_slabs(grads["meta_tokens"], "meta_tokens"),
                                 _pack_small([grads[n] for n in SMALL])], [True, False], "scatter_grads")
    recv.update(meta_tokens=r_meta)

    res = {n: _adamw(recv[n], w[n], m[n], v[n], "adamw_" + n) for n in BIG}
    sml = _adamw(r_small, _pack_small([w[n] for n in SMALL]), _pack_small([m[n] for n in SMALL]),
                 _pack_small([v[n] for n in SMALL]), "adamw_small")
    outs = []
    for k in range(4):
        d = {n: res[n][k] for n in BIG}
        d.update(zip(SMALL, _unpack_small(sml[k], small_shapes)))
        outs.extend(d[n] for n in WEIGHTS)
    return (loss, grad_x[None], *outs)
```

```python
import functools
import math

import jax
import jax.numpy as jnp
from jax import lax
from jax.experimental import pallas as pl
from jax.experimental.pallas import tpu as pltpu

F32 = jnp.float32
BF16 = jnp.bfloat16
EPS = 1e-6
N_META = 16
LANES = 128
HEAD_ROWS = 256
ROW_TILE = 768
N_PAD = HEAD_ROWS - N_META
FOX_DH = 64
HGRN_CHUNK = 64
HGRN_TILE = 384
N_DEV = 8
NEG = -1e30
PAD_SHIFT = 1e4
VMEM_LIMIT = 56 * 1024 * 1024
HI = lax.Precision.HIGHEST
LOG2E = 1.0 / math.log(2.0)
LN2 = math.log(2.0)

ADAM_LR = 0.001
ADAM_B1 = 0.9
ADAM_B2 = 0.999
ADAM_EPS = 1e-08
ADAM_WD = 0.01
ADAM_STEP = 10

BIG = ("meta_tokens", "fox_w_in", "fox_w_out", "hgrn_w_in", "hgrn_w_out", "ffn_w_in", "ffn_w_out")
SMALL = ("attn_norm", "ffn_norm", "final_norm", "fox_b_f", "fox_q_norm", "fox_k_norm",
         "hgrn_lower_bounds", "hgrn_g_norm")
WEIGHTS = ("meta_tokens", "attn_norm", "ffn_norm", "final_norm", "fox_w_in", "fox_b_f", "fox_q_norm",
           "fox_k_norm", "fox_w_out", "hgrn_w_in", "hgrn_lower_bounds", "hgrn_g_norm", "hgrn_w_out",
           "ffn_w_in", "ffn_w_out")
COL_SHARDED = ("meta_tokens", "fox_w_in", "hgrn_w_in", "ffn_w_in")
LATE = ("fox_w_out", "hgrn_w_in", "hgrn_w_out", "ffn_w_in", "ffn_w_out")


def _params(sem=None):
    return pltpu.CompilerParams(dimension_semantics=sem, vmem_limit_bytes=VMEM_LIMIT)


def _tile(n, cap):
    best = None
    for t in range(LANES, min(n, cap) + 1, LANES):
        if n % t == 0:
            best = t
    assert best is not None, (n, cap)
    return best


def _row_chunk(n, cap, mult=8):
    best = n
    for t in range(mult, min(n, cap) + 1, mult):
        if n % t == 0:
            best = t
    return best


def _dg(a, b, ca, cb):
    return lax.dot_general(a.astype(BF16), b.astype(BF16), (((ca,), (cb,)), ((), ())),
                           preferred_element_type=F32)


@jax.custom_vjp
def _d_nn(a, b):
    return _dg(a, b, 1, 0)


@jax.custom_vjp
def _d_nt(a, b):
    return _dg(a, b, 1, 1)


@jax.custom_vjp
def _d_tn(a, b):
    return _dg(a, b, 0, 0)


_d_nn.defvjp(lambda a, b: (_d_nn(a, b), (a, b)), lambda r, g: (_d_nt(g, r[1]), _d_tn(r[0], g)))
_d_nt.defvjp(lambda a, b: (_d_nt(a, b), (a, b)), lambda r, g: (_d_nn(g, r[1]), _d_tn(g, r[0])))
_d_tn.defvjp(lambda a, b: (_d_tn(a, b), (a, b)), lambda r, g: (_d_nt(r[1], g), _d_nn(r[0], g)))


def _log_sigmoid(x):
    return jnp.minimum(x, 0.0) - jnp.log1p(jnp.exp(-jnp.abs(x)))


def _rms(x, g):
    return x * lax.rsqrt(jnp.mean(x * x, axis=-1, keepdims=True) + EPS) * g


def _mm(a, b, mode, out_dtype, name, res=None, tm=None, tn=None, tk=None):
    assert a.dtype == BF16 and b.dtype == BF16, (name, a.dtype, b.dtype)
    if mode == "nn":
        (M, K), N = a.shape, b.shape[1]
    elif mode == "nt":
        (M, K), N = a.shape, b.shape[0]
    else:
        (K, M), N = a.shape, b.shape[1]
    if mode == "nn":
        tm, tn, tk = tm or _tile(M, ROW_TILE), tn or _tile(N, 1408), tk or _tile(K, 2816)
    elif mode == "nt":
        tm, tn, tk = tm or _tile(M, ROW_TILE if K <= 2048 else ROW_TILE // 2), tn or N, tk or K
    else:
        tm, tn, tk = tm or _tile(M, 1408), tn or _tile(N, 1408), tk or _tile(K, ROW_TILE)
    nk = K // tk
    if mode == "tn":
        a_spec = pl.BlockSpec((tk, tm), lambda j, i, k: (k, i))
        dims = (((0,), (0,)), ((), ()))
    else:
        a_spec = pl.BlockSpec((tm, tk), lambda j, i, k: (i, k))
        dims = (((1,), (1 if mode == "nt" else 0,)), ((), ()))
    if mode == "nt":
        b_spec = pl.BlockSpec((tn, tk), lambda j, i, k: (j, k))
    else:
        b_spec = pl.BlockSpec((tk, tn), lambda j, i, k: (k, j))

    o_spec = pl.BlockSpec((tm, tn), lambda j, i, k: (i, j))

    def body(a_ref, b_ref, *rest):
        r_ref = rest[0] if res is not None else None
        o_ref, acc_ref = rest[-2:]
        k = pl.program_id(2)

        @pl.when(k == 0)
        def _():
            acc_ref[...] = jnp.zeros_like(acc_ref)

        acc_ref[...] += lax.dot_general(a_ref[...], b_ref[...], dims, preferred_element_type=F32)

        @pl.when(k == nk - 1)
        def _():
            out = acc_ref[...] if r_ref is None else acc_ref[...] + r_ref[...]
            o_ref[...] = out.astype(out_dtype)

    return pl.pallas_call(
        body, name=name, grid=(N // tn, M // tm, nk),
        in_specs=[a_spec, b_spec] + ([o_spec] if res is not None else []),
        out_specs=o_spec,
        out_shape=jax.ShapeDtypeStruct((M, N), out_dtype),
        scratch_shapes=[pltpu.VMEM((tm, tn), F32)],
        compiler_params=_params(("parallel", "parallel", "arbitrary")),
    )(a, b, *([res] if res is not None else []))


def _out_proj_fwd(a, w, res, gain, name):
    Lp, K = a.shape
    D = w.shape[1]
    tm = _tile(Lp, ROW_TILE)

    def body(a_ref, w_ref, r_ref, g_ref, h_ref, hn_ref, hnt_ref):
        h = jnp.dot(a_ref[...], w_ref[...], preferred_element_type=F32) + r_ref[...]
        h_ref[...] = h
        hn = _rms(h, g_ref[...])
        hn_ref[...] = hn.astype(BF16)
        hnt_ref[...] = hn.T.astype(BF16)

    row = pl.BlockSpec((tm, D), lambda i: (i, 0))
    return pl.pallas_call(
        body, name=name, grid=(Lp // tm,),
        in_specs=[pl.BlockSpec((tm, K), lambda i: (i, 0)), pl.BlockSpec((K, D), lambda i: (0, 0)), row,
                  pl.BlockSpec((1, D), lambda i: (0, 0))],
        out_specs=[row, row, pl.BlockSpec((D, tm), lambda i: (0, i))],
        out_shape=[jax.ShapeDtypeStruct((Lp, D), F32), jax.ShapeDtypeStruct((Lp, D), BF16),
                   jax.ShapeDtypeStruct((D, Lp), BF16)],
        compiler_params=_params(("parallel",)),
    )(a, w, res, gain)


def _in_proj_dx(dy, w, x, gain, dres, name, xchg=(), split_head=False):
    Lp, N = dy.shape
    D = w.shape[0]
    tm = HEAD_ROWS if split_head else _tile(Lp, ROW_TILE // 2)
    nt = Lp // tm
    nx = len(xchg)

    def body(dy_ref, w_ref, x_ref, g_ref, dr_ref, *rest):
        x_in, (dx_ref, dxb_ref, dg_ref), x_out, sems = rest[:nx], rest[nx:nx + 3], rest[nx + 3:2 * nx + 3], rest[2 * nx + 3:]

        @pl.when(pl.program_id(0) == 0)
        def _():
            dg_ref[...] = jnp.zeros_like(dg_ref)
            if nx:
                for cp in _xchg_copies(x_in, x_out, [True] * nx, sems):
                    cp.start()

        dhn = lax.dot_general(dy_ref[...], w_ref[...], (((1,), (1,)), ((), ())), preferred_element_type=F32)
        _, vjp = jax.vjp(_rms, x_ref[...], g_ref[...])
        dx, dg = vjp(dhn)
        dx = dx + dr_ref[...]
        if split_head:
            @pl.when(pl.program_id(0) == 0)
            def _():
                dx_ref[...] = dx

            @pl.when(pl.program_id(0) > 0)
            def _():
                dxb_ref[...] = dx
        else:
            dx_ref[...] = dx
            dxb_ref[...] = dx.astype(BF16)
        dg_ref[...] += dg

        if nx:
            @pl.when(pl.program_id(0) == nt - 1)
            def _():
                for cp in _xchg_copies(x_in, x_out, [True] * nx, sems):
                    cp.wait()

    row = pl.BlockSpec((tm, D), lambda i: (i, 0))
    vec = pl.BlockSpec((1, D), lambda i: (0, 0))
    HBM = pl.BlockSpec(memory_space=pltpu.HBM)
    return pl.pallas_call(
        body, name=name, grid=(nt,),
        in_specs=[pl.BlockSpec((tm, N), lambda i: (i, 0)), pl.BlockSpec((D, N), lambda i: (0, 0)), row, vec, row]
        + [HBM] * nx,
        out_specs=([pl.BlockSpec((tm, D), lambda i: (0, 0)), pl.BlockSpec((tm, D), lambda i: (jnp.maximum(i - 1, 0), 0))]
                   if split_head else [row, row]) + [vec] + [HBM] * nx,
        out_shape=([jax.ShapeDtypeStruct((tm, D), F32), jax.ShapeDtypeStruct((Lp - tm, D), F32)] if split_head else
                   [jax.ShapeDtypeStruct((Lp, D), F32), jax.ShapeDtypeStruct((Lp, D), BF16)])
        + [jax.ShapeDtypeStruct((1, D), F32)] + _xchg_out_shapes(xchg, [True] * nx),
        scratch_shapes=_xchg_sems(nx) if nx else [],
        compiler_params=_params(("arbitrary",)),
    )(dy, w, x, gain, dres, *xchg)


def _embed_rms_fwd(x, meta, g, gather, name):
    S, D = x.shape
    TR = HEAD_ROWS
    Lp = S + TR
    ng = len(gather)

    def body(x_ref, m_ref, g_ref, *rest):
        g_in, (h_ref, o_ref, ot_ref), g_out, sems = rest[:ng], rest[ng:ng + 3], rest[ng + 3:2 * ng + 3], rest[2 * ng + 3:]
        i = pl.program_id(0)

        @pl.when(i == 0)
        def _():
            local, first = _gather2_first(g_in, g_out, sems)
            for cp in local + first:
                cp.start()

        @pl.when(i == 0)
        def _():
            h_ref[...] = jnp.zeros_like(h_ref)
            h_ref[N_PAD:, :] = m_ref[...]

        @pl.when(i > 0)
        def _():
            h_ref[...] = x_ref[...]

        y = _rms(h_ref[...], g_ref[...])
        o_ref[...] = y.astype(BF16)
        ot_ref[...] = y.T.astype(BF16)

        @pl.when(i == Lp // TR - 1)
        def _():
            _gather2_finish(g_in, g_out, sems)

    row = pl.BlockSpec((TR, D), lambda i: (i, 0))
    HBM = pl.BlockSpec(memory_space=pltpu.HBM)
    return pl.pallas_call(
        body, name=name, grid=(Lp // TR,),
        in_specs=[pl.BlockSpec((TR, D), lambda i: (jnp.maximum(i - 1, 0), 0)),
                  pl.BlockSpec((N_META, D), lambda i: (0, 0)), pl.BlockSpec((1, D), lambda i: (0, 0))] + [HBM] * ng,
        out_specs=[row, row, pl.BlockSpec((D, TR), lambda i: (0, i))] + [HBM] * ng,
        out_shape=[jax.ShapeDtypeStruct((Lp, D), F32), jax.ShapeDtypeStruct((Lp, D), BF16),
                   jax.ShapeDtypeStruct((D, Lp), BF16)] + _xchg_out_shapes(gather, [False] * ng),
        scratch_shapes=_xchg_sems(ng),
        compiler_params=_params(("arbitrary",)),
    )(x, meta, g, *gather)


def _swiglu(gate, up):
    return gate * jax.nn.sigmoid(gate) * up


def _ffn_in_fwd(hf, w_in, name):
    Lp, D = hf.shape
    F = w_in.shape[1] // 2
    tm = _tile(Lp, ROW_TILE)
    tn = _tile(F, 1408)
    nj = F // tn

    def body(a_ref, bg_ref, bu_ref, g_ref, u_ref, act_ref, actt_ref):
        a = a_ref[...]
        g = jnp.dot(a, bg_ref[...], preferred_element_type=F32)
        u = jnp.dot(a, bu_ref[...], preferred_element_type=F32)
        g_ref[...] = g.astype(BF16)
        u_ref[...] = u.astype(BF16)
        act = _swiglu(g, u)
        act_ref[...] = act.astype(BF16)
        actt_ref[...] = act.T.astype(BF16)

    tile = pl.BlockSpec((tm, tn), lambda j, i: (i, j))
    return pl.pallas_call(
        body, name=name, grid=(nj, Lp // tm),
        in_specs=[pl.BlockSpec((tm, D), lambda j, i: (i, 0)), pl.BlockSpec((D, tn), lambda j, i: (0, j)),
                  pl.BlockSpec((D, tn), lambda j, i: (0, nj + j))],
        out_specs=[tile, tile, tile, pl.BlockSpec((tn, tm), lambda j, i: (j, i))],
        out_shape=[jax.ShapeDtypeStruct((Lp, F), BF16)] * 3 + [jax.ShapeDtypeStruct((F, Lp), BF16)],
        compiler_params=_params(("parallel", "parallel")),
    )(hf, w_in, w_in)


def _ffn_out_dx(dhb, w_out, g, u, name):
    Lp, D = dhb.shape
    F = w_out.shape[0]
    tm = HEAD_ROWS

    def body(a_ref, b_ref, g_ref, u_ref, o_ref):
        dact = lax.dot_general(a_ref[...], b_ref[...], (((1,), (1,)), ((), ())), preferred_element_type=F32)
        _, vjp = jax.vjp(_swiglu, g_ref[...].astype(F32), u_ref[...].astype(F32))
        dg, du = vjp(dact)
        o_ref[:, :F] = dg.astype(BF16)
        o_ref[:, F:] = du.astype(BF16)

    wide = pl.BlockSpec((tm, F), lambda i: (i, 0))
    return pl.pallas_call(
        body, name=name, grid=(Lp // tm,),
        in_specs=[pl.BlockSpec((tm, D), lambda i: (i, 0)), pl.BlockSpec((F, D), lambda i: (0, 0)), wide, wide],
        out_specs=pl.BlockSpec((tm, 2 * F), lambda i: (i, 0)),
        out_shape=jax.ShapeDtypeStruct((Lp, 2 * F), BF16),
        compiler_params=_params(("parallel",)),
    )(dhb, w_out, g, u)


def _final_loss(h, g, target, name):
    Lp, D = h.shape
    TR = HEAD_ROWS

    def loss_fn(hh, gg, tt):
        err = _rms(hh, gg) - tt
        return 0.5 * jnp.sum(jnp.mean(err * err, axis=-1))

    def body(h_ref, g_ref, t_ref, loss_ref, dh_ref, dhb_ref, dg_ref):
        i = pl.program_id(0)

        @pl.when(i == 0)
        def _():
            loss_ref[...] = jnp.zeros_like(loss_ref)
            dg_ref[...] = jnp.zeros_like(dg_ref)
            dh_ref[...] = jnp.zeros_like(dh_ref)
            dhb_ref[...] = jnp.zeros_like(dhb_ref)

        @pl.when(i > 0)
        def _():
            val, vjp = jax.vjp(lambda hh, gg: loss_fn(hh, gg, t_ref[...]), h_ref[...], g_ref[...])
            dh, dg = vjp(jnp.ones((), F32))
            dh_ref[...] = dh
            dhb_ref[...] = dh.astype(BF16)
            dg_ref[...] += dg
            loss_ref[...] += val

    row = pl.BlockSpec((TR, D), lambda i: (i, 0))
    return pl.pallas_call(
        body, name=name, grid=(Lp // TR,),
        in_specs=[row, pl.BlockSpec((1, D), lambda i: (0, 0)),
                  pl.BlockSpec((TR, D), lambda i: (jnp.maximum(i - 1, 0), 0))],
        out_specs=[pl.BlockSpec((8, LANES), lambda i: (0, 0)), row, row, pl.BlockSpec((1, D), lambda i: (0, 0))],
        out_shape=[jax.ShapeDtypeStruct((8, LANES), F32), jax.ShapeDtypeStruct((Lp, D), F32),
                   jax.ShapeDtypeStruct((Lp, D), BF16), jax.ShapeDtypeStruct((1, D), F32)],
        compiler_params=_params(("arbitrary",)),
    )(h, g, target)


def _lane_lo():
    return lax.broadcasted_iota(jnp.int32, (1, LANES), 1) < FOX_DH


def _headnorm(x, g, scale):
    lo = _lane_lo()
    x2 = x * x
    s0 = jnp.sum(jnp.where(lo, x2, 0.0), axis=-1, keepdims=True)
    s1 = jnp.sum(jnp.where(lo, 0.0, x2), axis=-1, keepdims=True)
    r = jnp.where(lo, lax.rsqrt(s0 / FOX_DH + EPS), lax.rsqrt(s1 / FOX_DH + EPS))
    return x * r * g * scale


AUG = 3


def _split3(x):
    hi = x.astype(BF16).astype(F32)
    mid = (x - hi).astype(BF16).astype(F32)
    return hi, mid, x - hi - mid


def _fox_prep_fwd(proj, bf, qg, kg, T, D, name):
    Lp = proj.shape[0]
    nb = D // LANES
    scale = FOX_DH ** -0.5 * LOG2E

    def body(q_ref, k_ref, v_ref, fl_ref, bf_ref, qg_ref, kg_ref, qn_ref, ka_ref, kb_ref, vb_ref, c_ref, carry_ref):
        i = pl.program_id(0)

        @pl.when(i == 0)
        def _():
            carry_ref[...] = jnp.zeros_like(carry_ref)

        vb_ref[...] = v_ref[...].astype(BF16)
        log_f = _log_sigmoid(fl_ref[...] + bf_ref[...])
        row = lax.broadcasted_iota(jnp.int32, (T, T), 0)
        col = lax.broadcasted_iota(jnp.int32, (T, T), 1)
        tri = (col <= row).astype(F32)
        c = jnp.dot(tri, log_f, precision=HI, preferred_element_type=F32) + carry_ref[...]
        c2 = c * LOG2E
        c_ref[...] = c2
        last = lax.broadcasted_iota(jnp.int32, (T, 1), 0) == T - 1
        carry_ref[...] = jnp.sum(jnp.where(last, c, 0.0), axis=0, keepdims=True)

        is_pad = (i * T + lax.broadcasted_iota(jnp.int32, (T, 1), 0)) < N_PAD
        negc = jnp.where(is_pad, -PAD_SHIFT, -c2)
        lane = lax.broadcasted_iota(jnp.int32, (1, LANES), 1)
        for b in range(nb):
            sl = slice(b * LANES, (b + 1) * LANES)
            qn_ref[:, sl] = _headnorm(q_ref[:, sl], qg_ref[...], scale).astype(BF16)
            kn = _headnorm(k_ref[:, sl], kg_ref[...], 1.0)
            ka = jnp.where(lane < FOX_DH, kn, 0.0)
            kb = jnp.where(lane < FOX_DH, 0.0, kn)
            for n, (pa, pb) in enumerate(zip(_split3(_pick_col(negc, 2 * b)), _split3(_pick_col(negc, 2 * b + 1)))):
                ka = jnp.where(lane == FOX_DH + n, pa, ka)
                kb = jnp.where(lane == n, pb, kb)
            ka_ref[:, sl] = ka.astype(BF16)
            kb_ref[:, sl] = kb.astype(BF16)

    wide = lambda j: pl.BlockSpec((T, D), lambda i: (i, j))
    vec = pl.BlockSpec((1, LANES), lambda i: (0, 0))
    return pl.pallas_call(
        body, name=name, grid=(Lp // T,),
        in_specs=[wide(0), wide(1), wide(2), pl.BlockSpec((T, LANES), lambda i: (i, 4 * nb)), vec, vec, vec],
        out_specs=[wide(0), wide(0), wide(0), wide(0), pl.BlockSpec((T, LANES), lambda i: (i, 0))],
        out_shape=[jax.ShapeDtypeStruct((Lp, D), BF16)] * 4 + [jax.ShapeDtypeStruct((Lp, LANES), F32)],
        scratch_shapes=[pltpu.VMEM((1, LANES), F32)],
        compiler_params=_params(("arbitrary",)),
    )(proj, proj, proj, proj, bf, qg, kg)


def _fox_prep_bwd(proj, bf, qg, kg, dqn, dkn, dv, dgate, dc, T, D, name):
    Lp = proj.shape[0]
    nb = D // LANES
    nt = Lp // T
    scale = FOX_DH ** -0.5 * LOG2E

    def body(q_ref, k_ref, fl_ref, bf_ref, qg_ref, kg_ref, dqn_ref, dkn_ref, dv_ref, dgate_ref, dc_ref,
             dproj_ref, sm_ref, carry_ref):
        @pl.when(pl.program_id(0) == 0)
        def _():
            carry_ref[...] = jnp.zeros_like(carry_ref)
            sm_ref[...] = jnp.zeros_like(sm_ref)

        dqg = jnp.zeros((1, LANES), F32)
        dkg = jnp.zeros((1, LANES), F32)
        for b in range(nb):
            sl = slice(b * LANES, (b + 1) * LANES)
            _, vjp = jax.vjp(lambda x, g: _headnorm(x, g, scale), q_ref[:, sl], qg_ref[...])
            dx, dg = vjp(dqn_ref[:, sl] * LN2)
            dproj_ref[:, sl] = dx.astype(BF16)
            dqg = dqg + dg
            _, vjp = jax.vjp(lambda x, g: _headnorm(x, g, 1.0), k_ref[:, sl], kg_ref[...])
            dx, dg = vjp(dkn_ref[:, sl] * LN2)
            dproj_ref[:, D + b * LANES:D + (b + 1) * LANES] = dx.astype(BF16)
            dkg = dkg + dg
        dproj_ref[:, 2 * D:3 * D] = dv_ref[...].astype(BF16)
        dproj_ref[:, 3 * D:4 * D] = dgate_ref[...]
        dcv = dc_ref[...]
        row = lax.broadcasted_iota(jnp.int32, (T, T), 0)
        col = lax.broadcasted_iota(jnp.int32, (T, T), 1)
        triu = (col >= row).astype(F32)
        dlogf = jnp.dot(triu, dcv, precision=HI, preferred_element_type=F32) + carry_ref[...]
        carry_ref[...] += jnp.sum(dcv, axis=0, keepdims=True)
        _, vjp = jax.vjp(_log_sigmoid, fl_ref[...] + bf_ref[...])
        (dfl,) = vjp(dlogf)
        dproj_ref[:, 4 * D:] = dfl.astype(BF16)
        sm_ref[0:1, :] += jnp.sum(dfl, axis=0, keepdims=True)
        sm_ref[1:2, :] += dqg
        sm_ref[2:3, :] += dkg

    wide = lambda j: pl.BlockSpec((T, D), lambda i: (nt - 1 - i, j))
    narrow = lambda j: pl.BlockSpec((T, LANES), lambda i: (nt - 1 - i, j))
    vec = pl.BlockSpec((1, LANES), lambda i: (0, 0))
    return pl.pallas_call(
        body, name=name, grid=(nt,),
        in_specs=[wide(0), wide(1), narrow(4 * nb), vec, vec, vec, wide(0), wide(0), wide(0), wide(0), narrow(0)],
        out_specs=[pl.BlockSpec((T, 4 * D + LANES), lambda i: (nt - 1 - i, 0)), pl.BlockSpec((8, LANES), lambda i: (0, 0))],
        out_shape=[jax.ShapeDtypeStruct((Lp, 4 * D + LANES), BF16), jax.ShapeDtypeStruct((8, LANES), F32)],
        scratch_shapes=[pltpu.VMEM((1, LANES), F32)],
        compiler_params=_params(("arbitrary",)),
    )(proj, proj, proj, bf, qg, kg, dqn, dkn, dv, dgate, dc)


def _fox_q_operands(q):
    lane = lax.broadcasted_iota(jnp.int32, (1, LANES), 1)
    zero, one = jnp.zeros_like(q), jnp.ones_like(q)
    return (jnp.where(lane < FOX_DH, q, jnp.where(lane < FOX_DH + AUG, one, zero)),
            jnp.where(lane < FOX_DH, jnp.where(lane < AUG, one, zero), q))


def _fox_mask(q0, nq, k0, nk):
    qpos = q0 + lax.broadcasted_iota(jnp.int32, (nq, 1), 0)
    kpos = k0 + lax.broadcasted_iota(jnp.int32, (1, nk), 1)
    return (kpos <= qpos) & ((kpos >= N_PAD) | (qpos < N_PAD))


def _diag_split(T):
    return 512 if T == 768 else T // 2


def _rows_joined(head_rows, tail_rows, n_head):
    return jnp.concatenate([head_rows[:n_head], tail_rows], axis=0)


def _pick_col(blk, idx):
    lane = lax.broadcasted_iota(jnp.int32, (1, LANES), 1)
    return jnp.sum(jnp.where(lane == idx, blk, 0.0), axis=1, keepdims=True)


def _split_halves(blk):
    lo = _lane_lo()
    return (jnp.max(jnp.where(lo, blk, -jnp.inf), axis=1, keepdims=True),
            jnp.max(jnp.where(lo, -jnp.inf, blk), axis=1, keepdims=True))


def _fox_attn_fwd(qn, ka, kb, vb, c, proj, xchg, T, D, name):
    Lp = qn.shape[0]
    P = D // LANES
    nt = Lp // T
    nx = len(xchg)

    def body(q_ref, ka_ref, kb_ref, v_ref, c_ref, g_ref, *rest):
        x_in, (o_ref, og_ref, m_ref, li_ref), x_out, sems = rest[:nx], rest[nx:nx + 4], rest[nx + 4:2 * nx + 4], rest[2 * nx + 4:]
        p = pl.program_id(0)
        i = pl.program_id(1)

        @pl.when((p == 0) & (i == 0))
        def _():
            for cp in _xchg_copies(x_in, x_out, [False] * nx, sems):
                cp.start()

        lo = _lane_lo()
        q = q_ref[...]
        qh = _fox_q_operands(q)
        cblk = c_ref[...]
        cq = tuple(_pick_col(cblk, 2 * p + h) for h in (0, 1))
        one = jnp.ones_like(q)

        def block(r0, nr, k0, nk, carry, masked):
            kj = (ka_ref[pl.ds(k0, nk), :], kb_ref[pl.ds(k0, nk), :])
            vj = v_ref[pl.ds(k0, nk), :]
            vh = (jnp.where(lo, vj, one[:nk]), jnp.where(lo, one[:nk], vj))
            mask = _fox_mask(i * T + r0, nr, k0, nk) if masked else None
            out = []
            for h in (0, 1):
                m, acc = carry[h]
                cqr = cq[h][r0:r0 + nr]
                t = lax.dot_general(qh[h][r0:r0 + nr], kj[h], (((1,), (1,)), ((), ())), preferred_element_type=F32)
                if masked:
                    t = jnp.where(mask, t, NEG)
                m_new = jnp.ceil(jnp.maximum(m, cqr + jnp.max(t, axis=1, keepdims=True)))
                pr = jnp.exp2(t + (cqr - m_new)).astype(BF16)
                acc = jnp.exp2(m - m_new) * acc + jnp.dot(pr, vh[h], preferred_element_type=F32)
                out.append((m_new, acc))
            return tuple(out)

        init = tuple((jnp.full((T, 1), NEG, F32), jnp.zeros((T, LANES), F32)) for _ in (0, 1))
        carry = lax.fori_loop(0, i, lambda j, cr: block(0, T, pl.multiple_of(j * T, LANES), T, cr, False), init)
        (m0, a0), (m1, a1) = block(0, T, pl.multiple_of(i * T, LANES), T, carry, True)
        l0 = pltpu.roll(a0, FOX_DH, 1)
        l1 = pltpu.roll(a1, FOX_DH, 1)
        o = jnp.where(lo, a0 / l0, a1 / l1)
        o_ref[...] = o
        m_ref[...] = jnp.where(lo, m0, m1)
        li_ref[...] = jnp.where(lo, 1.0 / l0, 1.0 / l1)
        og_ref[...] = (o * jax.nn.sigmoid(g_ref[...])).astype(BF16)

        @pl.when((p == P - 1) & (i == nt - 1))
        def _():
            for cp in _xchg_copies(x_in, x_out, [False] * nx, sems):
                cp.wait()

    tile = pl.BlockSpec((T, LANES), lambda p, i: (i, p))
    full = pl.BlockSpec((Lp, LANES), lambda p, i: (0, p))
    HBM = pl.BlockSpec(memory_space=pltpu.HBM)
    return pl.pallas_call(
        body, name=name, grid=(P, nt),
        in_specs=[tile, full, full, full, pl.BlockSpec((T, LANES), lambda p, i: (i, 0)),
                  pl.BlockSpec((T, LANES), lambda p, i: (i, 3 * P + p))] + [HBM] * nx,
        out_specs=[tile, tile, tile, tile] + [HBM] * nx,
        out_shape=[jax.ShapeDtypeStruct((Lp, D), F32), jax.ShapeDtypeStruct((Lp, D), BF16),
                   jax.ShapeDtypeStruct((Lp, D), F32), jax.ShapeDtypeStruct((Lp, D), F32)]
        + _xchg_out_shapes(xchg, [False] * nx),
        scratch_shapes=_xchg_sems(nx),
        compiler_params=_params(("arbitrary", "arbitrary")),
    )(qn, ka, kb, vb, c, proj, *xchg)


def _fox_out_dx(dhb, w_out, o, proj, linv, D, name):
    Lp = o.shape[0]
    tm = HEAD_ROWS

    def body(a_ref, w_ref, o_ref, g_ref, li_ref, do_ref, dg_ref, dl_ref):
        lo = _lane_lo()
        dog_all = lax.dot_general(a_ref[...], w_ref[...], (((1,), (1,)), ((), ())), preferred_element_type=F32)
        for b in range(D // LANES):
            sl = slice(b * LANES, (b + 1) * LANES)
            dog = dog_all[:, sl]
            sig = jax.nn.sigmoid(g_ref[:, sl])
            ov = o_ref[:, sl]
            do = (dog * sig * li_ref[:, sl]).astype(BF16)
            do_ref[:, sl] = do
            dg_ref[:, sl] = (dog * ov * sig * (1.0 - sig)).astype(BF16)
            t = do.astype(F32) * ov
            d0 = jnp.sum(jnp.where(lo, t, 0.0), axis=1, keepdims=True)
            d1 = jnp.sum(jnp.where(lo, 0.0, t), axis=1, keepdims=True)
            dl_ref[:, sl] = jnp.where(lo, d0, d1)

    row = pl.BlockSpec((tm, D), lambda i: (i, 0))
    return pl.pallas_call(
        body, name=name, grid=(Lp // tm,),
        in_specs=[row, pl.BlockSpec((D, D), lambda i: (0, 0)), row, pl.BlockSpec((tm, D), lambda i: (i, 3)), row],
        out_specs=[row, row, row],
        out_shape=[jax.ShapeDtypeStruct((Lp, D), BF16), jax.ShapeDtypeStruct((Lp, D), BF16),
                   jax.ShapeDtypeStruct((Lp, D), F32)],
        compiler_params=_params(("parallel",)),
    )(dhb, w_out, o, proj, linv)


def _fox_attn_bwd(qn, ka, kb, vb, c, do, mshift, delta, xchg, T, D, name):
    Lp = qn.shape[0]
    P = D // LANES
    nt = Lp // T
    nx = len(xchg)

    def body(q_ref, do_ref, m_ref, dl_ref, c_ref, ka_ref, kb_ref, v_ref, *rest):
        x_in, (dq_ref, dk_ref, dv_ref, dc_ref), x_out, sems = rest[:nx], rest[nx:nx + 4], rest[nx + 4:2 * nx + 4], rest[2 * nx + 4:]
        p = pl.program_id(0)
        i = pl.program_id(1)

        @pl.when((p == 0) & (i == 0))
        def _():
            for cp in _xchg_copies(x_in, x_out, [True] * nx, sems):
                cp.start()

        @pl.when(i == 0)
        def _():
            dk_ref[...] = jnp.zeros_like(dk_ref)
            dv_ref[...] = jnp.zeros_like(dv_ref)
            dc_ref[...] = jnp.zeros_like(dc_ref)

        lo = _lane_lo()
        q = q_ref[...]
        do = do_ref[...]
        zero = jnp.zeros_like(q)
        qh = _fox_q_operands(q)
        doh = (jnp.where(lo, do, zero), jnp.where(lo, zero, do))
        msh = _split_halves(m_ref[...])
        dlt = _split_halves(dl_ref[...])
        cblk = c_ref[...]
        shift = tuple(_pick_col(cblk, 2 * p + h) - msh[h] for h in (0, 1))

        def block(r0, nr, k0, nk, carry, masked):
            kj = (ka_ref[pl.ds(k0, nk), :], kb_ref[pl.ds(k0, nk), :])
            vj = v_ref[pl.ds(k0, nk), :]
            mask = _fox_mask(i * T + r0, nr, k0, nk) if masked else None
            qr, dor = q[r0:r0 + nr], do[r0:r0 + nr]
            dqs, dks, dvs = [], [], []
            for h in (0, 1):
                t = lax.dot_general(qh[h][r0:r0 + nr], kj[h], (((1,), (1,)), ((), ())), preferred_element_type=F32)
                if masked:
                    t = jnp.where(mask, t, NEG)
                pb = jnp.exp2(t + shift[h][r0:r0 + nr]).astype(BF16)
                dp = lax.dot_general(doh[h][r0:r0 + nr], vj, (((1,), (1,)), ((), ())), preferred_element_type=F32)
                ds = pb.astype(F32) * (dp - dlt[h][r0:r0 + nr])
                dsb = ds.astype(BF16)
                dqs.append(carry[h] + jnp.dot(dsb, kj[h], preferred_element_type=F32))
                dks.append(lax.dot_general(dsb, qr, (((0,), (0,)), ((), ())), preferred_element_type=F32))
                dvs.append(lax.dot_general(pb, dor, (((0,), (0,)), ((), ())), preferred_element_type=F32))
                dc_ref[0, h:h + 1, pl.ds(k0, nk)] += -jnp.sum(ds, axis=0, keepdims=True)
            dk_ref[pl.ds(k0, nk), :] += jnp.where(lo, dks[0], dks[1])
            dv_ref[pl.ds(k0, nk), :] += jnp.where(lo, dvs[0], dvs[1])
            return tuple(dqs)

        init = (jnp.zeros((T, LANES), F32), jnp.zeros((T, LANES), F32))
        carry = lax.fori_loop(0, i, lambda j, cr: block(0, T, pl.multiple_of(j * T, LANES), T, cr, False), init)
        kd = pl.multiple_of(i * T, LANES)
        ta = _diag_split(T)
        da = block(0, T, kd, ta, carry, True)
        db = block(ta, T - ta, pl.multiple_of(kd + ta, LANES), T - ta, tuple(d[ta:] for d in da), True)
        dq_ref[...] = jnp.where(lo, _rows_joined(da[0], db[0], ta), _rows_joined(da[1], db[1], ta))

        @pl.when((p == P - 1) & (i == nt - 1))
        def _():
            for cp in _xchg_copies(x_in, x_out, [True] * nx, sems):
                cp.wait()

    tile = pl.BlockSpec((T, LANES), lambda p, i: (i, p))
    full = pl.BlockSpec((Lp, LANES), lambda p, i: (0, p))
    HBM = pl.BlockSpec(memory_space=pltpu.HBM)
    return pl.pallas_call(
        body, name=name, grid=(P, nt),
        in_specs=[tile, tile, tile, tile, pl.BlockSpec((T, LANES), lambda p, i: (i, 0)), full, full, full]
        + [HBM] * nx,
        out_specs=[tile, full, full, pl.BlockSpec((1, 8, Lp), lambda p, i: (p, 0, 0))] + [HBM] * nx,
        out_shape=[jax.ShapeDtypeStruct((Lp, D), F32)] * 3 + [jax.ShapeDtypeStruct((P, 8, Lp), F32)]
        + _xchg_out_shapes(xchg, [True] * nx),
        scratch_shapes=_xchg_sems(nx),
        compiler_params=_params(("arbitrary", "arbitrary")),
    )(qn, do, mshift, delta, c, ka, kb, vb, *xchg)


def _scan_rows(x, reverse):
    C = x.shape[0]
    row = lax.broadcasted_iota(jnp.int32, (C, 1), 0)
    step = 1
    while step < C:
        if reverse:
            x = x + jnp.where(row < C - step, pltpu.roll(x, C - step, 0), 0.0)
        else:
            x = x + jnp.where(row >= step, pltpu.roll(x, step, 0), 0.0)
        step *= 2
    return x


@jax.custom_vjp
def _cumsum_rows(x):
    return _scan_rows(x, False)


_cumsum_rows.defvjp(lambda x: (_scan_rows(x, False), None), lambda _, g: (_scan_rows(g, True),))


def _hgrn_chunk(St, qr, z, vi, go, p0, p1, gg):
    C = qr.shape[0]
    lb = jax.nn.sigmoid(p1 - p0)
    a = jnp.log(lb)
    cc = jnp.log1p(-lb) + _log_sigmoid(z)
    log_f = jnp.maximum(a, cc) + jnp.log1p(jnp.exp(-jnp.abs(a - cc)))
    k = (1.0 - lb) * jax.nn.sigmoid(-z)
    q = qr * jax.nn.sigmoid(qr)
    row = lax.broadcasted_iota(jnp.int32, (C, C), 0)
    col = lax.broadcasted_iota(jnp.int32, (C, C), 1)
    causal = col <= row
    b = _cumsum_rows(log_f)
    mid = lax.broadcasted_iota(jnp.int32, (C, 1), 0) == C // 2 - 1
    r = jnp.sum(jnp.where(mid, b, 0.0), axis=0, keepdims=True)
    b_last = jnp.sum(log_f, axis=0, keepdims=True)
    attn = jnp.where(causal, _d_nt(q * jnp.exp(b - r), k * jnp.exp(r - b)), 0.0)
    o = _d_nn(attn, vi) + _d_nt(q * jnp.exp(b), St)
    St_new = St * jnp.exp(b_last) + _d_tn(vi, k * jnp.exp(b_last - b))
    og = _rms(o, gg) * (go * jax.nn.sigmoid(go))
    return St_new, og


def _hgrn_heads_per_step(H):
    return 8 if H % 8 == 0 else 4 if H % 4 == 0 else 1


def _hgrn_specs(T, W, nhb, rev_nt=None):
    if rev_nt is None:
        return [pl.BlockSpec((T, W), functools.partial(lambda hb, t, g: (t, g * nhb + hb), g=g)) for g in range(4)]
    return [pl.BlockSpec((T, W), functools.partial(lambda hb, t, g: (rev_nt - 1 - t, g * nhb + hb), g=g))
            for g in range(4)]


def _hgrn_fwd(proj, lbp, gg, T, name):
    Lp = proj.shape[0]
    D = proj.shape[1] // 4
    H = D // LANES
    hps = _hgrn_heads_per_step(H)
    W = hps * LANES
    nhb = H // hps
    nt = Lp // T
    ncc = T // HGRN_CHUNK

    def body(q_ref, z_ref, i_ref, go_ref, p_ref, gg_ref, og_ref, ss_ref, st_ref):
        @pl.when(pl.program_id(1) == 0)
        def _():
            st_ref[...] = jnp.zeros_like(st_ref)

        gain = gg_ref[...]

        def chunk(cidx, states):
            sl = pl.ds(pl.multiple_of(cidx * HGRN_CHUNK, HGRN_CHUNK), HGRN_CHUNK)
            new = []
            for hh in range(hps):
                ln = slice(hh * LANES, (hh + 1) * LANES)
                ss_ref[hh, cidx] = states[hh]
                St_new, og = _hgrn_chunk(states[hh], q_ref[sl, ln], z_ref[sl, ln], i_ref[sl, ln], go_ref[sl, ln],
                                         p_ref[0:1, ln], p_ref[1:2, ln], gain)
                og_ref[sl, ln] = og.astype(BF16)
                new.append(St_new)
            return tuple(new)

        assert ncc % 2 == 0
        states = lax.fori_loop(0, ncc // 2, lambda c2, st: chunk(2 * c2 + 1, chunk(2 * c2, st)),
                               tuple(st_ref[hh] for hh in range(hps)))
        for hh in range(hps):
            st_ref[hh] = states[hh]

    return pl.pallas_call(
        body, name=name, grid=(nhb, nt),
        in_specs=_hgrn_specs(T, W, nhb) + [pl.BlockSpec((2, W), lambda hb, t: (0, hb)),
                                           pl.BlockSpec((1, LANES), lambda hb, t: (0, 0))],
        out_specs=[pl.BlockSpec((T, W), lambda hb, t: (t, hb)),
                   pl.BlockSpec((hps, ncc, LANES, LANES), lambda hb, t: (hb, t, 0, 0))],
        out_shape=[jax.ShapeDtypeStruct((Lp, D), BF16),
                   jax.ShapeDtypeStruct((H, Lp // HGRN_CHUNK, LANES, LANES), F32)],
        scratch_shapes=[pltpu.VMEM((hps, LANES, LANES), F32)],
        compiler_params=_params(("parallel", "arbitrary")),
    )(proj, proj, proj, proj, lbp, gg)


def _hgrn_bwd(proj, lbp, gg, dog, ss, T, name):
    Lp = proj.shape[0]
    D = proj.shape[1] // 4
    H = D // LANES
    hps = _hgrn_heads_per_step(H)
    W = hps * LANES
    nhb = H // hps
    assert nhb == 1, "d proj is written as whole rows: every head in one grid step"
    nt = Lp // T
    ncc = T // HGRN_CHUNK

    def body(q_ref, z_ref, i_ref, go_ref, p_ref, gg_ref, dog_ref, ss_ref, dproj_ref, dp_ref, dgg_ref, dst_ref):
        hb = pl.program_id(0)
        t = pl.program_id(1)

        @pl.when(t == 0)
        def _():
            dst_ref[...] = jnp.zeros_like(dst_ref)
            dp_ref[...] = jnp.zeros_like(dp_ref)

        @pl.when((t == 0) & (hb == 0))
        def _():
            dgg_ref[...] = jnp.zeros_like(dgg_ref)

        gain = gg_ref[...]
        row0 = (nt - 1 - t) * T

        def chunk(cc, carry):
            dstates, dps, dgain_sum = carry
            cidx = ncc - 1 - cc
            r0 = pl.multiple_of(cidx * HGRN_CHUNK, HGRN_CHUNK)
            sl = pl.ds(r0, HGRN_CHUNK)
            real = (row0 + r0 + lax.broadcasted_iota(jnp.int32, (HGRN_CHUNK, 1), 0)) >= N_PAD
            new_d, new_p = [], []
            for hh in range(hps):
                ln = slice(hh * LANES, (hh + 1) * LANES)
                _, vjp = jax.vjp(_hgrn_chunk, ss_ref[hh, cidx], q_ref[sl, ln], z_ref[sl, ln], i_ref[sl, ln],
                                 go_ref[sl, ln], p_ref[0:1, ln], p_ref[1:2, ln], gain)
                dSt, dq, dz, di, dgo, dp0, dp1, dgain = vjp((dstates[hh], dog_ref[sl, ln]))
                for grp, dval in enumerate((dq, dz, di, dgo)):
                    dproj_ref[sl, grp * D + hh * LANES:grp * D + (hh + 1) * LANES] = (
                        jnp.where(real, dval, 0.0).astype(BF16))
                new_d.append(dSt)
                new_p.append((dps[hh][0] + dp0, dps[hh][1] + dp1))
                dgain_sum = dgain_sum + dgain
            return tuple(new_d), tuple(new_p), dgain_sum

        zero_row = jnp.zeros((1, LANES), F32)
        init = (tuple(dst_ref[hh] for hh in range(hps)), tuple((zero_row, zero_row) for _ in range(hps)), zero_row)
        dstates, dps, dgain_sum = lax.fori_loop(0, ncc, chunk, init)
        for hh in range(hps):
            ln = slice(hh * LANES, (hh + 1) * LANES)
            dst_ref[hh] = dstates[hh]
            dp_ref[0:1, ln] += dps[hh][0]
            dp_ref[1:2, ln] += dps[hh][1]
        dgg_ref[0:1, :] += dgain_sum

    rev = pl.BlockSpec((T, W), lambda hb, t: (nt - 1 - t, hb))
    return pl.pallas_call(
        body, name=name, grid=(nhb, nt),
        in_specs=_hgrn_specs(T, W, nhb, nt) + [pl.BlockSpec((2, W), lambda hb, t: (0, hb)),
                                               pl.BlockSpec((1, LANES), lambda hb, t: (0, 0)), rev,
                                               pl.BlockSpec((hps, ncc, LANES, LANES),
                                                            lambda hb, t: (hb, nt - 1 - t, 0, 0))],
        out_specs=[pl.BlockSpec((T, 4 * D), lambda hb, t: (nt - 1 - t, 0)), pl.BlockSpec((8, W), lambda hb, t: (0, hb)),
                   pl.BlockSpec((8, LANES), lambda hb, t: (0, 0))],
        out_shape=[jax.ShapeDtypeStruct((Lp, 4 * D), BF16), jax.ShapeDtypeStruct((8, D), F32),
                   jax.ShapeDtypeStruct((8, LANES), F32)],
        scratch_shapes=[pltpu.VMEM((hps, LANES, LANES), F32)],
        compiler_params=_params(("arbitrary", "arbitrary")),
    )(proj, proj, proj, proj, lbp, gg, dog, ss)


def _xchg_sems(n_arr):
    return [pltpu.SemaphoreType.DMA((n_arr * (N_DEV - 1),)), pltpu.SemaphoreType.DMA((n_arr * (N_DEV - 1),)),
            pltpu.SemaphoreType.DMA((n_arr,))]


def _xchg_copies(ins, outs, per_peer, sems):
    send_sems, recv_sems, local_sems = sems
    x, y, c = lax.axis_index("x"), lax.axis_index("y"), lax.axis_index("c")
    me = 4 * x + 2 * y + c
    copies = []
    for n in range(len(ins)):
        src = ins[n].at[me] if per_peer[n] else ins[n]
        copies.append(pltpu.make_async_copy(src, outs[n].at[me], local_sems.at[n]))
    for rel in range(1, N_DEV):
        fx, fy, fc = (rel >> 2) & 1, (rel >> 1) & 1, rel & 1
        px = 1 - x if fx else x
        py = 1 - y if fy else y
        pc = 1 - c if fc else c
        peer = 4 * px + 2 * py + pc
        for n in range(len(ins)):
            src = ins[n].at[peer] if per_peer[n] else ins[n]
            copies.append(pltpu.make_async_remote_copy(
                src_ref=src, dst_ref=outs[n].at[me],
                send_sem=send_sems.at[n * (N_DEV - 1) + rel - 1],
                recv_sem=recv_sems.at[n * (N_DEV - 1) + rel - 1],
                device_id=(px, py, pc), device_id_type=pl.DeviceIdType.MESH))
    return copies


def _xchg_out_shapes(arrays, per_peer):
    return [jax.ShapeDtypeStruct(a.shape if pp else (N_DEV,) + a.shape, a.dtype) for a, pp in zip(arrays, per_peer)]


def _exchange(arrays, per_peer, name):
    n_arr = len(arrays)
    HBM = pl.BlockSpec(memory_space=pltpu.HBM)

    def body(*refs):
        copies = _xchg_copies(refs[:n_arr], refs[n_arr:2 * n_arr], per_peer, refs[2 * n_arr:])
        for cp in copies:
            cp.start()
        for cp in copies:
            cp.wait()

    return pl.pallas_call(
        body, name=name,
        in_specs=[HBM] * n_arr, out_specs=[HBM] * n_arr, out_shape=_xchg_out_shapes(arrays, per_peer),
        scratch_shapes=_xchg_sems(n_arr),
    )(*arrays)


def _gather2_copy(ins, outs, sems, a, k, block, to, own=False):
    send_sems, recv_sems, _ = sems
    px, py, pc = block
    dst = outs[a].at[4 * px + 2 * py + pc]
    return pltpu.make_async_remote_copy(
        src_ref=ins[a] if own else dst, dst_ref=dst,
        send_sem=send_sems.at[a * (N_DEV - 1) + k], recv_sem=recv_sems.at[a * (N_DEV - 1) + k],
        device_id=to, device_id_type=pl.DeviceIdType.MESH)


def _gather2_first(ins, outs, sems):
    x, y, c = lax.axis_index("x"), lax.axis_index("y"), lax.axis_index("c")
    me, chips = (x, y, c), [(1 - x, y), (x, 1 - y), (1 - x, 1 - y)]
    n_arr = len(ins)
    local = [pltpu.make_async_copy(ins[a], outs[a].at[4 * x + 2 * y + c], sems[2].at[a]) for a in range(n_arr)]
    first = [_gather2_copy(ins, outs, sems, a, 0, me, (x, y, 1 - c), own=True) for a in range(n_arr)]
    first += [_gather2_copy(ins, outs, sems, a, 1 + j, me, (*chip, c), own=True)
              for j, chip in enumerate(chips) for a in range(n_arr)]
    return local, first


def _gather2_finish(ins, outs, sems):
    x, y, c = lax.axis_index("x"), lax.axis_index("y"), lax.axis_index("c")
    me, sibling, chips = (x, y, c), (x, y, 1 - c), [(1 - x, y), (x, 1 - y), (1 - x, 1 - y)]
    n_arr = len(ins)
    local, first = _gather2_first(ins, outs, sems)
    passed = []
    for j, chip in enumerate(chips):
        for a in range(n_arr):
            _gather2_copy(ins, outs, sems, a, 1 + j, (*chip, c), me).wait_recv()
            cp = _gather2_copy(ins, outs, sems, a, 4 + j, (*chip, c), sibling)
            cp.start()
            passed.append(cp)
    for a in range(n_arr):
        _gather2_copy(ins, outs, sems, a, 0, sibling, me).wait_recv()
        for j, chip in enumerate(chips):
            _gather2_copy(ins, outs, sems, a, 4 + j, (*chip, 1 - c), me).wait_recv()
    for cp in first + passed:
        cp.wait_send()
    for cp in local:
        cp.wait()


def _gather_two_level(arrays, name):
    n_arr = len(arrays)
    HBM = pl.BlockSpec(memory_space=pltpu.HBM)

    def body(*refs):
        ins, outs, sems = refs[:n_arr], refs[n_arr:2 * n_arr], refs[2 * n_arr:]
        local, first = _gather2_first(ins, outs, sems)
        for cp in local + first:
            cp.start()
        _gather2_finish(ins, outs, sems)

    return pl.pallas_call(
        body, name=name,
        in_specs=[HBM] * n_arr, out_specs=[HBM] * n_arr, out_shape=_xchg_out_shapes(arrays, [False] * n_arr),
        scratch_shapes=_xchg_sems(n_arr),
    )(*arrays)


ADAMW_VMEM_BUDGET = 36 * 1024 * 1024


def _adamw(recv, w, m, v, name):
    shape = w.shape
    C = shape[-1]
    R = math.prod(shape[:-1])
    lanes = -(-C // LANES) * LANES
    row_bytes = 2 * lanes * (N_DEV * recv.dtype.itemsize + 7 * 4)
    rc = _row_chunk(R, max(16, ADAMW_VMEM_BUDGET // row_bytes), 16 if recv.dtype == BF16 else 8)

    def body(r_ref, w_ref, m_ref, v_ref, g_ref, d_ref, mo_ref, vo_ref):
        g = r_ref[0].astype(F32)
        for s in range(1, N_DEV):
            g = g + r_ref[s].astype(F32)
        mn = ADAM_B1 * m_ref[...] + (1.0 - ADAM_B1) * g
        vn = ADAM_B2 * v_ref[...] + (1.0 - ADAM_B2) * (g * g)
        m_hat = mn / (1.0 - ADAM_B1 ** ADAM_STEP)
        v_hat = vn / (1.0 - ADAM_B2 ** ADAM_STEP)
        g_ref[...] = g
        d_ref[...] = -ADAM_LR * (m_hat / (jnp.sqrt(v_hat) + ADAM_EPS) + ADAM_WD * w_ref[...])
        mo_ref[...] = mn
        vo_ref[...] = vn

    row = pl.BlockSpec((rc, C), lambda i: (i, 0))
    outs = pl.pallas_call(
        body, name=name, grid=(R // rc,),
        in_specs=[pl.BlockSpec((N_DEV, rc, C), lambda i: (0, i, 0)), row, row, row],
        out_specs=[row] * 4,
        out_shape=[jax.ShapeDtypeStruct((R, C), F32)] * 4,
        compiler_params=_params(("parallel",)),
    )(recv.reshape(N_DEV, R, C), w.reshape(R, C), m.reshape(R, C), v.reshape(R, C))
    return [o.reshape(shape) for o in outs]


def _gathered_to_full(g, name):
    if name in COL_SHARDED:
        g = jnp.moveaxis(g, 0, -2)
        return g.reshape(g.shape[:-2] + (g.shape[-2] * g.shape[-1],))
    g = jnp.moveaxis(g, 0, -3)
    return g.reshape(g.shape[:-3] + (g.shape[-3] * g.shape[-2], g.shape[-1]))


def _full_to_slabs(full, name):
    if name in COL_SHARDED:
        f = full.reshape(full.shape[:-1] + (N_DEV, full.shape[-1] // N_DEV))
        return jnp.moveaxis(f, -2, 0)
    f = full.reshape(full.shape[:-2] + (N_DEV, full.shape[-2] // N_DEV, full.shape[-1]))
    return jnp.moveaxis(f, -3, 0)


def _pack_small(arrs):
    rows = []
    for a in arrs:
        flat = a.astype(F32).reshape(-1)
        pad = (-flat.shape[0]) % LANES
        rows.append(jnp.pad(flat, (0, pad)).reshape(-1, LANES))
    p = jnp.concatenate(rows, axis=0)
    return jnp.pad(p, ((0, (-p.shape[0]) % 8), (0, 0)))


def _unpack_small(packed, shapes):
    out, off = [], 0
    for shp in shapes:
        n = math.prod(shp)
        nr = -(-n // LANES)
        out.append(packed[off:off + nr].reshape(-1)[:n].reshape(shp))
        off += nr
    return out


def _local_step(x, target, meta, fox_in_shard, late, small):
    S, D = x.shape
    Lp = S + HEAD_ROWS
    T = ROW_TILE if Lp % ROW_TILE == 0 else HEAD_ROWS
    P = D // LANES
    row = lambda v: v.reshape(1, -1).astype(F32)

    n_heads = fox_in_shard.shape[-1] * N_DEV - 4 * D
    bf = jnp.pad(row(small["fox_b_f"]), ((0, 0), (0, LANES - small["fox_b_f"].size)))
    qg = jnp.tile(row(small["fox_q_norm"]), (1, 2))
    kg = jnp.tile(row(small["fox_k_norm"]), (1, 2))


    h0, hn0, hn0t, g_fin = _embed_rms_fwd(x, meta, row(small["attn_norm"][0]), [fox_in_shard], "rms0_fwd")
    w_fox_in = _gathered_to_full(g_fin, "fox_w_in")
    w_fin = jnp.pad(w_fox_in[0], ((0, 0), (0, LANES - w_fox_in.shape[-1] % LANES)))
    proj0 = _mm(hn0, w_fin, "nn", F32, "fox_in_fwd")
    qn, ka, kb, vb, c = _fox_prep_fwd(proj0, bf, qg, kg, T, D, "fox_prep_fwd")
    o, og0, mshift, linv, *gathered = _fox_attn_fwd(qn, ka, kb, vb, c, proj0, [late[n] for n in LATE], T, D,
                                                    "fox_attn_fwd")
    wl = {n: _gathered_to_full(g, n) for n, g in zip(LATE, gathered)}
    w_fout, w_hin, w_hout = wl["fox_w_out"][0], wl["hgrn_w_in"][0], wl["hgrn_w_out"][0]
    w_uin, w_uout = wl["ffn_w_in"], wl["ffn_w_out"]
    h1, hf0, hf0t = _out_proj_fwd(og0, w_fout, h0, row(small["ffn_norm"][0]), "fox_out_fwd")
    gu0 = _ffn_in_fwd(hf0, w_uin[0], "ffn0_in_fwd")
    act0 = gu0[2]
    h2, hn1, hn1t = _out_proj_fwd(act0, w_uout[0], h1, row(small["attn_norm"][1]), "ffn0_out_fwd")
    proj1 = _mm(hn1, w_hin, "nn", F32, "hgrn_in_fwd")
    lbp = small["hgrn_lower_bounds"].astype(F32)
    ggn = row(small["hgrn_g_norm"])
    Th = HGRN_TILE if Lp % HGRN_TILE == 0 else HEAD_ROWS
    og1, ss = _hgrn_fwd(proj1, lbp, ggn, Th, "hgrn_fwd")
    h3, hf1, hf1t = _out_proj_fwd(og1, w_hout, h2, row(small["ffn_norm"][1]), "hgrn_out_fwd")
    gu1 = _ffn_in_fwd(hf1, w_uin[1], "ffn1_in_fwd")
    act1 = gu1[2]
    h4 = _mm(act1, w_uout[1], "nn", F32, "ffn1_out_fwd", res=h3)
    loss_blk, dh4, dh4b, d_final = _final_loss(h4, row(small["final_norm"]), target, "final_loss")

    grads = {}

    def ffn_bwd(i, dh, dhb, h_in, hft, gu):
        grads_out = _mm(gu[3], dhb, "nn", F32, f"ffn{i}_out_dw", tm=_tile(gu[3].shape[0], 1408), tk=_tile(Lp, 1408))
        dgu = _ffn_out_dx(dhb, w_uout[i], gu[0], gu[1], f"ffn{i}_out_dx")
        grads_in = _mm(hft, dgu, "nn", F32, f"ffn{i}_in_dw", tm=D, tk=_tile(Lp, 1408))
        dh_new, dh_newb, dgain = _in_proj_dx(dgu, w_uin[i], h_in, row(small["ffn_norm"][i]), dh, f"ffn{i}_in_dx")
        return dh_new, dh_newb, grads_in, grads_out, dgain

    dh3, dh3b, g_uin1, g_uout1, d_fn1 = ffn_bwd(1, dh4, dh4b, h3, hf1t, gu1)
    grads["hgrn_w_out"] = _mm(og1, dh3b, "tn", F32, "hgrn_out_dw")[None]
    dog1 = _mm(dh3b, w_hout, "nt", F32, "hgrn_out_dx")
    dproj1, d_lb, d_gg = _hgrn_bwd(proj1, lbp, ggn, dog1, ss, Th, "hgrn_bwd")
    grads["hgrn_w_in"] = _mm(hn1t, dproj1, "nn", F32, "hgrn_in_dw", tm=D, tk=_tile(Lp, 1408))[None]
    dh2, dh2b, d_an1 = _in_proj_dx(dproj1, w_hin, h2, row(small["attn_norm"][1]), dh3, "hgrn_in_dx")
    dh1, dh1b, g_uin0, g_uout0, d_fn0 = ffn_bwd(0, dh2, dh2b, h1, hf0t, gu0)
    grads["ffn_w_in"] = jnp.stack([g_uin0, g_uin1])
    grads["ffn_w_out"] = jnp.stack([g_uout0, g_uout1])
    grads["fox_w_out"] = _mm(og0, dh1b, "tn", F32, "fox_out_dw")[None]
    do, dgate, delta = _fox_out_dx(dh1b, w_fout, o, proj0, linv, D, "fox_out_dx")
    slabs = [_full_to_slabs(grads[n], n).astype(BF16) for n in LATE]
    dqn, dkn, dv, dcr, *recv = _fox_attn_bwd(qn, ka, kb, vb, c, do, mshift, delta, slabs, T, D, "fox_attn_bwd")
    for n in LATE:
        del grads[n]
    dc = jnp.pad(dcr[:, :2, :].reshape(2 * P, Lp).T, ((0, 0), (0, LANES - 2 * P)))
    Tp = T // 2 if T == ROW_TILE else T
    dproj0, sm = _fox_prep_bwd(proj0, bf, qg, kg, dqn, dkn, dv, dgate, dc, Tp, D, "fox_prep_bwd")
    g_fin = _mm(hn0t, dproj0, "nn", F32, "fox_in_dw", tm=D, tk=_tile(Lp, 1408))[:, :4 * D + n_heads][None]
    dh_head, grad_x, d_an0, r_fin = _in_proj_dx(dproj0, w_fin, h0, row(small["attn_norm"][0]), dh1, "fox_in_dx",
                                                xchg=[_full_to_slabs(g_fin, "fox_w_in").astype(BF16)], split_head=True)

    grads["meta_tokens"] = dh_head[N_PAD:]
    grads["attn_norm"] = jnp.concatenate([d_an0, d_an1], axis=0)
    grads["ffn_norm"] = jnp.concatenate([d_fn0, d_fn1], axis=0)
    grads["final_norm"] = d_final[0]
    grads["fox_b_f"] = sm[0:1, :n_heads]
    grads["fox_q_norm"] = sm[1:2, :FOX_DH] + sm[1:2, FOX_DH:]
    grads["fox_k_norm"] = sm[2:3, :FOX_DH] + sm[2:3, FOX_DH:]
    grads["hgrn_lower_bounds"] = d_lb[0:2]
    grads["hgrn_g_norm"] = d_gg[0:1]
    return loss_blk[0, 0], grad_x, grads, dict(zip(LATE, recv), fox_w_in=r_fin)


def kernel(x, meta_tokens, attn_norm, ffn_norm, final_norm, fox_w_in, fox_b_f, fox_q_norm, fox_k_norm, fox_w_out, hgrn_w_in, hgrn_lower_bounds, hgrn_g_norm, hgrn_w_out, ffn_w_in, ffn_w_out, loss_target, m_meta_tokens, m_attn_norm, m_ffn_norm, m_final_norm, m_fox_w_in, m_fox_b_f, m_fox_q_norm, m_fox_k_norm, m_fox_w_out, m_hgrn_w_in, m_hgrn_lower_bounds, m_hgrn_g_norm, m_hgrn_w_out, m_ffn_w_in, m_ffn_w_out, v_meta_tokens, v_attn_norm, v_ffn_norm, v_final_norm, v_fox_w_in, v_fox_b_f, v_fox_q_norm, v_fox_k_norm, v_fox_w_out, v_hgrn_w_in, v_hgrn_lower_bounds, v_hgrn_g_norm, v_hgrn_w_out, v_ffn_w_in, v_ffn_w_out):
    w = dict(meta_tokens=meta_tokens, attn_norm=attn_norm, ffn_norm=ffn_norm, final_norm=final_norm,
             fox_w_in=fox_w_in, fox_b_f=fox_b_f, fox_q_norm=fox_q_norm, fox_k_norm=fox_k_norm,
             fox_w_out=fox_w_out, hgrn_w_in=hgrn_w_in, hgrn_lower_bounds=hgrn_lower_bounds,
             hgrn_g_norm=hgrn_g_norm, hgrn_w_out=hgrn_w_out, ffn_w_in=ffn_w_in, ffn_w_out=ffn_w_out)
    m = dict(meta_tokens=m_meta_tokens, attn_norm=m_attn_norm, ffn_norm=m_ffn_norm, final_norm=m_final_norm,
             fox_w_in=m_fox_w_in, fox_b_f=m_fox_b_f, fox_q_norm=m_fox_q_norm, fox_k_norm=m_fox_k_norm,
             fox_w_out=m_fox_w_out, hgrn_w_in=m_hgrn_w_in, hgrn_lower_bounds=m_hgrn_lower_bounds,
             hgrn_g_norm=m_hgrn_g_norm, hgrn_w_out=m_hgrn_w_out, ffn_w_in=m_ffn_w_in, ffn_w_out=m_ffn_w_out)
    v = dict(meta_tokens=v_meta_tokens, attn_norm=v_attn_norm, ffn_norm=v_ffn_norm, final_norm=v_final_norm,
             fox_w_in=v_fox_w_in, fox_b_f=v_fox_b_f, fox_q_norm=v_fox_q_norm, fox_k_norm=v_fox_k_norm,
             fox_w_out=v_fox_w_out, hgrn_w_in=v_hgrn_w_in, hgrn_lower_bounds=v_hgrn_lower_bounds,
             hgrn_g_norm=v_hgrn_g_norm, hgrn_w_out=v_hgrn_w_out, ffn_w_in=v_ffn_w_in, ffn_w_out=v_ffn_w_out)
    axes = ("x", "y", "c")
    small_shapes = [w[n].shape for n in SMALL]

    (g_meta,) = _gather_two_level([w["meta_tokens"].astype(F32)], "gather_meta")
    loss_local, grad_x, grads, recv = _local_step(
        x[0], loss_target[0], _gathered_to_full(g_meta, "meta_tokens"), w["fox_w_in"].astype(BF16),
        {n: w[n].astype(BF16) for n in LATE}, {n: w[n] for n in SMALL})
    loss = lax.psum(loss_local, axes)

    r_meta, r_small = _exchange([_full_to_slabs(grads["meta_tokens"], "meta_tokens"),
                                 _pack_small([grads[n] for n in SMALL])], [True, False], "scatter_grads")
    recv.update(meta_tokens=r_meta)

    res = {n: _adamw(recv[n], w[n], m[n], v[n], "adamw_" + n) for n in BIG}
    sml = _adamw(r_small, _pack_small([w[n] for n in SMALL]), _pack_small([m[n] for n in SMALL]),
                 _pack_small([v[n] for n in SMALL]), "adamw_small")
    outs = []
    for k in range(4):
        d = {n: res[n][k] for n in BIG}
        d.update(zip(SMALL, _unpack_small(sml[k], small_shapes)))
        outs.extend(d[n] for n in WEIGHTS)
    return (loss, grad_x[None], *outs)
```

```python
import functools
import math

import jax
import jax.numpy as jnp
from jax import lax
from jax.experimental import pallas as pl
from jax.experimental.pallas import tpu as pltpu

F32 = jnp.float32
BF16 = jnp.bfloat16
EPS = 1e-6
N_META = 16
LANES = 128
HEAD_ROWS = 256
ROW_TILE = 768
N_PAD = HEAD_ROWS - N_META
FOX_DH = 64
HGRN_CHUNK = 64
HGRN_TILE = 384
N_DEV = 8
NEG = -1e30
PAD_SHIFT = 1e4
VMEM_LIMIT = 56 * 1024 * 1024
HI = lax.Precision.HIGHEST
LOG2E = 1.0 / math.log(2.0)
LN2 = math.log(2.0)

ADAM_LR = 0.001
ADAM_B1 = 0.9
ADAM_B2 = 0.999
ADAM_EPS = 1e-08
ADAM_WD = 0.01
ADAM_STEP = 10

BIG = ("meta_tokens", "fox_w_in", "fox_w_out", "hgrn_w_in", "hgrn_w_out", "ffn_w_in", "ffn_w_out")
SMALL = ("attn_norm", "ffn_norm", "final_norm", "fox_b_f", "fox_q_norm", "fox_k_norm",
         "hgrn_lower_bounds", "hgrn_g_norm")
WEIGHTS = ("meta_tokens", "attn_norm", "ffn_norm", "final_norm", "fox_w_in", "fox_b_f", "fox_q_norm",
           "fox_k_norm", "fox_w_out", "hgrn_w_in", "hgrn_lower_bounds", "hgrn_g_norm", "hgrn_w_out",
           "ffn_w_in", "ffn_w_out")
COL_SHARDED = ("meta_tokens", "fox_w_in", "hgrn_w_in", "ffn_w_in")
LATE = ("fox_w_out", "hgrn_w_in", "hgrn_w_out", "ffn_w_in", "ffn_w_out")


def _params(sem=None):
    return pltpu.CompilerParams(dimension_semantics=sem, vmem_limit_bytes=VMEM_LIMIT)


def _tile(n, cap):
    best = None
    for t in range(LANES, min(n, cap) + 1, LANES):
        if n % t == 0:
            best = t
    assert best is not None, (n, cap)
    return best


def _row_chunk(n, cap, mult=8):
    best = n
    for t in range(mult, min(n, cap) + 1, mult):
        if n % t == 0:
            best = t
    return best


def _dg(a, b, ca, cb):
    return lax.dot_general(a.astype(BF16), b.astype(BF16), (((ca,), (cb,)), ((), ())),
                           preferred_element_type=F32)


@jax.custom_vjp
def _d_nn(a, b):
    return _dg(a, b, 1, 0)


@jax.custom_vjp
def _d_nt(a, b):
    return _dg(a, b, 1, 1)


@jax.custom_vjp
def _d_tn(a, b):
    return _dg(a, b, 0, 0)


_d_nn.defvjp(lambda a, b: (_d_nn(a, b), (a, b)), lambda r, g: (_d_nt(g, r[1]), _d_tn(r[0], g)))
_d_nt.defvjp(lambda a, b: (_d_nt(a, b), (a, b)), lambda r, g: (_d_nn(g, r[1]), _d_tn(g, r[0])))
_d_tn.defvjp(lambda a, b: (_d_tn(a, b), (a, b)), lambda r, g: (_d_nt(r[1], g), _d_nn(r[0], g)))


def _log_sigmoid(x):
    return jnp.minimum(x, 0.0) - jnp.log1p(jnp.exp(-jnp.abs(x)))


def _rms(x, g):
    return x * lax.rsqrt(jnp.mean(x * x, axis=-1, keepdims=True) + EPS) * g


def _mm(a, b, mode, out_dtype, name, res=None, tm=None, tn=None, tk=None):
    assert a.dtype == BF16 and b.dtype == BF16, (name, a.dtype, b.dtype)
    if mode == "nn":
        (M, K), N = a.shape, b.shape[1]
    elif mode == "nt":
        (M, K), N = a.shape, b.shape[0]
    else:
        (K, M), N = a.shape, b.shape[1]
    if mode == "nn":
        tm, tn, tk = tm or _tile(M, ROW_TILE), tn or _tile(N, 1408), tk or _tile(K, 2816)
    elif mode == "nt":
        tm, tn, tk = tm or _tile(M, ROW_TILE if K <= 2048 else ROW_TILE // 2), tn or N, tk or K
    else:
        tm, tn, tk = tm or _tile(M, 1408), tn or _tile(N, 1408), tk or _tile(K, ROW_TILE)
    nk = K // tk
    if mode == "tn":
        a_spec = pl.BlockSpec((tk, tm), lambda j, i, k: (k, i))
        dims = (((0,), (0,)), ((), ()))
    else:
        a_spec = pl.BlockSpec((tm, tk), lambda j, i, k: (i, k))
        dims = (((1,), (1 if mode == "nt" else 0,)), ((), ()))
    if mode == "nt":
        b_spec = pl.BlockSpec((tn, tk), lambda j, i, k: (j, k))
    else:
        b_spec = pl.BlockSpec((tk, tn), lambda j, i, k: (k, j))

    o_spec = pl.BlockSpec((tm, tn), lambda j, i, k: (i, j))

    def body(a_ref, b_ref, *rest):
        r_ref = rest[0] if res is not None else None
        o_ref, acc_ref = rest[-2:]
        k = pl.program_id(2)

        @pl.when(k == 0)
        def _():
            acc_ref[...] = jnp.zeros_like(acc_ref)

        acc_ref[...] += lax.dot_general(a_ref[...], b_ref[...], dims, preferred_element_type=F32)

        @pl.when(k == nk - 1)
        def _():
            out = acc_ref[...] if r_ref is None else acc_ref[...] + r_ref[...]
            o_ref[...] = out.astype(out_dtype)

    return pl.pallas_call(
        body, name=name, grid=(N // tn, M // tm, nk),
        in_specs=[a_spec, b_spec] + ([o_spec] if res is not None else []),
        out_specs=o_spec,
        out_shape=jax.ShapeDtypeStruct((M, N), out_dtype),
        scratch_shapes=[pltpu.VMEM((tm, tn), F32)],
        compiler_params=_params(("parallel", "parallel", "arbitrary")),
    )(a, b, *([res] if res is not None else []))


def _out_proj_fwd(a, w, res, gain, name):
    Lp, K = a.shape
    D = w.shape[1]
    tm = _tile(Lp, ROW_TILE)

    def body(a_ref, w_ref, r_ref, g_ref, h_ref, hn_ref, hnt_ref):
        h = jnp.dot(a_ref[...], w_ref[...], preferred_element_type=F32) + r_ref[...]
        h_ref[...] = h
        hn = _rms(h, g_ref[...])
        hn_ref[...] = hn.astype(BF16)
        hnt_ref[...] = hn.T.astype(BF16)

    row = pl.BlockSpec((tm, D), lambda i: (i, 0))
    return pl.pallas_call(
        body, name=name, grid=(Lp // tm,),
        in_specs=[pl.BlockSpec((tm, K), lambda i: (i, 0)), pl.BlockSpec((K, D), lambda i: (0, 0)), row,
                  pl.BlockSpec((1, D), lambda i: (0, 0))],
        out_specs=[row, row, pl.BlockSpec((D, tm), lambda i: (0, i))],
        out_shape=[jax.ShapeDtypeStruct((Lp, D), F32), jax.ShapeDtypeStruct((Lp, D), BF16),
                   jax.ShapeDtypeStruct((D, Lp), BF16)],
        compiler_params=_params(("parallel",)),
    )(a, w, res, gain)


def _in_proj_dx(dy, w, x, gain, dres, name, xchg=(), split_head=False):
    Lp, N = dy.shape
    D = w.shape[0]
    tm = HEAD_ROWS if split_head else _tile(Lp, ROW_TILE // 2)
    nt = Lp // tm
    nx = len(xchg)

    def body(dy_ref, w_ref, x_ref, g_ref, dr_ref, *rest):
        x_in, (dx_ref, dxb_ref, dg_ref), x_out, sems = rest[:nx], rest[nx:nx + 3], rest[nx + 3:2 * nx + 3], rest[2 * nx + 3:]

        @pl.when(pl.program_id(0) == 0)
        def _():
            dg_ref[...] = jnp.zeros_like(dg_ref)
            if nx:
                for cp in _xchg_copies(x_in, x_out, [True] * nx, sems):
                    cp.start()

        dhn = lax.dot_general(dy_ref[...], w_ref[...], (((1,), (1,)), ((), ())), preferred_element_type=F32)
        _, vjp = jax.vjp(_rms, x_ref[...], g_ref[...])
        dx, dg = vjp(dhn)
        dx = dx + dr_ref[...]
        if split_head:
            @pl.when(pl.program_id(0) == 0)
            def _():
                dx_ref[...] = dx

            @pl.when(pl.program_id(0) > 0)
            def _():
                dxb_ref[...] = dx
        else:
            dx_ref[...] = dx
            dxb_ref[...] = dx.astype(BF16)
        dg_ref[...] += dg

        if nx:
            @pl.when(pl.program_id(0) == nt - 1)
            def _():
                for cp in _xchg_copies(x_in, x_out, [True] * nx, sems):
                    cp.wait()

    row = pl.BlockSpec((tm, D), lambda i: (i, 0))
    vec = pl.BlockSpec((1, D), lambda i: (0, 0))
    HBM = pl.BlockSpec(memory_space=pltpu.HBM)
    return pl.pallas_call(
        body, name=name, grid=(nt,),
        in_specs=[pl.BlockSpec((tm, N), lambda i: (i, 0)), pl.BlockSpec((D, N), lambda i: (0, 0)), row, vec, row]
        + [HBM] * nx,
        out_specs=([pl.BlockSpec((tm, D), lambda i: (0, 0)), pl.BlockSpec((tm, D), lambda i: (jnp.maximum(i - 1, 0), 0))]
                   if split_head else [row, row]) + [vec] + [HBM] * nx,
        out_shape=([jax.ShapeDtypeStruct((tm, D), F32), jax.ShapeDtypeStruct((Lp - tm, D), F32)] if split_head else
                   [jax.ShapeDtypeStruct((Lp, D), F32), jax.ShapeDtypeStruct((Lp, D), BF16)])
        + [jax.ShapeDtypeStruct((1, D), F32)] + _xchg_out_shapes(xchg, [True] * nx),
        scratch_shapes=_xchg_sems(nx) if nx else [],
        compiler_params=_params(("arbitrary",)),
    )(dy, w, x, gain, dres, *xchg)


def _embed_rms_fwd(x, meta, g, gather, name):
    S, D = x.shape
    TR = HEAD_ROWS
    Lp = S + TR
    ng = len(gather)

    def body(x_ref, m_ref, g_ref, *rest):
        g_in, (h_ref, o_ref, ot_ref), g_out, sems = rest[:ng], rest[ng:ng + 3], rest[ng + 3:2 * ng + 3], rest[2 * ng + 3:]
        i = pl.program_id(0)

        @pl.when(i == 0)
        def _():
            local, first = _gather2_first(g_in, g_out, sems)
            for cp in local + first:
                cp.start()

        @pl.when(i == 0)
        def _():
            h_ref[...] = jnp.zeros_like(h_ref)
            h_ref[N_PAD:, :] = m_ref[...]

        @pl.when(i > 0)
        def _():
            h_ref[...] = x_ref[...]

        y = _rms(h_ref[...], g_ref[...])
        o_ref[...] = y.astype(BF16)
        ot_ref[...] = y.T.astype(BF16)

        @pl.when(i == Lp // TR - 1)
        def _():
            _gather2_finish(g_in, g_out, sems)

    row = pl.BlockSpec((TR, D), lambda i: (i, 0))
    HBM = pl.BlockSpec(memory_space=pltpu.HBM)
    return pl.pallas_call(
        body, name=name, grid=(Lp // TR,),
        in_specs=[pl.BlockSpec((TR, D), lambda i: (jnp.maximum(i - 1, 0), 0)),
                  pl.BlockSpec((N_META, D), lambda i: (0, 0)), pl.BlockSpec((1, D), lambda i: (0, 0))] + [HBM] * ng,
        out_specs=[row, row, pl.BlockSpec((D, TR), lambda i: (0, i))] + [HBM] * ng,
        out_shape=[jax.ShapeDtypeStruct((Lp, D), F32), jax.ShapeDtypeStruct((Lp, D), BF16),
                   jax.ShapeDtypeStruct((D, Lp), BF16)] + _xchg_out_shapes(gather, [False] * ng),
        scratch_shapes=_xchg_sems(ng),
        compiler_params=_params(("arbitrary",)),
    )(x, meta, g, *gather)


def _swiglu(gate, up):
    return gate * jax.nn.sigmoid(gate) * up


def _ffn_in_fwd(hf, w_in, name):
    Lp, D = hf.shape
    F = w_in.shape[1] // 2
    tm = _tile(Lp, ROW_TILE)
    tn = _tile(F, 1408)
    nj = F // tn

    def body(a_ref, bg_ref, bu_ref, g_ref, u_ref, act_ref):
        a = a_ref[...]
        g = jnp.dot(a, bg_ref[...], preferred_element_type=F32)
        u = jnp.dot(a, bu_ref[...], preferred_element_type=F32)
        g_ref[...] = g.astype(BF16)
        u_ref[...] = u.astype(BF16)
        act_ref[...] = _swiglu(g, u).astype(BF16)

    tile = pl.BlockSpec((tm, tn), lambda j, i: (i, j))
    return pl.pallas_call(
        body, name=name, grid=(nj, Lp // tm),
        in_specs=[pl.BlockSpec((tm, D), lambda j, i: (i, 0)), pl.BlockSpec((D, tn), lambda j, i: (0, j)),
                  pl.BlockSpec((D, tn), lambda j, i: (0, nj + j))],
        out_specs=[tile, tile, tile],
        out_shape=[jax.ShapeDtypeStruct((Lp, F), BF16)] * 3,
        compiler_params=_params(("parallel", "parallel")),
    )(hf, w_in, w_in)


def _ffn_out_dx(dhb, w_out, g, u, name):
    Lp, D = dhb.shape
    F = w_out.shape[0]
    tm = HEAD_ROWS

    def body(a_ref, b_ref, g_ref, u_ref, o_ref):
        dact = lax.dot_general(a_ref[...], b_ref[...], (((1,), (1,)), ((), ())), preferred_element_type=F32)
        _, vjp = jax.vjp(_swiglu, g_ref[...].astype(F32), u_ref[...].astype(F32))
        dg, du = vjp(dact)
        o_ref[:, :F] = dg.astype(BF16)
        o_ref[:, F:] = du.astype(BF16)

    wide = pl.BlockSpec((tm, F), lambda i: (i, 0))
    return pl.pallas_call(
        body, name=name, grid=(Lp // tm,),
        in_specs=[pl.BlockSpec((tm, D), lambda i: (i, 0)), pl.BlockSpec((F, D), lambda i: (0, 0)), wide, wide],
        out_specs=pl.BlockSpec((tm, 2 * F), lambda i: (i, 0)),
        out_shape=jax.ShapeDtypeStruct((Lp, 2 * F), BF16),
        compiler_params=_params(("parallel",)),
    )(dhb, w_out, g, u)


def _final_loss(h, g, target, name):
    Lp, D = h.shape
    TR = HEAD_ROWS

    def loss_fn(hh, gg, tt):
        err = _rms(hh, gg) - tt
        return 0.5 * jnp.sum(jnp.mean(err * err, axis=-1))

    def body(h_ref, g_ref, t_ref, loss_ref, dh_ref, dhb_ref, dg_ref):
        i = pl.program_id(0)

        @pl.when(i == 0)
        def _():
            loss_ref[...] = jnp.zeros_like(loss_ref)
            dg_ref[...] = jnp.zeros_like(dg_ref)
            dh_ref[...] = jnp.zeros_like(dh_ref)
            dhb_ref[...] = jnp.zeros_like(dhb_ref)

        @pl.when(i > 0)
        def _():
            val, vjp = jax.vjp(lambda hh, gg: loss_fn(hh, gg, t_ref[...]), h_ref[...], g_ref[...])
            dh, dg = vjp(jnp.ones((), F32))
            dh_ref[...] = dh
            dhb_ref[...] = dh.astype(BF16)
            dg_ref[...] += dg
            loss_ref[...] += val

    row = pl.BlockSpec((TR, D), lambda i: (i, 0))
    return pl.pallas_call(
        body, name=name, grid=(Lp // TR,),
        in_specs=[row, pl.BlockSpec((1, D), lambda i: (0, 0)),
                  pl.BlockSpec((TR, D), lambda i: (jnp.maximum(i - 1, 0), 0))],
        out_specs=[pl.BlockSpec((8, LANES), lambda i: (0, 0)), row, row, pl.BlockSpec((1, D), lambda i: (0, 0))],
        out_shape=[jax.ShapeDtypeStruct((8, LANES), F32), jax.ShapeDtypeStruct((Lp, D), F32),
                   jax.ShapeDtypeStruct((Lp, D), BF16), jax.ShapeDtypeStruct((1, D), F32)],
        compiler_params=_params(("arbitrary",)),
    )(h, g, target)


def _lane_lo():
    return lax.broadcasted_iota(jnp.int32, (1, LANES), 1) < FOX_DH


def _headnorm(x, g, scale):
    lo = _lane_lo()
    x2 = x * x
    s0 = jnp.sum(jnp.where(lo, x2, 0.0), axis=-1, keepdims=True)
    s1 = jnp.sum(jnp.where(lo, 0.0, x2), axis=-1, keepdims=True)
    r = jnp.where(lo, lax.rsqrt(s0 / FOX_DH + EPS), lax.rsqrt(s1 / FOX_DH + EPS))
    return x * r * g * scale


AUG = 3


def _split3(x):
    hi = x.astype(BF16).astype(F32)
    mid = (x - hi).astype(BF16).astype(F32)
    return hi, mid, x - hi - mid


def _fox_prep_fwd(proj, bf, qg, kg, T, D, name):
    Lp = proj.shape[0]
    nb = D // LANES
    scale = FOX_DH ** -0.5 * LOG2E

    def body(q_ref, k_ref, v_ref, fl_ref, bf_ref, qg_ref, kg_ref, qn_ref, ka_ref, kb_ref, vb_ref, c_ref, carry_ref):
        i = pl.program_id(0)

        @pl.when(i == 0)
        def _():
            carry_ref[...] = jnp.zeros_like(carry_ref)

        vb_ref[...] = v_ref[...].astype(BF16)
        log_f = _log_sigmoid(fl_ref[...] + bf_ref[...])
        row = lax.broadcasted_iota(jnp.int32, (T, T), 0)
        col = lax.broadcasted_iota(jnp.int32, (T, T), 1)
        tri = (col <= row).astype(F32)
        c = jnp.dot(tri, log_f, precision=HI, preferred_element_type=F32) + carry_ref[...]
        c2 = c * LOG2E
        c_ref[...] = c2
        last = lax.broadcasted_iota(jnp.int32, (T, 1), 0) == T - 1
        carry_ref[...] = jnp.sum(jnp.where(last, c, 0.0), axis=0, keepdims=True)

        is_pad = (i * T + lax.broadcasted_iota(jnp.int32, (T, 1), 0)) < N_PAD
        negc = jnp.where(is_pad, -PAD_SHIFT, -c2)
        lane = lax.broadcasted_iota(jnp.int32, (1, LANES), 1)
        for b in range(nb):
            sl = slice(b * LANES, (b + 1) * LANES)
            qn_ref[:, sl] = _headnorm(q_ref[:, sl], qg_ref[...], scale).astype(BF16)
            kn = _headnorm(k_ref[:, sl], kg_ref[...], 1.0)
            ka = jnp.where(lane < FOX_DH, kn, 0.0)
            kb = jnp.where(lane < FOX_DH, 0.0, kn)
            for n, (pa, pb) in enumerate(zip(_split3(_pick_col(negc, 2 * b)), _split3(_pick_col(negc, 2 * b + 1)))):
                ka = jnp.where(lane == FOX_DH + n, pa, ka)
                kb = jnp.where(lane == n, pb, kb)
            ka_ref[:, sl] = ka.astype(BF16)
            kb_ref[:, sl] = kb.astype(BF16)

    wide = lambda j: pl.BlockSpec((T, D), lambda i: (i, j))
    vec = pl.BlockSpec((1, LANES), lambda i: (0, 0))
    return pl.pallas_call(
        body, name=name, grid=(Lp // T,),
        in_specs=[wide(0), wide(1), wide(2), pl.BlockSpec((T, LANES), lambda i: (i, 4 * nb)), vec, vec, vec],
        out_specs=[wide(0), wide(0), wide(0), wide(0), pl.BlockSpec((T, LANES), lambda i: (i, 0))],
        out_shape=[jax.ShapeDtypeStruct((Lp, D), BF16)] * 4 + [jax.ShapeDtypeStruct((Lp, LANES), F32)],
        scratch_shapes=[pltpu.VMEM((1, LANES), F32)],
        compiler_params=_params(("arbitrary",)),
    )(proj, proj, proj, proj, bf, qg, kg)


def _fox_prep_bwd(proj, bf, qg, kg, dqn, dkn, dv, dgate, dc, T, D, name):
    Lp = proj.shape[0]
    nb = D // LANES
    nt = Lp // T
    scale = FOX_DH ** -0.5 * LOG2E

    def body(q_ref, k_ref, fl_ref, bf_ref, qg_ref, kg_ref, dqn_ref, dkn_ref, dv_ref, dgate_ref, dc_ref,
             dproj_ref, sm_ref, carry_ref):
        @pl.when(pl.program_id(0) == 0)
        def _():
            carry_ref[...] = jnp.zeros_like(carry_ref)
            sm_ref[...] = jnp.zeros_like(sm_ref)

        dqg = jnp.zeros((1, LANES), F32)
        dkg = jnp.zeros((1, LANES), F32)
        for b in range(nb):
            sl = slice(b * LANES, (b + 1) * LANES)
            _, vjp = jax.vjp(lambda x, g: _headnorm(x, g, scale), q_ref[:, sl], qg_ref[...])
            dx, dg = vjp(dqn_ref[:, sl] * LN2)
            dproj_ref[:, sl] = dx.astype(BF16)
            dqg = dqg + dg
            _, vjp = jax.vjp(lambda x, g: _headnorm(x, g, 1.0), k_ref[:, sl], kg_ref[...])
            dx, dg = vjp(dkn_ref[:, sl] * LN2)
            dproj_ref[:, D + b * LANES:D + (b + 1) * LANES] = dx.astype(BF16)
            dkg = dkg + dg
        dproj_ref[:, 2 * D:3 * D] = dv_ref[...].astype(BF16)
        dproj_ref[:, 3 * D:4 * D] = dgate_ref[...]
        dcv = dc_ref[...]
        row = lax.broadcasted_iota(jnp.int32, (T, T), 0)
        col = lax.broadcasted_iota(jnp.int32, (T, T), 1)
        triu = (col >= row).astype(F32)
        dlogf = jnp.dot(triu, dcv, precision=HI, preferred_element_type=F32) + carry_ref[...]
        carry_ref[...] += jnp.sum(dcv, axis=0, keepdims=True)
        _, vjp = jax.vjp(_log_sigmoid, fl_ref[...] + bf_ref[...])
        (dfl,) = vjp(dlogf)
        dproj_ref[:, 4 * D:] = dfl.astype(BF16)
        sm_ref[0:1, :] += jnp.sum(dfl, axis=0, keepdims=True)
        sm_ref[1:2, :] += dqg
        sm_ref[2:3, :] += dkg

    wide = lambda j: pl.BlockSpec((T, D), lambda i: (nt - 1 - i, j))
    narrow = lambda j: pl.BlockSpec((T, LANES), lambda i: (nt - 1 - i, j))
    vec = pl.BlockSpec((1, LANES), lambda i: (0, 0))
    return pl.pallas_call(
        body, name=name, grid=(nt,),
        in_specs=[wide(0), wide(1), narrow(4 * nb), vec, vec, vec, wide(0), wide(0), wide(0), wide(0), narrow(0)],
        out_specs=[pl.BlockSpec((T, 4 * D + LANES), lambda i: (nt - 1 - i, 0)), pl.BlockSpec((8, LANES), lambda i: (0, 0))],
        out_shape=[jax.ShapeDtypeStruct((Lp, 4 * D + LANES), BF16), jax.ShapeDtypeStruct((8, LANES), F32)],
        scratch_shapes=[pltpu.VMEM((1, LANES), F32)],
        compiler_params=_params(("arbitrary",)),
    )(proj, proj, proj, bf, qg, kg, dqn, dkn, dv, dgate, dc)


def _fox_q_operands(q):
    lane = lax.broadcasted_iota(jnp.int32, (1, LANES), 1)
    zero, one = jnp.zeros_like(q), jnp.ones_like(q)
    return (jnp.where(lane < FOX_DH, q, jnp.where(lane < FOX_DH + AUG, one, zero)),
            jnp.where(lane < FOX_DH, jnp.where(lane < AUG, one, zero), q))


def _fox_mask(q0, nq, k0, nk):
    qpos = q0 + lax.broadcasted_iota(jnp.int32, (nq, 1), 0)
    kpos = k0 + lax.broadcasted_iota(jnp.int32, (1, nk), 1)
    return (kpos <= qpos) & ((kpos >= N_PAD) | (qpos < N_PAD))


def _diag_split(T):
    return 512 if T == 768 else T // 2


def _rows_joined(head_rows, tail_rows, n_head):
    return jnp.concatenate([head_rows[:n_head], tail_rows], axis=0)


def _pick_col(blk, idx):
    lane = lax.broadcasted_iota(jnp.int32, (1, LANES), 1)
    return jnp.sum(jnp.where(lane == idx, blk, 0.0), axis=1, keepdims=True)


def _split_halves(blk):
    lo = _lane_lo()
    return (jnp.max(jnp.where(lo, blk, -jnp.inf), axis=1, keepdims=True),
            jnp.max(jnp.where(lo, -jnp.inf, blk), axis=1, keepdims=True))


def _fox_attn_fwd(qn, ka, kb, vb, c, proj, xchg, T, D, name):
    Lp = qn.shape[0]
    P = D // LANES
    nt = Lp // T
    nx = len(xchg)

    def body(q_ref, ka_ref, kb_ref, v_ref, c_ref, g_ref, *rest):
        x_in, (o_ref, og_ref, m_ref, li_ref), x_out, sems = rest[:nx], rest[nx:nx + 4], rest[nx + 4:2 * nx + 4], rest[2 * nx + 4:]
        p = pl.program_id(0)
        i = pl.program_id(1)

        @pl.when((p == 0) & (i == 0))
        def _():
            for cp in _xchg_copies(x_in, x_out, [False] * nx, sems):
                cp.start()

        lo = _lane_lo()
        q = q_ref[...]
        qh = _fox_q_operands(q)
        cblk = c_ref[...]
        cq = tuple(_pick_col(cblk, 2 * p + h) for h in (0, 1))
        one = jnp.ones_like(q)

        def block(r0, nr, k0, nk, carry, masked):
            kj = (ka_ref[pl.ds(k0, nk), :], kb_ref[pl.ds(k0, nk), :])
            vj = v_ref[pl.ds(k0, nk), :]
            vh = (jnp.where(lo, vj, one[:nk]), jnp.where(lo, one[:nk], vj))
            mask = _fox_mask(i * T + r0, nr, k0, nk) if masked else None
            out = []
            for h in (0, 1):
                m, acc = carry[h]
                cqr = cq[h][r0:r0 + nr]
                t = lax.dot_general(qh[h][r0:r0 + nr], kj[h], (((1,), (1,)), ((), ())), preferred_element_type=F32)
                if masked:
                    t = jnp.where(mask, t, NEG)
                m_new = jnp.ceil(jnp.maximum(m, cqr + jnp.max(t, axis=1, keepdims=True)))
                pr = jnp.exp2(t + (cqr - m_new)).astype(BF16)
                acc = jnp.exp2(m - m_new) * acc + jnp.dot(pr, vh[h], preferred_element_type=F32)
                out.append((m_new, acc))
            return tuple(out)

        init = tuple((jnp.full((T, 1), NEG, F32), jnp.zeros((T, LANES), F32)) for _ in (0, 1))
        carry = lax.fori_loop(0, i, lambda j, cr: block(0, T, pl.multiple_of(j * T, LANES), T, cr, False), init)
        (m0, a0), (m1, a1) = block(0, T, pl.multiple_of(i * T, LANES), T, carry, True)
        l0 = pltpu.roll(a0, FOX_DH, 1)
        l1 = pltpu.roll(a1, FOX_DH, 1)
        o = jnp.where(lo, a0 / l0, a1 / l1)
        o_ref[...] = o
        m_ref[...] = jnp.where(lo, m0, m1)
        li_ref[...] = jnp.where(lo, 1.0 / l0, 1.0 / l1)
        og_ref[...] = (o * jax.nn.sigmoid(g_ref[...])).astype(BF16)

        @pl.when((p == P - 1) & (i == nt - 1))
        def _():
            for cp in _xchg_copies(x_in, x_out, [False] * nx, sems):
                cp.wait()

    tile = pl.BlockSpec((T, LANES), lambda p, i: (i, p))
    full = pl.BlockSpec((Lp, LANES), lambda p, i: (0, p))
    HBM = pl.BlockSpec(memory_space=pltpu.HBM)
    return pl.pallas_call(
        body, name=name, grid=(P, nt),
        in_specs=[tile, full, full, full, pl.BlockSpec((T, LANES), lambda p, i: (i, 0)),
                  pl.BlockSpec((T, LANES), lambda p, i: (i, 3 * P + p))] + [HBM] * nx,
        out_specs=[tile, tile, tile, tile] + [HBM] * nx,
        out_shape=[jax.ShapeDtypeStruct((Lp, D), F32), jax.ShapeDtypeStruct((Lp, D), BF16),
                   jax.ShapeDtypeStruct((Lp, D), F32), jax.ShapeDtypeStruct((Lp, D), F32)]
        + _xchg_out_shapes(xchg, [False] * nx),
        scratch_shapes=_xchg_sems(nx),
        compiler_params=_params(("arbitrary", "arbitrary")),
    )(qn, ka, kb, vb, c, proj, *xchg)


def _fox_out_dx(dhb, w_out, o, proj, linv, D, name):
    Lp = o.shape[0]
    tm = HEAD_ROWS

    def body(a_ref, w_ref, o_ref, g_ref, li_ref, do_ref, dg_ref, dl_ref):
        lo = _lane_lo()
        dog_all = lax.dot_general(a_ref[...], w_ref[...], (((1,), (1,)), ((), ())), preferred_element_type=F32)
        for b in range(D // LANES):
            sl = slice(b * LANES, (b + 1) * LANES)
            dog = dog_all[:, sl]
            sig = jax.nn.sigmoid(g_ref[:, sl])
            ov = o_ref[:, sl]
            do = (dog * sig * li_ref[:, sl]).astype(BF16)
            do_ref[:, sl] = do
            dg_ref[:, sl] = (dog * ov * sig * (1.0 - sig)).astype(BF16)
            t = do.astype(F32) * ov
            d0 = jnp.sum(jnp.where(lo, t, 0.0), axis=1, keepdims=True)
            d1 = jnp.sum(jnp.where(lo, 0.0, t), axis=1, keepdims=True)
            dl_ref[:, sl] = jnp.where(lo, d0, d1)

    row = pl.BlockSpec((tm, D), lambda i: (i, 0))
    return pl.pallas_call(
        body, name=name, grid=(Lp // tm,),
        in_specs=[row, pl.BlockSpec((D, D), lambda i: (0, 0)), row, pl.BlockSpec((tm, D), lambda i: (i, 3)), row],
        out_specs=[row, row, row],
        out_shape=[jax.ShapeDtypeStruct((Lp, D), BF16), jax.ShapeDtypeStruct((Lp, D), BF16),
                   jax.ShapeDtypeStruct((Lp, D), F32)],
        compiler_params=_params(("parallel",)),
    )(dhb, w_out, o, proj, linv)


def _fox_attn_bwd(qn, ka, kb, vb, c, do, mshift, delta, xchg, T, D, name):
    Lp = qn.shape[0]
    P = D // LANES
    nt = Lp // T
    nx = len(xchg)

    def body(q_ref, do_ref, m_ref, dl_ref, c_ref, ka_ref, kb_ref, v_ref, *rest):
        x_in, (dq_ref, dk_ref, dv_ref, dc_ref), x_out, sems = rest[:nx], rest[nx:nx + 4], rest[nx + 4:2 * nx + 4], rest[2 * nx + 4:]
        p = pl.program_id(0)
        i = pl.program_id(1)

        @pl.when((p == 0) & (i == 0))
        def _():
            for cp in _xchg_copies(x_in, x_out, [True] * nx, sems):
                cp.start()

        @pl.when(i == 0)
        def _():
            dk_ref[...] = jnp.zeros_like(dk_ref)
            dv_ref[...] = jnp.zeros_like(dv_ref)
            dc_ref[...] = jnp.zeros_like(dc_ref)

        lo = _lane_lo()
        q = q_ref[...]
        do = do_ref[...]
        zero = jnp.zeros_like(q)
        qh = _fox_q_operands(q)
        doh = (jnp.where(lo, do, zero), jnp.where(lo, zero, do))
        msh = _split_halves(m_ref[...])
        dlt = _split_halves(dl_ref[...])
        cblk = c_ref[...]
        shift = tuple(_pick_col(cblk, 2 * p + h) - msh[h] for h in (0, 1))

        def block(r0, nr, k0, nk, carry, masked):
            kj = (ka_ref[pl.ds(k0, nk), :], kb_ref[pl.ds(k0, nk), :])
            vj = v_ref[pl.ds(k0, nk), :]
            mask = _fox_mask(i * T + r0, nr, k0, nk) if masked else None
            qr, dor = q[r0:r0 + nr], do[r0:r0 + nr]
            dqs, dks, dvs = [], [], []
            for h in (0, 1):
                t = lax.dot_general(qh[h][r0:r0 + nr], kj[h], (((1,), (1,)), ((), ())), preferred_element_type=F32)
                if masked:
                    t = jnp.where(mask, t, NEG)
                pb = jnp.exp2(t + shift[h][r0:r0 + nr]).astype(BF16)
                dp = lax.dot_general(doh[h][r0:r0 + nr], vj, (((1,), (1,)), ((), ())), preferred_element_type=F32)
                ds = pb.astype(F32) * (dp - dlt[h][r0:r0 + nr])
                dsb = ds.astype(BF16)
                dqs.append(carry[h] + jnp.dot(dsb, kj[h], preferred_element_type=F32))
                dks.append(lax.dot_general(dsb, qr, (((0,), (0,)), ((), ())), preferred_element_type=F32))
                dvs.append(lax.dot_general(pb, dor, (((0,), (0,)), ((), ())), preferred_element_type=F32))
                dc_ref[0, h:h + 1, pl.ds(k0, nk)] += -jnp.sum(ds, axis=0, keepdims=True)
            dk_ref[pl.ds(k0, nk), :] += jnp.where(lo, dks[0], dks[1])
            dv_ref[pl.ds(k0, nk), :] += jnp.where(lo, dvs[0], dvs[1])
            return tuple(dqs)

        init = (jnp.zeros((T, LANES), F32), jnp.zeros((T, LANES), F32))
        carry = lax.fori_loop(0, i, lambda j, cr: block(0, T, pl.multiple_of(j * T, LANES), T, cr, False), init)
        kd = pl.multiple_of(i * T, LANES)
        ta = _diag_split(T)
        da = block(0, T, kd, ta, carry, True)
        db = block(ta, T - ta, pl.multiple_of(kd + ta, LANES), T - ta, tuple(d[ta:] for d in da), True)
        dq_ref[...] = jnp.where(lo, _rows_joined(da[0], db[0], ta), _rows_joined(da[1], db[1], ta))

        @pl.when((p == P - 1) & (i == nt - 1))
        def _():
            for cp in _xchg_copies(x_in, x_out, [True] * nx, sems):
                cp.wait()

    tile = pl.BlockSpec((T, LANES), lambda p, i: (i, p))
    full = pl.BlockSpec((Lp, LANES), lambda p, i: (0, p))
    HBM = pl.BlockSpec(memory_space=pltpu.HBM)
    return pl.pallas_call(
        body, name=name, grid=(P, nt),
        in_specs=[tile, tile, tile, tile, pl.BlockSpec((T, LANES), lambda p, i: (i, 0)), full, full, full]
        + [HBM] * nx,
        out_specs=[tile, full, full, pl.BlockSpec((1, 8, Lp), lambda p, i: (p, 0, 0))] + [HBM] * nx,
        out_shape=[jax.ShapeDtypeStruct((Lp, D), F32)] * 3 + [jax.ShapeDtypeStruct((P, 8, Lp), F32)]
        + _xchg_out_shapes(xchg, [True] * nx),
        scratch_shapes=_xchg_sems(nx),
        compiler_params=_params(("arbitrary", "arbitrary")),
    )(qn, do, mshift, delta, c, ka, kb, vb, *xchg)


def _scan_rows(x, reverse):
    C = x.shape[0]
    row = lax.broadcasted_iota(jnp.int32, (C, 1), 0)
    step = 1
    while step < C:
        if reverse:
            x = x + jnp.where(row < C - step, pltpu.roll(x, C - step, 0), 0.0)
        else:
            x = x + jnp.where(row >= step, pltpu.roll(x, step, 0), 0.0)
        step *= 2
    return x


@jax.custom_vjp
def _cumsum_rows(x):
    return _scan_rows(x, False)


_cumsum_rows.defvjp(lambda x: (_scan_rows(x, False), None), lambda _, g: (_scan_rows(g, True),))


def _hgrn_chunk(St, qr, z, vi, go, p0, p1, gg):
    C = qr.shape[0]
    lb = jax.nn.sigmoid(p1 - p0)
    a = jnp.log(lb)
    cc = jnp.log1p(-lb) + _log_sigmoid(z)
    log_f = jnp.maximum(a, cc) + jnp.log1p(jnp.exp(-jnp.abs(a - cc)))
    k = (1.0 - lb) * jax.nn.sigmoid(-z)
    q = qr * jax.nn.sigmoid(qr)
    row = lax.broadcasted_iota(jnp.int32, (C, C), 0)
    col = lax.broadcasted_iota(jnp.int32, (C, C), 1)
    causal = col <= row
    b = _cumsum_rows(log_f)
    mid = lax.broadcasted_iota(jnp.int32, (C, 1), 0) == C // 2 - 1
    r = jnp.sum(jnp.where(mid, b, 0.0), axis=0, keepdims=True)
    b_last = jnp.sum(log_f, axis=0, keepdims=True)
    attn = jnp.where(causal, _d_nt(q * jnp.exp(b - r), k * jnp.exp(r - b)), 0.0)
    o = _d_nn(attn, vi) + _d_nt(q * jnp.exp(b), St)
    St_new = St * jnp.exp(b_last) + _d_tn(vi, k * jnp.exp(b_last - b))
    og = _rms(o, gg) * (go * jax.nn.sigmoid(go))
    return St_new, og


def _hgrn_heads_per_step(H):
    return 8 if H % 8 == 0 else 4 if H % 4 == 0 else 1


def _hgrn_specs(T, W, nhb, rev_nt=None):
    if rev_nt is None:
        return [pl.BlockSpec((T, W), functools.partial(lambda hb, t, g: (t, g * nhb + hb), g=g)) for g in range(4)]
    return [pl.BlockSpec((T, W), functools.partial(lambda hb, t, g: (rev_nt - 1 - t, g * nhb + hb), g=g))
            for g in range(4)]


def _hgrn_fwd(proj, lbp, gg, T, name):
    Lp = proj.shape[0]
    D = proj.shape[1] // 4
    H = D // LANES
    hps = _hgrn_heads_per_step(H)
    W = hps * LANES
    nhb = H // hps
    nt = Lp // T
    ncc = T // HGRN_CHUNK

    def body(q_ref, z_ref, i_ref, go_ref, p_ref, gg_ref, og_ref, ss_ref, st_ref):
        @pl.when(pl.program_id(1) == 0)
        def _():
            st_ref[...] = jnp.zeros_like(st_ref)

        gain = gg_ref[...]

        def chunk(cidx, states):
            sl = pl.ds(pl.multiple_of(cidx * HGRN_CHUNK, HGRN_CHUNK), HGRN_CHUNK)
            new = []
            for hh in range(hps):
                ln = slice(hh * LANES, (hh + 1) * LANES)
                ss_ref[hh, cidx] = states[hh]
                St_new, og = _hgrn_chunk(states[hh], q_ref[sl, ln], z_ref[sl, ln], i_ref[sl, ln], go_ref[sl, ln],
                                         p_ref[0:1, ln], p_ref[1:2, ln], gain)
                og_ref[sl, ln] = og.astype(BF16)
                new.append(St_new)
            return tuple(new)

        assert ncc % 2 == 0
        states = lax.fori_loop(0, ncc // 2, lambda c2, st: chunk(2 * c2 + 1, chunk(2 * c2, st)),
                               tuple(st_ref[hh] for hh in range(hps)))
        for hh in range(hps):
            st_ref[hh] = states[hh]

    return pl.pallas_call(
        body, name=name, grid=(nhb, nt),
        in_specs=_hgrn_specs(T, W, nhb) + [pl.BlockSpec((2, W), lambda hb, t: (0, hb)),
                                           pl.BlockSpec((1, LANES), lambda hb, t: (0, 0))],
        out_specs=[pl.BlockSpec((T, W), lambda hb, t: (t, hb)),
                   pl.BlockSpec((hps, ncc, LANES, LANES), lambda hb, t: (hb, t, 0, 0))],
        out_shape=[jax.ShapeDtypeStruct((Lp, D), BF16),
                   jax.ShapeDtypeStruct((H, Lp // HGRN_CHUNK, LANES, LANES), F32)],
        scratch_shapes=[pltpu.VMEM((hps, LANES, LANES), F32)],
        compiler_params=_params(("parallel", "arbitrary")),
    )(proj, proj, proj, proj, lbp, gg)


def _hgrn_bwd(proj, lbp, gg, dog, ss, T, name):
    Lp = proj.shape[0]
    D = proj.shape[1] // 4
    H = D // LANES
    hps = _hgrn_heads_per_step(H)
    W = hps * LANES
    nhb = H // hps
    assert nhb == 1, "d proj is written as whole rows: every head in one grid step"
    nt = Lp // T
    ncc = T // HGRN_CHUNK

    def body(q_ref, z_ref, i_ref, go_ref, p_ref, gg_ref, dog_ref, ss_ref, dproj_ref, dp_ref, dgg_ref, dst_ref):
        hb = pl.program_id(0)
        t = pl.program_id(1)

        @pl.when(t == 0)
        def _():
            dst_ref[...] = jnp.zeros_like(dst_ref)
            dp_ref[...] = jnp.zeros_like(dp_ref)

        @pl.when((t == 0) & (hb == 0))
        def _():
            dgg_ref[...] = jnp.zeros_like(dgg_ref)

        gain = gg_ref[...]
        row0 = (nt - 1 - t) * T

        def chunk(cc, carry):
            dstates, dps, dgain_sum = carry
            cidx = ncc - 1 - cc
            r0 = pl.multiple_of(cidx * HGRN_CHUNK, HGRN_CHUNK)
            sl = pl.ds(r0, HGRN_CHUNK)
            real = (row0 + r0 + lax.broadcasted_iota(jnp.int32, (HGRN_CHUNK, 1), 0)) >= N_PAD
            new_d, new_p = [], []
            for hh in range(hps):
                ln = slice(hh * LANES, (hh + 1) * LANES)
                _, vjp = jax.vjp(_hgrn_chunk, ss_ref[hh, cidx], q_ref[sl, ln], z_ref[sl, ln], i_ref[sl, ln],
                                 go_ref[sl, ln], p_ref[0:1, ln], p_ref[1:2, ln], gain)
                dSt, dq, dz, di, dgo, dp0, dp1, dgain = vjp((dstates[hh], dog_ref[sl, ln]))
                for grp, dval in enumerate((dq, dz, di, dgo)):
                    dproj_ref[sl, grp * D + hh * LANES:grp * D + (hh + 1) * LANES] = (
                        jnp.where(real, dval, 0.0).astype(BF16))
                new_d.append(dSt)
                new_p.append((dps[hh][0] + dp0, dps[hh][1] + dp1))
                dgain_sum = dgain_sum + dgain
            return tuple(new_d), tuple(new_p), dgain_sum

        zero_row = jnp.zeros((1, LANES), F32)
        init = (tuple(dst_ref[hh] for hh in range(hps)), tuple((zero_row, zero_row) for _ in range(hps)), zero_row)
        dstates, dps, dgain_sum = lax.fori_loop(0, ncc, chunk, init)
        for hh in range(hps):
            ln = slice(hh * LANES, (hh + 1) * LANES)
            dst_ref[hh] = dstates[hh]
            dp_ref[0:1, ln] += dps[hh][0]
            dp_ref[1:2, ln] += dps[hh][1]
        dgg_ref[0:1, :] += dgain_sum

    rev = pl.BlockSpec((T, W), lambda hb, t: (nt - 1 - t, hb))
    return pl.pallas_call(
        body, name=name, grid=(nhb, nt),
        in_specs=_hgrn_specs(T, W, nhb, nt) + [pl.BlockSpec((2, W), lambda hb, t: (0, hb)),
                                               pl.BlockSpec((1, LANES), lambda hb, t: (0, 0)), rev,
                                               pl.BlockSpec((hps, ncc, LANES, LANES),
                                                            lambda hb, t: (hb, nt - 1 - t, 0, 0))],
        out_specs=[pl.BlockSpec((T, 4 * D), lambda hb, t: (nt - 1 - t, 0)), pl.BlockSpec((8, W), lambda hb, t: (0, hb)),
                   pl.BlockSpec((8, LANES), lambda hb, t: (0, 0))],
        out_shape=[jax.ShapeDtypeStruct((Lp, 4 * D), BF16), jax.ShapeDtypeStruct((8, D), F32),
                   jax.ShapeDtypeStruct((8, LANES), F32)],
        scratch_shapes=[pltpu.VMEM((hps, LANES, LANES), F32)],
        compiler_params=_params(("arbitrary", "arbitrary")),
    )(proj, proj, proj, proj, lbp, gg, dog, ss)


def _xchg_sems(n_arr):
    return [pltpu.SemaphoreType.DMA((n_arr * (N_DEV - 1),)), pltpu.SemaphoreType.DMA((n_arr * (N_DEV - 1),)),
            pltpu.SemaphoreType.DMA((n_arr,))]


def _xchg_copies(ins, outs, per_peer, sems):
    send_sems, recv_sems, local_sems = sems
    x, y, c = lax.axis_index("x"), lax.axis_index("y"), lax.axis_index("c")
    me = 4 * x + 2 * y + c
    copies = []
    for n in range(len(ins)):
        src = ins[n].at[me] if per_peer[n] else ins[n]
        copies.append(pltpu.make_async_copy(src, outs[n].at[me], local_sems.at[n]))
    for rel in range(1, N_DEV):
        fx, fy, fc = (rel >> 2) & 1, (rel >> 1) & 1, rel & 1
        px = 1 - x if fx else x
        py = 1 - y if fy else y
        pc = 1 - c if fc else c
        peer = 4 * px + 2 * py + pc
        for n in range(len(ins)):
            src = ins[n].at[peer] if per_peer[n] else ins[n]
            copies.append(pltpu.make_async_remote_copy(
                src_ref=src, dst_ref=outs[n].at[me],
                send_sem=send_sems.at[n * (N_DEV - 1) + rel - 1],
                recv_sem=recv_sems.at[n * (N_DEV - 1) + rel - 1],
                device_id=(px, py, pc), device_id_type=pl.DeviceIdType.MESH))
    return copies


def _xchg_out_shapes(arrays, per_peer):
    return [jax.ShapeDtypeStruct(a.shape if pp else (N_DEV,) + a.shape, a.dtype) for a, pp in zip(arrays, per_peer)]


def _exchange(arrays, per_peer, name):
    n_arr = len(arrays)
    HBM = pl.BlockSpec(memory_space=pltpu.HBM)

    def body(*refs):
        copies = _xchg_copies(refs[:n_arr], refs[n_arr:2 * n_arr], per_peer, refs[2 * n_arr:])
        for cp in copies:
            cp.start()
        for cp in copies:
            cp.wait()

    return pl.pallas_call(
        body, name=name,
        in_specs=[HBM] * n_arr, out_specs=[HBM] * n_arr, out_shape=_xchg_out_shapes(arrays, per_peer),
        scratch_shapes=_xchg_sems(n_arr),
    )(*arrays)


def _gather2_copy(ins, outs, sems, a, k, block, to, own=False):
    send_sems, recv_sems, _ = sems
    px, py, pc = block
    dst = outs[a].at[4 * px + 2 * py + pc]
    return pltpu.make_async_remote_copy(
        src_ref=ins[a] if own else dst, dst_ref=dst,
        send_sem=send_sems.at[a * (N_DEV - 1) + k], recv_sem=recv_sems.at[a * (N_DEV - 1) + k],
        device_id=to, device_id_type=pl.DeviceIdType.MESH)


def _gather2_first(ins, outs, sems):
    x, y, c = lax.axis_index("x"), lax.axis_index("y"), lax.axis_index("c")
    me, chips = (x, y, c), [(1 - x, y), (x, 1 - y), (1 - x, 1 - y)]
    n_arr = len(ins)
    local = [pltpu.make_async_copy(ins[a], outs[a].at[4 * x + 2 * y + c], sems[2].at[a]) for a in range(n_arr)]
    first = [_gather2_copy(ins, outs, sems, a, 0, me, (x, y, 1 - c), own=True) for a in range(n_arr)]
    first += [_gather2_copy(ins, outs, sems, a, 1 + j, me, (*chip, c), own=True)
              for j, chip in enumerate(chips) for a in range(n_arr)]
    return local, first


def _gather2_finish(ins, outs, sems):
    x, y, c = lax.axis_index("x"), lax.axis_index("y"), lax.axis_index("c")
    me, sibling, chips = (x, y, c), (x, y, 1 - c), [(1 - x, y), (x, 1 - y), (1 - x, 1 - y)]
    n_arr = len(ins)
    local, first = _gather2_first(ins, outs, sems)
    passed = []
    for j, chip in enumerate(chips):
        for a in range(n_arr):
            _gather2_copy(ins, outs, sems, a, 1 + j, (*chip, c), me).wait_recv()
            cp = _gather2_copy(ins, outs, sems, a, 4 + j, (*chip, c), sibling)
            cp.start()
            passed.append(cp)
    for a in range(n_arr):
        _gather2_copy(ins, outs, sems, a, 0, sibling, me).wait_recv()
        for j, chip in enumerate(chips):
            _gather2_copy(ins, outs, sems, a, 4 + j, (*chip, 1 - c), me).wait_recv()
    for cp in first + passed:
        cp.wait_send()
    for cp in local:
        cp.wait()


def _gather_two_level(arrays, name):
    n_arr = len(arrays)
    HBM = pl.BlockSpec(memory_space=pltpu.HBM)

    def body(*refs):
        ins, outs, sems = refs[:n_arr], refs[n_arr:2 * n_arr], refs[2 * n_arr:]
        local, first = _gather2_first(ins, outs, sems)
        for cp in local + first:
            cp.start()
        _gather2_finish(ins, outs, sems)

    return pl.pallas_call(
        body, name=name,
        in_specs=[HBM] * n_arr, out_specs=[HBM] * n_arr, out_shape=_xchg_out_shapes(arrays, [False] * n_arr),
        scratch_shapes=_xchg_sems(n_arr),
    )(*arrays)


ADAMW_VMEM_BUDGET = 36 * 1024 * 1024


def _adamw(recv, w, m, v, name):
    shape = w.shape
    C = shape[-1]
    R = math.prod(shape[:-1])
    lanes = -(-C // LANES) * LANES
    row_bytes = 2 * lanes * (N_DEV * recv.dtype.itemsize + 7 * 4)
    rc = _row_chunk(R, max(16, ADAMW_VMEM_BUDGET // row_bytes), 16 if recv.dtype == BF16 else 8)

    def body(r_ref, w_ref, m_ref, v_ref, g_ref, d_ref, mo_ref, vo_ref):
        g = r_ref[0].astype(F32)
        for s in range(1, N_DEV):
            g = g + r_ref[s].astype(F32)
        mn = ADAM_B1 * m_ref[...] + (1.0 - ADAM_B1) * g
        vn = ADAM_B2 * v_ref[...] + (1.0 - ADAM_B2) * (g * g)
        m_hat = mn / (1.0 - ADAM_B1 ** ADAM_STEP)
        v_hat = vn / (1.0 - ADAM_B2 ** ADAM_STEP)
        g_ref[...] = g
        d_ref[...] = -ADAM_LR * (m_hat / (jnp.sqrt(v_hat) + ADAM_EPS) + ADAM_WD * w_ref[...])
        mo_ref[...] = mn
        vo_ref[...] = vn

    row = pl.BlockSpec((rc, C), lambda i: (i, 0))
    outs = pl.pallas_call(
        body, name=name, grid=(R // rc,),
        in_specs=[pl.BlockSpec((N_DEV, rc, C), lambda i: (0, i, 0)), row, row, row],
        out_specs=[row] * 4,
        out_shape=[jax.ShapeDtypeStruct((R, C), F32)] * 4,
        compiler_params=_params(("parallel",)),
    )(recv.reshape(N_DEV, R, C), w.reshape(R, C), m.reshape(R, C), v.reshape(R, C))
    return [o.reshape(shape) for o in outs]


def _gathered_to_full(g, name):
    if name in COL_SHARDED:
        g = jnp.moveaxis(g, 0, -2)
        return g.reshape(g.shape[:-2] + (g.shape[-2] * g.shape[-1],))
    g = jnp.moveaxis(g, 0, -3)
    return g.reshape(g.shape[:-3] + (g.shape[-3] * g.shape[-2], g.shape[-1]))


def _full_to_slabs(full, name):
    if name in COL_SHARDED:
        f = full.reshape(full.shape[:-1] + (N_DEV, full.shape[-1] // N_DEV))
        return jnp.moveaxis(f, -2, 0)
    f = full.reshape(full.shape[:-2] + (N_DEV, full.shape[-2] // N_DEV, full.shape[-1]))
    return jnp.moveaxis(f, -3, 0)


def _pack_small(arrs):
    rows = []
    for a in arrs:
        flat = a.astype(F32).reshape(-1)
        pad = (-flat.shape[0]) % LANES
        rows.append(jnp.pad(flat, (0, pad)).reshape(-1, LANES))
    p = jnp.concatenate(rows, axis=0)
    return jnp.pad(p, ((0, (-p.shape[0]) % 8), (0, 0)))


def _unpack_small(packed, shapes):
    out, off = [], 0
    for shp in shapes:
        n = math.prod(shp)
        nr = -(-n // LANES)
        out.append(packed[off:off + nr].reshape(-1)[:n].reshape(shp))
        off += nr
    return out


def _local_step(x, target, meta, fox_in_shard, late, small):
    S, D = x.shape
    Lp = S + HEAD_ROWS
    T = ROW_TILE if Lp % ROW_TILE == 0 else HEAD_ROWS
    P = D // LANES
    row = lambda v: v.reshape(1, -1).astype(F32)

    n_heads = fox_in_shard.shape[-1] * N_DEV - 4 * D
    bf = jnp.pad(row(small["fox_b_f"]), ((0, 0), (0, LANES - small["fox_b_f"].size)))
    qg = jnp.tile(row(small["fox_q_norm"]), (1, 2))
    kg = jnp.tile(row(small["fox_k_norm"]), (1, 2))


    h0, hn0, hn0t, g_fin = _embed_rms_fwd(x, meta, row(small["attn_norm"][0]), [fox_in_shard], "rms0_fwd")
    w_fox_in = _gathered_to_full(g_fin, "fox_w_in")
    w_fin = jnp.pad(w_fox_in[0], ((0, 0), (0, LANES - w_fox_in.shape[-1] % LANES)))
    proj0 = _mm(hn0, w_fin, "nn", F32, "fox_in_fwd")
    qn, ka, kb, vb, c = _fox_prep_fwd(proj0, bf, qg, kg, T, D, "fox_prep_fwd")
    o, og0, mshift, linv, *gathered = _fox_attn_fwd(qn, ka, kb, vb, c, proj0, [late[n] for n in LATE], T, D,
                                                    "fox_attn_fwd")
    wl = {n: _gathered_to_full(g, n) for n, g in zip(LATE, gathered)}
    w_fout, w_hin, w_hout = wl["fox_w_out"][0], wl["hgrn_w_in"][0], wl["hgrn_w_out"][0]
    w_uin, w_uout = wl["ffn_w_in"], wl["ffn_w_out"]
    h1, hf0, hf0t = _out_proj_fwd(og0, w_fout, h0, row(small["ffn_norm"][0]), "fox_out_fwd")
    gu0 = _ffn_in_fwd(hf0, w_uin[0], "ffn0_in_fwd")
    act0 = gu0[2]
    h2, hn1, hn1t = _out_proj_fwd(act0, w_uout[0], h1, row(small["attn_norm"][1]), "ffn0_out_fwd")
    proj1 = _mm(hn1, w_hin, "nn", F32, "hgrn_in_fwd")
    lbp = small["hgrn_lower_bounds"].astype(F32)
    ggn = row(small["hgrn_g_norm"])
    Th = HGRN_TILE if Lp % HGRN_TILE == 0 else HEAD_ROWS
    og1, ss = _hgrn_fwd(proj1, lbp, ggn, T, "hgrn_fwd")
    h3, hf1, hf1t = _out_proj_fwd(og1, w_hout, h2, row(small["ffn_norm"][1]), "hgrn_out_fwd")
    gu1 = _ffn_in_fwd(hf1, w_uin[1], "ffn1_in_fwd")
    act1 = gu1[2]
    h4 = _mm(act1, w_uout[1], "nn", F32, "ffn1_out_fwd", res=h3)
    loss_blk, dh4, dh4b, d_final = _final_loss(h4, row(small["final_norm"]), target, "final_loss")

    grads = {}

    def ffn_bwd(i, dh, dhb, h_in, hft, gu):
        grads_out = _mm(gu[2], dhb, "tn", F32, f"ffn{i}_out_dw")
        dgu = _ffn_out_dx(dhb, w_uout[i], gu[0], gu[1], f"ffn{i}_out_dx")
        grads_in = _mm(hft, dgu, "nn", F32, f"ffn{i}_in_dw", tm=D, tk=_tile(Lp, 1408))
        dh_new, dh_newb, dgain = _in_proj_dx(dgu, w_uin[i], h_in, row(small["ffn_norm"][i]), dh, f"ffn{i}_in_dx")
        return dh_new, dh_newb, grads_in, grads_out, dgain

    dh3, dh3b, g_uin1, g_uout1, d_fn1 = ffn_bwd(1, dh4, dh4b, h3, hf1t, gu1)
    grads["hgrn_w_out"] = _mm(og1, dh3b, "tn", F32, "hgrn_out_dw")[None]
    dog1 = _mm(dh3b, w_hout, "nt", F32, "hgrn_out_dx")
    dproj1, d_lb, d_gg = _hgrn_bwd(proj1, lbp, ggn, dog1, ss, Th, "hgrn_bwd")
    grads["hgrn_w_in"] = _mm(hn1t, dproj1, "nn", F32, "hgrn_in_dw", tm=D, tk=_tile(Lp, 1408))[None]
    dh2, dh2b, d_an1 = _in_proj_dx(dproj1, w_hin, h2, row(small["attn_norm"][1]), dh3, "hgrn_in_dx")
    dh1, dh1b, g_uin0, g_uout0, d_fn0 = ffn_bwd(0, dh2, dh2b, h1, hf0t, gu0)
    grads["ffn_w_in"] = jnp.stack([g_uin0, g_uin1])
    grads["ffn_w_out"] = jnp.stack([g_uout0, g_uout1])
    grads["fox_w_out"] = _mm(og0, dh1b, "tn", F32, "fox_out_dw")[None]
    do, dgate, delta = _fox_out_dx(dh1b, w_fout, o, proj0, linv, D, "fox_out_dx")
    slabs = [_full_to_slabs(grads[n], n).astype(BF16) for n in LATE]
    dqn, dkn, dv, dcr, *recv = _fox_attn_bwd(qn, ka, kb, vb, c, do, mshift, delta, slabs, T, D, "fox_attn_bwd")
    for n in LATE:
        del grads[n]
    dc = jnp.pad(dcr[:, :2, :].reshape(2 * P, Lp).T, ((0, 0), (0, LANES - 2 * P)))
    Tp = T // 2 if T == ROW_TILE else T
    dproj0, sm = _fox_prep_bwd(proj0, bf, qg, kg, dqn, dkn, dv, dgate, dc, Tp, D, "fox_prep_bwd")
    g_fin = _mm(hn0t, dproj0, "nn", F32, "fox_in_dw", tm=D, tk=_tile(Lp, 1408))[:, :4 * D + n_heads][None]
    dh_head, grad_x, d_an0, r_fin = _in_proj_dx(dproj0, w_fin, h0, row(small["attn_norm"][0]), dh1, "fox_in_dx",
                                                xchg=[_full_to_slabs(g_fin, "fox_w_in").astype(BF16)], split_head=True)

    grads["meta_tokens"] = dh_head[N_PAD:]
    grads["attn_norm"] = jnp.concatenate([d_an0, d_an1], axis=0)
    grads["ffn_norm"] = jnp.concatenate([d_fn0, d_fn1], axis=0)
    grads["final_norm"] = d_final[0]
    grads["fox_b_f"] = sm[0:1, :n_heads]
    grads["fox_q_norm"] = sm[1:2, :FOX_DH] + sm[1:2, FOX_DH:]
    grads["fox_k_norm"] = sm[2:3, :FOX_DH] + sm[2:3, FOX_DH:]
    grads["hgrn_lower_bounds"] = d_lb[0:2]
    grads["hgrn_g_norm"] = d_gg[0:1]
    return loss_blk[0, 0], grad_x, grads, dict(zip(LATE, recv), fox_w_in=r_fin)


def kernel(x, meta_tokens, attn_norm, ffn_norm, final_norm, fox_w_in, fox_b_f, fox_q_norm, fox_k_norm, fox_w_out, hgrn_w_in, hgrn_lower_bounds, hgrn_g_norm, hgrn_w_out, ffn_w_in, ffn_w_out, loss_target, m_meta_tokens, m_attn_norm, m_ffn_norm, m_final_norm, m_fox_w_in, m_fox_b_f, m_fox_q_norm, m_fox_k_norm, m_fox_w_out, m_hgrn_w_in, m_hgrn_lower_bounds, m_hgrn_g_norm, m_hgrn_w_out, m_ffn_w_in, m_ffn_w_out, v_meta_tokens, v_attn_norm, v_ffn_norm, v_final_norm, v_fox_w_in, v_fox_b_f, v_fox_q_norm, v_fox_k_norm, v_fox_w_out, v_hgrn_w_in, v_hgrn_lower_bounds, v_hgrn_g_norm, v_hgrn_w_out, v_ffn_w_in, v_ffn_w_out):
    w = dict(meta_tokens=meta_tokens, attn_norm=attn_norm, ffn_norm=ffn_norm, final_norm=final_norm,
             fox_w_in=fox_w_in, fox_b_f=fox_b_f, fox_q_norm=fox_q_norm, fox_k_norm=fox_k_norm,
             fox_w_out=fox_w_out, hgrn_w_in=hgrn_w_in, hgrn_lower_bounds=hgrn_lower_bounds,
             hgrn_g_norm=hgrn_g_norm, hgrn_w_out=hgrn_w_out, ffn_w_in=ffn_w_in, ffn_w_out=ffn_w_out)
    m = dict(meta_tokens=m_meta_tokens, attn_norm=m_attn_norm, ffn_norm=m_ffn_norm, final_norm=m_final_norm,
             fox_w_in=m_fox_w_in, fox_b_f=m_fox_b_f, fox_q_norm=m_fox_q_norm, fox_k_norm=m_fox_k_norm,
             fox_w_out=m_fox_w_out, hgrn_w_in=m_hgrn_w_in, hgrn_lower_bounds=m_hgrn_lower_bounds,
             hgrn_g_norm=m_hgrn_g_norm, hgrn_w_out=m_hgrn_w_out, ffn_w_in=m_ffn_w_in, ffn_w_out=m_ffn_w_out)
    v = dict(meta_tokens=v_meta_tokens, attn_norm=v_attn_norm, ffn_norm=v_ffn_norm, final_norm=v_final_norm,
             fox_w_in=v_fox_w_in, fox_b_f=v_fox_b_f, fox_q_norm=v_fox_q_norm, fox_k_norm=v_fox_k_norm,
             fox_w_out=v_fox_w_out, hgrn_w_in=v_hgrn_w_in, hgrn_lower_bounds=v_hgrn_lower_bounds,
             hgrn_g_norm=v_hgrn_g_norm, hgrn_w_out=v_hgrn_w_out, ffn_w_in=v_ffn_w_in, ffn_w_out=v_ffn_w_out)
    axes = ("x", "y", "c")
    small_shapes = [w[n].shape for n in SMALL]

    (g_meta,) = _gather_two_level([w["meta_tokens"].astype(F32)], "gather_meta")
    loss_local, grad_x, grads, recv = _local_step(
        x[0], loss_target[0], _gathered_to_full(g_meta, "meta_tokens"), w["fox_w_in"].astype(BF16),
        {n: w[n].astype(BF16) for n in LATE}, {n: w[n] for n in SMALL})
    loss = lax.psum(loss_local, axes)

    r_meta, r_small = _exchange([_full_to_slabs(grads["meta_tokens"], "meta_tokens"),
                                 _pack_small([grads[n] for n in SMALL])], [True, False], "scatter_grads")
    recv.update(meta_tokens=r_meta)

    res = {n: _adamw(recv[n], w[n], m[n], v[n], "adamw_" + n) for n in BIG}
    sml = _adamw(r_small, _pack_small([w[n] for n in SMALL]), _pack_small([m[n] for n in SMALL]),
                 _pack_small([v[n] for n in SMALL]), "adamw_small")
    outs = []
    for k in range(4):
        d = {n: res[n][k] for n in BIG}
        d.update(zip(SMALL, _unpack_small(sml[k], small_shapes)))
        outs.extend(d[n] for n in WEIGHTS)
    return (loss, grad_x[None], *outs)
```

```python
import functools
import math

import jax
import jax.numpy as jnp
from jax import lax
from jax.experimental import pallas as pl
from jax.experimental.pallas import tpu as pltpu

F32 = jnp.float32
BF16 = jnp.bfloat16
EPS = 1e-6
N_META = 16
LANES = 128
HEAD_ROWS = 256
ROW_TILE = 768
N_PAD = HEAD_ROWS - N_META
FOX_DH = 64
HGRN_CHUNK = 64
HGRN_TILE = 384
N_DEV = 8
NEG = -1e30
PAD_SHIFT = 1e4
VMEM_LIMIT = 56 * 1024 * 1024
HI = lax.Precision.HIGHEST
LOG2E = 1.0 / math.log(2.0)
LN2 = math.log(2.0)

ADAM_LR = 0.001
ADAM_B1 = 0.9
ADAM_B2 = 0.999
ADAM_EPS = 1e-08
ADAM_WD = 0.01
ADAM_STEP = 10

BIG = ("meta_tokens", "fox_w_in", "fox_w_out", "hgrn_w_in", "hgrn_w_out", "ffn_w_in", "ffn_w_out")
SMALL = ("attn_norm", "ffn_norm", "final_norm", "fox_b_f", "fox_q_norm", "fox_k_norm",
         "hgrn_lower_bounds", "hgrn_g_norm")
WEIGHTS = ("meta_tokens", "attn_norm", "ffn_norm", "final_norm", "fox_w_in", "fox_b_f", "fox_q_norm",
           "fox_k_norm", "fox_w_out", "hgrn_w_in", "hgrn_lower_bounds", "hgrn_g_norm", "hgrn_w_out",
           "ffn_w_in", "ffn_w_out")
COL_SHARDED = ("meta_tokens", "fox_w_in", "hgrn_w_in", "ffn_w_in")
LATE = ("fox_w_out", "hgrn_w_in", "hgrn_w_out", "ffn_w_in", "ffn_w_out")


def _params(sem=None):
    return pltpu.CompilerParams(dimension_semantics=sem, vmem_limit_bytes=VMEM_LIMIT)


def _tile(n, cap):
    best = None
    for t in range(LANES, min(n, cap) + 1, LANES):
        if n % t == 0:
            best = t
    assert best is not None, (n, cap)
    return best


def _row_chunk(n, cap, mult=8):
    best = n
    for t in range(mult, min(n, cap) + 1, mult):
        if n % t == 0:
            best = t
    return best


def _dg(a, b, ca, cb):
    return lax.dot_general(a.astype(BF16), b.astype(BF16), (((ca,), (cb,)), ((), ())),
                           preferred_element_type=F32)


@jax.custom_vjp
def _d_nn(a, b):
    return _dg(a, b, 1, 0)


@jax.custom_vjp
def _d_nt(a, b):
    return _dg(a, b, 1, 1)


@jax.custom_vjp
def _d_tn(a, b):
    return _dg(a, b, 0, 0)


_d_nn.defvjp(lambda a, b: (_d_nn(a, b), (a, b)), lambda r, g: (_d_nt(g, r[1]), _d_tn(r[0], g)))
_d_nt.defvjp(lambda a, b: (_d_nt(a, b), (a, b)), lambda r, g: (_d_nn(g, r[1]), _d_tn(g, r[0])))
_d_tn.defvjp(lambda a, b: (_d_tn(a, b), (a, b)), lambda r, g: (_d_nt(r[1], g), _d_nn(r[0], g)))


def _log_sigmoid(x):
    return jnp.minimum(x, 0.0) - jnp.log1p(jnp.exp(-jnp.abs(x)))


def _rms(x, g):
    return x * lax.rsqrt(jnp.mean(x * x, axis=-1, keepdims=True) + EPS) * g


def _mm(a, b, mode, out_dtype, name, res=None, tm=None, tn=None, tk=None):
    assert a.dtype == BF16 and b.dtype == BF16, (name, a.dtype, b.dtype)
    if mode == "nn":
        (M, K), N = a.shape, b.shape[1]
    elif mode == "nt":
        (M, K), N = a.shape, b.shape[0]
    else:
        (K, M), N = a.shape, b.shape[1]
    if mode == "nn":
        tm, tn, tk = tm or _tile(M, ROW_TILE), tn or _tile(N, 1408), tk or _tile(K, 2816)
    elif mode == "nt":
        tm, tn, tk = tm or _tile(M, ROW_TILE if K <= 2048 else ROW_TILE // 2), tn or N, tk or K
    else:
        tm, tn, tk = tm or _tile(M, 1408), tn or _tile(N, 1408), tk or _tile(K, ROW_TILE)
    nk = K // tk
    if mode == "tn":
        a_spec = pl.BlockSpec((tk, tm), lambda j, i, k: (k, i))
        dims = (((0,), (0,)), ((), ()))
    else:
        a_spec = pl.BlockSpec((tm, tk), lambda j, i, k: (i, k))
        dims = (((1,), (1 if mode == "nt" else 0,)), ((), ()))
    if mode == "nt":
        b_spec = pl.BlockSpec((tn, tk), lambda j, i, k: (j, k))
    else:
        b_spec = pl.BlockSpec((tk, tn), lambda j, i, k: (k, j))

    o_spec = pl.BlockSpec((tm, tn), lambda j, i, k: (i, j))

    def body(a_ref, b_ref, *rest):
        r_ref = rest[0] if res is not None else None
        o_ref, acc_ref = rest[-2:]
        k = pl.program_id(2)

        @pl.when(k == 0)
        def _():
            acc_ref[...] = jnp.zeros_like(acc_ref)

        acc_ref[...] += lax.dot_general(a_ref[...], b_ref[...], dims, preferred_element_type=F32)

        @pl.when(k == nk - 1)
        def _():
            out = acc_ref[...] if r_ref is None else acc_ref[...] + r_ref[...]
            o_ref[...] = out.astype(out_dtype)

    return pl.pallas_call(
        body, name=name, grid=(N // tn, M // tm, nk),
        in_specs=[a_spec, b_spec] + ([o_spec] if res is not None else []),
        out_specs=o_spec,
        out_shape=jax.ShapeDtypeStruct((M, N), out_dtype),
        scratch_shapes=[pltpu.VMEM((tm, tn), F32)],
        compiler_params=_params(("parallel", "parallel", "arbitrary")),
    )(a, b, *([res] if res is not None else []))


def _out_proj_fwd(a, w, res, gain, name):
    Lp, K = a.shape
    D = w.shape[1]
    tm = _tile(Lp, ROW_TILE)

    def body(a_ref, w_ref, r_ref, g_ref, h_ref, hn_ref, hnt_ref):
        h = jnp.dot(a_ref[...], w_ref[...], preferred_element_type=F32) + r_ref[...]
        h_ref[...] = h
        hn = _rms(h, g_ref[...])
        hn_ref[...] = hn.astype(BF16)
        hnt_ref[...] = hn.T.astype(BF16)

    row = pl.BlockSpec((tm, D), lambda i: (i, 0))
    return pl.pallas_call(
        body, name=name, grid=(Lp // tm,),
        in_specs=[pl.BlockSpec((tm, K), lambda i: (i, 0)), pl.BlockSpec((K, D), lambda i: (0, 0)), row,
                  pl.BlockSpec((1, D), lambda i: (0, 0))],
        out_specs=[row, row, pl.BlockSpec((D, tm), lambda i: (0, i))],
        out_shape=[jax.ShapeDtypeStruct((Lp, D), F32), jax.ShapeDtypeStruct((Lp, D), BF16),
                   jax.ShapeDtypeStruct((D, Lp), BF16)],
        compiler_params=_params(("parallel",)),
    )(a, w, res, gain)


def _in_proj_dx(dy, w, x, gain, dres, name, xchg=(), split_head=False):
    Lp, N = dy.shape
    D = w.shape[0]
    tm = HEAD_ROWS if split_head else _tile(Lp, ROW_TILE // 2)
    nt = Lp // tm
    nx = len(xchg)

    def body(dy_ref, w_ref, x_ref, g_ref, dr_ref, *rest):
        x_in, (dx_ref, dxb_ref, dg_ref), x_out, sems = rest[:nx], rest[nx:nx + 3], rest[nx + 3:2 * nx + 3], rest[2 * nx + 3:]

        @pl.when(pl.program_id(0) == 0)
        def _():
            dg_ref[...] = jnp.zeros_like(dg_ref)
            if nx:
                for cp in _xchg_copies(x_in, x_out, [True] * nx, sems):
                    cp.start()

        dhn = lax.dot_general(dy_ref[...], w_ref[...], (((1,), (1,)), ((), ())), preferred_element_type=F32)
        _, vjp = jax.vjp(_rms, x_ref[...], g_ref[...])
        dx, dg = vjp(dhn)
        dx = dx + dr_ref[...]
        if split_head:
            @pl.when(pl.program_id(0) == 0)
            def _():
                dx_ref[...] = dx

            @pl.when(pl.program_id(0) > 0)
            def _():
                dxb_ref[...] = dx
        else:
            dx_ref[...] = dx
            dxb_ref[...] = dx.astype(BF16)
        dg_ref[...] += dg

        if nx:
            @pl.when(pl.program_id(0) == nt - 1)
            def _():
                for cp in _xchg_copies(x_in, x_out, [True] * nx, sems):
                    cp.wait()

    row = pl.BlockSpec((tm, D), lambda i: (i, 0))
    vec = pl.BlockSpec((1, D), lambda i: (0, 0))
    HBM = pl.BlockSpec(memory_space=pltpu.HBM)
    return pl.pallas_call(
        body, name=name, grid=(nt,),
        in_specs=[pl.BlockSpec((tm, N), lambda i: (i, 0)), pl.BlockSpec((D, N), lambda i: (0, 0)), row, vec, row]
        + [HBM] * nx,
        out_specs=([pl.BlockSpec((tm, D), lambda i: (0, 0)), pl.BlockSpec((tm, D), lambda i: (jnp.maximum(i - 1, 0), 0))]
                   if split_head else [row, row]) + [vec] + [HBM] * nx,
        out_shape=([jax.ShapeDtypeStruct((tm, D), F32), jax.ShapeDtypeStruct((Lp - tm, D), F32)] if split_head else
                   [jax.ShapeDtypeStruct((Lp, D), F32), jax.ShapeDtypeStruct((Lp, D), BF16)])
        + [jax.ShapeDtypeStruct((1, D), F32)] + _xchg_out_shapes(xchg, [True] * nx),
        scratch_shapes=_xchg_sems(nx) if nx else [],
        compiler_params=_params(("arbitrary",)),
    )(dy, w, x, gain, dres, *xchg)


def _embed_rms_fwd(x, meta, g, gather, name):
    S, D = x.shape
    TR = HEAD_ROWS
    Lp = S + TR
    ng = len(gather)

    def body(x_ref, m_ref, g_ref, *rest):
        g_in, (h_ref, o_ref, ot_ref), g_out, sems = rest[:ng], rest[ng:ng + 3], rest[ng + 3:2 * ng + 3], rest[2 * ng + 3:]
        i = pl.program_id(0)

        @pl.when(i == 0)
        def _():
            local, first = _gather2_first(g_in, g_out, sems)
            for cp in local + first:
                cp.start()

        @pl.when(i == 0)
        def _():
            h_ref[...] = jnp.zeros_like(h_ref)
            h_ref[N_PAD:, :] = m_ref[...]

        @pl.when(i > 0)
        def _():
            h_ref[...] = x_ref[...]

        y = _rms(h_ref[...], g_ref[...])
        o_ref[...] = y.astype(BF16)
        ot_ref[...] = y.T.astype(BF16)

        @pl.when(i == Lp // TR - 1)
        def _():
            _gather2_finish(g_in, g_out, sems)

    row = pl.BlockSpec((TR, D), lambda i: (i, 0))
    HBM = pl.BlockSpec(memory_space=pltpu.HBM)
    return pl.pallas_call(
        body, name=name, grid=(Lp // TR,),
        in_specs=[pl.BlockSpec((TR, D), lambda i: (jnp.maximum(i - 1, 0), 0)),
                  pl.BlockSpec((N_META, D), lambda i: (0, 0)), pl.BlockSpec((1, D), lambda i: (0, 0))] + [HBM] * ng,
        out_specs=[row, row, pl.BlockSpec((D, TR), lambda i: (0, i))] + [HBM] * ng,
        out_shape=[jax.ShapeDtypeStruct((Lp, D), F32), jax.ShapeDtypeStruct((Lp, D), BF16),
                   jax.ShapeDtypeStruct((D, Lp), BF16)] + _xchg_out_shapes(gather, [False] * ng),
        scratch_shapes=_xchg_sems(ng),
        compiler_params=_params(("arbitrary",)),
    )(x, meta, g, *gather)


def _swiglu(gate, up):
    return gate * jax.nn.sigmoid(gate) * up


def _ffn_in_fwd(hf, w_in, name):
    Lp, D = hf.shape
    F = w_in.shape[1] // 2
    tm = _tile(Lp, ROW_TILE)
    tn = _tile(F, 1408)
    nj = F // tn

    def body(a_ref, bg_ref, bu_ref, g_ref, u_ref, act_ref):
        a = a_ref[...]
        g = jnp.dot(a, bg_ref[...], preferred_element_type=F32)
        u = jnp.dot(a, bu_ref[...], preferred_element_type=F32)
        g_ref[...] = g.astype(BF16)
        u_ref[...] = u.astype(BF16)
        act_ref[...] = _swiglu(g, u).astype(BF16)

    tile = pl.BlockSpec((tm, tn), lambda j, i: (i, j))
    return pl.pallas_call(
        body, name=name, grid=(nj, Lp // tm),
        in_specs=[pl.BlockSpec((tm, D), lambda j, i: (i, 0)), pl.BlockSpec((D, tn), lambda j, i: (0, j)),
                  pl.BlockSpec((D, tn), lambda j, i: (0, nj + j))],
        out_specs=[tile, tile, tile],
        out_shape=[jax.ShapeDtypeStruct((Lp, F), BF16)] * 3,
        compiler_params=_params(("parallel", "parallel")),
    )(hf, w_in, w_in)


def _ffn_out_dx(dhb, w_out, g, u, name):
    Lp, D = dhb.shape
    F = w_out.shape[0]
    tm = HEAD_ROWS

    def body(a_ref, b_ref, g_ref, u_ref, o_ref):
        dact = lax.dot_general(a_ref[...], b_ref[...], (((1,), (1,)), ((), ())), preferred_element_type=F32)
        _, vjp = jax.vjp(_swiglu, g_ref[...].astype(F32), u_ref[...].astype(F32))
        dg, du = vjp(dact)
        o_ref[:, :F] = dg.astype(BF16)
        o_ref[:, F:] = du.astype(BF16)

    wide = pl.BlockSpec((tm, F), lambda i: (i, 0))
    return pl.pallas_call(
        body, name=name, grid=(Lp // tm,),
        in_specs=[pl.BlockSpec((tm, D), lambda i: (i, 0)), pl.BlockSpec((F, D), lambda i: (0, 0)), wide, wide],
        out_specs=pl.BlockSpec((tm, 2 * F), lambda i: (i, 0)),
        out_shape=jax.ShapeDtypeStruct((Lp, 2 * F), BF16),
        compiler_params=_params(("parallel",)),
    )(dhb, w_out, g, u)


def _final_loss(h, g, target, name):
    Lp, D = h.shape
    TR = HEAD_ROWS

    def loss_fn(hh, gg, tt):
        err = _rms(hh, gg) - tt
        return 0.5 * jnp.sum(jnp.mean(err * err, axis=-1))

    def body(h_ref, g_ref, t_ref, loss_ref, dh_ref, dhb_ref, dg_ref):
        i = pl.program_id(0)

        @pl.when(i == 0)
        def _():
            loss_ref[...] = jnp.zeros_like(loss_ref)
            dg_ref[...] = jnp.zeros_like(dg_ref)
            dh_ref[...] = jnp.zeros_like(dh_ref)
            dhb_ref[...] = jnp.zeros_like(dhb_ref)

        @pl.when(i > 0)
        def _():
            val, vjp = jax.vjp(lambda hh, gg: loss_fn(hh, gg, t_ref[...]), h_ref[...], g_ref[...])
            dh, dg = vjp(jnp.ones((), F32))
            dh_ref[...] = dh
            dhb_ref[...] = dh.astype(BF16)
            dg_ref[...] += dg
            loss_ref[...] += val

    row = pl.BlockSpec((TR, D), lambda i: (i, 0))
    return pl.pallas_call(
        body, name=name, grid=(Lp // TR,),
        in_specs=[row, pl.BlockSpec((1, D), lambda i: (0, 0)),
                  pl.BlockSpec((TR, D), lambda i: (jnp.maximum(i - 1, 0), 0))],
        out_specs=[pl.BlockSpec((8, LANES), lambda i: (0, 0)), row, row, pl.BlockSpec((1, D), lambda i: (0, 0))],
        out_shape=[jax.ShapeDtypeStruct((8, LANES), F32), jax.ShapeDtypeStruct((Lp, D), F32),
                   jax.ShapeDtypeStruct((Lp, D), BF16), jax.ShapeDtypeStruct((1, D), F32)],
        compiler_params=_params(("arbitrary",)),
    )(h, g, target)


def _lane_lo():
    return lax.broadcasted_iota(jnp.int32, (1, LANES), 1) < FOX_DH


def _headnorm(x, g, scale):
    lo = _lane_lo()
    x2 = x * x
    s0 = jnp.sum(jnp.where(lo, x2, 0.0), axis=-1, keepdims=True)
    s1 = jnp.sum(jnp.where(lo, 0.0, x2), axis=-1, keepdims=True)
    r = jnp.where(lo, lax.rsqrt(s0 / FOX_DH + EPS), lax.rsqrt(s1 / FOX_DH + EPS))
    return x * r * g * scale


AUG = 3


def _split3(x):
    hi = x.astype(BF16).astype(F32)
    mid = (x - hi).astype(BF16).astype(F32)
    return hi, mid, x - hi - mid


def _fox_prep_fwd(proj, bf, qg, kg, T, D, name):
    Lp = proj.shape[0]
    nb = D // LANES
    scale = FOX_DH ** -0.5 * LOG2E

    def body(q_ref, k_ref, v_ref, fl_ref, bf_ref, qg_ref, kg_ref, qn_ref, ka_ref, kb_ref, vb_ref, c_ref, carry_ref):
        i = pl.program_id(0)

        @pl.when(i == 0)
        def _():
            carry_ref[...] = jnp.zeros_like(carry_ref)

        vb_ref[...] = v_ref[...].astype(BF16)
        log_f = _log_sigmoid(fl_ref[...] + bf_ref[...])
        row = lax.broadcasted_iota(jnp.int32, (T, T), 0)
        col = lax.broadcasted_iota(jnp.int32, (T, T), 1)
        tri = (col <= row).astype(F32)
        c = jnp.dot(tri, log_f, precision=HI, preferred_element_type=F32) + carry_ref[...]
        c2 = c * LOG2E
        c_ref[...] = c2
        last = lax.broadcasted_iota(jnp.int32, (T, 1), 0) == T - 1
        carry_ref[...] = jnp.sum(jnp.where(last, c, 0.0), axis=0, keepdims=True)

        is_pad = (i * T + lax.broadcasted_iota(jnp.int32, (T, 1), 0)) < N_PAD
        negc = jnp.where(is_pad, -PAD_SHIFT, -c2)
        lane = lax.broadcasted_iota(jnp.int32, (1, LANES), 1)
        for b in range(nb):
            sl = slice(b * LANES, (b + 1) * LANES)
            qn_ref[:, sl] = _headnorm(q_ref[:, sl], qg_ref[...], scale).astype(BF16)
            kn = _headnorm(k_ref[:, sl], kg_ref[...], 1.0)
            ka = jnp.where(lane < FOX_DH, kn, 0.0)
            kb = jnp.where(lane < FOX_DH, 0.0, kn)
            for n, (pa, pb) in enumerate(zip(_split3(_pick_col(negc, 2 * b)), _split3(_pick_col(negc, 2 * b + 1)))):
                ka = jnp.where(lane == FOX_DH + n, pa, ka)
                kb = jnp.where(lane == n, pb, kb)
            ka_ref[:, sl] = ka.astype(BF16)
            kb_ref[:, sl] = kb.astype(BF16)

    wide = lambda j: pl.BlockSpec((T, D), lambda i: (i, j))
    vec = pl.BlockSpec((1, LANES), lambda i: (0, 0))
    return pl.pallas_call(
        body, name=name, grid=(Lp // T,),
        in_specs=[wide(0), wide(1), wide(2), pl.BlockSpec((T, LANES), lambda i: (i, 4 * nb)), vec, vec, vec],
        out_specs=[wide(0), wide(0), wide(0), wide(0), pl.BlockSpec((T, LANES), lambda i: (i, 0))],
        out_shape=[jax.ShapeDtypeStruct((Lp, D), BF16)] * 4 + [jax.ShapeDtypeStruct((Lp, LANES), F32)],
        scratch_shapes=[pltpu.VMEM((1, LANES), F32)],
        compiler_params=_params(("arbitrary",)),
    )(proj, proj, proj, proj, bf, qg, kg)


def _fox_prep_bwd(proj, bf, qg, kg, dqn, dkn, dv, dgate, dc, T, D, name):
    Lp = proj.shape[0]
    nb = D // LANES
    nt = Lp // T
    scale = FOX_DH ** -0.5 * LOG2E

    def body(q_ref, k_ref, fl_ref, bf_ref, qg_ref, kg_ref, dqn_ref, dkn_ref, dv_ref, dgate_ref, dc_ref,
             dproj_ref, sm_ref, carry_ref):
        @pl.when(pl.program_id(0) == 0)
        def _():
            carry_ref[...] = jnp.zeros_like(carry_ref)
            sm_ref[...] = jnp.zeros_like(sm_ref)

        dqg = jnp.zeros((1, LANES), F32)
        dkg = jnp.zeros((1, LANES), F32)
        for b in range(nb):
            sl = slice(b * LANES, (b + 1) * LANES)
            _, vjp = jax.vjp(lambda x, g: _headnorm(x, g, scale), q_ref[:, sl], qg_ref[...])
            dx, dg = vjp(dqn_ref[:, sl] * LN2)
            dproj_ref[:, sl] = dx.astype(BF16)
            dqg = dqg + dg
            _, vjp = jax.vjp(lambda x, g: _headnorm(x, g, 1.0), k_ref[:, sl], kg_ref[...])
            dx, dg = vjp(dkn_ref[:, sl] * LN2)
            dproj_ref[:, D + b * LANES:D + (b + 1) * LANES] = dx.astype(BF16)
            dkg = dkg + dg
        dproj_ref[:, 2 * D:3 * D] = dv_ref[...].astype(BF16)
        dproj_ref[:, 3 * D:4 * D] = dgate_ref[...]
        dcv = dc_ref[...]
        row = lax.broadcasted_iota(jnp.int32, (T, T), 0)
        col = lax.broadcasted_iota(jnp.int32, (T, T), 1)
        triu = (col >= row).astype(F32)
        dlogf = jnp.dot(triu, dcv, precision=HI, preferred_element_type=F32) + carry_ref[...]
        carry_ref[...] += jnp.sum(dcv, axis=0, keepdims=True)
        _, vjp = jax.vjp(_log_sigmoid, fl_ref[...] + bf_ref[...])
        (dfl,) = vjp(dlogf)
        dproj_ref[:, 4 * D:] = dfl.astype(BF16)
        sm_ref[0:1, :] += jnp.sum(dfl, axis=0, keepdims=True)
        sm_ref[1:2, :] += dqg
        sm_ref[2:3, :] += dkg

    wide = lambda j: pl.BlockSpec((T, D), lambda i: (nt - 1 - i, j))
    narrow = lambda j: pl.BlockSpec((T, LANES), lambda i: (nt - 1 - i, j))
    vec = pl.BlockSpec((1, LANES), lambda i: (0, 0))
    return pl.pallas_call(
        body, name=name, grid=(nt,),
        in_specs=[wide(0), wide(1), narrow(4 * nb), vec, vec, vec, wide(0), wide(0), wide(0), wide(0), narrow(0)],
        out_specs=[pl.BlockSpec((T, 4 * D + LANES), lambda i: (nt - 1 - i, 0)), pl.BlockSpec((8, LANES), lambda i: (0, 0))],
        out_shape=[jax.ShapeDtypeStruct((Lp, 4 * D + LANES), BF16), jax.ShapeDtypeStruct((8, LANES), F32)],
        scratch_shapes=[pltpu.VMEM((1, LANES), F32)],
        compiler_params=_params(("arbitrary",)),
    )(proj, proj, proj, bf, qg, kg, dqn, dkn, dv, dgate, dc)


def _fox_q_operands(q):
    lane = lax.broadcasted_iota(jnp.int32, (1, LANES), 1)
    zero, one = jnp.zeros_like(q), jnp.ones_like(q)
    return (jnp.where(lane < FOX_DH, q, jnp.where(lane < FOX_DH + AUG, one, zero)),
            jnp.where(lane < FOX_DH, jnp.where(lane < AUG, one, zero), q))


def _fox_mask(q0, nq, k0, nk):
    qpos = q0 + lax.broadcasted_iota(jnp.int32, (nq, 1), 0)
    kpos = k0 + lax.broadcasted_iota(jnp.int32, (1, nk), 1)
    return (kpos <= qpos) & ((kpos >= N_PAD) | (qpos < N_PAD))


def _diag_split(T):
    return 512 if T == 768 else T // 2


def _rows_joined(head_rows, tail_rows, n_head):
    return jnp.concatenate([head_rows[:n_head], tail_rows], axis=0)


def _pick_col(blk, idx):
    lane = lax.broadcasted_iota(jnp.int32, (1, LANES), 1)
    return jnp.sum(jnp.where(lane == idx, blk, 0.0), axis=1, keepdims=True)


def _split_halves(blk):
    lo = _lane_lo()
    return (jnp.max(jnp.where(lo, blk, -jnp.inf), axis=1, keepdims=True),
            jnp.max(jnp.where(lo, -jnp.inf, blk), axis=1, keepdims=True))


def _fox_attn_fwd(qn, ka, kb, vb, c, proj, xchg, T, D, name):
    Lp = qn.shape[0]
    P = D // LANES
    nt = Lp // T
    nx = len(xchg)

    def body(q_ref, ka_ref, kb_ref, v_ref, c_ref, g_ref, *rest):
        x_in, (o_ref, og_ref, m_ref, li_ref), x_out, sems = rest[:nx], rest[nx:nx + 4], rest[nx + 4:2 * nx + 4], rest[2 * nx + 4:]
        p = pl.program_id(0)
        i = pl.program_id(1)

        @pl.when((p == 0) & (i == 0))
        def _():
            for cp in _xchg_copies(x_in, x_out, [False] * nx, sems):
                cp.start()

        lo = _lane_lo()
        q = q_ref[...]
        qh = _fox_q_operands(q)
        cblk = c_ref[...]
        cq = tuple(_pick_col(cblk, 2 * p + h) for h in (0, 1))
        one = jnp.ones_like(q)

        def block(r0, nr, k0, nk, carry, masked):
            kj = (ka_ref[pl.ds(k0, nk), :], kb_ref[pl.ds(k0, nk), :])
            vj = v_ref[pl.ds(k0, nk), :]
            vh = (jnp.where(lo, vj, one[:nk]), jnp.where(lo, one[:nk], vj))
            mask = _fox_mask(i * T + r0, nr, k0, nk) if masked else None
            out = []
            for h in (0, 1):
                m, acc = carry[h]
                cqr = cq[h][r0:r0 + nr]
                t = lax.dot_general(qh[h][r0:r0 + nr], kj[h], (((1,), (1,)), ((), ())), preferred_element_type=F32)
                if masked:
                    t = jnp.where(mask, t, NEG)
                m_new = jnp.ceil(jnp.maximum(m, cqr + jnp.max(t, axis=1, keepdims=True)))
                pr = jnp.exp2(t + (cqr - m_new)).astype(BF16)
                acc = jnp.exp2(m - m_new) * acc + jnp.dot(pr, vh[h], preferred_element_type=F32)
                out.append((m_new, acc))
            return tuple(out)

        init = tuple((jnp.full((T, 1), NEG, F32), jnp.zeros((T, LANES), F32)) for _ in (0, 1))
        kw = 256 if T % 256 == 0 else T

        def parts(j, cr):
            for s in range(T // kw):
                cr = block(0, T, pl.multiple_of(j * T + s * kw, LANES), kw, cr, False)
            return cr

        carry = lax.fori_loop(0, i, parts, init)
        (m0, a0), (m1, a1) = block(0, T, pl.multiple_of(i * T, LANES), T, carry, True)
        l0 = pltpu.roll(a0, FOX_DH, 1)
        l1 = pltpu.roll(a1, FOX_DH, 1)
        o = jnp.where(lo, a0 / l0, a1 / l1)
        o_ref[...] = o
        m_ref[...] = jnp.where(lo, m0, m1)
        li_ref[...] = jnp.where(lo, 1.0 / l0, 1.0 / l1)
        og_ref[...] = (o * jax.nn.sigmoid(g_ref[...])).astype(BF16)

        @pl.when((p == P - 1) & (i == nt - 1))
        def _():
            for cp in _xchg_copies(x_in, x_out, [False] * nx, sems):
                cp.wait()

    tile = pl.BlockSpec((T, LANES), lambda p, i: (i, p))
    full = pl.BlockSpec((Lp, LANES), lambda p, i: (0, p))
    HBM = pl.BlockSpec(memory_space=pltpu.HBM)
    return pl.pallas_call(
        body, name=name, grid=(P, nt),
        in_specs=[tile, full, full, full, pl.BlockSpec((T, LANES), lambda p, i: (i, 0)),
                  pl.BlockSpec((T, LANES), lambda p, i: (i, 3 * P + p))] + [HBM] * nx,
        out_specs=[tile, tile, tile, tile] + [HBM] * nx,
        out_shape=[jax.ShapeDtypeStruct((Lp, D), F32), jax.ShapeDtypeStruct((Lp, D), BF16),
                   jax.ShapeDtypeStruct((Lp, D), F32), jax.ShapeDtypeStruct((Lp, D), F32)]
        + _xchg_out_shapes(xchg, [False] * nx),
        scratch_shapes=_xchg_sems(nx),
        compiler_params=_params(("arbitrary", "arbitrary")),
    )(qn, ka, kb, vb, c, proj, *xchg)


def _fox_out_dx(dhb, w_out, o, proj, linv, D, name):
    Lp = o.shape[0]
    tm = HEAD_ROWS

    def body(a_ref, w_ref, o_ref, g_ref, li_ref, do_ref, dg_ref, dl_ref):
        lo = _lane_lo()
        dog_all = lax.dot_general(a_ref[...], w_ref[...], (((1,), (1,)), ((), ())), preferred_element_type=F32)
        for b in range(D // LANES):
            sl = slice(b * LANES, (b + 1) * LANES)
            dog = dog_all[:, sl]
            sig = jax.nn.sigmoid(g_ref[:, sl])
            ov = o_ref[:, sl]
            do = (dog * sig * li_ref[:, sl]).astype(BF16)
            do_ref[:, sl] = do
            dg_ref[:, sl] = (dog * ov * sig * (1.0 - sig)).astype(BF16)
            t = do.astype(F32) * ov
            d0 = jnp.sum(jnp.where(lo, t, 0.0), axis=1, keepdims=True)
            d1 = jnp.sum(jnp.where(lo, 0.0, t), axis=1, keepdims=True)
            dl_ref[:, sl] = jnp.where(lo, d0, d1)

    row = pl.BlockSpec((tm, D), lambda i: (i, 0))
    return pl.pallas_call(
        body, name=name, grid=(Lp // tm,),
        in_specs=[row, pl.BlockSpec((D, D), lambda i: (0, 0)), row, pl.BlockSpec((tm, D), lambda i: (i, 3)), row],
        out_specs=[row, row, row],
        out_shape=[jax.ShapeDtypeStruct((Lp, D), BF16), jax.ShapeDtypeStruct((Lp, D), BF16),
                   jax.ShapeDtypeStruct((Lp, D), F32)],
        compiler_params=_params(("parallel",)),
    )(dhb, w_out, o, proj, linv)


def _fox_attn_bwd(qn, ka, kb, vb, c, do, mshift, delta, xchg, T, D, name):
    Lp = qn.shape[0]
    P = D // LANES
    nt = Lp // T
    nx = len(xchg)

    def body(q_ref, do_ref, m_ref, dl_ref, c_ref, ka_ref, kb_ref, v_ref, *rest):
        x_in, (dq_ref, dk_ref, dv_ref, dc_ref), x_out, sems = rest[:nx], rest[nx:nx + 4], rest[nx + 4:2 * nx + 4], rest[2 * nx + 4:]
        p = pl.program_id(0)
        i = pl.program_id(1)

        @pl.when((p == 0) & (i == 0))
        def _():
            for cp in _xchg_copies(x_in, x_out, [True] * nx, sems):
                cp.start()

        @pl.when(i == 0)
        def _():
            dk_ref[...] = jnp.zeros_like(dk_ref)
            dv_ref[...] = jnp.zeros_like(dv_ref)
            dc_ref[...] = jnp.zeros_like(dc_ref)

        lo = _lane_lo()
        q = q_ref[...]
        do = do_ref[...]
        zero = jnp.zeros_like(q)
        qh = _fox_q_operands(q)
        doh = (jnp.where(lo, do, zero), jnp.where(lo, zero, do))
        msh = _split_halves(m_ref[...])
        dlt = _split_halves(dl_ref[...])
        cblk = c_ref[...]
        shift = tuple(_pick_col(cblk, 2 * p + h) - msh[h] for h in (0, 1))

        def block(r0, nr, k0, nk, carry, masked):
            kj = (ka_ref[pl.ds(k0, nk), :], kb_ref[pl.ds(k0, nk), :])
            vj = v_ref[pl.ds(k0, nk), :]
            mask = _fox_mask(i * T + r0, nr, k0, nk) if masked else None
            qr, dor = q[r0:r0 + nr], do[r0:r0 + nr]
            dqs, dks, dvs = [], [], []
            for h in (0, 1):
                t = lax.dot_general(qh[h][r0:r0 + nr], kj[h], (((1,), (1,)), ((), ())), preferred_element_type=F32)
                if masked:
                    t = jnp.where(mask, t, NEG)
                pb = jnp.exp2(t + shift[h][r0:r0 + nr]).astype(BF16)
                dp = lax.dot_general(doh[h][r0:r0 + nr], vj, (((1,), (1,)), ((), ())), preferred_element_type=F32)
                ds = pb.astype(F32) * (dp - dlt[h][r0:r0 + nr])
                dsb = ds.astype(BF16)
                dqs.append(carry[h] + jnp.dot(dsb, kj[h], preferred_element_type=F32))
                dks.append(lax.dot_general(dsb, qr, (((0,), (0,)), ((), ())), preferred_element_type=F32))
                dvs.append(lax.dot_general(pb, dor, (((0,), (0,)), ((), ())), preferred_element_type=F32))
                dc_ref[0, h:h + 1, pl.ds(k0, nk)] += -jnp.sum(ds, axis=0, keepdims=True)
            dk_ref[pl.ds(k0, nk), :] += jnp.where(lo, dks[0], dks[1])
            dv_ref[pl.ds(k0, nk), :] += jnp.where(lo, dvs[0], dvs[1])
            return tuple(dqs)

        init = (jnp.zeros((T, LANES), F32), jnp.zeros((T, LANES), F32))
        carry = lax.fori_loop(0, i, lambda j, cr: block(0, T, pl.multiple_of(j * T, LANES), T, cr, False), init)
        kd = pl.multiple_of(i * T, LANES)
        ta = _diag_split(T)
        da = block(0, T, kd, ta, carry, True)
        db = block(ta, T - ta, pl.multiple_of(kd + ta, LANES), T - ta, tuple(d[ta:] for d in da), True)
        dq_ref[...] = jnp.where(lo, _rows_joined(da[0], db[0], ta), _rows_joined(da[1], db[1], ta))

        @pl.when((p == P - 1) & (i == nt - 1))
        def _():
            for cp in _xchg_copies(x_in, x_out, [True] * nx, sems):
                cp.wait()

    tile = pl.BlockSpec((T, LANES), lambda p, i: (i, p))
    full = pl.BlockSpec((Lp, LANES), lambda p, i: (0, p))
    HBM = pl.BlockSpec(memory_space=pltpu.HBM)
    return pl.pallas_call(
        body, name=name, grid=(P, nt),
        in_specs=[tile, tile, tile, tile, pl.BlockSpec((T, LANES), lambda p, i: (i, 0)), full, full, full]
        + [HBM] * nx,
        out_specs=[tile, full, full, pl.BlockSpec((1, 8, Lp), lambda p, i: (p, 0, 0))] + [HBM] * nx,
        out_shape=[jax.ShapeDtypeStruct((Lp, D), F32)] * 3 + [jax.ShapeDtypeStruct((P, 8, Lp), F32)]
        + _xchg_out_shapes(xchg, [True] * nx),
        scratch_shapes=_xchg_sems(nx),
        compiler_params=_params(("arbitrary", "arbitrary")),
    )(qn, do, mshift, delta, c, ka, kb, vb, *xchg)


def _scan_rows(x, reverse):
    C = x.shape[0]
    row = lax.broadcasted_iota(jnp.int32, (C, 1), 0)
    step = 1
    while step < C:
        if reverse:
            x = x + jnp.where(row < C - step, pltpu.roll(x, C - step, 0), 0.0)
        else:
            x = x + jnp.where(row >= step, pltpu.roll(x, step, 0), 0.0)
        step *= 2
    return x


@jax.custom_vjp
def _cumsum_rows(x):
    return _scan_rows(x, False)


_cumsum_rows.defvjp(lambda x: (_scan_rows(x, False), None), lambda _, g: (_scan_rows(g, True),))


def _hgrn_chunk(St, qr, z, vi, go, p0, p1, gg):
    C = qr.shape[0]
    lb = jax.nn.sigmoid(p1 - p0)
    a = jnp.log(lb)
    cc = jnp.log1p(-lb) + _log_sigmoid(z)
    log_f = jnp.maximum(a, cc) + jnp.log1p(jnp.exp(-jnp.abs(a - cc)))
    k = (1.0 - lb) * jax.nn.sigmoid(-z)
    q = qr * jax.nn.sigmoid(qr)
    row = lax.broadcasted_iota(jnp.int32, (C, C), 0)
    col = lax.broadcasted_iota(jnp.int32, (C, C), 1)
    causal = col <= row
    b = _cumsum_rows(log_f)
    mid = lax.broadcasted_iota(jnp.int32, (C, 1), 0) == C // 2 - 1
    r = jnp.sum(jnp.where(mid, b, 0.0), axis=0, keepdims=True)
    b_last = jnp.sum(log_f, axis=0, keepdims=True)
    attn = jnp.where(causal, _d_nt(q * jnp.exp(b - r), k * jnp.exp(r - b)), 0.0)
    o = _d_nn(attn, vi) + _d_nt(q * jnp.exp(b), St)
    St_new = St * jnp.exp(b_last) + _d_tn(vi, k * jnp.exp(b_last - b))
    og = _rms(o, gg) * (go * jax.nn.sigmoid(go))
    return St_new, og


def _hgrn_heads_per_step(H):
    return 8 if H % 8 == 0 else 4 if H % 4 == 0 else 1


def _hgrn_specs(T, W, nhb, rev_nt=None):
    if rev_nt is None:
        return [pl.BlockSpec((T, W), functools.partial(lambda hb, t, g: (t, g * nhb + hb), g=g)) for g in range(4)]
    return [pl.BlockSpec((T, W), functools.partial(lambda hb, t, g: (rev_nt - 1 - t, g * nhb + hb), g=g))
            for g in range(4)]


def _hgrn_fwd(proj, lbp, gg, T, name):
    Lp = proj.shape[0]
    D = proj.shape[1] // 4
    H = D // LANES
    hps = _hgrn_heads_per_step(H)
    W = hps * LANES
    nhb = H // hps
    nt = Lp // T
    ncc = T // HGRN_CHUNK

    def body(q_ref, z_ref, i_ref, go_ref, p_ref, gg_ref, og_ref, ss_ref, st_ref):
        @pl.when(pl.program_id(1) == 0)
        def _():
            st_ref[...] = jnp.zeros_like(st_ref)

        gain = gg_ref[...]

        def chunk(cidx, states):
            sl = pl.ds(pl.multiple_of(cidx * HGRN_CHUNK, HGRN_CHUNK), HGRN_CHUNK)
            new = []
            for hh in range(hps):
                ln = slice(hh * LANES, (hh + 1) * LANES)
                ss_ref[hh, cidx] = states[hh]
                St_new, og = _hgrn_chunk(states[hh], q_ref[sl, ln], z_ref[sl, ln], i_ref[sl, ln], go_ref[sl, ln],
                                         p_ref[0:1, ln], p_ref[1:2, ln], gain)
                og_ref[sl, ln] = og.astype(BF16)
                new.append(St_new)
            return tuple(new)

        assert ncc % 2 == 0
        states = lax.fori_loop(0, ncc // 2, lambda c2, st: chunk(2 * c2 + 1, chunk(2 * c2, st)),
                               tuple(st_ref[hh] for hh in range(hps)))
        for hh in range(hps):
            st_ref[hh] = states[hh]

    return pl.pallas_call(
        body, name=name, grid=(nhb, nt),
        in_specs=_hgrn_specs(T, W, nhb) + [pl.BlockSpec((2, W), lambda hb, t: (0, hb)),
                                           pl.BlockSpec((1, LANES), lambda hb, t: (0, 0))],
        out_specs=[pl.BlockSpec((T, W), lambda hb, t: (t, hb)),
                   pl.BlockSpec((hps, ncc, LANES, LANES), lambda hb, t: (hb, t, 0, 0))],
        out_shape=[jax.ShapeDtypeStruct((Lp, D), BF16),
                   jax.ShapeDtypeStruct((H, Lp // HGRN_CHUNK, LANES, LANES), F32)],
        scratch_shapes=[pltpu.VMEM((hps, LANES, LANES), F32)],
        compiler_params=_params(("parallel", "arbitrary")),
    )(proj, proj, proj, proj, lbp, gg)


def _hgrn_bwd(proj, lbp, gg, dog, ss, T, name):
    Lp = proj.shape[0]
    D = proj.shape[1] // 4
    H = D // LANES
    hps = _hgrn_heads_per_step(H)
    W = hps * LANES
    nhb = H // hps
    assert nhb == 1, "d proj is written as whole rows: every head in one grid step"
    nt = Lp // T
    ncc = T // HGRN_CHUNK

    def body(q_ref, z_ref, i_ref, go_ref, p_ref, gg_ref, dog_ref, ss_ref, dproj_ref, dp_ref, dgg_ref, dst_ref):
        hb = pl.program_id(0)
        t = pl.program_id(1)

        @pl.when(t == 0)
        def _():
            dst_ref[...] = jnp.zeros_like(dst_ref)
            dp_ref[...] = jnp.zeros_like(dp_ref)

        @pl.when((t == 0) & (hb == 0))
        def _():
            dgg_ref[...] = jnp.zeros_like(dgg_ref)

        gain = gg_ref[...]
        row0 = (nt - 1 - t) * T

        def chunk(cc, carry):
            dstates, dps, dgain_sum = carry
            cidx = ncc - 1 - cc
            r0 = pl.multiple_of(cidx * HGRN_CHUNK, HGRN_CHUNK)
            sl = pl.ds(r0, HGRN_CHUNK)
            real = (row0 + r0 + lax.broadcasted_iota(jnp.int32, (HGRN_CHUNK, 1), 0)) >= N_PAD
            new_d, new_p = [], []
            for hh in range(hps):
                ln = slice(hh * LANES, (hh + 1) * LANES)
                _, vjp = jax.vjp(_hgrn_chunk, ss_ref[hh, cidx], q_ref[sl, ln], z_ref[sl, ln], i_ref[sl, ln],
                                 go_ref[sl, ln], p_ref[0:1, ln], p_ref[1:2, ln], gain)
                dSt, dq, dz, di, dgo, dp0, dp1, dgain = vjp((dstates[hh], dog_ref[sl, ln]))
                for grp, dval in enumerate((dq, dz, di, dgo)):
                    dproj_ref[sl, grp * D + hh * LANES:grp * D + (hh + 1) * LANES] = (
                        jnp.where(real, dval, 0.0).astype(BF16))
                new_d.append(dSt)
                new_p.append((dps[hh][0] + dp0, dps[hh][1] + dp1))
                dgain_sum = dgain_sum + dgain
            return tuple(new_d), tuple(new_p), dgain_sum

        zero_row = jnp.zeros((1, LANES), F32)
        init = (tuple(dst_ref[hh] for hh in range(hps)), tuple((zero_row, zero_row) for _ in range(hps)), zero_row)
        dstates, dps, dgain_sum = lax.fori_loop(0, ncc, chunk, init)
        for hh in range(hps):
            ln = slice(hh * LANES, (hh + 1) * LANES)
            dst_ref[hh] = dstates[hh]
            dp_ref[0:1, ln] += dps[hh][0]
            dp_ref[1:2, ln] += dps[hh][1]
        dgg_ref[0:1, :] += dgain_sum

    rev = pl.BlockSpec((T, W), lambda hb, t: (nt - 1 - t, hb))
    return pl.pallas_call(
        body, name=name, grid=(nhb, nt),
        in_specs=_hgrn_specs(T, W, nhb, nt) + [pl.BlockSpec((2, W), lambda hb, t: (0, hb)),
                                               pl.BlockSpec((1, LANES), lambda hb, t: (0, 0)), rev,
                                               pl.BlockSpec((hps, ncc, LANES, LANES),
                                                            lambda hb, t: (hb, nt - 1 - t, 0, 0))],
        out_specs=[pl.BlockSpec((T, 4 * D), lambda hb, t: (nt - 1 - t, 0)), pl.BlockSpec((8, W), lambda hb, t: (0, hb)),
                   pl.BlockSpec((8, LANES), lambda hb, t: (0, 0))],
        out_shape=[jax.ShapeDtypeStruct((Lp, 4 * D), BF16), jax.ShapeDtypeStruct((8, D), F32),
                   jax.ShapeDtypeStruct((8, LANES), F32)],
        scratch_shapes=[pltpu.VMEM((hps, LANES, LANES), F32)],
        compiler_params=_params(("arbitrary", "arbitrary")),
    )(proj, proj, proj, proj, lbp, gg, dog, ss)


def _xchg_sems(n_arr):
    return [pltpu.SemaphoreType.DMA((n_arr * (N_DEV - 1),)), pltpu.SemaphoreType.DMA((n_arr * (N_DEV - 1),)),
            pltpu.SemaphoreType.DMA((n_arr,))]


def _xchg_copies(ins, outs, per_peer, sems):
    send_sems, recv_sems, local_sems = sems
    x, y, c = lax.axis_index("x"), lax.axis_index("y"), lax.axis_index("c")
    me = 4 * x + 2 * y + c
    copies = []
    for n in range(len(ins)):
        src = ins[n].at[me] if per_peer[n] else ins[n]
        copies.append(pltpu.make_async_copy(src, outs[n].at[me], local_sems.at[n]))
    for rel in range(1, N_DEV):
        fx, fy, fc = (rel >> 2) & 1, (rel >> 1) & 1, rel & 1
        px = 1 - x if fx else x
        py = 1 - y if fy else y
        pc = 1 - c if fc else c
        peer = 4 * px + 2 * py + pc
        for n in range(len(ins)):
            src = ins[n].at[peer] if per_peer[n] else ins[n]
            copies.append(pltpu.make_async_remote_copy(
                src_ref=src, dst_ref=outs[n].at[me],
                send_sem=send_sems.at[n * (N_DEV - 1) + rel - 1],
                recv_sem=recv_sems.at[n * (N_DEV - 1) + rel - 1],
                device_id=(px, py, pc), device_id_type=pl.DeviceIdType.MESH))
    return copies


def _xchg_out_shapes(arrays, per_peer):
    return [jax.ShapeDtypeStruct(a.shape if pp else (N_DEV,) + a.shape, a.dtype) for a, pp in zip(arrays, per_peer)]


def _exchange(arrays, per_peer, name):
    n_arr = len(arrays)
    HBM = pl.BlockSpec(memory_space=pltpu.HBM)

    def body(*refs):
        copies = _xchg_copies(refs[:n_arr], refs[n_arr:2 * n_arr], per_peer, refs[2 * n_arr:])
        for cp in copies:
            cp.start()
        for cp in copies:
            cp.wait()

    return pl.pallas_call(
        body, name=name,
        in_specs=[HBM] * n_arr, out_specs=[HBM] * n_arr, out_shape=_xchg_out_shapes(arrays, per_peer),
        scratch_shapes=_xchg_sems(n_arr),
    )(*arrays)


def _gather2_copy(ins, outs, sems, a, k, block, to, own=False):
    send_sems, recv_sems, _ = sems
    px, py, pc = block
    dst = outs[a].at[4 * px + 2 * py + pc]
    return pltpu.make_async_remote_copy(
        src_ref=ins[a] if own else dst, dst_ref=dst,
        send_sem=send_sems.at[a * (N_DEV - 1) + k], recv_sem=recv_sems.at[a * (N_DEV - 1) + k],
        device_id=to, device_id_type=pl.DeviceIdType.MESH)


def _gather2_first(ins, outs, sems):
    x, y, c = lax.axis_index("x"), lax.axis_index("y"), lax.axis_index("c")
    me, chips = (x, y, c), [(1 - x, y), (x, 1 - y), (1 - x, 1 - y)]
    n_arr = len(ins)
    local = [pltpu.make_async_copy(ins[a], outs[a].at[4 * x + 2 * y + c], sems[2].at[a]) for a in range(n_arr)]
    first = [_gather2_copy(ins, outs, sems, a, 0, me, (x, y, 1 - c), own=True) for a in range(n_arr)]
    first += [_gather2_copy(ins, outs, sems, a, 1 + j, me, (*chip, c), own=True)
              for j, chip in enumerate(chips) for a in range(n_arr)]
    return local, first


def _gather2_finish(ins, outs, sems):
    x, y, c = lax.axis_index("x"), lax.axis_index("y"), lax.axis_index("c")
    me, sibling, chips = (x, y, c), (x, y, 1 - c), [(1 - x, y), (x, 1 - y), (1 - x, 1 - y)]
    n_arr = len(ins)
    local, first = _gather2_first(ins, outs, sems)
    passed = []
    for j, chip in enumerate(chips):
        for a in range(n_arr):
            _gather2_copy(ins, outs, sems, a, 1 + j, (*chip, c), me).wait_recv()
            cp = _gather2_copy(ins, outs, sems, a, 4 + j, (*chip, c), sibling)
            cp.start()
            passed.append(cp)
    for a in range(n_arr):
        _gather2_copy(ins, outs, sems, a, 0, sibling, me).wait_recv()
        for j, chip in enumerate(chips):
            _gather2_copy(ins, outs, sems, a, 4 + j, (*chip, 1 - c), me).wait_recv()
    for cp in first + passed:
        cp.wait_send()
    for cp in local:
        cp.wait()


def _gather_two_level(arrays, name):
    n_arr = len(arrays)
    HBM = pl.BlockSpec(memory_space=pltpu.HBM)

    def body(*refs):
        ins, outs, sems = refs[:n_arr], refs[n_arr:2 * n_arr], refs[2 * n_arr:]
        local, first = _gather2_first(ins, outs, sems)
        for cp in local + first:
            cp.start()
        _gather2_finish(ins, outs, sems)

    return pl.pallas_call(
        body, name=name,
        in_specs=[HBM] * n_arr, out_specs=[HBM] * n_arr, out_shape=_xchg_out_shapes(arrays, [False] * n_arr),
        scratch_shapes=_xchg_sems(n_arr),
    )(*arrays)


ADAMW_VMEM_BUDGET = 36 * 1024 * 1024


def _adamw(recv, w, m, v, name):
    shape = w.shape
    C = shape[-1]
    R = math.prod(shape[:-1])
    lanes = -(-C // LANES) * LANES
    row_bytes = 2 * lanes * (N_DEV * recv.dtype.itemsize + 7 * 4)
    rc = _row_chunk(R, max(16, ADAMW_VMEM_BUDGET // row_bytes), 16 if recv.dtype == BF16 else 8)

    def body(r_ref, w_ref, m_ref, v_ref, g_ref, d_ref, mo_ref, vo_ref):
        g = r_ref[0].astype(F32)
        for s in range(1, N_DEV):
            g = g + r_ref[s].astype(F32)
        mn = ADAM_B1 * m_ref[...] + (1.0 - ADAM_B1) * g
        vn = ADAM_B2 * v_ref[...] + (1.0 - ADAM_B2) * (g * g)
        m_hat = mn / (1.0 - ADAM_B1 ** ADAM_STEP)
        v_hat = vn / (1.0 - ADAM_B2 ** ADAM_STEP)
        g_ref[...] = g
        d_ref[...] = -ADAM_LR * (m_hat / (jnp.sqrt(v_hat) + ADAM_EPS) + ADAM_WD * w_ref[...])
        mo_ref[...] = mn
        vo_ref[...] = vn

    row = pl.BlockSpec((rc, C), lambda i: (i, 0))
    outs = pl.pallas_call(
        body, name=name, grid=(R // rc,),
        in_specs=[pl.BlockSpec((N_DEV, rc, C), lambda i: (0, i, 0)), row, row, row],
        out_specs=[row] * 4,
        out_shape=[jax.ShapeDtypeStruct((R, C), F32)] * 4,
        compiler_params=_params(("parallel",)),
    )(recv.reshape(N_DEV, R, C), w.reshape(R, C), m.reshape(R, C), v.reshape(R, C))
    return [o.reshape(shape) for o in outs]


def _gathered_to_full(g, name):
    if name in COL_SHARDED:
        g = jnp.moveaxis(g, 0, -2)
        return g.reshape(g.shape[:-2] + (g.shape[-2] * g.shape[-1],))
    g = jnp.moveaxis(g, 0, -3)
    return g.reshape(g.shape[:-3] + (g.shape[-3] * g.shape[-2], g.shape[-1]))


def _full_to_slabs(full, name):
    if name in COL_SHARDED:
        f = full.reshape(full.shape[:-1] + (N_DEV, full.shape[-1] // N_DEV))
        return jnp.moveaxis(f, -2, 0)
    f = full.reshape(full.shape[:-2] + (N_DEV, full.shape[-2] // N_DEV, full.shape[-1]))
    return jnp.moveaxis(f, -3, 0)


def _pack_small(arrs):
    rows = []
    for a in arrs:
        flat = a.astype(F32).reshape(-1)
        pad = (-flat.shape[0]) % LANES
        rows.append(jnp.pad(flat, (0, pad)).reshape(-1, LANES))
    p = jnp.concatenate(rows, axis=0)
    return jnp.pad(p, ((0, (-p.shape[0]) % 8), (0, 0)))


def _unpack_small(packed, shapes):
    out, off = [], 0
    for shp in shapes:
        n = math.prod(shp)
        nr = -(-n // LANES)
        out.append(packed[off:off + nr].reshape(-1)[:n].reshape(shp))
        off += nr
    return out


def _local_step(x, target, meta, fox_in_shard, late, small):
    S, D = x.shape
    Lp = S + HEAD_ROWS
    T = ROW_TILE if Lp % ROW_TILE == 0 else HEAD_ROWS
    P = D // LANES
    row = lambda v: v.reshape(1, -1).astype(F32)

    n_heads = fox_in_shard.shape[-1] * N_DEV - 4 * D
    bf = jnp.pad(row(small["fox_b_f"]), ((0, 0), (0, LANES - small["fox_b_f"].size)))
    qg = jnp.tile(row(small["fox_q_norm"]), (1, 2))
    kg = jnp.tile(row(small["fox_k_norm"]), (1, 2))


    h0, hn0, hn0t, g_fin = _embed_rms_fwd(x, meta, row(small["attn_norm"][0]), [fox_in_shard], "rms0_fwd")
    w_fox_in = _gathered_to_full(g_fin, "fox_w_in")
    w_fin = jnp.pad(w_fox_in[0], ((0, 0), (0, LANES - w_fox_in.shape[-1] % LANES)))
    proj0 = _mm(hn0, w_fin, "nn", F32, "fox_in_fwd")
    qn, ka, kb, vb, c = _fox_prep_fwd(proj0, bf, qg, kg, T, D, "fox_prep_fwd")
    o, og0, mshift, linv, *gathered = _fox_attn_fwd(qn, ka, kb, vb, c, proj0, [late[n] for n in LATE], T, D,
                                                    "fox_attn_fwd")
    wl = {n: _gathered_to_full(g, n) for n, g in zip(LATE, gathered)}
    w_fout, w_hin, w_hout = wl["fox_w_out"][0], wl["hgrn_w_in"][0], wl["hgrn_w_out"][0]
    w_uin, w_uout = wl["ffn_w_in"], wl["ffn_w_out"]
    h1, hf0, hf0t = _out_proj_fwd(og0, w_fout, h0, row(small["ffn_norm"][0]), "fox_out_fwd")
    gu0 = _ffn_in_fwd(hf0, w_uin[0], "ffn0_in_fwd")
    act0 = gu0[2]
    h2, hn1, hn1t = _out_proj_fwd(act0, w_uout[0], h1, row(small["attn_norm"][1]), "ffn0_out_fwd")
    proj1 = _mm(hn1, w_hin, "nn", F32, "hgrn_in_fwd")
    lbp = small["hgrn_lower_bounds"].astype(F32)
    ggn = row(small["hgrn_g_norm"])
    Th = HGRN_TILE if Lp % HGRN_TILE == 0 else HEAD_ROWS
    og1, ss = _hgrn_fwd(proj1, lbp, ggn, T, "hgrn_fwd")
    h3, hf1, hf1t = _out_proj_fwd(og1, w_hout, h2, row(small["ffn_norm"][1]), "hgrn_out_fwd")
    gu1 = _ffn_in_fwd(hf1, w_uin[1], "ffn1_in_fwd")
    act1 = gu1[2]
    h4 = _mm(act1, w_uout[1], "nn", F32, "ffn1_out_fwd", res=h3)
    loss_blk, dh4, dh4b, d_final = _final_loss(h4, row(small["final_norm"]), target, "final_loss")

    grads = {}

    def ffn_bwd(i, dh, dhb, h_in, hft, gu):
        grads_out = _mm(gu[2], dhb, "tn", F32, f"ffn{i}_out_dw")
        dgu = _ffn_out_dx(dhb, w_uout[i], gu[0], gu[1], f"ffn{i}_out_dx")
        grads_in = _mm(hft, dgu, "nn", F32, f"ffn{i}_in_dw", tm=D, tk=_tile(Lp, 1408))
        dh_new, dh_newb, dgain = _in_proj_dx(dgu, w_uin[i], h_in, row(small["ffn_norm"][i]), dh, f"ffn{i}_in_dx")
        return dh_new, dh_newb, grads_in, grads_out, dgain

    dh3, dh3b, g_uin1, g_uout1, d_fn1 = ffn_bwd(1, dh4, dh4b, h3, hf1t, gu1)
    grads["hgrn_w_out"] = _mm(og1, dh3b, "tn", F32, "hgrn_out_dw")[None]
    dog1 = _mm(dh3b, w_hout, "nt", F32, "hgrn_out_dx")
    dproj1, d_lb, d_gg = _hgrn_bwd(proj1, lbp, ggn, dog1, ss, Th, "hgrn_bwd")
    grads["hgrn_w_in"] = _mm(hn1t, dproj1, "nn", F32, "hgrn_in_dw", tm=D, tk=_tile(Lp, 1408))[None]
    dh2, dh2b, d_an1 = _in_proj_dx(dproj1, w_hin, h2, row(small["attn_norm"][1]), dh3, "hgrn_in_dx")
    dh1, dh1b, g_uin0, g_uout0, d_fn0 = ffn_bwd(0, dh2, dh2b, h1, hf0t, gu0)
    grads["ffn_w_in"] = jnp.stack([g_uin0, g_uin1])
    grads["ffn_w_out"] = jnp.stack([g_uout0, g_uout1])
    grads["fox_w_out"] = _mm(og0, dh1b, "tn", F32, "fox_out_dw")[None]
    do, dgate, delta = _fox_out_dx(dh1b, w_fout, o, proj0, linv, D, "fox_out_dx")
    slabs = [_full_to_slabs(grads[n], n).astype(BF16) for n in LATE]
    dqn, dkn, dv, dcr, *recv = _fox_attn_bwd(qn, ka, kb, vb, c, do, mshift, delta, slabs, T, D, "fox_attn_bwd")
    for n in LATE:
        del grads[n]
    dc = jnp.pad(dcr[:, :2, :].reshape(2 * P, Lp).T, ((0, 0), (0, LANES - 2 * P)))
    Tp = T // 2 if T == ROW_TILE else T
    dproj0, sm = _fox_prep_bwd(proj0, bf, qg, kg, dqn, dkn, dv, dgate, dc, Tp, D, "fox_prep_bwd")
    g_fin = _mm(hn0t, dproj0, "nn", F32, "fox_in_dw", tm=D, tk=_tile(Lp, 1408))[:, :4 * D + n_heads][None]
    dh_head, grad_x, d_an0, r_fin = _in_proj_dx(dproj0, w_fin, h0, row(small["attn_norm"][0]), dh1, "fox_in_dx",
                                                xchg=[_full_to_slabs(g_fin, "fox_w_in").astype(BF16)], split_head=True)

    grads["meta_tokens"] = dh_head[N_PAD:]
    grads["attn_norm"] = jnp.concatenate([d_an0, d_an1], axis=0)
    grads["ffn_norm"] = jnp.concatenate([d_fn0, d_fn1], axis=0)
    grads["final_norm"] = d_final[0]
    grads["fox_b_f"] = sm[0:1, :n_heads]
    grads["fox_q_norm"] = sm[1:2, :FOX_DH] + sm[1:2, FOX_DH:]
    grads["fox_k_norm"] = sm[2:3, :FOX_DH] + sm[2:3, FOX_DH:]
    grads["hgrn_lower_bounds"] = d_lb[0:2]
    grads["hgrn_g_norm"] = d_gg[0:1]
    return loss_blk[0, 0], grad_x, grads, dict(zip(LATE, recv), fox_w_in=r_fin)


def kernel(x, meta_tokens, attn_norm, ffn_norm, final_norm, fox_w_in, fox_b_f, fox_q_norm, fox_k_norm, fox_w_out, hgrn_w_in, hgrn_lower_bounds, hgrn_g_norm, hgrn_w_out, ffn_w_in, ffn_w_out, loss_target, m_meta_tokens, m_attn_norm, m_ffn_norm, m_final_norm, m_fox_w_in, m_fox_b_f, m_fox_q_norm, m_fox_k_norm, m_fox_w_out, m_hgrn_w_in, m_hgrn_lower_bounds, m_hgrn_g_norm, m_hgrn_w_out, m_ffn_w_in, m_ffn_w_out, v_meta_tokens, v_attn_norm, v_ffn_norm, v_final_norm, v_fox_w_in, v_fox_b_f, v_fox_q_norm, v_fox_k_norm, v_fox_w_out, v_hgrn_w_in, v_hgrn_lower_bounds, v_hgrn_g_norm, v_hgrn_w_out, v_ffn_w_in, v_ffn_w_out):
    w = dict(meta_tokens=meta_tokens, attn_norm=attn_norm, ffn_norm=ffn_norm, final_norm=final_norm,
             fox_w_in=fox_w_in, fox_b_f=fox_b_f, fox_q_norm=fox_q_norm, fox_k_norm=fox_k_norm,
             fox_w_out=fox_w_out, hgrn_w_in=hgrn_w_in, hgrn_lower_bounds=hgrn_lower_bounds,
             hgrn_g_norm=hgrn_g_norm, hgrn_w_out=hgrn_w_out, ffn_w_in=ffn_w_in, ffn_w_out=ffn_w_out)
    m = dict(meta_tokens=m_meta_tokens, attn_norm=m_attn_norm, ffn_norm=m_ffn_norm, final_norm=m_final_norm,
             fox_w_in=m_fox_w_in, fox_b_f=m_fox_b_f, fox_q_norm=m_fox_q_norm, fox_k_norm=m_fox_k_norm,
             fox_w_out=m_fox_w_out, hgrn_w_in=m_hgrn_w_in, hgrn_lower_bounds=m_hgrn_lower_bounds,
             hgrn_g_norm=m_hgrn_g_norm, hgrn_w_out=m_hgrn_w_out, ffn_w_in=m_ffn_w_in, ffn_w_out=m_ffn_w_out)
    v = dict(meta_tokens=v_meta_tokens, attn_norm=v_attn_norm, ffn_norm=v_ffn_norm, final_norm=v_final_norm,
             fox_w_in=v_fox_w_in, fox_b_f=v_fox_b_f, fox_q_norm=v_fox_q_norm, fox_k_norm=v_fox_k_norm,
             fox_w_out=v_fox_w_out, hgrn_w_in=v_hgrn_w_in, hgrn_lower_bounds=v_hgrn_lower_bounds,
             hgrn_g_norm=v_hgrn_g_norm, hgrn_w_out=v_hgrn_w_out, ffn_w_in=v_ffn_w_in, ffn_w_out=v_ffn_w_out)
    axes = ("x", "y", "c")
    small_shapes = [w[n].shape for n in SMALL]

    (g_meta,) = _gather_two_level([w["meta_tokens"].astype(F32)], "gather_meta")
    loss_local, grad_x, grads, recv = _local_step(
        x[0], loss_target[0], _gathered_to_full(g_meta, "meta_tokens"), w["fox_w_in"].astype(BF16),
        {n: w[n].astype(BF16) for n in LATE}, {n: w[n] for n in SMALL})
    loss = lax.psum(loss_local, axes)

    r_meta, r_small = _exchange([_full_to_slabs(grads["meta_tokens"], "meta_tokens"),
                                 _pack_small([grads[n] for n in SMALL])], [True, False], "scatter_grads")
    recv.update(meta_tokens=r_meta)

    res = {n: _adamw(recv[n], w[n], m[n], v[n], "adamw_" + n) for n in BIG}
    sml = _adamw(r_small, _pack_small([w[n] for n in SMALL]), _pack_small([m[n] for n in SMALL]),
                 _pack_small([v[n] for n in SMALL]), "adamw_small")
    outs = []
    for k in range(4):
        d = {n: res[n][k] for n in BIG}
        d.update(zip(SMALL, _unpack_small(sml[k], small_shapes)))
        outs.extend(d[n] for n in WEIGHTS)
    return (loss, grad_x[None], *outs)
```
